```python
import math
import jax, jax.numpy as jnp
from jax import lax
import numpy as np

D_MODEL = 1024
BATCH = 8
SEQ = 8192
DEPTH = 2

HEAD_DIM = 64
MEM_LEN = 256
MEM_HEADS = 4
MEM_WIDTH = MEM_HEADS * HEAD_DIM
MIX_WIDTH = D_MODEL
TOK_WIDTH = MIX_WIDTH - MEM_WIDTH
FOX_HEADS = TOK_WIDTH // HEAD_DIM
GMLP_GROUPS = TOK_WIDTH // HEAD_DIM
CHUNK = 128
Q_BLOCK = 128
D_FF = 2816
N_MIXERS = 2
N_FOX = (DEPTH + 1) // 2
N_GMLP = DEPTH // 2
FOX_IN = 3 * TOK_WIDTH + FOX_HEADS + MEM_WIDTH
GMLP_IN = 2 * TOK_WIDTH + MEM_WIDTH
EPS = 1e-6

kernel_name = "hybrid_fox_gmlp_macaron_memxattn"


def rms_norm(x, g):
    xf = x.astype(jnp.float32)
    y = xf * lax.rsqrt(jnp.mean(xf * xf, axis=-1, keepdims=True) + EPS)
    return (y * g.astype(jnp.float32)).astype(x.dtype)


def swiglu(h, w_in, w_out):
    a, b = jnp.split(h @ w_in, 2, axis=-1)
    return (jax.nn.silu(a) * b) @ w_out


def memory_attention(mq, mem_n, w_kv, g_q, g_k):
    b, s, _ = mq.shape
    q = rms_norm(mq.reshape(b, s, MEM_HEADS, HEAD_DIM), g_q)
    kv = mem_n @ w_kv
    k, v = jnp.split(kv, 2, axis=-1)
    k = rms_norm(k.reshape(b, MEM_LEN, MEM_HEADS, HEAD_DIM), g_k)
    v = v.reshape(b, MEM_LEN, MEM_HEADS, HEAD_DIM)
    logits = jnp.einsum('bshd,bmhd->bhsm', q, k).astype(jnp.float32) / math.sqrt(HEAD_DIM)
    p = jax.nn.softmax(logits, axis=-1).astype(v.dtype)
    o = jnp.einsum('bhsm,bmhd->bshd', p, v)
    return o.reshape(b, s, MEM_WIDTH)


def forgetting_attention(q, k, v, c):
    b, s, h, d = q.shape
    nblk = s // Q_BLOCK
    qb = q.reshape(b, nblk, Q_BLOCK, h, d).transpose(1, 0, 2, 3, 4)
    cq = c.reshape(b, h, nblk, Q_BLOCK).transpose(2, 0, 1, 3)
    starts = jnp.arange(nblk, dtype=jnp.int32) * Q_BLOCK
    key_pos = jnp.arange(s, dtype=jnp.int32)
    scale = 1.0 / math.sqrt(d)

    def block(args):
        q_i, c_i, start = args
        logits = jnp.einsum('bqhd,bkhd->bhqk', q_i, k).astype(jnp.float32) * scale
        logits = logits + c_i[:, :, :, None] - c[:, :, None, :]
        q_pos = start + jnp.arange(Q_BLOCK, dtype=jnp.int32)
        mask = key_pos[None, :] <= q_pos[:, None]
        logits = jnp.where(mask[None, None], logits, -jnp.inf)
        p = jax.nn.softmax(logits, axis=-1).astype(v.dtype)
        return jnp.einsum('bhqk,bkhd->bqhd', p, v)

    out = lax.map(block, (qb, cq, starts))
    return out.transpose(1, 0, 2, 3, 4).reshape(b, s, h * d)


def fox_token_mixer(proj, b_f, g_q, g_k):
    b, s, _ = proj.shape
    t = TOK_WIDTH
    q = rms_norm(proj[..., :t].reshape(b, s, FOX_HEADS, HEAD_DIM), g_q)
    k = rms_norm(proj[..., t:2 * t].reshape(b, s, FOX_HEADS, HEAD_DIM), g_k)
    v = proj[..., 2 * t:3 * t].reshape(b, s, FOX_HEADS, HEAD_DIM)
    f_logit = proj[..., 3 * t:3 * t + FOX_HEADS].astype(jnp.float32) + b_f.astype(jnp.float32)
    log_f = jax.nn.log_sigmoid(f_logit)
    c = jnp.cumsum(log_f, axis=1).transpose(0, 2, 1)
    mq = proj[..., 3 * t + FOX_HEADS:]
    return forgetting_attention(q, k, v, c), mq


def gmlp_token_mixer(proj, v_gain, w_s, b_s):
    b, s, _ = proj.shape
    t = TOK_WIDTH
    z = jax.nn.gelu(proj[..., :2 * t])
    u, v = z[..., :t], z[..., t:]
    v = rms_norm(v.reshape(b, s, GMLP_GROUPS, HEAD_DIM), v_gain.reshape(GMLP_GROUPS, HEAD_DIM))
    n_chunk = s // CHUNK
    vc = v.reshape(b, n_chunk, CHUNK, GMLP_GROUPS, HEAD_DIM)
    w = jnp.tril(w_s)
    gate = jnp.einsum('gts,bcsgd->bctgd', w, vc) + b_s.T[None, None, :, :, None]
    out = u.reshape(b, n_chunk, CHUNK, GMLP_GROUPS, HEAD_DIM) * gate
    return out.reshape(b, s, t), proj[..., 2 * t:]


def _fwd_setup_inputs(seed: int = 0) -> dict:
    key = jax.random.key(seed)
    ks = jax.random.split(key, 32)
    f32 = jnp.float32
    D, F = D_MODEL, D_FF

    def nrm(k, shape, scale):
        return jax.random.normal(k, shape, f32) * scale

    def gain(k, shape):
        return 1.0 + 0.02 * jax.random.normal(k, shape, f32)

    return {
        "x": jax.random.normal(ks[0], (BATCH, SEQ, D), f32),
        "mem": jax.random.normal(ks[1], (BATCH, MEM_LEN, D), f32),
        "norm_ffn1": gain(ks[2], (DEPTH, D)),
        "ffn1_w_in": nrm(ks[3], (DEPTH, D, 2 * F), D ** -0.5),
        "ffn1_w_out": nrm(ks[4], (DEPTH, F, D), F ** -0.5),
        "norm_mix": gain(ks[5], (DEPTH, D)),
        "norm_ffn2": gain(ks[6], (DEPTH, D)),
        "ffn2_w_in": nrm(ks[7], (DEPTH, D, 2 * F), D ** -0.5),
        "ffn2_w_out": nrm(ks[8], (DEPTH, F, D), F ** -0.5),
        "w_out": nrm(ks[9], (DEPTH, MIX_WIDTH, D), MIX_WIDTH ** -0.5),
        "mem_norm": gain(ks[10], (D,)),
        "mem_w_kv": nrm(ks[11], (DEPTH, D, 2 * MEM_WIDTH), D ** -0.5),
        "mem_q_norm": gain(ks[12], (DEPTH, HEAD_DIM)),
        "mem_k_norm": gain(ks[13], (DEPTH, HEAD_DIM)),
        "fox_w_in": nrm(ks[14], (N_FOX, D, FOX_IN), D ** -0.5),
        "fox_b_f": 2.0 + 4.0 * jax.random.uniform(ks[15], (N_FOX, FOX_HEADS), f32),
        "fox_q_norm": gain(ks[16], (N_FOX, HEAD_DIM)),
        "fox_k_norm": gain(ks[17], (N_FOX, HEAD_DIM)),
        "gmlp_w_in": nrm(ks[18], (N_GMLP, D, GMLP_IN), D ** -0.5),
        "gmlp_v_norm": gain(ks[19], (N_GMLP, TOK_WIDTH)),
        "gmlp_w_s": nrm(ks[20], (N_GMLP, GMLP_GROUPS, CHUNK, CHUNK), 0.5 * CHUNK ** -0.5),
        "gmlp_b_s": 1.0 + 0.02 * jax.random.normal(ks[21], (N_GMLP, GMLP_GROUPS, CHUNK), f32),
    }


def _fwd_reference(x, mem, norm_ffn1, ffn1_w_in, ffn1_w_out, norm_mix, norm_ffn2, ffn2_w_in,
              ffn2_w_out, w_out, mem_norm, mem_w_kv, mem_q_norm, mem_k_norm, fox_w_in,
              fox_b_f, fox_q_norm, fox_k_norm, gmlp_w_in, gmlp_v_norm, gmlp_w_s, gmlp_b_s):
    mem_n = rms_norm(mem, mem_norm)
    for i in range(DEPTH):
        kind, j = i % N_MIXERS, i // N_MIXERS
        x = x + 0.5 * swiglu(rms_norm(x, norm_ffn1[i]), ffn1_w_in[i], ffn1_w_out[i])
        h = rms_norm(x, norm_mix[i])
        if kind == 0:
            tok, mq = fox_token_mixer(h @ fox_w_in[j], fox_b_f[j], fox_q_norm[j], fox_k_norm[j])
        else:
            tok, mq = gmlp_token_mixer(h @ gmlp_w_in[j], gmlp_v_norm[j], gmlp_w_s[j], gmlp_b_s[j])
        mo = memory_attention(mq, mem_n, mem_w_kv[i], mem_q_norm[i], mem_k_norm[i])
        x = x + jnp.concatenate([tok, mo], axis=-1) @ w_out[i]
        x = x + 0.5 * swiglu(rms_norm(x, norm_ffn2[i]), ffn2_w_in[i], ffn2_w_out[i])
    return x


import jax as _jax
import jax.numpy as _jnp

TWIN_FORMAT = 'train_step'
FWD_PARAMS = ['x', 'mem', 'norm_ffn1', 'ffn1_w_in', 'ffn1_w_out', 'norm_mix', 'norm_ffn2', 'ffn2_w_in', 'ffn2_w_out', 'w_out', 'mem_norm', 'mem_w_kv', 'mem_q_norm', 'mem_k_norm', 'fox_w_in', 'fox_b_f', 'fox_q_norm', 'fox_k_norm', 'gmlp_w_in', 'gmlp_v_norm', 'gmlp_w_s', 'gmlp_b_s']
TWIN_WEIGHTS = ['norm_ffn1', 'ffn1_w_in', 'ffn1_w_out', 'norm_mix', 'norm_ffn2', 'ffn2_w_in', 'ffn2_w_out', 'w_out', 'mem_norm', 'mem_w_kv', 'mem_q_norm', 'mem_k_norm', 'fox_w_in', 'fox_b_f', 'fox_q_norm', 'fox_k_norm', 'gmlp_w_in', 'gmlp_v_norm', 'gmlp_w_s', 'gmlp_b_s']
TWIN_DIFF_INPUT = 'x'
TWIN_INPUTS = ['x', 'mem', 'norm_ffn1', 'ffn1_w_in', 'ffn1_w_out', 'norm_mix', 'norm_ffn2', 'ffn2_w_in', 'ffn2_w_out', 'w_out', 'mem_norm', 'mem_w_kv', 'mem_q_norm', 'mem_k_norm', 'fox_w_in', 'fox_b_f', 'fox_q_norm', 'fox_k_norm', 'gmlp_w_in', 'gmlp_v_norm', 'gmlp_w_s', 'gmlp_b_s', 'loss_target', 'm_norm_ffn1', 'm_ffn1_w_in', 'm_ffn1_w_out', 'm_norm_mix', 'm_norm_ffn2', 'm_ffn2_w_in', 'm_ffn2_w_out', 'm_w_out', 'm_mem_norm', 'm_mem_w_kv', 'm_mem_q_norm', 'm_mem_k_norm', 'm_fox_w_in', 'm_fox_b_f', 'm_fox_q_norm', 'm_fox_k_norm', 'm_gmlp_w_in', 'm_gmlp_v_norm', 'm_gmlp_w_s', 'm_gmlp_b_s', 'v_norm_ffn1', 'v_ffn1_w_in', 'v_ffn1_w_out', 'v_norm_mix', 'v_norm_ffn2', 'v_ffn2_w_in', 'v_ffn2_w_out', 'v_w_out', 'v_mem_norm', 'v_mem_w_kv', 'v_mem_q_norm', 'v_mem_k_norm', 'v_fox_w_in', 'v_fox_b_f', 'v_fox_q_norm', 'v_fox_k_norm', 'v_gmlp_w_in', 'v_gmlp_v_norm', 'v_gmlp_w_s', 'v_gmlp_b_s']
TWIN_OUTPUTS = ['loss', 'grad_x', 'grad_norm_ffn1', 'grad_ffn1_w_in', 'grad_ffn1_w_out', 'grad_norm_mix', 'grad_norm_ffn2', 'grad_ffn2_w_in', 'grad_ffn2_w_out', 'grad_w_out', 'grad_mem_norm', 'grad_mem_w_kv', 'grad_mem_q_norm', 'grad_mem_k_norm', 'grad_fox_w_in', 'grad_fox_b_f', 'grad_fox_q_norm', 'grad_fox_k_norm', 'grad_gmlp_w_in', 'grad_gmlp_v_norm', 'grad_gmlp_w_s', 'grad_gmlp_b_s', 'delta_norm_ffn1', 'delta_ffn1_w_in', 'delta_ffn1_w_out', 'delta_norm_mix', 'delta_norm_ffn2', 'delta_ffn2_w_in', 'delta_ffn2_w_out', 'delta_w_out', 'delta_mem_norm', 'delta_mem_w_kv', 'delta_mem_q_norm', 'delta_mem_k_norm', 'delta_fox_w_in', 'delta_fox_b_f', 'delta_fox_q_norm', 'delta_fox_k_norm', 'delta_gmlp_w_in', 'delta_gmlp_v_norm', 'delta_gmlp_w_s', 'delta_gmlp_b_s', 'new_m_norm_ffn1', 'new_m_ffn1_w_in', 'new_m_ffn1_w_out', 'new_m_norm_mix', 'new_m_norm_ffn2', 'new_m_ffn2_w_in', 'new_m_ffn2_w_out', 'new_m_w_out', 'new_m_mem_norm', 'new_m_mem_w_kv', 'new_m_mem_q_norm', 'new_m_mem_k_norm', 'new_m_fox_w_in', 'new_m_fox_b_f', 'new_m_fox_q_norm', 'new_m_fox_k_norm', 'new_m_gmlp_w_in', 'new_m_gmlp_v_norm', 'new_m_gmlp_w_s', 'new_m_gmlp_b_s', 'new_v_norm_ffn1', 'new_v_ffn1_w_in', 'new_v_ffn1_w_out', 'new_v_norm_mix', 'new_v_norm_ffn2', 'new_v_ffn2_w_in', 'new_v_ffn2_w_out', 'new_v_w_out', 'new_v_mem_norm', 'new_v_mem_w_kv', 'new_v_mem_q_norm', 'new_v_mem_k_norm', 'new_v_fox_w_in', 'new_v_fox_b_f', 'new_v_fox_q_norm', 'new_v_fox_k_norm', 'new_v_gmlp_w_in', 'new_v_gmlp_v_norm', 'new_v_gmlp_w_s', 'new_v_gmlp_b_s']
TWIN_LEAF_KINDS = {'loss': 'loss', 'grad_x': 'grad_x', 'grad_norm_ffn1': 'grad_w', 'grad_ffn1_w_in': 'grad_w', 'grad_ffn1_w_out': 'grad_w', 'grad_norm_mix': 'grad_w', 'grad_norm_ffn2': 'grad_w', 'grad_ffn2_w_in': 'grad_w', 'grad_ffn2_w_out': 'grad_w', 'grad_w_out': 'grad_w', 'grad_mem_norm': 'grad_w', 'grad_mem_w_kv': 'grad_w', 'grad_mem_q_norm': 'grad_w', 'grad_mem_k_norm': 'grad_w', 'grad_fox_w_in': 'grad_w', 'grad_fox_b_f': 'grad_w', 'grad_fox_q_norm': 'grad_w', 'grad_fox_k_norm': 'grad_w', 'grad_gmlp_w_in': 'grad_w', 'grad_gmlp_v_norm': 'grad_w', 'grad_gmlp_w_s': 'grad_w', 'grad_gmlp_b_s': 'grad_w', 'delta_norm_ffn1': 'delta_w', 'delta_ffn1_w_in': 'delta_w', 'delta_ffn1_w_out': 'delta_w', 'delta_norm_mix': 'delta_w', 'delta_norm_ffn2': 'delta_w', 'delta_ffn2_w_in': 'delta_w', 'delta_ffn2_w_out': 'delta_w', 'delta_w_out': 'delta_w', 'delta_mem_norm': 'delta_w', 'delta_mem_w_kv': 'delta_w', 'delta_mem_q_norm': 'delta_w', 'delta_mem_k_norm': 'delta_w', 'delta_fox_w_in': 'delta_w', 'delta_fox_b_f': 'delta_w', 'delta_fox_q_norm': 'delta_w', 'delta_fox_k_norm': 'delta_w', 'delta_gmlp_w_in': 'delta_w', 'delta_gmlp_v_norm': 'delta_w', 'delta_gmlp_w_s': 'delta_w', 'delta_gmlp_b_s': 'delta_w', 'new_m_norm_ffn1': 'new_m', 'new_m_ffn1_w_in': 'new_m', 'new_m_ffn1_w_out': 'new_m', 'new_m_norm_mix': 'new_m', 'new_m_norm_ffn2': 'new_m', 'new_m_ffn2_w_in': 'new_m', 'new_m_ffn2_w_out': 'new_m', 'new_m_w_out': 'new_m', 'new_m_mem_norm': 'new_m', 'new_m_mem_w_kv': 'new_m', 'new_m_mem_q_norm': 'new_m', 'new_m_mem_k_norm': 'new_m', 'new_m_fox_w_in': 'new_m', 'new_m_fox_b_f': 'new_m', 'new_m_fox_q_norm': 'new_m', 'new_m_fox_k_norm': 'new_m', 'new_m_gmlp_w_in': 'new_m', 'new_m_gmlp_v_norm': 'new_m', 'new_m_gmlp_w_s': 'new_m', 'new_m_gmlp_b_s': 'new_m', 'new_v_norm_ffn1': 'new_v', 'new_v_ffn1_w_in': 'new_v', 'new_v_ffn1_w_out': 'new_v', 'new_v_norm_mix': 'new_v', 'new_v_norm_ffn2': 'new_v', 'new_v_ffn2_w_in': 'new_v', 'new_v_ffn2_w_out': 'new_v', 'new_v_w_out': 'new_v', 'new_v_mem_norm': 'new_v', 'new_v_mem_w_kv': 'new_v', 'new_v_mem_q_norm': 'new_v', 'new_v_mem_k_norm': 'new_v', 'new_v_fox_w_in': 'new_v', 'new_v_fox_b_f': 'new_v', 'new_v_fox_q_norm': 'new_v', 'new_v_fox_k_norm': 'new_v', 'new_v_gmlp_w_in': 'new_v', 'new_v_gmlp_v_norm': 'new_v', 'new_v_gmlp_w_s': 'new_v', 'new_v_gmlp_b_s': 'new_v'}


def _forward(args):
    return _fwd_reference(*[args[k] for k in FWD_PARAMS])


def _output_shape():
    def fwd():
        inp = _fwd_setup_inputs(0)
        return _fwd_reference(*[inp[k] for k in FWD_PARAMS])
    out = _jax.eval_shape(fwd)
    return out.shape, out.dtype

N_MICROBATCH = 1
ADAM_LR = 0.001
ADAM_B1 = 0.9
ADAM_B2 = 0.999
ADAM_EPS = 1e-08
ADAM_WD = 0.01
ADAM_STEP = 10
PER_EXAMPLE_BATCH_AXIS = {'x': 0, 'mem': 0, 'loss_target': 0}
SHARED_INPUTS = []
_WEIGHT_DTYPES = {'norm_ffn1': _jnp.float32, 'ffn1_w_in': _jnp.float32, 'ffn1_w_out': _jnp.float32, 'norm_mix': _jnp.float32, 'norm_ffn2': _jnp.float32, 'ffn2_w_in': _jnp.float32, 'ffn2_w_out': _jnp.float32, 'w_out': _jnp.float32, 'mem_norm': _jnp.float32, 'mem_w_kv': _jnp.float32, 'mem_q_norm': _jnp.float32, 'mem_k_norm': _jnp.float32, 'fox_w_in': _jnp.float32, 'fox_b_f': _jnp.float32, 'fox_q_norm': _jnp.float32, 'fox_k_norm': _jnp.float32, 'gmlp_w_in': _jnp.float32, 'gmlp_v_norm': _jnp.float32, 'gmlp_w_s': _jnp.float32, 'gmlp_b_s': _jnp.float32}
MOMENT_SCALE = {'norm_ffn1': 1.236720e+01, 'ffn1_w_in': 1.605579e-01, 'ffn1_w_out': 2.911374e-01, 'norm_mix': 1.898364e+01, 'norm_ffn2': 1.232600e+01, 'ffn2_w_in': 2.110253e-01, 'ffn2_w_out': 3.354886e-01, 'w_out': 2.873952e+00, 'mem_norm': 4.752460e-01, 'mem_w_kv': 3.094777e-01, 'mem_q_norm': 2.456588e+00, 'mem_k_norm': 2.460065e+00, 'fox_w_in': 1.662958e-01, 'fox_b_f': 5.384941e+01, 'fox_q_norm': 1.325808e+01, 'fox_k_norm': 1.324962e+01, 'gmlp_w_in': 8.565325e-01, 'gmlp_v_norm': 3.433979e+00, 'gmlp_w_s': 4.229056e+00, 'gmlp_b_s': 1.394574e+01}


def _to_microbatches(a, axis):
    t = _jnp.moveaxis(a, axis, 0)
    t = t.reshape((N_MICROBATCH, t.shape[0] // N_MICROBATCH) + t.shape[1:])
    return _jnp.moveaxis(t, 1, axis + 1)


def setup_inputs(seed: int = 0) -> dict:
    inp = _fwd_setup_inputs(seed)
    key = _jax.random.fold_in(_jax.random.key(seed), 7919)
    shape, _ = _output_shape()
    out = dict(inp)
    out["loss_target"] = _jax.random.normal(_jax.random.fold_in(key, 0), shape, _jnp.float32)
    for i, name in enumerate(TWIN_WEIGHTS):
        w = inp[name].astype(_jnp.float32)
        if MOMENT_SCALE is None:
            s = _jnp.sqrt(_jnp.mean(_jnp.square(w)) + 1e-30)
        else:
            s = MOMENT_SCALE[name]
        km, kv = _jax.random.split(_jax.random.fold_in(key, i + 1))
        out[name] = w
        out["m_" + name] = s * _jax.random.normal(km, w.shape, _jnp.float32)
        out["v_" + name] = (s * s) * _jax.random.uniform(kv, w.shape, _jnp.float32, 0.5, 1.5)
    if N_MICROBATCH > 1:
        for name, axis in PER_EXAMPLE_BATCH_AXIS.items():
            out[name] = _to_microbatches(out[name], axis)
    return {'x': out['x'], 'mem': out['mem'], 'norm_ffn1': out['norm_ffn1'], 'ffn1_w_in': out['ffn1_w_in'], 'ffn1_w_out': out['ffn1_w_out'], 'norm_mix': out['norm_mix'], 'norm_ffn2': out['norm_ffn2'], 'ffn2_w_in': out['ffn2_w_in'], 'ffn2_w_out': out['ffn2_w_out'], 'w_out': out['w_out'], 'mem_norm': out['mem_norm'], 'mem_w_kv': out['mem_w_kv'], 'mem_q_norm': out['mem_q_norm'], 'mem_k_norm': out['mem_k_norm'], 'fox_w_in': out['fox_w_in'], 'fox_b_f': out['fox_b_f'], 'fox_q_norm': out['fox_q_norm'], 'fox_k_norm': out['fox_k_norm'], 'gmlp_w_in': out['gmlp_w_in'], 'gmlp_v_norm': out['gmlp_v_norm'], 'gmlp_w_s': out['gmlp_w_s'], 'gmlp_b_s': out['gmlp_b_s'], 'loss_target': out['loss_target'], 'm_norm_ffn1': out['m_norm_ffn1'], 'm_ffn1_w_in': out['m_ffn1_w_in'], 'm_ffn1_w_out': out['m_ffn1_w_out'], 'm_norm_mix': out['m_norm_mix'], 'm_norm_ffn2': out['m_norm_ffn2'], 'm_ffn2_w_in': out['m_ffn2_w_in'], 'm_ffn2_w_out': out['m_ffn2_w_out'], 'm_w_out': out['m_w_out'], 'm_mem_norm': out['m_mem_norm'], 'm_mem_w_kv': out['m_mem_w_kv'], 'm_mem_q_norm': out['m_mem_q_norm'], 'm_mem_k_norm': out['m_mem_k_norm'], 'm_fox_w_in': out['m_fox_w_in'], 'm_fox_b_f': out['m_fox_b_f'], 'm_fox_q_norm': out['m_fox_q_norm'], 'm_fox_k_norm': out['m_fox_k_norm'], 'm_gmlp_w_in': out['m_gmlp_w_in'], 'm_gmlp_v_norm': out['m_gmlp_v_norm'], 'm_gmlp_w_s': out['m_gmlp_w_s'], 'm_gmlp_b_s': out['m_gmlp_b_s'], 'v_norm_ffn1': out['v_norm_ffn1'], 'v_ffn1_w_in': out['v_ffn1_w_in'], 'v_ffn1_w_out': out['v_ffn1_w_out'], 'v_norm_mix': out['v_norm_mix'], 'v_norm_ffn2': out['v_norm_ffn2'], 'v_ffn2_w_in': out['v_ffn2_w_in'], 'v_ffn2_w_out': out['v_ffn2_w_out'], 'v_w_out': out['v_w_out'], 'v_mem_norm': out['v_mem_norm'], 'v_mem_w_kv': out['v_mem_w_kv'], 'v_mem_q_norm': out['v_mem_q_norm'], 'v_mem_k_norm': out['v_mem_k_norm'], 'v_fox_w_in': out['v_fox_w_in'], 'v_fox_b_f': out['v_fox_b_f'], 'v_fox_q_norm': out['v_fox_q_norm'], 'v_fox_k_norm': out['v_fox_k_norm'], 'v_gmlp_w_in': out['v_gmlp_w_in'], 'v_gmlp_v_norm': out['v_gmlp_v_norm'], 'v_gmlp_w_s': out['v_gmlp_w_s'], 'v_gmlp_b_s': out['v_gmlp_b_s']}


def _loss(weights, diff, rest, loss_target):
    with _jax.named_scope("forward"):
        args = {**rest, TWIN_DIFF_INPUT: diff, **{k: w.astype(_WEIGHT_DTYPES[k]) for k, w in weights.items()}}
        y = _forward(args)
    with _jax.named_scope("loss_head"):
        err = _jnp.square(y.astype(_jnp.float32) - loss_target)
        return 0.5 * _jnp.sum(_jnp.mean(err, axis=-1)) if err.ndim else 0.5 * err


def _adamw(w, g, m, v):
    m = ADAM_B1 * m + (1.0 - ADAM_B1) * g
    v = ADAM_B2 * v + (1.0 - ADAM_B2) * _jnp.square(g)
    m_hat = m / (1.0 - ADAM_B1 ** ADAM_STEP)
    v_hat = v / (1.0 - ADAM_B2 ** ADAM_STEP)
    delta = -ADAM_LR * (m_hat / (_jnp.sqrt(v_hat) + ADAM_EPS) + ADAM_WD * w)
    return delta, m, v


def reference(x, mem, norm_ffn1, ffn1_w_in, ffn1_w_out, norm_mix, norm_ffn2, ffn2_w_in, ffn2_w_out, w_out, mem_norm, mem_w_kv, mem_q_norm, mem_k_norm, fox_w_in, fox_b_f, fox_q_norm, fox_k_norm, gmlp_w_in, gmlp_v_norm, gmlp_w_s, gmlp_b_s, loss_target, m_norm_ffn1, m_ffn1_w_in, m_ffn1_w_out, m_norm_mix, m_norm_ffn2, m_ffn2_w_in, m_ffn2_w_out, m_w_out, m_mem_norm, m_mem_w_kv, m_mem_q_norm, m_mem_k_norm, m_fox_w_in, m_fox_b_f, m_fox_q_norm, m_fox_k_norm, m_gmlp_w_in, m_gmlp_v_norm, m_gmlp_w_s, m_gmlp_b_s, v_norm_ffn1, v_ffn1_w_in, v_ffn1_w_out, v_norm_mix, v_norm_ffn2, v_ffn2_w_in, v_ffn2_w_out, v_w_out, v_mem_norm, v_mem_w_kv, v_mem_q_norm, v_mem_k_norm, v_fox_w_in, v_fox_b_f, v_fox_q_norm, v_fox_k_norm, v_gmlp_w_in, v_gmlp_v_norm, v_gmlp_w_s, v_gmlp_b_s):
    given = dict(x=x, mem=mem, norm_ffn1=norm_ffn1, ffn1_w_in=ffn1_w_in, ffn1_w_out=ffn1_w_out, norm_mix=norm_mix, norm_ffn2=norm_ffn2, ffn2_w_in=ffn2_w_in, ffn2_w_out=ffn2_w_out, w_out=w_out, mem_norm=mem_norm, mem_w_kv=mem_w_kv, mem_q_norm=mem_q_norm, mem_k_norm=mem_k_norm, fox_w_in=fox_w_in, fox_b_f=fox_b_f, fox_q_norm=fox_q_norm, fox_k_norm=fox_k_norm, gmlp_w_in=gmlp_w_in, gmlp_v_norm=gmlp_v_norm, gmlp_w_s=gmlp_w_s, gmlp_b_s=gmlp_b_s, loss_target=loss_target, m_norm_ffn1=m_norm_ffn1, m_ffn1_w_in=m_ffn1_w_in, m_ffn1_w_out=m_ffn1_w_out, m_norm_mix=m_norm_mix, m_norm_ffn2=m_norm_ffn2, m_ffn2_w_in=m_ffn2_w_in, m_ffn2_w_out=m_ffn2_w_out, m_w_out=m_w_out, m_mem_norm=m_mem_norm, m_mem_w_kv=m_mem_w_kv, m_mem_q_norm=m_mem_q_norm, m_mem_k_norm=m_mem_k_norm, m_fox_w_in=m_fox_w_in, m_fox_b_f=m_fox_b_f, m_fox_q_norm=m_fox_q_norm, m_fox_k_norm=m_fox_k_norm, m_gmlp_w_in=m_gmlp_w_in, m_gmlp_v_norm=m_gmlp_v_norm, m_gmlp_w_s=m_gmlp_w_s, m_gmlp_b_s=m_gmlp_b_s, v_norm_ffn1=v_norm_ffn1, v_ffn1_w_in=v_ffn1_w_in, v_ffn1_w_out=v_ffn1_w_out, v_norm_mix=v_norm_mix, v_norm_ffn2=v_norm_ffn2, v_ffn2_w_in=v_ffn2_w_in, v_ffn2_w_out=v_ffn2_w_out, v_w_out=v_w_out, v_mem_norm=v_mem_norm, v_mem_w_kv=v_mem_w_kv, v_mem_q_norm=v_mem_q_norm, v_mem_k_norm=v_mem_k_norm, v_fox_w_in=v_fox_w_in, v_fox_b_f=v_fox_b_f, v_fox_q_norm=v_fox_q_norm, v_fox_k_norm=v_fox_k_norm, v_gmlp_w_in=v_gmlp_w_in, v_gmlp_v_norm=v_gmlp_v_norm, v_gmlp_w_s=v_gmlp_w_s, v_gmlp_b_s=v_gmlp_b_s)
    weights = {n: given[n] for n in TWIN_WEIGHTS}
    shared = {n: given[n] for n in SHARED_INPUTS}
    per_example = {n: given[n] for n in ['x', 'mem']}
    grad_fn = _jax.value_and_grad(_loss, argnums=(0, 1))

    def one_microbatch(ex, loss_target):
        ex = dict(ex)
        diff = ex.pop(TWIN_DIFF_INPUT)
        return grad_fn(weights, diff, {**shared, **ex}, loss_target)

    if N_MICROBATCH == 1:
        loss, (grad_w, grad_x) = one_microbatch(per_example, given["loss_target"])
    else:
        def body(carry, xs):
            loss_sum, grad_sum = carry
            l_k, (gw_k, gx_k) = one_microbatch(xs[0], xs[1])
            with _jax.named_scope("update"):
                return (loss_sum + l_k, _jax.tree.map(_jnp.add, grad_sum, gw_k)), gx_k

        init = (_jnp.zeros((), _jnp.float32), _jax.tree.map(_jnp.zeros_like, weights))
        (loss, grad_w), grad_x = _jax.lax.scan(body, init, (per_example, given["loss_target"]))
    with _jax.named_scope("update"):
        delta_w, new_m, new_v = {}, {}, {}
        for n in TWIN_WEIGHTS:
            delta_w[n], new_m[n], new_v[n] = _adamw(weights[n], grad_w[n], given["m_" + n], given["v_" + n])
    return (loss, grad_x, *[grad_w[n] for n in TWIN_WEIGHTS], *[delta_w[n] for n in TWIN_WEIGHTS],
            *[new_m[n] for n in TWIN_WEIGHTS], *[new_v[n] for n in TWIN_WEIGHTS])
```

```python
import functools
import math

import jax
import jax.numpy as jnp
from jax import lax
from jax.experimental import pallas as pl
from jax.experimental.pallas import tpu as pltpu

F32 = jnp.float32
BF16 = jnp.bfloat16
EPS = 1e-6
HEAD_DIM = 64
MEM_WIDTH = 256
CHUNK = 128
LANES = 128
FFN_COL_BLOCK = 256
NEG = -1e30
VMEM_LIMIT_BYTES = 56 * 1024 * 1024
PACK_COLS = 1024
PACK_ROW_ALIGN = 16
PACK_ROW_MULTIPLE = 256
ATTN_Q_BLOCK = 1024
ATTN_K_BLOCK = 1024
MESH_ID = pl.DeviceIdType.MESH

ADAM_LR = 0.001
ADAM_B1 = 0.9
ADAM_B2 = 0.999
ADAM_EPS = 1e-08
ADAM_WD = 0.01
ADAM_STEP = 10


def _tile(n, pref, align):
    t = (min(pref, n) // align) * align
    while t >= align:
        if n % t == 0:
            return t
        t -= align
    return n


def _params(sem):
    return pltpu.CompilerParams(dimension_semantics=sem, vmem_limit_bytes=VMEM_LIMIT_BYTES)


def _dot(a, b, ca, cb):
    return lax.dot_general(a, b, (((ca,), (cb,)), ((), ())), preferred_element_type=F32)


def _sigmoid(x):
    return 1.0 / (1.0 + jnp.exp(-x))


_GELU_C = math.sqrt(2.0 / math.pi)


def _gelu(x):
    return 0.5 * x * (1.0 + jnp.tanh(_GELU_C * (x + 0.044715 * (x * x * x))))


def _gelu_grad(x):
    t = jnp.tanh(_GELU_C * (x + 0.044715 * (x * x * x)))
    return 0.5 * (1.0 + t) + 0.5 * x * (1.0 - t * t) * (_GELU_C * (1.0 + 3.0 * 0.044715 * (x * x)))


def matmul(a, b, *, ta=False, tb=False, out_dtype=F32, scale=None, res=None,
           tm=1024, tn=512, tk=1024, name):
    if ta:
        K, M = a.shape
    else:
        M, K = a.shape
    N = b.shape[0] if tb else b.shape[1]
    tm = _tile(M, tm, LANES if ta else 16)
    tn = _tile(N, tn, LANES)
    tk = _tile(K, tk, LANES)
    nk = K // tk
    a_spec = pl.BlockSpec((tk, tm), lambda i, j, k: (k, i)) if ta else pl.BlockSpec((tm, tk), lambda i, j, k: (i, k))
    b_spec = pl.BlockSpec((tn, tk), lambda i, j, k: (j, k)) if tb else pl.BlockSpec((tk, tn), lambda i, j, k: (k, j))
    o_spec = pl.BlockSpec((tm, tn), lambda i, j, k: (i, j))
    ca, cb = (0 if ta else 1), (1 if tb else 0)
    has_res = res is not None

    def body(*refs):
        a_ref, b_ref = refs[0], refs[1]
        res_ref = refs[2] if has_res else None
        o_ref = refs[3] if has_res else refs[2]
        acc_ref = refs[-1]
        k = pl.program_id(2)
        prod = _dot(a_ref[...].astype(BF16), b_ref[...].astype(BF16), ca, cb)

        def finish(acc):
            if scale is not None:
                acc = acc * scale
            if has_res:
                acc = res_ref[...] + acc
            o_ref[...] = acc.astype(out_dtype)

        if nk == 1:
            finish(prod)
        else:
            @pl.when(k == 0)
            def _():
                acc_ref[...] = prod

            @pl.when(k > 0)
            def _():
                acc_ref[...] += prod

            @pl.when(k == nk - 1)
            def _():
                finish(acc_ref[...])

    in_specs = [a_spec, b_spec] + ([o_spec] if has_res else [])
    args = (a, b) + ((res,) if has_res else ())
    return pl.pallas_call(
        body, grid=(M // tm, N // tn, nk), in_specs=in_specs, out_specs=o_spec,
        out_shape=jax.ShapeDtypeStruct((M, N), out_dtype),
        scratch_shapes=[pltpu.VMEM((tm, tn) if nk > 1 else (8, LANES), F32)],
        compiler_params=_params(("parallel", "parallel", "arbitrary")), name=name)(*args)


def swiglu_fwd(h, w_perm, *, name):
    S, D = h.shape
    F2 = w_perm.shape[1]
    fc = FFN_COL_BLOCK
    tm = _tile(S, 2048, 16)

    def body(h_ref, w_ref, ab_ref, act_ref):
        acc = _dot(h_ref[...], w_ref[...], 1, 0)
        ab_ref[...] = acc
        a, b = acc[:, :fc], acc[:, fc:]
        act_ref[...] = (a * _sigmoid(a) * b).astype(BF16)

    return pl.pallas_call(
        body, grid=(S // tm, F2 // (2 * fc)),
        in_specs=[pl.BlockSpec((tm, D), lambda i, j: (i, 0)), pl.BlockSpec((D, 2 * fc), lambda i, j: (0, j))],
        out_specs=[pl.BlockSpec((tm, 2 * fc), lambda i, j: (i, j)), pl.BlockSpec((tm, fc), lambda i, j: (i, j))],
        out_shape=[jax.ShapeDtypeStruct((S, F2), F32), jax.ShapeDtypeStruct((S, F2 // 2), BF16)],
        compiler_params=_params(("parallel", "parallel")), name=name)(h, w_perm)


def swiglu_bwd(dy, w_out, ab, *, name):
    S, D = dy.shape
    F = w_out.shape[0]
    fc = FFN_COL_BLOCK
    tm = _tile(S, 2048, 16)

    def body(dy_ref, w_ref, ab_ref, o_ref):
        dact = 0.5 * _dot(dy_ref[...].astype(BF16), w_ref[...], 1, 1)
        a, b = ab_ref[:, :fc], ab_ref[:, fc:]
        sg = _sigmoid(a)
        o_ref[:, :fc] = (dact * b * (sg * (1.0 + a * (1.0 - sg)))).astype(BF16)
        o_ref[:, fc:] = (dact * (a * sg)).astype(BF16)

    return pl.pallas_call(
        body, grid=(S // tm, F // fc),
        in_specs=[pl.BlockSpec((tm, D), lambda i, j: (i, 0)), pl.BlockSpec((fc, D), lambda i, j: (j, 0)),
                  pl.BlockSpec((tm, 2 * fc), lambda i, j: (i, j))],
        out_specs=pl.BlockSpec((tm, 2 * fc), lambda i, j: (i, j)),
        out_shape=jax.ShapeDtypeStruct((S, 2 * F), BF16),
        compiler_params=_params(("parallel", "parallel")), name=name)(dy, w_out, ab)


def rms_fwd(x, g, *, name):
    S, D = x.shape
    ts = _tile(S, 1024, 16)

    def body(x_ref, g_ref, h_ref):
        xv = x_ref[...]
        r = lax.rsqrt(jnp.mean(xv * xv, axis=-1, keepdims=True) + EPS)
        h_ref[...] = (xv * r * g_ref[...]).astype(BF16)

    return pl.pallas_call(
        body, grid=(S // ts,),
        in_specs=[pl.BlockSpec((ts, D), lambda i: (i, 0)), pl.BlockSpec((1, D), lambda i: (0, 0))],
        out_specs=pl.BlockSpec((ts, D), lambda i: (i, 0)),
        out_shape=jax.ShapeDtypeStruct((S, D), BF16),
        compiler_params=_params(("parallel",)), name=name)(x, g.reshape(1, D))


def rms_bwd(x, dh, g, res, *, name):
    S, D = x.shape
    ts = _tile(S, 512, 16)
    has_res = res is not None

    def body(*refs):
        x_ref, dh_ref, g_ref = refs[:3]
        res_ref = refs[3] if has_res else None
        dx_ref, dg_ref = refs[-2:]
        i = pl.program_id(0)
        xv, dhv = x_ref[...], dh_ref[...].astype(F32)
        r = lax.rsqrt(jnp.mean(xv * xv, axis=-1, keepdims=True) + EPS)
        u = dhv * g_ref[...]
        dx = r * u - xv * (r * r * r) * jnp.mean(xv * u, axis=-1, keepdims=True)
        if has_res:
            dx = res_ref[...] + dx
        dx_ref[...] = dx
        part = jnp.sum(dhv * xv * r, axis=0, keepdims=True)

        @pl.when(i == 0)
        def _():
            dg_ref[...] = part

        @pl.when(i > 0)
        def _():
            dg_ref[...] += part

    row = pl.BlockSpec((ts, D), lambda i: (i, 0))
    vec = pl.BlockSpec((1, D), lambda i: (0, 0))
    args = (x, dh, g.reshape(1, D)) + ((res,) if has_res else ())
    return pl.pallas_call(
        body, grid=(S // ts,), in_specs=[row, row, vec] + ([row] if has_res else []),
        out_specs=[row, vec],
        out_shape=[jax.ShapeDtypeStruct((S, D), F32), jax.ShapeDtypeStruct((1, D), F32)],
        compiler_params=_params(("arbitrary",)), name=name)(*args)


def headnorm_fwd(x, g, *, name):
    H, S, d = x.shape
    ts = _tile(S, 1024, 16)
    g = jnp.broadcast_to(g.reshape(-1, 1, d), (H, 1, d))

    def body(x_ref, g_ref, o_ref):
        xv = x_ref[...]
        r = lax.rsqrt(jnp.mean(xv * xv, axis=-1, keepdims=True) + EPS)
        o_ref[...] = (xv * r * g_ref[...]).astype(BF16)

    return pl.pallas_call(
        body, grid=(H, S // ts),
        in_specs=[pl.BlockSpec((None, ts, d), lambda h, i: (h, i, 0)), pl.BlockSpec((None, 1, d), lambda h, i: (h, 0, 0))],
        out_specs=pl.BlockSpec((None, ts, d), lambda h, i: (h, i, 0)),
        out_shape=jax.ShapeDtypeStruct((H, S, d), BF16),
        compiler_params=_params(("parallel", "parallel")), name=name)(x, g)


def headnorm_bwd(x, dy, g, *, name):
    H, S, d = x.shape
    ts = _tile(S, 1024, 16)

    def body(x_ref, dy_ref, g_ref, dx_ref, dg_ref):
        first = jnp.logical_and(pl.program_id(0) == 0, pl.program_id(1) == 0)
        xv, dyv = x_ref[...], dy_ref[...]
        r = lax.rsqrt(jnp.mean(xv * xv, axis=-1, keepdims=True) + EPS)
        u = dyv * g_ref[...]
        dx_ref[...] = r * u - xv * (r * r * r) * jnp.mean(xv * u, axis=-1, keepdims=True)
        part = jnp.sum(dyv * xv * r, axis=0, keepdims=True)

        @pl.when(first)
        def _():
            dg_ref[...] = part

        @pl.when(jnp.logical_not(first))
        def _():
            dg_ref[...] += part

    blk = pl.BlockSpec((None, ts, d), lambda h, i: (h, i, 0))
    vec = pl.BlockSpec((1, d), lambda h, i: (0, 0))
    return pl.pallas_call(
        body, grid=(H, S // ts), in_specs=[blk, blk, vec], out_specs=[blk, vec],
        out_shape=[jax.ShapeDtypeStruct((H, S, d), F32), jax.ShapeDtypeStruct((1, d), F32)],
        compiler_params=_params(("arbitrary", "arbitrary")), name=name)(x, dy, g.reshape(1, d))


def _split3(x):
    x1 = x.astype(BF16)
    r1 = x - x1.astype(F32)
    x2 = r1.astype(BF16)
    x3 = (r1 - x2.astype(F32)).astype(BF16)
    return x1, x2, x3


def _tri_ones(n, lower):
    r = lax.broadcasted_iota(jnp.int32, (n, n), 0)
    c = lax.broadcasted_iota(jnp.int32, (n, n), 1)
    return jnp.where((c <= r) if lower else (c >= r), 1.0, 0.0).astype(BF16)


def fgate_fwd(z, bias, *, name):
    S, L = z.shape
    tb = _tile(S, 256, 16)

    def body(z_ref, b_ref, c_ref, carry):
        i = pl.program_id(0)

        @pl.when(i == 0)
        def _():
            carry[...] = jnp.zeros_like(carry)

        zz = z_ref[...] + b_ref[...]
        lf = jnp.minimum(zz, 0.0) - jnp.log(1.0 + jnp.exp(-jnp.abs(zz)))
        tri = _tri_ones(tb, True)
        x1, x2, x3 = _split3(lf)
        c = (_dot(tri, x1, 1, 0) + _dot(tri, x2, 1, 0)) + _dot(tri, x3, 1, 0) + carry[...]
        c_ref[...] = c
        carry[...] += jnp.sum(lf, axis=0, keepdims=True)

    return pl.pallas_call(
        body, grid=(S // tb,),
        in_specs=[pl.BlockSpec((tb, L), lambda i: (i, 0)), pl.BlockSpec((1, L), lambda i: (0, 0))],
        out_specs=pl.BlockSpec((tb, L), lambda i: (i, 0)),
        out_shape=jax.ShapeDtypeStruct((S, L), F32),
        scratch_shapes=[pltpu.VMEM((1, L), F32)],
        compiler_params=_params(("arbitrary",)), name=name)(z, bias)


def fgate_bwd(z, bias, dcs, *, name):
    S, L = z.shape
    tb = _tile(S, 256, 16)
    nb = S // tb

    def body(z_ref, b_ref, dcs_ref, dz_ref, db_ref, carry):
        i = pl.program_id(0)

        @pl.when(i == 0)
        def _():
            carry[...] = jnp.zeros_like(carry)

        tri = _tri_ones(tb, False)
        dc = -dcs_ref[...]
        x1, x2, x3 = _split3(dc)
        dlf = (_dot(tri, x1, 1, 0) + _dot(tri, x2, 1, 0)) + _dot(tri, x3, 1, 0) + carry[...]
        carry[...] += jnp.sum(dc, axis=0, keepdims=True)
        dz = dlf * _sigmoid(-(z_ref[...] + b_ref[...]))
        dz_ref[...] = dz
        part = jnp.sum(dz, axis=0, keepdims=True)

        @pl.when(i == 0)
        def _():
            db_ref[...] = part

        @pl.when(i > 0)
        def _():
            db_ref[...] += part

    rev = pl.BlockSpec((tb, L), lambda i: (nb - 1 - i, 0))
    vec = pl.BlockSpec((1, L), lambda i: (0, 0))
    return pl.pallas_call(
        body, grid=(nb,), in_specs=[rev, vec, rev], out_specs=[rev, vec],
        out_shape=[jax.ShapeDtypeStruct((S, L), F32), jax.ShapeDtypeStruct((1, L), F32)],
        scratch_shapes=[pltpu.VMEM((1, L), F32)],
        compiler_params=_params(("arbitrary",)), name=name)(z, bias, dcs)


def attn_fwd(q, k, v, cq, ck, *, causal, name):
    H, Sq, d = q.shape
    Sk = k.shape[1]
    tq = _tile(Sq, ATTN_Q_BLOCK, LANES)
    tk = _tile(Sk, ATTN_K_BLOCK, LANES)
    nq, nk = Sq // tq, Sk // tk
    scale = 1.0 / math.sqrt(d)
    bias = cq is not None

    def body(*refs):
        q_ref, k_ref, v_ref = refs[:3]
        cq_ref, ck_ref = (refs[3], refs[4]) if bias else (None, None)
        o_ref, lse_ref, m_sc, l_sc, acc_sc = refs[-5:]
        i, j = pl.program_id(1), pl.program_id(2)

        @pl.when(j == 0)
        def _():
            m_sc[...] = jnp.full_like(m_sc, NEG)
            l_sc[...] = jnp.zeros_like(l_sc)
            acc_sc[...] = jnp.zeros_like(acc_sc)

        def compute():
            s = _dot(q_ref[...], k_ref[...], 1, 1) * scale
            if bias:
                s = s + (cq_ref[...] - ck_ref[...])
            if causal:
                row = i * tq + lax.broadcasted_iota(jnp.int32, (tq, tk), 0)
                col = j * tk + lax.broadcasted_iota(jnp.int32, (tq, tk), 1)
                s = jnp.where(col <= row, s, NEG)
            m_prev = m_sc[...]
            m_new = jnp.maximum(m_prev, jnp.max(s, axis=1, keepdims=True))
            alpha = jnp.exp(m_prev - m_new)
            p = jnp.exp(s - m_new)
            l_sc[...] = alpha * l_sc[...] + jnp.sum(p, axis=1, keepdims=True)
            acc_sc[...] = alpha * acc_sc[...] + _dot(p.astype(BF16), v_ref[...], 1, 0)
            m_sc[...] = m_new

        if causal:
            pl.when(j * tk <= i * tq + (tq - 1))(compute)
        else:
            compute()

        @pl.when(j == nk - 1)
        def _():
            o_ref[...] = acc_sc[...] / l_sc[...]
            lse_ref[...] = m_sc[...] + jnp.log(l_sc[...])

    def kv_idx(h, i, j):
        return (h, jnp.minimum(j, (i * tq + tq - 1) // tk) if causal else j, 0)

    in_specs = [pl.BlockSpec((None, tq, d), lambda h, i, j: (h, i, 0)),
                pl.BlockSpec((None, tk, d), kv_idx), pl.BlockSpec((None, tk, d), kv_idx)]
    args = [q, k, v]
    if bias:
        in_specs += [pl.BlockSpec((None, tq, 1), lambda h, i, j: (h, i, 0)),
                     pl.BlockSpec((None, 1, tk), lambda h, i, j: (h, 0, kv_idx(h, i, j)[1]))]
        args += [cq, ck]
    return pl.pallas_call(
        body, grid=(H, nq, nk), in_specs=in_specs,
        out_specs=[pl.BlockSpec((None, tq, d), lambda h, i, j: (h, i, 0)),
                   pl.BlockSpec((None, tq, 1), lambda h, i, j: (h, i, 0))],
        out_shape=[jax.ShapeDtypeStruct((H, Sq, d), F32), jax.ShapeDtypeStruct((H, Sq, 1), F32)],
        scratch_shapes=[pltpu.VMEM((tq, 1), F32), pltpu.VMEM((tq, 1), F32), pltpu.VMEM((tq, d), F32)],
        compiler_params=_params(("parallel", "parallel", "arbitrary")), name=name)(*args)


def attn_delta(o, do, *, name):
    H, S, d = o.shape
    ts = _tile(S, 1024, 16)

    def body(o_ref, do_ref, out_ref):
        out_ref[...] = jnp.sum(o_ref[...] * do_ref[...], axis=-1, keepdims=True)

    blk = pl.BlockSpec((None, ts, d), lambda h, i: (h, i, 0))
    return pl.pallas_call(
        body, grid=(H, S // ts), in_specs=[blk, blk],
        out_specs=pl.BlockSpec((None, ts, 1), lambda h, i: (h, i, 0)),
        out_shape=jax.ShapeDtypeStruct((H, S, 1), F32),
        compiler_params=_params(("parallel", "parallel")), name=name)(o, do)


def attn_bwd(q, k, v, cq, ck, do, lse, delta, *, causal, name):
    H, Sq, d = q.shape
    Sk = k.shape[1]
    tq = _tile(Sq, ATTN_Q_BLOCK, LANES)
    tk = _tile(Sk, ATTN_K_BLOCK, LANES)
    nq, nk = Sq // tq, Sk // tk
    scale = 1.0 / math.sqrt(d)
    bias = cq is not None

    def body(*refs):
        q_ref, k_ref, v_ref, do_ref, lse_ref, dl_ref = refs[:6]
        cq_ref, ck_ref = (refs[6], refs[7]) if bias else (None, None)
        outs = refs[8:] if bias else refs[6:]
        dq_ref, dk_ref, dv_ref = outs[:3]
        dcs_ref = outs[3] if bias else None
        j, i = pl.program_id(1), pl.program_id(2)

        @pl.when(i == 0)
        def _():
            dk_ref[...] = jnp.zeros_like(dk_ref)
            dv_ref[...] = jnp.zeros_like(dv_ref)
            if bias:
                dcs_ref[...] = jnp.zeros_like(dcs_ref)

        rows = pl.ds(pl.multiple_of(i * tq, tq), tq)

        def compute():
            qv, kv, vv, dov = q_ref[...], k_ref[...], v_ref[...], do_ref[...]
            s = _dot(qv, kv, 1, 1) * scale
            if bias:
                s = s + (cq_ref[...] - ck_ref[...])
            p = jnp.exp(s - lse_ref[...])
            if causal:
                row = i * tq + lax.broadcasted_iota(jnp.int32, (tq, tk), 0)
                col = j * tk + lax.broadcasted_iota(jnp.int32, (tq, tk), 1)
                p = jnp.where(col <= row, p, 0.0)
            dv_ref[...] += _dot(p.astype(BF16), dov, 0, 0)
            dp = _dot(dov, vv, 1, 1)
            ds = p * (dp - dl_ref[...])
            dsb = ds.astype(BF16)
            dk_ref[...] += _dot(dsb, qv, 0, 0) * scale
            if bias:
                dcs_ref[...] += jnp.sum(ds, axis=0, keepdims=True)
            dq_part = _dot(dsb, kv, 1, 0) * scale

            @pl.when(j == 0)
            def _():
                dq_ref[rows, :] = dq_part

            @pl.when(j > 0)
            def _():
                dq_ref[rows, :] += dq_part

        if causal:
            pl.when(j * tk <= i * tq + (tq - 1))(compute)
        else:
            compute()

    def q_idx(h, j, i):
        return (h, jnp.maximum(i, (j * tk) // tq) if causal else i, 0)

    qblk = pl.BlockSpec((None, tq, d), q_idx)
    kblk = pl.BlockSpec((None, tk, d), lambda h, j, i: (h, j, 0))
    col1 = pl.BlockSpec((None, tq, 1), q_idx)
    in_specs = [qblk, kblk, kblk, qblk, col1, col1]
    args = [q, k, v, do, lse, delta]
    out_specs = [pl.BlockSpec((None, Sq, d), lambda h, j, i: (h, 0, 0)), kblk, kblk]
    out_shape = [jax.ShapeDtypeStruct((H, Sq, d), F32), jax.ShapeDtypeStruct((H, Sk, d), F32),
                 jax.ShapeDtypeStruct((H, Sk, d), F32)]
    if bias:
        in_specs += [col1, pl.BlockSpec((None, 1, tk), lambda h, j, i: (h, 0, j))]
        args += [cq, ck]
        out_specs.append(pl.BlockSpec((None, 1, tk), lambda h, j, i: (h, 0, j)))
        out_shape.append(jax.ShapeDtypeStruct((H, 1, Sk), F32))
    return pl.pallas_call(
        body, grid=(H, nk, nq), in_specs=in_specs, out_specs=out_specs, out_shape=out_shape,
        compiler_params=_params(("parallel", "arbitrary", "arbitrary")), name=name)(*args)


def _tril_mask(n):
    r = lax.broadcasted_iota(jnp.int32, (n, n), 0)
    c = lax.broadcasted_iota(jnp.int32, (n, n), 1)
    return c <= r


def gmlp_fwd(up, vp, vg, w, b, *, name):
    G, S, d = up.shape
    ts = _tile(S, 1024, CHUNK)

    def body(up_ref, vp_ref, vg_ref, w_ref, b_ref, o_ref):
        wt = jnp.where(_tril_mask(CHUNK), w_ref[...], 0.0).astype(BF16)
        for c in range(ts // CHUNK):
            sl = pl.ds(c * CHUNK, CHUNK)
            vz = _gelu(vp_ref[sl, :])
            r = lax.rsqrt(jnp.mean(vz * vz, axis=-1, keepdims=True) + EPS)
            vh = (vz * r * vg_ref[...]).astype(BF16)
            gate = _dot(wt, vh, 1, 0) + b_ref[...]
            o_ref[sl, :] = _gelu(up_ref[sl, :]) * gate

    blk = pl.BlockSpec((None, ts, d), lambda g, i: (g, i, 0))
    return pl.pallas_call(
        body, grid=(G, S // ts),
        in_specs=[blk, blk, pl.BlockSpec((None, 1, d), lambda g, i: (g, 0, 0)),
                  pl.BlockSpec((None, CHUNK, CHUNK), lambda g, i: (g, 0, 0)),
                  pl.BlockSpec((None, CHUNK, 1), lambda g, i: (g, 0, 0))],
        out_specs=blk, out_shape=jax.ShapeDtypeStruct((G, S, d), F32),
        compiler_params=_params(("parallel", "parallel")), name=name)(up, vp, vg, w, b)


def gmlp_bwd(up, vp, vg, w, wT, b, do, *, name):
    G, S, d = up.shape
    ts = _tile(S, 1024, CHUNK)

    def body(up_ref, vp_ref, vg_ref, w_ref, wT_ref, b_ref, do_ref, dup_ref, dvp_ref, dw_ref, db_ref, dvg_ref):
        i = pl.program_id(1)

        @pl.when(i == 0)
        def _():
            dw_ref[...] = jnp.zeros_like(dw_ref)
            db_ref[...] = jnp.zeros_like(db_ref)
            dvg_ref[...] = jnp.zeros_like(dvg_ref)

        mask = _tril_mask(CHUNK)
        wt = jnp.where(mask, w_ref[...], 0.0).astype(BF16)
        wtT = jnp.where(_tril_mask(CHUNK).T, wT_ref[...], 0.0).astype(BF16)
        vgain = vg_ref[...]
        for c in range(ts // CHUNK):
            sl = pl.ds(c * CHUNK, CHUNK)
            u_pre, v_pre, dout = up_ref[sl, :], vp_ref[sl, :], do_ref[sl, :]
            vz = _gelu(v_pre)
            r = lax.rsqrt(jnp.mean(vz * vz, axis=-1, keepdims=True) + EPS)
            vh = (vz * r * vgain).astype(BF16)
            gate = _dot(wt, vh, 1, 0) + b_ref[...]
            dgate = dout * _gelu(u_pre)
            dup_ref[sl, :] = dout * gate * _gelu_grad(u_pre)
            dgb = dgate.astype(BF16)
            dw_ref[...] += jnp.where(mask, _dot(dgb, vh, 1, 1), 0.0)
            db_ref[...] += jnp.sum(dgate, axis=1, keepdims=True)
            dvh = _dot(wtT, dgb, 1, 0)
            dvg_ref[...] += jnp.sum(dvh * vz * r, axis=0, keepdims=True)
            t = dvh * vgain
            dvz = r * t - vz * (r * r * r) * jnp.mean(vz * t, axis=-1, keepdims=True)
            dvp_ref[sl, :] = dvz * _gelu_grad(v_pre)

    blk = pl.BlockSpec((None, ts, d), lambda g, i: (g, i, 0))
    wblk = pl.BlockSpec((None, CHUNK, CHUNK), lambda g, i: (g, 0, 0))
    bblk = pl.BlockSpec((None, CHUNK, 1), lambda g, i: (g, 0, 0))
    gblk = pl.BlockSpec((None, 1, d), lambda g, i: (g, 0, 0))
    return pl.pallas_call(
        body, grid=(G, S // ts), in_specs=[blk, blk, gblk, wblk, wblk, bblk, blk],
        out_specs=[blk, blk, wblk, bblk, gblk],
        out_shape=[jax.ShapeDtypeStruct((G, S, d), F32), jax.ShapeDtypeStruct((G, S, d), F32),
                   jax.ShapeDtypeStruct((G, CHUNK, CHUNK), F32), jax.ShapeDtypeStruct((G, CHUNK, 1), F32),
                   jax.ShapeDtypeStruct((G, 1, d), F32)],
        compiler_params=_params(("parallel", "arbitrary")), name=name)(up, vp, vg, w, wT, b, do)


def loss_head(y, target, *, name):
    S, D = y.shape
    ts = _tile(S, 512, 8)

    def body(y_ref, t_ref, dy_ref, loss_ref):
        i = pl.program_id(0)
        e = y_ref[...] - t_ref[...]
        dy_ref[...] = e * (1.0 / D)
        part = jnp.sum(jnp.sum(e * e, axis=1, keepdims=True), axis=0, keepdims=True) * (0.5 / D)

        @pl.when(i == 0)
        def _():
            loss_ref[...] = part

        @pl.when(i > 0)
        def _():
            loss_ref[...] += part

    row = pl.BlockSpec((ts, D), lambda i: (i, 0))
    return pl.pallas_call(
        body, grid=(S // ts,), in_specs=[row, row],
        out_specs=[row, pl.BlockSpec((1, 1), lambda i: (0, 0))],
        out_shape=[jax.ShapeDtypeStruct((S, D), F32), jax.ShapeDtypeStruct((1, 1), F32)],
        compiler_params=_params(("arbitrary",)), name=name)(y, target)


def adamw(w, g, m, v, *, name):
    shape = w.shape
    C = shape[-1]
    R = w.size // C
    tr = _tile(R, max(8, (256 * 1024) // C // 8 * 8), 8)

    def body(w_ref, g_ref, m_ref, v_ref, d_ref, nm_ref, nv_ref):
        gv = g_ref[...]
        nm = ADAM_B1 * m_ref[...] + (1.0 - ADAM_B1) * gv
        nv = ADAM_B2 * v_ref[...] + (1.0 - ADAM_B2) * (gv * gv)
        m_hat = nm / (1.0 - ADAM_B1 ** ADAM_STEP)
        v_hat = nv / (1.0 - ADAM_B2 ** ADAM_STEP)
        d_ref[...] = -ADAM_LR * (m_hat / (jnp.sqrt(v_hat) + ADAM_EPS) + ADAM_WD * w_ref[...])
        nm_ref[...] = nm
        nv_ref[...] = nv

    blk = pl.BlockSpec((tr, C), lambda i: (i, 0))
    out = pl.pallas_call(
        body, grid=(R // tr,), in_specs=[blk] * 4, out_specs=[blk] * 3,
        out_shape=[jax.ShapeDtypeStruct((R, C), F32)] * 3,
        compiler_params=_params(("parallel",)), name=name)(*(a.reshape(R, C) for a in (w, g, m, v)))
    return tuple(o.reshape(shape) for o in out)


def pair_sum(p, landed, half, *, name):
    _, n, R, C = p.shape
    tr = _tile(R, 256, 16)

    def body(half_ref, p_ref, l_ref, o_ref):
        o_ref[...] = (p_ref[...] + l_ref[...]).astype(BF16)

    return pl.pallas_call(
        body,
        grid_spec=pltpu.PrefetchScalarGridSpec(
            num_scalar_prefetch=1, grid=(n, R // tr),
            in_specs=[pl.BlockSpec((None, None, tr, C), lambda k, r, half_ref: (half_ref[0], k, r, 0)),
                      pl.BlockSpec((None, tr, C), lambda k, r, half_ref: (k, r, 0))],
            out_specs=pl.BlockSpec((None, tr, C), lambda k, r, half_ref: (k, r, 0))),
        out_shape=jax.ShapeDtypeStruct((n, R, C), BF16),
        compiler_params=_params(("parallel", "parallel")), name=name)(half, p, landed)


def ordered_sum(parts, *, name):
    n, R, C = parts.shape
    tr = _tile(R, 256, 16)

    def body(p_ref, o_ref):
        acc = p_ref[0].astype(F32)
        for d in range(1, n):
            acc = acc + p_ref[d].astype(F32)
        o_ref[...] = acc

    return pl.pallas_call(
        body, grid=(R // tr,), in_specs=[pl.BlockSpec((n, tr, C), lambda r: (0, r, 0))],
        out_specs=pl.BlockSpec((tr, C), lambda r: (r, 0)),
        out_shape=jax.ShapeDtypeStruct((R, C), F32),
        compiler_params=_params(("parallel",)), name=name)(parts)


_ANY = pl.BlockSpec(memory_space=pl.ANY)


def _position():
    return lax.axis_index("x"), lax.axis_index("y"), lax.axis_index("c")


def _remote(src, dst, send_sem, recv_sem, device):
    return pltpu.make_async_remote_copy(src_ref=src, dst_ref=dst, send_sem=send_sem, recv_sem=recv_sem,
                                        device_id=device, device_id_type=MESH_ID)


def gather_weight_halves(packed, *, name):
    _, R, C = packed.shape

    def body(src_ref, out_ref, send_sems, recv_sems, local_sem):
        x, y, c = _position()
        k = 2 * x + y
        sibling = (x, y, 1 - c)
        chips = [(1 - x, y), (x, 1 - y), (1 - x, 1 - y)]
        mine = pltpu.make_async_copy(src_ref, out_ref.at[k], local_sem)
        mine.start()
        first = [_remote(src_ref.at[c], out_ref.at[k, c], send_sems.at[j], recv_sems.at[j], (px, py, c))
                 for j, (px, py) in enumerate(chips)]
        for cp in first:
            cp.start()
        passed = []
        for j, (px, py) in enumerate(chips):
            slot = out_ref.at[2 * px + py, c]
            _remote(slot, slot, send_sems.at[j], recv_sems.at[j], (px, py, c)).wait_recv()
            cp = _remote(slot, slot, send_sems.at[3 + j], recv_sems.at[3 + j], sibling)
            cp.start()
            passed.append(cp)
        for j, (px, py) in enumerate(chips):
            slot = out_ref.at[2 * px + py, 1 - c]
            _remote(slot, slot, send_sems.at[3 + j], recv_sems.at[3 + j], sibling).wait_recv()
        for cp in first + passed:
            cp.wait_send()
        mine.wait()

    return pl.pallas_call(
        body, in_specs=[_ANY], out_specs=_ANY,
        out_shape=jax.ShapeDtypeStruct((4, 2, R, C), packed.dtype),
        scratch_shapes=[pltpu.SemaphoreType.DMA((6,)), pltpu.SemaphoreType.DMA((6,)), pltpu.SemaphoreType.DMA],
        name=name)(packed)


def exchange_with_sibling(p, small, *, name):
    _, n, R, C = p.shape
    Rs, Cs = small.shape

    def body(p_ref, s_ref, land_ref, all_ref, send_sems, recv_sems, local_sem):
        x, y, c = _position()
        me = 4 * x + 2 * y + c
        big = _remote(p_ref.at[1 - c], land_ref, send_sems.at[0], recv_sems.at[0], (x, y, 1 - c))
        big.start()
        own = pltpu.make_async_copy(s_ref, all_ref.at[me], local_sem)
        own.start()
        copies = []
        for f in range(1, 8):
            peer = ((1 - x) if f & 4 else x, (1 - y) if f & 2 else y, (1 - c) if f & 1 else c)
            cp = _remote(s_ref, all_ref.at[me], send_sems.at[f], recv_sems.at[f], peer)
            cp.start()
            copies.append((cp, peer, f))
        big.wait_recv()
        for cp, peer, f in copies:
            slot = all_ref.at[4 * peer[0] + 2 * peer[1] + peer[2]]
            _remote(slot, slot, send_sems.at[f], recv_sems.at[f], peer).wait_recv()
        big.wait_send()
        for cp, _, _ in copies:
            cp.wait_send()
        own.wait()

    return pl.pallas_call(
        body, in_specs=[_ANY, _ANY], out_specs=[_ANY, _ANY],
        out_shape=[jax.ShapeDtypeStruct((n, R, C), p.dtype), jax.ShapeDtypeStruct((8, Rs, Cs), small.dtype)],
        scratch_shapes=[pltpu.SemaphoreType.DMA((8,)), pltpu.SemaphoreType.DMA((8,)), pltpu.SemaphoreType.DMA],
        name=name)(p, small)


def scatter_to_chips(q, *, name):
    n, R, C = q.shape

    def body(q_ref, out_ref, send_sems, recv_sems, local_sem):
        x, y, c = _position()
        k = 2 * x + y
        chips = [(1 - x, y), (x, 1 - y), (1 - x, 1 - y)]
        own = pltpu.make_async_copy(q_ref.at[k], out_ref.at[k], local_sem)
        own.start()
        sends = [_remote(q_ref.at[2 * px + py], out_ref.at[k], send_sems.at[j], recv_sems.at[j], (px, py, c))
                 for j, (px, py) in enumerate(chips)]
        for cp in sends:
            cp.start()
        for j, (px, py) in enumerate(chips):
            slot = out_ref.at[2 * px + py]
            _remote(slot, slot, send_sems.at[j], recv_sems.at[j], (px, py, c)).wait_recv()
        for cp in sends:
            cp.wait_send()
        own.wait()

    return pl.pallas_call(
        body, in_specs=[_ANY], out_specs=_ANY, out_shape=jax.ShapeDtypeStruct((n, R, C), q.dtype),
        scratch_shapes=[pltpu.SemaphoreType.DMA((3,)), pltpu.SemaphoreType.DMA((3,)), pltpu.SemaphoreType.DMA],
        name=name)(q)


def share_with_sibling(r, *, name):
    R, C = r.shape

    def body(r_ref, out_ref, send_sem, recv_sem, local_sem):
        x, y, c = _position()
        own = pltpu.make_async_copy(r_ref, out_ref.at[c], local_sem)
        own.start()
        cp = _remote(r_ref, out_ref.at[c], send_sem, recv_sem, (x, y, 1 - c))
        cp.start()
        _remote(r_ref, out_ref.at[1 - c], send_sem, recv_sem, (x, y, 1 - c)).wait_recv()
        cp.wait_send()
        own.wait()

    return pl.pallas_call(
        body, in_specs=[_ANY], out_specs=_ANY, out_shape=jax.ShapeDtypeStruct((2, R, C), r.dtype),
        scratch_shapes=[pltpu.SemaphoreType.DMA, pltpu.SemaphoreType.DMA, pltpu.SemaphoreType.DMA],
        name=name)(r)


def _half_rows(n_elems):
    half = n_elems // 2
    unit = PACK_ROW_ALIGN * PACK_COLS
    return -(-half // unit) * PACK_ROW_ALIGN


def _pack_halves(arrays, lead):
    cols = []
    for a in arrays:
        n = a.shape[-1]
        rows = _half_rows(n)
        h = a.reshape(lead + (2, n // 2))
        h = jnp.pad(h, [(0, 0)] * (len(lead) + 1) + [(0, rows * PACK_COLS - n // 2)])
        cols.append(h.reshape(lead + (2, rows, PACK_COLS)))
    total = sum(c.shape[-2] for c in cols)
    extra = -total % PACK_ROW_MULTIPLE
    if extra:
        cols.append(jnp.zeros(lead + (2, extra, PACK_COLS), arrays[0].dtype))
    return jnp.concatenate(cols, axis=len(lead) + 1)


def _unpack_halves(buf, sizes, lead):
    out, off = [], 0
    for n in sizes:
        rows = _half_rows(n)
        h = lax.slice_in_dim(buf, off, off + rows, axis=len(lead) + 1)
        h = h.reshape(lead + (2, rows * PACK_COLS))[..., : n // 2]
        out.append(h.reshape(lead + (n,)))
        off += rows
    return out


def _cols_to_chips(full):
    *lead, R, C4 = full.shape
    t = full.reshape(*lead, R, 4, C4 // 4)
    return jnp.moveaxis(t, -2, 0)


def _chips_to_cols(sh):
    t = jnp.moveaxis(sh, 0, -2)
    return t.reshape(*t.shape[:-2], t.shape[-2] * t.shape[-1])


def _rows_to_chips(full):
    L, R4, C = full.shape
    return jnp.moveaxis(full.reshape(L, 4, R4 // 4, C), 1, 0)


def _chips_to_rows(sh):
    _, L, R, C = sh.shape
    return jnp.moveaxis(sh, 0, 1).reshape(L, 4 * R, C)


def _ffn_in_perm(w):
    *lead, F2 = w.shape
    nb = F2 // 2 // FFN_COL_BLOCK
    t = w.reshape(*lead, 2, nb, FFN_COL_BLOCK)
    return jnp.swapaxes(t, -3, -2).reshape(*lead, F2)


def _ffn_in_unperm(w):
    *lead, F2 = w.shape
    nb = F2 // 2 // FFN_COL_BLOCK
    t = w.reshape(*lead, nb, 2, FFN_COL_BLOCK)
    return jnp.swapaxes(t, -3, -2).reshape(*lead, F2)


def _to_heads(x):
    S, W = x.shape
    return x.reshape(S, W // HEAD_DIM, HEAD_DIM).transpose(1, 0, 2)


def _from_heads(x):
    H, S, d = x.shape
    return x.transpose(1, 0, 2).reshape(S, H * d)


def _ffn_fwd(x, g, w_in_perm, w_out, tag):
    h = rms_fwd(x, g, name=f"{tag}_rms")
    ab, act = swiglu_fwd(h, w_in_perm, name=f"{tag}_in")
    y = matmul(act, w_out, res=x, scale=0.5, tm=1024, tn=512, tk=w_out.shape[0], name=f"{tag}_out")
    return y, (x, h, ab, act)


def _ffn_bwd(dy, saved, g, w_in_perm, w_out, tag):
    x, h, ab, act = saved
    dab = swiglu_bwd(dy, w_out, ab, name=f"{tag}_dact")
    dw_out = matmul(act, dy, ta=True, scale=0.5, tm=1408, tn=1024, tk=1024, name=f"{tag}_dwout")
    dw_in = matmul(h, dab, ta=True, tm=1024, tn=1408, tk=1024, name=f"{tag}_dwin")
    dh = matmul(dab, w_in_perm, tb=True, tm=1024, tn=1024, tk=1408, name=f"{tag}_dh")
    dx, dg = rms_bwd(x, dh, g, dy, name=f"{tag}_drms")
    return dx, dg[0], dw_in, dw_out


def _mem_kv(mem_n, w_kv, g_k, tag):
    kv = matmul(mem_n, w_kv, tm=256, tn=512, tk=1024, name=f"{tag}_kv")
    k_pre = _to_heads(kv[:, :MEM_WIDTH])
    kh = headnorm_fwd(k_pre, g_k, name=f"{tag}_knorm")
    vh = _to_heads(kv[:, MEM_WIDTH:]).astype(BF16)
    return k_pre, kh, vh


def _mem_attn_fwd(mq, mem_n, w_kv, g_q, g_k, tag):
    q_pre = _to_heads(mq)
    qh = headnorm_fwd(q_pre, g_q, name=f"{tag}_qnorm")
    k_pre, kh, vh = _mem_kv(mem_n, w_kv, g_k, tag)
    o, lse = attn_fwd(qh, kh, vh, None, None, causal=False, name=f"{tag}_attn")
    return o, (q_pre, qh, k_pre, kh, vh, o, lse)


def _mem_attn_bwd(do, saved, mem_n, w_kv, g_q, g_k, tag):
    q_pre, qh, k_pre, kh, vh, o, lse = saved
    delta = attn_delta(o, do, name=f"{tag}_delta")
    dqh, dkh, dvh = attn_bwd(qh, kh, vh, None, None, do.astype(BF16), lse, delta, causal=False, name=f"{tag}_dattn")
    dq_pre, dgq = headnorm_bwd(q_pre, dqh, g_q, name=f"{tag}_dqnorm")
    dk_pre, dgk = headnorm_bwd(k_pre, dkh, g_k, name=f"{tag}_dknorm")
    dkv = jnp.concatenate([_from_heads(dk_pre), _from_heads(dvh)], axis=1)
    dw_kv = matmul(mem_n, dkv, ta=True, tm=1024, tn=512, tk=256, name=f"{tag}_dwkv")
    return _from_heads(dq_pre), dgq[0], dgk[0], dw_kv, dkv


def _fox_fwd(proj, b_f, g_q, g_k, tok, tag):
    S = proj.shape[0]
    H = tok // HEAD_DIM
    q_pre, k_pre = _to_heads(proj[:, :tok]), _to_heads(proj[:, tok:2 * tok])
    vh = _to_heads(proj[:, 2 * tok:3 * tok]).astype(BF16)
    z = proj[:, 3 * tok + MEM_WIDTH:]
    bias = jnp.pad(b_f.reshape(1, H), ((0, 0), (0, LANES - H)))
    qh = headnorm_fwd(q_pre, g_q, name=f"{tag}_qnorm")
    kh = headnorm_fwd(k_pre, g_k, name=f"{tag}_knorm")
    c = fgate_fwd(z, bias, name=f"{tag}_fgate")
    ch = c[:, :H].T
    cq, ck = ch.reshape(H, S, 1), ch.reshape(H, 1, S)
    o, lse = attn_fwd(qh, kh, vh, cq, ck, causal=True, name=f"{tag}_attn")
    return o, (q_pre, k_pre, qh, kh, vh, z, bias, cq, ck, o, lse)


def _fox_bwd(do, saved, g_q, g_k, tag):
    q_pre, k_pre, qh, kh, vh, z, bias, cq, ck, o, lse = saved
    H, S, _ = q_pre.shape
    delta = attn_delta(o, do, name=f"{tag}_delta")
    dqh, dkh, dvh, dcs = attn_bwd(qh, kh, vh, cq, ck, do.astype(BF16), lse, delta, causal=True, name=f"{tag}_dattn")
    dq_pre, dgq = headnorm_bwd(q_pre, dqh, g_q, name=f"{tag}_dqnorm")
    dk_pre, dgk = headnorm_bwd(k_pre, dkh, g_k, name=f"{tag}_dknorm")
    dcs_t = jnp.pad(dcs.reshape(H, S).T, ((0, 0), (0, LANES - H)))
    dz, dbias = fgate_bwd(z, bias, dcs_t, name=f"{tag}_dfgate")
    dqkv = jnp.concatenate([_from_heads(dq_pre), _from_heads(dk_pre), _from_heads(dvh)], axis=1)
    return dqkv, dz, dbias[0, :H], dgq[0], dgk[0]


def _gmlp_args(v_gain, w_s, b_s):
    G = w_s.shape[0]
    return v_gain.reshape(G, 1, HEAD_DIM), w_s, b_s.reshape(G, CHUNK, 1)


def local_step(x, mem, target, W):
    S, D = x.shape
    tok = D - MEM_WIDTH
    H = tok // HEAD_DIM
    depth = W["norm_ffn1"].shape[0]
    mem_n = rms_fwd(mem, W["mem_norm"], name="mem_rms")
    saved = []
    for i in range(depth):
        kind, j = i % 2, i // 2
        t = f"l{i}"
        x1, s1 = _ffn_fwd(x, W["norm_ffn1"][i], W["ffn1_w_in"][i], W["ffn1_w_out"][i], f"{t}_ffn1")
        h = rms_fwd(x1, W["norm_mix"][i], name=f"{t}_mix_rms")
        w_mix = W["fox_w_in"][j] if kind == 0 else W["gmlp_w_in"][j]
        proj = matmul(h, w_mix, tm=1024, tn=896, tk=D, name=f"{t}_mix_in")
        if kind == 0:
            o_tok, s_tok = _fox_fwd(proj, W["fox_b_f"][j], W["fox_q_norm"][j], W["fox_k_norm"][j], tok, f"{t}_fox")
            mq = proj[:, 3 * tok:3 * tok + MEM_WIDTH]
        else:
            up, vp = _to_heads(proj[:, :tok]), _to_heads(proj[:, tok:2 * tok])
            vg, ws, bs = _gmlp_args(W["gmlp_v_norm"][j], W["gmlp_w_s"][j], W["gmlp_b_s"][j])
            o_tok = gmlp_fwd(up, vp, vg, ws, bs, name=f"{t}_gmlp")
            s_tok = (up, vp)
            mq = proj[:, 2 * tok:2 * tok + MEM_WIDTH]
        o_mem, s_mem = _mem_attn_fwd(mq, mem_n, W["mem_w_kv"][i], W["mem_q_norm"][i], W["mem_k_norm"][i], f"{t}_mem")
        mix = jnp.concatenate([_from_heads(o_tok), _from_heads(o_mem)], axis=1).astype(BF16)
        x2 = matmul(mix, W["w_out"][i], res=x1, tm=1024, tn=512, tk=D, name=f"{t}_mix_out")
        x3, s3 = _ffn_fwd(x2, W["norm_ffn2"][i], W["ffn2_w_in"][i], W["ffn2_w_out"][i], f"{t}_ffn2")
        saved.append((s1, x1, h, s_tok, s_mem, mix, s3))
        x = x3

    dx, loss = loss_head(x, target, name="loss_head")

    G = {k: [None] * depth for k in ("norm_ffn1", "ffn1_w_in", "ffn1_w_out", "norm_mix", "norm_ffn2", "ffn2_w_in",
                                     "ffn2_w_out", "w_out", "mem_w_kv", "mem_q_norm", "mem_k_norm")}
    n_fox, n_gmlp = (depth + 1) // 2, depth // 2
    for k in ("fox_w_in", "fox_b_f", "fox_q_norm", "fox_k_norm"):
        G[k] = [None] * n_fox
    for k in ("gmlp_w_in", "gmlp_v_norm", "gmlp_w_s", "gmlp_b_s"):
        G[k] = [None] * n_gmlp
    dkv_all = [None] * depth
    for i in reversed(range(depth)):
        kind, j = i % 2, i // 2
        t = f"l{i}"
        s1, x1, h, s_tok, s_mem, mix, s3 = saved[i]
        dx, G["norm_ffn2"][i], G["ffn2_w_in"][i], G["ffn2_w_out"][i] = _ffn_bwd(
            dx, s3, W["norm_ffn2"][i], W["ffn2_w_in"][i], W["ffn2_w_out"][i], f"{t}_ffn2")
        dmix = matmul(dx, W["w_out"][i], tb=True, tm=1024, tn=1024, tk=D, name=f"{t}_dmix")
        G["w_out"][i] = matmul(mix, dx, ta=True, tm=1024, tn=1024, tk=1024, name=f"{t}_dwmixout")
        do_tok, do_mem = _to_heads(dmix[:, :tok]), _to_heads(dmix[:, tok:])
        dmq, G["mem_q_norm"][i], G["mem_k_norm"][i], G["mem_w_kv"][i], dkv_all[i] = _mem_attn_bwd(
            do_mem, s_mem, mem_n, W["mem_w_kv"][i], W["mem_q_norm"][i], W["mem_k_norm"][i], f"{t}_mem")
        if kind == 0:
            dqkv, dz, G["fox_b_f"][j], G["fox_q_norm"][j], G["fox_k_norm"][j] = _fox_bwd(
                do_tok, s_tok, W["fox_q_norm"][j], W["fox_k_norm"][j], f"{t}_fox")
            dproj = jnp.concatenate([dqkv, dmq, dz], axis=1).astype(BF16)
            w_mix, wkey = W["fox_w_in"][j], "fox_w_in"
        else:
            up, vp = s_tok
            vg, ws, bs = _gmlp_args(W["gmlp_v_norm"][j], W["gmlp_w_s"][j], W["gmlp_b_s"][j])
            dup, dvp, dws, dbs, dvg = gmlp_bwd(up, vp, vg, ws, jnp.swapaxes(ws, 1, 2), bs, do_tok, name=f"{t}_dgmlp")
            G["gmlp_w_s"][j], G["gmlp_b_s"][j], G["gmlp_v_norm"][j] = dws, dbs[:, :, 0], dvg.reshape(-1)
            dproj = jnp.concatenate([_from_heads(dup), _from_heads(dvp), dmq], axis=1).astype(BF16)
            w_mix, wkey = W["gmlp_w_in"][j], "gmlp_w_in"
        G[wkey][j] = matmul(h, dproj, ta=True, tm=1024, tn=896, tk=1024, name=f"{t}_dwmixin")
        dh = matmul(dproj, w_mix, tb=True, tm=1024, tn=1024, tk=896, name=f"{t}_dhmix")
        dx, dgm = rms_bwd(x1, dh, W["norm_mix"][i], dx, name=f"{t}_dmixrms")
        G["norm_mix"][i] = dgm[0]
        dx, G["norm_ffn1"][i], G["ffn1_w_in"][i], G["ffn1_w_out"][i] = _ffn_bwd(
            dx, s1, W["norm_ffn1"][i], W["ffn1_w_in"][i], W["ffn1_w_out"][i], f"{t}_ffn1")
    w_kv_all = jnp.concatenate([W["mem_w_kv"][i] for i in range(depth)], axis=1)
    dmem_n = matmul(jnp.concatenate(dkv_all, axis=1), w_kv_all, tb=True, tm=256, tn=512, tk=1024, name="dmem_n")
    _, dmemg = rms_bwd(mem, dmem_n, W["mem_norm"], None, name="dmem_rms")
    grads = {k: jnp.stack(v) for k, v in G.items()}
    grads["mem_norm"] = dmemg[0]
    return loss, dx, grads


def _fox_cols_to_compute(w, tok):
    H = tok // HEAD_DIM
    qkv, f, mq = w[..., :3 * tok], w[..., 3 * tok:3 * tok + H], w[..., 3 * tok + H:]
    f = jnp.pad(f, [(0, 0)] * (w.ndim - 1) + [(0, LANES - H)])
    return jnp.concatenate([qkv, mq, f], axis=-1)


def _fox_cols_from_compute(w, tok):
    H = tok // HEAD_DIM
    qkv, mq, f = w[..., :3 * tok], w[..., 3 * tok:3 * tok + MEM_WIDTH], w[..., 3 * tok + MEM_WIDTH:3 * tok + MEM_WIDTH + H]
    return jnp.concatenate([qkv, f, mq], axis=-1)


_COL_SHARDED = ("ffn1_w_in", "ffn2_w_in", "fox_w_in", "gmlp_w_in")
_ROW_SHARDED = ("ffn1_w_out", "ffn2_w_out", "w_out", "mem_w_kv")
_BIG = ("ffn1_w_in", "ffn1_w_out", "ffn2_w_in", "ffn2_w_out", "w_out", "mem_w_kv", "fox_w_in", "gmlp_w_in")
_SMALL = ("norm_ffn1", "norm_mix", "norm_ffn2", "mem_norm", "mem_q_norm", "mem_k_norm", "fox_b_f", "fox_q_norm",
          "fox_k_norm", "gmlp_v_norm", "gmlp_w_s", "gmlp_b_s")
WEIGHT_ORDER = ("norm_ffn1", "ffn1_w_in", "ffn1_w_out", "norm_mix", "norm_ffn2", "ffn2_w_in", "ffn2_w_out", "w_out",
                "mem_norm", "mem_w_kv", "mem_q_norm", "mem_k_norm", "fox_w_in", "fox_b_f", "fox_q_norm", "fox_k_norm",
                "gmlp_w_in", "gmlp_v_norm", "gmlp_w_s", "gmlp_b_s")


def _gather_weights(shards):
    flat = [shards[k].astype(BF16).reshape(-1) for k in _BIG]
    vn_shape = shards["gmlp_v_norm"].shape
    flat.append(jnp.concatenate(_split3(shards["gmlp_v_norm"].reshape(-1))))
    sizes = [f.shape[0] for f in flat]
    packed = _pack_halves(flat, ())
    gathered = gather_weight_halves(packed, name="gather_weights")
    parts = _unpack_halves(gathered, sizes, (4,))
    full = {}
    for k, p in zip(_BIG, parts[:-1]):
        sh = p.reshape((4,) + shards[k].shape)
        full[k] = _chips_to_cols(sh) if k in _COL_SHARDED else _chips_to_rows(sh)
    pieces = parts[-1].reshape((4, 3) + vn_shape).astype(F32)
    full["gmlp_v_norm"] = _chips_to_cols((pieces[:, 0] + pieces[:, 1]) + pieces[:, 2])
    return full


def kernel(x, mem, norm_ffn1, ffn1_w_in, ffn1_w_out, norm_mix, norm_ffn2, ffn2_w_in, ffn2_w_out, w_out, mem_norm, mem_w_kv, mem_q_norm, mem_k_norm, fox_w_in, fox_b_f, fox_q_norm, fox_k_norm, gmlp_w_in, gmlp_v_norm, gmlp_w_s, gmlp_b_s, loss_target, m_norm_ffn1, m_ffn1_w_in, m_ffn1_w_out, m_norm_mix, m_norm_ffn2, m_ffn2_w_in, m_ffn2_w_out, m_w_out, m_mem_norm, m_mem_w_kv, m_mem_q_norm, m_mem_k_norm, m_fox_w_in, m_fox_b_f, m_fox_q_norm, m_fox_k_norm, m_gmlp_w_in, m_gmlp_v_norm, m_gmlp_w_s, m_gmlp_b_s, v_norm_ffn1, v_ffn1_w_in, v_ffn1_w_out, v_norm_mix, v_norm_ffn2, v_ffn2_w_in, v_ffn2_w_out, v_w_out, v_mem_norm, v_mem_w_kv, v_mem_q_norm, v_mem_k_norm, v_fox_w_in, v_fox_b_f, v_fox_q_norm, v_fox_k_norm, v_gmlp_w_in, v_gmlp_v_norm, v_gmlp_w_s, v_gmlp_b_s):
    w = dict(norm_ffn1=norm_ffn1, ffn1_w_in=ffn1_w_in, ffn1_w_out=ffn1_w_out, norm_mix=norm_mix, norm_ffn2=norm_ffn2,
             ffn2_w_in=ffn2_w_in, ffn2_w_out=ffn2_w_out, w_out=w_out, mem_norm=mem_norm, mem_w_kv=mem_w_kv,
             mem_q_norm=mem_q_norm, mem_k_norm=mem_k_norm, fox_w_in=fox_w_in, fox_b_f=fox_b_f, fox_q_norm=fox_q_norm,
             fox_k_norm=fox_k_norm, gmlp_w_in=gmlp_w_in, gmlp_v_norm=gmlp_v_norm, gmlp_w_s=gmlp_w_s, gmlp_b_s=gmlp_b_s)
    m = dict(norm_ffn1=m_norm_ffn1, ffn1_w_in=m_ffn1_w_in, ffn1_w_out=m_ffn1_w_out, norm_mix=m_norm_mix,
             norm_ffn2=m_norm_ffn2, ffn2_w_in=m_ffn2_w_in, ffn2_w_out=m_ffn2_w_out, w_out=m_w_out, mem_norm=m_mem_norm,
             mem_w_kv=m_mem_w_kv, mem_q_norm=m_mem_q_norm, mem_k_norm=m_mem_k_norm, fox_w_in=m_fox_w_in,
             fox_b_f=m_fox_b_f, fox_q_norm=m_fox_q_norm, fox_k_norm=m_fox_k_norm, gmlp_w_in=m_gmlp_w_in,
             gmlp_v_norm=m_gmlp_v_norm, gmlp_w_s=m_gmlp_w_s, gmlp_b_s=m_gmlp_b_s)
    v = dict(norm_ffn1=v_norm_ffn1, ffn1_w_in=v_ffn1_w_in, ffn1_w_out=v_ffn1_w_out, norm_mix=v_norm_mix,
             norm_ffn2=v_norm_ffn2, ffn2_w_in=v_ffn2_w_in, ffn2_w_out=v_ffn2_w_out, w_out=v_w_out, mem_norm=v_mem_norm,
             mem_w_kv=v_mem_w_kv, mem_q_norm=v_mem_q_norm, mem_k_norm=v_mem_k_norm, fox_w_in=v_fox_w_in,
             fox_b_f=v_fox_b_f, fox_q_norm=v_fox_q_norm, fox_k_norm=v_fox_k_norm, gmlp_w_in=v_gmlp_w_in,
             gmlp_v_norm=v_gmlp_v_norm, gmlp_w_s=v_gmlp_w_s, gmlp_b_s=v_gmlp_b_s)
    D = x.shape[-1]
    tok = D - MEM_WIDTH
    xi, yi, ci = _position()
    chip = 2 * xi + yi

    full = _gather_weights(w)
    W = {k: w[k] for k in _SMALL}
    W["gmlp_v_norm"] = full["gmlp_v_norm"]
    for k in ("ffn1_w_out", "ffn2_w_out", "w_out", "mem_w_kv", "gmlp_w_in"):
        W[k] = full[k]
    W["ffn1_w_in"] = _ffn_in_perm(full["ffn1_w_in"])
    W["ffn2_w_in"] = _ffn_in_perm(full["ffn2_w_in"])
    W["fox_w_in"] = _fox_cols_to_compute(full["fox_w_in"], tok)

    loss, grad_x, g = local_step(x[0], mem[0], loss_target[0], W)

    g["ffn1_w_in"] = _ffn_in_unperm(g["ffn1_w_in"])
    g["ffn2_w_in"] = _ffn_in_unperm(g["ffn2_w_in"])
    g["fox_w_in"] = _fox_cols_from_compute(g["fox_w_in"], tok)
    per_chip = []
    for k in _BIG:
        sh = _cols_to_chips(g[k]) if k in _COL_SHARDED else _rows_to_chips(g[k])
        per_chip.append(sh.reshape(4, -1))
    big_sizes = [p.shape[1] for p in per_chip]
    p = jnp.moveaxis(_pack_halves(per_chip, (4,)), 1, 0)
    small_list = [g[k].reshape(-1) for k in _SMALL] + [loss.reshape(-1)]
    small_sizes = [s.shape[0] for s in small_list]
    small_rows = [-(-n // LANES) for n in small_sizes]
    small = jnp.concatenate([jnp.pad(s, (0, r * LANES - n)).reshape(r, LANES)
                             for s, n, r in zip(small_list, small_sizes, small_rows)], axis=0)
    small = jnp.pad(small, ((0, -small.shape[0] % 64), (0, 0)))

    landed, small_all = exchange_with_sibling(p, small, name="grad_pair_exchange")
    q = pair_sum(p, landed, ci.reshape(1).astype(jnp.int32), name="grad_pair_sum")
    from_chips = scatter_to_chips(q, name="grad_scatter")
    mine = ordered_sum(from_chips, name="grad_chip_sum")
    both = share_with_sibling(mine, name="grad_share")
    small_sum = ordered_sum(small_all, name="small_sum")

    big_grads = dict(zip(_BIG, _unpack_halves(both, big_sizes, ())))
    red = {k: big_grads[k].reshape(w[k].shape) for k in _BIG}
    off = 0
    for k, n, r in zip(_SMALL, small_sizes, small_rows):
        red[k] = small_sum[off:off + r].reshape(-1)[:n].reshape(g[k].shape)
        off += r
    loss_total = small_sum[off, 0]
    vn_cols = w["gmlp_v_norm"].shape[-1]
    red["gmlp_v_norm"] = lax.dynamic_slice_in_dim(red["gmlp_v_norm"], chip * vn_cols, vn_cols, axis=-1)

    deltas, new_m, new_v = {}, {}, {}
    for k in WEIGHT_ORDER:
        wk = w[k] if w[k].ndim > 1 else w[k].reshape(1, -1)
        upd = adamw(wk, red[k].reshape(wk.shape), m[k].reshape(wk.shape), v[k].reshape(wk.shape), name=f"adamw_{k}")
        deltas[k], new_m[k], new_v[k] = (u.reshape(w[k].shape) for u in upd)
    return (loss_total, grad_x[None], *[red[k].reshape(w[k].shape) for k in WEIGHT_ORDER],
            *[deltas[k] for k in WEIGHT_ORDER], *[new_m[k] for k in WEIGHT_ORDER], *[new_v[k] for k in WEIGHT_ORDER])
```

```python
import functools
import math

import jax
import jax.numpy as jnp
from jax import lax
from jax.experimental import pallas as pl
from jax.experimental.pallas import tpu as pltpu

F32 = jnp.float32
BF16 = jnp.bfloat16
EPS = 1e-6
HEAD_DIM = 64
MEM_WIDTH = 256
CHUNK = 128
LANES = 128
NEG = -1e30
VMEM_LIMIT_BYTES = 56 * 1024 * 1024
ATTN_Q_BLOCK = 1024
ATTN_K_BLOCK = 1024
QK_SCALE = 0.125
MESH_ID = pl.DeviceIdType.MESH

ADAM_LR = 0.001
ADAM_B1 = 0.9
ADAM_B2 = 0.999
ADAM_EPS = 1e-08
ADAM_WD = 0.01
ADAM_STEP = 10


def _tile(n, pref, align):
    t = (min(pref, n) // align) * align
    while t >= align:
        if n % t == 0:
            return t
        t -= align
    return n


def _params(sem):
    return pltpu.CompilerParams(dimension_semantics=sem, vmem_limit_bytes=VMEM_LIMIT_BYTES)


def _dot(a, b, ca, cb):
    return lax.dot_general(a, b, (((ca,), (cb,)), ((), ())), preferred_element_type=F32)


def _sigmoid(x):
    return 1.0 / (1.0 + jnp.exp(-x))


_GELU_C = math.sqrt(2.0 / math.pi)


def _gelu(x):
    return 0.5 * x * (1.0 + jnp.tanh(_GELU_C * (x + 0.044715 * (x * x * x))))


def _gelu_grad(x):
    t = jnp.tanh(_GELU_C * (x + 0.044715 * (x * x * x)))
    return 0.5 * (1.0 + t) + 0.5 * x * (1.0 - t * t) * (_GELU_C * (1.0 + 3.0 * 0.044715 * (x * x)))


def matmul(a, b, *, ta=False, tb=False, out_dtype=F32, scale=None, res=None,
           tm=1024, tn=512, tk=1024, name):
    if ta:
        K, M = a.shape
    else:
        M, K = a.shape
    N = b.shape[0] if tb else b.shape[1]
    tm = _tile(M, tm, LANES if ta else 16)
    tn = _tile(N, tn, LANES)
    tk = _tile(K, tk, LANES)
    nk = K // tk
    a_spec = pl.BlockSpec((tk, tm), lambda i, j, k: (k, i)) if ta else pl.BlockSpec((tm, tk), lambda i, j, k: (i, k))
    b_spec = pl.BlockSpec((tn, tk), lambda i, j, k: (j, k)) if tb else pl.BlockSpec((tk, tn), lambda i, j, k: (k, j))
    o_spec = pl.BlockSpec((tm, tn), lambda i, j, k: (i, j))
    ca, cb = (0 if ta else 1), (1 if tb else 0)
    has_res = res is not None

    def body(*refs):
        a_ref, b_ref = refs[0], refs[1]
        res_ref = refs[2] if has_res else None
        o_ref = refs[3] if has_res else refs[2]
        acc_ref = refs[-1]
        k = pl.program_id(2)
        prod = _dot(a_ref[...].astype(BF16), b_ref[...].astype(BF16), ca, cb)

        def finish(acc):
            if scale is not None:
                acc = acc * scale
            if has_res:
                acc = res_ref[...] + acc
            o_ref[...] = acc.astype(out_dtype)

        if nk == 1:
            finish(prod)
        else:
            @pl.when(k == 0)
            def _():
                acc_ref[...] = prod

            @pl.when(k > 0)
            def _():
                acc_ref[...] += prod

            @pl.when(k == nk - 1)
            def _():
                finish(acc_ref[...])

    in_specs = [a_spec, b_spec] + ([o_spec] if has_res else [])
    args = (a, b) + ((res,) if has_res else ())
    return pl.pallas_call(
        body, grid=(M // tm, N // tn, nk), in_specs=in_specs, out_specs=o_spec,
        out_shape=jax.ShapeDtypeStruct((M, N), out_dtype),
        scratch_shapes=[pltpu.VMEM((tm, tn) if nk > 1 else (8, LANES), F32)],
        compiler_params=_params(("parallel", "parallel", "arbitrary")), name=name)(*args)


def swiglu_fwd(h, w_slab, layer, *, name):
    S, D = h.shape
    Fc = w_slab.shape[-1]
    tm = _tile(S, 512, 16)

    def body(h_ref, wa_ref, wb_ref, a_ref, b_ref, act_ref):
        hv = h_ref[...]
        a = _dot(hv, wa_ref[...], 1, 0)
        b = _dot(hv, wb_ref[...], 1, 0)
        a_ref[...] = a
        b_ref[...] = b
        act_ref[...] = (a * _sigmoid(a) * b).astype(BF16)

    out = pl.BlockSpec((tm, Fc), lambda j, i: (i, j))
    return pl.pallas_call(
        body, grid=(2, S // tm),
        in_specs=[pl.BlockSpec((tm, D), lambda j, i: (i, 0)),
                  pl.BlockSpec((None, None, D, Fc), lambda j, i: (j, layer, 0, 0)),
                  pl.BlockSpec((None, None, D, Fc), lambda j, i: (j + 2, layer, 0, 0))],
        out_specs=[out, out, out],
        out_shape=[jax.ShapeDtypeStruct((S, 2 * Fc), F32), jax.ShapeDtypeStruct((S, 2 * Fc), F32),
                   jax.ShapeDtypeStruct((S, 2 * Fc), BF16)],
        compiler_params=_params(("parallel", "parallel")), name=name)(h, w_slab, w_slab)


def swiglu_bwd(dy, w_out, a, b, *, name):
    S, D = dy.shape
    F = w_out.shape[0]
    fc = F // 2
    tm = _tile(S, 512, 16)

    def body(dy_ref, w_ref, a_ref, b_ref, da_ref, db_ref):
        dact = 0.5 * _dot(dy_ref[...].astype(BF16), w_ref[...], 1, 1)
        av = a_ref[...]
        sg = _sigmoid(av)
        da_ref[...] = (dact * b_ref[...] * (sg * (1.0 + av * (1.0 - sg)))).astype(BF16)
        db_ref[...] = (dact * (av * sg)).astype(BF16)

    blk = pl.BlockSpec((tm, fc), lambda j, i: (i, j))
    return pl.pallas_call(
        body, grid=(2, S // tm),
        in_specs=[pl.BlockSpec((tm, D), lambda j, i: (i, 0)), pl.BlockSpec((fc, D), lambda j, i: (j, 0)), blk, blk],
        out_specs=[blk, blk],
        out_shape=[jax.ShapeDtypeStruct((S, F), BF16), jax.ShapeDtypeStruct((S, F), BF16)],
        compiler_params=_params(("parallel", "parallel")), name=name)(dy, w_out, a, b)


def ffn_dh(da, db, w_slab, layer, *, name):
    S, F = da.shape
    D, Fc = w_slab.shape[-2:]
    tm = _tile(S, 1024, 16)

    def body(da_ref, db_ref, w_ref, o_ref, acc_ref):
        k = pl.program_id(1)

        @pl.when(k == 0)
        def _():
            acc_ref[...] = jnp.zeros_like(acc_ref)

        @pl.when(k < 2)
        def _():
            acc_ref[...] += _dot(da_ref[...], w_ref[...], 1, 1)

        @pl.when(k >= 2)
        def _():
            acc_ref[...] += _dot(db_ref[...], w_ref[...], 1, 1)

        @pl.when(k == 3)
        def _():
            o_ref[...] = acc_ref[...]

    return pl.pallas_call(
        body, grid=(S // tm, 4),
        in_specs=[pl.BlockSpec((tm, Fc), lambda i, k: (i, jnp.minimum(k, 1))),
                  pl.BlockSpec((tm, Fc), lambda i, k: (i, jnp.maximum(k - 2, 0))),
                  pl.BlockSpec((None, None, D, Fc), lambda i, k: (k, layer, 0, 0))],
        out_specs=pl.BlockSpec((tm, D), lambda i, k: (i, 0)),
        out_shape=jax.ShapeDtypeStruct((S, D), F32),
        scratch_shapes=[pltpu.VMEM((tm, D), F32)],
        compiler_params=_params(("parallel", "arbitrary")), name=name)(da, db, w_slab)


def _slab_alias(slab, n_in):
    if slab is None:
        return (), [], {}
    return (slab,), [_ANY], {n_in: 0}


def grad_cols(h, da, db, slab, layer, n_layers, *, name):
    S, D = h.shape
    Fc = da.shape[1] // 2
    tk = _tile(S, 512, 16)
    nk = S // tk
    extra, extra_specs, alias = _slab_alias(slab, 3)

    def body(*refs):
        h_ref, da_ref, db_ref = refs[:3]
        o_ref, acc_ref = refs[-2:]
        ch, k = pl.program_id(0), pl.program_id(1)

        @pl.when(k == 0)
        def _():
            acc_ref[...] = jnp.zeros_like(acc_ref)

        @pl.when(ch < 2)
        def _():
            acc_ref[...] += _dot(h_ref[...], da_ref[...], 0, 0)

        @pl.when(ch >= 2)
        def _():
            acc_ref[...] += _dot(h_ref[...], db_ref[...], 0, 0)

        @pl.when(k == nk - 1)
        def _():
            o_ref[...] = acc_ref[...]

    return pl.pallas_call(
        body, grid=(4, nk),
        in_specs=[pl.BlockSpec((tk, D), lambda ch, k: (k, 0)),
                  pl.BlockSpec((tk, Fc), lambda ch, k: (jnp.where(ch < 2, k, 0), jnp.minimum(ch, 1))),
                  pl.BlockSpec((tk, Fc), lambda ch, k: (jnp.where(ch >= 2, k, 0), jnp.maximum(ch - 2, 0)))] + extra_specs,
        out_specs=pl.BlockSpec((None, None, D, Fc), lambda ch, k: (ch, layer, 0, 0)),
        out_shape=jax.ShapeDtypeStruct((4, n_layers, D, Fc), F32),
        scratch_shapes=[pltpu.VMEM((D, Fc), F32)], input_output_aliases=alias,
        compiler_params=_params(("parallel", "arbitrary")), name=name)(h, da, db, *extra)


def grad_rows(a, b, slab, layer, n_layers, *, scale=None, name):
    S, M = a.shape
    N = b.shape[1]
    R = M // 4
    tn = _tile(N, 512, LANES)
    tk = _tile(S, 512, 16)
    nk = S // tk
    extra, extra_specs, alias = _slab_alias(slab, 2)

    def body(*refs):
        a_ref, b_ref = refs[:2]
        o_ref, acc_ref = refs[-2:]
        k = pl.program_id(1)

        @pl.when(k == 0)
        def _():
            acc_ref[...] = jnp.zeros_like(acc_ref)

        acc_ref[...] += _dot(a_ref[...].astype(BF16), b_ref[...].astype(BF16), 0, 0)

        @pl.when(k == nk - 1)
        def _():
            for d in range(4):
                part = acc_ref[d * R:(d + 1) * R, :]
                o_ref[d] = part if scale is None else part * scale

    return pl.pallas_call(
        body, grid=(N // tn, nk),
        in_specs=[pl.BlockSpec((tk, M), lambda j, k: (k, 0)), pl.BlockSpec((tk, tn), lambda j, k: (k, j))] + extra_specs,
        out_specs=pl.BlockSpec((4, None, R, tn), lambda j, k: (0, layer, 0, j)),
        out_shape=jax.ShapeDtypeStruct((4, n_layers, R, N), F32),
        scratch_shapes=[pltpu.VMEM((M, tn), F32)], input_output_aliases=alias,
        compiler_params=_params(("parallel", "arbitrary")), name=name)(a, b, *extra)


def rms_fwd(x, g, *, name):
    S, D = x.shape
    ts = _tile(S, 1024, 16)

    def body(x_ref, g_ref, h_ref):
        xv = x_ref[...]
        r = lax.rsqrt(jnp.mean(xv * xv, axis=-1, keepdims=True) + EPS)
        h_ref[...] = (xv * r * g_ref[...]).astype(BF16)

    return pl.pallas_call(
        body, grid=(S // ts,),
        in_specs=[pl.BlockSpec((ts, D), lambda i: (i, 0)), pl.BlockSpec((1, D), lambda i: (0, 0))],
        out_specs=pl.BlockSpec((ts, D), lambda i: (i, 0)),
        out_shape=jax.ShapeDtypeStruct((S, D), BF16),
        compiler_params=_params(("parallel",)), name=name)(x, g.reshape(1, D))


def rms_bwd(x, dh, g, res, *, name):
    S, D = x.shape
    ts = _tile(S, 512, 16)
    has_res = res is not None

    def body(*refs):
        x_ref, dh_ref, g_ref = refs[:3]
        res_ref = refs[3] if has_res else None
        dx_ref, dg_ref = refs[-2:]
        i = pl.program_id(0)
        xv, dhv = x_ref[...], dh_ref[...].astype(F32)
        r = lax.rsqrt(jnp.mean(xv * xv, axis=-1, keepdims=True) + EPS)
        u = dhv * g_ref[...]
        dx = r * u - xv * (r * r * r) * jnp.mean(xv * u, axis=-1, keepdims=True)
        if has_res:
            dx = res_ref[...] + dx
        dx_ref[...] = dx
        part = jnp.sum(dhv * xv * r, axis=0, keepdims=True)

        @pl.when(i == 0)
        def _():
            dg_ref[...] = part

        @pl.when(i > 0)
        def _():
            dg_ref[...] += part

    row = pl.BlockSpec((ts, D), lambda i: (i, 0))
    vec = pl.BlockSpec((1, D), lambda i: (0, 0))
    args = (x, dh, g.reshape(1, D)) + ((res,) if has_res else ())
    return pl.pallas_call(
        body, grid=(S // ts,), in_specs=[row, row, vec] + ([row] if has_res else []),
        out_specs=[row, vec],
        out_shape=[jax.ShapeDtypeStruct((S, D), F32), jax.ShapeDtypeStruct((1, D), F32)],
        compiler_params=_params(("arbitrary",)), name=name)(*args)


def headnorm_fwd(x, g, *, scale=None, name):
    H, S, d = x.shape
    ts = _tile(S, 1024, 16)
    g = jnp.broadcast_to(g.reshape(-1, 1, d), (H, 1, d))

    def body(x_ref, g_ref, o_ref):
        xv = x_ref[...]
        r = lax.rsqrt(jnp.mean(xv * xv, axis=-1, keepdims=True) + EPS)
        y = xv * r * g_ref[...]
        o_ref[...] = (y if scale is None else y * scale).astype(BF16)

    return pl.pallas_call(
        body, grid=(H, S // ts),
        in_specs=[pl.BlockSpec((None, ts, d), lambda h, i: (h, i, 0)), pl.BlockSpec((None, 1, d), lambda h, i: (h, 0, 0))],
        out_specs=pl.BlockSpec((None, ts, d), lambda h, i: (h, i, 0)),
        out_shape=jax.ShapeDtypeStruct((H, S, d), BF16),
        compiler_params=_params(("parallel", "parallel")), name=name)(x, g)


def headnorm_bwd(x, dy, g, *, name):
    H, S, d = x.shape
    ts = _tile(S, 1024, 16)

    def body(x_ref, dy_ref, g_ref, dx_ref, dg_ref):
        first = jnp.logical_and(pl.program_id(0) == 0, pl.program_id(1) == 0)
        xv, dyv = x_ref[...], dy_ref[...]
        r = lax.rsqrt(jnp.mean(xv * xv, axis=-1, keepdims=True) + EPS)
        u = dyv * g_ref[...]
        dx_ref[...] = r * u - xv * (r * r * r) * jnp.mean(xv * u, axis=-1, keepdims=True)
        part = jnp.sum(dyv * xv * r, axis=0, keepdims=True)

        @pl.when(first)
        def _():
            dg_ref[...] = part

        @pl.when(jnp.logical_not(first))
        def _():
            dg_ref[...] += part

    blk = pl.BlockSpec((None, ts, d), lambda h, i: (h, i, 0))
    vec = pl.BlockSpec((1, d), lambda h, i: (0, 0))
    return pl.pallas_call(
        body, grid=(H, S // ts), in_specs=[blk, blk, vec], out_specs=[blk, vec],
        out_shape=[jax.ShapeDtypeStruct((H, S, d), F32), jax.ShapeDtypeStruct((1, d), F32)],
        compiler_params=_params(("arbitrary", "arbitrary")), name=name)(x, dy, g.reshape(1, d))


def _split3(x):
    x1 = x.astype(BF16)
    r1 = x - x1.astype(F32)
    x2 = r1.astype(BF16)
    x3 = (r1 - x2.astype(F32)).astype(BF16)
    return x1, x2, x3


def _tri_ones(n, lower):
    r = lax.broadcasted_iota(jnp.int32, (n, n), 0)
    c = lax.broadcasted_iota(jnp.int32, (n, n), 1)
    return jnp.where((c <= r) if lower else (c >= r), 1.0, 0.0).astype(BF16)


def fgate_fwd(z, bias, *, name):
    S, L = z.shape
    tb = _tile(S, 256, 16)

    def body(z_ref, b_ref, c_ref, carry):
        i = pl.program_id(0)

        @pl.when(i == 0)
        def _():
            carry[...] = jnp.zeros_like(carry)

        zz = z_ref[...] + b_ref[...]
        lf = jnp.minimum(zz, 0.0) - jnp.log(1.0 + jnp.exp(-jnp.abs(zz)))
        tri = _tri_ones(tb, True)
        x1, x2, x3 = _split3(lf)
        c = (_dot(tri, x1, 1, 0) + _dot(tri, x2, 1, 0)) + _dot(tri, x3, 1, 0) + carry[...]
        c_ref[...] = c
        carry[...] += jnp.sum(lf, axis=0, keepdims=True)

    return pl.pallas_call(
        body, grid=(S // tb,),
        in_specs=[pl.BlockSpec((tb, L), lambda i: (i, 0)), pl.BlockSpec((1, L), lambda i: (0, 0))],
        out_specs=pl.BlockSpec((tb, L), lambda i: (i, 0)),
        out_shape=jax.ShapeDtypeStruct((S, L), F32),
        scratch_shapes=[pltpu.VMEM((1, L), F32)],
        compiler_params=_params(("arbitrary",)), name=name)(z, bias)


def fgate_bwd(z, bias, drs, dcs, *, name):
    S, L = z.shape
    tb = _tile(S, 256, 16)
    nb = S // tb

    def body(z_ref, b_ref, drs_ref, dcs_ref, dz_ref, db_ref, carry):
        i = pl.program_id(0)

        @pl.when(i == 0)
        def _():
            carry[...] = jnp.zeros_like(carry)

        tri = _tri_ones(tb, False)
        dc = drs_ref[...] - dcs_ref[...]
        x1, x2, x3 = _split3(dc)
        dlf = (_dot(tri, x1, 1, 0) + _dot(tri, x2, 1, 0)) + _dot(tri, x3, 1, 0) + carry[...]
        carry[...] += jnp.sum(dc, axis=0, keepdims=True)
        dz = dlf * _sigmoid(-(z_ref[...] + b_ref[...]))
        dz_ref[...] = dz
        part = jnp.sum(dz, axis=0, keepdims=True)

        @pl.when(i == 0)
        def _():
            db_ref[...] = part

        @pl.when(i > 0)
        def _():
            db_ref[...] += part

    rev = pl.BlockSpec((tb, L), lambda i: (nb - 1 - i, 0))
    vec = pl.BlockSpec((1, L), lambda i: (0, 0))
    return pl.pallas_call(
        body, grid=(nb,), in_specs=[rev, vec, rev, rev], out_specs=[rev, vec],
        out_shape=[jax.ShapeDtypeStruct((S, L), F32), jax.ShapeDtypeStruct((1, L), F32)],
        scratch_shapes=[pltpu.VMEM((1, L), F32)],
        compiler_params=_params(("arbitrary",)), name=name)(z, bias, drs, dcs)


def attn_fwd(q, k, v, cq, ck, *, causal, name):
    H, Sq, d = q.shape
    Sk = k.shape[1]
    tq = _tile(Sq, ATTN_Q_BLOCK, LANES)
    tk = _tile(Sk, ATTN_K_BLOCK, LANES)
    nq, nk = Sq // tq, Sk // tk
    scale = 1.0 / math.sqrt(d)
    bias = cq is not None

    def body(*refs):
        q_ref, k_ref, v_ref = refs[:3]
        cq_ref, ck_ref = (refs[3], refs[4]) if bias else (None, None)
        o_ref, lse_ref, m_sc, l_sc, acc_sc = refs[-5:]
        i, j = pl.program_id(1), pl.program_id(2)

        @pl.when(j == 0)
        def _():
            m_sc[...] = jnp.full_like(m_sc, NEG)
            l_sc[...] = jnp.zeros_like(l_sc)
            acc_sc[...] = jnp.zeros_like(acc_sc)

        def compute(masked):
            s = _dot(q_ref[...], k_ref[...], 1, 1)
            if bias:
                s = s + (cq_ref[...] - ck_ref[...])
            if masked:
                row = i * tq + lax.broadcasted_iota(jnp.int32, (tq, tk), 0)
                col = j * tk + lax.broadcasted_iota(jnp.int32, (tq, tk), 1)
                s = jnp.where(col <= row, s, NEG)
            m_prev = m_sc[...]
            m_new = jnp.maximum(m_prev, jnp.max(s, axis=1, keepdims=True))
            alpha = jnp.exp(m_prev - m_new)
            p = jnp.exp(s - m_new)
            l_sc[...] = alpha * l_sc[...] + jnp.sum(p, axis=1, keepdims=True)
            acc_sc[...] = alpha * acc_sc[...] + _dot(p.astype(BF16), v_ref[...], 1, 0)
            m_sc[...] = m_new

        if causal:
            live = j * tk <= i * tq + (tq - 1)
            crosses = j * tk + (tk - 1) > i * tq
            pl.when(jnp.logical_and(live, crosses))(functools.partial(compute, True))
            pl.when(jnp.logical_and(live, jnp.logical_not(crosses)))(functools.partial(compute, False))
        else:
            compute(False)

        @pl.when(j == nk - 1)
        def _():
            o_ref[...] = acc_sc[...] / l_sc[...]
            lse_ref[...] = m_sc[...] + jnp.log(l_sc[...])

    def kv_idx(h, i, j):
        return (h, jnp.minimum(j, (i * tq + tq - 1) // tk) if causal else j, 0)

    in_specs = [pl.BlockSpec((None, tq, d), lambda h, i, j: (h, i, 0)),
                pl.BlockSpec((None, tk, d), kv_idx), pl.BlockSpec((None, tk, d), kv_idx)]
    args = [q, k, v]
    if bias:
        in_specs += [pl.BlockSpec((None, tq, 1), lambda h, i, j: (h, i, 0)),
                     pl.BlockSpec((None, 1, tk), lambda h, i, j: (h, 0, kv_idx(h, i, j)[1]))]
        args += [cq, ck]
    return pl.pallas_call(
        body, grid=(H, nq, nk), in_specs=in_specs,
        out_specs=[pl.BlockSpec((None, tq, d), lambda h, i, j: (h, i, 0)),
                   pl.BlockSpec((None, tq, 1), lambda h, i, j: (h, i, 0))],
        out_shape=[jax.ShapeDtypeStruct((H, Sq, d), F32), jax.ShapeDtypeStruct((H, Sq, 1), F32)],
        scratch_shapes=[pltpu.VMEM((tq, 1), F32), pltpu.VMEM((tq, 1), F32), pltpu.VMEM((tq, d), F32)],
        compiler_params=_params(("parallel", "parallel", "arbitrary")), name=name)(*args)


def attn_delta(o, do, *, name):
    H, S, d = o.shape
    ts = _tile(S, 1024, 16)

    def body(o_ref, do_ref, out_ref):
        out_ref[...] = jnp.sum(o_ref[...] * do_ref[...], axis=-1, keepdims=True)

    blk = pl.BlockSpec((None, ts, d), lambda h, i: (h, i, 0))
    return pl.pallas_call(
        body, grid=(H, S // ts), in_specs=[blk, blk],
        out_specs=pl.BlockSpec((None, ts, 1), lambda h, i: (h, i, 0)),
        out_shape=jax.ShapeDtypeStruct((H, S, 1), F32),
        compiler_params=_params(("parallel", "parallel")), name=name)(o, do)


def attn_bwd(q, k, v, cq, ck, do, lse, delta, *, causal, name):
    H, Sq, d = q.shape
    Sk = k.shape[1]
    tq = _tile(Sq, ATTN_Q_BLOCK, LANES)
    tk = _tile(Sk, ATTN_K_BLOCK, LANES)
    nq, nk = Sq // tq, Sk // tk
    scale = 1.0 / math.sqrt(d)
    bias = cq is not None

    def body(*refs):
        q_ref, k_ref, v_ref, do_ref, lse_ref, dl_ref = refs[:6]
        cq_ref, ck_ref = (refs[6], refs[7]) if bias else (None, None)
        outs = refs[8:] if bias else refs[6:]
        dq_ref, dk_ref, dv_ref = outs[:3]
        dcs_ref, drs_ref = (outs[3], outs[4]) if bias else (None, None)
        j, i = pl.program_id(1), pl.program_id(2)

        @pl.when(i == 0)
        def _():
            dk_ref[...] = jnp.zeros_like(dk_ref)
            dv_ref[...] = jnp.zeros_like(dv_ref)
            if bias:
                dcs_ref[...] = jnp.zeros_like(dcs_ref)

        rows = pl.ds(pl.multiple_of(i * tq, tq), tq)

        def compute(masked):
            qv, kv, vv, dov = q_ref[...], k_ref[...], v_ref[...], do_ref[...]
            s = _dot(qv, kv, 1, 1)
            if bias:
                s = s + (cq_ref[...] - ck_ref[...])
            p = jnp.exp(s - lse_ref[...])
            if masked:
                row = i * tq + lax.broadcasted_iota(jnp.int32, (tq, tk), 0)
                col = j * tk + lax.broadcasted_iota(jnp.int32, (tq, tk), 1)
                p = jnp.where(col <= row, p, 0.0)
            dv_ref[...] += _dot(p.astype(BF16), dov, 0, 0)
            dp = _dot(dov, vv, 1, 1)
            ds = p * (dp - dl_ref[...])
            dsb = ds.astype(BF16)
            dk_ref[...] += _dot(dsb, qv, 0, 0)
            if bias:
                dcs_ref[...] += jnp.sum(ds, axis=0, keepdims=True)
            dq_part = _dot(dsb, kv, 1, 0) * scale
            row_part = jnp.sum(ds, axis=1, keepdims=True) if bias else None

            @pl.when(j == 0)
            def _():
                dq_ref[rows, :] = dq_part
                if bias:
                    drs_ref[rows, :] = row_part

            @pl.when(j > 0)
            def _():
                dq_ref[rows, :] += dq_part
                if bias:
                    drs_ref[rows, :] += row_part

        if causal:
            live = j * tk <= i * tq + (tq - 1)
            crosses = j * tk + (tk - 1) > i * tq
            pl.when(jnp.logical_and(live, crosses))(functools.partial(compute, True))
            pl.when(jnp.logical_and(live, jnp.logical_not(crosses)))(functools.partial(compute, False))
        else:
            compute(False)

    def q_idx(h, j, i):
        return (h, jnp.maximum(i, (j * tk) // tq) if causal else i, 0)

    qblk = pl.BlockSpec((None, tq, d), q_idx)
    kblk = pl.BlockSpec((None, tk, d), lambda h, j, i: (h, j, 0))
    col1 = pl.BlockSpec((None, tq, 1), q_idx)
    in_specs = [qblk, kblk, kblk, qblk, col1, col1]
    args = [q, k, v, do, lse, delta]
    out_specs = [pl.BlockSpec((None, Sq, d), lambda h, j, i: (h, 0, 0)), kblk, kblk]
    out_shape = [jax.ShapeDtypeStruct((H, Sq, d), F32), jax.ShapeDtypeStruct((H, Sk, d), F32),
                 jax.ShapeDtypeStruct((H, Sk, d), F32)]
    if bias:
        in_specs += [col1, pl.BlockSpec((None, 1, tk), lambda h, j, i: (h, 0, j))]
        args += [cq, ck]
        out_specs += [pl.BlockSpec((None, 1, tk), lambda h, j, i: (h, 0, j)),
                      pl.BlockSpec((None, Sq, 1), lambda h, j, i: (h, 0, 0))]
        out_shape += [jax.ShapeDtypeStruct((H, 1, Sk), F32), jax.ShapeDtypeStruct((H, Sq, 1), F32)]
    return pl.pallas_call(
        body, grid=(H, nk, nq), in_specs=in_specs, out_specs=out_specs, out_shape=out_shape,
        compiler_params=_params(("parallel", "arbitrary", "arbitrary")), name=name)(*args)


def _tril_mask(n):
    r = lax.broadcasted_iota(jnp.int32, (n, n), 0)
    c = lax.broadcasted_iota(jnp.int32, (n, n), 1)
    return c <= r


def gmlp_fwd(up, vp, vg, w, b, *, name):
    G, S, d = up.shape
    ts = _tile(S, 1024, CHUNK)

    def body(up_ref, vp_ref, vg_ref, w_ref, b_ref, o_ref):
        wt = jnp.where(_tril_mask(CHUNK), w_ref[...], 0.0).astype(BF16)
        for c in range(ts // CHUNK):
            sl = pl.ds(c * CHUNK, CHUNK)
            vz = _gelu(vp_ref[sl, :])
            r = lax.rsqrt(jnp.mean(vz * vz, axis=-1, keepdims=True) + EPS)
            vh = (vz * r * vg_ref[...]).astype(BF16)
            gate = _dot(wt, vh, 1, 0) + b_ref[...]
            o_ref[sl, :] = _gelu(up_ref[sl, :]) * gate

    blk = pl.BlockSpec((None, ts, d), lambda g, i: (g, i, 0))
    return pl.pallas_call(
        body, grid=(G, S // ts),
        in_specs=[blk, blk, pl.BlockSpec((None, 1, d), lambda g, i: (g, 0, 0)),
                  pl.BlockSpec((None, CHUNK, CHUNK), lambda g, i: (g, 0, 0)),
                  pl.BlockSpec((None, CHUNK, 1), lambda g, i: (g, 0, 0))],
        out_specs=blk, out_shape=jax.ShapeDtypeStruct((G, S, d), F32),
        compiler_params=_params(("parallel", "parallel")), name=name)(up, vp, vg, w, b)


def gmlp_bwd(up, vp, vg, w, wT, b, do, *, name):
    G, S, d = up.shape
    ts = _tile(S, 1024, CHUNK)

    def body(up_ref, vp_ref, vg_ref, w_ref, wT_ref, b_ref, do_ref, dup_ref, dvp_ref, dw_ref, db_ref, dvg_ref):
        i = pl.program_id(1)

        @pl.when(i == 0)
        def _():
            dw_ref[...] = jnp.zeros_like(dw_ref)
            db_ref[...] = jnp.zeros_like(db_ref)
            dvg_ref[...] = jnp.zeros_like(dvg_ref)

        mask = _tril_mask(CHUNK)
        wt = jnp.where(mask, w_ref[...], 0.0).astype(BF16)
        wtT = jnp.where(_tril_mask(CHUNK).T, wT_ref[...], 0.0).astype(BF16)
        vgain = vg_ref[...]
        for c in range(ts // CHUNK):
            sl = pl.ds(c * CHUNK, CHUNK)
            u_pre, v_pre, dout = up_ref[sl, :], vp_ref[sl, :], do_ref[sl, :]
            vz = _gelu(v_pre)
            r = lax.rsqrt(jnp.mean(vz * vz, axis=-1, keepdims=True) + EPS)
            vh = (vz * r * vgain).astype(BF16)
            gate = _dot(wt, vh, 1, 0) + b_ref[...]
            dgate = dout * _gelu(u_pre)
            dup_ref[sl, :] = dout * gate * _gelu_grad(u_pre)
            dgb = dgate.astype(BF16)
            dw_ref[...] += jnp.where(mask, _dot(dgb, vh, 1, 1), 0.0)
            db_ref[...] += jnp.sum(dgate, axis=1, keepdims=True)
            dvh = _dot(wtT, dgb, 1, 0)
            dvg_ref[...] += jnp.sum(dvh * vz * r, axis=0, keepdims=True)
            t = dvh * vgain
            dvz = r * t - vz * (r * r * r) * jnp.mean(vz * t, axis=-1, keepdims=True)
            dvp_ref[sl, :] = dvz * _gelu_grad(v_pre)

    blk = pl.BlockSpec((None, ts, d), lambda g, i: (g, i, 0))
    wblk = pl.BlockSpec((None, CHUNK, CHUNK), lambda g, i: (g, 0, 0))
    bblk = pl.BlockSpec((None, CHUNK, 1), lambda g, i: (g, 0, 0))
    gblk = pl.BlockSpec((None, 1, d), lambda g, i: (g, 0, 0))
    return pl.pallas_call(
        body, grid=(G, S // ts), in_specs=[blk, blk, gblk, wblk, wblk, bblk, blk],
        out_specs=[blk, blk, wblk, bblk, gblk],
        out_shape=[jax.ShapeDtypeStruct((G, S, d), F32), jax.ShapeDtypeStruct((G, S, d), F32),
                   jax.ShapeDtypeStruct((G, CHUNK, CHUNK), F32), jax.ShapeDtypeStruct((G, CHUNK, 1), F32),
                   jax.ShapeDtypeStruct((G, 1, d), F32)],
        compiler_params=_params(("parallel", "arbitrary")), name=name)(up, vp, vg, w, wT, b, do)


def loss_head(y, target, *, name):
    S, D = y.shape
    ts = _tile(S, 512, 8)

    def body(y_ref, t_ref, dy_ref, loss_ref):
        i = pl.program_id(0)
        e = y_ref[...] - t_ref[...]
        dy_ref[...] = e * (1.0 / D)
        part = jnp.sum(jnp.sum(e * e, axis=1, keepdims=True), axis=0, keepdims=True) * (0.5 / D)

        @pl.when(i == 0)
        def _():
            loss_ref[...] = part

        @pl.when(i > 0)
        def _():
            loss_ref[...] += part

    row = pl.BlockSpec((ts, D), lambda i: (i, 0))
    return pl.pallas_call(
        body, grid=(S // ts,), in_specs=[row, row],
        out_specs=[row, pl.BlockSpec((1, 1), lambda i: (0, 0))],
        out_shape=[jax.ShapeDtypeStruct((S, D), F32), jax.ShapeDtypeStruct((1, 1), F32)],
        compiler_params=_params(("arbitrary",)), name=name)(y, target)


def adamw(w, g, m, v, *, name):
    shape = w.shape
    C = shape[-1]
    R = w.size // C
    tr = _tile(R, max(8, (256 * 1024) // C // 8 * 8), 8)

    def body(w_ref, g_ref, m_ref, v_ref, d_ref, nm_ref, nv_ref):
        gv = g_ref[...]
        nm = ADAM_B1 * m_ref[...] + (1.0 - ADAM_B1) * gv
        nv = ADAM_B2 * v_ref[...] + (1.0 - ADAM_B2) * (gv * gv)
        m_hat = nm / (1.0 - ADAM_B1 ** ADAM_STEP)
        v_hat = nv / (1.0 - ADAM_B2 ** ADAM_STEP)
        d_ref[...] = -ADAM_LR * (m_hat / (jnp.sqrt(v_hat) + ADAM_EPS) + ADAM_WD * w_ref[...])
        nm_ref[...] = nm
        nv_ref[...] = nv

    blk = pl.BlockSpec((tr, C), lambda i: (i, 0))
    out = pl.pallas_call(
        body, grid=(R // tr,), in_specs=[blk] * 4, out_specs=[blk] * 3,
        out_shape=[jax.ShapeDtypeStruct((R, C), F32)] * 3,
        compiler_params=_params(("parallel",)), name=name)(*(a.reshape(R, C) for a in (w, g, m, v)))
    return tuple(o.reshape(shape) for o in out)


def pair_sum(p, landed, half, *, name):
    n, _, R, C = p.shape
    tr = _tile(R, 256, 16)

    def body(half_ref, p_ref, l_ref, o_ref):
        o_ref[...] = (p_ref[...] + l_ref[...]).astype(BF16)

    return pl.pallas_call(
        body,
        grid_spec=pltpu.PrefetchScalarGridSpec(
            num_scalar_prefetch=1, grid=(n, R // tr),
            in_specs=[pl.BlockSpec((None, None, tr, C), lambda k, r, half_ref: (k, half_ref[0], r, 0)),
                      pl.BlockSpec((None, tr, C), lambda k, r, half_ref: (k, r, 0))],
            out_specs=pl.BlockSpec((None, tr, C), lambda k, r, half_ref: (k, r, 0))),
        out_shape=jax.ShapeDtypeStruct((n, R, C), BF16),
        compiler_params=_params(("parallel", "parallel")), name=name)(half, p, landed)


def chip_sum(own, landed, chip, *, name):
    n, R, C = own.shape
    tr = _tile(R, 256, 16)

    def body(chip_ref, own_ref, *rest):
        l_refs, o_ref = rest[:n], rest[n]
        me = chip_ref[0]
        acc = None
        for d in range(n):
            term = jnp.where(me == d, own_ref[...], l_refs[d][...]).astype(F32)
            acc = term if acc is None else acc + term
        o_ref[...] = acc

    def landed_spec(d):
        return pl.BlockSpec((None, tr, C), lambda r, chip_ref: (jnp.where(chip_ref[0] == d, (d + 1) % n, d), r, 0))

    return pl.pallas_call(
        body,
        grid_spec=pltpu.PrefetchScalarGridSpec(
            num_scalar_prefetch=1, grid=(R // tr,),
            in_specs=[pl.BlockSpec((None, tr, C), lambda r, chip_ref: (chip_ref[0], r, 0))]
            + [landed_spec(d) for d in range(n)],
            out_specs=pl.BlockSpec((tr, C), lambda r, chip_ref: (r, 0))),
        out_shape=jax.ShapeDtypeStruct((R, C), F32),
        compiler_params=_params(("parallel",)), name=name)(chip, own, *([landed] * n))


def ordered_sum(parts, *, name):
    n, R, C = parts.shape
    tr = _tile(R, 256, 16)

    def body(p_ref, o_ref):
        acc = p_ref[0].astype(F32)
        for d in range(1, n):
            acc = acc + p_ref[d].astype(F32)
        o_ref[...] = acc

    return pl.pallas_call(
        body, grid=(R // tr,), in_specs=[pl.BlockSpec((n, tr, C), lambda r: (0, r, 0))],
        out_specs=pl.BlockSpec((tr, C), lambda r: (r, 0)),
        out_shape=jax.ShapeDtypeStruct((R, C), F32),
        compiler_params=_params(("parallel",)), name=name)(parts)


_ANY = pl.BlockSpec(memory_space=pl.ANY)


def _position():
    return lax.axis_index("x"), lax.axis_index("y"), lax.axis_index("c")


def _remote(src, dst, send_sem, recv_sem, device):
    return pltpu.make_async_remote_copy(src_ref=src, dst_ref=dst, send_sem=send_sem, recv_sem=recv_sem,
                                        device_id=device, device_id_type=MESH_ID)


def _small_all_gather(s_ref, all_ref, send_sems, recv_sems, x, y, c):
    me = 4 * x + 2 * y + c
    copies = []
    for f in range(1, 8):
        peer = ((1 - x) if f & 4 else x, (1 - y) if f & 2 else y, (1 - c) if f & 1 else c)
        cp = _remote(s_ref, all_ref.at[me], send_sems.at[f - 1], recv_sems.at[f - 1], peer)
        cp.start()
        copies.append((cp, peer, f - 1))

    def finish():
        for cp, peer, s in copies:
            slot = all_ref.at[4 * peer[0] + 2 * peer[1] + peer[2]]
            _remote(slot, slot, send_sems.at[s], recv_sems.at[s], peer).wait_recv()
        for cp, _, _ in copies:
            cp.wait_send()

    return finish


def gather_weights(slabs, small_slab, *, name):
    n = len(slabs)

    def body(*refs):
        outs, all_ref = refs[n + 1:2 * n + 1], refs[2 * n + 1]
        send_sems, recv_sems, s_send, s_recv = refs[2 * n + 2:]
        x, y, c = _position()
        k = 2 * x + y
        sibling = (x, y, 1 - c)
        chips = [(1 - x, y), (x, 1 - y), (1 - x, 1 - y)]
        finish_small = _small_all_gather(all_ref.at[4 * x + 2 * y + c], all_ref, s_send, s_recv, x, y, c)
        first = []
        for j, (px, py) in enumerate(chips):
            for w in range(n):
                slot = outs[w].at[k, c]
                cp = _remote(slot, slot, send_sems.at[w, j], recv_sems.at[w, j], (px, py, c))
                cp.start()
                first.append(cp)
        passed = []
        for j, (px, py) in enumerate(chips):
            for w in range(n):
                slot = outs[w].at[2 * px + py, c]
                _remote(slot, slot, send_sems.at[w, j], recv_sems.at[w, j], (px, py, c)).wait_recv()
                cp = _remote(slot, slot, send_sems.at[w, 3 + j], recv_sems.at[w, 3 + j], sibling)
                cp.start()
                passed.append(cp)
        for j, (px, py) in enumerate(chips):
            for w in range(n):
                slot = outs[w].at[2 * px + py, 1 - c]
                _remote(slot, slot, send_sems.at[w, 3 + j], recv_sems.at[w, 3 + j], sibling).wait_recv()
        for cp in first + passed:
            cp.wait_send()
        finish_small()

    args = list(slabs) + [small_slab]
    out = pl.pallas_call(
        body, in_specs=[_ANY] * (n + 1), out_specs=[_ANY] * (n + 1),
        out_shape=[jax.ShapeDtypeStruct(a.shape, a.dtype) for a in args],
        input_output_aliases={i: i for i in range(n + 1)},
        scratch_shapes=[pltpu.SemaphoreType.DMA((n, 6)), pltpu.SemaphoreType.DMA((n, 6)),
                        pltpu.SemaphoreType.DMA((7,)), pltpu.SemaphoreType.DMA((7,))],
        name=name)(*args)
    return out[:n], out[n]


def exchange_with_sibling(parts, small_slab, *, name):
    n = len(parts)

    def body(*refs):
        p_refs = refs[:n]
        lands, all_ref = refs[n + 1:2 * n + 1], refs[2 * n + 1]
        send_sems, recv_sems, s_send, s_recv = refs[2 * n + 2:]
        x, y, c = _position()
        sibling = (x, y, 1 - c)
        finish_small = _small_all_gather(all_ref.at[4 * x + 2 * y + c], all_ref, s_send, s_recv, x, y, c)
        sends = []
        for w in range(n):
            for d in range(4):
                cp = _remote(p_refs[w].at[d, 1 - c], lands[w].at[d], send_sems.at[w, d], recv_sems.at[w, d], sibling)
                cp.start()
                sends.append(cp)
        for cp in sends:
            cp.wait_recv()
        for cp in sends:
            cp.wait_send()
        finish_small()

    out = pl.pallas_call(
        body, in_specs=[_ANY] * (n + 1), out_specs=[_ANY] * (n + 1),
        out_shape=[jax.ShapeDtypeStruct((4,) + p.shape[2:], p.dtype) for p in parts]
        + [jax.ShapeDtypeStruct(small_slab.shape, small_slab.dtype)],
        input_output_aliases={n: n},
        scratch_shapes=[pltpu.SemaphoreType.DMA((n, 4)), pltpu.SemaphoreType.DMA((n, 4)),
                        pltpu.SemaphoreType.DMA((7,)), pltpu.SemaphoreType.DMA((7,))],
        name=name)(*parts, small_slab)
    return out[:n], out[n]


def scatter_to_chips(parts, *, name):
    n = len(parts)

    def body(*refs):
        q_refs, outs = refs[:n], refs[n:2 * n]
        send_sems, recv_sems = refs[2 * n:]
        x, y, c = _position()
        k = 2 * x + y
        chips = [(1 - x, y), (x, 1 - y), (1 - x, 1 - y)]
        sends = []
        for j, (px, py) in enumerate(chips):
            for w in range(n):
                cp = _remote(q_refs[w].at[2 * px + py], outs[w].at[k], send_sems.at[w, j], recv_sems.at[w, j], (px, py, c))
                cp.start()
                sends.append(cp)
        for j, (px, py) in enumerate(chips):
            for w in range(n):
                slot = outs[w].at[2 * px + py]
                _remote(slot, slot, send_sems.at[w, j], recv_sems.at[w, j], (px, py, c)).wait_recv()
        for cp in sends:
            cp.wait_send()

    return pl.pallas_call(
        body, in_specs=[_ANY] * n, out_specs=[_ANY] * n,
        out_shape=[jax.ShapeDtypeStruct(q.shape, q.dtype) for q in parts],
        scratch_shapes=[pltpu.SemaphoreType.DMA((n, 3)), pltpu.SemaphoreType.DMA((n, 3))],
        name=name)(*parts)


def share_with_sibling(parts, *, name):
    n = len(parts)

    def body(*refs):
        r_refs, outs = refs[:n], refs[n:2 * n]
        send_sems, recv_sems = refs[2 * n:]
        x, y, c = _position()
        sends = []
        for w in range(n):
            cp = _remote(r_refs[w], outs[w], send_sems.at[w], recv_sems.at[w], (x, y, 1 - c))
            cp.start()
            sends.append(cp)
        for cp in sends:
            cp.wait_recv()
        for cp in sends:
            cp.wait_send()

    return pl.pallas_call(
        body, in_specs=[_ANY] * n, out_specs=[_ANY] * n,
        out_shape=[jax.ShapeDtypeStruct(r.shape, r.dtype) for r in parts],
        scratch_shapes=[pltpu.SemaphoreType.DMA((n,)), pltpu.SemaphoreType.DMA((n,))],
        name=name)(*parts)


def _cols_to_chips(full):
    *lead, R, C4 = full.shape
    t = full.reshape(*lead, R, 4, C4 // 4)
    return jnp.moveaxis(t, -2, 0)


def _chips_to_cols(sh):
    t = jnp.moveaxis(sh, 0, -2)
    return t.reshape(*t.shape[:-2], t.shape[-2] * t.shape[-1])


def _slot_in_empty(own, index, n):
    return lax.dynamic_update_slice(lax.empty((n,) + own.shape, own.dtype), own[None], (index,) + (0,) * own.ndim)


def _to_heads(x):
    S, W = x.shape
    return x.reshape(S, W // HEAD_DIM, HEAD_DIM).transpose(1, 0, 2)


def _from_heads(x):
    H, S, d = x.shape
    return x.transpose(1, 0, 2).reshape(S, H * d)


def _ffn_fwd(x, g, w_in_slab, w_out, layer, tag):
    h = rms_fwd(x, g, name=f"{tag}_rms")
    a, b, act = swiglu_fwd(h, w_in_slab, layer, name=f"{tag}_in")
    y = matmul(act, w_out, res=x, scale=0.5, tm=1024, tn=512, tk=w_out.shape[0], name=f"{tag}_out")
    return y, (x, h, a, b, act)


def _ffn_bwd(dy, saved, g, w_in_slab, w_out, layer, n_layers, dw_in_slab, dw_out_slab, tag):
    x, h, a, b, act = saved
    da, db = swiglu_bwd(dy, w_out, a, b, name=f"{tag}_dact")
    dw_out_slab = grad_rows(act, dy, dw_out_slab, layer, n_layers, scale=0.5, name=f"{tag}_dwout")
    dw_in_slab = grad_cols(h, da, db, dw_in_slab, layer, n_layers, name=f"{tag}_dwin")
    dh = ffn_dh(da, db, w_in_slab, layer, name=f"{tag}_dh")
    dx, dg = rms_bwd(x, dh, g, dy, name=f"{tag}_drms")
    return dx, dg[0], dw_in_slab, dw_out_slab


def _mem_kv(mem_n, w_kv, g_k, tag):
    kv = matmul(mem_n, w_kv, tm=256, tn=512, tk=1024, name=f"{tag}_kv")
    k_pre = _to_heads(kv[:, :MEM_WIDTH])
    kh = headnorm_fwd(k_pre, g_k, name=f"{tag}_knorm")
    vh = _to_heads(kv[:, MEM_WIDTH:]).astype(BF16)
    return k_pre, kh, vh


def _mem_attn_fwd(mq, mem_n, w_kv, g_q, g_k, tag):
    q_pre = _to_heads(mq)
    qh = headnorm_fwd(q_pre, g_q, scale=QK_SCALE, name=f"{tag}_qnorm")
    k_pre, kh, vh = _mem_kv(mem_n, w_kv, g_k, tag)
    o, lse = attn_fwd(qh, kh, vh, None, None, causal=False, name=f"{tag}_attn")
    return o, (q_pre, qh, k_pre, kh, vh, o, lse)


def _mem_attn_bwd(do, saved, mem_n, g_q, g_k, layer, n_layers, dw_kv_slab, tag):
    q_pre, qh, k_pre, kh, vh, o, lse = saved
    delta = attn_delta(o, do, name=f"{tag}_delta")
    dqh, dkh, dvh = attn_bwd(qh, kh, vh, None, None, do.astype(BF16), lse, delta, causal=False, name=f"{tag}_dattn")
    dq_pre, dgq = headnorm_bwd(q_pre, dqh, g_q, name=f"{tag}_dqnorm")
    dk_pre, dgk = headnorm_bwd(k_pre, dkh, g_k, name=f"{tag}_dknorm")
    dkv = jnp.concatenate([_from_heads(dk_pre), _from_heads(dvh)], axis=1)
    dw_kv_slab = grad_rows(mem_n, dkv, dw_kv_slab, layer, n_layers, name=f"{tag}_dwkv")
    return _from_heads(dq_pre), dgq[0], dgk[0], dw_kv_slab, dkv


def _fox_fwd(proj, b_f, g_q, g_k, tok, tag):
    S = proj.shape[0]
    H = tok // HEAD_DIM
    q_pre, k_pre = _to_heads(proj[:, :tok]), _to_heads(proj[:, tok:2 * tok])
    vh = _to_heads(proj[:, 2 * tok:3 * tok]).astype(BF16)
    z = proj[:, 3 * tok + MEM_WIDTH:]
    bias = jnp.pad(b_f.reshape(1, H), ((0, 0), (0, LANES - H)))
    qh = headnorm_fwd(q_pre, g_q, scale=QK_SCALE, name=f"{tag}_qnorm")
    kh = headnorm_fwd(k_pre, g_k, name=f"{tag}_knorm")
    c = fgate_fwd(z, bias, name=f"{tag}_fgate")
    ch = c[:, :H].T
    cq, ck = ch.reshape(H, S, 1), ch.reshape(H, 1, S)
    o, lse = attn_fwd(qh, kh, vh, cq, ck, causal=True, name=f"{tag}_attn")
    return o, (q_pre, k_pre, qh, kh, vh, z, bias, cq, ck, o, lse)


def _fox_bwd(do, saved, g_q, g_k, tag):
    q_pre, k_pre, qh, kh, vh, z, bias, cq, ck, o, lse = saved
    H, S, _ = q_pre.shape
    delta = attn_delta(o, do, name=f"{tag}_delta")
    dqh, dkh, dvh, dcs, drs = attn_bwd(qh, kh, vh, cq, ck, do.astype(BF16), lse, delta, causal=True,
                                       name=f"{tag}_dattn")
    dq_pre, dgq = headnorm_bwd(q_pre, dqh, g_q, name=f"{tag}_dqnorm")
    dk_pre, dgk = headnorm_bwd(k_pre, dkh, g_k, name=f"{tag}_dknorm")
    dcs_t = jnp.pad(dcs.reshape(H, S).T, ((0, 0), (0, LANES - H)))
    drs_t = jnp.pad(drs.reshape(H, S).T, ((0, 0), (0, LANES - H)))
    dz, dbias = fgate_bwd(z, bias, drs_t, dcs_t, name=f"{tag}_dfgate")
    dqkv = jnp.concatenate([_from_heads(dq_pre), _from_heads(dk_pre), _from_heads(dvh)], axis=1)
    return dqkv, dz, dbias[0, :H], dgq[0], dgk[0]


def _gmlp_args(v_gain, w_s, b_s):
    G = w_s.shape[0]
    return v_gain.reshape(G, 1, HEAD_DIM), w_s, b_s.reshape(G, CHUNK, 1)


def local_step(x, mem, target, W):
    S, D = x.shape
    tok = D - MEM_WIDTH
    H = tok // HEAD_DIM
    depth = W["norm_ffn1"].shape[0]
    mem_n = rms_fwd(mem, W["mem_norm"], name="mem_rms")
    saved = []
    for i in range(depth):
        kind, j = i % 2, i // 2
        t = f"l{i}"
        x1, s1 = _ffn_fwd(x, W["norm_ffn1"][i], W["ffn1_w_in"], W["ffn1_w_out"][i], i, f"{t}_ffn1")
        h = rms_fwd(x1, W["norm_mix"][i], name=f"{t}_mix_rms")
        w_mix = W["fox_w_in"][j] if kind == 0 else W["gmlp_w_in"][j]
        proj = matmul(h, w_mix, tm=1024, tn=896, tk=D, name=f"{t}_mix_in")
        if kind == 0:
            o_tok, s_tok = _fox_fwd(proj, W["fox_b_f"][j], W["fox_q_norm"][j], W["fox_k_norm"][j], tok, f"{t}_fox")
            mq = proj[:, 3 * tok:3 * tok + MEM_WIDTH]
        else:
            up, vp = _to_heads(proj[:, :tok]), _to_heads(proj[:, tok:2 * tok])
            vg, ws, bs = _gmlp_args(W["gmlp_v_norm"][j], W["gmlp_w_s"][j], W["gmlp_b_s"][j])
            o_tok = gmlp_fwd(up, vp, vg, ws, bs, name=f"{t}_gmlp")
            s_tok = (up, vp)
            mq = proj[:, 2 * tok:2 * tok + MEM_WIDTH]
        o_mem, s_mem = _mem_attn_fwd(mq, mem_n, W["mem_w_kv"][i], W["mem_q_norm"][i], W["mem_k_norm"][i], f"{t}_mem")
        mix = jnp.concatenate([_from_heads(o_tok), _from_heads(o_mem)], axis=1).astype(BF16)
        x2 = matmul(mix, W["w_out"][i], res=x1, tm=1024, tn=512, tk=D, name=f"{t}_mix_out")
        x3, s3 = _ffn_fwd(x2, W["norm_ffn2"][i], W["ffn2_w_in"], W["ffn2_w_out"][i], i, f"{t}_ffn2")
        saved.append((s1, x1, h, s_tok, s_mem, mix, s3))
        x = x3

    dx, loss = loss_head(x, target, name="loss_head")

    G = {k: [None] * depth for k in ("norm_ffn1", "norm_mix", "norm_ffn2", "mem_q_norm", "mem_k_norm")}
    n_fox, n_gmlp = (depth + 1) // 2, depth // 2
    for k in ("fox_w_in", "fox_b_f", "fox_q_norm", "fox_k_norm"):
        G[k] = [None] * n_fox
    for k in ("gmlp_w_in", "gmlp_v_norm", "gmlp_w_s", "gmlp_b_s"):
        G[k] = [None] * n_gmlp
    slabs = {k: None for k in ("ffn1_w_in", "ffn1_w_out", "ffn2_w_in", "ffn2_w_out", "w_out", "mem_w_kv")}
    dkv_all = [None] * depth
    for i in reversed(range(depth)):
        kind, j = i % 2, i // 2
        t = f"l{i}"
        s1, x1, h, s_tok, s_mem, mix, s3 = saved[i]
        dx, G["norm_ffn2"][i], slabs["ffn2_w_in"], slabs["ffn2_w_out"] = _ffn_bwd(
            dx, s3, W["norm_ffn2"][i], W["ffn2_w_in"], W["ffn2_w_out"][i], i, depth,
            slabs["ffn2_w_in"], slabs["ffn2_w_out"], f"{t}_ffn2")
        dmix = matmul(dx, W["w_out"][i], tb=True, tm=1024, tn=1024, tk=D, name=f"{t}_dmix")
        slabs["w_out"] = grad_rows(mix, dx, slabs["w_out"], i, depth, name=f"{t}_dwmixout")
        do_tok, do_mem = _to_heads(dmix[:, :tok]), _to_heads(dmix[:, tok:])
        dmq, G["mem_q_norm"][i], G["mem_k_norm"][i], slabs["mem_w_kv"], dkv_all[i] = _mem_attn_bwd(
            do_mem, s_mem, mem_n, W["mem_q_norm"][i], W["mem_k_norm"][i], i, depth, slabs["mem_w_kv"], f"{t}_mem")
        if kind == 0:
            dqkv, dz, G["fox_b_f"][j], G["fox_q_norm"][j], G["fox_k_norm"][j] = _fox_bwd(
                do_tok, s_tok, W["fox_q_norm"][j], W["fox_k_norm"][j], f"{t}_fox")
            dproj = jnp.concatenate([dqkv, dmq, dz], axis=1).astype(BF16)
            w_mix, wkey = W["fox_w_in"][j], "fox_w_in"
        else:
            up, vp = s_tok
            vg, ws, bs = _gmlp_args(W["gmlp_v_norm"][j], W["gmlp_w_s"][j], W["gmlp_b_s"][j])
            dup, dvp, dws, dbs, dvg = gmlp_bwd(up, vp, vg, ws, jnp.swapaxes(ws, 1, 2), bs, do_tok, name=f"{t}_dgmlp")
            G["gmlp_w_s"][j], G["gmlp_b_s"][j], G["gmlp_v_norm"][j] = dws, dbs[:, :, 0], dvg.reshape(-1)
            dproj = jnp.concatenate([_from_heads(dup), _from_heads(dvp), dmq], axis=1).astype(BF16)
            w_mix, wkey = W["gmlp_w_in"][j], "gmlp_w_in"
        G[wkey][j] = matmul(h, dproj, ta=True, tm=1024, tn=896, tk=1024, name=f"{t}_dwmixin")
        dh = matmul(dproj, w_mix, tb=True, tm=1024, tn=1024, tk=896, name=f"{t}_dhmix")
        dx, dgm = rms_bwd(x1, dh, W["norm_mix"][i], dx, name=f"{t}_dmixrms")
        G["norm_mix"][i] = dgm[0]
        dx, G["norm_ffn1"][i], slabs["ffn1_w_in"], slabs["ffn1_w_out"] = _ffn_bwd(
            dx, s1, W["norm_ffn1"][i], W["ffn1_w_in"], W["ffn1_w_out"][i], i, depth,
            slabs["ffn1_w_in"], slabs["ffn1_w_out"], f"{t}_ffn1")
    w_kv_all = jnp.concatenate([W["mem_w_kv"][i] for i in range(depth)], axis=1)
    dmem_n = matmul(jnp.concatenate(dkv_all, axis=1), w_kv_all, tb=True, tm=256, tn=512, tk=1024, name="dmem_n")
    _, dmemg = rms_bwd(mem, dmem_n, W["mem_norm"], None, name="dmem_rms")
    grads = {k: jnp.stack(v) for k, v in G.items()}
    grads["mem_norm"] = dmemg[0]
    grads.update(slabs)
    return loss, dx, grads


def _fox_cols_to_compute(w, tok):
    H = tok // HEAD_DIM
    qkv, f, mq = w[..., :3 * tok], w[..., 3 * tok:3 * tok + H], w[..., 3 * tok + H:]
    f = jnp.pad(f, [(0, 0)] * (w.ndim - 1) + [(0, LANES - H)])
    return jnp.concatenate([qkv, mq, f], axis=-1)


def _fox_cols_from_compute(w, tok):
    H = tok // HEAD_DIM
    qkv, mq, f = w[..., :3 * tok], w[..., 3 * tok:3 * tok + MEM_WIDTH], w[..., 3 * tok + MEM_WIDTH:3 * tok + MEM_WIDTH + H]
    return jnp.concatenate([qkv, f, mq], axis=-1)


_BIG = ("ffn1_w_in", "ffn1_w_out", "ffn2_w_in", "ffn2_w_out", "w_out", "mem_w_kv", "fox_w_in", "gmlp_w_in")
_SMALL = ("norm_ffn1", "norm_mix", "norm_ffn2", "mem_norm", "mem_q_norm", "mem_k_norm", "fox_b_f", "fox_q_norm",
          "fox_k_norm", "gmlp_v_norm", "gmlp_w_s", "gmlp_b_s")
WEIGHT_ORDER = ("norm_ffn1", "ffn1_w_in", "ffn1_w_out", "norm_mix", "norm_ffn2", "ffn2_w_in", "ffn2_w_out", "w_out",
                "mem_norm", "mem_w_kv", "mem_q_norm", "mem_k_norm", "fox_w_in", "fox_b_f", "fox_q_norm", "fox_k_norm",
                "gmlp_w_in", "gmlp_v_norm", "gmlp_w_s", "gmlp_b_s")


def _halves(a):
    if a.shape[0] == 2:
        return a
    return a.reshape(2, a.shape[1] // 2, a.shape[2])


def _small_slab(rows_list, index):
    sizes = [s.shape[0] for s in rows_list]
    n_rows = [-(-n // LANES) for n in sizes]
    small = jnp.concatenate([jnp.pad(s, (0, r * LANES - n)).reshape(r, LANES)
                             for s, n, r in zip(rows_list, sizes, n_rows)], axis=0)
    small = jnp.pad(small, ((0, -small.shape[0] % 64), (0, 0)))
    return _slot_in_empty(small, index, 8), sizes, n_rows


def _gather_weights(shards, chip, device):
    slabs = [_slot_in_empty(_halves(shards[k].astype(BF16)), chip, 4) for k in _BIG]
    vn = shards["gmlp_v_norm"]
    small, _, _ = _small_slab([vn.reshape(-1)], device)
    slabs, small_all = gather_weights(slabs, small, name="gather_weights")
    per_chip = small_all[0::2].reshape(4, -1)[:, :vn.size].reshape((4,) + vn.shape)
    return dict(zip(_BIG, slabs)), _chips_to_cols(per_chip)


def kernel(x, mem, norm_ffn1, ffn1_w_in, ffn1_w_out, norm_mix, norm_ffn2, ffn2_w_in, ffn2_w_out, w_out, mem_norm, mem_w_kv, mem_q_norm, mem_k_norm, fox_w_in, fox_b_f, fox_q_norm, fox_k_norm, gmlp_w_in, gmlp_v_norm, gmlp_w_s, gmlp_b_s, loss_target, m_norm_ffn1, m_ffn1_w_in, m_ffn1_w_out, m_norm_mix, m_norm_ffn2, m_ffn2_w_in, m_ffn2_w_out, m_w_out, m_mem_norm, m_mem_w_kv, m_mem_q_norm, m_mem_k_norm, m_fox_w_in, m_fox_b_f, m_fox_q_norm, m_fox_k_norm, m_gmlp_w_in, m_gmlp_v_norm, m_gmlp_w_s, m_gmlp_b_s, v_norm_ffn1, v_ffn1_w_in, v_ffn1_w_out, v_norm_mix, v_norm_ffn2, v_ffn2_w_in, v_ffn2_w_out, v_w_out, v_mem_norm, v_mem_w_kv, v_mem_q_norm, v_mem_k_norm, v_fox_w_in, v_fox_b_f, v_fox_q_norm, v_fox_k_norm, v_gmlp_w_in, v_gmlp_v_norm, v_gmlp_w_s, v_gmlp_b_s):
    w = dict(norm_ffn1=norm_ffn1, ffn1_w_in=ffn1_w_in, ffn1_w_out=ffn1_w_out, norm_mix=norm_mix, norm_ffn2=norm_ffn2,
             ffn2_w_in=ffn2_w_in, ffn2_w_out=ffn2_w_out, w_out=w_out, mem_norm=mem_norm, mem_w_kv=mem_w_kv,
             mem_q_norm=mem_q_norm, mem_k_norm=mem_k_norm, fox_w_in=fox_w_in, fox_b_f=fox_b_f, fox_q_norm=fox_q_norm,
             fox_k_norm=fox_k_norm, gmlp_w_in=gmlp_w_in, gmlp_v_norm=gmlp_v_norm, gmlp_w_s=gmlp_w_s, gmlp_b_s=gmlp_b_s)
    m = dict(norm_ffn1=m_norm_ffn1, ffn1_w_in=m_ffn1_w_in, ffn1_w_out=m_ffn1_w_out, norm_mix=m_norm_mix,
             norm_ffn2=m_norm_ffn2, ffn2_w_in=m_ffn2_w_in, ffn2_w_out=m_ffn2_w_out, w_out=m_w_out, mem_norm=m_mem_norm,
             mem_w_kv=m_mem_w_kv, mem_q_norm=m_mem_q_norm, mem_k_norm=m_mem_k_norm, fox_w_in=m_fox_w_in,
             fox_b_f=m_fox_b_f, fox_q_norm=m_fox_q_norm, fox_k_norm=m_fox_k_norm, gmlp_w_in=m_gmlp_w_in,
             gmlp_v_norm=m_gmlp_v_norm, gmlp_w_s=m_gmlp_w_s, gmlp_b_s=m_gmlp_b_s)
    v = dict(norm_ffn1=v_norm_ffn1, ffn1_w_in=v_ffn1_w_in, ffn1_w_out=v_ffn1_w_out, norm_mix=v_norm_mix,
             norm_ffn2=v_norm_ffn2, ffn2_w_in=v_ffn2_w_in, ffn2_w_out=v_ffn2_w_out, w_out=v_w_out, mem_norm=v_mem_norm,
             mem_w_kv=v_mem_w_kv, mem_q_norm=v_mem_q_norm, mem_k_norm=v_mem_k_norm, fox_w_in=v_fox_w_in,
             fox_b_f=v_fox_b_f, fox_q_norm=v_fox_q_norm, fox_k_norm=v_fox_k_norm, gmlp_w_in=v_gmlp_w_in,
             gmlp_v_norm=v_gmlp_v_norm, gmlp_w_s=v_gmlp_w_s, gmlp_b_s=v_gmlp_b_s)
    D = x.shape[-1]
    tok = D - MEM_WIDTH
    xi, yi, ci = _position()
    chip = 2 * xi + yi

    device = 4 * xi + 2 * yi + ci

    slabs, v_norm_full = _gather_weights(w, chip, device)
    depth = norm_ffn1.shape[0]
    W = {k: w[k] for k in _SMALL}
    W["gmlp_v_norm"] = v_norm_full
    W["ffn1_w_in"], W["ffn2_w_in"] = slabs["ffn1_w_in"], slabs["ffn2_w_in"]
    for k in ("ffn1_w_out", "ffn2_w_out", "w_out", "mem_w_kv"):
        s = slabs[k]
        W[k] = [s[:, i].reshape(4 * s.shape[2], s.shape[3]) for i in range(depth)]
    for k in ("fox_w_in", "gmlp_w_in"):
        s = slabs[k]
        full = _chips_to_cols(s.reshape((4, w[k].shape[0], 2 * s.shape[2], s.shape[3])))
        W[k] = [_fox_cols_to_compute(full[j], tok) if k == "fox_w_in" else full[j] for j in range(full.shape[0])]

    loss, grad_x, g = local_step(x[0], mem[0], loss_target[0], W)

    parts = []
    for k in _BIG:
        if k == "fox_w_in":
            p = _cols_to_chips(_fox_cols_from_compute(g[k], tok))
        elif k == "gmlp_w_in":
            p = _cols_to_chips(g[k])
        else:
            p = g[k]
        parts.append(p if p.shape[1] == 2 else p.reshape(4, 2, p.shape[2] // 2, p.shape[3]))
    small_list = [g[k].reshape(-1) for k in _SMALL] + [loss.reshape(-1)]
    small, small_sizes, small_rows = _small_slab(small_list, device)

    half = ci.reshape(1).astype(jnp.int32)
    landed, small_all = exchange_with_sibling(parts, small, name="grad_pair_exchange")
    pair = [pair_sum(p, l, half, name=f"grad_pair_sum_{k}") for k, p, l in zip(_BIG, parts, landed)]
    from_chips = scatter_to_chips(pair, name="grad_scatter")
    chip_id = chip.reshape(1).astype(jnp.int32)
    mine = [chip_sum(q, l, chip_id, name=f"grad_chip_sum_{k}") for k, q, l in zip(_BIG, pair, from_chips)]
    other = share_with_sibling(mine, name="grad_share")
    small_sum = ordered_sum(small_all, name="small_sum")

    red = {}
    for k, a, b in zip(_BIG, mine, other):
        red[k] = jnp.where(ci == 0, jnp.stack([a, b]), jnp.stack([b, a])).reshape(w[k].shape)
    off = 0
    for k, n, r in zip(_SMALL, small_sizes, small_rows):
        red[k] = small_sum[off:off + r].reshape(-1)[:n].reshape(g[k].shape)
        off += r
    loss_total = small_sum[off, 0]
    vn_cols = w["gmlp_v_norm"].shape[-1]
    red["gmlp_v_norm"] = lax.dynamic_slice_in_dim(red["gmlp_v_norm"], chip * vn_cols, vn_cols, axis=-1)

    deltas, new_m, new_v = {}, {}, {}
    for k in WEIGHT_ORDER:
        wk = w[k] if w[k].ndim > 1 else w[k].reshape(1, -1)
        upd = adamw(wk, red[k].reshape(wk.shape), m[k].reshape(wk.shape), v[k].reshape(wk.shape), name=f"adamw_{k}")
        deltas[k], new_m[k], new_v[k] = (u.reshape(w[k].shape) for u in upd)
    return (loss_total, grad_x[None], *[red[k].reshape(w[k].shape) for k in WEIGHT_ORDER],
            *[deltas[k] for k in WEIGHT_ORDER], *[new_m[k] for k in WEIGHT_ORDER], *[new_v[k] for k in WEIGHT_ORDER])
```

```python
import functools
import math

import jax
import jax.numpy as jnp
from jax import lax
from jax.experimental import pallas as pl
from jax.experimental.pallas import tpu as pltpu

F32 = jnp.float32
BF16 = jnp.bfloat16
EPS = 1e-6
HEAD_DIM = 64
MEM_WIDTH = 256
CHUNK = 128
LANES = 128
NEG = -1e30
VMEM_LIMIT_BYTES = 56 * 1024 * 1024
ATTN_Q_BLOCK = 1024
ATTN_K_BLOCK = 1024
QK_SCALE = 0.125
MESH_ID = pl.DeviceIdType.MESH

ADAM_LR = 0.001
ADAM_B1 = 0.9
ADAM_B2 = 0.999
ADAM_EPS = 1e-08
ADAM_WD = 0.01
ADAM_STEP = 10


def _tile(n, pref, align):
    t = (min(pref, n) // align) * align
    while t >= align:
        if n % t == 0:
            return t
        t -= align
    return n


def _params(sem):
    return pltpu.CompilerParams(dimension_semantics=sem, vmem_limit_bytes=VMEM_LIMIT_BYTES)


def _dot(a, b, ca, cb):
    return lax.dot_general(a, b, (((ca,), (cb,)), ((), ())), preferred_element_type=F32)


def _sigmoid(x):
    return 1.0 / (1.0 + jnp.exp(-x))


_GELU_C = math.sqrt(2.0 / math.pi)


def _gelu(x):
    return 0.5 * x * (1.0 + jnp.tanh(_GELU_C * (x + 0.044715 * (x * x * x))))


def _gelu_grad(x):
    t = jnp.tanh(_GELU_C * (x + 0.044715 * (x * x * x)))
    return 0.5 * (1.0 + t) + 0.5 * x * (1.0 - t * t) * (_GELU_C * (1.0 + 3.0 * 0.044715 * (x * x)))


def matmul(a, b, *, ta=False, tb=False, out_dtype=F32, scale=None, res=None,
           tm=1024, tn=512, tk=1024, name):
    if ta:
        K, M = a.shape
    else:
        M, K = a.shape
    N = b.shape[0] if tb else b.shape[1]
    tm = _tile(M, tm, LANES if ta else 16)
    tn = _tile(N, tn, LANES)
    tk = _tile(K, tk, LANES)
    nk = K // tk
    a_spec = pl.BlockSpec((tk, tm), lambda i, j, k: (k, i)) if ta else pl.BlockSpec((tm, tk), lambda i, j, k: (i, k))
    b_spec = pl.BlockSpec((tn, tk), lambda i, j, k: (j, k)) if tb else pl.BlockSpec((tk, tn), lambda i, j, k: (k, j))
    o_spec = pl.BlockSpec((tm, tn), lambda i, j, k: (i, j))
    ca, cb = (0 if ta else 1), (1 if tb else 0)
    has_res = res is not None

    def body(*refs):
        a_ref, b_ref = refs[0], refs[1]
        res_ref = refs[2] if has_res else None
        o_ref = refs[3] if has_res else refs[2]
        acc_ref = refs[-1]
        k = pl.program_id(2)
        prod = _dot(a_ref[...].astype(BF16), b_ref[...].astype(BF16), ca, cb)

        def finish(acc):
            if scale is not None:
                acc = acc * scale
            if has_res:
                acc = res_ref[...] + acc
            o_ref[...] = acc.astype(out_dtype)

        if nk == 1:
            finish(prod)
        else:
            @pl.when(k == 0)
            def _():
                acc_ref[...] = prod

            @pl.when(k > 0)
            def _():
                acc_ref[...] += prod

            @pl.when(k == nk - 1)
            def _():
                finish(acc_ref[...])

    in_specs = [a_spec, b_spec] + ([o_spec] if has_res else [])
    args = (a, b) + ((res,) if has_res else ())
    return pl.pallas_call(
        body, grid=(M // tm, N // tn, nk), in_specs=in_specs, out_specs=o_spec,
        out_shape=jax.ShapeDtypeStruct((M, N), out_dtype),
        scratch_shapes=[pltpu.VMEM((tm, tn) if nk > 1 else (8, LANES), F32)],
        compiler_params=_params(("parallel", "parallel", "arbitrary")), name=name)(*args)


def swiglu_fwd(h, w_slab, layer, *, name):
    S, D = h.shape
    Fc = w_slab.shape[-1]
    tm = _tile(S, 512, 16)

    def body(h_ref, wa_ref, wb_ref, a_ref, b_ref, act_ref):
        hv = h_ref[...]
        a = _dot(hv, wa_ref[...], 1, 0)
        b = _dot(hv, wb_ref[...], 1, 0)
        a_ref[...] = a
        b_ref[...] = b
        act_ref[...] = (a * _sigmoid(a) * b).astype(BF16)

    out = pl.BlockSpec((tm, Fc), lambda j, i: (i, j))
    return pl.pallas_call(
        body, grid=(2, S // tm),
        in_specs=[pl.BlockSpec((tm, D), lambda j, i: (i, 0)),
                  pl.BlockSpec((None, None, D, Fc), lambda j, i: (j, layer, 0, 0)),
                  pl.BlockSpec((None, None, D, Fc), lambda j, i: (j + 2, layer, 0, 0))],
        out_specs=[out, out, out],
        out_shape=[jax.ShapeDtypeStruct((S, 2 * Fc), F32), jax.ShapeDtypeStruct((S, 2 * Fc), F32),
                   jax.ShapeDtypeStruct((S, 2 * Fc), BF16)],
        compiler_params=_params(("parallel", "parallel")), name=name)(h, w_slab, w_slab)


def swiglu_bwd(dy, w_out, a, b, *, name):
    S, D = dy.shape
    F = w_out.shape[0]
    fc = F // 2
    tm = _tile(S, 512, 16)

    def body(dy_ref, w_ref, a_ref, b_ref, da_ref, db_ref):
        dact = 0.5 * _dot(dy_ref[...].astype(BF16), w_ref[...], 1, 1)
        av = a_ref[...]
        sg = _sigmoid(av)
        da_ref[...] = (dact * b_ref[...] * (sg * (1.0 + av * (1.0 - sg)))).astype(BF16)
        db_ref[...] = (dact * (av * sg)).astype(BF16)

    blk = pl.BlockSpec((tm, fc), lambda j, i: (i, j))
    return pl.pallas_call(
        body, grid=(2, S // tm),
        in_specs=[pl.BlockSpec((tm, D), lambda j, i: (i, 0)), pl.BlockSpec((fc, D), lambda j, i: (j, 0)), blk, blk],
        out_specs=[blk, blk],
        out_shape=[jax.ShapeDtypeStruct((S, F), BF16), jax.ShapeDtypeStruct((S, F), BF16)],
        compiler_params=_params(("parallel", "parallel")), name=name)(dy, w_out, a, b)


def ffn_dh(da, db, w_slab, layer, *, name):
    S, F = da.shape
    D, Fc = w_slab.shape[-2:]
    tm = _tile(S, 1024, 16)

    def body(da_ref, db_ref, w_ref, o_ref, acc_ref):
        k = pl.program_id(1)

        @pl.when(k == 0)
        def _():
            acc_ref[...] = jnp.zeros_like(acc_ref)

        @pl.when(k < 2)
        def _():
            acc_ref[...] += _dot(da_ref[...], w_ref[...], 1, 1)

        @pl.when(k >= 2)
        def _():
            acc_ref[...] += _dot(db_ref[...], w_ref[...], 1, 1)

        @pl.when(k == 3)
        def _():
            o_ref[...] = acc_ref[...]

    return pl.pallas_call(
        body, grid=(S // tm, 4),
        in_specs=[pl.BlockSpec((tm, Fc), lambda i, k: (i, jnp.minimum(k, 1))),
                  pl.BlockSpec((tm, Fc), lambda i, k: (i, jnp.maximum(k - 2, 0))),
                  pl.BlockSpec((None, None, D, Fc), lambda i, k: (k, layer, 0, 0))],
        out_specs=pl.BlockSpec((tm, D), lambda i, k: (i, 0)),
        out_shape=jax.ShapeDtypeStruct((S, D), F32),
        scratch_shapes=[pltpu.VMEM((tm, D), F32)],
        compiler_params=_params(("parallel", "arbitrary")), name=name)(da, db, w_slab)


def _slab_alias(slab, n_in):
    if slab is None:
        return (), [], {}
    return (slab,), [_ANY], {n_in: 0}


def grad_cols(h, da, db, slab, layer, n_layers, *, name):
    S, D = h.shape
    Fc = da.shape[1] // 2
    tk = _tile(S, 512, 16)
    nk = S // tk
    extra, extra_specs, alias = _slab_alias(slab, 3)

    def body(*refs):
        h_ref, da_ref, db_ref = refs[:3]
        o_ref, acc_ref = refs[-2:]
        ch, k = pl.program_id(0), pl.program_id(1)

        @pl.when(k == 0)
        def _():
            acc_ref[...] = jnp.zeros_like(acc_ref)

        @pl.when(ch < 2)
        def _():
            acc_ref[...] += _dot(h_ref[...], da_ref[...], 0, 0)

        @pl.when(ch >= 2)
        def _():
            acc_ref[...] += _dot(h_ref[...], db_ref[...], 0, 0)

        @pl.when(k == nk - 1)
        def _():
            o_ref[...] = acc_ref[...]

    return pl.pallas_call(
        body, grid=(4, nk),
        in_specs=[pl.BlockSpec((tk, D), lambda ch, k: (k, 0)),
                  pl.BlockSpec((tk, Fc), lambda ch, k: (jnp.where(ch < 2, k, 0), jnp.minimum(ch, 1))),
                  pl.BlockSpec((tk, Fc), lambda ch, k: (jnp.where(ch >= 2, k, 0), jnp.maximum(ch - 2, 0)))] + extra_specs,
        out_specs=pl.BlockSpec((None, None, D, Fc), lambda ch, k: (ch, layer, 0, 0)),
        out_shape=jax.ShapeDtypeStruct((4, n_layers, D, Fc), F32),
        scratch_shapes=[pltpu.VMEM((D, Fc), F32)], input_output_aliases=alias,
        compiler_params=_params(("parallel", "arbitrary")), name=name)(h, da, db, *extra)


def grad_rows(a, b, slab, layer, n_layers, *, scale=None, name):
    S, M = a.shape
    N = b.shape[1]
    R = M // 4
    tn = _tile(N, 512, LANES)
    tk = _tile(S, 512, 16)
    nk = S // tk
    extra, extra_specs, alias = _slab_alias(slab, 2)

    def body(*refs):
        a_ref, b_ref = refs[:2]
        o_ref, acc_ref = refs[-2:]
        k = pl.program_id(1)

        @pl.when(k == 0)
        def _():
            acc_ref[...] = jnp.zeros_like(acc_ref)

        acc_ref[...] += _dot(a_ref[...].astype(BF16), b_ref[...].astype(BF16), 0, 0)

        @pl.when(k == nk - 1)
        def _():
            for d in range(4):
                part = acc_ref[d * R:(d + 1) * R, :]
                o_ref[d] = part if scale is None else part * scale

    return pl.pallas_call(
        body, grid=(N // tn, nk),
        in_specs=[pl.BlockSpec((tk, M), lambda j, k: (k, 0)), pl.BlockSpec((tk, tn), lambda j, k: (k, j))] + extra_specs,
        out_specs=pl.BlockSpec((4, None, R, tn), lambda j, k: (0, layer, 0, j)),
        out_shape=jax.ShapeDtypeStruct((4, n_layers, R, N), F32),
        scratch_shapes=[pltpu.VMEM((M, tn), F32)], input_output_aliases=alias,
        compiler_params=_params(("parallel", "arbitrary")), name=name)(a, b, *extra)


def rms_fwd(x, g, *, name):
    S, D = x.shape
    ts = _tile(S, 1024, 16)

    def body(x_ref, g_ref, h_ref):
        xv = x_ref[...]
        r = lax.rsqrt(jnp.mean(xv * xv, axis=-1, keepdims=True) + EPS)
        h_ref[...] = (xv * r * g_ref[...]).astype(BF16)

    return pl.pallas_call(
        body, grid=(S // ts,),
        in_specs=[pl.BlockSpec((ts, D), lambda i: (i, 0)), pl.BlockSpec((1, D), lambda i: (0, 0))],
        out_specs=pl.BlockSpec((ts, D), lambda i: (i, 0)),
        out_shape=jax.ShapeDtypeStruct((S, D), BF16),
        compiler_params=_params(("parallel",)), name=name)(x, g.reshape(1, D))


def rms_bwd(x, dh, g, res, *, name):
    S, D = x.shape
    ts = _tile(S, 512, 16)
    has_res = res is not None

    def body(*refs):
        x_ref, dh_ref, g_ref = refs[:3]
        res_ref = refs[3] if has_res else None
        dx_ref, dg_ref = refs[-2:]
        i = pl.program_id(0)
        xv, dhv = x_ref[...], dh_ref[...].astype(F32)
        r = lax.rsqrt(jnp.mean(xv * xv, axis=-1, keepdims=True) + EPS)
        u = dhv * g_ref[...]
        dx = r * u - xv * (r * r * r) * jnp.mean(xv * u, axis=-1, keepdims=True)
        if has_res:
            dx = res_ref[...] + dx
        dx_ref[...] = dx
        part = jnp.sum(dhv * xv * r, axis=0, keepdims=True)

        @pl.when(i == 0)
        def _():
            dg_ref[...] = part

        @pl.when(i > 0)
        def _():
            dg_ref[...] += part

    row = pl.BlockSpec((ts, D), lambda i: (i, 0))
    vec = pl.BlockSpec((1, D), lambda i: (0, 0))
    args = (x, dh, g.reshape(1, D)) + ((res,) if has_res else ())
    return pl.pallas_call(
        body, grid=(S // ts,), in_specs=[row, row, vec] + ([row] if has_res else []),
        out_specs=[row, vec],
        out_shape=[jax.ShapeDtypeStruct((S, D), F32), jax.ShapeDtypeStruct((1, D), F32)],
        compiler_params=_params(("arbitrary",)), name=name)(*args)


def _low_half(shape):
    return lax.broadcasted_iota(jnp.int32, shape, len(shape) - 1) < HEAD_DIM


def _half_sums(x, low):
    sa = jnp.sum(jnp.where(low, x, 0.0), axis=1, keepdims=True)
    sb = jnp.sum(jnp.where(low, 0.0, x), axis=1, keepdims=True)
    return jnp.where(low, sa, sb)


def pairnorm_fwd(x, col0, n_pairs, g, *, scale=None, name):
    S = x.shape[0]
    ts = _tile(S, 1024, 16)

    def body(x_ref, g_ref, o_ref):
        xv = x_ref[...]
        r = lax.rsqrt(_half_sums(xv * xv, _low_half(xv.shape)) * (1.0 / HEAD_DIM) + EPS)
        y = xv * r * g_ref[...]
        o_ref[...] = (y if scale is None else y * scale).astype(BF16)

    return pl.pallas_call(
        body, grid=(S // ts, n_pairs),
        in_specs=[pl.BlockSpec((ts, LANES), lambda i, j: (i, col0 + j)), pl.BlockSpec((1, LANES), lambda i, j: (0, 0))],
        out_specs=pl.BlockSpec((ts, LANES), lambda i, j: (i, j)),
        out_shape=jax.ShapeDtypeStruct((S, n_pairs * LANES), BF16),
        compiler_params=_params(("parallel", "parallel")), name=name)(x, jnp.tile(g.reshape(1, HEAD_DIM), (1, 2)))


def pairnorm_bwd(x, col0, n_pairs, dy, g, *, name):
    S = x.shape[0]
    ts = _tile(S, 1024, 16)

    def body(x_ref, dy_ref, g_ref, dx_ref, dg_ref):
        first = jnp.logical_and(pl.program_id(0) == 0, pl.program_id(1) == 0)
        xv, dyv = x_ref[...], dy_ref[...]
        low = _low_half(xv.shape)
        r = lax.rsqrt(_half_sums(xv * xv, low) * (1.0 / HEAD_DIM) + EPS)
        u = dyv * g_ref[...]
        dx_ref[...] = r * u - xv * (r * r * r) * (_half_sums(xv * u, low) * (1.0 / HEAD_DIM))
        part = jnp.sum(dyv * xv * r, axis=0, keepdims=True)

        @pl.when(first)
        def _():
            dg_ref[...] = part

        @pl.when(jnp.logical_not(first))
        def _():
            dg_ref[...] += part

    vec = pl.BlockSpec((1, LANES), lambda i, j: (0, 0))
    blk = pl.BlockSpec((ts, LANES), lambda i, j: (i, j))
    return pl.pallas_call(
        body, grid=(S // ts, n_pairs),
        in_specs=[pl.BlockSpec((ts, LANES), lambda i, j: (i, col0 + j)), blk, vec], out_specs=[blk, vec],
        out_shape=[jax.ShapeDtypeStruct((S, n_pairs * LANES), F32), jax.ShapeDtypeStruct((1, LANES), F32)],
        compiler_params=_params(("arbitrary", "arbitrary")), name=name)(x, dy, jnp.tile(g.reshape(1, HEAD_DIM), (1, 2)))


def _split3(x):
    x1 = x.astype(BF16)
    r1 = x - x1.astype(F32)
    x2 = r1.astype(BF16)
    x3 = (r1 - x2.astype(F32)).astype(BF16)
    return x1, x2, x3


def _tri_ones(n, lower):
    r = lax.broadcasted_iota(jnp.int32, (n, n), 0)
    c = lax.broadcasted_iota(jnp.int32, (n, n), 1)
    return jnp.where((c <= r) if lower else (c >= r), 1.0, 0.0).astype(BF16)


def fgate_fwd(z, col0, bias, *, name):
    S, L = z.shape[0], LANES
    tb = _tile(S, 256, 16)

    def body(z_ref, b_ref, c_ref, carry):
        i = pl.program_id(0)

        @pl.when(i == 0)
        def _():
            carry[...] = jnp.zeros_like(carry)

        zz = z_ref[...] + b_ref[...]
        lf = jnp.minimum(zz, 0.0) - jnp.log(1.0 + jnp.exp(-jnp.abs(zz)))
        tri = _tri_ones(tb, True)
        x1, x2, x3 = _split3(lf)
        c = (_dot(tri, x1, 1, 0) + _dot(tri, x2, 1, 0)) + _dot(tri, x3, 1, 0) + carry[...]
        c_ref[...] = c
        carry[...] += jnp.sum(lf, axis=0, keepdims=True)

    return pl.pallas_call(
        body, grid=(S // tb,),
        in_specs=[pl.BlockSpec((tb, L), lambda i: (i, col0)), pl.BlockSpec((1, L), lambda i: (0, 0))],
        out_specs=pl.BlockSpec((tb, L), lambda i: (i, 0)),
        out_shape=jax.ShapeDtypeStruct((S, L), F32),
        scratch_shapes=[pltpu.VMEM((1, L), F32)],
        compiler_params=_params(("arbitrary",)), name=name)(z, bias)


def fgate_bwd(z, col0, bias, drs, dcs, *, name):
    S, L = z.shape[0], LANES
    tb = _tile(S, 256, 16)
    nb = S // tb

    def body(z_ref, b_ref, drs_ref, dcs_ref, dz_ref, db_ref, carry):
        i = pl.program_id(0)

        @pl.when(i == 0)
        def _():
            carry[...] = jnp.zeros_like(carry)

        tri = _tri_ones(tb, False)
        dc = drs_ref[...] - dcs_ref[...]
        x1, x2, x3 = _split3(dc)
        dlf = (_dot(tri, x1, 1, 0) + _dot(tri, x2, 1, 0)) + _dot(tri, x3, 1, 0) + carry[...]
        carry[...] += jnp.sum(dc, axis=0, keepdims=True)
        dz = dlf * _sigmoid(-(z_ref[...] + b_ref[...]))
        dz_ref[...] = dz
        part = jnp.sum(dz, axis=0, keepdims=True)

        @pl.when(i == 0)
        def _():
            db_ref[...] = part

        @pl.when(i > 0)
        def _():
            db_ref[...] += part

    rev = pl.BlockSpec((tb, L), lambda i: (nb - 1 - i, 0))
    vec = pl.BlockSpec((1, L), lambda i: (0, 0))
    return pl.pallas_call(
        body, grid=(nb,), in_specs=[pl.BlockSpec((tb, L), lambda i: (nb - 1 - i, col0)), vec, rev, rev],
        out_specs=[rev, vec],
        out_shape=[jax.ShapeDtypeStruct((S, L), F32), jax.ShapeDtypeStruct((1, L), F32)],
        scratch_shapes=[pltpu.VMEM((1, L), F32)],
        compiler_params=_params(("arbitrary",)), name=name)(z, bias, drs, dcs)


def _one_head(x, low, a):
    return jnp.where(low if a == 0 else jnp.logical_not(low), x, jnp.zeros_like(x))


def attn_fwd(q, q0, k, k0, v, v0, n_pairs, cq, ck, *, causal, name):
    Sq, Sk = q.shape[0], k.shape[0]
    tq = _tile(Sq, ATTN_Q_BLOCK, LANES)
    tk = _tile(Sk, ATTN_K_BLOCK, LANES)
    nq, nk = Sq // tq, Sk // tk
    bias = cq is not None

    def body(*refs):
        q_ref, k_ref, v_ref = refs[:3]
        cq_ref, ck_ref = (refs[3], refs[4]) if bias else (None, None)
        o_ref, lse_ref, m_sc, l_sc, acc_sc = refs[-5:]
        i, j = pl.program_id(1), pl.program_id(2)

        @pl.when(j == 0)
        def _():
            m_sc[...] = jnp.full_like(m_sc, NEG)
            l_sc[...] = jnp.zeros_like(l_sc)
            acc_sc[...] = jnp.zeros_like(acc_sc)

        def compute(masked):
            qv, kv, vv = q_ref[...], k_ref[...], v_ref[...].astype(BF16)
            low = _low_half(qv.shape)
            for a in range(2):
                s = _dot(_one_head(qv, low, a), kv, 1, 1)
                if bias:
                    s = s + (cq_ref[a] - ck_ref[a])
                if masked:
                    row = i * tq + lax.broadcasted_iota(jnp.int32, (tq, tk), 0)
                    col = j * tk + lax.broadcasted_iota(jnp.int32, (tq, tk), 1)
                    s = jnp.where(col <= row, s, NEG)
                m_prev = m_sc[a]
                m_new = jnp.maximum(m_prev, jnp.max(s, axis=1, keepdims=True))
                alpha = jnp.exp(m_prev - m_new)
                p = jnp.exp(s - m_new)
                l_sc[a] = alpha * l_sc[a] + jnp.sum(p, axis=1, keepdims=True)
                acc_sc[a] = alpha * acc_sc[a] + _dot(p.astype(BF16), vv, 1, 0)
                m_sc[a] = m_new

        if causal:
            live = j * tk <= i * tq + (tq - 1)
            crosses = j * tk + (tk - 1) > i * tq
            pl.when(jnp.logical_and(live, crosses))(functools.partial(compute, True))
            pl.when(jnp.logical_and(live, jnp.logical_not(crosses)))(functools.partial(compute, False))
        else:
            compute(False)

        @pl.when(j == nk - 1)
        def _():
            low = _low_half((tq, LANES))
            o_ref[...] = jnp.where(low, acc_sc[0] / l_sc[0], acc_sc[1] / l_sc[1])
            for a in range(2):
                lse_ref[a] = m_sc[a] + jnp.log(l_sc[a])

    def kv_blk(i, j):
        return jnp.minimum(j, (i * tq + tq - 1) // tk) if causal else j

    in_specs = [pl.BlockSpec((tq, LANES), lambda p, i, j: (i, q0 + p)),
                pl.BlockSpec((tk, LANES), lambda p, i, j: (kv_blk(i, j), k0 + p)),
                pl.BlockSpec((tk, LANES), lambda p, i, j: (kv_blk(i, j), v0 + p))]
    args = [q, k, v]
    if bias:
        in_specs += [pl.BlockSpec((None, 2, tq, 1), lambda p, i, j: (p, 0, i, 0)),
                     pl.BlockSpec((None, 2, 1, tk), lambda p, i, j: (p, 0, 0, kv_blk(i, j)))]
        args += [cq, ck]
    return pl.pallas_call(
        body, grid=(n_pairs, nq, nk), in_specs=in_specs,
        out_specs=[pl.BlockSpec((tq, LANES), lambda p, i, j: (i, p)),
                   pl.BlockSpec((None, 2, tq, 1), lambda p, i, j: (p, 0, i, 0))],
        out_shape=[jax.ShapeDtypeStruct((Sq, n_pairs * LANES), F32), jax.ShapeDtypeStruct((n_pairs, 2, Sq, 1), F32)],
        scratch_shapes=[pltpu.VMEM((2, tq, 1), F32), pltpu.VMEM((2, tq, 1), F32), pltpu.VMEM((2, tq, LANES), F32)],
        compiler_params=_params(("parallel", "parallel", "arbitrary")), name=name)(*args)


def attn_delta(o, do, do0, n_pairs, *, name):
    S = o.shape[0]
    ts = _tile(S, 1024, 16)

    def body(o_ref, do_ref, out_ref):
        prod = o_ref[...] * do_ref[...]
        low = _low_half(prod.shape)
        out_ref[0] = jnp.sum(jnp.where(low, prod, 0.0), axis=1, keepdims=True)
        out_ref[1] = jnp.sum(jnp.where(low, 0.0, prod), axis=1, keepdims=True)

    return pl.pallas_call(
        body, grid=(n_pairs, S // ts),
        in_specs=[pl.BlockSpec((ts, LANES), lambda p, i: (i, p)), pl.BlockSpec((ts, LANES), lambda p, i: (i, do0 + p))],
        out_specs=pl.BlockSpec((None, 2, ts, 1), lambda p, i: (p, 0, i, 0)),
        out_shape=jax.ShapeDtypeStruct((n_pairs, 2, S, 1), F32),
        compiler_params=_params(("parallel", "parallel")), name=name)(o, do)


def attn_bwd(q, q0, k, k0, v, v0, do, do0, n_pairs, lse, delta, cq, ck, *, causal, name):
    Sq, Sk = q.shape[0], k.shape[0]
    tq = _tile(Sq, ATTN_Q_BLOCK, LANES)
    tk = _tile(Sk, ATTN_K_BLOCK, LANES)
    nq, nk = Sq // tq, Sk // tk
    bias = cq is not None

    def body(*refs):
        q_ref, k_ref, v_ref, do_ref, lse_ref, dl_ref = refs[:6]
        cq_ref, ck_ref = (refs[6], refs[7]) if bias else (None, None)
        outs = refs[8:] if bias else refs[6:]
        dq_ref, dk_ref, dv_ref = outs[:3]
        dcs_ref, drs_ref = (outs[3], outs[4]) if bias else (None, None)
        j, i = pl.program_id(1), pl.program_id(2)

        @pl.when(i == 0)
        def _():
            dk_ref[...] = jnp.zeros_like(dk_ref)
            dv_ref[...] = jnp.zeros_like(dv_ref)
            if bias:
                dcs_ref[...] = jnp.zeros_like(dcs_ref)

        rows = pl.ds(pl.multiple_of(i * tq, tq), tq)

        def compute(masked):
            qv, kv, vv, dov = q_ref[...], k_ref[...], v_ref[...].astype(BF16), do_ref[...].astype(BF16)
            low = _low_half(qv.shape)
            dq_part, row_parts = None, []
            for a in range(2):
                qa, ka, doa = _one_head(qv, low, a), _one_head(kv, _low_half(kv.shape), a), _one_head(dov, low, a)
                s = _dot(qa, kv, 1, 1)
                if bias:
                    s = s + (cq_ref[a] - ck_ref[a])
                p = jnp.exp(s - lse_ref[a])
                if masked:
                    row = i * tq + lax.broadcasted_iota(jnp.int32, (tq, tk), 0)
                    col = j * tk + lax.broadcasted_iota(jnp.int32, (tq, tk), 1)
                    p = jnp.where(col <= row, p, 0.0)
                dv_ref[...] += _dot(p.astype(BF16), doa, 0, 0)
                dp = _dot(doa, vv, 1, 1)
                ds = p * (dp - dl_ref[a])
                dsb = ds.astype(BF16)
                dk_ref[...] += _dot(dsb, qa, 0, 0)
                if bias:
                    dcs_ref[a] += jnp.sum(ds, axis=0, keepdims=True)
                    row_parts.append(jnp.sum(ds, axis=1, keepdims=True))
                part = _dot(dsb, ka, 1, 0) * QK_SCALE
                dq_part = part if dq_part is None else dq_part + part

            @pl.when(j == 0)
            def _():
                dq_ref[rows, :] = dq_part
                for a, rp in enumerate(row_parts):
                    drs_ref[a, rows, :] = rp

            @pl.when(j > 0)
            def _():
                dq_ref[rows, :] += dq_part
                for a, rp in enumerate(row_parts):
                    drs_ref[a, rows, :] += rp

        if causal:
            live = j * tk <= i * tq + (tq - 1)
            crosses = j * tk + (tk - 1) > i * tq
            pl.when(jnp.logical_and(live, crosses))(functools.partial(compute, True))
            pl.when(jnp.logical_and(live, jnp.logical_not(crosses)))(functools.partial(compute, False))
        else:
            compute(False)

    def q_blk(j, i):
        return jnp.maximum(i, (j * tk) // tq) if causal else i

    col1 = pl.BlockSpec((None, 2, tq, 1), lambda p, j, i: (p, 0, q_blk(j, i), 0))
    in_specs = [pl.BlockSpec((tq, LANES), lambda p, j, i: (q_blk(j, i), q0 + p)),
                pl.BlockSpec((tk, LANES), lambda p, j, i: (j, k0 + p)),
                pl.BlockSpec((tk, LANES), lambda p, j, i: (j, v0 + p)),
                pl.BlockSpec((tq, LANES), lambda p, j, i: (q_blk(j, i), do0 + p)), col1, col1]
    args = [q, k, v, do, lse, delta]
    kout = pl.BlockSpec((tk, LANES), lambda p, j, i: (j, p))
    out_specs = [pl.BlockSpec((Sq, LANES), lambda p, j, i: (0, p)), kout, kout]
    out_shape = [jax.ShapeDtypeStruct((Sq, n_pairs * LANES), F32), jax.ShapeDtypeStruct((Sk, n_pairs * LANES), F32),
                 jax.ShapeDtypeStruct((Sk, n_pairs * LANES), F32)]
    if bias:
        in_specs += [col1, pl.BlockSpec((None, 2, 1, tk), lambda p, j, i: (p, 0, 0, j))]
        args += [cq, ck]
        out_specs += [pl.BlockSpec((None, 2, 1, tk), lambda p, j, i: (p, 0, 0, j)),
                      pl.BlockSpec((None, 2, Sq, 1), lambda p, j, i: (p, 0, 0, 0))]
        out_shape += [jax.ShapeDtypeStruct((n_pairs, 2, 1, Sk), F32), jax.ShapeDtypeStruct((n_pairs, 2, Sq, 1), F32)]
    return pl.pallas_call(
        body, grid=(n_pairs, nk, nq), in_specs=in_specs, out_specs=out_specs, out_shape=out_shape,
        compiler_params=_params(("parallel", "arbitrary", "arbitrary")), name=name)(*args)


def _tril_mask(n):
    r = lax.broadcasted_iota(jnp.int32, (n, n), 0)
    c = lax.broadcasted_iota(jnp.int32, (n, n), 1)
    return c <= r


def _gmlp_operands(v_gain, w_s, b_s):
    G = w_s.shape[0]
    return (v_gain.reshape(G // 2, 1, LANES), w_s.reshape(G // 2, 2, CHUNK, CHUNK), b_s.reshape(G // 2, 2, CHUNK, 1))


def _gmlp_gate(wt, vh, b_ref, low):
    gate = _dot(wt[0], _one_head(vh, low, 0), 1, 0) + _dot(wt[1], _one_head(vh, low, 1), 1, 0)
    return gate + jnp.where(low, b_ref[0], b_ref[1])


def gmlp_fwd(proj, v0, n_pairs, vg, w, b, *, name):
    S = proj.shape[0]
    ts = _tile(S, 1024, CHUNK)

    def body(up_ref, vp_ref, vg_ref, w_ref, b_ref, o_ref):
        mask = _tril_mask(CHUNK)
        wt = [jnp.where(mask, w_ref[a], 0.0).astype(BF16) for a in range(2)]
        low = _low_half((CHUNK, LANES))
        for c in range(ts // CHUNK):
            sl = pl.ds(c * CHUNK, CHUNK)
            vz = _gelu(vp_ref[sl, :])
            r = lax.rsqrt(_half_sums(vz * vz, low) * (1.0 / HEAD_DIM) + EPS)
            vh = (vz * r * vg_ref[...]).astype(BF16)
            o_ref[sl, :] = _gelu(up_ref[sl, :]) * _gmlp_gate(wt, vh, b_ref, low)

    return pl.pallas_call(
        body, grid=(n_pairs, S // ts),
        in_specs=[pl.BlockSpec((ts, LANES), lambda p, i: (i, p)), pl.BlockSpec((ts, LANES), lambda p, i: (i, v0 + p)),
                  pl.BlockSpec((None, 1, LANES), lambda p, i: (p, 0, 0)),
                  pl.BlockSpec((None, 2, CHUNK, CHUNK), lambda p, i: (p, 0, 0, 0)),
                  pl.BlockSpec((None, 2, CHUNK, 1), lambda p, i: (p, 0, 0, 0))],
        out_specs=pl.BlockSpec((ts, LANES), lambda p, i: (i, p)),
        out_shape=jax.ShapeDtypeStruct((S, n_pairs * LANES), F32),
        compiler_params=_params(("parallel", "parallel")), name=name)(proj, proj, vg, w, b)


def gmlp_bwd(proj, v0, n_pairs, vg, w, wT, b, do, *, name):
    S = proj.shape[0]
    ts = _tile(S, 1024, CHUNK)

    def body(up_ref, vp_ref, vg_ref, w_ref, wT_ref, b_ref, do_ref, dup_ref, dvp_ref, dw_ref, db_ref, dvg_ref):
        i = pl.program_id(1)

        @pl.when(i == 0)
        def _():
            dw_ref[...] = jnp.zeros_like(dw_ref)
            db_ref[...] = jnp.zeros_like(db_ref)
            dvg_ref[...] = jnp.zeros_like(dvg_ref)

        mask = _tril_mask(CHUNK)
        wt = [jnp.where(mask, w_ref[a], 0.0).astype(BF16) for a in range(2)]
        wtT = [jnp.where(mask.T, wT_ref[a], 0.0).astype(BF16) for a in range(2)]
        low = _low_half((CHUNK, LANES))
        vgain = vg_ref[...]
        for c in range(ts // CHUNK):
            sl = pl.ds(c * CHUNK, CHUNK)
            u_pre, v_pre, dout = up_ref[sl, :], vp_ref[sl, :], do_ref[sl, :]
            vz = _gelu(v_pre)
            r = lax.rsqrt(_half_sums(vz * vz, low) * (1.0 / HEAD_DIM) + EPS)
            vh = (vz * r * vgain).astype(BF16)
            gate = _gmlp_gate(wt, vh, b_ref, low)
            dgate = dout * _gelu(u_pre)
            dup_ref[sl, :] = dout * gate * _gelu_grad(u_pre)
            dvh = None
            for a in range(2):
                dga = _one_head(dgate, low, a)
                dgb = dga.astype(BF16)
                dw_ref[a] += jnp.where(mask, _dot(dgb, vh, 1, 1), 0.0)
                db_ref[a] += jnp.sum(dga, axis=1, keepdims=True)
                part = _dot(wtT[a], dgb, 1, 0)
                dvh = part if dvh is None else dvh + part
            dvg_ref[...] += jnp.sum(dvh * vz * r, axis=0, keepdims=True)
            t = dvh * vgain
            dvz = r * t - vz * (r * r * r) * (_half_sums(vz * t, low) * (1.0 / HEAD_DIM))
            dvp_ref[sl, :] = dvz * _gelu_grad(v_pre)

    ublk = pl.BlockSpec((ts, LANES), lambda p, i: (i, p))
    wblk = pl.BlockSpec((None, 2, CHUNK, CHUNK), lambda p, i: (p, 0, 0, 0))
    bblk = pl.BlockSpec((None, 2, CHUNK, 1), lambda p, i: (p, 0, 0, 0))
    gblk = pl.BlockSpec((None, 1, LANES), lambda p, i: (p, 0, 0))
    return pl.pallas_call(
        body, grid=(n_pairs, S // ts),
        in_specs=[ublk, pl.BlockSpec((ts, LANES), lambda p, i: (i, v0 + p)), gblk, wblk, wblk, bblk, ublk],
        out_specs=[ublk, ublk, wblk, bblk, gblk],
        out_shape=[jax.ShapeDtypeStruct((S, n_pairs * LANES), F32), jax.ShapeDtypeStruct((S, n_pairs * LANES), F32),
                   jax.ShapeDtypeStruct((n_pairs, 2, CHUNK, CHUNK), F32), jax.ShapeDtypeStruct((n_pairs, 2, CHUNK, 1), F32),
                   jax.ShapeDtypeStruct((n_pairs, 1, LANES), F32)],
        compiler_params=_params(("parallel", "arbitrary")), name=name)(proj, proj, vg, w, wT, b, do)


def loss_head(y, target, *, name):
    S, D = y.shape
    ts = _tile(S, 512, 8)

    def body(y_ref, t_ref, dy_ref, loss_ref):
        i = pl.program_id(0)
        e = y_ref[...] - t_ref[...]
        dy_ref[...] = e * (1.0 / D)
        part = jnp.sum(jnp.sum(e * e, axis=1, keepdims=True), axis=0, keepdims=True) * (0.5 / D)

        @pl.when(i == 0)
        def _():
            loss_ref[...] = part

        @pl.when(i > 0)
        def _():
            loss_ref[...] += part

    row = pl.BlockSpec((ts, D), lambda i: (i, 0))
    return pl.pallas_call(
        body, grid=(S // ts,), in_specs=[row, row],
        out_specs=[row, pl.BlockSpec((1, 1), lambda i: (0, 0))],
        out_shape=[jax.ShapeDtypeStruct((S, D), F32), jax.ShapeDtypeStruct((1, 1), F32)],
        compiler_params=_params(("arbitrary",)), name=name)(y, target)


def adamw(w, g, m, v, *, name):
    shape = w.shape
    C = shape[-1]
    R = w.size // C
    tr = _tile(R, max(8, (256 * 1024) // C // 8 * 8), 8)

    def body(w_ref, g_ref, m_ref, v_ref, d_ref, nm_ref, nv_ref):
        gv = g_ref[...]
        nm = ADAM_B1 * m_ref[...] + (1.0 - ADAM_B1) * gv
        nv = ADAM_B2 * v_ref[...] + (1.0 - ADAM_B2) * (gv * gv)
        m_hat = nm / (1.0 - ADAM_B1 ** ADAM_STEP)
        v_hat = nv / (1.0 - ADAM_B2 ** ADAM_STEP)
        d_ref[...] = -ADAM_LR * (m_hat / (jnp.sqrt(v_hat) + ADAM_EPS) + ADAM_WD * w_ref[...])
        nm_ref[...] = nm
        nv_ref[...] = nv

    blk = pl.BlockSpec((tr, C), lambda i: (i, 0))
    out = pl.pallas_call(
        body, grid=(R // tr,), in_specs=[blk] * 4, out_specs=[blk] * 3,
        out_shape=[jax.ShapeDtypeStruct((R, C), F32)] * 3,
        compiler_params=_params(("parallel",)), name=name)(*(a.reshape(R, C) for a in (w, g, m, v)))
    return tuple(o.reshape(shape) for o in out)


def pair_sum(p, landed, half, *, name):
    n, _, R, C = p.shape
    tr = _tile(R, 256, 16)

    def body(half_ref, p_ref, l_ref, o_ref):
        o_ref[...] = (p_ref[...] + l_ref[...]).astype(BF16)

    return pl.pallas_call(
        body,
        grid_spec=pltpu.PrefetchScalarGridSpec(
            num_scalar_prefetch=1, grid=(n, R // tr),
            in_specs=[pl.BlockSpec((None, None, tr, C), lambda k, r, half_ref: (k, half_ref[0], r, 0)),
                      pl.BlockSpec((None, tr, C), lambda k, r, half_ref: (k, r, 0))],
            out_specs=pl.BlockSpec((None, tr, C), lambda k, r, half_ref: (k, r, 0))),
        out_shape=jax.ShapeDtypeStruct((n, R, C), BF16),
        compiler_params=_params(("parallel", "parallel")), name=name)(half, p, landed)


def chip_sum(own, landed, chip, *, name):
    n, R, C = own.shape
    tr = _tile(R, 256, 16)

    def body(chip_ref, own_ref, *rest):
        l_refs, o_ref = rest[:n], rest[n]
        me = chip_ref[0]
        acc = None
        for d in range(n):
            term = jnp.where(me == d, own_ref[...], l_refs[d][...]).astype(F32)
            acc = term if acc is None else acc + term
        o_ref[...] = acc

    def landed_spec(d):
        return pl.BlockSpec((None, tr, C), lambda r, chip_ref: (jnp.where(chip_ref[0] == d, (d + 1) % n, d), r, 0))

    return pl.pallas_call(
        body,
        grid_spec=pltpu.PrefetchScalarGridSpec(
            num_scalar_prefetch=1, grid=(R // tr,),
            in_specs=[pl.BlockSpec((None, tr, C), lambda r, chip_ref: (chip_ref[0], r, 0))]
            + [landed_spec(d) for d in range(n)],
            out_specs=pl.BlockSpec((tr, C), lambda r, chip_ref: (r, 0))),
        out_shape=jax.ShapeDtypeStruct((R, C), F32),
        compiler_params=_params(("parallel",)), name=name)(chip, own, *([landed] * n))


def ordered_sum(parts, *, name):
    n, R, C = parts.shape
    tr = _tile(R, 256, 16)

    def body(p_ref, o_ref):
        acc = p_ref[0].astype(F32)
        for d in range(1, n):
            acc = acc + p_ref[d].astype(F32)
        o_ref[...] = acc

    return pl.pallas_call(
        body, grid=(R // tr,), in_specs=[pl.BlockSpec((n, tr, C), lambda r: (0, r, 0))],
        out_specs=pl.BlockSpec((tr, C), lambda r: (r, 0)),
        out_shape=jax.ShapeDtypeStruct((R, C), F32),
        compiler_params=_params(("parallel",)), name=name)(parts)


_ANY = pl.BlockSpec(memory_space=pl.ANY)


def _position():
    return lax.axis_index("x"), lax.axis_index("y"), lax.axis_index("c")


def _remote(src, dst, send_sem, recv_sem, device):
    return pltpu.make_async_remote_copy(src_ref=src, dst_ref=dst, send_sem=send_sem, recv_sem=recv_sem,
                                        device_id=device, device_id_type=MESH_ID)


def _small_all_gather(s_ref, all_ref, send_sems, recv_sems, x, y, c):
    me = 4 * x + 2 * y + c
    copies = []
    for f in range(1, 8):
        peer = ((1 - x) if f & 4 else x, (1 - y) if f & 2 else y, (1 - c) if f & 1 else c)
        cp = _remote(s_ref, all_ref.at[me], send_sems.at[f - 1], recv_sems.at[f - 1], peer)
        cp.start()
        copies.append((cp, peer, f - 1))

    def finish():
        for cp, peer, s in copies:
            slot = all_ref.at[4 * peer[0] + 2 * peer[1] + peer[2]]
            _remote(slot, slot, send_sems.at[s], recv_sems.at[s], peer).wait_recv()
        for cp, _, _ in copies:
            cp.wait_send()

    return finish


def gather_weights(slabs, small_slab, *, name):
    n = len(slabs)

    def body(*refs):
        outs, all_ref = refs[n + 1:2 * n + 1], refs[2 * n + 1]
        send_sems, recv_sems, s_send, s_recv = refs[2 * n + 2:]
        x, y, c = _position()
        k = 2 * x + y
        sibling = (x, y, 1 - c)
        chips = [(1 - x, y), (x, 1 - y), (1 - x, 1 - y)]
        finish_small = _small_all_gather(all_ref.at[4 * x + 2 * y + c], all_ref, s_send, s_recv, x, y, c)
        first = []
        for j, (px, py) in enumerate(chips):
            for w in range(n):
                slot = outs[w].at[k, c]
                cp = _remote(slot, slot, send_sems.at[w, j], recv_sems.at[w, j], (px, py, c))
                cp.start()
                first.append(cp)
        passed = []
        for j, (px, py) in enumerate(chips):
            for w in range(n):
                slot = outs[w].at[2 * px + py, c]
                _remote(slot, slot, send_sems.at[w, j], recv_sems.at[w, j], (px, py, c)).wait_recv()
                cp = _remote(slot, slot, send_sems.at[w, 3 + j], recv_sems.at[w, 3 + j], sibling)
                cp.start()
                passed.append(cp)
        for j, (px, py) in enumerate(chips):
            for w in range(n):
                slot = outs[w].at[2 * px + py, 1 - c]
                _remote(slot, slot, send_sems.at[w, 3 + j], recv_sems.at[w, 3 + j], sibling).wait_recv()
        for cp in first + passed:
            cp.wait_send()
        finish_small()

    args = list(slabs) + [small_slab]
    out = pl.pallas_call(
        body, in_specs=[_ANY] * (n + 1), out_specs=[_ANY] * (n + 1),
        out_shape=[jax.ShapeDtypeStruct(a.shape, a.dtype) for a in args],
        input_output_aliases={i: i for i in range(n + 1)},
        scratch_shapes=[pltpu.SemaphoreType.DMA((n, 6)), pltpu.SemaphoreType.DMA((n, 6)),
                        pltpu.SemaphoreType.DMA((7,)), pltpu.SemaphoreType.DMA((7,))],
        name=name)(*args)
    return out[:n], out[n]


def exchange_with_sibling(parts, small_slab, *, name):
    n = len(parts)

    def body(*refs):
        p_refs = refs[:n]
        lands, all_ref = refs[n + 1:2 * n + 1], refs[2 * n + 1]
        send_sems, recv_sems, s_send, s_recv = refs[2 * n + 2:]
        x, y, c = _position()
        sibling = (x, y, 1 - c)
        finish_small = _small_all_gather(all_ref.at[4 * x + 2 * y + c], all_ref, s_send, s_recv, x, y, c)
        sends = []
        for w in range(n):
            for d in range(4):
                cp = _remote(p_refs[w].at[d, 1 - c], lands[w].at[d], send_sems.at[w, d], recv_sems.at[w, d], sibling)
                cp.start()
                sends.append(cp)
        for cp in sends:
            cp.wait_recv()
        for cp in sends:
            cp.wait_send()
        finish_small()

    out = pl.pallas_call(
        body, in_specs=[_ANY] * (n + 1), out_specs=[_ANY] * (n + 1),
        out_shape=[jax.ShapeDtypeStruct((4,) + p.shape[2:], p.dtype) for p in parts]
        + [jax.ShapeDtypeStruct(small_slab.shape, small_slab.dtype)],
        input_output_aliases={n: n},
        scratch_shapes=[pltpu.SemaphoreType.DMA((n, 4)), pltpu.SemaphoreType.DMA((n, 4)),
                        pltpu.SemaphoreType.DMA((7,)), pltpu.SemaphoreType.DMA((7,))],
        name=name)(*parts, small_slab)
    return out[:n], out[n]


def scatter_to_chips(parts, *, name):
    n = len(parts)

    def body(*refs):
        q_refs, outs = refs[:n], refs[n:2 * n]
        send_sems, recv_sems = refs[2 * n:]
        x, y, c = _position()
        k = 2 * x + y
        chips = [(1 - x, y), (x, 1 - y), (1 - x, 1 - y)]
        sends = []
        for j, (px, py) in enumerate(chips):
            for w in range(n):
                cp = _remote(q_refs[w].at[2 * px + py], outs[w].at[k], send_sems.at[w, j], recv_sems.at[w, j], (px, py, c))
                cp.start()
                sends.append(cp)
        for j, (px, py) in enumerate(chips):
            for w in range(n):
                slot = outs[w].at[2 * px + py]
                _remote(slot, slot, send_sems.at[w, j], recv_sems.at[w, j], (px, py, c)).wait_recv()
        for cp in sends:
            cp.wait_send()

    return pl.pallas_call(
        body, in_specs=[_ANY] * n, out_specs=[_ANY] * n,
        out_shape=[jax.ShapeDtypeStruct(q.shape, q.dtype) for q in parts],
        scratch_shapes=[pltpu.SemaphoreType.DMA((n, 3)), pltpu.SemaphoreType.DMA((n, 3))],
        name=name)(*parts)


def share_with_sibling(parts, *, name):
    n = len(parts)

    def body(*refs):
        r_refs, outs = refs[:n], refs[n:2 * n]
        send_sems, recv_sems = refs[2 * n:]
        x, y, c = _position()
        sends = []
        for w in range(n):
            cp = _remote(r_refs[w], outs[w], send_sems.at[w], recv_sems.at[w], (x, y, 1 - c))
            cp.start()
            sends.append(cp)
        for cp in sends:
            cp.wait_recv()
        for cp in sends:
            cp.wait_send()

    return pl.pallas_call(
        body, in_specs=[_ANY] * n, out_specs=[_ANY] * n,
        out_shape=[jax.ShapeDtypeStruct(r.shape, r.dtype) for r in parts],
        scratch_shapes=[pltpu.SemaphoreType.DMA((n,)), pltpu.SemaphoreType.DMA((n,))],
        name=name)(*parts)


def _cols_to_chips(full):
    *lead, R, C4 = full.shape
    t = full.reshape(*lead, R, 4, C4 // 4)
    return jnp.moveaxis(t, -2, 0)


def _chips_to_cols(sh):
    t = jnp.moveaxis(sh, 0, -2)
    return t.reshape(*t.shape[:-2], t.shape[-2] * t.shape[-1])


def _slot_in_empty(own, index, n):
    return lax.dynamic_update_slice(lax.empty((n,) + own.shape, own.dtype), own[None], (index,) + (0,) * own.ndim)


def _fold_pair(dg):
    return dg[0, :HEAD_DIM] + dg[0, HEAD_DIM:]


def _ffn_fwd(x, g, w_in_slab, w_out, layer, tag):
    h = rms_fwd(x, g, name=f"{tag}_rms")
    a, b, act = swiglu_fwd(h, w_in_slab, layer, name=f"{tag}_in")
    y = matmul(act, w_out, res=x, scale=0.5, tm=1024, tn=512, tk=w_out.shape[0], name=f"{tag}_out")
    return y, (x, h, a, b, act)


def _ffn_bwd(dy, saved, g, w_in_slab, w_out, layer, n_layers, dw_in_slab, dw_out_slab, tag):
    x, h, a, b, act = saved
    da, db = swiglu_bwd(dy, w_out, a, b, name=f"{tag}_dact")
    dw_out_slab = grad_rows(act, dy, dw_out_slab, layer, n_layers, scale=0.5, name=f"{tag}_dwout")
    dw_in_slab = grad_cols(h, da, db, dw_in_slab, layer, n_layers, name=f"{tag}_dwin")
    dh = ffn_dh(da, db, w_in_slab, layer, name=f"{tag}_dh")
    dx, dg = rms_bwd(x, dh, g, dy, name=f"{tag}_drms")
    return dx, dg[0], dw_in_slab, dw_out_slab


MEM_PAIRS = MEM_WIDTH // LANES


def _mem_attn_fwd(proj, mq0, mem_n, w_kv, g_q, g_k, tag):
    qh = pairnorm_fwd(proj, mq0, MEM_PAIRS, g_q, scale=QK_SCALE, name=f"{tag}_qnorm")
    kv = matmul(mem_n, w_kv, tm=256, tn=512, tk=1024, name=f"{tag}_kv")
    kh = pairnorm_fwd(kv, 0, MEM_PAIRS, g_k, name=f"{tag}_knorm")
    o, lse = attn_fwd(qh, 0, kh, 0, kv, MEM_PAIRS, MEM_PAIRS, None, None, causal=False, name=f"{tag}_attn")
    return o, (qh, kv, kh, o, lse)


def _mem_attn_bwd(dmix, do0, proj, mq0, saved, mem_n, g_q, g_k, layer, n_layers, dw_kv_slab, tag):
    qh, kv, kh, o, lse = saved
    delta = attn_delta(o, dmix, do0, MEM_PAIRS, name=f"{tag}_delta")
    dqh, dkh, dv = attn_bwd(qh, 0, kh, 0, kv, MEM_PAIRS, dmix, do0, MEM_PAIRS, lse, delta, None, None,
                            causal=False, name=f"{tag}_dattn")
    dq_pre, dgq = pairnorm_bwd(proj, mq0, MEM_PAIRS, dqh, g_q, name=f"{tag}_dqnorm")
    dk_pre, dgk = pairnorm_bwd(kv, 0, MEM_PAIRS, dkh, g_k, name=f"{tag}_dknorm")
    dkv = jnp.concatenate([dk_pre, dv], axis=1)
    dw_kv_slab = grad_rows(mem_n, dkv, dw_kv_slab, layer, n_layers, name=f"{tag}_dwkv")
    return dq_pre, _fold_pair(dgq), _fold_pair(dgk), dw_kv_slab, dkv


def _decay_terms(c, H):
    S = c.shape[0]
    ch = c[:, :H].T
    return ch.reshape(H // 2, 2, S, 1), ch.reshape(H // 2, 2, 1, S)


def _per_head_lanes(x, H):
    return jnp.pad(x.reshape(H, -1).T, ((0, 0), (0, LANES - H)))


def _fox_fwd(proj, b_f, g_q, g_k, tok, tag):
    H, P = tok // HEAD_DIM, tok // LANES
    bias = jnp.pad(b_f.reshape(1, H), ((0, 0), (0, LANES - H)))
    qh = pairnorm_fwd(proj, 0, P, g_q, scale=QK_SCALE, name=f"{tag}_qnorm")
    kh = pairnorm_fwd(proj, P, P, g_k, name=f"{tag}_knorm")
    c = fgate_fwd(proj, 3 * P + MEM_PAIRS, bias, name=f"{tag}_fgate")
    cq, ck = _decay_terms(c, H)
    o, lse = attn_fwd(qh, 0, kh, 0, proj, 2 * P, P, cq, ck, causal=True, name=f"{tag}_attn")
    return o, (qh, kh, bias, cq, ck, o, lse)


def _fox_bwd(dmix, proj, saved, g_q, g_k, tok, tag):
    qh, kh, bias, cq, ck, o, lse = saved
    H, P = tok // HEAD_DIM, tok // LANES
    delta = attn_delta(o, dmix, 0, P, name=f"{tag}_delta")
    dqh, dkh, dv, dcs, drs = attn_bwd(qh, 0, kh, 0, proj, 2 * P, dmix, 0, P, lse, delta, cq, ck, causal=True,
                                      name=f"{tag}_dattn")
    dq_pre, dgq = pairnorm_bwd(proj, 0, P, dqh, g_q, name=f"{tag}_dqnorm")
    dk_pre, dgk = pairnorm_bwd(proj, P, P, dkh, g_k, name=f"{tag}_dknorm")
    dz, dbias = fgate_bwd(proj, 3 * P + MEM_PAIRS, bias, _per_head_lanes(drs, H), _per_head_lanes(dcs, H),
                          name=f"{tag}_dfgate")
    dqkv = jnp.concatenate([dq_pre, dk_pre, dv], axis=1)
    return dqkv, dz, dbias[0, :H], _fold_pair(dgq), _fold_pair(dgk)


def local_step(x, mem, target, W):
    S, D = x.shape
    tok = D - MEM_WIDTH
    P = tok // LANES
    depth = W["norm_ffn1"].shape[0]
    mem_n = rms_fwd(mem, W["mem_norm"], name="mem_rms")
    saved = []
    for i in range(depth):
        kind, j = i % 2, i // 2
        t = f"l{i}"
        x1, s1 = _ffn_fwd(x, W["norm_ffn1"][i], W["ffn1_w_in"], W["ffn1_w_out"][i], i, f"{t}_ffn1")
        h = rms_fwd(x1, W["norm_mix"][i], name=f"{t}_mix_rms")
        w_mix = W["fox_w_in"][j] if kind == 0 else W["gmlp_w_in"][j]
        proj = matmul(h, w_mix, tm=1024, tn=896, tk=D, name=f"{t}_mix_in")
        if kind == 0:
            o_tok, s_tok = _fox_fwd(proj, W["fox_b_f"][j], W["fox_q_norm"][j], W["fox_k_norm"][j], tok, f"{t}_fox")
            mq0 = 3 * P
        else:
            vg, ws, bs = _gmlp_operands(W["gmlp_v_norm"][j], W["gmlp_w_s"][j], W["gmlp_b_s"][j])
            o_tok = gmlp_fwd(proj, P, P, vg, ws, bs, name=f"{t}_gmlp")
            s_tok = None
            mq0 = 2 * P
        o_mem, s_mem = _mem_attn_fwd(proj, mq0, mem_n, W["mem_w_kv"][i], W["mem_q_norm"][i], W["mem_k_norm"][i],
                                     f"{t}_mem")
        mix = jnp.concatenate([o_tok, o_mem], axis=1).astype(BF16)
        x2 = matmul(mix, W["w_out"][i], res=x1, tm=1024, tn=512, tk=D, name=f"{t}_mix_out")
        x3, s3 = _ffn_fwd(x2, W["norm_ffn2"][i], W["ffn2_w_in"], W["ffn2_w_out"][i], i, f"{t}_ffn2")
        saved.append((s1, x1, h, proj, mq0, s_tok, s_mem, mix, s3))
        x = x3

    dx, loss = loss_head(x, target, name="loss_head")

    G = {k: [None] * depth for k in ("norm_ffn1", "norm_mix", "norm_ffn2", "mem_q_norm", "mem_k_norm")}
    n_fox, n_gmlp = (depth + 1) // 2, depth // 2
    for k in ("fox_w_in", "fox_b_f", "fox_q_norm", "fox_k_norm"):
        G[k] = [None] * n_fox
    for k in ("gmlp_w_in", "gmlp_v_norm", "gmlp_w_s", "gmlp_b_s"):
        G[k] = [None] * n_gmlp
    slabs = {k: None for k in ("ffn1_w_in", "ffn1_w_out", "ffn2_w_in", "ffn2_w_out", "w_out", "mem_w_kv")}
    dkv_all = [None] * depth
    for i in reversed(range(depth)):
        kind, j = i % 2, i // 2
        t = f"l{i}"
        s1, x1, h, proj, mq0, s_tok, s_mem, mix, s3 = saved[i]
        dx, G["norm_ffn2"][i], slabs["ffn2_w_in"], slabs["ffn2_w_out"] = _ffn_bwd(
            dx, s3, W["norm_ffn2"][i], W["ffn2_w_in"], W["ffn2_w_out"][i], i, depth,
            slabs["ffn2_w_in"], slabs["ffn2_w_out"], f"{t}_ffn2")
        dmix = matmul(dx, W["w_out"][i], tb=True, tm=1024, tn=1024, tk=D, name=f"{t}_dmix")
        slabs["w_out"] = grad_rows(mix, dx, slabs["w_out"], i, depth, name=f"{t}_dwmixout")
        dmq, G["mem_q_norm"][i], G["mem_k_norm"][i], slabs["mem_w_kv"], dkv_all[i] = _mem_attn_bwd(
            dmix, P, proj, mq0, s_mem, mem_n, W["mem_q_norm"][i], W["mem_k_norm"][i], i, depth, slabs["mem_w_kv"],
            f"{t}_mem")
        if kind == 0:
            dqkv, dz, G["fox_b_f"][j], G["fox_q_norm"][j], G["fox_k_norm"][j] = _fox_bwd(
                dmix, proj, s_tok, W["fox_q_norm"][j], W["fox_k_norm"][j], tok, f"{t}_fox")
            dproj = jnp.concatenate([dqkv, dmq, dz], axis=1).astype(BF16)
            w_mix, wkey = W["fox_w_in"][j], "fox_w_in"
        else:
            vg, ws, bs = _gmlp_operands(W["gmlp_v_norm"][j], W["gmlp_w_s"][j], W["gmlp_b_s"][j])
            dup, dvp, dws, dbs, dvg = gmlp_bwd(proj, P, P, vg, ws, jnp.swapaxes(ws, 2, 3), bs, dmix,
                                               name=f"{t}_dgmlp")
            G["gmlp_w_s"][j] = dws.reshape(W["gmlp_w_s"][j].shape)
            G["gmlp_b_s"][j] = dbs.reshape(W["gmlp_b_s"][j].shape)
            G["gmlp_v_norm"][j] = dvg.reshape(-1)
            dproj = jnp.concatenate([dup, dvp, dmq], axis=1).astype(BF16)
            w_mix, wkey = W["gmlp_w_in"][j], "gmlp_w_in"
        G[wkey][j] = matmul(h, dproj, ta=True, tm=1024, tn=896, tk=1024, name=f"{t}_dwmixin")
        dh = matmul(dproj, w_mix, tb=True, tm=1024, tn=1024, tk=896, name=f"{t}_dhmix")
        dx, dgm = rms_bwd(x1, dh, W["norm_mix"][i], dx, name=f"{t}_dmixrms")
        G["norm_mix"][i] = dgm[0]
        dx, G["norm_ffn1"][i], slabs["ffn1_w_in"], slabs["ffn1_w_out"] = _ffn_bwd(
            dx, s1, W["norm_ffn1"][i], W["ffn1_w_in"], W["ffn1_w_out"][i], i, depth,
            slabs["ffn1_w_in"], slabs["ffn1_w_out"], f"{t}_ffn1")
    w_kv_all = jnp.concatenate([W["mem_w_kv"][i] for i in range(depth)], axis=1)
    dmem_n = matmul(jnp.concatenate(dkv_all, axis=1), w_kv_all, tb=True, tm=256, tn=512, tk=1024, name="dmem_n")
    _, dmemg = rms_bwd(mem, dmem_n, W["mem_norm"], None, name="dmem_rms")
    grads = {k: jnp.stack(v) for k, v in G.items()}
    grads["mem_norm"] = dmemg[0]
    grads.update(slabs)
    return loss, dx, grads


def _fox_cols_to_compute(w, tok):
    H = tok // HEAD_DIM
    qkv, f, mq = w[..., :3 * tok], w[..., 3 * tok:3 * tok + H], w[..., 3 * tok + H:]
    f = jnp.pad(f, [(0, 0)] * (w.ndim - 1) + [(0, LANES - H)])
    return jnp.concatenate([qkv, mq, f], axis=-1)


def _fox_cols_from_compute(w, tok):
    H = tok // HEAD_DIM
    qkv, mq, f = w[..., :3 * tok], w[..., 3 * tok:3 * tok + MEM_WIDTH], w[..., 3 * tok + MEM_WIDTH:3 * tok + MEM_WIDTH + H]
    return jnp.concatenate([qkv, f, mq], axis=-1)


_BIG = ("ffn1_w_in", "ffn1_w_out", "ffn2_w_in", "ffn2_w_out", "w_out", "mem_w_kv", "fox_w_in", "gmlp_w_in")
_SMALL = ("norm_ffn1", "norm_mix", "norm_ffn2", "mem_norm", "mem_q_norm", "mem_k_norm", "fox_b_f", "fox_q_norm",
          "fox_k_norm", "gmlp_v_norm", "gmlp_w_s", "gmlp_b_s")
WEIGHT_ORDER = ("norm_ffn1", "ffn1_w_in", "ffn1_w_out", "norm_mix", "norm_ffn2", "ffn2_w_in", "ffn2_w_out", "w_out",
                "mem_norm", "mem_w_kv", "mem_q_norm", "mem_k_norm", "fox_w_in", "fox_b_f", "fox_q_norm", "fox_k_norm",
                "gmlp_w_in", "gmlp_v_norm", "gmlp_w_s", "gmlp_b_s")


def _halves(a):
    if a.shape[0] == 2:
        return a
    return a.reshape(2, a.shape[1] // 2, a.shape[2])


def _small_slab(rows_list, index):
    sizes = [s.shape[0] for s in rows_list]
    n_rows = [-(-n // LANES) for n in sizes]
    small = jnp.concatenate([jnp.pad(s, (0, r * LANES - n)).reshape(r, LANES)
                             for s, n, r in zip(rows_list, sizes, n_rows)], axis=0)
    small = jnp.pad(small, ((0, -small.shape[0] % 64), (0, 0)))
    return _slot_in_empty(small, index, 8), sizes, n_rows


def _gather_weights(shards, chip, device):
    slabs = [_slot_in_empty(_halves(shards[k].astype(BF16)), chip, 4) for k in _BIG]
    vn = shards["gmlp_v_norm"]
    small, _, _ = _small_slab([vn.reshape(-1)], device)
    slabs, small_all = gather_weights(slabs, small, name="gather_weights")
    per_chip = small_all[0::2].reshape(4, -1)[:, :vn.size].reshape((4,) + vn.shape)
    return dict(zip(_BIG, slabs)), _chips_to_cols(per_chip)


def kernel(x, mem, norm_ffn1, ffn1_w_in, ffn1_w_out, norm_mix, norm_ffn2, ffn2_w_in, ffn2_w_out, w_out, mem_norm, mem_w_kv, mem_q_norm, mem_k_norm, fox_w_in, fox_b_f, fox_q_norm, fox_k_norm, gmlp_w_in, gmlp_v_norm, gmlp_w_s, gmlp_b_s, loss_target, m_norm_ffn1, m_ffn1_w_in, m_ffn1_w_out, m_norm_mix, m_norm_ffn2, m_ffn2_w_in, m_ffn2_w_out, m_w_out, m_mem_norm, m_mem_w_kv, m_mem_q_norm, m_mem_k_norm, m_fox_w_in, m_fox_b_f, m_fox_q_norm, m_fox_k_norm, m_gmlp_w_in, m_gmlp_v_norm, m_gmlp_w_s, m_gmlp_b_s, v_norm_ffn1, v_ffn1_w_in, v_ffn1_w_out, v_norm_mix, v_norm_ffn2, v_ffn2_w_in, v_ffn2_w_out, v_w_out, v_mem_norm, v_mem_w_kv, v_mem_q_norm, v_mem_k_norm, v_fox_w_in, v_fox_b_f, v_fox_q_norm, v_fox_k_norm, v_gmlp_w_in, v_gmlp_v_norm, v_gmlp_w_s, v_gmlp_b_s):
    w = dict(norm_ffn1=norm_ffn1, ffn1_w_in=ffn1_w_in, ffn1_w_out=ffn1_w_out, norm_mix=norm_mix, norm_ffn2=norm_ffn2,
             ffn2_w_in=ffn2_w_in, ffn2_w_out=ffn2_w_out, w_out=w_out, mem_norm=mem_norm, mem_w_kv=mem_w_kv,
             mem_q_norm=mem_q_norm, mem_k_norm=mem_k_norm, fox_w_in=fox_w_in, fox_b_f=fox_b_f, fox_q_norm=fox_q_norm,
             fox_k_norm=fox_k_norm, gmlp_w_in=gmlp_w_in, gmlp_v_norm=gmlp_v_norm, gmlp_w_s=gmlp_w_s, gmlp_b_s=gmlp_b_s)
    m = dict(norm_ffn1=m_norm_ffn1, ffn1_w_in=m_ffn1_w_in, ffn1_w_out=m_ffn1_w_out, norm_mix=m_norm_mix,
             norm_ffn2=m_norm_ffn2, ffn2_w_in=m_ffn2_w_in, ffn2_w_out=m_ffn2_w_out, w_out=m_w_out, mem_norm=m_mem_norm,
             mem_w_kv=m_mem_w_kv, mem_q_norm=m_mem_q_norm, mem_k_norm=m_mem_k_norm, fox_w_in=m_fox_w_in,
             fox_b_f=m_fox_b_f, fox_q_norm=m_fox_q_norm, fox_k_norm=m_fox_k_norm, gmlp_w_in=m_gmlp_w_in,
             gmlp_v_norm=m_gmlp_v_norm, gmlp_w_s=m_gmlp_w_s, gmlp_b_s=m_gmlp_b_s)
    v = dict(norm_ffn1=v_norm_ffn1, ffn1_w_in=v_ffn1_w_in, ffn1_w_out=v_ffn1_w_out, norm_mix=v_norm_mix,
             norm_ffn2=v_norm_ffn2, ffn2_w_in=v_ffn2_w_in, ffn2_w_out=v_ffn2_w_out, w_out=v_w_out, mem_norm=v_mem_norm,
             mem_w_kv=v_mem_w_kv, mem_q_norm=v_mem_q_norm, mem_k_norm=v_mem_k_norm, fox_w_in=v_fox_w_in,
             fox_b_f=v_fox_b_f, fox_q_norm=v_fox_q_norm, fox_k_norm=v_fox_k_norm, gmlp_w_in=v_gmlp_w_in,
             gmlp_v_norm=v_gmlp_v_norm, gmlp_w_s=v_gmlp_w_s, gmlp_b_s=v_gmlp_b_s)
    D = x.shape[-1]
    tok = D - MEM_WIDTH
    xi, yi, ci = _position()
    chip = 2 * xi + yi

    device = 4 * xi + 2 * yi + ci

    slabs, v_norm_full = _gather_weights(w, chip, device)
    depth = norm_ffn1.shape[0]
    W = {k: w[k] for k in _SMALL}
    W["gmlp_v_norm"] = v_norm_full
    W["ffn1_w_in"], W["ffn2_w_in"] = slabs["ffn1_w_in"], slabs["ffn2_w_in"]
    for k in ("ffn1_w_out", "ffn2_w_out", "w_out", "mem_w_kv"):
        s = slabs[k]
        W[k] = [s[:, i].reshape(4 * s.shape[2], s.shape[3]) for i in range(depth)]
    for k in ("fox_w_in", "gmlp_w_in"):
        s = slabs[k]
        full = _chips_to_cols(s.reshape((4, w[k].shape[0], 2 * s.shape[2], s.shape[3])))
        W[k] = [_fox_cols_to_compute(full[j], tok) if k == "fox_w_in" else full[j] for j in range(full.shape[0])]

    loss, grad_x, g = local_step(x[0], mem[0], loss_target[0], W)

    parts = []
    for k in _BIG:
        if k == "fox_w_in":
            p = _cols_to_chips(_fox_cols_from_compute(g[k], tok))
        elif k == "gmlp_w_in":
            p = _cols_to_chips(g[k])
        else:
            p = g[k]
        parts.append(p if p.shape[1] == 2 else p.reshape(4, 2, p.shape[2] // 2, p.shape[3]))
    small_list = [g[k].reshape(-1) for k in _SMALL] + [loss.reshape(-1)]
    small, small_sizes, small_rows = _small_slab(small_list, device)

    half = ci.reshape(1).astype(jnp.int32)
    landed, small_all = exchange_with_sibling(parts, small, name="grad_pair_exchange")
    pair = [pair_sum(p, l, half, name=f"grad_pair_sum_{k}") for k, p, l in zip(_BIG, parts, landed)]
    from_chips = scatter_to_chips(pair, name="grad_scatter")
    chip_id = chip.reshape(1).astype(jnp.int32)
    mine = [chip_sum(q, l, chip_id, name=f"grad_chip_sum_{k}") for k, q, l in zip(_BIG, pair, from_chips)]
    other = share_with_sibling(mine, name="grad_share")
    small_sum = ordered_sum(small_all, name="small_sum")

    red = {}
    for k, a, b in zip(_BIG, mine, other):
        red[k] = jnp.where(ci == 0, jnp.stack([a, b]), jnp.stack([b, a])).reshape(w[k].shape)
    off = 0
    for k, n, r in zip(_SMALL, small_sizes, small_rows):
        red[k] = small_sum[off:off + r].reshape(-1)[:n].reshape(g[k].shape)
        off += r
    loss_total = small_sum[off, 0]
    vn_cols = w["gmlp_v_norm"].shape[-1]
    red["gmlp_v_norm"] = lax.dynamic_slice_in_dim(red["gmlp_v_norm"], chip * vn_cols, vn_cols, axis=-1)

    deltas, new_m, new_v = {}, {}, {}
    for k in WEIGHT_ORDER:
        wk = w[k] if w[k].ndim > 1 else w[k].reshape(1, -1)
        upd = adamw(wk, red[k].reshape(wk.shape), m[k].reshape(wk.shape), v[k].reshape(wk.shape), name=f"adamw_{k}")
        deltas[k], new_m[k], new_v[k] = (u.reshape(w[k].shape) for u in upd)
    return (loss_total, grad_x[None], *[red[k].reshape(w[k].shape) for k in WEIGHT_ORDER],
            *[deltas[k] for k in WEIGHT_ORDER], *[new_m[k] for k in WEIGHT_ORDER], *[new_v[k] for k in WEIGHT_ORDER])
```

```python
import functools
import math
from typing import Callable, NamedTuple

import jax
import jax.numpy as jnp
from jax import lax
from jax.experimental import pallas as pl
from jax.experimental.pallas import tpu as pltpu

F32 = jnp.float32
BF16 = jnp.bfloat16
EPS = 1e-6
HEAD_DIM = 64
MEM_WIDTH = 256
CHUNK = 128
LANES = 128
NEG = -1e30
VMEM_LIMIT_BYTES = 56 * 1024 * 1024
ATTN_Q_BLOCK = 1024
ATTN_K_BLOCK = 1024
QK_SCALE = 0.125
MESH_ID = pl.DeviceIdType.MESH

ADAM_LR = 0.001
ADAM_B1 = 0.9
ADAM_B2 = 0.999
ADAM_EPS = 1e-08
ADAM_WD = 0.01
ADAM_STEP = 10


def _tile(n, pref, align):
    t = (min(pref, n) // align) * align
    while t >= align:
        if n % t == 0:
            return t
        t -= align
    return n


def _params(sem):
    return pltpu.CompilerParams(dimension_semantics=sem, vmem_limit_bytes=VMEM_LIMIT_BYTES)


def _dot(a, b, ca, cb):
    return lax.dot_general(a, b, (((ca,), (cb,)), ((), ())), preferred_element_type=F32)


def _sigmoid(x):
    return 1.0 / (1.0 + jnp.exp(-x))


_GELU_C = math.sqrt(2.0 / math.pi)


def _gelu(x):
    return 0.5 * x * (1.0 + jnp.tanh(_GELU_C * (x + 0.044715 * (x * x * x))))


def _gelu_grad(x):
    t = jnp.tanh(_GELU_C * (x + 0.044715 * (x * x * x)))
    return 0.5 * (1.0 + t) + 0.5 * x * (1.0 - t * t) * (_GELU_C * (1.0 + 3.0 * 0.044715 * (x * x)))


def matmul(a, b, *, ta=False, tb=False, out_dtype=F32, scale=None, res=None,
           tm=1024, tn=512, tk=1024, name):
    if ta:
        K, M = a.shape
    else:
        M, K = a.shape
    N = b.shape[0] if tb else b.shape[1]
    tm = _tile(M, tm, LANES if ta else 16)
    tn = _tile(N, tn, LANES)
    tk = _tile(K, tk, LANES)
    nk = K // tk
    a_spec = pl.BlockSpec((tk, tm), lambda i, j, k: (k, i)) if ta else pl.BlockSpec((tm, tk), lambda i, j, k: (i, k))
    b_spec = pl.BlockSpec((tn, tk), lambda i, j, k: (j, k)) if tb else pl.BlockSpec((tk, tn), lambda i, j, k: (k, j))
    o_spec = pl.BlockSpec((tm, tn), lambda i, j, k: (i, j))
    ca, cb = (0 if ta else 1), (1 if tb else 0)
    has_res = res is not None

    def body(*refs):
        a_ref, b_ref = refs[0], refs[1]
        res_ref = refs[2] if has_res else None
        o_ref = refs[3] if has_res else refs[2]
        acc_ref = refs[-1]
        k = pl.program_id(2)
        prod = _dot(a_ref[...].astype(BF16), b_ref[...].astype(BF16), ca, cb)

        def finish(acc):
            if scale is not None:
                acc = acc * scale
            if has_res:
                acc = res_ref[...] + acc
            o_ref[...] = acc.astype(out_dtype)

        if nk == 1:
            finish(prod)
        else:
            @pl.when(k == 0)
            def _():
                acc_ref[...] = prod

            @pl.when(k > 0)
            def _():
                acc_ref[...] += prod

            @pl.when(k == nk - 1)
            def _():
                finish(acc_ref[...])

    in_specs = [a_spec, b_spec] + ([o_spec] if has_res else [])
    args = (a, b) + ((res,) if has_res else ())
    return pl.pallas_call(
        body, grid=(M // tm, N // tn, nk), in_specs=in_specs, out_specs=o_spec,
        out_shape=jax.ShapeDtypeStruct((M, N), out_dtype),
        scratch_shapes=[pltpu.VMEM((tm, tn) if nk > 1 else (8, LANES), F32)],
        compiler_params=_params(("parallel", "parallel", "arbitrary")), name=name)(*args)


def swiglu_fwd(h, w_slab, *, name):
    S, D = h.shape
    Fc = w_slab.shape[-1]
    tm = _tile(S, 512, 16)

    def body(h_ref, wa_ref, wb_ref, a_ref, b_ref, act_ref):
        hv = h_ref[...]
        a = _dot(hv, wa_ref[...], 1, 0)
        b = _dot(hv, wb_ref[...], 1, 0)
        a_ref[...] = a.astype(BF16)
        b_ref[...] = b.astype(BF16)
        act_ref[...] = (a * _sigmoid(a) * b).astype(BF16)

    out = pl.BlockSpec((tm, Fc), lambda j, i: (i, j))
    return pl.pallas_call(
        body, grid=(2, S // tm),
        in_specs=[pl.BlockSpec((tm, D), lambda j, i: (i, 0)),
                  pl.BlockSpec((None, D, Fc), lambda j, i: (j, 0, 0)),
                  pl.BlockSpec((None, D, Fc), lambda j, i: (j + 2, 0, 0))],
        out_specs=[out, out, out],
        out_shape=[jax.ShapeDtypeStruct((S, 2 * Fc), BF16)] * 3,
        compiler_params=_params(("parallel", "parallel")), name=name)(h, w_slab, w_slab)


def swiglu_bwd(dy, w_out, a, b, *, name):
    S, D = dy.shape
    F = w_out.shape[0]
    fc = F // 2
    tm = _tile(S, 512, 16)

    def body(dy_ref, w_ref, a_ref, b_ref, da_ref, db_ref):
        dact = 0.5 * _dot(dy_ref[...].astype(BF16), w_ref[...], 1, 1)
        av = a_ref[...].astype(F32)
        sg = _sigmoid(av)
        da_ref[...] = (dact * b_ref[...].astype(F32) * (sg * (1.0 + av * (1.0 - sg)))).astype(BF16)
        db_ref[...] = (dact * (av * sg)).astype(BF16)

    blk = pl.BlockSpec((tm, fc), lambda j, i: (i, j))
    return pl.pallas_call(
        body, grid=(2, S // tm),
        in_specs=[pl.BlockSpec((tm, D), lambda j, i: (i, 0)), pl.BlockSpec((fc, D), lambda j, i: (j, 0)), blk, blk],
        out_specs=[blk, blk],
        out_shape=[jax.ShapeDtypeStruct((S, F), BF16), jax.ShapeDtypeStruct((S, F), BF16)],
        compiler_params=_params(("parallel", "parallel")), name=name)(dy, w_out, a, b)


def ffn_dh(da, db, w_slab, *, name):
    S, F = da.shape
    D, Fc = w_slab.shape[-2:]
    tm = _tile(S, 1024, 16)

    def body(da_ref, db_ref, w_ref, o_ref, acc_ref):
        k = pl.program_id(1)

        @pl.when(k == 0)
        def _():
            acc_ref[...] = jnp.zeros_like(acc_ref)

        @pl.when(k < 2)
        def _():
            acc_ref[...] += _dot(da_ref[...], w_ref[...], 1, 1)

        @pl.when(k >= 2)
        def _():
            acc_ref[...] += _dot(db_ref[...], w_ref[...], 1, 1)

        @pl.when(k == 3)
        def _():
            o_ref[...] = acc_ref[...]

    return pl.pallas_call(
        body, grid=(S // tm, 4),
        in_specs=[pl.BlockSpec((tm, Fc), lambda i, k: (i, jnp.minimum(k, 1))),
                  pl.BlockSpec((tm, Fc), lambda i, k: (i, jnp.maximum(k - 2, 0))),
                  pl.BlockSpec((None, D, Fc), lambda i, k: (k, 0, 0))],
        out_specs=pl.BlockSpec((tm, D), lambda i, k: (i, 0)),
        out_shape=jax.ShapeDtypeStruct((S, D), F32),
        scratch_shapes=[pltpu.VMEM((tm, D), F32)],
        compiler_params=_params(("parallel", "arbitrary")), name=name)(da, db, w_slab)


def grad_cols(h, da, db, *, name):
    S, D = h.shape
    Fc = da.shape[1] // 2
    tk = _tile(S, 512, 16)
    nk = S // tk

    def body(h_ref, da_ref, db_ref, o_ref, acc_ref):
        ch, k = pl.program_id(0), pl.program_id(1)

        @pl.when(k == 0)
        def _():
            acc_ref[...] = jnp.zeros_like(acc_ref)

        @pl.when(ch < 2)
        def _():
            acc_ref[...] += _dot(h_ref[...], da_ref[...], 0, 0)

        @pl.when(ch >= 2)
        def _():
            acc_ref[...] += _dot(h_ref[...], db_ref[...], 0, 0)

        @pl.when(k == nk - 1)
        def _():
            o_ref[...] = acc_ref[...]

    return pl.pallas_call(
        body, grid=(4, nk),
        in_specs=[pl.BlockSpec((tk, D), lambda ch, k: (k, 0)),
                  pl.BlockSpec((tk, Fc), lambda ch, k: (jnp.where(ch < 2, k, 0), jnp.minimum(ch, 1))),
                  pl.BlockSpec((tk, Fc), lambda ch, k: (jnp.where(ch >= 2, k, 0), jnp.maximum(ch - 2, 0)))],
        out_specs=pl.BlockSpec((None, D, Fc), lambda ch, k: (ch, 0, 0)),
        out_shape=jax.ShapeDtypeStruct((4, D, Fc), F32),
        scratch_shapes=[pltpu.VMEM((D, Fc), F32)],
        compiler_params=_params(("parallel", "arbitrary")), name=name)(h, da, db)


def grad_rows(a, b, *, scale=None, name):
    S, M = a.shape
    N = b.shape[1]
    R = M // 4
    tn = _tile(N, 512, LANES)
    tk = _tile(S, 512, 16)
    nk = S // tk

    def body(a_ref, b_ref, o_ref, acc_ref):
        k = pl.program_id(1)

        @pl.when(k == 0)
        def _():
            acc_ref[...] = jnp.zeros_like(acc_ref)

        acc_ref[...] += _dot(a_ref[...].astype(BF16), b_ref[...].astype(BF16), 0, 0)

        @pl.when(k == nk - 1)
        def _():
            for d in range(4):
                part = acc_ref[d * R:(d + 1) * R, :]
                o_ref[d] = part if scale is None else part * scale

    return pl.pallas_call(
        body, grid=(N // tn, nk),
        in_specs=[pl.BlockSpec((tk, M), lambda j, k: (k, 0)), pl.BlockSpec((tk, tn), lambda j, k: (k, j))],
        out_specs=pl.BlockSpec((4, R, tn), lambda j, k: (0, 0, j)),
        out_shape=jax.ShapeDtypeStruct((4, R, N), F32),
        scratch_shapes=[pltpu.VMEM((M, tn), F32)],
        compiler_params=_params(("parallel", "arbitrary")), name=name)(a, b)


def rms_fwd(x, g, *, name):
    S, D = x.shape
    ts = _tile(S, 1024, 16)

    def body(x_ref, g_ref, h_ref):
        xv = x_ref[...]
        r = lax.rsqrt(jnp.mean(xv * xv, axis=-1, keepdims=True) + EPS)
        h_ref[...] = (xv * r * g_ref[...]).astype(BF16)

    return pl.pallas_call(
        body, grid=(S // ts,),
        in_specs=[pl.BlockSpec((ts, D), lambda i: (i, 0)), pl.BlockSpec((1, D), lambda i: (0, 0))],
        out_specs=pl.BlockSpec((ts, D), lambda i: (i, 0)),
        out_shape=jax.ShapeDtypeStruct((S, D), BF16),
        compiler_params=_params(("parallel",)), name=name)(x, g.reshape(1, D))


def rms_bwd(x, dh, g, res, *, name):
    S, D = x.shape
    ts = _tile(S, 512, 16)
    has_res = res is not None

    def body(*refs):
        x_ref, dh_ref, g_ref = refs[:3]
        res_ref = refs[3] if has_res else None
        dx_ref, dg_ref = refs[-2:]
        i = pl.program_id(0)
        xv, dhv = x_ref[...], dh_ref[...].astype(F32)
        r = lax.rsqrt(jnp.mean(xv * xv, axis=-1, keepdims=True) + EPS)
        u = dhv * g_ref[...]
        dx = r * u - xv * (r * r * r) * jnp.mean(xv * u, axis=-1, keepdims=True)
        if has_res:
            dx = res_ref[...] + dx
        dx_ref[...] = dx
        part = jnp.sum(dhv * xv * r, axis=0, keepdims=True)

        @pl.when(i == 0)
        def _():
            dg_ref[...] = part

        @pl.when(i > 0)
        def _():
            dg_ref[...] += part

    row = pl.BlockSpec((ts, D), lambda i: (i, 0))
    vec = pl.BlockSpec((1, D), lambda i: (0, 0))
    args = (x, dh, g.reshape(1, D)) + ((res,) if has_res else ())
    return pl.pallas_call(
        body, grid=(S // ts,), in_specs=[row, row, vec] + ([row] if has_res else []),
        out_specs=[row, vec],
        out_shape=[jax.ShapeDtypeStruct((S, D), F32), jax.ShapeDtypeStruct((1, D), F32)],
        compiler_params=_params(("arbitrary",)), name=name)(*args)


def _low_half(shape):
    return lax.broadcasted_iota(jnp.int32, shape, len(shape) - 1) < HEAD_DIM


def _half_sums(x, low):
    sa = jnp.sum(jnp.where(low, x, 0.0), axis=1, keepdims=True)
    sb = jnp.sum(jnp.where(low, 0.0, x), axis=1, keepdims=True)
    return jnp.where(low, sa, sb)


def pairnorm_fwd(x, col0, n_pairs, g, *, scale=None, name):
    S = x.shape[0]
    ts = _tile(S, 1024, 16)

    def body(x_ref, g_ref, o_ref):
        xv = x_ref[...]
        r = lax.rsqrt(_half_sums(xv * xv, _low_half(xv.shape)) * (1.0 / HEAD_DIM) + EPS)
        y = xv * r * g_ref[...]
        o_ref[...] = (y if scale is None else y * scale).astype(BF16)

    return pl.pallas_call(
        body, grid=(S // ts, n_pairs),
        in_specs=[pl.BlockSpec((ts, LANES), lambda i, j: (i, col0 + j)), pl.BlockSpec((1, LANES), lambda i, j: (0, 0))],
        out_specs=pl.BlockSpec((ts, LANES), lambda i, j: (i, j)),
        out_shape=jax.ShapeDtypeStruct((S, n_pairs * LANES), BF16),
        compiler_params=_params(("parallel", "parallel")), name=name)(x, jnp.tile(g.reshape(1, HEAD_DIM), (1, 2)))


def pairnorm_bwd(x, col0, n_pairs, dy, g, *, name):
    S = x.shape[0]
    ts = _tile(S, 1024, 16)

    def body(x_ref, dy_ref, g_ref, dx_ref, dg_ref):
        first = jnp.logical_and(pl.program_id(0) == 0, pl.program_id(1) == 0)
        xv, dyv = x_ref[...], dy_ref[...]
        low = _low_half(xv.shape)
        r = lax.rsqrt(_half_sums(xv * xv, low) * (1.0 / HEAD_DIM) + EPS)
        u = dyv * g_ref[...]
        dx_ref[...] = r * u - xv * (r * r * r) * (_half_sums(xv * u, low) * (1.0 / HEAD_DIM))
        part = jnp.sum(dyv * xv * r, axis=0, keepdims=True)

        @pl.when(first)
        def _():
            dg_ref[...] = part

        @pl.when(jnp.logical_not(first))
        def _():
            dg_ref[...] += part

    vec = pl.BlockSpec((1, LANES), lambda i, j: (0, 0))
    blk = pl.BlockSpec((ts, LANES), lambda i, j: (i, j))
    return pl.pallas_call(
        body, grid=(S // ts, n_pairs),
        in_specs=[pl.BlockSpec((ts, LANES), lambda i, j: (i, col0 + j)), blk, vec], out_specs=[blk, vec],
        out_shape=[jax.ShapeDtypeStruct((S, n_pairs * LANES), F32), jax.ShapeDtypeStruct((1, LANES), F32)],
        compiler_params=_params(("arbitrary", "arbitrary")), name=name)(x, dy, jnp.tile(g.reshape(1, HEAD_DIM), (1, 2)))


def _split3(x):
    x1 = x.astype(BF16)
    r1 = x - x1.astype(F32)
    x2 = r1.astype(BF16)
    x3 = (r1 - x2.astype(F32)).astype(BF16)
    return x1, x2, x3


def _tri_ones(n, lower):
    r = lax.broadcasted_iota(jnp.int32, (n, n), 0)
    c = lax.broadcasted_iota(jnp.int32, (n, n), 1)
    return jnp.where((c <= r) if lower else (c >= r), 1.0, 0.0).astype(BF16)


def fgate_fwd(z, col0, bias, *, name):
    S, L = z.shape[0], LANES
    tb = _tile(S, 256, 16)

    def body(z_ref, b_ref, c_ref, carry):
        i = pl.program_id(0)

        @pl.when(i == 0)
        def _():
            carry[...] = jnp.zeros_like(carry)

        zz = z_ref[...] + b_ref[...]
        lf = jnp.minimum(zz, 0.0) - jnp.log(1.0 + jnp.exp(-jnp.abs(zz)))
        tri = _tri_ones(tb, True)
        x1, x2, x3 = _split3(lf)
        c = (_dot(tri, x1, 1, 0) + _dot(tri, x2, 1, 0)) + _dot(tri, x3, 1, 0) + carry[...]
        c_ref[...] = c
        carry[...] += jnp.sum(lf, axis=0, keepdims=True)

    return pl.pallas_call(
        body, grid=(S // tb,),
        in_specs=[pl.BlockSpec((tb, L), lambda i: (i, col0)), pl.BlockSpec((1, L), lambda i: (0, 0))],
        out_specs=pl.BlockSpec((tb, L), lambda i: (i, 0)),
        out_shape=jax.ShapeDtypeStruct((S, L), F32),
        scratch_shapes=[pltpu.VMEM((1, L), F32)],
        compiler_params=_params(("arbitrary",)), name=name)(z, bias)


def fgate_bwd(z, col0, bias, drs, dcs, *, name):
    S, L = z.shape[0], LANES
    tb = _tile(S, 256, 16)
    nb = S // tb

    def body(z_ref, b_ref, drs_ref, dcs_ref, dz_ref, db_ref, carry):
        i = pl.program_id(0)

        @pl.when(i == 0)
        def _():
            carry[...] = jnp.zeros_like(carry)

        tri = _tri_ones(tb, False)
        dc = drs_ref[...] - dcs_ref[...]
        x1, x2, x3 = _split3(dc)
        dlf = (_dot(tri, x1, 1, 0) + _dot(tri, x2, 1, 0)) + _dot(tri, x3, 1, 0) + carry[...]
        carry[...] += jnp.sum(dc, axis=0, keepdims=True)
        dz = dlf * _sigmoid(-(z_ref[...] + b_ref[...]))
        dz_ref[...] = dz
        part = jnp.sum(dz, axis=0, keepdims=True)

        @pl.when(i == 0)
        def _():
            db_ref[...] = part

        @pl.when(i > 0)
        def _():
            db_ref[...] += part

    rev = pl.BlockSpec((tb, L), lambda i: (nb - 1 - i, 0))
    vec = pl.BlockSpec((1, L), lambda i: (0, 0))
    return pl.pallas_call(
        body, grid=(nb,), in_specs=[pl.BlockSpec((tb, L), lambda i: (nb - 1 - i, col0)), vec, rev, rev],
        out_specs=[rev, vec],
        out_shape=[jax.ShapeDtypeStruct((S, L), F32), jax.ShapeDtypeStruct((1, L), F32)],
        scratch_shapes=[pltpu.VMEM((1, L), F32)],
        compiler_params=_params(("arbitrary",)), name=name)(z, bias, drs, dcs)


def _one_head(x, low, a):
    return jnp.where(low if a == 0 else jnp.logical_not(low), x, jnp.zeros_like(x))


class Rider(NamedTuple):
    inputs: tuple
    out_shapes: tuple
    aliases: dict
    sems: tuple
    plan: Callable


def _with_rider(rider, n_in, n_out, n_scratch):
    if rider is None:
        return [], [], [], [], {}, lambda refs: (refs[:n_in], refs[n_in:n_in + n_out], refs[n_in + n_out:], None)
    e_in, e_out = len(rider.inputs), len(rider.out_shapes)

    def split(refs):
        ins, r_in = refs[:n_in], refs[n_in:n_in + e_in]
        o0 = n_in + e_in
        outs, r_out = refs[o0:o0 + n_out], refs[o0 + n_out:o0 + n_out + e_out]
        s0 = o0 + n_out + e_out
        return ins, outs, refs[s0:s0 + n_scratch], rider.plan(r_in, r_out, refs[s0 + n_scratch:])

    aliases = {n_in + a: n_out + b for a, b in rider.aliases.items()}
    return list(rider.inputs), [_ANY] * e_in, list(rider.out_shapes), [_ANY] * e_out, aliases, split


def attn_fwd(q, q0, k, k0, v, v0, n_pairs, cq, ck, *, causal, rider=None, name):
    Sq, Sk = q.shape[0], k.shape[0]
    tq = _tile(Sq, ATTN_Q_BLOCK, LANES)
    tk = _tile(Sk, ATTN_K_BLOCK, LANES)
    nq, nk = Sq // tq, Sk // tk
    bias = cq is not None
    r_args, r_in_specs, r_shapes, r_out_specs, aliases, split = _with_rider(rider, 5 if bias else 3, 2, 3)

    def body(*refs):
        ins, (o_ref, lse_ref), (m_sc, l_sc, acc_sc), ride = split(refs)
        q_ref, k_ref, v_ref = ins[:3]
        cq_ref, ck_ref = (ins[3], ins[4]) if bias else (None, None)
        pr, i, j = pl.program_id(0), pl.program_id(1), pl.program_id(2)
        if ride is not None:
            pl.when(jnp.logical_and(pr == 0, jnp.logical_and(i == 0, j == 0)))(ride[0])

        @pl.when(j == 0)
        def _():
            m_sc[...] = jnp.full_like(m_sc, NEG)
            l_sc[...] = jnp.zeros_like(l_sc)
            acc_sc[...] = jnp.zeros_like(acc_sc)

        def compute(masked):
            qv, kv, vv = q_ref[...], k_ref[...], v_ref[...].astype(BF16)
            low = _low_half(qv.shape)
            for a in range(2):
                s = _dot(_one_head(qv, low, a), kv, 1, 1)
                if bias:
                    s = s + (cq_ref[a] - ck_ref[a])
                if masked:
                    row = i * tq + lax.broadcasted_iota(jnp.int32, (tq, tk), 0)
                    col = j * tk + lax.broadcasted_iota(jnp.int32, (tq, tk), 1)
                    s = jnp.where(col <= row, s, NEG)
                m_prev = m_sc[a]
                m_new = jnp.maximum(m_prev, jnp.max(s, axis=1, keepdims=True))
                alpha = jnp.exp(m_prev - m_new)
                p = jnp.exp(s - m_new)
                l_sc[a] = alpha * l_sc[a] + jnp.sum(p, axis=1, keepdims=True)
                acc_sc[a] = alpha * acc_sc[a] + _dot(p.astype(BF16), vv, 1, 0)
                m_sc[a] = m_new

        if causal:
            live = j * tk <= i * tq + (tq - 1)
            crosses = j * tk + (tk - 1) > i * tq
            pl.when(jnp.logical_and(live, crosses))(functools.partial(compute, True))
            pl.when(jnp.logical_and(live, jnp.logical_not(crosses)))(functools.partial(compute, False))
        else:
            compute(False)

        @pl.when(j == nk - 1)
        def _():
            low = _low_half((tq, LANES))
            o_ref[...] = jnp.where(low, acc_sc[0] / l_sc[0], acc_sc[1] / l_sc[1])
            for a in range(2):
                lse_ref[a] = m_sc[a] + jnp.log(l_sc[a])

        if ride is not None:
            pl.when(jnp.logical_and(pr == n_pairs - 1, jnp.logical_and(i == nq - 1, j == nk - 1)))(ride[1])

    def kv_blk(i, j):
        return jnp.minimum(j, (i * tq + tq - 1) // tk) if causal else j

    in_specs = [pl.BlockSpec((tq, LANES), lambda p, i, j: (i, q0 + p)),
                pl.BlockSpec((tk, LANES), lambda p, i, j: (kv_blk(i, j), k0 + p)),
                pl.BlockSpec((tk, LANES), lambda p, i, j: (kv_blk(i, j), v0 + p))]
    args = [q, k, v]
    if bias:
        in_specs += [pl.BlockSpec((None, 2, tq, 1), lambda p, i, j: (p, 0, i, 0)),
                     pl.BlockSpec((None, 2, 1, tk), lambda p, i, j: (p, 0, 0, kv_blk(i, j)))]
        args += [cq, ck]
    out = pl.pallas_call(
        body, grid=(n_pairs, nq, nk), in_specs=in_specs + r_in_specs,
        out_specs=[pl.BlockSpec((tq, LANES), lambda p, i, j: (i, p)),
                   pl.BlockSpec((None, 2, tq, 1), lambda p, i, j: (p, 0, i, 0))] + r_out_specs,
        out_shape=[jax.ShapeDtypeStruct((Sq, n_pairs * LANES), F32),
                   jax.ShapeDtypeStruct((n_pairs, 2, Sq, 1), F32)] + r_shapes,
        scratch_shapes=[pltpu.VMEM((2, tq, 1), F32), pltpu.VMEM((2, tq, 1), F32), pltpu.VMEM((2, tq, LANES), F32)]
        + (list(rider.sems) if rider else []),
        input_output_aliases=aliases,
        compiler_params=_params(("arbitrary",) * 3 if rider else ("parallel", "parallel", "arbitrary")),
        name=name)(*args, *r_args)
    return out[0], out[1], out[2:]


def attn_delta(o, do, do0, n_pairs, *, name):
    S = o.shape[0]
    ts = _tile(S, 1024, 16)

    def body(o_ref, do_ref, out_ref):
        prod = o_ref[...] * do_ref[...]
        low = _low_half(prod.shape)
        out_ref[0] = jnp.sum(jnp.where(low, prod, 0.0), axis=1, keepdims=True)
        out_ref[1] = jnp.sum(jnp.where(low, 0.0, prod), axis=1, keepdims=True)

    return pl.pallas_call(
        body, grid=(n_pairs, S // ts),
        in_specs=[pl.BlockSpec((ts, LANES), lambda p, i: (i, p)), pl.BlockSpec((ts, LANES), lambda p, i: (i, do0 + p))],
        out_specs=pl.BlockSpec((None, 2, ts, 1), lambda p, i: (p, 0, i, 0)),
        out_shape=jax.ShapeDtypeStruct((n_pairs, 2, S, 1), F32),
        compiler_params=_params(("parallel", "parallel")), name=name)(o, do)


def attn_bwd(q, q0, k, k0, v, v0, do, do0, n_pairs, lse, delta, cq, ck, *, causal, rider=None, name):
    Sq, Sk = q.shape[0], k.shape[0]
    tq = _tile(Sq, ATTN_Q_BLOCK, LANES)
    tk = _tile(Sk, ATTN_K_BLOCK, LANES)
    nq, nk = Sq // tq, Sk // tk
    bias = cq is not None

    r_args, r_in_specs, r_shapes, r_out_specs, aliases, split = _with_rider(
        rider, 8 if bias else 6, 5 if bias else 3, 0)

    def body(*refs):
        ins, outs, _, ride = split(refs)
        q_ref, k_ref, v_ref, do_ref, lse_ref, dl_ref = ins[:6]
        cq_ref, ck_ref = (ins[6], ins[7]) if bias else (None, None)
        dq_ref, dk_ref, dv_ref = outs[:3]
        dcs_ref, drs_ref = (outs[3], outs[4]) if bias else (None, None)
        pr, j, i = pl.program_id(0), pl.program_id(1), pl.program_id(2)
        if ride is not None:
            pl.when(jnp.logical_and(pr == 0, jnp.logical_and(i == 0, j == 0)))(ride[0])

        @pl.when(i == 0)
        def _():
            dk_ref[...] = jnp.zeros_like(dk_ref)
            dv_ref[...] = jnp.zeros_like(dv_ref)
            if bias:
                dcs_ref[...] = jnp.zeros_like(dcs_ref)

        rows = pl.ds(pl.multiple_of(i * tq, tq), tq)

        def compute(masked):
            qv, kv, vv, dov = q_ref[...], k_ref[...], v_ref[...].astype(BF16), do_ref[...].astype(BF16)
            low = _low_half(qv.shape)
            dq_part, row_parts = None, []
            for a in range(2):
                qa, ka, doa = _one_head(qv, low, a), _one_head(kv, _low_half(kv.shape), a), _one_head(dov, low, a)
                s = _dot(qa, kv, 1, 1)
                if bias:
                    s = s + (cq_ref[a] - ck_ref[a])
                p = jnp.exp(s - lse_ref[a])
                if masked:
                    row = i * tq + lax.broadcasted_iota(jnp.int32, (tq, tk), 0)
                    col = j * tk + lax.broadcasted_iota(jnp.int32, (tq, tk), 1)
                    p = jnp.where(col <= row, p, 0.0)
                dv_ref[...] += _dot(p.astype(BF16), doa, 0, 0)
                dp = _dot(doa, vv, 1, 1)
                ds = p * (dp - dl_ref[a])
                dsb = ds.astype(BF16)
                dk_ref[...] += _dot(dsb, qa, 0, 0)
                if bias:
                    dcs_ref[a] += jnp.sum(ds, axis=0, keepdims=True)
                    row_parts.append(jnp.sum(ds, axis=1, keepdims=True))
                part = _dot(dsb, ka, 1, 0) * QK_SCALE
                dq_part = part if dq_part is None else dq_part + part

            @pl.when(j == 0)
            def _():
                dq_ref[rows, :] = dq_part
                for a, rp in enumerate(row_parts):
                    drs_ref[a, rows, :] = rp

            @pl.when(j > 0)
            def _():
                dq_ref[rows, :] += dq_part
                for a, rp in enumerate(row_parts):
                    drs_ref[a, rows, :] += rp

        if causal:
            live = j * tk <= i * tq + (tq - 1)
            crosses = j * tk + (tk - 1) > i * tq
            pl.when(jnp.logical_and(live, crosses))(functools.partial(compute, True))
            pl.when(jnp.logical_and(live, jnp.logical_not(crosses)))(functools.partial(compute, False))
        else:
            compute(False)

        if ride is not None:
            pl.when(jnp.logical_and(pr == n_pairs - 1, jnp.logical_and(i == nq - 1, j == nk - 1)))(ride[1])

    def q_blk(j, i):
        return jnp.maximum(i, (j * tk) // tq) if causal else i

    col1 = pl.BlockSpec((None, 2, tq, 1), lambda p, j, i: (p, 0, q_blk(j, i), 0))
    in_specs = [pl.BlockSpec((tq, LANES), lambda p, j, i: (q_blk(j, i), q0 + p)),
                pl.BlockSpec((tk, LANES), lambda p, j, i: (j, k0 + p)),
                pl.BlockSpec((tk, LANES), lambda p, j, i: (j, v0 + p)),
                pl.BlockSpec((tq, LANES), lambda p, j, i: (q_blk(j, i), do0 + p)), col1, col1]
    args = [q, k, v, do, lse, delta]
    kout = pl.BlockSpec((tk, LANES), lambda p, j, i: (j, p))
    out_specs = [pl.BlockSpec((Sq, LANES), lambda p, j, i: (0, p)), kout, kout]
    out_shape = [jax.ShapeDtypeStruct((Sq, n_pairs * LANES), F32), jax.ShapeDtypeStruct((Sk, n_pairs * LANES), F32),
                 jax.ShapeDtypeStruct((Sk, n_pairs * LANES), F32)]
    if bias:
        in_specs += [col1, pl.BlockSpec((None, 2, 1, tk), lambda p, j, i: (p, 0, 0, j))]
        args += [cq, ck]
        out_specs += [pl.BlockSpec((None, 2, 1, tk), lambda p, j, i: (p, 0, 0, j)),
                      pl.BlockSpec((None, 2, Sq, 1), lambda p, j, i: (p, 0, 0, 0))]
        out_shape += [jax.ShapeDtypeStruct((n_pairs, 2, 1, Sk), F32), jax.ShapeDtypeStruct((n_pairs, 2, Sq, 1), F32)]
    n_own = len(out_shape)
    out = pl.pallas_call(
        body, grid=(n_pairs, nk, nq), in_specs=in_specs + r_in_specs, out_specs=out_specs + r_out_specs,
        out_shape=out_shape + r_shapes, scratch_shapes=list(rider.sems) if rider else [],
        input_output_aliases=aliases,
        compiler_params=_params(("arbitrary",) * 3 if rider else ("parallel", "arbitrary", "arbitrary")),
        name=name)(*args, *r_args)
    return tuple(out[:n_own]), out[n_own:]


def _tril_mask(n):
    r = lax.broadcasted_iota(jnp.int32, (n, n), 0)
    c = lax.broadcasted_iota(jnp.int32, (n, n), 1)
    return c <= r


def _gmlp_operands(v_gain, w_s, b_s):
    G = w_s.shape[0]
    return (v_gain.reshape(G // 2, 1, LANES), w_s.reshape(G // 2, 2, CHUNK, CHUNK), b_s.reshape(G // 2, 2, CHUNK, 1))


def _gmlp_gate(wt, vh, b_ref, low):
    gate = _dot(wt[0], _one_head(vh, low, 0), 1, 0) + _dot(wt[1], _one_head(vh, low, 1), 1, 0)
    return gate + jnp.where(low, b_ref[0], b_ref[1])


def gmlp_fwd(proj, v0, n_pairs, vg, w, b, *, name):
    S = proj.shape[0]
    ts = _tile(S, 1024, CHUNK)

    def body(up_ref, vp_ref, vg_ref, w_ref, b_ref, o_ref):
        mask = _tril_mask(CHUNK)
        wt = [jnp.where(mask, w_ref[a], 0.0).astype(BF16) for a in range(2)]
        low = _low_half((CHUNK, LANES))
        for c in range(ts // CHUNK):
            sl = pl.ds(c * CHUNK, CHUNK)
            vz = _gelu(vp_ref[sl, :])
            r = lax.rsqrt(_half_sums(vz * vz, low) * (1.0 / HEAD_DIM) + EPS)
            vh = (vz * r * vg_ref[...]).astype(BF16)
            o_ref[sl, :] = _gelu(up_ref[sl, :]) * _gmlp_gate(wt, vh, b_ref, low)

    return pl.pallas_call(
        body, grid=(n_pairs, S // ts),
        in_specs=[pl.BlockSpec((ts, LANES), lambda p, i: (i, p)), pl.BlockSpec((ts, LANES), lambda p, i: (i, v0 + p)),
                  pl.BlockSpec((None, 1, LANES), lambda p, i: (p, 0, 0)),
                  pl.BlockSpec((None, 2, CHUNK, CHUNK), lambda p, i: (p, 0, 0, 0)),
                  pl.BlockSpec((None, 2, CHUNK, 1), lambda p, i: (p, 0, 0, 0))],
        out_specs=pl.BlockSpec((ts, LANES), lambda p, i: (i, p)),
        out_shape=jax.ShapeDtypeStruct((S, n_pairs * LANES), F32),
        compiler_params=_params(("parallel", "parallel")), name=name)(proj, proj, vg, w, b)


def gmlp_bwd(proj, v0, n_pairs, vg, w, wT, b, do, *, name):
    S = proj.shape[0]
    ts = _tile(S, 1024, CHUNK)

    def body(up_ref, vp_ref, vg_ref, w_ref, wT_ref, b_ref, do_ref, dup_ref, dvp_ref, dw_ref, db_ref, dvg_ref):
        i = pl.program_id(1)

        @pl.when(i == 0)
        def _():
            dw_ref[...] = jnp.zeros_like(dw_ref)
            db_ref[...] = jnp.zeros_like(db_ref)
            dvg_ref[...] = jnp.zeros_like(dvg_ref)

        mask = _tril_mask(CHUNK)
        wt = [jnp.where(mask, w_ref[a], 0.0).astype(BF16) for a in range(2)]
        wtT = [jnp.where(mask.T, wT_ref[a], 0.0).astype(BF16) for a in range(2)]
        low = _low_half((CHUNK, LANES))
        vgain = vg_ref[...]
        for c in range(ts // CHUNK):
            sl = pl.ds(c * CHUNK, CHUNK)
            u_pre, v_pre, dout = up_ref[sl, :], vp_ref[sl, :], do_ref[sl, :]
            vz = _gelu(v_pre)
            r = lax.rsqrt(_half_sums(vz * vz, low) * (1.0 / HEAD_DIM) + EPS)
            vh = (vz * r * vgain).astype(BF16)
            gate = _gmlp_gate(wt, vh, b_ref, low)
            dgate = dout * _gelu(u_pre)
            dup_ref[sl, :] = dout * gate * _gelu_grad(u_pre)
            dvh = None
            for a in range(2):
                dga = _one_head(dgate, low, a)
                dgb = dga.astype(BF16)
                dw_ref[a] += jnp.where(mask, _dot(dgb, vh, 1, 1), 0.0)
                db_ref[a] += jnp.sum(dga, axis=1, keepdims=True)
                part = _dot(wtT[a], dgb, 1, 0)
                dvh = part if dvh is None else dvh + part
            dvg_ref[...] += jnp.sum(dvh * vz * r, axis=0, keepdims=True)
            t = dvh * vgain
            dvz = r * t - vz * (r * r * r) * (_half_sums(vz * t, low) * (1.0 / HEAD_DIM))
            dvp_ref[sl, :] = dvz * _gelu_grad(v_pre)

    ublk = pl.BlockSpec((ts, LANES), lambda p, i: (i, p))
    wblk = pl.BlockSpec((None, 2, CHUNK, CHUNK), lambda p, i: (p, 0, 0, 0))
    bblk = pl.BlockSpec((None, 2, CHUNK, 1), lambda p, i: (p, 0, 0, 0))
    gblk = pl.BlockSpec((None, 1, LANES), lambda p, i: (p, 0, 0))
    return pl.pallas_call(
        body, grid=(n_pairs, S // ts),
        in_specs=[ublk, pl.BlockSpec((ts, LANES), lambda p, i: (i, v0 + p)), gblk, wblk, wblk, bblk, ublk],
        out_specs=[ublk, ublk, wblk, bblk, gblk],
        out_shape=[jax.ShapeDtypeStruct((S, n_pairs * LANES), F32), jax.ShapeDtypeStruct((S, n_pairs * LANES), F32),
                   jax.ShapeDtypeStruct((n_pairs, 2, CHUNK, CHUNK), F32), jax.ShapeDtypeStruct((n_pairs, 2, CHUNK, 1), F32),
                   jax.ShapeDtypeStruct((n_pairs, 1, LANES), F32)],
        compiler_params=_params(("parallel", "arbitrary")), name=name)(proj, proj, vg, w, wT, b, do)


def loss_head(y, target, *, name):
    S, D = y.shape
    ts = _tile(S, 512, 8)

    def body(y_ref, t_ref, dy_ref, loss_ref):
        i = pl.program_id(0)
        e = y_ref[...] - t_ref[...]
        dy_ref[...] = e * (1.0 / D)
        part = jnp.sum(jnp.sum(e * e, axis=1, keepdims=True), axis=0, keepdims=True) * (0.5 / D)

        @pl.when(i == 0)
        def _():
            loss_ref[...] = part

        @pl.when(i > 0)
        def _():
            loss_ref[...] += part

    row = pl.BlockSpec((ts, D), lambda i: (i, 0))
    return pl.pallas_call(
        body, grid=(S // ts,), in_specs=[row, row],
        out_specs=[row, pl.BlockSpec((1, 1), lambda i: (0, 0))],
        out_shape=[jax.ShapeDtypeStruct((S, D), F32), jax.ShapeDtypeStruct((1, 1), F32)],
        compiler_params=_params(("arbitrary",)), name=name)(y, target)


def adamw(w, g, m, v, *, name):
    shape = w.shape
    C = shape[-1]
    R = w.size // C
    tr = _tile(R, max(8, (256 * 1024) // C // 8 * 8), 8)

    def body(w_ref, g_ref, m_ref, v_ref, d_ref, nm_ref, nv_ref):
        gv = g_ref[...]
        nm = ADAM_B1 * m_ref[...] + (1.0 - ADAM_B1) * gv
        nv = ADAM_B2 * v_ref[...] + (1.0 - ADAM_B2) * (gv * gv)
        m_hat = nm / (1.0 - ADAM_B1 ** ADAM_STEP)
        v_hat = nv / (1.0 - ADAM_B2 ** ADAM_STEP)
        d_ref[...] = -ADAM_LR * (m_hat / (jnp.sqrt(v_hat) + ADAM_EPS) + ADAM_WD * w_ref[...])
        nm_ref[...] = nm
        nv_ref[...] = nv

    blk = pl.BlockSpec((tr, C), lambda i: (i, 0))
    out = pl.pallas_call(
        body, grid=(R // tr,), in_specs=[blk] * 4, out_specs=[blk] * 3,
        out_shape=[jax.ShapeDtypeStruct((R, C), F32)] * 3,
        compiler_params=_params(("parallel",)), name=name)(*(a.reshape(R, C) for a in (w, g, m, v)))
    return tuple(o.reshape(shape) for o in out)


def pair_sum(p, landed, half, *, name):
    n, R, C = landed.shape
    tr = _tile(R, 256, 16)
    nr = R // tr

    def body(half_ref, p_ref, l_ref, o_ref):
        o_ref[...] = (p_ref[...] + l_ref[...]).astype(BF16)

    return pl.pallas_call(
        body,
        grid_spec=pltpu.PrefetchScalarGridSpec(
            num_scalar_prefetch=1, grid=(n, nr),
            in_specs=[pl.BlockSpec((None, tr, C), lambda k, r, half_ref: (k, half_ref[0] * nr + r, 0)),
                      pl.BlockSpec((None, tr, C), lambda k, r, half_ref: (k, r, 0))],
            out_specs=pl.BlockSpec((None, tr, C), lambda k, r, half_ref: (k, r, 0))),
        out_shape=jax.ShapeDtypeStruct((n, R, C), BF16),
        compiler_params=_params(("parallel", "parallel")), name=name)(half, p, landed)


def chip_sum(own, landed, chip, *, name):
    n, R, C = own.shape
    tr = _tile(R, 256, 16)

    def body(chip_ref, own_ref, *rest):
        l_refs, o_ref = rest[:n], rest[n]
        me = chip_ref[0]
        acc = None
        for d in range(n):
            term = jnp.where(me == d, own_ref[...], l_refs[d][...]).astype(F32)
            acc = term if acc is None else acc + term
        o_ref[...] = acc

    def landed_spec(d):
        return pl.BlockSpec((None, tr, C), lambda r, chip_ref: (jnp.where(chip_ref[0] == d, (d + 1) % n, d), r, 0))

    return pl.pallas_call(
        body,
        grid_spec=pltpu.PrefetchScalarGridSpec(
            num_scalar_prefetch=1, grid=(R // tr,),
            in_specs=[pl.BlockSpec((None, tr, C), lambda r, chip_ref: (chip_ref[0], r, 0))]
            + [landed_spec(d) for d in range(n)],
            out_specs=pl.BlockSpec((tr, C), lambda r, chip_ref: (r, 0))),
        out_shape=jax.ShapeDtypeStruct((R, C), F32),
        compiler_params=_params(("parallel",)), name=name)(chip, own, *([landed] * n))


def ordered_sum(parts, *, name):
    n, R, C = parts.shape
    tr = _tile(R, 256, 16)

    def body(p_ref, o_ref):
        acc = p_ref[0].astype(F32)
        for d in range(1, n):
            acc = acc + p_ref[d].astype(F32)
        o_ref[...] = acc

    return pl.pallas_call(
        body, grid=(R // tr,), in_specs=[pl.BlockSpec((n, tr, C), lambda r: (0, r, 0))],
        out_specs=pl.BlockSpec((tr, C), lambda r: (r, 0)),
        out_shape=jax.ShapeDtypeStruct((R, C), F32),
        compiler_params=_params(("parallel",)), name=name)(parts)


_ANY = pl.BlockSpec(memory_space=pl.ANY)


def _position():
    return lax.axis_index("x"), lax.axis_index("y"), lax.axis_index("c")


def _remote(src, dst, send_sem, recv_sem, device):
    return pltpu.make_async_remote_copy(src_ref=src, dst_ref=dst, send_sem=send_sem, recv_sem=recv_sem,
                                        device_id=device, device_id_type=MESH_ID)


def _small_all_gather(s_ref, all_ref, send_sems, recv_sems, x, y, c):
    me = 4 * x + 2 * y + c
    copies = []
    for f in range(1, 8):
        peer = ((1 - x) if f & 4 else x, (1 - y) if f & 2 else y, (1 - c) if f & 1 else c)
        cp = _remote(s_ref, all_ref.at[me], send_sems.at[f - 1], recv_sems.at[f - 1], peer)
        cp.start()
        copies.append((cp, peer, f - 1))

    def finish():
        for cp, peer, s in copies:
            slot = all_ref.at[4 * peer[0] + 2 * peer[1] + peer[2]]
            _remote(slot, slot, send_sems.at[s], recv_sems.at[s], peer).wait_recv()
        for cp, _, _ in copies:
            cp.wait_send()

    return finish


def _core_rows(ref, core):
    h = ref.shape[1] // 2
    return pl.ds(core * h, h)


def _gather_plan(outs, send_sems, recv_sems):
    n = len(outs)
    x, y, c = _position()
    k = 2 * x + y
    sibling = (x, y, 1 - c)
    chips = [(1 - x, y), (x, 1 - y), (1 - x, 1 - y)]

    def first():
        return [_remote(outs[w].at[k, _core_rows(outs[w], c)], outs[w].at[k, _core_rows(outs[w], c)],
                        send_sems.at[w, j], recv_sems.at[w, j], (px, py, c))
                for j, (px, py) in enumerate(chips) for w in range(n)]

    def start():
        for cp in first():
            cp.start()

    def finish():
        passed = []
        for j, (px, py) in enumerate(chips):
            for w in range(n):
                slot = outs[w].at[2 * px + py, _core_rows(outs[w], c)]
                _remote(slot, slot, send_sems.at[w, j], recv_sems.at[w, j], (px, py, c)).wait_recv()
                cp = _remote(slot, slot, send_sems.at[w, 3 + j], recv_sems.at[w, 3 + j], sibling)
                cp.start()
                passed.append(cp)
        for j, (px, py) in enumerate(chips):
            for w in range(n):
                slot = outs[w].at[2 * px + py, _core_rows(outs[w], 1 - c)]
                _remote(slot, slot, send_sems.at[w, 3 + j], recv_sems.at[w, 3 + j], sibling).wait_recv()
        for cp in first() + passed:
            cp.wait_send()

    return start, finish


def _gather_sems(n):
    return (pltpu.SemaphoreType.DMA((n, 6)), pltpu.SemaphoreType.DMA((n, 6)))


def gather_rider(slabs):
    return Rider(tuple(slabs), tuple(jax.ShapeDtypeStruct(a.shape, a.dtype) for a in slabs),
                 {i: i for i in range(len(slabs))}, _gather_sems(len(slabs)),
                 lambda ins, outs, sems: _gather_plan(outs, sems[0], sems[1]))


def gather_weights(slabs, small_slab, *, name):
    n = len(slabs)

    def body(*refs):
        outs, all_ref = refs[n + 1:2 * n + 1], refs[2 * n + 1]
        send_sems, recv_sems, s_send, s_recv = refs[2 * n + 2:]
        x, y, c = _position()
        finish_small = _small_all_gather(all_ref.at[4 * x + 2 * y + c], all_ref, s_send, s_recv, x, y, c)
        start, finish = _gather_plan(outs, send_sems, recv_sems)
        start()
        finish()
        finish_small()

    args = list(slabs) + [small_slab]
    out = pl.pallas_call(
        body, in_specs=[_ANY] * (n + 1), out_specs=[_ANY] * (n + 1),
        out_shape=[jax.ShapeDtypeStruct(a.shape, a.dtype) for a in args],
        input_output_aliases={i: i for i in range(n + 1)},
        scratch_shapes=list(_gather_sems(n)) + [pltpu.SemaphoreType.DMA((7,)), pltpu.SemaphoreType.DMA((7,))],
        name=name)(*args)
    return out[:n], out[n]


def exchange_with_sibling(parts, small_slab, *, name):
    n = len(parts)
    has_small = small_slab is not None
    n_arg = n + (1 if has_small else 0)

    def body(*refs):
        p_refs = refs[:n]
        lands = refs[n_arg:n_arg + n]
        send_sems, recv_sems = refs[2 * n_arg], refs[2 * n_arg + 1]
        x, y, c = _position()
        sibling = (x, y, 1 - c)
        if has_small:
            all_ref = refs[n_arg + n]
            finish_small = _small_all_gather(all_ref.at[4 * x + 2 * y + c], all_ref, refs[2 * n_arg + 2],
                                             refs[2 * n_arg + 3], x, y, c)
        sends = []
        for w in range(n):
            for d in range(4):
                cp = _remote(p_refs[w].at[d, _core_rows(p_refs[w], 1 - c)], lands[w].at[d],
                             send_sems.at[w, d], recv_sems.at[w, d], sibling)
                cp.start()
                sends.append(cp)
        for cp in sends:
            cp.wait_recv()
        for cp in sends:
            cp.wait_send()
        if has_small:
            finish_small()

    small_args = [small_slab] if has_small else []
    out = pl.pallas_call(
        body, in_specs=[_ANY] * n_arg, out_specs=[_ANY] * n_arg,
        out_shape=[jax.ShapeDtypeStruct((4, p.shape[1] // 2, p.shape[2]), p.dtype) for p in parts]
        + [jax.ShapeDtypeStruct(s.shape, s.dtype) for s in small_args],
        input_output_aliases={n: n} if has_small else {},
        scratch_shapes=[pltpu.SemaphoreType.DMA((n, 4)), pltpu.SemaphoreType.DMA((n, 4))]
        + ([pltpu.SemaphoreType.DMA((7,)), pltpu.SemaphoreType.DMA((7,))] if has_small else []),
        name=name)(*parts, *small_args)
    return out[:n], (out[n] if has_small else None)


def _scatter_plan(q_refs, outs, send_sems, recv_sems):
    n = len(q_refs)
    x, y, c = _position()
    k = 2 * x + y
    chips = [(1 - x, y), (x, 1 - y), (1 - x, 1 - y)]

    def sends():
        return [_remote(q_refs[w].at[2 * px + py], outs[w].at[k], send_sems.at[w, j], recv_sems.at[w, j], (px, py, c))
                for j, (px, py) in enumerate(chips) for w in range(n)]

    def start():
        for cp in sends():
            cp.start()

    def finish():
        for j, (px, py) in enumerate(chips):
            for w in range(n):
                slot = outs[w].at[2 * px + py]
                _remote(slot, slot, send_sems.at[w, j], recv_sems.at[w, j], (px, py, c)).wait_recv()
        for cp in sends():
            cp.wait_send()

    return start, finish


def _scatter_sems(n):
    return (pltpu.SemaphoreType.DMA((n, 3)), pltpu.SemaphoreType.DMA((n, 3)))


def scatter_rider(parts):
    return Rider(tuple(parts), tuple(jax.ShapeDtypeStruct(q.shape, q.dtype) for q in parts), {},
                 _scatter_sems(len(parts)), lambda ins, outs, sems: _scatter_plan(ins, outs, sems[0], sems[1]))


def scatter_to_chips(parts, *, name):
    n = len(parts)

    def body(*refs):
        start, finish = _scatter_plan(refs[:n], refs[n:2 * n], refs[2 * n], refs[2 * n + 1])
        start()
        finish()

    return pl.pallas_call(
        body, in_specs=[_ANY] * n, out_specs=[_ANY] * n,
        out_shape=[jax.ShapeDtypeStruct(q.shape, q.dtype) for q in parts],
        scratch_shapes=list(_scatter_sems(n)), name=name)(*parts)


def share_with_sibling(parts, *, name):
    n = len(parts)

    def body(*refs):
        r_refs, outs = refs[:n], refs[n:2 * n]
        send_sems, recv_sems = refs[2 * n:]
        x, y, c = _position()
        sends = []
        for w in range(n):
            cp = _remote(r_refs[w], outs[w], send_sems.at[w], recv_sems.at[w], (x, y, 1 - c))
            cp.start()
            sends.append(cp)
        for cp in sends:
            cp.wait_recv()
        for cp in sends:
            cp.wait_send()

    return pl.pallas_call(
        body, in_specs=[_ANY] * n, out_specs=[_ANY] * n,
        out_shape=[jax.ShapeDtypeStruct(r.shape, r.dtype) for r in parts],
        scratch_shapes=[pltpu.SemaphoreType.DMA((n,)), pltpu.SemaphoreType.DMA((n,))],
        name=name)(*parts)


def _cols_to_chips(full):
    *lead, R, C4 = full.shape
    t = full.reshape(*lead, R, 4, C4 // 4)
    return jnp.moveaxis(t, -2, 0)


def _chips_to_cols(sh):
    t = jnp.moveaxis(sh, 0, -2)
    return t.reshape(*t.shape[:-2], t.shape[-2] * t.shape[-1])


def _slot_in_empty(own, index, n):
    return lax.dynamic_update_slice(lax.empty((n,) + own.shape, own.dtype), own[None], (index,) + (0,) * own.ndim)


def _fold_pair(dg):
    return dg[0, :HEAD_DIM] + dg[0, HEAD_DIM:]


def _ffn_fwd(x, g, w_in_slab, w_out, tag):
    h = rms_fwd(x, g, name=f"{tag}_rms")
    a, b, act = swiglu_fwd(h, w_in_slab, name=f"{tag}_in")
    y = matmul(act, w_out, res=x, scale=0.5, tm=1024, tn=512, tk=w_out.shape[0], name=f"{tag}_out")
    return y, (x, h, a, b, act)


def _ffn_bwd(dy, saved, g, w_in_slab, w_out, tag):
    x, h, a, b, act = saved
    da, db = swiglu_bwd(dy, w_out, a, b, name=f"{tag}_dact")
    dw_out = grad_rows(act, dy, scale=0.5, name=f"{tag}_dwout")
    dw_in = grad_cols(h, da, db, name=f"{tag}_dwin")
    dh = ffn_dh(da, db, w_in_slab, name=f"{tag}_dh")
    dx, dg = rms_bwd(x, dh, g, dy, name=f"{tag}_drms")
    return dx, dg[0], dw_in, dw_out


MEM_PAIRS = MEM_WIDTH // LANES


def _mem_attn_fwd(proj, mq0, mem_n, w_kv, g_q, g_k, tag):
    qh = pairnorm_fwd(proj, mq0, MEM_PAIRS, g_q, scale=QK_SCALE, name=f"{tag}_qnorm")
    kv = matmul(mem_n, w_kv, tm=256, tn=512, tk=1024, name=f"{tag}_kv")
    kh = pairnorm_fwd(kv, 0, MEM_PAIRS, g_k, name=f"{tag}_knorm")
    o, lse, _ = attn_fwd(qh, 0, kh, 0, kv, MEM_PAIRS, MEM_PAIRS, None, None, causal=False, name=f"{tag}_attn")
    return o, (qh, kv, kh, o, lse)


def _mem_attn_bwd(dmix, do0, proj, mq0, saved, mem_n, g_q, g_k, tag):
    qh, kv, kh, o, lse = saved
    delta = attn_delta(o, dmix, do0, MEM_PAIRS, name=f"{tag}_delta")
    (dqh, dkh, dv), _ = attn_bwd(qh, 0, kh, 0, kv, MEM_PAIRS, dmix, do0, MEM_PAIRS, lse, delta, None, None,
                                 causal=False, name=f"{tag}_dattn")
    dq_pre, dgq = pairnorm_bwd(proj, mq0, MEM_PAIRS, dqh, g_q, name=f"{tag}_dqnorm")
    dk_pre, dgk = pairnorm_bwd(kv, 0, MEM_PAIRS, dkh, g_k, name=f"{tag}_dknorm")
    dkv = jnp.concatenate([dk_pre, dv], axis=1)
    dw_kv = grad_rows(mem_n, dkv, name=f"{tag}_dwkv")
    return dq_pre, _fold_pair(dgq), _fold_pair(dgk), dw_kv, dkv


def _decay_terms(c, H):
    S = c.shape[0]
    ch = c[:, :H].T
    return ch.reshape(H // 2, 2, S, 1), ch.reshape(H // 2, 2, 1, S)


def _per_head_lanes(x, H):
    return jnp.pad(x.reshape(H, -1).T, ((0, 0), (0, LANES - H)))


def _fox_fwd(proj, b_f, g_q, g_k, tok, rider, tag):
    H, P = tok // HEAD_DIM, tok // LANES
    bias = jnp.pad(b_f.reshape(1, H), ((0, 0), (0, LANES - H)))
    qh = pairnorm_fwd(proj, 0, P, g_q, scale=QK_SCALE, name=f"{tag}_qnorm")
    kh = pairnorm_fwd(proj, P, P, g_k, name=f"{tag}_knorm")
    c = fgate_fwd(proj, 3 * P + MEM_PAIRS, bias, name=f"{tag}_fgate")
    cq, ck = _decay_terms(c, H)
    o, lse, rode = attn_fwd(qh, 0, kh, 0, proj, 2 * P, P, cq, ck, causal=True, rider=rider, name=f"{tag}_attn")
    return o, (qh, kh, bias, cq, ck, o, lse), rode


def _fox_bwd(dmix, proj, saved, g_q, g_k, tok, rider, tag):
    qh, kh, bias, cq, ck, o, lse = saved
    H, P = tok // HEAD_DIM, tok // LANES
    delta = attn_delta(o, dmix, 0, P, name=f"{tag}_delta")
    (dqh, dkh, dv, dcs, drs), rode = attn_bwd(qh, 0, kh, 0, proj, 2 * P, dmix, 0, P, lse, delta, cq, ck, causal=True,
                                              rider=rider, name=f"{tag}_dattn")
    dq_pre, dgq = pairnorm_bwd(proj, 0, P, dqh, g_q, name=f"{tag}_dqnorm")
    dk_pre, dgk = pairnorm_bwd(proj, P, P, dkh, g_k, name=f"{tag}_dknorm")
    dz, dbias = fgate_bwd(proj, 3 * P + MEM_PAIRS, bias, _per_head_lanes(drs, H), _per_head_lanes(dcs, H),
                          name=f"{tag}_dfgate")
    dqkv = jnp.concatenate([dq_pre, dk_pre, dv], axis=1)
    return dqkv, dz, dbias[0, :H], _fold_pair(dgq), _fold_pair(dgk), rode


def local_step(x, mem, target, W, comm=None):
    S, D = x.shape
    tok = D - MEM_WIDTH
    P = tok // LANES
    depth = W["norm_ffn1"].shape[0]
    mem_n = rms_fwd(mem, W["mem_norm"], name="mem_rms")
    saved = []
    for i in range(depth):
        kind, j = i % 2, i // 2
        t = f"l{i}"
        x1, s1 = _ffn_fwd(x, W["norm_ffn1"][i], W["ffn1_w_in"][i], W["ffn1_w_out"][i], f"{t}_ffn1")
        h = rms_fwd(x1, W["norm_mix"][i], name=f"{t}_mix_rms")
        w_mix = W["fox_w_in"][j] if kind == 0 else W["gmlp_w_in"][j]
        proj = matmul(h, w_mix, tm=1024, tn=896, tk=D, name=f"{t}_mix_in")
        if kind == 0:
            rider = comm.late_weights_rider() if (comm is not None and i == 0) else None
            o_tok, s_tok, rode = _fox_fwd(proj, W["fox_b_f"][j], W["fox_q_norm"][j], W["fox_k_norm"][j], tok, rider,
                                          f"{t}_fox")
            if rider is not None:
                comm.accept_late_weights(W, rode)
            mq0 = 3 * P
        else:
            vg, ws, bs = _gmlp_operands(W["gmlp_v_norm"][j], W["gmlp_w_s"][j], W["gmlp_b_s"][j])
            o_tok = gmlp_fwd(proj, P, P, vg, ws, bs, name=f"{t}_gmlp")
            s_tok = None
            mq0 = 2 * P
        o_mem, s_mem = _mem_attn_fwd(proj, mq0, mem_n, W["mem_w_kv"][i], W["mem_q_norm"][i], W["mem_k_norm"][i],
                                     f"{t}_mem")
        mix = jnp.concatenate([o_tok, o_mem], axis=1).astype(BF16)
        x2 = matmul(mix, W["w_out"][i], res=x1, tm=1024, tn=512, tk=D, name=f"{t}_mix_out")
        x3, s3 = _ffn_fwd(x2, W["norm_ffn2"][i], W["ffn2_w_in"][i], W["ffn2_w_out"][i], f"{t}_ffn2")
        saved.append((s1, x1, h, proj, mq0, s_tok, s_mem, mix, s3))
        x = x3

    dx, loss = loss_head(x, target, name="loss_head")

    G = {k: [None] * depth for k in ("norm_ffn1", "norm_mix", "norm_ffn2", "mem_q_norm", "mem_k_norm", "ffn1_w_in",
                                     "ffn1_w_out", "ffn2_w_in", "ffn2_w_out", "w_out", "mem_w_kv")}
    n_fox, n_gmlp = (depth + 1) // 2, depth // 2
    for k in ("fox_w_in", "fox_b_f", "fox_q_norm", "fox_k_norm"):
        G[k] = [None] * n_fox
    for k in ("gmlp_w_in", "gmlp_v_norm", "gmlp_w_s", "gmlp_b_s"):
        G[k] = [None] * n_gmlp
    dkv_all = [None] * depth
    for i in reversed(range(depth)):
        kind, j = i % 2, i // 2
        t = f"l{i}"
        s1, x1, h, proj, mq0, s_tok, s_mem, mix, s3 = saved[i]
        dx, G["norm_ffn2"][i], G["ffn2_w_in"][i], G["ffn2_w_out"][i] = _ffn_bwd(
            dx, s3, W["norm_ffn2"][i], W["ffn2_w_in"][i], W["ffn2_w_out"][i], f"{t}_ffn2")
        dmix = matmul(dx, W["w_out"][i], tb=True, tm=1024, tn=1024, tk=D, name=f"{t}_dmix")
        G["w_out"][i] = grad_rows(mix, dx, name=f"{t}_dwmixout")
        dmq, G["mem_q_norm"][i], G["mem_k_norm"][i], G["mem_w_kv"][i], dkv_all[i] = _mem_attn_bwd(
            dmix, P, proj, mq0, s_mem, mem_n, W["mem_q_norm"][i], W["mem_k_norm"][i], f"{t}_mem")
        if kind == 0:
            rider = comm.early_grads_rider(G) if (comm is not None and i == 0) else None
            dqkv, dz, G["fox_b_f"][j], G["fox_q_norm"][j], G["fox_k_norm"][j], rode = _fox_bwd(
                dmix, proj, s_tok, W["fox_q_norm"][j], W["fox_k_norm"][j], tok, rider, f"{t}_fox")
            if rider is not None:
                comm.accept_early_grads(rode)
            dproj = jnp.concatenate([dqkv, dmq, dz], axis=1).astype(BF16)
            w_mix, wkey = W["fox_w_in"][j], "fox_w_in"
        else:
            vg, ws, bs = _gmlp_operands(W["gmlp_v_norm"][j], W["gmlp_w_s"][j], W["gmlp_b_s"][j])
            dup, dvp, dws, dbs, dvg = gmlp_bwd(proj, P, P, vg, ws, jnp.swapaxes(ws, 2, 3), bs, dmix,
                                               name=f"{t}_dgmlp")
            G["gmlp_w_s"][j] = dws.reshape(W["gmlp_w_s"][j].shape)
            G["gmlp_b_s"][j] = dbs.reshape(W["gmlp_b_s"][j].shape)
            G["gmlp_v_norm"][j] = dvg.reshape(-1)
            dproj = jnp.concatenate([dup, dvp, dmq], axis=1).astype(BF16)
            w_mix, wkey = W["gmlp_w_in"][j], "gmlp_w_in"
        G[wkey][j] = matmul(h, dproj, ta=True, tm=1024, tn=896, tk=1024, name=f"{t}_dwmixin")
        dh = matmul(dproj, w_mix, tb=True, tm=1024, tn=1024, tk=896, name=f"{t}_dhmix")
        dx, dgm = rms_bwd(x1, dh, W["norm_mix"][i], dx, name=f"{t}_dmixrms")
        G["norm_mix"][i] = dgm[0]
        dx, G["norm_ffn1"][i], G["ffn1_w_in"][i], G["ffn1_w_out"][i] = _ffn_bwd(
            dx, s1, W["norm_ffn1"][i], W["ffn1_w_in"][i], W["ffn1_w_out"][i], f"{t}_ffn1")
    w_kv_all = jnp.concatenate([W["mem_w_kv"][i] for i in range(depth)], axis=1)
    dmem_n = matmul(jnp.concatenate(dkv_all, axis=1), w_kv_all, tb=True, tm=256, tn=512, tk=1024, name="dmem_n")
    _, dmemg = rms_bwd(mem, dmem_n, W["mem_norm"], None, name="dmem_rms")
    G["mem_norm"] = [dmemg[0]]
    return loss, dx, G


def _fox_cols_to_compute(w, tok):
    H = tok // HEAD_DIM
    qkv, f, mq = w[..., :3 * tok], w[..., 3 * tok:3 * tok + H], w[..., 3 * tok + H:]
    f = jnp.pad(f, [(0, 0)] * (w.ndim - 1) + [(0, LANES - H)])
    return jnp.concatenate([qkv, mq, f], axis=-1)


def _fox_cols_from_compute(w, tok):
    H = tok // HEAD_DIM
    qkv, mq, f = w[..., :3 * tok], w[..., 3 * tok:3 * tok + MEM_WIDTH], w[..., 3 * tok + MEM_WIDTH:3 * tok + MEM_WIDTH + H]
    return jnp.concatenate([qkv, f, mq], axis=-1)


_BIG = ("ffn1_w_in", "ffn1_w_out", "ffn2_w_in", "ffn2_w_out", "w_out", "mem_w_kv", "fox_w_in", "gmlp_w_in")
_SMALL = ("norm_ffn1", "norm_mix", "norm_ffn2", "mem_norm", "mem_q_norm", "mem_k_norm", "fox_b_f", "fox_q_norm",
          "fox_k_norm", "gmlp_v_norm", "gmlp_w_s", "gmlp_b_s")
WEIGHT_ORDER = ("norm_ffn1", "ffn1_w_in", "ffn1_w_out", "norm_mix", "norm_ffn2", "ffn2_w_in", "ffn2_w_out", "w_out",
                "mem_norm", "mem_w_kv", "mem_q_norm", "mem_k_norm", "fox_w_in", "fox_b_f", "fox_q_norm", "fox_k_norm",
                "gmlp_w_in", "gmlp_v_norm", "gmlp_w_s", "gmlp_b_s")


def _small_slab(rows_list, index):
    sizes = [s.shape[0] for s in rows_list]
    n_rows = [-(-n // LANES) for n in sizes]
    small = jnp.concatenate([jnp.pad(s, (0, r * LANES - n)).reshape(r, LANES)
                             for s, n, r in zip(rows_list, sizes, n_rows)], axis=0)
    small = jnp.pad(small, ((0, -small.shape[0] % 64), (0, 0)))
    return _slot_in_empty(small, index, 8), sizes, n_rows


_FIRST_WEIGHTS = (("ffn1_w_in", 0), ("ffn1_w_out", 0), ("fox_w_in", 0))


def _weight_from_slab(name, slab, tok):
    if name in ("ffn1_w_in", "ffn2_w_in"):
        return slab
    if name == "fox_w_in":
        return _fox_cols_to_compute(_chips_to_cols(slab), tok)
    if name == "gmlp_w_in":
        return _chips_to_cols(slab)
    return slab.reshape(4 * slab.shape[1], slab.shape[2])


def _grad_to_slab(name, g, tok):
    if name == "fox_w_in":
        return _cols_to_chips(_fox_cols_from_compute(g, tok))
    if name == "gmlp_w_in":
        return _cols_to_chips(g)
    return g


class _Exchange:
    def __init__(self, shards, tok, chip, core):
        self.tok, self.core = tok, core
        self.half = core.reshape(1).astype(jnp.int32)
        self.chip_id = chip.reshape(1).astype(jnp.int32)
        items = [(k, i) for k in _BIG for i in range(shards[k].shape[0])]
        self.slabs = {it: _slot_in_empty(shards[it[0]][it[1]].astype(BF16), chip, 4) for it in items}
        self.late = [it for it in items if it not in _FIRST_WEIGHTS]
        self.reduced = {}
        self.early = None

    def first_weights(self, small_slab):
        got, small_all = gather_weights([self.slabs[it] for it in _FIRST_WEIGHTS], small_slab, name="gather_first")
        return {it: _weight_from_slab(it[0], s, self.tok) for it, s in zip(_FIRST_WEIGHTS, got)}, small_all

    def late_weights_rider(self):
        return gather_rider([self.slabs[it] for it in self.late])

    def accept_late_weights(self, W, got):
        for (k, i), s in zip(self.late, got):
            W[k][i] = _weight_from_slab(k, s, self.tok)

    def _pair_sums(self, G, items, small_slab, tag):
        parts = [_grad_to_slab(k, G[k][i], self.tok) for k, i in items]
        landed, small_all = exchange_with_sibling(parts, small_slab, name=f"grad_exchange_{tag}")
        pair = [pair_sum(p, l, self.half, name=f"grad_pair_sum_{k}{i}") for (k, i), p, l in zip(items, parts, landed)]
        return pair, small_all

    def early_grads_rider(self, G):
        items = [(k, i) for k in _BIG for i in range(len(G[k])) if G[k][i] is not None]
        pair, _ = self._pair_sums(G, items, None, "early")
        self.early = (items, pair)
        return scatter_rider(pair)

    def accept_early_grads(self, landed):
        items, pair = self.early
        self._chip_sums(items, pair, landed)

    def _chip_sums(self, items, pair, landed):
        for (k, i), q, l in zip(items, pair, landed):
            self.reduced[(k, i)] = chip_sum(q, l, self.chip_id, name=f"grad_chip_sum_{k}{i}")

    def finish_grads(self, G, small_slab):
        items = [(k, i) for k in _BIG for i in range(len(G[k])) if (k, i) not in self.reduced]
        pair, small_all = self._pair_sums(G, items, small_slab, "late")
        self._chip_sums(items, pair, scatter_to_chips(pair, name="grad_scatter_late"))
        order = sorted(self.reduced)
        other = share_with_sibling([self.reduced[it] for it in order], name="grad_share")
        full = {}
        for it, a, b in zip(order, [self.reduced[it] for it in order], other):
            full[it] = jnp.where(self.core == 0, jnp.concatenate([a, b]), jnp.concatenate([b, a]))
        names = sorted({k for k, _ in order})
        return {k: jnp.stack([full[(k, i)] for i in range(len(G[k]))]) for k in names}, small_all


def kernel(x, mem, norm_ffn1, ffn1_w_in, ffn1_w_out, norm_mix, norm_ffn2, ffn2_w_in, ffn2_w_out, w_out, mem_norm, mem_w_kv, mem_q_norm, mem_k_norm, fox_w_in, fox_b_f, fox_q_norm, fox_k_norm, gmlp_w_in, gmlp_v_norm, gmlp_w_s, gmlp_b_s, loss_target, m_norm_ffn1, m_ffn1_w_in, m_ffn1_w_out, m_norm_mix, m_norm_ffn2, m_ffn2_w_in, m_ffn2_w_out, m_w_out, m_mem_norm, m_mem_w_kv, m_mem_q_norm, m_mem_k_norm, m_fox_w_in, m_fox_b_f, m_fox_q_norm, m_fox_k_norm, m_gmlp_w_in, m_gmlp_v_norm, m_gmlp_w_s, m_gmlp_b_s, v_norm_ffn1, v_ffn1_w_in, v_ffn1_w_out, v_norm_mix, v_norm_ffn2, v_ffn2_w_in, v_ffn2_w_out, v_w_out, v_mem_norm, v_mem_w_kv, v_mem_q_norm, v_mem_k_norm, v_fox_w_in, v_fox_b_f, v_fox_q_norm, v_fox_k_norm, v_gmlp_w_in, v_gmlp_v_norm, v_gmlp_w_s, v_gmlp_b_s):
    w = dict(norm_ffn1=norm_ffn1, ffn1_w_in=ffn1_w_in, ffn1_w_out=ffn1_w_out, norm_mix=norm_mix, norm_ffn2=norm_ffn2,
             ffn2_w_in=ffn2_w_in, ffn2_w_out=ffn2_w_out, w_out=w_out, mem_norm=mem_norm, mem_w_kv=mem_w_kv,
             mem_q_norm=mem_q_norm, mem_k_norm=mem_k_norm, fox_w_in=fox_w_in, fox_b_f=fox_b_f, fox_q_norm=fox_q_norm,
             fox_k_norm=fox_k_norm, gmlp_w_in=gmlp_w_in, gmlp_v_norm=gmlp_v_norm, gmlp_w_s=gmlp_w_s, gmlp_b_s=gmlp_b_s)
    m = dict(norm_ffn1=m_norm_ffn1, ffn1_w_in=m_ffn1_w_in, ffn1_w_out=m_ffn1_w_out, norm_mix=m_norm_mix,
             norm_ffn2=m_norm_ffn2, ffn2_w_in=m_ffn2_w_in, ffn2_w_out=m_ffn2_w_out, w_out=m_w_out, mem_norm=m_mem_norm,
             mem_w_kv=m_mem_w_kv, mem_q_norm=m_mem_q_norm, mem_k_norm=m_mem_k_norm, fox_w_in=m_fox_w_in,
             fox_b_f=m_fox_b_f, fox_q_norm=m_fox_q_norm, fox_k_norm=m_fox_k_norm, gmlp_w_in=m_gmlp_w_in,
             gmlp_v_norm=m_gmlp_v_norm, gmlp_w_s=m_gmlp_w_s, gmlp_b_s=m_gmlp_b_s)
    v = dict(norm_ffn1=v_norm_ffn1, ffn1_w_in=v_ffn1_w_in, ffn1_w_out=v_ffn1_w_out, norm_mix=v_norm_mix,
             norm_ffn2=v_norm_ffn2, ffn2_w_in=v_ffn2_w_in, ffn2_w_out=v_ffn2_w_out, w_out=v_w_out, mem_norm=v_mem_norm,
             mem_w_kv=v_mem_w_kv, mem_q_norm=v_mem_q_norm, mem_k_norm=v_mem_k_norm, fox_w_in=v_fox_w_in,
             fox_b_f=v_fox_b_f, fox_q_norm=v_fox_q_norm, fox_k_norm=v_fox_k_norm, gmlp_w_in=v_gmlp_w_in,
             gmlp_v_norm=v_gmlp_v_norm, gmlp_w_s=v_gmlp_w_s, gmlp_b_s=v_gmlp_b_s)
    D = x.shape[-1]
    tok = D - MEM_WIDTH
    xi, yi, ci = _position()
    chip = 2 * xi + yi

    device = 4 * xi + 2 * yi + ci

    comm = _Exchange(w, tok, chip, ci)
    vn = w["gmlp_v_norm"]
    vn_slab, _, _ = _small_slab([vn.reshape(-1)], device)
    first, vn_all = comm.first_weights(vn_slab)
    W = {k: w[k] for k in _SMALL}
    W["gmlp_v_norm"] = _chips_to_cols(vn_all[0::2].reshape(4, -1)[:, :vn.size].reshape((4,) + vn.shape))
    for k in _BIG:
        W[k] = [first.get((k, i)) for i in range(w[k].shape[0])]

    loss, grad_x, g = local_step(x[0], mem[0], loss_target[0], W, comm)

    small_list = [jnp.stack(g[k]).reshape(-1) for k in _SMALL] + [loss.reshape(-1)]
    small, small_sizes, small_rows = _small_slab(small_list, device)
    red, small_all = comm.finish_grads(g, small)
    small_sum = ordered_sum(small_all, name="small_sum")
    off = 0
    for k, n, r in zip(_SMALL, small_sizes, small_rows):
        red[k] = small_sum[off:off + r].reshape(-1)[:n].reshape((-1,) + w[k].shape[1:] if k != "gmlp_v_norm"
                                                                else (w[k].shape[0], -1))
        off += r
    loss_total = small_sum[off, 0]
    vn_cols = w["gmlp_v_norm"].shape[-1]
    red["gmlp_v_norm"] = lax.dynamic_slice_in_dim(red["gmlp_v_norm"], chip * vn_cols, vn_cols, axis=-1)

    deltas, new_m, new_v = {}, {}, {}
    for k in WEIGHT_ORDER:
        wk = w[k] if w[k].ndim > 1 else w[k].reshape(1, -1)
        upd = adamw(wk, red[k].reshape(wk.shape), m[k].reshape(wk.shape), v[k].reshape(wk.shape), name=f"adamw_{k}")
        deltas[k], new_m[k], new_v[k] = (u.reshape(w[k].shape) for u in upd)
    return (loss_total, grad_x[None], *[red[k].reshape(w[k].shape) for k in WEIGHT_ORDER],
            *[deltas[k] for k in WEIGHT_ORDER], *[new_m[k] for k in WEIGHT_ORDER], *[new_v[k] for k in WEIGHT_ORDER])
```

```python
import functools
import math
from typing import Callable, NamedTuple

import jax
import jax.numpy as jnp
from jax import lax
from jax.experimental import pallas as pl
from jax.experimental.pallas import tpu as pltpu

F32 = jnp.float32
BF16 = jnp.bfloat16
EPS = 1e-6
HEAD_DIM = 64
MEM_WIDTH = 256
CHUNK = 128
LANES = 128
NEG = -1e30
VMEM_LIMIT_BYTES = 56 * 1024 * 1024
ATTN_Q_BLOCK = 1024
ATTN_K_BLOCK = 1024
QK_SCALE = 0.125
LOG2E = 1.4426950408889634
LN2 = 0.6931471805599453
Q_SCALE = QK_SCALE * LOG2E
MESH_ID = pl.DeviceIdType.MESH

ADAM_LR = 0.001
ADAM_B1 = 0.9
ADAM_B2 = 0.999
ADAM_EPS = 1e-08
ADAM_WD = 0.01
ADAM_STEP = 10


def _tile(n, pref, align):
    t = (min(pref, n) // align) * align
    while t >= align:
        if n % t == 0:
            return t
        t -= align
    return n


def _params(sem):
    return pltpu.CompilerParams(dimension_semantics=sem, vmem_limit_bytes=VMEM_LIMIT_BYTES)


def _dot(a, b, ca, cb):
    return lax.dot_general(a, b, (((ca,), (cb,)), ((), ())), preferred_element_type=F32)


def _sigmoid(x):
    return 1.0 / (1.0 + jnp.exp(-x))


_GELU_C = math.sqrt(2.0 / math.pi)


def _gelu(x):
    return 0.5 * x * (1.0 + jnp.tanh(_GELU_C * (x + 0.044715 * (x * x * x))))


def _gelu_grad(x):
    t = jnp.tanh(_GELU_C * (x + 0.044715 * (x * x * x)))
    return 0.5 * (1.0 + t) + 0.5 * x * (1.0 - t * t) * (_GELU_C * (1.0 + 3.0 * 0.044715 * (x * x)))


def matmul(a, b, *, ta=False, tb=False, out_dtype=F32, scale=None, res=None,
           tm=1024, tn=512, tk=1024, name):
    if ta:
        K, M = a.shape
    else:
        M, K = a.shape
    N = b.shape[0] if tb else b.shape[1]
    tm = _tile(M, tm, LANES if ta else 16)
    tn = _tile(N, tn, LANES)
    tk = _tile(K, tk, LANES)
    nk = K // tk
    a_spec = pl.BlockSpec((tk, tm), lambda i, j, k: (k, i)) if ta else pl.BlockSpec((tm, tk), lambda i, j, k: (i, k))
    b_spec = pl.BlockSpec((tn, tk), lambda i, j, k: (j, k)) if tb else pl.BlockSpec((tk, tn), lambda i, j, k: (k, j))
    o_spec = pl.BlockSpec((tm, tn), lambda i, j, k: (i, j))
    ca, cb = (0 if ta else 1), (1 if tb else 0)
    has_res = res is not None

    def body(*refs):
        a_ref, b_ref = refs[0], refs[1]
        res_ref = refs[2] if has_res else None
        o_ref = refs[3] if has_res else refs[2]
        acc_ref = refs[-1]
        k = pl.program_id(2)
        prod = _dot(a_ref[...].astype(BF16), b_ref[...].astype(BF16), ca, cb)

        def finish(acc):
            if scale is not None:
                acc = acc * scale
            if has_res:
                acc = res_ref[...] + acc
            o_ref[...] = acc.astype(out_dtype)

        if nk == 1:
            finish(prod)
        else:
            @pl.when(k == 0)
            def _():
                acc_ref[...] = prod

            @pl.when(k > 0)
            def _():
                acc_ref[...] += prod

            @pl.when(k == nk - 1)
            def _():
                finish(acc_ref[...])

    in_specs = [a_spec, b_spec] + ([o_spec] if has_res else [])
    args = (a, b) + ((res,) if has_res else ())
    return pl.pallas_call(
        body, grid=(M // tm, N // tn, nk), in_specs=in_specs, out_specs=o_spec,
        out_shape=jax.ShapeDtypeStruct((M, N), out_dtype),
        scratch_shapes=[pltpu.VMEM((tm, tn) if nk > 1 else (8, LANES), F32)],
        compiler_params=_params(("parallel", "parallel", "arbitrary")), name=name)(*args)


def swiglu_fwd(h, w_slab, *, name):
    S, D = h.shape
    Fc = w_slab.shape[-1]
    tm = _tile(S, 512, 16)

    def body(h_ref, wa_ref, wb_ref, a_ref, b_ref, act_ref):
        hv = h_ref[...]
        a = _dot(hv, wa_ref[...], 1, 0)
        b = _dot(hv, wb_ref[...], 1, 0)
        a_ref[...] = a.astype(BF16)
        b_ref[...] = b.astype(BF16)
        act_ref[...] = (a * _sigmoid(a) * b).astype(BF16)

    out = pl.BlockSpec((tm, Fc), lambda j, i: (i, j))
    return pl.pallas_call(
        body, grid=(2, S // tm),
        in_specs=[pl.BlockSpec((tm, D), lambda j, i: (i, 0)),
                  pl.BlockSpec((None, D, Fc), lambda j, i: (j, 0, 0)),
                  pl.BlockSpec((None, D, Fc), lambda j, i: (j + 2, 0, 0))],
        out_specs=[out, out, out],
        out_shape=[jax.ShapeDtypeStruct((S, 2 * Fc), BF16)] * 3,
        compiler_params=_params(("parallel", "parallel")), name=name)(h, w_slab, w_slab)


def swiglu_bwd(dy, w_out, a, b, *, name):
    S, D = dy.shape
    F = w_out.shape[0]
    fc = F // 2
    tm = _tile(S, 512, 16)

    def body(dy_ref, w_ref, a_ref, b_ref, da_ref, db_ref):
        dact = 0.5 * _dot(dy_ref[...].astype(BF16), w_ref[...], 1, 1)
        av = a_ref[...].astype(F32)
        sg = _sigmoid(av)
        da_ref[...] = (dact * b_ref[...].astype(F32) * (sg * (1.0 + av * (1.0 - sg)))).astype(BF16)
        db_ref[...] = (dact * (av * sg)).astype(BF16)

    blk = pl.BlockSpec((tm, fc), lambda j, i: (i, j))
    return pl.pallas_call(
        body, grid=(2, S // tm),
        in_specs=[pl.BlockSpec((tm, D), lambda j, i: (i, 0)), pl.BlockSpec((fc, D), lambda j, i: (j, 0)), blk, blk],
        out_specs=[blk, blk],
        out_shape=[jax.ShapeDtypeStruct((S, F), BF16), jax.ShapeDtypeStruct((S, F), BF16)],
        compiler_params=_params(("parallel", "parallel")), name=name)(dy, w_out, a, b)


def ffn_dh(da, db, w_slab, *, name):
    S, F = da.shape
    D, Fc = w_slab.shape[-2:]
    tm = _tile(S, 1024, 16)

    def body(da_ref, db_ref, w_ref, o_ref, acc_ref):
        k = pl.program_id(1)

        @pl.when(k == 0)
        def _():
            acc_ref[...] = jnp.zeros_like(acc_ref)

        @pl.when(k < 2)
        def _():
            acc_ref[...] += _dot(da_ref[...], w_ref[...], 1, 1)

        @pl.when(k >= 2)
        def _():
            acc_ref[...] += _dot(db_ref[...], w_ref[...], 1, 1)

        @pl.when(k == 3)
        def _():
            o_ref[...] = acc_ref[...]

    return pl.pallas_call(
        body, grid=(S // tm, 4),
        in_specs=[pl.BlockSpec((tm, Fc), lambda i, k: (i, jnp.minimum(k, 1))),
                  pl.BlockSpec((tm, Fc), lambda i, k: (i, jnp.maximum(k - 2, 0))),
                  pl.BlockSpec((None, D, Fc), lambda i, k: (k, 0, 0))],
        out_specs=pl.BlockSpec((tm, D), lambda i, k: (i, 0)),
        out_shape=jax.ShapeDtypeStruct((S, D), F32),
        scratch_shapes=[pltpu.VMEM((tm, D), F32)],
        compiler_params=_params(("parallel", "arbitrary")), name=name)(da, db, w_slab)


def grad_cols(h, da, db, *, name):
    S, D = h.shape
    Fc = da.shape[1] // 2
    tk = _tile(S, 1024, 16)
    nk = S // tk

    def body(h_ref, da_ref, db_ref, o_ref, acc_ref):
        ch, k = pl.program_id(0), pl.program_id(1)

        @pl.when(k == 0)
        def _():
            acc_ref[...] = jnp.zeros_like(acc_ref)

        @pl.when(ch < 2)
        def _():
            acc_ref[...] += _dot(h_ref[...], da_ref[...], 0, 0)

        @pl.when(ch >= 2)
        def _():
            acc_ref[...] += _dot(h_ref[...], db_ref[...], 0, 0)

        @pl.when(k == nk - 1)
        def _():
            o_ref[...] = acc_ref[...]

    return pl.pallas_call(
        body, grid=(4, nk),
        in_specs=[pl.BlockSpec((tk, D), lambda ch, k: (k, 0)),
                  pl.BlockSpec((tk, Fc), lambda ch, k: (jnp.where(ch < 2, k, 0), jnp.minimum(ch, 1))),
                  pl.BlockSpec((tk, Fc), lambda ch, k: (jnp.where(ch >= 2, k, 0), jnp.maximum(ch - 2, 0)))],
        out_specs=pl.BlockSpec((None, D, Fc), lambda ch, k: (ch, 0, 0)),
        out_shape=jax.ShapeDtypeStruct((4, D, Fc), F32),
        scratch_shapes=[pltpu.VMEM((D, Fc), F32)],
        compiler_params=_params(("parallel", "arbitrary")), name=name)(h, da, db)


def grad_rows(a, b, *, scale=None, name):
    S, M = a.shape
    N = b.shape[1]
    R = M // 4
    tn = _tile(N, 512, LANES)
    tk = _tile(S, 1024, 16)
    nk = S // tk

    def body(a_ref, b_ref, o_ref, acc_ref):
        k = pl.program_id(1)

        @pl.when(k == 0)
        def _():
            acc_ref[...] = jnp.zeros_like(acc_ref)

        acc_ref[...] += _dot(a_ref[...].astype(BF16), b_ref[...].astype(BF16), 0, 0)

        @pl.when(k == nk - 1)
        def _():
            for d in range(4):
                part = acc_ref[d * R:(d + 1) * R, :]
                o_ref[d] = part if scale is None else part * scale

    return pl.pallas_call(
        body, grid=(N // tn, nk),
        in_specs=[pl.BlockSpec((tk, M), lambda j, k: (k, 0)), pl.BlockSpec((tk, tn), lambda j, k: (k, j))],
        out_specs=pl.BlockSpec((4, R, tn), lambda j, k: (0, 0, j)),
        out_shape=jax.ShapeDtypeStruct((4, R, N), F32),
        scratch_shapes=[pltpu.VMEM((M, tn), F32)],
        compiler_params=_params(("parallel", "arbitrary")), name=name)(a, b)


def rms_fwd(x, g, *, name):
    S, D = x.shape
    ts = _tile(S, 1024, 16)

    def body(x_ref, g_ref, h_ref):
        xv = x_ref[...]
        r = lax.rsqrt(jnp.mean(xv * xv, axis=-1, keepdims=True) + EPS)
        h_ref[...] = (xv * r * g_ref[...]).astype(BF16)

    return pl.pallas_call(
        body, grid=(S // ts,),
        in_specs=[pl.BlockSpec((ts, D), lambda i: (i, 0)), pl.BlockSpec((1, D), lambda i: (0, 0))],
        out_specs=pl.BlockSpec((ts, D), lambda i: (i, 0)),
        out_shape=jax.ShapeDtypeStruct((S, D), BF16),
        compiler_params=_params(("parallel",)), name=name)(x, g.reshape(1, D))


def rms_bwd(x, dh, g, res, *, name):
    S, D = x.shape
    ts = _tile(S, 512, 16)
    has_res = res is not None

    def body(*refs):
        x_ref, dh_ref, g_ref = refs[:3]
        res_ref = refs[3] if has_res else None
        dx_ref, dg_ref = refs[-2:]
        i = pl.program_id(0)
        xv, dhv = x_ref[...], dh_ref[...].astype(F32)
        r = lax.rsqrt(jnp.mean(xv * xv, axis=-1, keepdims=True) + EPS)
        u = dhv * g_ref[...]
        dx = r * u - xv * (r * r * r) * jnp.mean(xv * u, axis=-1, keepdims=True)
        if has_res:
            dx = res_ref[...] + dx
        dx_ref[...] = dx
        part = jnp.sum(dhv * xv * r, axis=0, keepdims=True)

        @pl.when(i == 0)
        def _():
            dg_ref[...] = part

        @pl.when(i > 0)
        def _():
            dg_ref[...] += part

    row = pl.BlockSpec((ts, D), lambda i: (i, 0))
    vec = pl.BlockSpec((1, D), lambda i: (0, 0))
    args = (x, dh, g.reshape(1, D)) + ((res,) if has_res else ())
    return pl.pallas_call(
        body, grid=(S // ts,), in_specs=[row, row, vec] + ([row] if has_res else []),
        out_specs=[row, vec],
        out_shape=[jax.ShapeDtypeStruct((S, D), F32), jax.ShapeDtypeStruct((1, D), F32)],
        compiler_params=_params(("arbitrary",)), name=name)(*args)


def _low_half(shape):
    return lax.broadcasted_iota(jnp.int32, shape, len(shape) - 1) < HEAD_DIM


def _half_sums(x, low):
    sa = jnp.sum(jnp.where(low, x, 0.0), axis=1, keepdims=True)
    sb = jnp.sum(jnp.where(low, 0.0, x), axis=1, keepdims=True)
    return jnp.where(low, sa, sb)


def pairnorm_fwd(x, col0, n_pairs, g, *, scale=None, name):
    S = x.shape[0]
    ts = _tile(S, 1024, 16)

    def body(x_ref, g_ref, o_ref):
        xv = x_ref[...]
        r = lax.rsqrt(_half_sums(xv * xv, _low_half(xv.shape)) * (1.0 / HEAD_DIM) + EPS)
        y = xv * r * g_ref[...]
        o_ref[...] = (y if scale is None else y * scale).astype(BF16)

    return pl.pallas_call(
        body, grid=(S // ts, n_pairs),
        in_specs=[pl.BlockSpec((ts, LANES), lambda i, j: (i, col0 + j)), pl.BlockSpec((1, LANES), lambda i, j: (0, 0))],
        out_specs=pl.BlockSpec((ts, LANES), lambda i, j: (i, j)),
        out_shape=jax.ShapeDtypeStruct((S, n_pairs * LANES), BF16),
        compiler_params=_params(("parallel", "parallel")), name=name)(x, jnp.tile(g.reshape(1, HEAD_DIM), (1, 2)))


def pairnorm_bwd(x, col0, n_pairs, dy, g, *, name):
    S = x.shape[0]
    ts = _tile(S, 1024, 16)

    def body(x_ref, dy_ref, g_ref, dx_ref, dg_ref):
        first = jnp.logical_and(pl.program_id(0) == 0, pl.program_id(1) == 0)
        xv, dyv = x_ref[...], dy_ref[...]
        low = _low_half(xv.shape)
        r = lax.rsqrt(_half_sums(xv * xv, low) * (1.0 / HEAD_DIM) + EPS)
        u = dyv * g_ref[...]
        dx_ref[...] = r * u - xv * (r * r * r) * (_half_sums(xv * u, low) * (1.0 / HEAD_DIM))
        part = jnp.sum(dyv * xv * r, axis=0, keepdims=True)

        @pl.when(first)
        def _():
            dg_ref[...] = part

        @pl.when(jnp.logical_not(first))
        def _():
            dg_ref[...] += part

    vec = pl.BlockSpec((1, LANES), lambda i, j: (0, 0))
    blk = pl.BlockSpec((ts, LANES), lambda i, j: (i, j))
    return pl.pallas_call(
        body, grid=(S // ts, n_pairs),
        in_specs=[pl.BlockSpec((ts, LANES), lambda i, j: (i, col0 + j)), blk, vec], out_specs=[blk, vec],
        out_shape=[jax.ShapeDtypeStruct((S, n_pairs * LANES), F32), jax.ShapeDtypeStruct((1, LANES), F32)],
        compiler_params=_params(("arbitrary", "arbitrary")), name=name)(x, dy, jnp.tile(g.reshape(1, HEAD_DIM), (1, 2)))


def _split3(x):
    x1 = x.astype(BF16)
    r1 = x - x1.astype(F32)
    x2 = r1.astype(BF16)
    x3 = (r1 - x2.astype(F32)).astype(BF16)
    return x1, x2, x3


def _tri_ones(n, lower):
    r = lax.broadcasted_iota(jnp.int32, (n, n), 0)
    c = lax.broadcasted_iota(jnp.int32, (n, n), 1)
    return jnp.where((c <= r) if lower else (c >= r), 1.0, 0.0).astype(BF16)


def fgate_fwd(z, col0, bias, *, out_scale, name):
    S, L = z.shape[0], LANES
    tb = _tile(S, 256, 16)

    def body(z_ref, b_ref, c_ref, carry):
        i = pl.program_id(0)

        @pl.when(i == 0)
        def _():
            carry[...] = jnp.zeros_like(carry)

        zz = z_ref[...] + b_ref[...]
        lf = jnp.minimum(zz, 0.0) - jnp.log(1.0 + jnp.exp(-jnp.abs(zz)))
        tri = _tri_ones(tb, True)
        x1, x2, x3 = _split3(lf)
        c = (_dot(tri, x1, 1, 0) + _dot(tri, x2, 1, 0)) + _dot(tri, x3, 1, 0) + carry[...]
        c_ref[...] = c * out_scale
        carry[...] += jnp.sum(lf, axis=0, keepdims=True)

    return pl.pallas_call(
        body, grid=(S // tb,),
        in_specs=[pl.BlockSpec((tb, L), lambda i: (i, col0)), pl.BlockSpec((1, L), lambda i: (0, 0))],
        out_specs=pl.BlockSpec((tb, L), lambda i: (i, 0)),
        out_shape=jax.ShapeDtypeStruct((S, L), F32),
        scratch_shapes=[pltpu.VMEM((1, L), F32)],
        compiler_params=_params(("arbitrary",)), name=name)(z, bias)


def fgate_bwd(z, col0, bias, drs, dcs, *, name):
    S, L = z.shape[0], LANES
    tb = _tile(S, 256, 16)
    nb = S // tb

    def body(z_ref, b_ref, drs_ref, dcs_ref, dz_ref, db_ref, carry):
        i = pl.program_id(0)

        @pl.when(i == 0)
        def _():
            carry[...] = jnp.zeros_like(carry)

        tri = _tri_ones(tb, False)
        dc = drs_ref[...] - dcs_ref[...]
        x1, x2, x3 = _split3(dc)
        dlf = (_dot(tri, x1, 1, 0) + _dot(tri, x2, 1, 0)) + _dot(tri, x3, 1, 0) + carry[...]
        carry[...] += jnp.sum(dc, axis=0, keepdims=True)
        dz = dlf * _sigmoid(-(z_ref[...] + b_ref[...]))
        dz_ref[...] = dz
        part = jnp.sum(dz, axis=0, keepdims=True)

        @pl.when(i == 0)
        def _():
            db_ref[...] = part

        @pl.when(i > 0)
        def _():
            db_ref[...] += part

    rev = pl.BlockSpec((tb, L), lambda i: (nb - 1 - i, 0))
    vec = pl.BlockSpec((1, L), lambda i: (0, 0))
    return pl.pallas_call(
        body, grid=(nb,), in_specs=[pl.BlockSpec((tb, L), lambda i: (nb - 1 - i, col0)), vec, rev, rev],
        out_specs=[rev, vec],
        out_shape=[jax.ShapeDtypeStruct((S, L), F32), jax.ShapeDtypeStruct((1, L), F32)],
        scratch_shapes=[pltpu.VMEM((1, L), F32)],
        compiler_params=_params(("arbitrary",)), name=name)(z, bias, drs, dcs)


def _one_head(x, low, a):
    return jnp.where(low if a == 0 else jnp.logical_not(low), x, jnp.zeros_like(x))


def _with_decay(x, low, a, c, is_query):
    lane = lax.broadcasted_iota(jnp.int32, x.shape, 1)
    base = HEAD_DIM if a == 0 else 0
    pieces = _split3(c)
    ones = [jnp.ones_like(pieces[0])] * 3
    vals = (list(pieces) + ones) if is_query else (ones + [-p for p in pieces])
    out = jnp.zeros_like(x)
    for t, val in enumerate(vals):
        out = jnp.where(lane == base + t, val.astype(x.dtype), out)
    return jnp.where(low if a == 0 else jnp.logical_not(low), x, out)


class Rider(NamedTuple):
    inputs: tuple
    out_shapes: tuple
    aliases: dict
    sems: tuple
    plan: Callable


def _with_rider(rider, n_in, n_out, n_scratch):
    if rider is None:
        return [], [], [], [], {}, lambda refs: (refs[:n_in], refs[n_in:n_in + n_out], refs[n_in + n_out:], None)
    e_in, e_out = len(rider.inputs), len(rider.out_shapes)

    def split(refs):
        ins, r_in = refs[:n_in], refs[n_in:n_in + e_in]
        o0 = n_in + e_in
        outs, r_out = refs[o0:o0 + n_out], refs[o0 + n_out:o0 + n_out + e_out]
        s0 = o0 + n_out + e_out
        return ins, outs, refs[s0:s0 + n_scratch], rider.plan(r_in, r_out, refs[s0 + n_scratch:])

    aliases = {n_in + a: n_out + b for a, b in rider.aliases.items()}
    return list(rider.inputs), [_ANY] * e_in, list(rider.out_shapes), [_ANY] * e_out, aliases, split


def attn_fwd(q, q0, k, k0, v, v0, n_pairs, decay, *, causal, rider=None, name):
    Sq, Sk = q.shape[0], k.shape[0]
    tq = _tile(Sq, ATTN_Q_BLOCK, LANES)
    tk = _tile(Sk, ATTN_K_BLOCK, LANES)
    nq, nk = Sq // tq, Sk // tk
    bias = decay is not None
    r_args, r_in_specs, r_shapes, r_out_specs, aliases, split = _with_rider(rider, 5 if bias else 3, 2, 3)

    def body(*refs):
        ins, (o_ref, lse_ref), (m_sc, l_sc, acc_sc), ride = split(refs)
        q_ref, k_ref, v_ref = ins[:3]
        cq_ref, ck_ref = (ins[3], ins[4]) if bias else (None, None)
        pr, i, j = pl.program_id(0), pl.program_id(1), pl.program_id(2)
        if ride is not None:
            pl.when(jnp.logical_and(pr == 0, jnp.logical_and(i == 0, j == 0)))(ride[0])

        @pl.when(j == 0)
        def _():
            m_sc[...] = jnp.full_like(m_sc, NEG)
            l_sc[...] = jnp.zeros_like(l_sc)
            acc_sc[...] = jnp.zeros_like(acc_sc)

        def compute(masked):
            qv, kv, vv = q_ref[...], k_ref[...], v_ref[...].astype(BF16)
            low, low_k = _low_half(qv.shape), _low_half(kv.shape)
            for a in range(2):
                if bias:
                    s = _dot(_with_decay(qv, low, a, cq_ref[a], True), _with_decay(kv, low_k, a, ck_ref[a], False), 1, 1)
                else:
                    s = _dot(_one_head(qv, low, a), kv, 1, 1)
                if masked:
                    row = i * tq + lax.broadcasted_iota(jnp.int32, (tq, tk), 0)
                    col = j * tk + lax.broadcasted_iota(jnp.int32, (tq, tk), 1)
                    s = jnp.where(col <= row, s, NEG)
                m_prev = m_sc[a]
                m_new = jnp.maximum(m_prev, jnp.max(s, axis=1, keepdims=True))
                alpha = jnp.exp2(m_prev - m_new)
                p = jnp.exp2(s - m_new)
                l_sc[a] = alpha * l_sc[a] + jnp.sum(p, axis=1, keepdims=True)
                acc_sc[a] = alpha * acc_sc[a] + _dot(p.astype(BF16), vv, 1, 0)
                m_sc[a] = m_new

        if causal:
            live = j * tk <= i * tq + (tq - 1)
            crosses = j * tk + (tk - 1) > i * tq
            pl.when(jnp.logical_and(live, crosses))(functools.partial(compute, True))
            pl.when(jnp.logical_and(live, jnp.logical_not(crosses)))(functools.partial(compute, False))
        else:
            compute(False)

        @pl.when(j == nk - 1)
        def _():
            low = _low_half((tq, LANES))
            o_ref[...] = jnp.where(low, acc_sc[0] / l_sc[0], acc_sc[1] / l_sc[1])
            for a in range(2):
                lse_ref[a] = m_sc[a] + jnp.log2(l_sc[a])

        if ride is not None:
            pl.when(jnp.logical_and(pr == n_pairs - 1, jnp.logical_and(i == nq - 1, j == nk - 1)))(ride[1])

    def kv_blk(i, j):
        return jnp.minimum(j, (i * tq + tq - 1) // tk) if causal else j

    in_specs = [pl.BlockSpec((tq, LANES), lambda p, i, j: (i, q0 + p)),
                pl.BlockSpec((tk, LANES), lambda p, i, j: (kv_blk(i, j), k0 + p)),
                pl.BlockSpec((tk, LANES), lambda p, i, j: (kv_blk(i, j), v0 + p))]
    args = [q, k, v]
    if bias:
        in_specs += [pl.BlockSpec((None, 2, tq, 1), lambda p, i, j: (p, 0, i, 0)),
                     pl.BlockSpec((None, 2, tk, 1), lambda p, i, j: (p, 0, kv_blk(i, j), 0))]
        args += [decay, decay]
    out = pl.pallas_call(
        body, grid=(n_pairs, nq, nk), in_specs=in_specs + r_in_specs,
        out_specs=[pl.BlockSpec((tq, LANES), lambda p, i, j: (i, p)),
                   pl.BlockSpec((None, 2, tq, 1), lambda p, i, j: (p, 0, i, 0))] + r_out_specs,
        out_shape=[jax.ShapeDtypeStruct((Sq, n_pairs * LANES), F32),
                   jax.ShapeDtypeStruct((n_pairs, 2, Sq, 1), F32)] + r_shapes,
        scratch_shapes=[pltpu.VMEM((2, tq, 1), F32), pltpu.VMEM((2, tq, 1), F32), pltpu.VMEM((2, tq, LANES), F32)]
        + (list(rider.sems) if rider else []),
        input_output_aliases=aliases,
        compiler_params=_params(("arbitrary",) * 3 if rider else ("parallel", "parallel", "arbitrary")),
        name=name)(*args, *r_args)
    return out[0], out[1], out[2:]


def attn_delta(o, do, do0, n_pairs, *, name):
    S = o.shape[0]
    ts = _tile(S, 1024, 16)

    def body(o_ref, do_ref, out_ref):
        prod = o_ref[...] * do_ref[...]
        low = _low_half(prod.shape)
        out_ref[0] = jnp.sum(jnp.where(low, prod, 0.0), axis=1, keepdims=True)
        out_ref[1] = jnp.sum(jnp.where(low, 0.0, prod), axis=1, keepdims=True)

    return pl.pallas_call(
        body, grid=(n_pairs, S // ts),
        in_specs=[pl.BlockSpec((ts, LANES), lambda p, i: (i, p)), pl.BlockSpec((ts, LANES), lambda p, i: (i, do0 + p))],
        out_specs=pl.BlockSpec((None, 2, ts, 1), lambda p, i: (p, 0, i, 0)),
        out_shape=jax.ShapeDtypeStruct((n_pairs, 2, S, 1), F32),
        compiler_params=_params(("parallel", "parallel")), name=name)(o, do)


def attn_bwd(q, q0, k, k0, v, v0, do, do0, n_pairs, lse, delta, decay, *, causal, rider=None, name):
    Sq, Sk = q.shape[0], k.shape[0]
    tq = _tile(Sq, ATTN_Q_BLOCK, LANES)
    tk = _tile(Sk, ATTN_K_BLOCK, LANES)
    nq, nk = Sq // tq, Sk // tk
    bias = decay is not None

    r_args, r_in_specs, r_shapes, r_out_specs, aliases, split = _with_rider(
        rider, 8 if bias else 6, 5 if bias else 3, 0)

    def body(*refs):
        ins, outs, _, ride = split(refs)
        q_ref, k_ref, v_ref, do_ref, lse_ref, dl_ref = ins[:6]
        cq_ref, ck_ref = (ins[6], ins[7]) if bias else (None, None)
        dq_ref, dk_ref, dv_ref = outs[:3]
        dcs_ref, drs_ref = (outs[3], outs[4]) if bias else (None, None)
        pr, j, i = pl.program_id(0), pl.program_id(1), pl.program_id(2)
        if ride is not None:
            pl.when(jnp.logical_and(pr == 0, jnp.logical_and(i == 0, j == 0)))(ride[0])

        @pl.when(i == 0)
        def _():
            dk_ref[...] = jnp.zeros_like(dk_ref)
            dv_ref[...] = jnp.zeros_like(dv_ref)
            if bias:
                dcs_ref[...] = jnp.zeros_like(dcs_ref)

        rows = pl.ds(pl.multiple_of(i * tq, tq), tq)

        def compute(masked):
            qv, kv, vv, dov = q_ref[...], k_ref[...], v_ref[...].astype(BF16), do_ref[...].astype(BF16)
            low, low_k = _low_half(qv.shape), _low_half(kv.shape)
            dq_part, row_parts = None, []
            for a in range(2):
                doa = _one_head(dov, low, a)
                if bias:
                    qa, ka = _with_decay(qv, low, a, cq_ref[a], True), _with_decay(kv, low_k, a, ck_ref[a], False)
                else:
                    qa, ka = _one_head(qv, low, a), _one_head(kv, low_k, a)
                s = _dot(qa, ka, 1, 1)
                p = jnp.exp2(s - lse_ref[a])
                if masked:
                    row = i * tq + lax.broadcasted_iota(jnp.int32, (tq, tk), 0)
                    col = j * tk + lax.broadcasted_iota(jnp.int32, (tq, tk), 1)
                    p = jnp.where(col <= row, p, 0.0)
                dv_ref[...] += _dot(p.astype(BF16), doa, 0, 0)
                dp = _dot(doa, vv, 1, 1)
                ds = p * (dp - dl_ref[a])
                dsb = ds.astype(BF16)
                dk_part = _dot(dsb, qa, 0, 0) * LN2
                part = _dot(dsb, ka, 1, 0) * QK_SCALE
                if bias:
                    dcs_ref[a] += jnp.sum(ds, axis=0, keepdims=True)
                    row_parts.append(jnp.sum(ds, axis=1, keepdims=True))
                    dk_part, part = _one_head(dk_part, low_k, a), _one_head(part, low, a)
                dk_ref[...] += dk_part
                dq_part = part if dq_part is None else dq_part + part

            @pl.when(j == 0)
            def _():
                dq_ref[rows, :] = dq_part
                for a, rp in enumerate(row_parts):
                    drs_ref[a, rows, :] = rp

            @pl.when(j > 0)
            def _():
                dq_ref[rows, :] += dq_part
                for a, rp in enumerate(row_parts):
                    drs_ref[a, rows, :] += rp

        if causal:
            live = j * tk <= i * tq + (tq - 1)
            crosses = j * tk + (tk - 1) > i * tq
            pl.when(jnp.logical_and(live, crosses))(functools.partial(compute, True))
            pl.when(jnp.logical_and(live, jnp.logical_not(crosses)))(functools.partial(compute, False))
        else:
            compute(False)

        if ride is not None:
            pl.when(jnp.logical_and(pr == n_pairs - 1, jnp.logical_and(i == nq - 1, j == nk - 1)))(ride[1])

    def q_blk(j, i):
        return jnp.maximum(i, (j * tk) // tq) if causal else i

    col1 = pl.BlockSpec((None, 2, tq, 1), lambda p, j, i: (p, 0, q_blk(j, i), 0))
    in_specs = [pl.BlockSpec((tq, LANES), lambda p, j, i: (q_blk(j, i), q0 + p)),
                pl.BlockSpec((tk, LANES), lambda p, j, i: (j, k0 + p)),
                pl.BlockSpec((tk, LANES), lambda p, j, i: (j, v0 + p)),
                pl.BlockSpec((tq, LANES), lambda p, j, i: (q_blk(j, i), do0 + p)), col1, col1]
    args = [q, k, v, do, lse, delta]
    kout = pl.BlockSpec((tk, LANES), lambda p, j, i: (j, p))
    out_specs = [pl.BlockSpec((Sq, LANES), lambda p, j, i: (0, p)), kout, kout]
    out_shape = [jax.ShapeDtypeStruct((Sq, n_pairs * LANES), F32), jax.ShapeDtypeStruct((Sk, n_pairs * LANES), F32),
                 jax.ShapeDtypeStruct((Sk, n_pairs * LANES), F32)]
    if bias:
        in_specs += [col1, pl.BlockSpec((None, 2, tk, 1), lambda p, j, i: (p, 0, j, 0))]
        args += [decay, decay]
        out_specs += [pl.BlockSpec((None, 2, 1, tk), lambda p, j, i: (p, 0, 0, j)),
                      pl.BlockSpec((None, 2, Sq, 1), lambda p, j, i: (p, 0, 0, 0))]
        out_shape += [jax.ShapeDtypeStruct((n_pairs, 2, 1, Sk), F32), jax.ShapeDtypeStruct((n_pairs, 2, Sq, 1), F32)]
    n_own = len(out_shape)
    out = pl.pallas_call(
        body, grid=(n_pairs, nk, nq), in_specs=in_specs + r_in_specs, out_specs=out_specs + r_out_specs,
        out_shape=out_shape + r_shapes, scratch_shapes=list(rider.sems) if rider else [],
        input_output_aliases=aliases,
        compiler_params=_params(("arbitrary",) * 3 if rider else ("parallel", "arbitrary", "arbitrary")),
        name=name)(*args, *r_args)
    return tuple(out[:n_own]), out[n_own:]


def _tril_mask(n):
    r = lax.broadcasted_iota(jnp.int32, (n, n), 0)
    c = lax.broadcasted_iota(jnp.int32, (n, n), 1)
    return c <= r


def _gmlp_operands(v_gain, w_s, b_s):
    G = w_s.shape[0]
    return (v_gain.reshape(G // 2, 1, LANES), w_s.reshape(G // 2, 2, CHUNK, CHUNK), b_s.reshape(G // 2, 2, CHUNK, 1))


def _gmlp_gate(wt, vh, b_ref, low):
    gate = _dot(wt[0], _one_head(vh, low, 0), 1, 0) + _dot(wt[1], _one_head(vh, low, 1), 1, 0)
    return gate + jnp.where(low, b_ref[0], b_ref[1])


def gmlp_fwd(proj, v0, n_pairs, vg, w, b, *, name):
    S = proj.shape[0]
    ts = _tile(S, 1024, CHUNK)

    def body(up_ref, vp_ref, vg_ref, w_ref, b_ref, o_ref):
        mask = _tril_mask(CHUNK)
        wt = [jnp.where(mask, w_ref[a], 0.0).astype(BF16) for a in range(2)]
        low = _low_half((CHUNK, LANES))
        for c in range(ts // CHUNK):
            sl = pl.ds(c * CHUNK, CHUNK)
            vz = _gelu(vp_ref[sl, :])
            r = lax.rsqrt(_half_sums(vz * vz, low) * (1.0 / HEAD_DIM) + EPS)
            vh = (vz * r * vg_ref[...]).astype(BF16)
            o_ref[sl, :] = _gelu(up_ref[sl, :]) * _gmlp_gate(wt, vh, b_ref, low)

    return pl.pallas_call(
        body, grid=(n_pairs, S // ts),
        in_specs=[pl.BlockSpec((ts, LANES), lambda p, i: (i, p)), pl.BlockSpec((ts, LANES), lambda p, i: (i, v0 + p)),
                  pl.BlockSpec((None, 1, LANES), lambda p, i: (p, 0, 0)),
                  pl.BlockSpec((None, 2, CHUNK, CHUNK), lambda p, i: (p, 0, 0, 0)),
                  pl.BlockSpec((None, 2, CHUNK, 1), lambda p, i: (p, 0, 0, 0))],
        out_specs=pl.BlockSpec((ts, LANES), lambda p, i: (i, p)),
        out_shape=jax.ShapeDtypeStruct((S, n_pairs * LANES), F32),
        compiler_params=_params(("parallel", "parallel")), name=name)(proj, proj, vg, w, b)


def gmlp_bwd(proj, v0, n_pairs, vg, w, wT, b, do, *, name):
    S = proj.shape[0]
    ts = _tile(S, 1024, CHUNK)

    def body(up_ref, vp_ref, vg_ref, w_ref, wT_ref, b_ref, do_ref, dup_ref, dvp_ref, dw_ref, db_ref, dvg_ref):
        i = pl.program_id(1)

        @pl.when(i == 0)
        def _():
            dw_ref[...] = jnp.zeros_like(dw_ref)
            db_ref[...] = jnp.zeros_like(db_ref)
            dvg_ref[...] = jnp.zeros_like(dvg_ref)

        mask = _tril_mask(CHUNK)
        wt = [jnp.where(mask, w_ref[a], 0.0).astype(BF16) for a in range(2)]
        wtT = [jnp.where(mask.T, wT_ref[a], 0.0).astype(BF16) for a in range(2)]
        low = _low_half((CHUNK, LANES))
        vgain = vg_ref[...]
        for c in range(ts // CHUNK):
            sl = pl.ds(c * CHUNK, CHUNK)
            u_pre, v_pre, dout = up_ref[sl, :], vp_ref[sl, :], do_ref[sl, :]
            vz = _gelu(v_pre)
            r = lax.rsqrt(_half_sums(vz * vz, low) * (1.0 / HEAD_DIM) + EPS)
            vh = (vz * r * vgain).astype(BF16)
            gate = _gmlp_gate(wt, vh, b_ref, low)
            dgate = dout * _gelu(u_pre)
            dup_ref[sl, :] = dout * gate * _gelu_grad(u_pre)
            dvh = None
            for a in range(2):
                dga = _one_head(dgate, low, a)
                dgb = dga.astype(BF16)
                dw_ref[a] += jnp.where(mask, _dot(dgb, vh, 1, 1), 0.0)
                db_ref[a] += jnp.sum(dga, axis=1, keepdims=True)
                part = _dot(wtT[a], dgb, 1, 0)
                dvh = part if dvh is None else dvh + part
            dvg_ref[...] += jnp.sum(dvh * vz * r, axis=0, keepdims=True)
            t = dvh * vgain
            dvz = r * t - vz * (r * r * r) * (_half_sums(vz * t, low) * (1.0 / HEAD_DIM))
            dvp_ref[sl, :] = dvz * _gelu_grad(v_pre)

    ublk = pl.BlockSpec((ts, LANES), lambda p, i: (i, p))
    wblk = pl.BlockSpec((None, 2, CHUNK, CHUNK), lambda p, i: (p, 0, 0, 0))
    bblk = pl.BlockSpec((None, 2, CHUNK, 1), lambda p, i: (p, 0, 0, 0))
    gblk = pl.BlockSpec((None, 1, LANES), lambda p, i: (p, 0, 0))
    return pl.pallas_call(
        body, grid=(n_pairs, S // ts),
        in_specs=[ublk, pl.BlockSpec((ts, LANES), lambda p, i: (i, v0 + p)), gblk, wblk, wblk, bblk, ublk],
        out_specs=[ublk, ublk, wblk, bblk, gblk],
        out_shape=[jax.ShapeDtypeStruct((S, n_pairs * LANES), F32), jax.ShapeDtypeStruct((S, n_pairs * LANES), F32),
                   jax.ShapeDtypeStruct((n_pairs, 2, CHUNK, CHUNK), F32), jax.ShapeDtypeStruct((n_pairs, 2, CHUNK, 1), F32),
                   jax.ShapeDtypeStruct((n_pairs, 1, LANES), F32)],
        compiler_params=_params(("parallel", "arbitrary")), name=name)(proj, proj, vg, w, wT, b, do)


def loss_head(y, target, *, name):
    S, D = y.shape
    ts = _tile(S, 512, 8)

    def body(y_ref, t_ref, dy_ref, loss_ref):
        i = pl.program_id(0)
        e = y_ref[...] - t_ref[...]
        dy_ref[...] = e * (1.0 / D)
        part = jnp.sum(jnp.sum(e * e, axis=1, keepdims=True), axis=0, keepdims=True) * (0.5 / D)

        @pl.when(i == 0)
        def _():
            loss_ref[...] = part

        @pl.when(i > 0)
        def _():
            loss_ref[...] += part

    row = pl.BlockSpec((ts, D), lambda i: (i, 0))
    return pl.pallas_call(
        body, grid=(S // ts,), in_specs=[row, row],
        out_specs=[row, pl.BlockSpec((1, 1), lambda i: (0, 0))],
        out_shape=[jax.ShapeDtypeStruct((S, D), F32), jax.ShapeDtypeStruct((1, 1), F32)],
        compiler_params=_params(("arbitrary",)), name=name)(y, target)


def adamw(w, g, m, v, *, name):
    shape = w.shape
    C = shape[-1]
    R = w.size // C
    tr = _tile(R, max(8, (256 * 1024) // C // 8 * 8), 8)

    def body(w_ref, g_ref, m_ref, v_ref, d_ref, nm_ref, nv_ref):
        gv = g_ref[...]
        nm = ADAM_B1 * m_ref[...] + (1.0 - ADAM_B1) * gv
        nv = ADAM_B2 * v_ref[...] + (1.0 - ADAM_B2) * (gv * gv)
        m_hat = nm / (1.0 - ADAM_B1 ** ADAM_STEP)
        v_hat = nv / (1.0 - ADAM_B2 ** ADAM_STEP)
        d_ref[...] = -ADAM_LR * (m_hat / (jnp.sqrt(v_hat) + ADAM_EPS) + ADAM_WD * w_ref[...])
        nm_ref[...] = nm
        nv_ref[...] = nv

    blk = pl.BlockSpec((tr, C), lambda i: (i, 0))
    out = pl.pallas_call(
        body, grid=(R // tr,), in_specs=[blk] * 4, out_specs=[blk] * 3,
        out_shape=[jax.ShapeDtypeStruct((R, C), F32)] * 3,
        compiler_params=_params(("parallel",)), name=name)(*(a.reshape(R, C) for a in (w, g, m, v)))
    return tuple(o.reshape(shape) for o in out)


def pair_sum(p, landed, half, *, name):
    n, R, C = landed.shape
    tr = _tile(R, 256, 16)
    nr = R // tr

    def body(half_ref, p_ref, l_ref, o_ref):
        o_ref[...] = (p_ref[...] + l_ref[...]).astype(BF16)

    return pl.pallas_call(
        body,
        grid_spec=pltpu.PrefetchScalarGridSpec(
            num_scalar_prefetch=1, grid=(n, nr),
            in_specs=[pl.BlockSpec((None, tr, C), lambda k, r, half_ref: (k, half_ref[0] * nr + r, 0)),
                      pl.BlockSpec((None, tr, C), lambda k, r, half_ref: (k, r, 0))],
            out_specs=pl.BlockSpec((None, tr, C), lambda k, r, half_ref: (k, r, 0))),
        out_shape=jax.ShapeDtypeStruct((n, R, C), BF16),
        compiler_params=_params(("parallel", "parallel")), name=name)(half, p, landed)


def chip_sum(own, landed, chip, *, name):
    n, R, C = own.shape
    tr = _tile(R, 256, 16)

    def body(chip_ref, own_ref, *rest):
        l_refs, o_ref = rest[:n], rest[n]
        me = chip_ref[0]
        acc = None
        for d in range(n):
            term = jnp.where(me == d, own_ref[...], l_refs[d][...]).astype(F32)
            acc = term if acc is None else acc + term
        o_ref[...] = acc

    def landed_spec(d):
        return pl.BlockSpec((None, tr, C), lambda r, chip_ref: (jnp.where(chip_ref[0] == d, (d + 1) % n, d), r, 0))

    return pl.pallas_call(
        body,
        grid_spec=pltpu.PrefetchScalarGridSpec(
            num_scalar_prefetch=1, grid=(R // tr,),
            in_specs=[pl.BlockSpec((None, tr, C), lambda r, chip_ref: (chip_ref[0], r, 0))]
            + [landed_spec(d) for d in range(n)],
            out_specs=pl.BlockSpec((tr, C), lambda r, chip_ref: (r, 0))),
        out_shape=jax.ShapeDtypeStruct((R, C), F32),
        compiler_params=_params(("parallel",)), name=name)(chip, own, *([landed] * n))


def ordered_sum(parts, *, name):
    n, R, C = parts.shape
    tr = _tile(R, 256, 16)

    def body(p_ref, o_ref):
        acc = p_ref[0].astype(F32)
        for d in range(1, n):
            acc = acc + p_ref[d].astype(F32)
        o_ref[...] = acc

    return pl.pallas_call(
        body, grid=(R // tr,), in_specs=[pl.BlockSpec((n, tr, C), lambda r: (0, r, 0))],
        out_specs=pl.BlockSpec((tr, C), lambda r: (r, 0)),
        out_shape=jax.ShapeDtypeStruct((R, C), F32),
        compiler_params=_params(("parallel",)), name=name)(parts)


_ANY = pl.BlockSpec(memory_space=pl.ANY)


def _position():
    return lax.axis_index("x"), lax.axis_index("y"), lax.axis_index("c")


def _remote(src, dst, send_sem, recv_sem, device):
    return pltpu.make_async_remote_copy(src_ref=src, dst_ref=dst, send_sem=send_sem, recv_sem=recv_sem,
                                        device_id=device, device_id_type=MESH_ID)


def _small_all_gather(s_ref, all_ref, send_sems, recv_sems, x, y, c):
    me = 4 * x + 2 * y + c
    copies = []
    for f in range(1, 8):
        peer = ((1 - x) if f & 4 else x, (1 - y) if f & 2 else y, (1 - c) if f & 1 else c)
        cp = _remote(s_ref, all_ref.at[me], send_sems.at[f - 1], recv_sems.at[f - 1], peer)
        cp.start()
        copies.append((cp, peer, f - 1))

    def finish():
        for cp, peer, s in copies:
            slot = all_ref.at[4 * peer[0] + 2 * peer[1] + peer[2]]
            _remote(slot, slot, send_sems.at[s], recv_sems.at[s], peer).wait_recv()
        for cp, _, _ in copies:
            cp.wait_send()

    return finish


def _core_rows(ref, core):
    h = ref.shape[1] // 2
    return pl.ds(core * h, h)


def _gather_plan(outs, send_sems, recv_sems):
    n = len(outs)
    x, y, c = _position()
    k = 2 * x + y
    sibling = (x, y, 1 - c)
    chips = [(1 - x, y), (x, 1 - y), (1 - x, 1 - y)]

    def first():
        return [_remote(outs[w].at[k, _core_rows(outs[w], c)], outs[w].at[k, _core_rows(outs[w], c)],
                        send_sems.at[w, j], recv_sems.at[w, j], (px, py, c))
                for j, (px, py) in enumerate(chips) for w in range(n)]

    def start():
        for cp in first():
            cp.start()

    def finish():
        passed = []
        for j, (px, py) in enumerate(chips):
            for w in range(n):
                slot = outs[w].at[2 * px + py, _core_rows(outs[w], c)]
                _remote(slot, slot, send_sems.at[w, j], recv_sems.at[w, j], (px, py, c)).wait_recv()
                cp = _remote(slot, slot, send_sems.at[w, 3 + j], recv_sems.at[w, 3 + j], sibling)
                cp.start()
                passed.append(cp)
        for j, (px, py) in enumerate(chips):
            for w in range(n):
                slot = outs[w].at[2 * px + py, _core_rows(outs[w], 1 - c)]
                _remote(slot, slot, send_sems.at[w, 3 + j], recv_sems.at[w, 3 + j], sibling).wait_recv()
        for cp in first() + passed:
            cp.wait_send()

    return start, finish


def _gather_sems(n):
    return (pltpu.SemaphoreType.DMA((n, 6)), pltpu.SemaphoreType.DMA((n, 6)))


def gather_rider(slabs):
    return Rider(tuple(slabs), tuple(jax.ShapeDtypeStruct(a.shape, a.dtype) for a in slabs),
                 {i: i for i in range(len(slabs))}, _gather_sems(len(slabs)),
                 lambda ins, outs, sems: _gather_plan(outs, sems[0], sems[1]))


def gather_weights(slabs, small_slab, *, name):
    n = len(slabs)

    def body(*refs):
        outs, all_ref = refs[n + 1:2 * n + 1], refs[2 * n + 1]
        send_sems, recv_sems, s_send, s_recv = refs[2 * n + 2:]
        x, y, c = _position()
        finish_small = _small_all_gather(all_ref.at[4 * x + 2 * y + c], all_ref, s_send, s_recv, x, y, c)
        start, finish = _gather_plan(outs, send_sems, recv_sems)
        start()
        finish()
        finish_small()

    args = list(slabs) + [small_slab]
    out = pl.pallas_call(
        body, in_specs=[_ANY] * (n + 1), out_specs=[_ANY] * (n + 1),
        out_shape=[jax.ShapeDtypeStruct(a.shape, a.dtype) for a in args],
        input_output_aliases={i: i for i in range(n + 1)},
        scratch_shapes=list(_gather_sems(n)) + [pltpu.SemaphoreType.DMA((7,)), pltpu.SemaphoreType.DMA((7,))],
        name=name)(*args)
    return out[:n], out[n]


def exchange_with_sibling(parts, small_slab, *, name):
    n = len(parts)
    has_small = small_slab is not None
    n_arg = n + (1 if has_small else 0)

    def body(*refs):
        p_refs = refs[:n]
        lands = refs[n_arg:n_arg + n]
        send_sems, recv_sems = refs[2 * n_arg], refs[2 * n_arg + 1]
        x, y, c = _position()
        sibling = (x, y, 1 - c)
        if has_small:
            all_ref = refs[n_arg + n]
            finish_small = _small_all_gather(all_ref.at[4 * x + 2 * y + c], all_ref, refs[2 * n_arg + 2],
                                             refs[2 * n_arg + 3], x, y, c)
        sends = []
        for w in range(n):
            for d in range(4):
                cp = _remote(p_refs[w].at[d, _core_rows(p_refs[w], 1 - c)], lands[w].at[d],
                             send_sems.at[w, d], recv_sems.at[w, d], sibling)
                cp.start()
                sends.append(cp)
        for cp in sends:
            cp.wait_recv()
        for cp in sends:
            cp.wait_send()
        if has_small:
            finish_small()

    small_args = [small_slab] if has_small else []
    out = pl.pallas_call(
        body, in_specs=[_ANY] * n_arg, out_specs=[_ANY] * n_arg,
        out_shape=[jax.ShapeDtypeStruct((4, p.shape[1] // 2, p.shape[2]), p.dtype) for p in parts]
        + [jax.ShapeDtypeStruct(s.shape, s.dtype) for s in small_args],
        input_output_aliases={n: n} if has_small else {},
        scratch_shapes=[pltpu.SemaphoreType.DMA((n, 4)), pltpu.SemaphoreType.DMA((n, 4))]
        + ([pltpu.SemaphoreType.DMA((7,)), pltpu.SemaphoreType.DMA((7,))] if has_small else []),
        name=name)(*parts, *small_args)
    return out[:n], (out[n] if has_small else None)


def _scatter_plan(q_refs, outs, send_sems, recv_sems):
    n = len(q_refs)
    x, y, c = _position()
    k = 2 * x + y
    chips = [(1 - x, y), (x, 1 - y), (1 - x, 1 - y)]

    def sends():
        return [_remote(q_refs[w].at[2 * px + py], outs[w].at[k], send_sems.at[w, j], recv_sems.at[w, j], (px, py, c))
                for j, (px, py) in enumerate(chips) for w in range(n)]

    def start():
        for cp in sends():
            cp.start()

    def finish():
        for j, (px, py) in enumerate(chips):
            for w in range(n):
                slot = outs[w].at[2 * px + py]
                _remote(slot, slot, send_sems.at[w, j], recv_sems.at[w, j], (px, py, c)).wait_recv()
        for cp in sends():
            cp.wait_send()

    return start, finish


def _scatter_sems(n):
    return (pltpu.SemaphoreType.DMA((n, 3)), pltpu.SemaphoreType.DMA((n, 3)))


def scatter_rider(parts):
    return Rider(tuple(parts), tuple(jax.ShapeDtypeStruct(q.shape, q.dtype) for q in parts), {},
                 _scatter_sems(len(parts)), lambda ins, outs, sems: _scatter_plan(ins, outs, sems[0], sems[1]))


def scatter_to_chips(parts, *, name):
    n = len(parts)

    def body(*refs):
        start, finish = _scatter_plan(refs[:n], refs[n:2 * n], refs[2 * n], refs[2 * n + 1])
        start()
        finish()

    return pl.pallas_call(
        body, in_specs=[_ANY] * n, out_specs=[_ANY] * n,
        out_shape=[jax.ShapeDtypeStruct(q.shape, q.dtype) for q in parts],
        scratch_shapes=list(_scatter_sems(n)), name=name)(*parts)


def share_with_sibling(parts, *, name):
    n = len(parts)

    def body(*refs):
        r_refs, outs = refs[:n], refs[n:2 * n]
        send_sems, recv_sems = refs[2 * n:]
        x, y, c = _position()
        sends = []
        for w in range(n):
            cp = _remote(r_refs[w], outs[w], send_sems.at[w], recv_sems.at[w], (x, y, 1 - c))
            cp.start()
            sends.append(cp)
        for cp in sends:
            cp.wait_recv()
        for cp in sends:
            cp.wait_send()

    return pl.pallas_call(
        body, in_specs=[_ANY] * n, out_specs=[_ANY] * n,
        out_shape=[jax.ShapeDtypeStruct(r.shape, r.dtype) for r in parts],
        scratch_shapes=[pltpu.SemaphoreType.DMA((n,)), pltpu.SemaphoreType.DMA((n,))],
        name=name)(*parts)


def _cols_to_chips(full):
    *lead, R, C4 = full.shape
    t = full.reshape(*lead, R, 4, C4 // 4)
    return jnp.moveaxis(t, -2, 0)


def _chips_to_cols(sh):
    t = jnp.moveaxis(sh, 0, -2)
    return t.reshape(*t.shape[:-2], t.shape[-2] * t.shape[-1])


def _slot_in_empty(own, index, n):
    return lax.dynamic_update_slice(lax.empty((n,) + own.shape, own.dtype), own[None], (index,) + (0,) * own.ndim)


def _fold_pair(dg):
    return dg[0, :HEAD_DIM] + dg[0, HEAD_DIM:]


def _ffn_fwd(x, g, w_in_slab, w_out, tag):
    h = rms_fwd(x, g, name=f"{tag}_rms")
    a, b, act = swiglu_fwd(h, w_in_slab, name=f"{tag}_in")
    y = matmul(act, w_out, res=x, scale=0.5, tm=1024, tn=512, tk=w_out.shape[0], name=f"{tag}_out")
    return y, (x, h, a, b, act)


def _ffn_bwd(dy, saved, g, w_in_slab, w_out, tag):
    x, h, a, b, act = saved
    da, db = swiglu_bwd(dy, w_out, a, b, name=f"{tag}_dact")
    dw_out = grad_rows(act, dy, scale=0.5, name=f"{tag}_dwout")
    dw_in = grad_cols(h, da, db, name=f"{tag}_dwin")
    dh = ffn_dh(da, db, w_in_slab, name=f"{tag}_dh")
    dx, dg = rms_bwd(x, dh, g, dy, name=f"{tag}_drms")
    return dx, dg[0], dw_in, dw_out


MEM_PAIRS = MEM_WIDTH // LANES


def _mem_attn_fwd(proj, mq0, mem_n, w_kv, g_q, g_k, tag):
    qh = pairnorm_fwd(proj, mq0, MEM_PAIRS, g_q, scale=Q_SCALE, name=f"{tag}_qnorm")
    kv = matmul(mem_n, w_kv, tm=256, tn=512, tk=1024, name=f"{tag}_kv")
    kh = pairnorm_fwd(kv, 0, MEM_PAIRS, g_k, name=f"{tag}_knorm")
    o, lse, _ = attn_fwd(qh, 0, kh, 0, kv, MEM_PAIRS, MEM_PAIRS, None, causal=False, name=f"{tag}_attn")
    return o, (qh, kv, kh, o, lse)


def _mem_attn_bwd(dmix, do0, proj, mq0, saved, mem_n, g_q, g_k, tag):
    qh, kv, kh, o, lse = saved
    delta = attn_delta(o, dmix, do0, MEM_PAIRS, name=f"{tag}_delta")
    (dqh, dkh, dv), _ = attn_bwd(qh, 0, kh, 0, kv, MEM_PAIRS, dmix, do0, MEM_PAIRS, lse, delta, None,
                                 causal=False, name=f"{tag}_dattn")
    dq_pre, dgq = pairnorm_bwd(proj, mq0, MEM_PAIRS, dqh, g_q, name=f"{tag}_dqnorm")
    dk_pre, dgk = pairnorm_bwd(kv, 0, MEM_PAIRS, dkh, g_k, name=f"{tag}_dknorm")
    dkv = jnp.concatenate([dk_pre, dv], axis=1)
    dw_kv = grad_rows(mem_n, dkv, name=f"{tag}_dwkv")
    return dq_pre, _fold_pair(dgq), _fold_pair(dgk), dw_kv, dkv


def _decay_terms(c, H):
    return c[:, :H].T.reshape(H // 2, 2, c.shape[0], 1)


def _per_head_lanes(x, H):
    return jnp.pad(x.reshape(H, -1).T, ((0, 0), (0, LANES - H)))


def _fox_fwd(proj, b_f, g_q, g_k, tok, rider, tag):
    H, P = tok // HEAD_DIM, tok // LANES
    bias = jnp.pad(b_f.reshape(1, H), ((0, 0), (0, LANES - H)))
    qh = pairnorm_fwd(proj, 0, P, g_q, scale=Q_SCALE, name=f"{tag}_qnorm")
    kh = pairnorm_fwd(proj, P, P, g_k, name=f"{tag}_knorm")
    decay = _decay_terms(fgate_fwd(proj, 3 * P + MEM_PAIRS, bias, out_scale=LOG2E, name=f"{tag}_fgate"), H)
    o, lse, rode = attn_fwd(qh, 0, kh, 0, proj, 2 * P, P, decay, causal=True, rider=rider, name=f"{tag}_attn")
    return o, (qh, kh, bias, decay, o, lse), rode


def _fox_bwd(dmix, proj, saved, g_q, g_k, tok, rider, tag):
    qh, kh, bias, decay, o, lse = saved
    H, P = tok // HEAD_DIM, tok // LANES
    delta = attn_delta(o, dmix, 0, P, name=f"{tag}_delta")
    (dqh, dkh, dv, dcs, drs), rode = attn_bwd(qh, 0, kh, 0, proj, 2 * P, dmix, 0, P, lse, delta, decay, causal=True,
                                              rider=rider, name=f"{tag}_dattn")
    dq_pre, dgq = pairnorm_bwd(proj, 0, P, dqh, g_q, name=f"{tag}_dqnorm")
    dk_pre, dgk = pairnorm_bwd(proj, P, P, dkh, g_k, name=f"{tag}_dknorm")
    dz, dbias = fgate_bwd(proj, 3 * P + MEM_PAIRS, bias, _per_head_lanes(drs, H), _per_head_lanes(dcs, H),
                          name=f"{tag}_dfgate")
    dqkv = jnp.concatenate([dq_pre, dk_pre, dv], axis=1)
    return dqkv, dz, dbias[0, :H], _fold_pair(dgq), _fold_pair(dgk), rode


def local_step(x, mem, target, W, comm=None):
    S, D = x.shape
    tok = D - MEM_WIDTH
    P = tok // LANES
    depth = W["norm_ffn1"].shape[0]
    mem_n = rms_fwd(mem, W["mem_norm"], name="mem_rms")
    saved = []
    for i in range(depth):
        kind, j = i % 2, i // 2
        t = f"l{i}"
        x1, s1 = _ffn_fwd(x, W["norm_ffn1"][i], W["ffn1_w_in"][i], W["ffn1_w_out"][i], f"{t}_ffn1")
        h = rms_fwd(x1, W["norm_mix"][i], name=f"{t}_mix_rms")
        w_mix = W["fox_w_in"][j] if kind == 0 else W["gmlp_w_in"][j]
        proj = matmul(h, w_mix, tm=1024, tn=896, tk=D, name=f"{t}_mix_in")
        if kind == 0:
            rider = comm.late_weights_rider() if (comm is not None and i == 0) else None
            o_tok, s_tok, rode = _fox_fwd(proj, W["fox_b_f"][j], W["fox_q_norm"][j], W["fox_k_norm"][j], tok, rider,
                                          f"{t}_fox")
            if rider is not None:
                comm.accept_late_weights(W, rode)
            mq0 = 3 * P
        else:
            vg, ws, bs = _gmlp_operands(W["gmlp_v_norm"][j], W["gmlp_w_s"][j], W["gmlp_b_s"][j])
            o_tok = gmlp_fwd(proj, P, P, vg, ws, bs, name=f"{t}_gmlp")
            s_tok = None
            mq0 = 2 * P
        o_mem, s_mem = _mem_attn_fwd(proj, mq0, mem_n, W["mem_w_kv"][i], W["mem_q_norm"][i], W["mem_k_norm"][i],
                                     f"{t}_mem")
        mix = jnp.concatenate([o_tok, o_mem], axis=1).astype(BF16)
        x2 = matmul(mix, W["w_out"][i], res=x1, tm=1024, tn=512, tk=D, name=f"{t}_mix_out")
        x3, s3 = _ffn_fwd(x2, W["norm_ffn2"][i], W["ffn2_w_in"][i], W["ffn2_w_out"][i], f"{t}_ffn2")
        saved.append((s1, x1, h, proj, mq0, s_tok, s_mem, mix, s3))
        x = x3

    dx, loss = loss_head(x, target, name="loss_head")

    G = {k: [None] * depth for k in ("norm_ffn1", "norm_mix", "norm_ffn2", "mem_q_norm", "mem_k_norm", "ffn1_w_in",
                                     "ffn1_w_out", "ffn2_w_in", "ffn2_w_out", "w_out", "mem_w_kv")}
    n_fox, n_gmlp = (depth + 1) // 2, depth // 2
    for k in ("fox_w_in", "fox_b_f", "fox_q_norm", "fox_k_norm"):
        G[k] = [None] * n_fox
    for k in ("gmlp_w_in", "gmlp_v_norm", "gmlp_w_s", "gmlp_b_s"):
        G[k] = [None] * n_gmlp
    dkv_all = [None] * depth
    for i in reversed(range(depth)):
        kind, j = i % 2, i // 2
        t = f"l{i}"
        s1, x1, h, proj, mq0, s_tok, s_mem, mix, s3 = saved[i]
        dx, G["norm_ffn2"][i], G["ffn2_w_in"][i], G["ffn2_w_out"][i] = _ffn_bwd(
            dx, s3, W["norm_ffn2"][i], W["ffn2_w_in"][i], W["ffn2_w_out"][i], f"{t}_ffn2")
        dmix = matmul(dx, W["w_out"][i], tb=True, tm=1024, tn=1024, tk=D, name=f"{t}_dmix")
        G["w_out"][i] = grad_rows(mix, dx, name=f"{t}_dwmixout")
        dmq, G["mem_q_norm"][i], G["mem_k_norm"][i], G["mem_w_kv"][i], dkv_all[i] = _mem_attn_bwd(
            dmix, P, proj, mq0, s_mem, mem_n, W["mem_q_norm"][i], W["mem_k_norm"][i], f"{t}_mem")
        if kind == 0:
            rider = comm.early_grads_rider(G) if (comm is not None and i == 0) else None
            dqkv, dz, G["fox_b_f"][j], G["fox_q_norm"][j], G["fox_k_norm"][j], rode = _fox_bwd(
                dmix, proj, s_tok, W["fox_q_norm"][j], W["fox_k_norm"][j], tok, rider, f"{t}_fox")
            if rider is not None:
                comm.accept_early_grads(rode)
            dproj = jnp.concatenate([dqkv, dmq, dz], axis=1).astype(BF16)
            w_mix, wkey = W["fox_w_in"][j], "fox_w_in"
        else:
            vg, ws, bs = _gmlp_operands(W["gmlp_v_norm"][j], W["gmlp_w_s"][j], W["gmlp_b_s"][j])
            dup, dvp, dws, dbs, dvg = gmlp_bwd(proj, P, P, vg, ws, jnp.swapaxes(ws, 2, 3), bs, dmix,
                                               name=f"{t}_dgmlp")
            G["gmlp_w_s"][j] = dws.reshape(W["gmlp_w_s"][j].shape)
            G["gmlp_b_s"][j] = dbs.reshape(W["gmlp_b_s"][j].shape)
            G["gmlp_v_norm"][j] = dvg.reshape(-1)
            dproj = jnp.concatenate([dup, dvp, dmq], axis=1).astype(BF16)
            w_mix, wkey = W["gmlp_w_in"][j], "gmlp_w_in"
        G[wkey][j] = matmul(h, dproj, ta=True, tm=1024, tn=896, tk=1024, name=f"{t}_dwmixin")
        dh = matmul(dproj, w_mix, tb=True, tm=1024, tn=1024, tk=896, name=f"{t}_dhmix")
        dx, dgm = rms_bwd(x1, dh, W["norm_mix"][i], dx, name=f"{t}_dmixrms")
        G["norm_mix"][i] = dgm[0]
        dx, G["norm_ffn1"][i], G["ffn1_w_in"][i], G["ffn1_w_out"][i] = _ffn_bwd(
            dx, s1, W["norm_ffn1"][i], W["ffn1_w_in"][i], W["ffn1_w_out"][i], f"{t}_ffn1")
    w_kv_all = jnp.concatenate([W["mem_w_kv"][i] for i in range(depth)], axis=1)
    dmem_n = matmul(jnp.concatenate(dkv_all, axis=1), w_kv_all, tb=True, tm=256, tn=512, tk=1024, name="dmem_n")
    _, dmemg = rms_bwd(mem, dmem_n, W["mem_norm"], None, name="dmem_rms")
    G["mem_norm"] = [dmemg[0]]
    return loss, dx, G


def _fox_cols_to_compute(w, tok):
    H = tok // HEAD_DIM
    qkv, f, mq = w[..., :3 * tok], w[..., 3 * tok:3 * tok + H], w[..., 3 * tok + H:]
    f = jnp.pad(f, [(0, 0)] * (w.ndim - 1) + [(0, LANES - H)])
    return jnp.concatenate([qkv, mq, f], axis=-1)


def _fox_cols_from_compute(w, tok):
    H = tok // HEAD_DIM
    qkv, mq, f = w[..., :3 * tok], w[..., 3 * tok:3 * tok + MEM_WIDTH], w[..., 3 * tok + MEM_WIDTH:3 * tok + MEM_WIDTH + H]
    return jnp.concatenate([qkv, f, mq], axis=-1)


_BIG = ("ffn1_w_in", "ffn1_w_out", "ffn2_w_in", "ffn2_w_out", "w_out", "mem_w_kv", "fox_w_in", "gmlp_w_in")
_SMALL = ("norm_ffn1", "norm_mix", "norm_ffn2", "mem_norm", "mem_q_norm", "mem_k_norm", "fox_b_f", "fox_q_norm",
          "fox_k_norm", "gmlp_v_norm", "gmlp_w_s", "gmlp_b_s")
WEIGHT_ORDER = ("norm_ffn1", "ffn1_w_in", "ffn1_w_out", "norm_mix", "norm_ffn2", "ffn2_w_in", "ffn2_w_out", "w_out",
                "mem_norm", "mem_w_kv", "mem_q_norm", "mem_k_norm", "fox_w_in", "fox_b_f", "fox_q_norm", "fox_k_norm",
                "gmlp_w_in", "gmlp_v_norm", "gmlp_w_s", "gmlp_b_s")


def _small_slab(rows_list, index):
    sizes = [s.shape[0] for s in rows_list]
    n_rows = [-(-n // LANES) for n in sizes]
    small = jnp.concatenate([jnp.pad(s, (0, r * LANES - n)).reshape(r, LANES)
                             for s, n, r in zip(rows_list, sizes, n_rows)], axis=0)
    small = jnp.pad(small, ((0, -small.shape[0] % 64), (0, 0)))
    return _slot_in_empty(small, index, 8), sizes, n_rows


_FIRST_WEIGHTS = (("ffn1_w_in", 0), ("ffn1_w_out", 0), ("fox_w_in", 0))


def _weight_from_slab(name, slab, tok):
    if name in ("ffn1_w_in", "ffn2_w_in"):
        return slab
    if name == "fox_w_in":
        return _fox_cols_to_compute(_chips_to_cols(slab), tok)
    if name == "gmlp_w_in":
        return _chips_to_cols(slab)
    return slab.reshape(4 * slab.shape[1], slab.shape[2])


def _grad_to_slab(name, g, tok):
    if name == "fox_w_in":
        return _cols_to_chips(_fox_cols_from_compute(g, tok))
    if name == "gmlp_w_in":
        return _cols_to_chips(g)
    return g


class _Exchange:
    def __init__(self, shards, tok, chip, core):
        self.tok, self.core = tok, core
        self.half = core.reshape(1).astype(jnp.int32)
        self.chip_id = chip.reshape(1).astype(jnp.int32)
        items = [(k, i) for k in _BIG for i in range(shards[k].shape[0])]
        self.slabs = {it: _slot_in_empty(shards[it[0]][it[1]].astype(BF16), chip, 4) for it in items}
        self.late = [it for it in items if it not in _FIRST_WEIGHTS]
        self.reduced = {}
        self.early = None

    def first_weights(self, small_slab):
        got, small_all = gather_weights([self.slabs[it] for it in _FIRST_WEIGHTS], small_slab, name="gather_first")
        return {it: _weight_from_slab(it[0], s, self.tok) for it, s in zip(_FIRST_WEIGHTS, got)}, small_all

    def late_weights_rider(self):
        return gather_rider([self.slabs[it] for it in self.late])

    def accept_late_weights(self, W, got):
        for (k, i), s in zip(self.late, got):
            W[k][i] = _weight_from_slab(k, s, self.tok)

    def _pair_sums(self, G, items, small_slab, tag):
        parts = [_grad_to_slab(k, G[k][i], self.tok) for k, i in items]
        landed, small_all = exchange_with_sibling(parts, small_slab, name=f"grad_exchange_{tag}")
        pair = [pair_sum(p, l, self.half, name=f"grad_pair_sum_{k}{i}") for (k, i), p, l in zip(items, parts, landed)]
        return pair, small_all

    def early_grads_rider(self, G):
        items = [(k, i) for k in _BIG for i in range(len(G[k])) if G[k][i] is not None]
        pair, _ = self._pair_sums(G, items, None, "early")
        self.early = (items, pair)
        return scatter_rider(pair)

    def accept_early_grads(self, landed):
        items, pair = self.early
        self._chip_sums(items, pair, landed)

    def _chip_sums(self, items, pair, landed):
        for (k, i), q, l in zip(items, pair, landed):
            self.reduced[(k, i)] = chip_sum(q, l, self.chip_id, name=f"grad_chip_sum_{k}{i}")

    def finish_grads(self, G, small_slab):
        items = [(k, i) for k in _BIG for i in range(len(G[k])) if (k, i) not in self.reduced]
        pair, small_all = self._pair_sums(G, items, small_slab, "late")
        self._chip_sums(items, pair, scatter_to_chips(pair, name="grad_scatter_late"))
        order = sorted(self.reduced)
        other = share_with_sibling([self.reduced[it] for it in order], name="grad_share")
        full = {}
        for it, a, b in zip(order, [self.reduced[it] for it in order], other):
            full[it] = jnp.where(self.core == 0, jnp.concatenate([a, b]), jnp.concatenate([b, a]))
        names = sorted({k for k, _ in order})
        return {k: jnp.stack([full[(k, i)] for i in range(len(G[k]))]) for k in names}, small_all


def kernel(x, mem, norm_ffn1, ffn1_w_in, ffn1_w_out, norm_mix, norm_ffn2, ffn2_w_in, ffn2_w_out, w_out, mem_norm, mem_w_kv, mem_q_norm, mem_k_norm, fox_w_in, fox_b_f, fox_q_norm, fox_k_norm, gmlp_w_in, gmlp_v_norm, gmlp_w_s, gmlp_b_s, loss_target, m_norm_ffn1, m_ffn1_w_in, m_ffn1_w_out, m_norm_mix, m_norm_ffn2, m_ffn2_w_in, m_ffn2_w_out, m_w_out, m_mem_norm, m_mem_w_kv, m_mem_q_norm, m_mem_k_norm, m_fox_w_in, m_fox_b_f, m_fox_q_norm, m_fox_k_norm, m_gmlp_w_in, m_gmlp_v_norm, m_gmlp_w_s, m_gmlp_b_s, v_norm_ffn1, v_ffn1_w_in, v_ffn1_w_out, v_norm_mix, v_norm_ffn2, v_ffn2_w_in, v_ffn2_w_out, v_w_out, v_mem_norm, v_mem_w_kv, v_mem_q_norm, v_mem_k_norm, v_fox_w_in, v_fox_b_f, v_fox_q_norm, v_fox_k_norm, v_gmlp_w_in, v_gmlp_v_norm, v_gmlp_w_s, v_gmlp_b_s):
    w = dict(norm_ffn1=norm_ffn1, ffn1_w_in=ffn1_w_in, ffn1_w_out=ffn1_w_out, norm_mix=norm_mix, norm_ffn2=norm_ffn2,
             ffn2_w_in=ffn2_w_in, ffn2_w_out=ffn2_w_out, w_out=w_out, mem_norm=mem_norm, mem_w_kv=mem_w_kv,
             mem_q_norm=mem_q_norm, mem_k_norm=mem_k_norm, fox_w_in=fox_w_in, fox_b_f=fox_b_f, fox_q_norm=fox_q_norm,
             fox_k_norm=fox_k_norm, gmlp_w_in=gmlp_w_in, gmlp_v_norm=gmlp_v_norm, gmlp_w_s=gmlp_w_s, gmlp_b_s=gmlp_b_s)
    m = dict(norm_ffn1=m_norm_ffn1, ffn1_w_in=m_ffn1_w_in, ffn1_w_out=m_ffn1_w_out, norm_mix=m_norm_mix,
             norm_ffn2=m_norm_ffn2, ffn2_w_in=m_ffn2_w_in, ffn2_w_out=m_ffn2_w_out, w_out=m_w_out, mem_norm=m_mem_norm,
             mem_w_kv=m_mem_w_kv, mem_q_norm=m_mem_q_norm, mem_k_norm=m_mem_k_norm, fox_w_in=m_fox_w_in,
             fox_b_f=m_fox_b_f, fox_q_norm=m_fox_q_norm, fox_k_norm=m_fox_k_norm, gmlp_w_in=m_gmlp_w_in,
             gmlp_v_norm=m_gmlp_v_norm, gmlp_w_s=m_gmlp_w_s, gmlp_b_s=m_gmlp_b_s)
    v = dict(norm_ffn1=v_norm_ffn1, ffn1_w_in=v_ffn1_w_in, ffn1_w_out=v_ffn1_w_out, norm_mix=v_norm_mix,
             norm_ffn2=v_norm_ffn2, ffn2_w_in=v_ffn2_w_in, ffn2_w_out=v_ffn2_w_out, w_out=v_w_out, mem_norm=v_mem_norm,
             mem_w_kv=v_mem_w_kv, mem_q_norm=v_mem_q_norm, mem_k_norm=v_mem_k_norm, fox_w_in=v_fox_w_in,
             fox_b_f=v_fox_b_f, fox_q_norm=v_fox_q_norm, fox_k_norm=v_fox_k_norm, gmlp_w_in=v_gmlp_w_in,
             gmlp_v_norm=v_gmlp_v_norm, gmlp_w_s=v_gmlp_w_s, gmlp_b_s=v_gmlp_b_s)
    D = x.shape[-1]
    tok = D - MEM_WIDTH
    xi, yi, ci = _position()
    chip = 2 * xi + yi

    device = 4 * xi + 2 * yi + ci

    comm = _Exchange(w, tok, chip, ci)
    vn = w["gmlp_v_norm"]
    vn_slab, _, _ = _small_slab([vn.reshape(-1)], device)
    first, vn_all = comm.first_weights(vn_slab)
    W = {k: w[k] for k in _SMALL}
    W["gmlp_v_norm"] = _chips_to_cols(vn_all[0::2].reshape(4, -1)[:, :vn.size].reshape((4,) + vn.shape))
    for k in _BIG:
        W[k] = [first.get((k, i)) for i in range(w[k].shape[0])]

    loss, grad_x, g = local_step(x[0], mem[0], loss_target[0], W, comm)

    small_list = [jnp.stack(g[k]).reshape(-1) for k in _SMALL] + [loss.reshape(-1)]
    small, small_sizes, small_rows = _small_slab(small_list, device)
    red, small_all = comm.finish_grads(g, small)
    small_sum = ordered_sum(small_all, name="small_sum")
    off = 0
    for k, n, r in zip(_SMALL, small_sizes, small_rows):
        red[k] = small_sum[off:off + r].reshape(-1)[:n].reshape((-1,) + w[k].shape[1:] if k != "gmlp_v_norm"
                                                                else (w[k].shape[0], -1))
        off += r
    loss_total = small_sum[off, 0]
    vn_cols = w["gmlp_v_norm"].shape[-1]
    red["gmlp_v_norm"] = lax.dynamic_slice_in_dim(red["gmlp_v_norm"], chip * vn_cols, vn_cols, axis=-1)

    deltas, new_m, new_v = {}, {}, {}
    for k in WEIGHT_ORDER:
        wk = w[k] if w[k].ndim > 1 else w[k].reshape(1, -1)
        upd = adamw(wk, red[k].reshape(wk.shape), m[k].reshape(wk.shape), v[k].reshape(wk.shape), name=f"adamw_{k}")
        deltas[k], new_m[k], new_v[k] = (u.reshape(w[k].shape) for u in upd)
    return (loss_total, grad_x[None], *[red[k].reshape(w[k].shape) for k in WEIGHT_ORDER],
            *[deltas[k] for k in WEIGHT_ORDER], *[new_m[k] for k in WEIGHT_ORDER], *[new_v[k] for k in WEIGHT_ORDER])
```

```python
import functools
import math
from typing import Callable, NamedTuple

import jax
import jax.numpy as jnp
from jax import lax
from jax.experimental import pallas as pl
from jax.experimental.pallas import tpu as pltpu

F32 = jnp.float32
BF16 = jnp.bfloat16
EPS = 1e-6
HEAD_DIM = 64
MEM_WIDTH = 256
CHUNK = 128
LANES = 128
NEG = -1e30
VMEM_LIMIT_BYTES = 56 * 1024 * 1024
ATTN_Q_BLOCK = 1024
ATTN_K_BLOCK = 1024
QK_SCALE = 0.125
LOG2E = 1.4426950408889634
LN2 = 0.6931471805599453
Q_SCALE = QK_SCALE * LOG2E
MESH_ID = pl.DeviceIdType.MESH

ADAM_LR = 0.001
ADAM_B1 = 0.9
ADAM_B2 = 0.999
ADAM_EPS = 1e-08
ADAM_WD = 0.01
ADAM_STEP = 10


def _tile(n, pref, align):
    t = (min(pref, n) // align) * align
    while t >= align:
        if n % t == 0:
            return t
        t -= align
    return n


def _params(sem):
    return pltpu.CompilerParams(dimension_semantics=sem, vmem_limit_bytes=VMEM_LIMIT_BYTES)


def _dot(a, b, ca, cb):
    return lax.dot_general(a, b, (((ca,), (cb,)), ((), ())), preferred_element_type=F32)


def _sigmoid(x):
    return 1.0 / (1.0 + jnp.exp(-x))


_GELU_C = math.sqrt(2.0 / math.pi)


def _gelu(x):
    return 0.5 * x * (1.0 + jnp.tanh(_GELU_C * (x + 0.044715 * (x * x * x))))


def _gelu_grad(x):
    t = jnp.tanh(_GELU_C * (x + 0.044715 * (x * x * x)))
    return 0.5 * (1.0 + t) + 0.5 * x * (1.0 - t * t) * (_GELU_C * (1.0 + 3.0 * 0.044715 * (x * x)))


def matmul(a, b, *, ta=False, tb=False, out_dtype=F32, scale=None, res=None,
           tm=1024, tn=512, tk=1024, name):
    if ta:
        K, M = a.shape
    else:
        M, K = a.shape
    N = b.shape[0] if tb else b.shape[1]
    tm = _tile(M, tm, LANES if ta else 16)
    tn = _tile(N, tn, LANES)
    tk = _tile(K, tk, LANES)
    nk = K // tk
    a_spec = pl.BlockSpec((tk, tm), lambda i, j, k: (k, i)) if ta else pl.BlockSpec((tm, tk), lambda i, j, k: (i, k))
    b_spec = pl.BlockSpec((tn, tk), lambda i, j, k: (j, k)) if tb else pl.BlockSpec((tk, tn), lambda i, j, k: (k, j))
    o_spec = pl.BlockSpec((tm, tn), lambda i, j, k: (i, j))
    ca, cb = (0 if ta else 1), (1 if tb else 0)
    has_res = res is not None

    def body(*refs):
        a_ref, b_ref = refs[0], refs[1]
        res_ref = refs[2] if has_res else None
        o_ref = refs[3] if has_res else refs[2]
        acc_ref = refs[-1]
        k = pl.program_id(2)
        prod = _dot(a_ref[...].astype(BF16), b_ref[...].astype(BF16), ca, cb)

        def finish(acc):
            if scale is not None:
                acc = acc * scale
            if has_res:
                acc = res_ref[...] + acc
            o_ref[...] = acc.astype(out_dtype)

        if nk == 1:
            finish(prod)
        else:
            @pl.when(k == 0)
            def _():
                acc_ref[...] = prod

            @pl.when(k > 0)
            def _():
                acc_ref[...] += prod

            @pl.when(k == nk - 1)
            def _():
                finish(acc_ref[...])

    in_specs = [a_spec, b_spec] + ([o_spec] if has_res else [])
    args = (a, b) + ((res,) if has_res else ())
    return pl.pallas_call(
        body, grid=(M // tm, N // tn, nk), in_specs=in_specs, out_specs=o_spec,
        out_shape=jax.ShapeDtypeStruct((M, N), out_dtype),
        scratch_shapes=[pltpu.VMEM((tm, tn) if nk > 1 else (8, LANES), F32)],
        compiler_params=_params(("parallel", "parallel", "arbitrary")), name=name)(*args)


def swiglu_fwd(h, w_slab, *, name):
    S, D = h.shape
    Fc = w_slab.shape[-1]
    tm = _tile(S, 512, 16)

    def body(h_ref, wa_ref, wb_ref, a_ref, b_ref, act_ref):
        hv = h_ref[...]
        a = _dot(hv, wa_ref[...], 1, 0)
        b = _dot(hv, wb_ref[...], 1, 0)
        a_ref[...] = a.astype(BF16)
        b_ref[...] = b.astype(BF16)
        act_ref[...] = (a * _sigmoid(a) * b).astype(BF16)

    out = pl.BlockSpec((tm, Fc), lambda j, i: (i, j))
    return pl.pallas_call(
        body, grid=(2, S // tm),
        in_specs=[pl.BlockSpec((tm, D), lambda j, i: (i, 0)),
                  pl.BlockSpec((None, D, Fc), lambda j, i: (j, 0, 0)),
                  pl.BlockSpec((None, D, Fc), lambda j, i: (j + 2, 0, 0))],
        out_specs=[out, out, out],
        out_shape=[jax.ShapeDtypeStruct((S, 2 * Fc), BF16)] * 3,
        compiler_params=_params(("parallel", "parallel")), name=name)(h, w_slab, w_slab)


def swiglu_bwd(dy, w_out, a, b, *, name):
    S, D = dy.shape
    F = w_out.shape[0]
    fc = F // 2
    tm = _tile(S, 512, 16)

    def body(dy_ref, w_ref, a_ref, b_ref, da_ref, db_ref):
        dact = 0.5 * _dot(dy_ref[...].astype(BF16), w_ref[...], 1, 1)
        av = a_ref[...].astype(F32)
        sg = _sigmoid(av)
        da_ref[...] = (dact * b_ref[...].astype(F32) * (sg * (1.0 + av * (1.0 - sg)))).astype(BF16)
        db_ref[...] = (dact * (av * sg)).astype(BF16)

    blk = pl.BlockSpec((tm, fc), lambda j, i: (i, j))
    return pl.pallas_call(
        body, grid=(2, S // tm),
        in_specs=[pl.BlockSpec((tm, D), lambda j, i: (i, 0)), pl.BlockSpec((fc, D), lambda j, i: (j, 0)), blk, blk],
        out_specs=[blk, blk],
        out_shape=[jax.ShapeDtypeStruct((S, F), BF16), jax.ShapeDtypeStruct((S, F), BF16)],
        compiler_params=_params(("parallel", "parallel")), name=name)(dy, w_out, a, b)


def ffn_dh(da, db, w_slab, x, g, dy, *, name):
    S, F = da.shape
    D, Fc = w_slab.shape[-2:]
    tm = _tile(S, 1024, 16)
    sub = _tile(tm, 256, 8)

    def body(da_ref, db_ref, w_ref, x_ref, g_ref, dy_ref, dx_ref, dg_ref, acc_ref):
        i, k = pl.program_id(0), pl.program_id(1)

        @pl.when(k == 0)
        def _():
            acc_ref[...] = jnp.zeros_like(acc_ref)

        @pl.when(k < 2)
        def _():
            acc_ref[...] += _dot(da_ref[...], w_ref[...], 1, 1)

        @pl.when(k >= 2)
        def _():
            acc_ref[...] += _dot(db_ref[...], w_ref[...], 1, 1)

        @pl.when(k == 3)
        def _():
            part = None
            for c in range(tm // sub):
                rows = pl.ds(c * sub, sub)
                xv, dh = x_ref[rows, :], acc_ref[rows, :]
                r = lax.rsqrt(jnp.mean(xv * xv, axis=-1, keepdims=True) + EPS)
                u = dh * g_ref[...]
                dx_ref[rows, :] = dy_ref[rows, :] + (r * u - xv * (r * r * r) * jnp.mean(xv * u, axis=-1, keepdims=True))
                p = jnp.sum(dh * xv * r, axis=0, keepdims=True)
                part = p if part is None else part + p

            @pl.when(i == 0)
            def _():
                dg_ref[...] = part

            @pl.when(i > 0)
            def _():
                dg_ref[...] += part

    row = pl.BlockSpec((tm, D), lambda i, k: (i, 0))
    vec = pl.BlockSpec((1, D), lambda i, k: (0, 0))
    return pl.pallas_call(
        body, grid=(S // tm, 4),
        in_specs=[pl.BlockSpec((tm, Fc), lambda i, k: (i, jnp.minimum(k, 1))),
                  pl.BlockSpec((tm, Fc), lambda i, k: (i, jnp.maximum(k - 2, 0))),
                  pl.BlockSpec((None, D, Fc), lambda i, k: (k, 0, 0)), row, vec, row],
        out_specs=[row, vec],
        out_shape=[jax.ShapeDtypeStruct((S, D), F32), jax.ShapeDtypeStruct((1, D), F32)],
        scratch_shapes=[pltpu.VMEM((tm, D), F32)],
        compiler_params=_params(("arbitrary", "arbitrary")), name=name)(da, db, w_slab, x, g.reshape(1, D), dy)


def grad_cols(h, da, db, *, name):
    S, D = h.shape
    Fc = da.shape[1] // 2
    tk = _tile(S, 1024, 16)
    nk = S // tk

    def body(h_ref, da_ref, db_ref, o_ref, acc_ref):
        ch, k = pl.program_id(0), pl.program_id(1)

        @pl.when(k == 0)
        def _():
            acc_ref[...] = jnp.zeros_like(acc_ref)

        @pl.when(ch < 2)
        def _():
            acc_ref[...] += _dot(h_ref[...], da_ref[...], 0, 0)

        @pl.when(ch >= 2)
        def _():
            acc_ref[...] += _dot(h_ref[...], db_ref[...], 0, 0)

        @pl.when(k == nk - 1)
        def _():
            o_ref[...] = acc_ref[...]

    return pl.pallas_call(
        body, grid=(4, nk),
        in_specs=[pl.BlockSpec((tk, D), lambda ch, k: (k, 0)),
                  pl.BlockSpec((tk, Fc), lambda ch, k: (jnp.where(ch < 2, k, 0), jnp.minimum(ch, 1))),
                  pl.BlockSpec((tk, Fc), lambda ch, k: (jnp.where(ch >= 2, k, 0), jnp.maximum(ch - 2, 0)))],
        out_specs=pl.BlockSpec((None, D, Fc), lambda ch, k: (ch, 0, 0)),
        out_shape=jax.ShapeDtypeStruct((4, D, Fc), F32),
        scratch_shapes=[pltpu.VMEM((D, Fc), F32)],
        compiler_params=_params(("parallel", "arbitrary")), name=name)(h, da, db)


def grad_rows(a, b, *, scale=None, name):
    S, M = a.shape
    N = b.shape[1]
    R = M // 4
    tn = _tile(N, 512, LANES)
    tk = _tile(S, 1024, 16)
    nk = S // tk

    def body(a_ref, b_ref, o_ref, acc_ref):
        k = pl.program_id(1)

        @pl.when(k == 0)
        def _():
            acc_ref[...] = jnp.zeros_like(acc_ref)

        acc_ref[...] += _dot(a_ref[...].astype(BF16), b_ref[...].astype(BF16), 0, 0)

        @pl.when(k == nk - 1)
        def _():
            for d in range(4):
                part = acc_ref[d * R:(d + 1) * R, :]
                o_ref[d] = part if scale is None else part * scale

    return pl.pallas_call(
        body, grid=(N // tn, nk),
        in_specs=[pl.BlockSpec((tk, M), lambda j, k: (k, 0)), pl.BlockSpec((tk, tn), lambda j, k: (k, j))],
        out_specs=pl.BlockSpec((4, R, tn), lambda j, k: (0, 0, j)),
        out_shape=jax.ShapeDtypeStruct((4, R, N), F32),
        scratch_shapes=[pltpu.VMEM((M, tn), F32)],
        compiler_params=_params(("parallel", "arbitrary")), name=name)(a, b)


def rms_fwd(x, g, *, name):
    S, D = x.shape
    ts = _tile(S, 1024, 16)

    def body(x_ref, g_ref, h_ref):
        xv = x_ref[...]
        r = lax.rsqrt(jnp.mean(xv * xv, axis=-1, keepdims=True) + EPS)
        h_ref[...] = (xv * r * g_ref[...]).astype(BF16)

    return pl.pallas_call(
        body, grid=(S // ts,),
        in_specs=[pl.BlockSpec((ts, D), lambda i: (i, 0)), pl.BlockSpec((1, D), lambda i: (0, 0))],
        out_specs=pl.BlockSpec((ts, D), lambda i: (i, 0)),
        out_shape=jax.ShapeDtypeStruct((S, D), BF16),
        compiler_params=_params(("parallel",)), name=name)(x, g.reshape(1, D))


def rms_bwd(x, dh, g, res, *, name):
    S, D = x.shape
    ts = _tile(S, 512, 16)
    has_res = res is not None

    def body(*refs):
        x_ref, dh_ref, g_ref = refs[:3]
        res_ref = refs[3] if has_res else None
        dx_ref, dg_ref = refs[-2:]
        i = pl.program_id(0)
        xv, dhv = x_ref[...], dh_ref[...].astype(F32)
        r = lax.rsqrt(jnp.mean(xv * xv, axis=-1, keepdims=True) + EPS)
        u = dhv * g_ref[...]
        dx = r * u - xv * (r * r * r) * jnp.mean(xv * u, axis=-1, keepdims=True)
        if has_res:
            dx = res_ref[...] + dx
        dx_ref[...] = dx
        part = jnp.sum(dhv * xv * r, axis=0, keepdims=True)

        @pl.when(i == 0)
        def _():
            dg_ref[...] = part

        @pl.when(i > 0)
        def _():
            dg_ref[...] += part

    row = pl.BlockSpec((ts, D), lambda i: (i, 0))
    vec = pl.BlockSpec((1, D), lambda i: (0, 0))
    args = (x, dh, g.reshape(1, D)) + ((res,) if has_res else ())
    return pl.pallas_call(
        body, grid=(S // ts,), in_specs=[row, row, vec] + ([row] if has_res else []),
        out_specs=[row, vec],
        out_shape=[jax.ShapeDtypeStruct((S, D), F32), jax.ShapeDtypeStruct((1, D), F32)],
        compiler_params=_params(("arbitrary",)), name=name)(*args)


def _low_half(shape):
    return lax.broadcasted_iota(jnp.int32, shape, len(shape) - 1) < HEAD_DIM


def _half_sums(x, low):
    sa = jnp.sum(jnp.where(low, x, 0.0), axis=1, keepdims=True)
    sb = jnp.sum(jnp.where(low, 0.0, x), axis=1, keepdims=True)
    return jnp.where(low, sa, sb)


def pairnorm_fwd(x, col0, n_pairs, g, *, scale=None, name):
    S = x.shape[0]
    ts = _tile(S, 1024, 16)

    def body(x_ref, g_ref, o_ref):
        xv = x_ref[...]
        r = lax.rsqrt(_half_sums(xv * xv, _low_half(xv.shape)) * (1.0 / HEAD_DIM) + EPS)
        y = xv * r * g_ref[...]
        o_ref[...] = (y if scale is None else y * scale).astype(BF16)

    return pl.pallas_call(
        body, grid=(S // ts, n_pairs),
        in_specs=[pl.BlockSpec((ts, LANES), lambda i, j: (i, col0 + j)), pl.BlockSpec((1, LANES), lambda i, j: (0, 0))],
        out_specs=pl.BlockSpec((ts, LANES), lambda i, j: (i, j)),
        out_shape=jax.ShapeDtypeStruct((S, n_pairs * LANES), BF16),
        compiler_params=_params(("parallel", "parallel")), name=name)(x, jnp.tile(g.reshape(1, HEAD_DIM), (1, 2)))


def pairnorm_bwd(x, col0, n_pairs, dy, g, *, name):
    S = x.shape[0]
    ts = _tile(S, 1024, 16)

    def body(x_ref, dy_ref, g_ref, dx_ref, dg_ref):
        first = jnp.logical_and(pl.program_id(0) == 0, pl.program_id(1) == 0)
        xv, dyv = x_ref[...], dy_ref[...]
        low = _low_half(xv.shape)
        r = lax.rsqrt(_half_sums(xv * xv, low) * (1.0 / HEAD_DIM) + EPS)
        u = dyv * g_ref[...]
        dx_ref[...] = r * u - xv * (r * r * r) * (_half_sums(xv * u, low) * (1.0 / HEAD_DIM))
        part = jnp.sum(dyv * xv * r, axis=0, keepdims=True)

        @pl.when(first)
        def _():
            dg_ref[...] = part

        @pl.when(jnp.logical_not(first))
        def _():
            dg_ref[...] += part

    vec = pl.BlockSpec((1, LANES), lambda i, j: (0, 0))
    blk = pl.BlockSpec((ts, LANES), lambda i, j: (i, j))
    return pl.pallas_call(
        body, grid=(S // ts, n_pairs),
        in_specs=[pl.BlockSpec((ts, LANES), lambda i, j: (i, col0 + j)), blk, vec], out_specs=[blk, vec],
        out_shape=[jax.ShapeDtypeStruct((S, n_pairs * LANES), F32), jax.ShapeDtypeStruct((1, LANES), F32)],
        compiler_params=_params(("arbitrary", "arbitrary")), name=name)(x, dy, jnp.tile(g.reshape(1, HEAD_DIM), (1, 2)))


def _split3(x):
    x1 = x.astype(BF16)
    r1 = x - x1.astype(F32)
    x2 = r1.astype(BF16)
    x3 = (r1 - x2.astype(F32)).astype(BF16)
    return x1, x2, x3


def _tri_ones(n, lower):
    r = lax.broadcasted_iota(jnp.int32, (n, n), 0)
    c = lax.broadcasted_iota(jnp.int32, (n, n), 1)
    return jnp.where((c <= r) if lower else (c >= r), 1.0, 0.0).astype(BF16)


def fgate_fwd(z, col0, bias, n_heads, *, out_scale, name):
    S, L = z.shape[0], LANES
    tb = _tile(S, 256, 16)

    def body(z_ref, b_ref, c_ref, col_ref, carry):
        i = pl.program_id(0)

        @pl.when(i == 0)
        def _():
            carry[...] = jnp.zeros_like(carry)

        zz = z_ref[...] + b_ref[...]
        lf = jnp.minimum(zz, 0.0) - jnp.log(1.0 + jnp.exp(-jnp.abs(zz)))
        tri = _tri_ones(tb, True)
        x1, x2, x3 = _split3(lf)
        c = ((_dot(tri, x1, 1, 0) + _dot(tri, x2, 1, 0)) + _dot(tri, x3, 1, 0) + carry[...]) * out_scale
        c_ref[...] = c
        lane = lax.broadcasted_iota(jnp.int32, c.shape, 1)
        for h in range(n_heads):
            col_ref[h // 2, h % 2] = jnp.sum(jnp.where(lane == h, c, 0.0), axis=1, keepdims=True)
        carry[...] += jnp.sum(lf, axis=0, keepdims=True)

    return pl.pallas_call(
        body, grid=(S // tb,),
        in_specs=[pl.BlockSpec((tb, L), lambda i: (i, col0)), pl.BlockSpec((1, L), lambda i: (0, 0))],
        out_specs=[pl.BlockSpec((tb, L), lambda i: (i, 0)),
                   pl.BlockSpec((n_heads // 2, 2, tb, 1), lambda i: (0, 0, i, 0))],
        out_shape=[jax.ShapeDtypeStruct((S, L), F32), jax.ShapeDtypeStruct((n_heads // 2, 2, S, 1), F32)],
        scratch_shapes=[pltpu.VMEM((1, L), F32)],
        compiler_params=_params(("arbitrary",)), name=name)(z, bias)


def fgate_bwd(z, col0, bias, drs, dcs, *, name):
    S, L = z.shape[0], LANES
    tb = _tile(S, 256, 16)
    nb = S // tb

    def body(z_ref, b_ref, drs_ref, dcs_ref, dz_ref, db_ref, carry):
        i = pl.program_id(0)

        @pl.when(i == 0)
        def _():
            carry[...] = jnp.zeros_like(carry)

        tri = _tri_ones(tb, False)
        dc = drs_ref[...] - dcs_ref[...]
        x1, x2, x3 = _split3(dc)
        dlf = (_dot(tri, x1, 1, 0) + _dot(tri, x2, 1, 0)) + _dot(tri, x3, 1, 0) + carry[...]
        carry[...] += jnp.sum(dc, axis=0, keepdims=True)
        dz = dlf * _sigmoid(-(z_ref[...] + b_ref[...]))
        dz_ref[...] = dz
        part = jnp.sum(dz, axis=0, keepdims=True)

        @pl.when(i == 0)
        def _():
            db_ref[...] = part

        @pl.when(i > 0)
        def _():
            db_ref[...] += part

    rev = pl.BlockSpec((tb, L), lambda i: (nb - 1 - i, 0))
    vec = pl.BlockSpec((1, L), lambda i: (0, 0))
    return pl.pallas_call(
        body, grid=(nb,), in_specs=[pl.BlockSpec((tb, L), lambda i: (nb - 1 - i, col0)), vec, rev, rev],
        out_specs=[rev, vec],
        out_shape=[jax.ShapeDtypeStruct((S, L), F32), jax.ShapeDtypeStruct((1, L), F32)],
        scratch_shapes=[pltpu.VMEM((1, L), F32)],
        compiler_params=_params(("arbitrary",)), name=name)(z, bias, drs, dcs)


def _one_head(x, low, a):
    return jnp.where(low if a == 0 else jnp.logical_not(low), x, jnp.zeros_like(x))


class Rider(NamedTuple):
    inputs: tuple
    out_shapes: tuple
    aliases: dict
    sems: tuple
    plan: Callable


def _with_rider(rider, n_in, n_out, n_scratch):
    if rider is None:
        return [], [], [], [], {}, lambda refs: (refs[:n_in], refs[n_in:n_in + n_out], refs[n_in + n_out:], None)
    e_in, e_out = len(rider.inputs), len(rider.out_shapes)

    def split(refs):
        ins, r_in = refs[:n_in], refs[n_in:n_in + e_in]
        o0 = n_in + e_in
        outs, r_out = refs[o0:o0 + n_out], refs[o0 + n_out:o0 + n_out + e_out]
        s0 = o0 + n_out + e_out
        return ins, outs, refs[s0:s0 + n_scratch], rider.plan(r_in, r_out, refs[s0 + n_scratch:])

    aliases = {n_in + a: n_out + b for a, b in rider.aliases.items()}
    return list(rider.inputs), [_ANY] * e_in, list(rider.out_shapes), [_ANY] * e_out, aliases, split


def attn_fwd(q, q0, k, k0, v, v0, n_pairs, decay, *, causal, rider=None, name):
    Sq, Sk = q.shape[0], k.shape[0]
    tq = _tile(Sq, ATTN_Q_BLOCK, LANES)
    tk = _tile(Sk, ATTN_K_BLOCK, LANES)
    nq, nk = Sq // tq, Sk // tk
    bias = decay is not None
    r_args, r_in_specs, r_shapes, r_out_specs, aliases, split = _with_rider(rider, 5 if bias else 3, 2, 3)

    def body(*refs):
        ins, (o_ref, lse_ref), (m_sc, l_sc, acc_sc), ride = split(refs)
        q_ref, k_ref, v_ref = ins[:3]
        cq_ref, ck_ref = (ins[3], ins[4]) if bias else (None, None)
        pr, i, j = pl.program_id(0), pl.program_id(1), pl.program_id(2)
        if ride is not None:
            pl.when(jnp.logical_and(pr == 0, jnp.logical_and(i == 0, j == 0)))(ride[0])

        @pl.when(j == 0)
        def _():
            m_sc[...] = jnp.full_like(m_sc, NEG)
            l_sc[...] = jnp.zeros_like(l_sc)
            acc_sc[...] = jnp.zeros_like(acc_sc)

        def compute(masked):
            qv, kv, vv = q_ref[...], k_ref[...], v_ref[...].astype(BF16)
            low = _low_half(qv.shape)
            for a in range(2):
                s = _dot(_one_head(qv, low, a), kv, 1, 1)
                if bias:
                    s = s + (cq_ref[a] - ck_ref[a])
                if masked:
                    row = i * tq + lax.broadcasted_iota(jnp.int32, (tq, tk), 0)
                    col = j * tk + lax.broadcasted_iota(jnp.int32, (tq, tk), 1)
                    s = jnp.where(col <= row, s, NEG)
                m_prev = m_sc[a]
                m_new = jnp.maximum(m_prev, jnp.max(s, axis=1, keepdims=True))
                alpha = jnp.exp2(m_prev - m_new)
                p = jnp.exp2(s - m_new)
                l_sc[a] = alpha * l_sc[a] + jnp.sum(p, axis=1, keepdims=True)
                acc_sc[a] = alpha * acc_sc[a] + _dot(p.astype(BF16), vv, 1, 0)
                m_sc[a] = m_new

        if causal:
            live = j * tk <= i * tq + (tq - 1)
            crosses = j * tk + (tk - 1) > i * tq
            pl.when(jnp.logical_and(live, crosses))(functools.partial(compute, True))
            pl.when(jnp.logical_and(live, jnp.logical_not(crosses)))(functools.partial(compute, False))
        else:
            compute(False)

        @pl.when(j == nk - 1)
        def _():
            low = _low_half((tq, LANES))
            o_ref[...] = jnp.where(low, acc_sc[0] / l_sc[0], acc_sc[1] / l_sc[1])
            for a in range(2):
                lse_ref[a] = m_sc[a] + jnp.log2(l_sc[a])

        if ride is not None:
            pl.when(jnp.logical_and(pr == n_pairs - 1, jnp.logical_and(i == nq - 1, j == nk - 1)))(ride[1])

    def kv_blk(i, j):
        return jnp.minimum(j, (i * tq + tq - 1) // tk) if causal else j

    in_specs = [pl.BlockSpec((tq, LANES), lambda p, i, j: (i, q0 + p)),
                pl.BlockSpec((tk, LANES), lambda p, i, j: (kv_blk(i, j), k0 + p)),
                pl.BlockSpec((tk, LANES), lambda p, i, j: (kv_blk(i, j), v0 + p))]
    args = [q, k, v]
    if bias:
        in_specs += [pl.BlockSpec((None, 2, tq, 1), lambda p, i, j: (p, 0, i, 0)),
                     pl.BlockSpec((None, 2, 1, tk), lambda p, i, j: (p, 0, 0, kv_blk(i, j)))]
        args += list(decay)
    out = pl.pallas_call(
        body, grid=(n_pairs, nq, nk), in_specs=in_specs + r_in_specs,
        out_specs=[pl.BlockSpec((tq, LANES), lambda p, i, j: (i, p)),
                   pl.BlockSpec((None, 2, tq, 1), lambda p, i, j: (p, 0, i, 0))] + r_out_specs,
        out_shape=[jax.ShapeDtypeStruct((Sq, n_pairs * LANES), F32),
                   jax.ShapeDtypeStruct((n_pairs, 2, Sq, 1), F32)] + r_shapes,
        scratch_shapes=[pltpu.VMEM((2, tq, 1), F32), pltpu.VMEM((2, tq, 1), F32), pltpu.VMEM((2, tq, LANES), F32)]
        + (list(rider.sems) if rider else []),
        input_output_aliases=aliases,
        compiler_params=_params(("arbitrary",) * 3 if rider else ("parallel", "parallel", "arbitrary")),
        name=name)(*args, *r_args)
    return out[0], out[1], out[2:]


def attn_delta(o, do, do0, n_pairs, *, name):
    S = o.shape[0]
    ts = _tile(S, 1024, 16)

    def body(o_ref, do_ref, out_ref):
        prod = o_ref[...] * do_ref[...]
        low = _low_half(prod.shape)
        out_ref[0] = jnp.sum(jnp.where(low, prod, 0.0), axis=1, keepdims=True)
        out_ref[1] = jnp.sum(jnp.where(low, 0.0, prod), axis=1, keepdims=True)

    return pl.pallas_call(
        body, grid=(n_pairs, S // ts),
        in_specs=[pl.BlockSpec((ts, LANES), lambda p, i: (i, p)), pl.BlockSpec((ts, LANES), lambda p, i: (i, do0 + p))],
        out_specs=pl.BlockSpec((None, 2, ts, 1), lambda p, i: (p, 0, i, 0)),
        out_shape=jax.ShapeDtypeStruct((n_pairs, 2, S, 1), F32),
        compiler_params=_params(("parallel", "parallel")), name=name)(o, do)


def attn_bwd(q, q0, k, k0, v, v0, do, do0, n_pairs, lse, delta, decay, *, causal, rider=None, name):
    Sq, Sk = q.shape[0], k.shape[0]
    tq = _tile(Sq, ATTN_Q_BLOCK, LANES)
    tk = _tile(Sk, ATTN_K_BLOCK, LANES)
    nq, nk = Sq // tq, Sk // tk
    bias = decay is not None

    r_args, r_in_specs, r_shapes, r_out_specs, aliases, split = _with_rider(
        rider, 8 if bias else 6, 5 if bias else 3, 0)

    def body(*refs):
        ins, outs, _, ride = split(refs)
        q_ref, k_ref, v_ref, do_ref, lse_ref, dl_ref = ins[:6]
        cq_ref, ck_ref = (ins[6], ins[7]) if bias else (None, None)
        dq_ref, dk_ref, dv_ref = outs[:3]
        dcs_ref, drs_ref = (outs[3], outs[4]) if bias else (None, None)
        pr, j, i = pl.program_id(0), pl.program_id(1), pl.program_id(2)
        if ride is not None:
            pl.when(jnp.logical_and(pr == 0, jnp.logical_and(i == 0, j == 0)))(ride[0])

        @pl.when(i == 0)
        def _():
            dk_ref[...] = jnp.zeros_like(dk_ref)
            dv_ref[...] = jnp.zeros_like(dv_ref)
            if bias:
                dcs_ref[...] = jnp.zeros_like(dcs_ref)

        rows = pl.ds(pl.multiple_of(i * tq, tq), tq)

        def compute(masked):
            qv, kv, vv, dov = q_ref[...], k_ref[...], v_ref[...].astype(BF16), do_ref[...].astype(BF16)
            low, low_k = _low_half(qv.shape), _low_half(kv.shape)
            dq_part, row_parts = None, []
            for a in range(2):
                qa, ka, doa = _one_head(qv, low, a), _one_head(kv, low_k, a), _one_head(dov, low, a)
                s = _dot(qa, kv, 1, 1)
                if bias:
                    s = s + (cq_ref[a] - ck_ref[a])
                p = jnp.exp2(s - lse_ref[a])
                if masked:
                    row = i * tq + lax.broadcasted_iota(jnp.int32, (tq, tk), 0)
                    col = j * tk + lax.broadcasted_iota(jnp.int32, (tq, tk), 1)
                    p = jnp.where(col <= row, p, 0.0)
                dv_ref[...] += _dot(p.astype(BF16), doa, 0, 0)
                dp = _dot(doa, vv, 1, 1)
                ds = p * (dp - dl_ref[a])
                dsb = ds.astype(BF16)
                dk_part = _dot(dsb, qa, 0, 0) * LN2
                part = _dot(dsb, ka, 1, 0) * QK_SCALE
                if bias:
                    dcs_ref[a] += jnp.sum(ds, axis=0, keepdims=True)
                    row_parts.append(jnp.sum(ds, axis=1, keepdims=True))
                dk_ref[...] += dk_part
                dq_part = part if dq_part is None else dq_part + part

            @pl.when(j == 0)
            def _():
                dq_ref[rows, :] = dq_part
                for a, rp in enumerate(row_parts):
                    drs_ref[a, rows, :] = rp

            @pl.when(j > 0)
            def _():
                dq_ref[rows, :] += dq_part
                for a, rp in enumerate(row_parts):
                    drs_ref[a, rows, :] += rp

        if causal:
            live = j * tk <= i * tq + (tq - 1)
            crosses = j * tk + (tk - 1) > i * tq
            pl.when(jnp.logical_and(live, crosses))(functools.partial(compute, True))
            pl.when(jnp.logical_and(live, jnp.logical_not(crosses)))(functools.partial(compute, False))
        else:
            compute(False)

        if ride is not None:
            pl.when(jnp.logical_and(pr == n_pairs - 1, jnp.logical_and(i == nq - 1, j == nk - 1)))(ride[1])

    def q_blk(j, i):
        return jnp.maximum(i, (j * tk) // tq) if causal else i

    col1 = pl.BlockSpec((None, 2, tq, 1), lambda p, j, i: (p, 0, q_blk(j, i), 0))
    in_specs = [pl.BlockSpec((tq, LANES), lambda p, j, i: (q_blk(j, i), q0 + p)),
                pl.BlockSpec((tk, LANES), lambda p, j, i: (j, k0 + p)),
                pl.BlockSpec((tk, LANES), lambda p, j, i: (j, v0 + p)),
                pl.BlockSpec((tq, LANES), lambda p, j, i: (q_blk(j, i), do0 + p)), col1, col1]
    args = [q, k, v, do, lse, delta]
    kout = pl.BlockSpec((tk, LANES), lambda p, j, i: (j, p))
    out_specs = [pl.BlockSpec((Sq, LANES), lambda p, j, i: (0, p)), kout, kout]
    out_shape = [jax.ShapeDtypeStruct((Sq, n_pairs * LANES), F32), jax.ShapeDtypeStruct((Sk, n_pairs * LANES), F32),
                 jax.ShapeDtypeStruct((Sk, n_pairs * LANES), F32)]
    if bias:
        in_specs += [col1, pl.BlockSpec((None, 2, 1, tk), lambda p, j, i: (p, 0, 0, j))]
        args += list(decay)
        out_specs += [pl.BlockSpec((None, 2, 1, tk), lambda p, j, i: (p, 0, 0, j)),
                      pl.BlockSpec((None, 2, Sq, 1), lambda p, j, i: (p, 0, 0, 0))]
        out_shape += [jax.ShapeDtypeStruct((n_pairs, 2, 1, Sk), F32), jax.ShapeDtypeStruct((n_pairs, 2, Sq, 1), F32)]
    n_own = len(out_shape)
    out = pl.pallas_call(
        body, grid=(n_pairs, nk, nq), in_specs=in_specs + r_in_specs, out_specs=out_specs + r_out_specs,
        out_shape=out_shape + r_shapes, scratch_shapes=list(rider.sems) if rider else [],
        input_output_aliases=aliases,
        compiler_params=_params(("arbitrary",) * 3 if rider else ("parallel", "arbitrary", "arbitrary")),
        name=name)(*args, *r_args)
    return tuple(out[:n_own]), out[n_own:]


def _tril_mask(n):
    r = lax.broadcasted_iota(jnp.int32, (n, n), 0)
    c = lax.broadcasted_iota(jnp.int32, (n, n), 1)
    return c <= r


def _gmlp_operands(v_gain, w_s, b_s):
    G = w_s.shape[0]
    return (v_gain.reshape(G // 2, 1, LANES), w_s.reshape(G // 2, 2, CHUNK, CHUNK), b_s.reshape(G // 2, 2, CHUNK, 1))


def _gmlp_gate(wt, vh, b_ref, low):
    gate = _dot(wt[0], _one_head(vh, low, 0), 1, 0) + _dot(wt[1], _one_head(vh, low, 1), 1, 0)
    return gate + jnp.where(low, b_ref[0], b_ref[1])


def gmlp_fwd(proj, v0, n_pairs, vg, w, b, *, name):
    S = proj.shape[0]
    ts = _tile(S, 1024, CHUNK)

    def body(up_ref, vp_ref, vg_ref, w_ref, b_ref, o_ref):
        mask = _tril_mask(CHUNK)
        wt = [jnp.where(mask, w_ref[a], 0.0).astype(BF16) for a in range(2)]
        low = _low_half((CHUNK, LANES))
        for c in range(ts // CHUNK):
            sl = pl.ds(c * CHUNK, CHUNK)
            vz = _gelu(vp_ref[sl, :])
            r = lax.rsqrt(_half_sums(vz * vz, low) * (1.0 / HEAD_DIM) + EPS)
            vh = (vz * r * vg_ref[...]).astype(BF16)
            o_ref[sl, :] = _gelu(up_ref[sl, :]) * _gmlp_gate(wt, vh, b_ref, low)

    return pl.pallas_call(
        body, grid=(n_pairs, S // ts),
        in_specs=[pl.BlockSpec((ts, LANES), lambda p, i: (i, p)), pl.BlockSpec((ts, LANES), lambda p, i: (i, v0 + p)),
                  pl.BlockSpec((None, 1, LANES), lambda p, i: (p, 0, 0)),
                  pl.BlockSpec((None, 2, CHUNK, CHUNK), lambda p, i: (p, 0, 0, 0)),
                  pl.BlockSpec((None, 2, CHUNK, 1), lambda p, i: (p, 0, 0, 0))],
        out_specs=pl.BlockSpec((ts, LANES), lambda p, i: (i, p)),
        out_shape=jax.ShapeDtypeStruct((S, n_pairs * LANES), F32),
        compiler_params=_params(("parallel", "parallel")), name=name)(proj, proj, vg, w, b)


def gmlp_bwd(proj, v0, n_pairs, vg, w, wT, b, do, *, name):
    S = proj.shape[0]
    ts = _tile(S, 1024, CHUNK)

    def body(up_ref, vp_ref, vg_ref, w_ref, wT_ref, b_ref, do_ref, dup_ref, dvp_ref, dw_ref, db_ref, dvg_ref):
        i = pl.program_id(1)

        @pl.when(i == 0)
        def _():
            dw_ref[...] = jnp.zeros_like(dw_ref)
            db_ref[...] = jnp.zeros_like(db_ref)
            dvg_ref[...] = jnp.zeros_like(dvg_ref)

        mask = _tril_mask(CHUNK)
        wt = [jnp.where(mask, w_ref[a], 0.0).astype(BF16) for a in range(2)]
        wtT = [jnp.where(mask.T, wT_ref[a], 0.0).astype(BF16) for a in range(2)]
        low = _low_half((CHUNK, LANES))
        vgain = vg_ref[...]
        for c in range(ts // CHUNK):
            sl = pl.ds(c * CHUNK, CHUNK)
            u_pre, v_pre, dout = up_ref[sl, :], vp_ref[sl, :], do_ref[sl, :]
            vz = _gelu(v_pre)
            r = lax.rsqrt(_half_sums(vz * vz, low) * (1.0 / HEAD_DIM) + EPS)
            vh = (vz * r * vgain).astype(BF16)
            gate = _gmlp_gate(wt, vh, b_ref, low)
            dgate = dout * _gelu(u_pre)
            dup_ref[sl, :] = dout * gate * _gelu_grad(u_pre)
            dvh = None
            for a in range(2):
                dga = _one_head(dgate, low, a)
                dgb = dga.astype(BF16)
                dw_ref[a] += jnp.where(mask, _dot(dgb, vh, 1, 1), 0.0)
                db_ref[a] += jnp.sum(dga, axis=1, keepdims=True)
                part = _dot(wtT[a], dgb, 1, 0)
                dvh = part if dvh is None else dvh + part
            dvg_ref[...] += jnp.sum(dvh * vz * r, axis=0, keepdims=True)
            t = dvh * vgain
            dvz = r * t - vz * (r * r * r) * (_half_sums(vz * t, low) * (1.0 / HEAD_DIM))
            dvp_ref[sl, :] = dvz * _gelu_grad(v_pre)

    ublk = pl.BlockSpec((ts, LANES), lambda p, i: (i, p))
    wblk = pl.BlockSpec((None, 2, CHUNK, CHUNK), lambda p, i: (p, 0, 0, 0))
    bblk = pl.BlockSpec((None, 2, CHUNK, 1), lambda p, i: (p, 0, 0, 0))
    gblk = pl.BlockSpec((None, 1, LANES), lambda p, i: (p, 0, 0))
    return pl.pallas_call(
        body, grid=(n_pairs, S // ts),
        in_specs=[ublk, pl.BlockSpec((ts, LANES), lambda p, i: (i, v0 + p)), gblk, wblk, wblk, bblk, ublk],
        out_specs=[ublk, ublk, wblk, bblk, gblk],
        out_shape=[jax.ShapeDtypeStruct((S, n_pairs * LANES), F32), jax.ShapeDtypeStruct((S, n_pairs * LANES), F32),
                   jax.ShapeDtypeStruct((n_pairs, 2, CHUNK, CHUNK), F32), jax.ShapeDtypeStruct((n_pairs, 2, CHUNK, 1), F32),
                   jax.ShapeDtypeStruct((n_pairs, 1, LANES), F32)],
        compiler_params=_params(("parallel", "arbitrary")), name=name)(proj, proj, vg, w, wT, b, do)


def loss_head(y, target, *, name):
    S, D = y.shape
    ts = _tile(S, 512, 8)

    def body(y_ref, t_ref, dy_ref, loss_ref):
        i = pl.program_id(0)
        e = y_ref[...] - t_ref[...]
        dy_ref[...] = e * (1.0 / D)
        part = jnp.sum(jnp.sum(e * e, axis=1, keepdims=True), axis=0, keepdims=True) * (0.5 / D)

        @pl.when(i == 0)
        def _():
            loss_ref[...] = part

        @pl.when(i > 0)
        def _():
            loss_ref[...] += part

    row = pl.BlockSpec((ts, D), lambda i: (i, 0))
    return pl.pallas_call(
        body, grid=(S // ts,), in_specs=[row, row],
        out_specs=[row, pl.BlockSpec((1, 1), lambda i: (0, 0))],
        out_shape=[jax.ShapeDtypeStruct((S, D), F32), jax.ShapeDtypeStruct((1, 1), F32)],
        compiler_params=_params(("arbitrary",)), name=name)(y, target)


def adamw(w, g, m, v, *, name):
    shape = w.shape
    C = shape[-1]
    R = w.size // C
    tr = _tile(R, max(8, (256 * 1024) // C // 8 * 8), 8)

    def body(w_ref, g_ref, m_ref, v_ref, d_ref, nm_ref, nv_ref):
        gv = g_ref[...]
        nm = ADAM_B1 * m_ref[...] + (1.0 - ADAM_B1) * gv
        nv = ADAM_B2 * v_ref[...] + (1.0 - ADAM_B2) * (gv * gv)
        m_hat = nm / (1.0 - ADAM_B1 ** ADAM_STEP)
        v_hat = nv / (1.0 - ADAM_B2 ** ADAM_STEP)
        d_ref[...] = -ADAM_LR * (m_hat / (jnp.sqrt(v_hat) + ADAM_EPS) + ADAM_WD * w_ref[...])
        nm_ref[...] = nm
        nv_ref[...] = nv

    blk = pl.BlockSpec((tr, C), lambda i: (i, 0))
    out = pl.pallas_call(
        body, grid=(R // tr,), in_specs=[blk] * 4, out_specs=[blk] * 3,
        out_shape=[jax.ShapeDtypeStruct((R, C), F32)] * 3,
        compiler_params=_params(("parallel",)), name=name)(*(a.reshape(R, C) for a in (w, g, m, v)))
    return tuple(o.reshape(shape) for o in out)


def pair_sum(p, landed, half, *, name):
    n, R, C = landed.shape
    tr = _tile(R, 256, 16)
    nr = R // tr

    def body(half_ref, p_ref, l_ref, o_ref):
        o_ref[...] = (p_ref[...] + l_ref[...]).astype(BF16)

    return pl.pallas_call(
        body,
        grid_spec=pltpu.PrefetchScalarGridSpec(
            num_scalar_prefetch=1, grid=(n, nr),
            in_specs=[pl.BlockSpec((None, tr, C), lambda k, r, half_ref: (k, half_ref[0] * nr + r, 0)),
                      pl.BlockSpec((None, tr, C), lambda k, r, half_ref: (k, r, 0))],
            out_specs=pl.BlockSpec((None, tr, C), lambda k, r, half_ref: (k, r, 0))),
        out_shape=jax.ShapeDtypeStruct((n, R, C), BF16),
        compiler_params=_params(("parallel", "parallel")), name=name)(half, p, landed)


def chip_sum(own, landed, chip, *, name):
    n, R, C = own.shape
    tr = _tile(R, 256, 16)

    def body(chip_ref, own_ref, *rest):
        l_refs, o_ref = rest[:n], rest[n]
        me = chip_ref[0]
        acc = None
        for d in range(n):
            term = jnp.where(me == d, own_ref[...], l_refs[d][...]).astype(F32)
            acc = term if acc is None else acc + term
        o_ref[...] = acc

    def landed_spec(d):
        return pl.BlockSpec((None, tr, C), lambda r, chip_ref: (jnp.where(chip_ref[0] == d, (d + 1) % n, d), r, 0))

    return pl.pallas_call(
        body,
        grid_spec=pltpu.PrefetchScalarGridSpec(
            num_scalar_prefetch=1, grid=(R // tr,),
            in_specs=[pl.BlockSpec((None, tr, C), lambda r, chip_ref: (chip_ref[0], r, 0))]
            + [landed_spec(d) for d in range(n)],
            out_specs=pl.BlockSpec((tr, C), lambda r, chip_ref: (r, 0))),
        out_shape=jax.ShapeDtypeStruct((R, C), F32),
        compiler_params=_params(("parallel",)), name=name)(chip, own, *([landed] * n))


def ordered_sum(parts, *, name):
    n, R, C = parts.shape
    tr = _tile(R, 256, 16)

    def body(p_ref, o_ref):
        acc = p_ref[0].astype(F32)
        for d in range(1, n):
            acc = acc + p_ref[d].astype(F32)
        o_ref[...] = acc

    return pl.pallas_call(
        body, grid=(R // tr,), in_specs=[pl.BlockSpec((n, tr, C), lambda r: (0, r, 0))],
        out_specs=pl.BlockSpec((tr, C), lambda r: (r, 0)),
        out_shape=jax.ShapeDtypeStruct((R, C), F32),
        compiler_params=_params(("parallel",)), name=name)(parts)


_ANY = pl.BlockSpec(memory_space=pl.ANY)


def _position():
    return lax.axis_index("x"), lax.axis_index("y"), lax.axis_index("c")


def _remote(src, dst, send_sem, recv_sem, device):
    return pltpu.make_async_remote_copy(src_ref=src, dst_ref=dst, send_sem=send_sem, recv_sem=recv_sem,
                                        device_id=device, device_id_type=MESH_ID)


def _small_all_gather(s_ref, all_ref, send_sems, recv_sems, x, y, c):
    me = 4 * x + 2 * y + c
    copies = []
    for f in range(1, 8):
        peer = ((1 - x) if f & 4 else x, (1 - y) if f & 2 else y, (1 - c) if f & 1 else c)
        cp = _remote(s_ref, all_ref.at[me], send_sems.at[f - 1], recv_sems.at[f - 1], peer)
        cp.start()
        copies.append((cp, peer, f - 1))

    def finish():
        for cp, peer, s in copies:
            slot = all_ref.at[4 * peer[0] + 2 * peer[1] + peer[2]]
            _remote(slot, slot, send_sems.at[s], recv_sems.at[s], peer).wait_recv()
        for cp, _, _ in copies:
            cp.wait_send()

    return finish


def _core_rows(ref, core):
    h = ref.shape[1] // 2
    return pl.ds(core * h, h)


def _gather_plan(outs, send_sems, recv_sems):
    n = len(outs)
    x, y, c = _position()
    k = 2 * x + y
    sibling = (x, y, 1 - c)
    chips = [(1 - x, y), (x, 1 - y), (1 - x, 1 - y)]

    def first():
        return [_remote(outs[w].at[k, _core_rows(outs[w], c)], outs[w].at[k, _core_rows(outs[w], c)],
                        send_sems.at[w, j], recv_sems.at[w, j], (px, py, c))
                for j, (px, py) in enumerate(chips) for w in range(n)]

    def start():
        for cp in first():
            cp.start()

    def finish():
        passed = []
        for j, (px, py) in enumerate(chips):
            for w in range(n):
                slot = outs[w].at[2 * px + py, _core_rows(outs[w], c)]
                _remote(slot, slot, send_sems.at[w, j], recv_sems.at[w, j], (px, py, c)).wait_recv()
                cp = _remote(slot, slot, send_sems.at[w, 3 + j], recv_sems.at[w, 3 + j], sibling)
                cp.start()
                passed.append(cp)
        for j, (px, py) in enumerate(chips):
            for w in range(n):
                slot = outs[w].at[2 * px + py, _core_rows(outs[w], 1 - c)]
                _remote(slot, slot, send_sems.at[w, 3 + j], recv_sems.at[w, 3 + j], sibling).wait_recv()
        for cp in first() + passed:
            cp.wait_send()

    return start, finish


def _gather_sems(n):
    return (pltpu.SemaphoreType.DMA((n, 6)), pltpu.SemaphoreType.DMA((n, 6)))


def gather_rider(slabs):
    return Rider(tuple(slabs), tuple(jax.ShapeDtypeStruct(a.shape, a.dtype) for a in slabs),
                 {i: i for i in range(len(slabs))}, _gather_sems(len(slabs)),
                 lambda ins, outs, sems: _gather_plan(outs, sems[0], sems[1]))


def gather_weights(slabs, small_slab, *, name):
    n = len(slabs)

    def body(*refs):
        outs, all_ref = refs[n + 1:2 * n + 1], refs[2 * n + 1]
        send_sems, recv_sems, s_send, s_recv = refs[2 * n + 2:]
        x, y, c = _position()
        finish_small = _small_all_gather(all_ref.at[4 * x + 2 * y + c], all_ref, s_send, s_recv, x, y, c)
        start, finish = _gather_plan(outs, send_sems, recv_sems)
        start()
        finish()
        finish_small()

    args = list(slabs) + [small_slab]
    out = pl.pallas_call(
        body, in_specs=[_ANY] * (n + 1), out_specs=[_ANY] * (n + 1),
        out_shape=[jax.ShapeDtypeStruct(a.shape, a.dtype) for a in args],
        input_output_aliases={i: i for i in range(n + 1)},
        scratch_shapes=list(_gather_sems(n)) + [pltpu.SemaphoreType.DMA((7,)), pltpu.SemaphoreType.DMA((7,))],
        name=name)(*args)
    return out[:n], out[n]


def exchange_with_sibling(parts, small_slab, *, name):
    n = len(parts)
    has_small = small_slab is not None
    n_arg = n + (1 if has_small else 0)

    def body(*refs):
        p_refs = refs[:n]
        lands = refs[n_arg:n_arg + n]
        send_sems, recv_sems = refs[2 * n_arg], refs[2 * n_arg + 1]
        x, y, c = _position()
        sibling = (x, y, 1 - c)
        if has_small:
            all_ref = refs[n_arg + n]
            finish_small = _small_all_gather(all_ref.at[4 * x + 2 * y + c], all_ref, refs[2 * n_arg + 2],
                                             refs[2 * n_arg + 3], x, y, c)
        sends = []
        for w in range(n):
            for d in range(4):
                cp = _remote(p_refs[w].at[d, _core_rows(p_refs[w], 1 - c)], lands[w].at[d],
                             send_sems.at[w, d], recv_sems.at[w, d], sibling)
                cp.start()
                sends.append(cp)
        for cp in sends:
            cp.wait_recv()
        for cp in sends:
            cp.wait_send()
        if has_small:
            finish_small()

    small_args = [small_slab] if has_small else []
    out = pl.pallas_call(
        body, in_specs=[_ANY] * n_arg, out_specs=[_ANY] * n_arg,
        out_shape=[jax.ShapeDtypeStruct((4, p.shape[1] // 2, p.shape[2]), p.dtype) for p in parts]
        + [jax.ShapeDtypeStruct(s.shape, s.dtype) for s in small_args],
        input_output_aliases={n: n} if has_small else {},
        scratch_shapes=[pltpu.SemaphoreType.DMA((n, 4)), pltpu.SemaphoreType.DMA((n, 4))]
        + ([pltpu.SemaphoreType.DMA((7,)), pltpu.SemaphoreType.DMA((7,))] if has_small else []),
        name=name)(*parts, *small_args)
    return out[:n], (out[n] if has_small else None)


def _scatter_plan(q_refs, outs, send_sems, recv_sems):
    n = len(q_refs)
    x, y, c = _position()
    k = 2 * x + y
    chips = [(1 - x, y), (x, 1 - y), (1 - x, 1 - y)]

    def sends():
        return [_remote(q_refs[w].at[2 * px + py], outs[w].at[k], send_sems.at[w, j], recv_sems.at[w, j], (px, py, c))
                for j, (px, py) in enumerate(chips) for w in range(n)]

    def start():
        for cp in sends():
            cp.start()

    def finish():
        for j, (px, py) in enumerate(chips):
            for w in range(n):
                slot = outs[w].at[2 * px + py]
                _remote(slot, slot, send_sems.at[w, j], recv_sems.at[w, j], (px, py, c)).wait_recv()
        for cp in sends():
            cp.wait_send()

    return start, finish


def _scatter_sems(n):
    return (pltpu.SemaphoreType.DMA((n, 3)), pltpu.SemaphoreType.DMA((n, 3)))


def scatter_rider(parts):
    return Rider(tuple(parts), tuple(jax.ShapeDtypeStruct(q.shape, q.dtype) for q in parts), {},
                 _scatter_sems(len(parts)), lambda ins, outs, sems: _scatter_plan(ins, outs, sems[0], sems[1]))


def scatter_to_chips(parts, *, name):
    n = len(parts)

    def body(*refs):
        start, finish = _scatter_plan(refs[:n], refs[n:2 * n], refs[2 * n], refs[2 * n + 1])
        start()
        finish()

    return pl.pallas_call(
        body, in_specs=[_ANY] * n, out_specs=[_ANY] * n,
        out_shape=[jax.ShapeDtypeStruct(q.shape, q.dtype) for q in parts],
        scratch_shapes=list(_scatter_sems(n)), name=name)(*parts)


def share_with_sibling(parts, *, name):
    n = len(parts)

    def body(*refs):
        r_refs, outs = refs[:n], refs[n:2 * n]
        send_sems, recv_sems = refs[2 * n:]
        x, y, c = _position()
        sends = []
        for w in range(n):
            cp = _remote(r_refs[w], outs[w], send_sems.at[w], recv_sems.at[w], (x, y, 1 - c))
            cp.start()
            sends.append(cp)
        for cp in sends:
            cp.wait_recv()
        for cp in sends:
            cp.wait_send()

    return pl.pallas_call(
        body, in_specs=[_ANY] * n, out_specs=[_ANY] * n,
        out_shape=[jax.ShapeDtypeStruct(r.shape, r.dtype) for r in parts],
        scratch_shapes=[pltpu.SemaphoreType.DMA((n,)), pltpu.SemaphoreType.DMA((n,))],
        name=name)(*parts)


def _cols_to_chips(full):
    *lead, R, C4 = full.shape
    t = full.reshape(*lead, R, 4, C4 // 4)
    return jnp.moveaxis(t, -2, 0)


def _chips_to_cols(sh):
    t = jnp.moveaxis(sh, 0, -2)
    return t.reshape(*t.shape[:-2], t.shape[-2] * t.shape[-1])


def _slot_in_empty(own, index, n):
    return lax.dynamic_update_slice(lax.empty((n,) + own.shape, own.dtype), own[None], (index,) + (0,) * own.ndim)


def _fold_pair(dg):
    return dg[0, :HEAD_DIM] + dg[0, HEAD_DIM:]


def _ffn_fwd(x, g, w_in_slab, w_out, tag):
    h = rms_fwd(x, g, name=f"{tag}_rms")
    a, b, act = swiglu_fwd(h, w_in_slab, name=f"{tag}_in")
    y = matmul(act, w_out, res=x, scale=0.5, tm=1024, tn=512, tk=w_out.shape[0], name=f"{tag}_out")
    return y, (x, h, a, b, act)


def _ffn_bwd(dy, saved, g, w_in_slab, w_out, tag):
    x, h, a, b, act = saved
    da, db = swiglu_bwd(dy, w_out, a, b, name=f"{tag}_dact")
    dw_out = grad_rows(act, dy, scale=0.5, name=f"{tag}_dwout")
    dw_in = grad_cols(h, da, db, name=f"{tag}_dwin")
    dx, dg = ffn_dh(da, db, w_in_slab, x, g, dy, name=f"{tag}_dh")
    return dx, dg[0], dw_in, dw_out


MEM_PAIRS = MEM_WIDTH // LANES


def _mem_attn_fwd(proj, mq0, mem_n, w_kv, g_q, g_k, tag):
    qh = pairnorm_fwd(proj, mq0, MEM_PAIRS, g_q, scale=Q_SCALE, name=f"{tag}_qnorm")
    kv = matmul(mem_n, w_kv, tm=256, tn=512, tk=1024, name=f"{tag}_kv")
    kh = pairnorm_fwd(kv, 0, MEM_PAIRS, g_k, name=f"{tag}_knorm")
    o, lse, _ = attn_fwd(qh, 0, kh, 0, kv, MEM_PAIRS, MEM_PAIRS, None, causal=False, name=f"{tag}_attn")
    return o, (qh, kv, kh, o, lse)


def _mem_attn_bwd(dmix, do0, proj, mq0, saved, mem_n, g_q, g_k, tag):
    qh, kv, kh, o, lse = saved
    delta = attn_delta(o, dmix, do0, MEM_PAIRS, name=f"{tag}_delta")
    (dqh, dkh, dv), _ = attn_bwd(qh, 0, kh, 0, kv, MEM_PAIRS, dmix, do0, MEM_PAIRS, lse, delta, None,
                                 causal=False, name=f"{tag}_dattn")
    dq_pre, dgq = pairnorm_bwd(proj, mq0, MEM_PAIRS, dqh, g_q, name=f"{tag}_dqnorm")
    dk_pre, dgk = pairnorm_bwd(kv, 0, MEM_PAIRS, dkh, g_k, name=f"{tag}_dknorm")
    dkv = jnp.concatenate([dk_pre, dv], axis=1)
    dw_kv = grad_rows(mem_n, dkv, name=f"{tag}_dwkv")
    return dq_pre, _fold_pair(dgq), _fold_pair(dgk), dw_kv, dkv


def _per_head_lanes(x, H):
    return jnp.pad(x.reshape(H, -1).T, ((0, 0), (0, LANES - H)))


def _fox_fwd(proj, b_f, g_q, g_k, tok, rider, tag):
    H, P = tok // HEAD_DIM, tok // LANES
    bias = jnp.pad(b_f.reshape(1, H), ((0, 0), (0, LANES - H)))
    qh = pairnorm_fwd(proj, 0, P, g_q, scale=Q_SCALE, name=f"{tag}_qnorm")
    kh = pairnorm_fwd(proj, P, P, g_k, name=f"{tag}_knorm")
    c, c_cols = fgate_fwd(proj, 3 * P + MEM_PAIRS, bias, H, out_scale=LOG2E, name=f"{tag}_fgate")
    decay = (c_cols, c[:, :H].T.reshape(P, 2, 1, c.shape[0]))
    o, lse, rode = attn_fwd(qh, 0, kh, 0, proj, 2 * P, P, decay, causal=True, rider=rider, name=f"{tag}_attn")
    return o, (qh, kh, bias, decay, o, lse), rode


def _fox_bwd(dmix, proj, saved, g_q, g_k, tok, rider, tag):
    qh, kh, bias, decay, o, lse = saved
    H, P = tok // HEAD_DIM, tok // LANES
    delta = attn_delta(o, dmix, 0, P, name=f"{tag}_delta")
    (dqh, dkh, dv, dcs, drs), rode = attn_bwd(qh, 0, kh, 0, proj, 2 * P, dmix, 0, P, lse, delta, decay, causal=True,
                                              rider=rider, name=f"{tag}_dattn")
    dq_pre, dgq = pairnorm_bwd(proj, 0, P, dqh, g_q, name=f"{tag}_dqnorm")
    dk_pre, dgk = pairnorm_bwd(proj, P, P, dkh, g_k, name=f"{tag}_dknorm")
    dz, dbias = fgate_bwd(proj, 3 * P + MEM_PAIRS, bias, _per_head_lanes(drs, H), _per_head_lanes(dcs, H),
                          name=f"{tag}_dfgate")
    dqkv = jnp.concatenate([dq_pre, dk_pre, dv], axis=1)
    return dqkv, dz, dbias[0, :H], _fold_pair(dgq), _fold_pair(dgk), rode


def local_step(x, mem, target, W, comm=None):
    S, D = x.shape
    tok = D - MEM_WIDTH
    P = tok // LANES
    depth = W["norm_ffn1"].shape[0]
    mem_n = rms_fwd(mem, W["mem_norm"], name="mem_rms")
    saved = []
    for i in range(depth):
        kind, j = i % 2, i // 2
        t = f"l{i}"
        x1, s1 = _ffn_fwd(x, W["norm_ffn1"][i], W["ffn1_w_in"][i], W["ffn1_w_out"][i], f"{t}_ffn1")
        h = rms_fwd(x1, W["norm_mix"][i], name=f"{t}_mix_rms")
        w_mix = W["fox_w_in"][j] if kind == 0 else W["gmlp_w_in"][j]
        proj = matmul(h, w_mix, tm=1024, tn=896, tk=D, name=f"{t}_mix_in")
        if kind == 0:
            rider = comm.late_weights_rider() if (comm is not None and i == 0) else None
            o_tok, s_tok, rode = _fox_fwd(proj, W["fox_b_f"][j], W["fox_q_norm"][j], W["fox_k_norm"][j], tok, rider,
                                          f"{t}_fox")
            if rider is not None:
                comm.accept_late_weights(W, rode)
            mq0 = 3 * P
        else:
            vg, ws, bs = _gmlp_operands(W["gmlp_v_norm"][j], W["gmlp_w_s"][j], W["gmlp_b_s"][j])
            o_tok = gmlp_fwd(proj, P, P, vg, ws, bs, name=f"{t}_gmlp")
            s_tok = None
            mq0 = 2 * P
        o_mem, s_mem = _mem_attn_fwd(proj, mq0, mem_n, W["mem_w_kv"][i], W["mem_q_norm"][i], W["mem_k_norm"][i],
                                     f"{t}_mem")
        mix = jnp.concatenate([o_tok, o_mem], axis=1).astype(BF16)
        x2 = matmul(mix, W["w_out"][i], res=x1, tm=1024, tn=512, tk=D, name=f"{t}_mix_out")
        x3, s3 = _ffn_fwd(x2, W["norm_ffn2"][i], W["ffn2_w_in"][i], W["ffn2_w_out"][i], f"{t}_ffn2")
        saved.append((s1, x1, h, proj, mq0, s_tok, s_mem, mix, s3))
        x = x3

    dx, loss = loss_head(x, target, name="loss_head")

    G = {k: [None] * depth for k in ("norm_ffn1", "norm_mix", "norm_ffn2", "mem_q_norm", "mem_k_norm", "ffn1_w_in",
                                     "ffn1_w_out", "ffn2_w_in", "ffn2_w_out", "w_out", "mem_w_kv")}
    n_fox, n_gmlp = (depth + 1) // 2, depth // 2
    for k in ("fox_w_in", "fox_b_f", "fox_q_norm", "fox_k_norm"):
        G[k] = [None] * n_fox
    for k in ("gmlp_w_in", "gmlp_v_norm", "gmlp_w_s", "gmlp_b_s"):
        G[k] = [None] * n_gmlp
    dkv_all = [None] * depth
    for i in reversed(range(depth)):
        kind, j = i % 2, i // 2
        t = f"l{i}"
        s1, x1, h, proj, mq0, s_tok, s_mem, mix, s3 = saved[i]
        dx, G["norm_ffn2"][i], G["ffn2_w_in"][i], G["ffn2_w_out"][i] = _ffn_bwd(
            dx, s3, W["norm_ffn2"][i], W["ffn2_w_in"][i], W["ffn2_w_out"][i], f"{t}_ffn2")
        dmix = matmul(dx, W["w_out"][i], tb=True, tm=1024, tn=1024, tk=D, name=f"{t}_dmix")
        G["w_out"][i] = grad_rows(mix, dx, name=f"{t}_dwmixout")
        dmq, G["mem_q_norm"][i], G["mem_k_norm"][i], G["mem_w_kv"][i], dkv_all[i] = _mem_attn_bwd(
            dmix, P, proj, mq0, s_mem, mem_n, W["mem_q_norm"][i], W["mem_k_norm"][i], f"{t}_mem")
        if kind == 0:
            rider = comm.early_grads_rider(G) if (comm is not None and i == 0) else None
            dqkv, dz, G["fox_b_f"][j], G["fox_q_norm"][j], G["fox_k_norm"][j], rode = _fox_bwd(
                dmix, proj, s_tok, W["fox_q_norm"][j], W["fox_k_norm"][j], tok, rider, f"{t}_fox")
            if rider is not None:
                comm.accept_early_grads(rode)
            dproj = jnp.concatenate([dqkv, dmq, dz], axis=1).astype(BF16)
            w_mix, wkey = W["fox_w_in"][j], "fox_w_in"
        else:
            vg, ws, bs = _gmlp_operands(W["gmlp_v_norm"][j], W["gmlp_w_s"][j], W["gmlp_b_s"][j])
            dup, dvp, dws, dbs, dvg = gmlp_bwd(proj, P, P, vg, ws, jnp.swapaxes(ws, 2, 3), bs, dmix,
                                               name=f"{t}_dgmlp")
            G["gmlp_w_s"][j] = dws.reshape(W["gmlp_w_s"][j].shape)
            G["gmlp_b_s"][j] = dbs.reshape(W["gmlp_b_s"][j].shape)
            G["gmlp_v_norm"][j] = dvg.reshape(-1)
            dproj = jnp.concatenate([dup, dvp, dmq], axis=1).astype(BF16)
            w_mix, wkey = W["gmlp_w_in"][j], "gmlp_w_in"
        G[wkey][j] = matmul(h, dproj, ta=True, tm=1024, tn=896, tk=1024, name=f"{t}_dwmixin")
        dh = matmul(dproj, w_mix, tb=True, tm=1024, tn=1024, tk=896, name=f"{t}_dhmix")
        dx, dgm = rms_bwd(x1, dh, W["norm_mix"][i], dx, name=f"{t}_dmixrms")
        G["norm_mix"][i] = dgm[0]
        dx, G["norm_ffn1"][i], G["ffn1_w_in"][i], G["ffn1_w_out"][i] = _ffn_bwd(
            dx, s1, W["norm_ffn1"][i], W["ffn1_w_in"][i], W["ffn1_w_out"][i], f"{t}_ffn1")
    w_kv_all = jnp.concatenate([W["mem_w_kv"][i] for i in range(depth)], axis=1)
    dmem_n = matmul(jnp.concatenate(dkv_all, axis=1), w_kv_all, tb=True, tm=256, tn=512, tk=1024, name="dmem_n")
    _, dmemg = rms_bwd(mem, dmem_n, W["mem_norm"], None, name="dmem_rms")
    G["mem_norm"] = [dmemg[0]]
    return loss, dx, G


def _fox_cols_to_compute(w, tok):
    H = tok // HEAD_DIM
    qkv, f, mq = w[..., :3 * tok], w[..., 3 * tok:3 * tok + H], w[..., 3 * tok + H:]
    f = jnp.pad(f, [(0, 0)] * (w.ndim - 1) + [(0, LANES - H)])
    return jnp.concatenate([qkv, mq, f], axis=-1)


def _fox_cols_from_compute(w, tok):
    H = tok // HEAD_DIM
    qkv, mq, f = w[..., :3 * tok], w[..., 3 * tok:3 * tok + MEM_WIDTH], w[..., 3 * tok + MEM_WIDTH:3 * tok + MEM_WIDTH + H]
    return jnp.concatenate([qkv, f, mq], axis=-1)


_BIG = ("ffn1_w_in", "ffn1_w_out", "ffn2_w_in", "ffn2_w_out", "w_out", "mem_w_kv", "fox_w_in", "gmlp_w_in")
_SMALL = ("norm_ffn1", "norm_mix", "norm_ffn2", "mem_norm", "mem_q_norm", "mem_k_norm", "fox_b_f", "fox_q_norm",
          "fox_k_norm", "gmlp_v_norm", "gmlp_w_s", "gmlp_b_s")
WEIGHT_ORDER = ("norm_ffn1", "ffn1_w_in", "ffn1_w_out", "norm_mix", "norm_ffn2", "ffn2_w_in", "ffn2_w_out", "w_out",
                "mem_norm", "mem_w_kv", "mem_q_norm", "mem_k_norm", "fox_w_in", "fox_b_f", "fox_q_norm", "fox_k_norm",
                "gmlp_w_in", "gmlp_v_norm", "gmlp_w_s", "gmlp_b_s")


def _small_slab(rows_list, index):
    sizes = [s.shape[0] for s in rows_list]
    n_rows = [-(-n // LANES) for n in sizes]
    small = jnp.concatenate([jnp.pad(s, (0, r * LANES - n)).reshape(r, LANES)
                             for s, n, r in zip(rows_list, sizes, n_rows)], axis=0)
    small = jnp.pad(small, ((0, -small.shape[0] % 64), (0, 0)))
    return _slot_in_empty(small, index, 8), sizes, n_rows


_FIRST_WEIGHTS = (("ffn1_w_in", 0), ("ffn1_w_out", 0), ("fox_w_in", 0))


def _weight_from_slab(name, slab, tok):
    if name in ("ffn1_w_in", "ffn2_w_in"):
        return slab
    if name == "fox_w_in":
        return _fox_cols_to_compute(_chips_to_cols(slab), tok)
    if name == "gmlp_w_in":
        return _chips_to_cols(slab)
    return slab.reshape(4 * slab.shape[1], slab.shape[2])


def _grad_to_slab(name, g, tok):
    if name == "fox_w_in":
        return _cols_to_chips(_fox_cols_from_compute(g, tok))
    if name == "gmlp_w_in":
        return _cols_to_chips(g)
    return g


class _Exchange:
    def __init__(self, shards, tok, chip, core):
        self.tok, self.core = tok, core
        self.half = core.reshape(1).astype(jnp.int32)
        self.chip_id = chip.reshape(1).astype(jnp.int32)
        items = [(k, i) for k in _BIG for i in range(shards[k].shape[0])]
        self.slabs = {it: _slot_in_empty(shards[it[0]][it[1]].astype(BF16), chip, 4) for it in items}
        self.late = [it for it in items if it not in _FIRST_WEIGHTS]
        self.reduced = {}
        self.early = None

    def first_weights(self, small_slab):
        got, small_all = gather_weights([self.slabs[it] for it in _FIRST_WEIGHTS], small_slab, name="gather_first")
        return {it: _weight_from_slab(it[0], s, self.tok) for it, s in zip(_FIRST_WEIGHTS, got)}, small_all

    def late_weights_rider(self):
        return gather_rider([self.slabs[it] for it in self.late])

    def accept_late_weights(self, W, got):
        for (k, i), s in zip(self.late, got):
            W[k][i] = _weight_from_slab(k, s, self.tok)

    def _pair_sums(self, G, items, small_slab, tag):
        parts = [_grad_to_slab(k, G[k][i], self.tok) for k, i in items]
        landed, small_all = exchange_with_sibling(parts, small_slab, name=f"grad_exchange_{tag}")
        pair = [pair_sum(p, l, self.half, name=f"grad_pair_sum_{k}{i}") for (k, i), p, l in zip(items, parts, landed)]
        return pair, small_all

    def early_grads_rider(self, G):
        items = [(k, i) for k in _BIG for i in range(len(G[k])) if G[k][i] is not None]
        pair, _ = self._pair_sums(G, items, None, "early")
        self.early = (items, pair)
        return scatter_rider(pair)

    def accept_early_grads(self, landed):
        items, pair = self.early
        self._chip_sums(items, pair, landed)

    def _chip_sums(self, items, pair, landed):
        for (k, i), q, l in zip(items, pair, landed):
            self.reduced[(k, i)] = chip_sum(q, l, self.chip_id, name=f"grad_chip_sum_{k}{i}")

    def finish_grads(self, G, small_slab):
        items = [(k, i) for k in _BIG for i in range(len(G[k])) if (k, i) not in self.reduced]
        pair, small_all = self._pair_sums(G, items, small_slab, "late")
        self._chip_sums(items, pair, scatter_to_chips(pair, name="grad_scatter_late"))
        order = sorted(self.reduced)
        other = share_with_sibling([self.reduced[it] for it in order], name="grad_share")
        full = {}
        for it, a, b in zip(order, [self.reduced[it] for it in order], other):
            full[it] = jnp.where(self.core == 0, jnp.concatenate([a, b]), jnp.concatenate([b, a]))
        names = sorted({k for k, _ in order})
        return {k: jnp.stack([full[(k, i)] for i in range(len(G[k]))]) for k in names}, small_all


def kernel(x, mem, norm_ffn1, ffn1_w_in, ffn1_w_out, norm_mix, norm_ffn2, ffn2_w_in, ffn2_w_out, w_out, mem_norm, mem_w_kv, mem_q_norm, mem_k_norm, fox_w_in, fox_b_f, fox_q_norm, fox_k_norm, gmlp_w_in, gmlp_v_norm, gmlp_w_s, gmlp_b_s, loss_target, m_norm_ffn1, m_ffn1_w_in, m_ffn1_w_out, m_norm_mix, m_norm_ffn2, m_ffn2_w_in, m_ffn2_w_out, m_w_out, m_mem_norm, m_mem_w_kv, m_mem_q_norm, m_mem_k_norm, m_fox_w_in, m_fox_b_f, m_fox_q_norm, m_fox_k_norm, m_gmlp_w_in, m_gmlp_v_norm, m_gmlp_w_s, m_gmlp_b_s, v_norm_ffn1, v_ffn1_w_in, v_ffn1_w_out, v_norm_mix, v_norm_ffn2, v_ffn2_w_in, v_ffn2_w_out, v_w_out, v_mem_norm, v_mem_w_kv, v_mem_q_norm, v_mem_k_norm, v_fox_w_in, v_fox_b_f, v_fox_q_norm, v_fox_k_norm, v_gmlp_w_in, v_gmlp_v_norm, v_gmlp_w_s, v_gmlp_b_s):
    w = dict(norm_ffn1=norm_ffn1, ffn1_w_in=ffn1_w_in, ffn1_w_out=ffn1_w_out, norm_mix=norm_mix, norm_ffn2=norm_ffn2,
             ffn2_w_in=ffn2_w_in, ffn2_w_out=ffn2_w_out, w_out=w_out, mem_norm=mem_norm, mem_w_kv=mem_w_kv,
             mem_q_norm=mem_q_norm, mem_k_norm=mem_k_norm, fox_w_in=fox_w_in, fox_b_f=fox_b_f, fox_q_norm=fox_q_norm,
             fox_k_norm=fox_k_norm, gmlp_w_in=gmlp_w_in, gmlp_v_norm=gmlp_v_norm, gmlp_w_s=gmlp_w_s, gmlp_b_s=gmlp_b_s)
    m = dict(norm_ffn1=m_norm_ffn1, ffn1_w_in=m_ffn1_w_in, ffn1_w_out=m_ffn1_w_out, norm_mix=m_norm_mix,
             norm_ffn2=m_norm_ffn2, ffn2_w_in=m_ffn2_w_in, ffn2_w_out=m_ffn2_w_out, w_out=m_w_out, mem_norm=m_mem_norm,
             mem_w_kv=m_mem_w_kv, mem_q_norm=m_mem_q_norm, mem_k_norm=m_mem_k_norm, fox_w_in=m_fox_w_in,
             fox_b_f=m_fox_b_f, fox_q_norm=m_fox_q_norm, fox_k_norm=m_fox_k_norm, gmlp_w_in=m_gmlp_w_in,
             gmlp_v_norm=m_gmlp_v_norm, gmlp_w_s=m_gmlp_w_s, gmlp_b_s=m_gmlp_b_s)
    v = dict(norm_ffn1=v_norm_ffn1, ffn1_w_in=v_ffn1_w_in, ffn1_w_out=v_ffn1_w_out, norm_mix=v_norm_mix,
             norm_ffn2=v_norm_ffn2, ffn2_w_in=v_ffn2_w_in, ffn2_w_out=v_ffn2_w_out, w_out=v_w_out, mem_norm=v_mem_norm,
             mem_w_kv=v_mem_w_kv, mem_q_norm=v_mem_q_norm, mem_k_norm=v_mem_k_norm, fox_w_in=v_fox_w_in,
             fox_b_f=v_fox_b_f, fox_q_norm=v_fox_q_norm, fox_k_norm=v_fox_k_norm, gmlp_w_in=v_gmlp_w_in,
             gmlp_v_norm=v_gmlp_v_norm, gmlp_w_s=v_gmlp_w_s, gmlp_b_s=v_gmlp_b_s)
    D = x.shape[-1]
    tok = D - MEM_WIDTH
    xi, yi, ci = _position()
    chip = 2 * xi + yi

    device = 4 * xi + 2 * yi + ci

    comm = _Exchange(w, tok, chip, ci)
    vn = w["gmlp_v_norm"]
    vn_slab, _, _ = _small_slab([vn.reshape(-1)], device)
    first, vn_all = comm.first_weights(vn_slab)
    W = {k: w[k] for k in _SMALL}
    W["gmlp_v_norm"] = _chips_to_cols(vn_all[0::2].reshape(4, -1)[:, :vn.size].reshape((4,) + vn.shape))
    for k in _BIG:
        W[k] = [first.get((k, i)) for i in range(w[k].shape[0])]

    loss, grad_x, g = local_step(x[0], mem[0], loss_target[0], W, comm)

    small_list = [jnp.stack(g[k]).reshape(-1) for k in _SMALL] + [loss.reshape(-1)]
    small, small_sizes, small_rows = _small_slab(small_list, device)
    red, small_all = comm.finish_grads(g, small)
    small_sum = ordered_sum(small_all, name="small_sum")
    off = 0
    for k, n, r in zip(_SMALL, small_sizes, small_rows):
        red[k] = small_sum[off:off + r].reshape(-1)[:n].reshape((-1,) + w[k].shape[1:] if k != "gmlp_v_norm"
                                                                else (w[k].shape[0], -1))
        off += r
    loss_total = small_sum[off, 0]
    vn_cols = w["gmlp_v_norm"].shape[-1]
    red["gmlp_v_norm"] = lax.dynamic_slice_in_dim(red["gmlp_v_norm"], chip * vn_cols, vn_cols, axis=-1)

    deltas, new_m, new_v = {}, {}, {}
    for k in WEIGHT_ORDER:
        wk = w[k] if w[k].ndim > 1 else w[k].reshape(1, -1)
        upd = adamw(wk, red[k].reshape(wk.shape), m[k].reshape(wk.shape), v[k].reshape(wk.shape), name=f"adamw_{k}")
        deltas[k], new_m[k], new_v[k] = (u.reshape(w[k].shape) for u in upd)
    return (loss_total, grad_x[None], *[red[k].reshape(w[k].shape) for k in WEIGHT_ORDER],
            *[deltas[k] for k in WEIGHT_ORDER], *[new_m[k] for k in WEIGHT_ORDER], *[new_v[k] for k in WEIGHT_ORDER])
```

```python
import functools
import math
from typing import Callable, NamedTuple

import jax
import jax.numpy as jnp
from jax import lax
from jax.experimental import pallas as pl
from jax.experimental.pallas import tpu as pltpu

F32 = jnp.float32
BF16 = jnp.bfloat16
EPS = 1e-6
HEAD_DIM = 64
MEM_WIDTH = 256
CHUNK = 128
LANES = 128
NEG = -1e30
VMEM_LIMIT_BYTES = 56 * 1024 * 1024
ATTN_Q_BLOCK = 1024
ATTN_K_BLOCK = 1024
QK_SCALE = 0.125
MESH_ID = pl.DeviceIdType.MESH

ADAM_LR = 0.001
ADAM_B1 = 0.9
ADAM_B2 = 0.999
ADAM_EPS = 1e-08
ADAM_WD = 0.01
ADAM_STEP = 10


def _tile(n, pref, align):
    t = (min(pref, n) // align) * align
    while t >= align:
        if n % t == 0:
            return t
        t -= align
    return n


def _params(sem):
    return pltpu.CompilerParams(dimension_semantics=sem, vmem_limit_bytes=VMEM_LIMIT_BYTES)


def _dot(a, b, ca, cb):
    return lax.dot_general(a, b, (((ca,), (cb,)), ((), ())), preferred_element_type=F32)


def _sigmoid(x):
    return 1.0 / (1.0 + jnp.exp(-x))


_GELU_C = math.sqrt(2.0 / math.pi)


def _gelu(x):
    return 0.5 * x * (1.0 + jnp.tanh(_GELU_C * (x + 0.044715 * (x * x * x))))


def _gelu_grad(x):
    t = jnp.tanh(_GELU_C * (x + 0.044715 * (x * x * x)))
    return 0.5 * (1.0 + t) + 0.5 * x * (1.0 - t * t) * (_GELU_C * (1.0 + 3.0 * 0.044715 * (x * x)))


def matmul(a, b, *, ta=False, tb=False, out_dtype=F32, scale=None, res=None,
           tm=1024, tn=512, tk=1024, name):
    if ta:
        K, M = a.shape
    else:
        M, K = a.shape
    N = b.shape[0] if tb else b.shape[1]
    tm = _tile(M, tm, LANES if ta else 16)
    tn = _tile(N, tn, LANES)
    tk = _tile(K, tk, LANES)
    nk = K // tk
    a_spec = pl.BlockSpec((tk, tm), lambda i, j, k: (k, i)) if ta else pl.BlockSpec((tm, tk), lambda i, j, k: (i, k))
    b_spec = pl.BlockSpec((tn, tk), lambda i, j, k: (j, k)) if tb else pl.BlockSpec((tk, tn), lambda i, j, k: (k, j))
    o_spec = pl.BlockSpec((tm, tn), lambda i, j, k: (i, j))
    ca, cb = (0 if ta else 1), (1 if tb else 0)
    has_res = res is not None

    def body(*refs):
        a_ref, b_ref = refs[0], refs[1]
        res_ref = refs[2] if has_res else None
        o_ref = refs[3] if has_res else refs[2]
        acc_ref = refs[-1]
        k = pl.program_id(2)
        prod = _dot(a_ref[...].astype(BF16), b_ref[...].astype(BF16), ca, cb)

        def finish(acc):
            if scale is not None:
                acc = acc * scale
            if has_res:
                acc = res_ref[...] + acc
            o_ref[...] = acc.astype(out_dtype)

        if nk == 1:
            finish(prod)
        else:
            @pl.when(k == 0)
            def _():
                acc_ref[...] = prod

            @pl.when(k > 0)
            def _():
                acc_ref[...] += prod

            @pl.when(k == nk - 1)
            def _():
                finish(acc_ref[...])

    in_specs = [a_spec, b_spec] + ([o_spec] if has_res else [])
    args = (a, b) + ((res,) if has_res else ())
    return pl.pallas_call(
        body, grid=(M // tm, N // tn, nk), in_specs=in_specs, out_specs=o_spec,
        out_shape=jax.ShapeDtypeStruct((M, N), out_dtype),
        scratch_shapes=[pltpu.VMEM((tm, tn) if nk > 1 else (8, LANES), F32)],
        compiler_params=_params(("parallel", "parallel", "arbitrary")), name=name)(*args)


def swiglu_fwd(h, w_slab, *, name):
    S, D = h.shape
    Fc = w_slab.shape[-1]
    tm = _tile(S, 512, 16)

    def body(h_ref, wa_ref, wb_ref, a_ref, b_ref, act_ref):
        hv = h_ref[...]
        a = _dot(hv, wa_ref[...], 1, 0)
        b = _dot(hv, wb_ref[...], 1, 0)
        a_ref[...] = a.astype(BF16)
        b_ref[...] = b.astype(BF16)
        act_ref[...] = (a * _sigmoid(a) * b).astype(BF16)

    out = pl.BlockSpec((tm, Fc), lambda j, i: (i, j))
    return pl.pallas_call(
        body, grid=(2, S // tm),
        in_specs=[pl.BlockSpec((tm, D), lambda j, i: (i, 0)),
                  pl.BlockSpec((None, D, Fc), lambda j, i: (j, 0, 0)),
                  pl.BlockSpec((None, D, Fc), lambda j, i: (j + 2, 0, 0))],
        out_specs=[out, out, out],
        out_shape=[jax.ShapeDtypeStruct((S, 2 * Fc), BF16)] * 3,
        compiler_params=_params(("parallel", "parallel")), name=name)(h, w_slab, w_slab)


def swiglu_bwd(dy, w_out, a, b, *, name):
    S, D = dy.shape
    F = w_out.shape[0]
    fc = F // 2
    tm = _tile(S, 512, 16)

    def body(dy_ref, w_ref, a_ref, b_ref, da_ref, db_ref):
        dact = 0.5 * _dot(dy_ref[...].astype(BF16), w_ref[...], 1, 1)
        av = a_ref[...].astype(F32)
        sg = _sigmoid(av)
        da_ref[...] = (dact * b_ref[...].astype(F32) * (sg * (1.0 + av * (1.0 - sg)))).astype(BF16)
        db_ref[...] = (dact * (av * sg)).astype(BF16)

    blk = pl.BlockSpec((tm, fc), lambda j, i: (i, j))
    return pl.pallas_call(
        body, grid=(2, S // tm),
        in_specs=[pl.BlockSpec((tm, D), lambda j, i: (i, 0)), pl.BlockSpec((fc, D), lambda j, i: (j, 0)), blk, blk],
        out_specs=[blk, blk],
        out_shape=[jax.ShapeDtypeStruct((S, F), BF16), jax.ShapeDtypeStruct((S, F), BF16)],
        compiler_params=_params(("parallel", "parallel")), name=name)(dy, w_out, a, b)


def ffn_dh(da, db, w_slab, x, g, dy, *, name):
    S, F = da.shape
    D, Fc = w_slab.shape[-2:]
    tm = _tile(S, 1024, 16)
    sub = _tile(tm, 256, 8)

    def body(da_ref, db_ref, w_ref, x_ref, g_ref, dy_ref, dx_ref, dg_ref, acc_ref):
        i, k = pl.program_id(0), pl.program_id(1)

        @pl.when(k == 0)
        def _():
            acc_ref[...] = jnp.zeros_like(acc_ref)

        @pl.when(k < 2)
        def _():
            acc_ref[...] += _dot(da_ref[...], w_ref[...], 1, 1)

        @pl.when(k >= 2)
        def _():
            acc_ref[...] += _dot(db_ref[...], w_ref[...], 1, 1)

        @pl.when(k == 3)
        def _():
            part = None
            for c in range(tm // sub):
                rows = pl.ds(c * sub, sub)
                xv, dh = x_ref[rows, :], acc_ref[rows, :]
                r = lax.rsqrt(jnp.mean(xv * xv, axis=-1, keepdims=True) + EPS)
                u = dh * g_ref[...]
                dx_ref[rows, :] = dy_ref[rows, :] + (r * u - xv * (r * r * r) * jnp.mean(xv * u, axis=-1, keepdims=True))
                p = jnp.sum(dh * xv * r, axis=0, keepdims=True)
                part = p if part is None else part + p

            @pl.when(i == 0)
            def _():
                dg_ref[...] = part

            @pl.when(i > 0)
            def _():
                dg_ref[...] += part

    row = pl.BlockSpec((tm, D), lambda i, k: (i, 0))
    vec = pl.BlockSpec((1, D), lambda i, k: (0, 0))
    return pl.pallas_call(
        body, grid=(S // tm, 4),
        in_specs=[pl.BlockSpec((tm, Fc), lambda i, k: (i, jnp.minimum(k, 1))),
                  pl.BlockSpec((tm, Fc), lambda i, k: (i, jnp.maximum(k - 2, 0))),
                  pl.BlockSpec((None, D, Fc), lambda i, k: (k, 0, 0)), row, vec, row],
        out_specs=[row, vec],
        out_shape=[jax.ShapeDtypeStruct((S, D), F32), jax.ShapeDtypeStruct((1, D), F32)],
        scratch_shapes=[pltpu.VMEM((tm, D), F32)],
        compiler_params=_params(("arbitrary", "arbitrary")), name=name)(da, db, w_slab, x, g.reshape(1, D), dy)


def grad_cols(h, da, db, *, name):
    S, D = h.shape
    Fc = da.shape[1] // 2
    tk = _tile(S, 1024, 16)
    nk = S // tk

    def body(h_ref, da_ref, db_ref, o_ref, acc_ref):
        ch, k = pl.program_id(0), pl.program_id(1)

        @pl.when(k == 0)
        def _():
            acc_ref[...] = jnp.zeros_like(acc_ref)

        @pl.when(ch < 2)
        def _():
            acc_ref[...] += _dot(h_ref[...], da_ref[...], 0, 0)

        @pl.when(ch >= 2)
        def _():
            acc_ref[...] += _dot(h_ref[...], db_ref[...], 0, 0)

        @pl.when(k == nk - 1)
        def _():
            o_ref[...] = acc_ref[...]

    return pl.pallas_call(
        body, grid=(4, nk),
        in_specs=[pl.BlockSpec((tk, D), lambda ch, k: (k, 0)),
                  pl.BlockSpec((tk, Fc), lambda ch, k: (jnp.where(ch < 2, k, 0), jnp.minimum(ch, 1))),
                  pl.BlockSpec((tk, Fc), lambda ch, k: (jnp.where(ch >= 2, k, 0), jnp.maximum(ch - 2, 0)))],
        out_specs=pl.BlockSpec((None, D, Fc), lambda ch, k: (ch, 0, 0)),
        out_shape=jax.ShapeDtypeStruct((4, D, Fc), F32),
        scratch_shapes=[pltpu.VMEM((D, Fc), F32)],
        compiler_params=_params(("parallel", "arbitrary")), name=name)(h, da, db)


def grad_rows(a, b, *, scale=None, name):
    S, M = a.shape
    N = b.shape[1]
    R = M // 4
    tn = _tile(N, 512, LANES)
    tk = _tile(S, 1024, 16)
    nk = S // tk

    def body(a_ref, b_ref, o_ref, acc_ref):
        k = pl.program_id(1)

        @pl.when(k == 0)
        def _():
            acc_ref[...] = jnp.zeros_like(acc_ref)

        acc_ref[...] += _dot(a_ref[...].astype(BF16), b_ref[...].astype(BF16), 0, 0)

        @pl.when(k == nk - 1)
        def _():
            for d in range(4):
                part = acc_ref[d * R:(d + 1) * R, :]
                o_ref[d] = part if scale is None else part * scale

    return pl.pallas_call(
        body, grid=(N // tn, nk),
        in_specs=[pl.BlockSpec((tk, M), lambda j, k: (k, 0)), pl.BlockSpec((tk, tn), lambda j, k: (k, j))],
        out_specs=pl.BlockSpec((4, R, tn), lambda j, k: (0, 0, j)),
        out_shape=jax.ShapeDtypeStruct((4, R, N), F32),
        scratch_shapes=[pltpu.VMEM((M, tn), F32)],
        compiler_params=_params(("parallel", "arbitrary")), name=name)(a, b)


def rms_fwd(x, g, *, name):
    S, D = x.shape
    ts = _tile(S, 1024, 16)

    def body(x_ref, g_ref, h_ref):
        xv = x_ref[...]
        r = lax.rsqrt(jnp.mean(xv * xv, axis=-1, keepdims=True) + EPS)
        h_ref[...] = (xv * r * g_ref[...]).astype(BF16)

    return pl.pallas_call(
        body, grid=(S // ts,),
        in_specs=[pl.BlockSpec((ts, D), lambda i: (i, 0)), pl.BlockSpec((1, D), lambda i: (0, 0))],
        out_specs=pl.BlockSpec((ts, D), lambda i: (i, 0)),
        out_shape=jax.ShapeDtypeStruct((S, D), BF16),
        compiler_params=_params(("parallel",)), name=name)(x, g.reshape(1, D))


def rms_bwd(x, dh, g, res, *, name):
    S, D = x.shape
    ts = _tile(S, 512, 16)
    has_res = res is not None

    def body(*refs):
        x_ref, dh_ref, g_ref = refs[:3]
        res_ref = refs[3] if has_res else None
        dx_ref, dg_ref = refs[-2:]
        i = pl.program_id(0)
        xv, dhv = x_ref[...], dh_ref[...].astype(F32)
        r = lax.rsqrt(jnp.mean(xv * xv, axis=-1, keepdims=True) + EPS)
        u = dhv * g_ref[...]
        dx = r * u - xv * (r * r * r) * jnp.mean(xv * u, axis=-1, keepdims=True)
        if has_res:
            dx = res_ref[...] + dx
        dx_ref[...] = dx
        part = jnp.sum(dhv * xv * r, axis=0, keepdims=True)

        @pl.when(i == 0)
        def _():
            dg_ref[...] = part

        @pl.when(i > 0)
        def _():
            dg_ref[...] += part

    row = pl.BlockSpec((ts, D), lambda i: (i, 0))
    vec = pl.BlockSpec((1, D), lambda i: (0, 0))
    args = (x, dh, g.reshape(1, D)) + ((res,) if has_res else ())
    return pl.pallas_call(
        body, grid=(S // ts,), in_specs=[row, row, vec] + ([row] if has_res else []),
        out_specs=[row, vec],
        out_shape=[jax.ShapeDtypeStruct((S, D), F32), jax.ShapeDtypeStruct((1, D), F32)],
        compiler_params=_params(("arbitrary",)), name=name)(*args)


def _low_half(shape):
    return lax.broadcasted_iota(jnp.int32, shape, len(shape) - 1) < HEAD_DIM


def _half_sums(x, low):
    sa = jnp.sum(jnp.where(low, x, 0.0), axis=1, keepdims=True)
    sb = jnp.sum(jnp.where(low, 0.0, x), axis=1, keepdims=True)
    return jnp.where(low, sa, sb)


def pairnorm_fwd(x, col0, n_pairs, g, *, scale=None, name):
    S = x.shape[0]
    ts = _tile(S, 1024, 16)

    def body(x_ref, g_ref, o_ref):
        xv = x_ref[...]
        r = lax.rsqrt(_half_sums(xv * xv, _low_half(xv.shape)) * (1.0 / HEAD_DIM) + EPS)
        y = xv * r * g_ref[...]
        o_ref[...] = (y if scale is None else y * scale).astype(BF16)

    return pl.pallas_call(
        body, grid=(S // ts, n_pairs),
        in_specs=[pl.BlockSpec((ts, LANES), lambda i, j: (i, col0 + j)), pl.BlockSpec((1, LANES), lambda i, j: (0, 0))],
        out_specs=pl.BlockSpec((ts, LANES), lambda i, j: (i, j)),
        out_shape=jax.ShapeDtypeStruct((S, n_pairs * LANES), BF16),
        compiler_params=_params(("parallel", "parallel")), name=name)(x, jnp.tile(g.reshape(1, HEAD_DIM), (1, 2)))


def pairnorm_bwd(x, col0, n_pairs, dy, g, *, name):
    S = x.shape[0]
    ts = _tile(S, 1024, 16)

    def body(x_ref, dy_ref, g_ref, dx_ref, dg_ref):
        first = jnp.logical_and(pl.program_id(0) == 0, pl.program_id(1) == 0)
        xv, dyv = x_ref[...], dy_ref[...]
        low = _low_half(xv.shape)
        r = lax.rsqrt(_half_sums(xv * xv, low) * (1.0 / HEAD_DIM) + EPS)
        u = dyv * g_ref[...]
        dx_ref[...] = r * u - xv * (r * r * r) * (_half_sums(xv * u, low) * (1.0 / HEAD_DIM))
        part = jnp.sum(dyv * xv * r, axis=0, keepdims=True)

        @pl.when(first)
        def _():
            dg_ref[...] = part

        @pl.when(jnp.logical_not(first))
        def _():
            dg_ref[...] += part

    vec = pl.BlockSpec((1, LANES), lambda i, j: (0, 0))
    blk = pl.BlockSpec((ts, LANES), lambda i, j: (i, j))
    return pl.pallas_call(
        body, grid=(S // ts, n_pairs),
        in_specs=[pl.BlockSpec((ts, LANES), lambda i, j: (i, col0 + j)), blk, vec], out_specs=[blk, vec],
        out_shape=[jax.ShapeDtypeStruct((S, n_pairs * LANES), F32), jax.ShapeDtypeStruct((1, LANES), F32)],
        compiler_params=_params(("arbitrary", "arbitrary")), name=name)(x, dy, jnp.tile(g.reshape(1, HEAD_DIM), (1, 2)))


def _split3(x):
    x1 = x.astype(BF16)
    r1 = x - x1.astype(F32)
    x2 = r1.astype(BF16)
    x3 = (r1 - x2.astype(F32)).astype(BF16)
    return x1, x2, x3


def _tri_ones(n, lower):
    r = lax.broadcasted_iota(jnp.int32, (n, n), 0)
    c = lax.broadcasted_iota(jnp.int32, (n, n), 1)
    return jnp.where((c <= r) if lower else (c >= r), 1.0, 0.0).astype(BF16)


def fgate_fwd(z, col0, bias, n_heads, *, name):
    S, L = z.shape[0], LANES
    tb = _tile(S, 256, 16)

    def body(z_ref, b_ref, c_ref, col_ref, carry):
        i = pl.program_id(0)

        @pl.when(i == 0)
        def _():
            carry[...] = jnp.zeros_like(carry)

        zz = z_ref[...] + b_ref[...]
        lf = jnp.minimum(zz, 0.0) - jnp.log(1.0 + jnp.exp(-jnp.abs(zz)))
        tri = _tri_ones(tb, True)
        x1, x2, x3 = _split3(lf)
        c = (_dot(tri, x1, 1, 0) + _dot(tri, x2, 1, 0)) + _dot(tri, x3, 1, 0) + carry[...]
        c_ref[...] = c
        lane = lax.broadcasted_iota(jnp.int32, c.shape, 1)
        for h in range(n_heads):
            col_ref[h // 2, h % 2] = jnp.sum(jnp.where(lane == h, c, 0.0), axis=1, keepdims=True)
        carry[...] += jnp.sum(lf, axis=0, keepdims=True)

    return pl.pallas_call(
        body, grid=(S // tb,),
        in_specs=[pl.BlockSpec((tb, L), lambda i: (i, col0)), pl.BlockSpec((1, L), lambda i: (0, 0))],
        out_specs=[pl.BlockSpec((tb, L), lambda i: (i, 0)),
                   pl.BlockSpec((n_heads // 2, 2, tb, 1), lambda i: (0, 0, i, 0))],
        out_shape=[jax.ShapeDtypeStruct((S, L), F32), jax.ShapeDtypeStruct((n_heads // 2, 2, S, 1), F32)],
        scratch_shapes=[pltpu.VMEM((1, L), F32)],
        compiler_params=_params(("arbitrary",)), name=name)(z, bias)


def fgate_bwd(z, col0, bias, drs, dcs, *, name):
    S, L = z.shape[0], LANES
    tb = _tile(S, 256, 16)
    nb = S // tb

    def body(z_ref, b_ref, drs_ref, dcs_ref, dz_ref, db_ref, carry):
        i = pl.program_id(0)

        @pl.when(i == 0)
        def _():
            carry[...] = jnp.zeros_like(carry)

        tri = _tri_ones(tb, False)
        dc = drs_ref[...] - dcs_ref[...]
        x1, x2, x3 = _split3(dc)
        dlf = (_dot(tri, x1, 1, 0) + _dot(tri, x2, 1, 0)) + _dot(tri, x3, 1, 0) + carry[...]
        carry[...] += jnp.sum(dc, axis=0, keepdims=True)
        dz = dlf * _sigmoid(-(z_ref[...] + b_ref[...]))
        dz_ref[...] = dz
        part = jnp.sum(dz, axis=0, keepdims=True)

        @pl.when(i == 0)
        def _():
            db_ref[...] = part

        @pl.when(i > 0)
        def _():
            db_ref[...] += part

    rev = pl.BlockSpec((tb, L), lambda i: (nb - 1 - i, 0))
    vec = pl.BlockSpec((1, L), lambda i: (0, 0))
    return pl.pallas_call(
        body, grid=(nb,), in_specs=[pl.BlockSpec((tb, L), lambda i: (nb - 1 - i, col0)), vec, rev, rev],
        out_specs=[rev, vec],
        out_shape=[jax.ShapeDtypeStruct((S, L), F32), jax.ShapeDtypeStruct((1, L), F32)],
        scratch_shapes=[pltpu.VMEM((1, L), F32)],
        compiler_params=_params(("arbitrary",)), name=name)(z, bias, drs, dcs)


def _one_head(x, low, a):
    return jnp.where(low if a == 0 else jnp.logical_not(low), x, jnp.zeros_like(x))


class Rider(NamedTuple):
    inputs: tuple
    out_shapes: tuple
    aliases: dict
    sems: tuple
    plan: Callable


def _with_rider(rider, n_in, n_out, n_scratch):
    if rider is None:
        return [], [], [], [], {}, lambda refs: (refs[:n_in], refs[n_in:n_in + n_out], refs[n_in + n_out:], None)
    e_in, e_out = len(rider.inputs), len(rider.out_shapes)

    def split(refs):
        ins, r_in = refs[:n_in], refs[n_in:n_in + e_in]
        o0 = n_in + e_in
        outs, r_out = refs[o0:o0 + n_out], refs[o0 + n_out:o0 + n_out + e_out]
        s0 = o0 + n_out + e_out
        return ins, outs, refs[s0:s0 + n_scratch], rider.plan(r_in, r_out, refs[s0 + n_scratch:])

    aliases = {n_in + a: n_out + b for a, b in rider.aliases.items()}
    return list(rider.inputs), [_ANY] * e_in, list(rider.out_shapes), [_ANY] * e_out, aliases, split


def attn_fwd(q, q0, k, k0, v, v0, n_pairs, decay, *, causal, rider=None, name):
    Sq, Sk = q.shape[0], k.shape[0]
    tq = _tile(Sq, ATTN_Q_BLOCK, LANES)
    tk = _tile(Sk, ATTN_K_BLOCK, LANES)
    nq, nk = Sq // tq, Sk // tk
    bias = decay is not None
    r_args, r_in_specs, r_shapes, r_out_specs, aliases, split = _with_rider(rider, 5 if bias else 3, 2, 2)

    def row_sum_lanes(acc, low, a):
        other = jnp.logical_not(low) if a == 0 else low
        return jnp.max(jnp.where(other, acc, 0.0), axis=1, keepdims=True)

    def body(*refs):
        ins, (o_ref, lse_ref), (m_sc, acc_sc), ride = split(refs)
        q_ref, k_ref, v_ref = ins[:3]
        cq_ref, ck_ref = (ins[3], ins[4]) if bias else (None, None)
        pr, i, j = pl.program_id(0), pl.program_id(1), pl.program_id(2)
        if ride is not None:
            pl.when(jnp.logical_and(pr == 0, jnp.logical_and(i == 0, j == 0)))(ride[0])

        @pl.when(j == 0)
        def _():
            m_sc[...] = jnp.full_like(m_sc, NEG)
            acc_sc[...] = jnp.zeros_like(acc_sc)

        def compute(masked):
            qv, kv, vv = q_ref[...], k_ref[...], v_ref[...].astype(BF16)
            low, low_k = _low_half(qv.shape), _low_half(kv.shape)
            for a in range(2):
                s = _dot(_one_head(qv, low, a), kv, 1, 1)
                if bias:
                    s = s + (cq_ref[a] - ck_ref[a])
                if masked:
                    row = i * tq + lax.broadcasted_iota(jnp.int32, (tq, tk), 0)
                    col = j * tk + lax.broadcasted_iota(jnp.int32, (tq, tk), 1)
                    s = jnp.where(col <= row, s, NEG)
                m_prev = m_sc[a]
                m_new = jnp.maximum(m_prev, jnp.max(s, axis=1, keepdims=True))
                alpha = jnp.exp(m_prev - m_new)
                p = jnp.exp((s - m_new).astype(BF16))
                va = jnp.where(low_k if a == 0 else jnp.logical_not(low_k), vv, jnp.ones_like(vv))
                acc_sc[a] = alpha * acc_sc[a] + _dot(p, va, 1, 0)
                m_sc[a] = m_new

        if causal:
            live = j * tk <= i * tq + (tq - 1)
            crosses = j * tk + (tk - 1) > i * tq
            pl.when(jnp.logical_and(live, crosses))(functools.partial(compute, True))
            pl.when(jnp.logical_and(live, jnp.logical_not(crosses)))(functools.partial(compute, False))
        else:
            compute(False)

        @pl.when(j == nk - 1)
        def _():
            low = _low_half((tq, LANES))
            l = [row_sum_lanes(acc_sc[a], low, a) for a in range(2)]
            o_ref[...] = jnp.where(low, acc_sc[0] / l[0], acc_sc[1] / l[1])
            for a in range(2):
                lse_ref[a] = m_sc[a] + jnp.log(l[a])

        if ride is not None:
            pl.when(jnp.logical_and(pr == n_pairs - 1, jnp.logical_and(i == nq - 1, j == nk - 1)))(ride[1])

    def kv_blk(i, j):
        return jnp.minimum(j, (i * tq + tq - 1) // tk) if causal else j

    in_specs = [pl.BlockSpec((tq, LANES), lambda p, i, j: (i, q0 + p)),
                pl.BlockSpec((tk, LANES), lambda p, i, j: (kv_blk(i, j), k0 + p)),
                pl.BlockSpec((tk, LANES), lambda p, i, j: (kv_blk(i, j), v0 + p))]
    args = [q, k, v]
    if bias:
        in_specs += [pl.BlockSpec((None, 2, tq, 1), lambda p, i, j: (p, 0, i, 0)),
                     pl.BlockSpec((None, 2, 1, tk), lambda p, i, j: (p, 0, 0, kv_blk(i, j)))]
        args += list(decay)
    out = pl.pallas_call(
        body, grid=(n_pairs, nq, nk), in_specs=in_specs + r_in_specs,
        out_specs=[pl.BlockSpec((tq, LANES), lambda p, i, j: (i, p)),
                   pl.BlockSpec((None, 2, tq, 1), lambda p, i, j: (p, 0, i, 0))] + r_out_specs,
        out_shape=[jax.ShapeDtypeStruct((Sq, n_pairs * LANES), F32),
                   jax.ShapeDtypeStruct((n_pairs, 2, Sq, 1), F32)] + r_shapes,
        scratch_shapes=[pltpu.VMEM((2, tq, 1), F32), pltpu.VMEM((2, tq, LANES), F32)]
        + (list(rider.sems) if rider else []),
        input_output_aliases=aliases,
        compiler_params=_params(("arbitrary",) * 3 if rider else ("parallel", "parallel", "arbitrary")),
        name=name)(*args, *r_args)
    return out[0], out[1], out[2:]


def attn_delta(o, do, do0, n_pairs, *, name):
    S = o.shape[0]
    ts = _tile(S, 1024, 16)

    def body(o_ref, do_ref, out_ref):
        prod = o_ref[...] * do_ref[...]
        low = _low_half(prod.shape)
        out_ref[0] = jnp.sum(jnp.where(low, prod, 0.0), axis=1, keepdims=True)
        out_ref[1] = jnp.sum(jnp.where(low, 0.0, prod), axis=1, keepdims=True)

    return pl.pallas_call(
        body, grid=(n_pairs, S // ts),
        in_specs=[pl.BlockSpec((ts, LANES), lambda p, i: (i, p)), pl.BlockSpec((ts, LANES), lambda p, i: (i, do0 + p))],
        out_specs=pl.BlockSpec((None, 2, ts, 1), lambda p, i: (p, 0, i, 0)),
        out_shape=jax.ShapeDtypeStruct((n_pairs, 2, S, 1), F32),
        compiler_params=_params(("parallel", "parallel")), name=name)(o, do)


def attn_bwd(q, q0, k, k0, v, v0, do, do0, n_pairs, lse, delta, decay, *, causal, rider=None, name):
    Sq, Sk = q.shape[0], k.shape[0]
    tq = _tile(Sq, ATTN_Q_BLOCK, LANES)
    tk = _tile(Sk, ATTN_K_BLOCK, LANES)
    nq, nk = Sq // tq, Sk // tk
    bias = decay is not None

    r_args, r_in_specs, r_shapes, r_out_specs, aliases, split = _with_rider(
        rider, 8 if bias else 6, 5 if bias else 3, 0)

    def body(*refs):
        ins, outs, _, ride = split(refs)
        q_ref, k_ref, v_ref, do_ref, lse_ref, dl_ref = ins[:6]
        cq_ref, ck_ref = (ins[6], ins[7]) if bias else (None, None)
        dq_ref, dk_ref, dv_ref = outs[:3]
        dcs_ref, drs_ref = (outs[3], outs[4]) if bias else (None, None)
        pr, j, i = pl.program_id(0), pl.program_id(1), pl.program_id(2)
        if ride is not None:
            pl.when(jnp.logical_and(pr == 0, jnp.logical_and(i == 0, j == 0)))(ride[0])

        @pl.when(i == 0)
        def _():
            dk_ref[...] = jnp.zeros_like(dk_ref)
            dv_ref[...] = jnp.zeros_like(dv_ref)
            if bias:
                dcs_ref[...] = jnp.zeros_like(dcs_ref)

        rows = pl.ds(pl.multiple_of(i * tq, tq), tq)

        def compute(masked):
            qv, kv, vv, dov = q_ref[...], k_ref[...], v_ref[...].astype(BF16), do_ref[...].astype(BF16)
            low, low_k = _low_half(qv.shape), _low_half(kv.shape)
            dq_part, row_parts = None, []
            for a in range(2):
                qa, ka, doa = _one_head(qv, low, a), _one_head(kv, low_k, a), _one_head(dov, low, a)
                s = _dot(qa, kv, 1, 1)
                if bias:
                    s = s + (cq_ref[a] - ck_ref[a])
                p = jnp.exp(s - lse_ref[a])
                if masked:
                    row = i * tq + lax.broadcasted_iota(jnp.int32, (tq, tk), 0)
                    col = j * tk + lax.broadcasted_iota(jnp.int32, (tq, tk), 1)
                    p = jnp.where(col <= row, p, 0.0)
                dv_ref[...] += _dot(p.astype(BF16), doa, 0, 0)
                dp = _dot(doa, vv, 1, 1)
                ds = p * (dp - dl_ref[a])
                dsb = ds.astype(BF16)
                dk_ref[...] += _dot(dsb, qa, 0, 0)
                if bias:
                    dcs_ref[a] += jnp.sum(ds, axis=0, keepdims=True)
                    row_parts.append(jnp.sum(ds, axis=1, keepdims=True))
                part = _dot(dsb, ka, 1, 0) * QK_SCALE
                dq_part = part if dq_part is None else dq_part + part

            @pl.when(j == 0)
            def _():
                dq_ref[rows, :] = dq_part
                for a, rp in enumerate(row_parts):
                    drs_ref[a, rows, :] = rp

            @pl.when(j > 0)
            def _():
                dq_ref[rows, :] += dq_part
                for a, rp in enumerate(row_parts):
                    drs_ref[a, rows, :] += rp

        if causal:
            live = j * tk <= i * tq + (tq - 1)
            crosses = j * tk + (tk - 1) > i * tq
            pl.when(jnp.logical_and(live, crosses))(functools.partial(compute, True))
            pl.when(jnp.logical_and(live, jnp.logical_not(crosses)))(functools.partial(compute, False))
        else:
            compute(False)

        if ride is not None:
            pl.when(jnp.logical_and(pr == n_pairs - 1, jnp.logical_and(i == nq - 1, j == nk - 1)))(ride[1])

    def q_blk(j, i):
        return jnp.maximum(i, (j * tk) // tq) if causal else i

    col1 = pl.BlockSpec((None, 2, tq, 1), lambda p, j, i: (p, 0, q_blk(j, i), 0))
    in_specs = [pl.BlockSpec((tq, LANES), lambda p, j, i: (q_blk(j, i), q0 + p)),
                pl.BlockSpec((tk, LANES), lambda p, j, i: (j, k0 + p)),
                pl.BlockSpec((tk, LANES), lambda p, j, i: (j, v0 + p)),
                pl.BlockSpec((tq, LANES), lambda p, j, i: (q_blk(j, i), do0 + p)), col1, col1]
    args = [q, k, v, do, lse, delta]
    kout = pl.BlockSpec((tk, LANES), lambda p, j, i: (j, p))
    out_specs = [pl.BlockSpec((Sq, LANES), lambda p, j, i: (0, p)), kout, kout]
    out_shape = [jax.ShapeDtypeStruct((Sq, n_pairs * LANES), F32), jax.ShapeDtypeStruct((Sk, n_pairs * LANES), F32),
                 jax.ShapeDtypeStruct((Sk, n_pairs * LANES), F32)]
    if bias:
        in_specs += [col1, pl.BlockSpec((None, 2, 1, tk), lambda p, j, i: (p, 0, 0, j))]
        args += list(decay)
        out_specs += [pl.BlockSpec((None, 2, 1, tk), lambda p, j, i: (p, 0, 0, j)),
                      pl.BlockSpec((None, 2, Sq, 1), lambda p, j, i: (p, 0, 0, 0))]
        out_shape += [jax.ShapeDtypeStruct((n_pairs, 2, 1, Sk), F32), jax.ShapeDtypeStruct((n_pairs, 2, Sq, 1), F32)]
    n_own = len(out_shape)
    out = pl.pallas_call(
        body, grid=(n_pairs, nk, nq), in_specs=in_specs + r_in_specs, out_specs=out_specs + r_out_specs,
        out_shape=out_shape + r_shapes, scratch_shapes=list(rider.sems) if rider else [],
        input_output_aliases=aliases,
        compiler_params=_params(("arbitrary",) * 3 if rider else ("parallel", "arbitrary", "arbitrary")),
        name=name)(*args, *r_args)
    return tuple(out[:n_own]), out[n_own:]


def _tril_mask(n):
    r = lax.broadcasted_iota(jnp.int32, (n, n), 0)
    c = lax.broadcasted_iota(jnp.int32, (n, n), 1)
    return c <= r


def _gmlp_operands(v_gain, w_s, b_s):
    G = w_s.shape[0]
    return (v_gain.reshape(G // 2, 1, LANES), w_s.reshape(G // 2, 2, CHUNK, CHUNK), b_s.reshape(G // 2, 2, CHUNK, 1))


def _gmlp_gate(wt, vh, b_ref, low):
    gate = _dot(wt[0], _one_head(vh, low, 0), 1, 0) + _dot(wt[1], _one_head(vh, low, 1), 1, 0)
    return gate + jnp.where(low, b_ref[0], b_ref[1])


def gmlp_fwd(proj, v0, n_pairs, vg, w, b, *, name):
    S = proj.shape[0]
    ts = _tile(S, 1024, CHUNK)

    def body(up_ref, vp_ref, vg_ref, w_ref, b_ref, o_ref):
        mask = _tril_mask(CHUNK)
        wt = [jnp.where(mask, w_ref[a], 0.0).astype(BF16) for a in range(2)]
        low = _low_half((CHUNK, LANES))
        for c in range(ts // CHUNK):
            sl = pl.ds(c * CHUNK, CHUNK)
            vz = _gelu(vp_ref[sl, :])
            r = lax.rsqrt(_half_sums(vz * vz, low) * (1.0 / HEAD_DIM) + EPS)
            vh = (vz * r * vg_ref[...]).astype(BF16)
            o_ref[sl, :] = _gelu(up_ref[sl, :]) * _gmlp_gate(wt, vh, b_ref, low)

    return pl.pallas_call(
        body, grid=(n_pairs, S // ts),
        in_specs=[pl.BlockSpec((ts, LANES), lambda p, i: (i, p)), pl.BlockSpec((ts, LANES), lambda p, i: (i, v0 + p)),
                  pl.BlockSpec((None, 1, LANES), lambda p, i: (p, 0, 0)),
                  pl.BlockSpec((None, 2, CHUNK, CHUNK), lambda p, i: (p, 0, 0, 0)),
                  pl.BlockSpec((None, 2, CHUNK, 1), lambda p, i: (p, 0, 0, 0))],
        out_specs=pl.BlockSpec((ts, LANES), lambda p, i: (i, p)),
        out_shape=jax.ShapeDtypeStruct((S, n_pairs * LANES), F32),
        compiler_params=_params(("parallel", "parallel")), name=name)(proj, proj, vg, w, b)


def gmlp_bwd(proj, v0, n_pairs, vg, w, wT, b, do, *, name):
    S = proj.shape[0]
    ts = _tile(S, 1024, CHUNK)

    def body(up_ref, vp_ref, vg_ref, w_ref, wT_ref, b_ref, do_ref, dup_ref, dvp_ref, dw_ref, db_ref, dvg_ref):
        i = pl.program_id(1)

        @pl.when(i == 0)
        def _():
            dw_ref[...] = jnp.zeros_like(dw_ref)
            db_ref[...] = jnp.zeros_like(db_ref)
            dvg_ref[...] = jnp.zeros_like(dvg_ref)

        mask = _tril_mask(CHUNK)
        wt = [jnp.where(mask, w_ref[a], 0.0).astype(BF16) for a in range(2)]
        wtT = [jnp.where(mask.T, wT_ref[a], 0.0).astype(BF16) for a in range(2)]
        low = _low_half((CHUNK, LANES))
        vgain = vg_ref[...]
        for c in range(ts // CHUNK):
            sl = pl.ds(c * CHUNK, CHUNK)
            u_pre, v_pre, dout = up_ref[sl, :], vp_ref[sl, :], do_ref[sl, :]
            vz = _gelu(v_pre)
            r = lax.rsqrt(_half_sums(vz * vz, low) * (1.0 / HEAD_DIM) + EPS)
            vh = (vz * r * vgain).astype(BF16)
            gate = _gmlp_gate(wt, vh, b_ref, low)
            dgate = dout * _gelu(u_pre)
            dup_ref[sl, :] = dout * gate * _gelu_grad(u_pre)
            dvh = None
            for a in range(2):
                dga = _one_head(dgate, low, a)
                dgb = dga.astype(BF16)
                dw_ref[a] += jnp.where(mask, _dot(dgb, vh, 1, 1), 0.0)
                db_ref[a] += jnp.sum(dga, axis=1, keepdims=True)
                part = _dot(wtT[a], dgb, 1, 0)
                dvh = part if dvh is None else dvh + part
            dvg_ref[...] += jnp.sum(dvh * vz * r, axis=0, keepdims=True)
            t = dvh * vgain
            dvz = r * t - vz * (r * r * r) * (_half_sums(vz * t, low) * (1.0 / HEAD_DIM))
            dvp_ref[sl, :] = dvz * _gelu_grad(v_pre)

    ublk = pl.BlockSpec((ts, LANES), lambda p, i: (i, p))
    wblk = pl.BlockSpec((None, 2, CHUNK, CHUNK), lambda p, i: (p, 0, 0, 0))
    bblk = pl.BlockSpec((None, 2, CHUNK, 1), lambda p, i: (p, 0, 0, 0))
    gblk = pl.BlockSpec((None, 1, LANES), lambda p, i: (p, 0, 0))
    return pl.pallas_call(
        body, grid=(n_pairs, S // ts),
        in_specs=[ublk, pl.BlockSpec((ts, LANES), lambda p, i: (i, v0 + p)), gblk, wblk, wblk, bblk, ublk],
        out_specs=[ublk, ublk, wblk, bblk, gblk],
        out_shape=[jax.ShapeDtypeStruct((S, n_pairs * LANES), F32), jax.ShapeDtypeStruct((S, n_pairs * LANES), F32),
                   jax.ShapeDtypeStruct((n_pairs, 2, CHUNK, CHUNK), F32), jax.ShapeDtypeStruct((n_pairs, 2, CHUNK, 1), F32),
                   jax.ShapeDtypeStruct((n_pairs, 1, LANES), F32)],
        compiler_params=_params(("parallel", "arbitrary")), name=name)(proj, proj, vg, w, wT, b, do)


def loss_head(y, target, *, name):
    S, D = y.shape
    ts = _tile(S, 512, 8)

    def body(y_ref, t_ref, dy_ref, loss_ref):
        i = pl.program_id(0)
        e = y_ref[...] - t_ref[...]
        dy_ref[...] = e * (1.0 / D)
        part = jnp.sum(jnp.sum(e * e, axis=1, keepdims=True), axis=0, keepdims=True) * (0.5 / D)

        @pl.when(i == 0)
        def _():
            loss_ref[...] = part

        @pl.when(i > 0)
        def _():
            loss_ref[...] += part

    row = pl.BlockSpec((ts, D), lambda i: (i, 0))
    return pl.pallas_call(
        body, grid=(S // ts,), in_specs=[row, row],
        out_specs=[row, pl.BlockSpec((1, 1), lambda i: (0, 0))],
        out_shape=[jax.ShapeDtypeStruct((S, D), F32), jax.ShapeDtypeStruct((1, 1), F32)],
        compiler_params=_params(("arbitrary",)), name=name)(y, target)


def adamw(w, g, m, v, *, name):
    shape = w.shape
    C = shape[-1]
    R = w.size // C
    tr = _tile(R, max(8, (256 * 1024) // C // 8 * 8), 8)

    def body(w_ref, g_ref, m_ref, v_ref, d_ref, nm_ref, nv_ref):
        gv = g_ref[...]
        nm = ADAM_B1 * m_ref[...] + (1.0 - ADAM_B1) * gv
        nv = ADAM_B2 * v_ref[...] + (1.0 - ADAM_B2) * (gv * gv)
        m_hat = nm / (1.0 - ADAM_B1 ** ADAM_STEP)
        v_hat = nv / (1.0 - ADAM_B2 ** ADAM_STEP)
        d_ref[...] = -ADAM_LR * (m_hat / (jnp.sqrt(v_hat) + ADAM_EPS) + ADAM_WD * w_ref[...])
        nm_ref[...] = nm
        nv_ref[...] = nv

    blk = pl.BlockSpec((tr, C), lambda i: (i, 0))
    out = pl.pallas_call(
        body, grid=(R // tr,), in_specs=[blk] * 4, out_specs=[blk] * 3,
        out_shape=[jax.ShapeDtypeStruct((R, C), F32)] * 3,
        compiler_params=_params(("parallel",)), name=name)(*(a.reshape(R, C) for a in (w, g, m, v)))
    return tuple(o.reshape(shape) for o in out)


def pair_sum(p, landed, half, *, name):
    n, R, C = landed.shape
    tr = _tile(R, 256, 16)
    nr = R // tr

    def body(half_ref, p_ref, l_ref, o_ref):
        o_ref[...] = (p_ref[...] + l_ref[...]).astype(BF16)

    return pl.pallas_call(
        body,
        grid_spec=pltpu.PrefetchScalarGridSpec(
            num_scalar_prefetch=1, grid=(n, nr),
            in_specs=[pl.BlockSpec((None, tr, C), lambda k, r, half_ref: (k, half_ref[0] * nr + r, 0)),
                      pl.BlockSpec((None, tr, C), lambda k, r, half_ref: (k, r, 0))],
            out_specs=pl.BlockSpec((None, tr, C), lambda k, r, half_ref: (k, r, 0))),
        out_shape=jax.ShapeDtypeStruct((n, R, C), BF16),
        compiler_params=_params(("parallel", "parallel")), name=name)(half, p, landed)


def chip_sum(own, landed, chip, *, name):
    n, R, C = own.shape
    tr = _tile(R, 256, 16)

    def body(chip_ref, own_ref, *rest):
        l_refs, o_ref = rest[:n], rest[n]
        me = chip_ref[0]
        acc = None
        for d in range(n):
            term = jnp.where(me == d, own_ref[...], l_refs[d][...]).astype(F32)
            acc = term if acc is None else acc + term
        o_ref[...] = acc

    def landed_spec(d):
        return pl.BlockSpec((None, tr, C), lambda r, chip_ref: (jnp.where(chip_ref[0] == d, (d + 1) % n, d), r, 0))

    return pl.pallas_call(
        body,
        grid_spec=pltpu.PrefetchScalarGridSpec(
            num_scalar_prefetch=1, grid=(R // tr,),
            in_specs=[pl.BlockSpec((None, tr, C), lambda r, chip_ref: (chip_ref[0], r, 0))]
            + [landed_spec(d) for d in range(n)],
            out_specs=pl.BlockSpec((tr, C), lambda r, chip_ref: (r, 0))),
        out_shape=jax.ShapeDtypeStruct((R, C), F32),
        compiler_params=_params(("parallel",)), name=name)(chip, own, *([landed] * n))


def ordered_sum(parts, *, name):
    n, R, C = parts.shape
    tr = _tile(R, 256, 16)

    def body(p_ref, o_ref):
        acc = p_ref[0].astype(F32)
        for d in range(1, n):
            acc = acc + p_ref[d].astype(F32)
        o_ref[...] = acc

    return pl.pallas_call(
        body, grid=(R // tr,), in_specs=[pl.BlockSpec((n, tr, C), lambda r: (0, r, 0))],
        out_specs=pl.BlockSpec((tr, C), lambda r: (r, 0)),
        out_shape=jax.ShapeDtypeStruct((R, C), F32),
        compiler_params=_params(("parallel",)), name=name)(parts)


_ANY = pl.BlockSpec(memory_space=pl.ANY)


def _position():
    return lax.axis_index("x"), lax.axis_index("y"), lax.axis_index("c")


def _remote(src, dst, send_sem, recv_sem, device):
    return pltpu.make_async_remote_copy(src_ref=src, dst_ref=dst, send_sem=send_sem, recv_sem=recv_sem,
                                        device_id=device, device_id_type=MESH_ID)


def _small_all_gather(s_ref, all_ref, send_sems, recv_sems, x, y, c):
    me = 4 * x + 2 * y + c
    copies = []
    for f in range(1, 8):
        peer = ((1 - x) if f & 4 else x, (1 - y) if f & 2 else y, (1 - c) if f & 1 else c)
        cp = _remote(s_ref, all_ref.at[me], send_sems.at[f - 1], recv_sems.at[f - 1], peer)
        cp.start()
        copies.append((cp, peer, f - 1))

    def finish():
        for cp, peer, s in copies:
            slot = all_ref.at[4 * peer[0] + 2 * peer[1] + peer[2]]
            _remote(slot, slot, send_sems.at[s], recv_sems.at[s], peer).wait_recv()
        for cp, _, _ in copies:
            cp.wait_send()

    return finish


def _core_rows(ref, core):
    h = ref.shape[1] // 2
    return pl.ds(core * h, h)


def _gather_plan(outs, send_sems, recv_sems):
    n = len(outs)
    x, y, c = _position()
    k = 2 * x + y
    sibling = (x, y, 1 - c)
    chips = [(1 - x, y), (x, 1 - y), (1 - x, 1 - y)]

    def first():
        return [_remote(outs[w].at[k, _core_rows(outs[w], c)], outs[w].at[k, _core_rows(outs[w], c)],
                        send_sems.at[w, j], recv_sems.at[w, j], (px, py, c))
                for j, (px, py) in enumerate(chips) for w in range(n)]

    def start():
        for cp in first():
            cp.start()

    def finish():
        passed = []
        for j, (px, py) in enumerate(chips):
            for w in range(n):
                slot = outs[w].at[2 * px + py, _core_rows(outs[w], c)]
                _remote(slot, slot, send_sems.at[w, j], recv_sems.at[w, j], (px, py, c)).wait_recv()
                cp = _remote(slot, slot, send_sems.at[w, 3 + j], recv_sems.at[w, 3 + j], sibling)
                cp.start()
                passed.append(cp)
        for j, (px, py) in enumerate(chips):
            for w in range(n):
                slot = outs[w].at[2 * px + py, _core_rows(outs[w], 1 - c)]
                _remote(slot, slot, send_sems.at[w, 3 + j], recv_sems.at[w, 3 + j], sibling).wait_recv()
        for cp in first() + passed:
            cp.wait_send()

    return start, finish


def _gather_sems(n):
    return (pltpu.SemaphoreType.DMA((n, 6)), pltpu.SemaphoreType.DMA((n, 6)))


def gather_rider(slabs):
    return Rider(tuple(slabs), tuple(jax.ShapeDtypeStruct(a.shape, a.dtype) for a in slabs),
                 {i: i for i in range(len(slabs))}, _gather_sems(len(slabs)),
                 lambda ins, outs, sems: _gather_plan(outs, sems[0], sems[1]))


def gather_weights(slabs, small_slab, *, name):
    n = len(slabs)

    def body(*refs):
        outs, all_ref = refs[n + 1:2 * n + 1], refs[2 * n + 1]
        send_sems, recv_sems, s_send, s_recv = refs[2 * n + 2:]
        x, y, c = _position()
        finish_small = _small_all_gather(all_ref.at[4 * x + 2 * y + c], all_ref, s_send, s_recv, x, y, c)
        start, finish = _gather_plan(outs, send_sems, recv_sems)
        start()
        finish()
        finish_small()

    args = list(slabs) + [small_slab]
    out = pl.pallas_call(
        body, in_specs=[_ANY] * (n + 1), out_specs=[_ANY] * (n + 1),
        out_shape=[jax.ShapeDtypeStruct(a.shape, a.dtype) for a in args],
        input_output_aliases={i: i for i in range(n + 1)},
        scratch_shapes=list(_gather_sems(n)) + [pltpu.SemaphoreType.DMA((7,)), pltpu.SemaphoreType.DMA((7,))],
        name=name)(*args)
    return out[:n], out[n]


def exchange_with_sibling(parts, small_slab, *, name):
    n = len(parts)
    has_small = small_slab is not None
    n_arg = n + (1 if has_small else 0)

    def body(*refs):
        p_refs = refs[:n]
        lands = refs[n_arg:n_arg + n]
        send_sems, recv_sems = refs[2 * n_arg], refs[2 * n_arg + 1]
        x, y, c = _position()
        sibling = (x, y, 1 - c)
        if has_small:
            all_ref = refs[n_arg + n]
            finish_small = _small_all_gather(all_ref.at[4 * x + 2 * y + c], all_ref, refs[2 * n_arg + 2],
                                             refs[2 * n_arg + 3], x, y, c)
        sends = []
        for w in range(n):
            for d in range(4):
                cp = _remote(p_refs[w].at[d, _core_rows(p_refs[w], 1 - c)], lands[w].at[d],
                             send_sems.at[w, d], recv_sems.at[w, d], sibling)
                cp.start()
                sends.append(cp)
        for cp in sends:
            cp.wait_recv()
        for cp in sends:
            cp.wait_send()
        if has_small:
            finish_small()

    small_args = [small_slab] if has_small else []
    out = pl.pallas_call(
        body, in_specs=[_ANY] * n_arg, out_specs=[_ANY] * n_arg,
        out_shape=[jax.ShapeDtypeStruct((4, p.shape[1] // 2, p.shape[2]), p.dtype) for p in parts]
        + [jax.ShapeDtypeStruct(s.shape, s.dtype) for s in small_args],
        input_output_aliases={n: n} if has_small else {},
        scratch_shapes=[pltpu.SemaphoreType.DMA((n, 4)), pltpu.SemaphoreType.DMA((n, 4))]
        + ([pltpu.SemaphoreType.DMA((7,)), pltpu.SemaphoreType.DMA((7,))] if has_small else []),
        name=name)(*parts, *small_args)
    return out[:n], (out[n] if has_small else None)


def _scatter_plan(q_refs, outs, send_sems, recv_sems):
    n = len(q_refs)
    x, y, c = _position()
    k = 2 * x + y
    chips = [(1 - x, y), (x, 1 - y), (1 - x, 1 - y)]

    def sends():
        return [_remote(q_refs[w].at[2 * px + py], outs[w].at[k], send_sems.at[w, j], recv_sems.at[w, j], (px, py, c))
                for j, (px, py) in enumerate(chips) for w in range(n)]

    def start():
        for cp in sends():
            cp.start()

    def finish():
        for j, (px, py) in enumerate(chips):
            for w in range(n):
                slot = outs[w].at[2 * px + py]
                _remote(slot, slot, send_sems.at[w, j], recv_sems.at[w, j], (px, py, c)).wait_recv()
        for cp in sends():
            cp.wait_send()

    return start, finish


def _scatter_sems(n):
    return (pltpu.SemaphoreType.DMA((n, 3)), pltpu.SemaphoreType.DMA((n, 3)))


def scatter_rider(parts):
    return Rider(tuple(parts), tuple(jax.ShapeDtypeStruct(q.shape, q.dtype) for q in parts), {},
                 _scatter_sems(len(parts)), lambda ins, outs, sems: _scatter_plan(ins, outs, sems[0], sems[1]))


def scatter_to_chips(parts, *, name):
    n = len(parts)

    def body(*refs):
        start, finish = _scatter_plan(refs[:n], refs[n:2 * n], refs[2 * n], refs[2 * n + 1])
        start()
        finish()

    return pl.pallas_call(
        body, in_specs=[_ANY] * n, out_specs=[_ANY] * n,
        out_shape=[jax.ShapeDtypeStruct(q.shape, q.dtype) for q in parts],
        scratch_shapes=list(_scatter_sems(n)), name=name)(*parts)


def share_with_sibling(parts, *, name):
    n = len(parts)

    def body(*refs):
        r_refs, outs = refs[:n], refs[n:2 * n]
        send_sems, recv_sems = refs[2 * n:]
        x, y, c = _position()
        sends = []
        for w in range(n):
            cp = _remote(r_refs[w], outs[w], send_sems.at[w], recv_sems.at[w], (x, y, 1 - c))
            cp.start()
            sends.append(cp)
        for cp in sends:
            cp.wait_recv()
        for cp in sends:
            cp.wait_send()

    return pl.pallas_call(
        body, in_specs=[_ANY] * n, out_specs=[_ANY] * n,
        out_shape=[jax.ShapeDtypeStruct(r.shape, r.dtype) for r in parts],
        scratch_shapes=[pltpu.SemaphoreType.DMA((n,)), pltpu.SemaphoreType.DMA((n,))],
        name=name)(*parts)


def _cols_to_chips(full):
    *lead, R, C4 = full.shape
    t = full.reshape(*lead, R, 4, C4 // 4)
    return jnp.moveaxis(t, -2, 0)


def _chips_to_cols(sh):
    t = jnp.moveaxis(sh, 0, -2)
    return t.reshape(*t.shape[:-2], t.shape[-2] * t.shape[-1])


def _slot_in_empty(own, index, n):
    return lax.dynamic_update_slice(lax.empty((n,) + own.shape, own.dtype), own[None], (index,) + (0,) * own.ndim)


def _fold_pair(dg):
    return dg[0, :HEAD_DIM] + dg[0, HEAD_DIM:]


def _ffn_fwd(x, g, w_in_slab, w_out, tag):
    h = rms_fwd(x, g, name=f"{tag}_rms")
    a, b, act = swiglu_fwd(h, w_in_slab, name=f"{tag}_in")
    y = matmul(act, w_out, res=x, scale=0.5, tm=1024, tn=512, tk=w_out.shape[0], name=f"{tag}_out")
    return y, (x, h, a, b, act)


def _ffn_bwd(dy, saved, g, w_in_slab, w_out, tag):
    x, h, a, b, act = saved
    da, db = swiglu_bwd(dy, w_out, a, b, name=f"{tag}_dact")
    dw_out = grad_rows(act, dy, scale=0.5, name=f"{tag}_dwout")
    dw_in = grad_cols(h, da, db, name=f"{tag}_dwin")
    dx, dg = ffn_dh(da, db, w_in_slab, x, g, dy, name=f"{tag}_dh")
    return dx, dg[0], dw_in, dw_out


MEM_PAIRS = MEM_WIDTH // LANES


def _mem_attn_fwd(proj, mq0, mem_n, w_kv, g_q, g_k, tag):
    qh = pairnorm_fwd(proj, mq0, MEM_PAIRS, g_q, scale=QK_SCALE,name=f"{tag}_qnorm")
    kv = matmul(mem_n, w_kv, tm=256, tn=512, tk=1024, name=f"{tag}_kv")
    kh = pairnorm_fwd(kv, 0, MEM_PAIRS, g_k, name=f"{tag}_knorm")
    o, lse, _ = attn_fwd(qh, 0, kh, 0, kv, MEM_PAIRS, MEM_PAIRS, None, causal=False, name=f"{tag}_attn")
    return o, (qh, kv, kh, o, lse)


def _mem_attn_bwd(dmix, do0, proj, mq0, saved, mem_n, g_q, g_k, tag):
    qh, kv, kh, o, lse = saved
    delta = attn_delta(o, dmix, do0, MEM_PAIRS, name=f"{tag}_delta")
    (dqh, dkh, dv), _ = attn_bwd(qh, 0, kh, 0, kv, MEM_PAIRS, dmix, do0, MEM_PAIRS, lse, delta, None,
                                 causal=False, name=f"{tag}_dattn")
    dq_pre, dgq = pairnorm_bwd(proj, mq0, MEM_PAIRS, dqh, g_q, name=f"{tag}_dqnorm")
    dk_pre, dgk = pairnorm_bwd(kv, 0, MEM_PAIRS, dkh, g_k, name=f"{tag}_dknorm")
    dkv = jnp.concatenate([dk_pre, dv], axis=1)
    dw_kv = grad_rows(mem_n, dkv, name=f"{tag}_dwkv")
    return dq_pre, _fold_pair(dgq), _fold_pair(dgk), dw_kv, dkv


def _per_head_lanes(x, H):
    return jnp.pad(x.reshape(H, -1).T, ((0, 0), (0, LANES - H)))


def _fox_fwd(proj, b_f, g_q, g_k, tok, rider, tag):
    H, P = tok // HEAD_DIM, tok // LANES
    bias = jnp.pad(b_f.reshape(1, H), ((0, 0), (0, LANES - H)))
    qh = pairnorm_fwd(proj, 0, P, g_q, scale=QK_SCALE,name=f"{tag}_qnorm")
    kh = pairnorm_fwd(proj, P, P, g_k, name=f"{tag}_knorm")
    c, c_cols = fgate_fwd(proj, 3 * P + MEM_PAIRS, bias, H, name=f"{tag}_fgate")
    decay = (c_cols, c[:, :H].T.reshape(P, 2, 1, c.shape[0]))
    o, lse, rode = attn_fwd(qh, 0, kh, 0, proj, 2 * P, P, decay, causal=True, rider=rider, name=f"{tag}_attn")
    return o, (qh, kh, bias, decay, o, lse), rode


def _fox_bwd(dmix, proj, saved, g_q, g_k, tok, rider, tag):
    qh, kh, bias, decay, o, lse = saved
    H, P = tok // HEAD_DIM, tok // LANES
    delta = attn_delta(o, dmix, 0, P, name=f"{tag}_delta")
    (dqh, dkh, dv, dcs, drs), rode = attn_bwd(qh, 0, kh, 0, proj, 2 * P, dmix, 0, P, lse, delta, decay, causal=True,
                                              rider=rider, name=f"{tag}_dattn")
    dq_pre, dgq = pairnorm_bwd(proj, 0, P, dqh, g_q, name=f"{tag}_dqnorm")
    dk_pre, dgk = pairnorm_bwd(proj, P, P, dkh, g_k, name=f"{tag}_dknorm")
    dz, dbias = fgate_bwd(proj, 3 * P + MEM_PAIRS, bias, _per_head_lanes(drs, H), _per_head_lanes(dcs, H),
                          name=f"{tag}_dfgate")
    dqkv = jnp.concatenate([dq_pre, dk_pre, dv], axis=1)
    return dqkv, dz, dbias[0, :H], _fold_pair(dgq), _fold_pair(dgk), rode


def local_step(x, mem, target, W, comm=None):
    S, D = x.shape
    tok = D - MEM_WIDTH
    P = tok // LANES
    depth = W["norm_ffn1"].shape[0]
    mem_n = rms_fwd(mem, W["mem_norm"], name="mem_rms")
    saved = []
    for i in range(depth):
        kind, j = i % 2, i // 2
        t = f"l{i}"
        x1, s1 = _ffn_fwd(x, W["norm_ffn1"][i], W["ffn1_w_in"][i], W["ffn1_w_out"][i], f"{t}_ffn1")
        h = rms_fwd(x1, W["norm_mix"][i], name=f"{t}_mix_rms")
        w_mix = W["fox_w_in"][j] if kind == 0 else W["gmlp_w_in"][j]
        proj = matmul(h, w_mix, tm=1024, tn=896, tk=D, name=f"{t}_mix_in")
        if kind == 0:
            rider = comm.late_weights_rider() if (comm is not None and i == 0) else None
            o_tok, s_tok, rode = _fox_fwd(proj, W["fox_b_f"][j], W["fox_q_norm"][j], W["fox_k_norm"][j], tok, rider,
                                          f"{t}_fox")
            if rider is not None:
                comm.accept_late_weights(W, rode)
            mq0 = 3 * P
        else:
            vg, ws, bs = _gmlp_operands(W["gmlp_v_norm"][j], W["gmlp_w_s"][j], W["gmlp_b_s"][j])
            o_tok = gmlp_fwd(proj, P, P, vg, ws, bs, name=f"{t}_gmlp")
            s_tok = None
            mq0 = 2 * P
        o_mem, s_mem = _mem_attn_fwd(proj, mq0, mem_n, W["mem_w_kv"][i], W["mem_q_norm"][i], W["mem_k_norm"][i],
                                     f"{t}_mem")
        mix = jnp.concatenate([o_tok, o_mem], axis=1).astype(BF16)
        x2 = matmul(mix, W["w_out"][i], res=x1, tm=1024, tn=512, tk=D, name=f"{t}_mix_out")
        x3, s3 = _ffn_fwd(x2, W["norm_ffn2"][i], W["ffn2_w_in"][i], W["ffn2_w_out"][i], f"{t}_ffn2")
        saved.append((s1, x1, h, proj, mq0, s_tok, s_mem, mix, s3))
        x = x3

    dx, loss = loss_head(x, target, name="loss_head")

    G = {k: [None] * depth for k in ("norm_ffn1", "norm_mix", "norm_ffn2", "mem_q_norm", "mem_k_norm", "ffn1_w_in",
                                     "ffn1_w_out", "ffn2_w_in", "ffn2_w_out", "w_out", "mem_w_kv")}
    n_fox, n_gmlp = (depth + 1) // 2, depth // 2
    for k in ("fox_w_in", "fox_b_f", "fox_q_norm", "fox_k_norm"):
        G[k] = [None] * n_fox
    for k in ("gmlp_w_in", "gmlp_v_norm", "gmlp_w_s", "gmlp_b_s"):
        G[k] = [None] * n_gmlp
    dkv_all = [None] * depth
    for i in reversed(range(depth)):
        kind, j = i % 2, i // 2
        t = f"l{i}"
        s1, x1, h, proj, mq0, s_tok, s_mem, mix, s3 = saved[i]
        dx, G["norm_ffn2"][i], G["ffn2_w_in"][i], G["ffn2_w_out"][i] = _ffn_bwd(
            dx, s3, W["norm_ffn2"][i], W["ffn2_w_in"][i], W["ffn2_w_out"][i], f"{t}_ffn2")
        dmix = matmul(dx, W["w_out"][i], tb=True, tm=1024, tn=1024, tk=D, name=f"{t}_dmix")
        G["w_out"][i] = grad_rows(mix, dx, name=f"{t}_dwmixout")
        dmq, G["mem_q_norm"][i], G["mem_k_norm"][i], G["mem_w_kv"][i], dkv_all[i] = _mem_attn_bwd(
            dmix, P, proj, mq0, s_mem, mem_n, W["mem_q_norm"][i], W["mem_k_norm"][i], f"{t}_mem")
        if kind == 0:
            rider = comm.early_grads_rider(G) if (comm is not None and i == 0) else None
            dqkv, dz, G["fox_b_f"][j], G["fox_q_norm"][j], G["fox_k_norm"][j], rode = _fox_bwd(
                dmix, proj, s_tok, W["fox_q_norm"][j], W["fox_k_norm"][j], tok, rider, f"{t}_fox")
            if rider is not None:
                comm.accept_early_grads(rode)
            dproj = jnp.concatenate([dqkv, dmq, dz], axis=1).astype(BF16)
            w_mix, wkey = W["fox_w_in"][j], "fox_w_in"
        else:
            vg, ws, bs = _gmlp_operands(W["gmlp_v_norm"][j], W["gmlp_w_s"][j], W["gmlp_b_s"][j])
            dup, dvp, dws, dbs, dvg = gmlp_bwd(proj, P, P, vg, ws, jnp.swapaxes(ws, 2, 3), bs, dmix,
                                               name=f"{t}_dgmlp")
            G["gmlp_w_s"][j] = dws.reshape(W["gmlp_w_s"][j].shape)
            G["gmlp_b_s"][j] = dbs.reshape(W["gmlp_b_s"][j].shape)
            G["gmlp_v_norm"][j] = dvg.reshape(-1)
            dproj = jnp.concatenate([dup, dvp, dmq], axis=1).astype(BF16)
            w_mix, wkey = W["gmlp_w_in"][j], "gmlp_w_in"
        G[wkey][j] = matmul(h, dproj, ta=True, tm=1024, tn=896, tk=1024, name=f"{t}_dwmixin")
        dh = matmul(dproj, w_mix, tb=True, tm=1024, tn=1024, tk=896, name=f"{t}_dhmix")
        dx, dgm = rms_bwd(x1, dh, W["norm_mix"][i], dx, name=f"{t}_dmixrms")
        G["norm_mix"][i] = dgm[0]
        dx, G["norm_ffn1"][i], G["ffn1_w_in"][i], G["ffn1_w_out"][i] = _ffn_bwd(
            dx, s1, W["norm_ffn1"][i], W["ffn1_w_in"][i], W["ffn1_w_out"][i], f"{t}_ffn1")
    w_kv_all = jnp.concatenate([W["mem_w_kv"][i] for i in range(depth)], axis=1)
    dmem_n = matmul(jnp.concatenate(dkv_all, axis=1), w_kv_all, tb=True, tm=256, tn=512, tk=1024, name="dmem_n")
    _, dmemg = rms_bwd(mem, dmem_n, W["mem_norm"], None, name="dmem_rms")
    G["mem_norm"] = [dmemg[0]]
    return loss, dx, G


def _fox_cols_to_compute(w, tok):
    H = tok // HEAD_DIM
    qkv, f, mq = w[..., :3 * tok], w[..., 3 * tok:3 * tok + H], w[..., 3 * tok + H:]
    f = jnp.pad(f, [(0, 0)] * (w.ndim - 1) + [(0, LANES - H)])
    return jnp.concatenate([qkv, mq, f], axis=-1)


def _fox_cols_from_compute(w, tok):
    H = tok // HEAD_DIM
    qkv, mq, f = w[..., :3 * tok], w[..., 3 * tok:3 * tok + MEM_WIDTH], w[..., 3 * tok + MEM_WIDTH:3 * tok + MEM_WIDTH + H]
    return jnp.concatenate([qkv, f, mq], axis=-1)


_BIG = ("ffn1_w_in", "ffn1_w_out", "ffn2_w_in", "ffn2_w_out", "w_out", "mem_w_kv", "fox_w_in", "gmlp_w_in")
_SMALL = ("norm_ffn1", "norm_mix", "norm_ffn2", "mem_norm", "mem_q_norm", "mem_k_norm", "fox_b_f", "fox_q_norm",
          "fox_k_norm", "gmlp_v_norm", "gmlp_w_s", "gmlp_b_s")
WEIGHT_ORDER = ("norm_ffn1", "ffn1_w_in", "ffn1_w_out", "norm_mix", "norm_ffn2", "ffn2_w_in", "ffn2_w_out", "w_out",
                "mem_norm", "mem_w_kv", "mem_q_norm", "mem_k_norm", "fox_w_in", "fox_b_f", "fox_q_norm", "fox_k_norm",
                "gmlp_w_in", "gmlp_v_norm", "gmlp_w_s", "gmlp_b_s")


def _small_slab(rows_list, index):
    sizes = [s.shape[0] for s in rows_list]
    n_rows = [-(-n // LANES) for n in sizes]
    small = jnp.concatenate([jnp.pad(s, (0, r * LANES - n)).reshape(r, LANES)
                             for s, n, r in zip(rows_list, sizes, n_rows)], axis=0)
    small = jnp.pad(small, ((0, -small.shape[0] % 64), (0, 0)))
    return _slot_in_empty(small, index, 8), sizes, n_rows


_FIRST_WEIGHTS = (("ffn1_w_in", 0), ("ffn1_w_out", 0), ("fox_w_in", 0))


def _weight_from_slab(name, slab, tok):
    if name in ("ffn1_w_in", "ffn2_w_in"):
        return slab
    if name == "fox_w_in":
        return _fox_cols_to_compute(_chips_to_cols(slab), tok)
    if name == "gmlp_w_in":
        return _chips_to_cols(slab)
    return slab.reshape(4 * slab.shape[1], slab.shape[2])


def _grad_to_slab(name, g, tok):
    if name == "fox_w_in":
        return _cols_to_chips(_fox_cols_from_compute(g, tok))
    if name == "gmlp_w_in":
        return _cols_to_chips(g)
    return g


class _Exchange:
    def __init__(self, shards, tok, chip, core):
        self.tok, self.core = tok, core
        self.half = core.reshape(1).astype(jnp.int32)
        self.chip_id = chip.reshape(1).astype(jnp.int32)
        items = [(k, i) for k in _BIG for i in range(shards[k].shape[0])]
        self.slabs = {it: _slot_in_empty(shards[it[0]][it[1]].astype(BF16), chip, 4) for it in items}
        self.late = [it for it in items if it not in _FIRST_WEIGHTS]
        self.reduced = {}
        self.early = None

    def first_weights(self, small_slab):
        got, small_all = gather_weights([self.slabs[it] for it in _FIRST_WEIGHTS], small_slab, name="gather_first")
        return {it: _weight_from_slab(it[0], s, self.tok) for it, s in zip(_FIRST_WEIGHTS, got)}, small_all

    def late_weights_rider(self):
        return gather_rider([self.slabs[it] for it in self.late])

    def accept_late_weights(self, W, got):
        for (k, i), s in zip(self.late, got):
            W[k][i] = _weight_from_slab(k, s, self.tok)

    def _pair_sums(self, G, items, small_slab, tag):
        parts = [_grad_to_slab(k, G[k][i], self.tok) for k, i in items]
        landed, small_all = exchange_with_sibling(parts, small_slab, name=f"grad_exchange_{tag}")
        pair = [pair_sum(p, l, self.half, name=f"grad_pair_sum_{k}{i}") for (k, i), p, l in zip(items, parts, landed)]
        return pair, small_all

    def early_grads_rider(self, G):
        items = [(k, i) for k in _BIG for i in range(len(G[k])) if G[k][i] is not None]
        pair, _ = self._pair_sums(G, items, None, "early")
        self.early = (items, pair)
        return scatter_rider(pair)

    def accept_early_grads(self, landed):
        items, pair = self.early
        self._chip_sums(items, pair, landed)

    def _chip_sums(self, items, pair, landed):
        for (k, i), q, l in zip(items, pair, landed):
            self.reduced[(k, i)] = chip_sum(q, l, self.chip_id, name=f"grad_chip_sum_{k}{i}")

    def finish_grads(self, G, small_slab):
        items = [(k, i) for k in _BIG for i in range(len(G[k])) if (k, i) not in self.reduced]
        pair, small_all = self._pair_sums(G, items, small_slab, "late")
        self._chip_sums(items, pair, scatter_to_chips(pair, name="grad_scatter_late"))
        order = sorted(self.reduced)
        other = share_with_sibling([self.reduced[it] for it in order], name="grad_share")
        full = {}
        for it, a, b in zip(order, [self.reduced[it] for it in order], other):
            full[it] = jnp.where(self.core == 0, jnp.concatenate([a, b]), jnp.concatenate([b, a]))
        names = sorted({k for k, _ in order})
        return {k: jnp.stack([full[(k, i)] for i in range(len(G[k]))]) for k in names}, small_all


def kernel(x, mem, norm_ffn1, ffn1_w_in, ffn1_w_out, norm_mix, norm_ffn2, ffn2_w_in, ffn2_w_out, w_out, mem_norm, mem_w_kv, mem_q_norm, mem_k_norm, fox_w_in, fox_b_f, fox_q_norm, fox_k_norm, gmlp_w_in, gmlp_v_norm, gmlp_w_s, gmlp_b_s, loss_target, m_norm_ffn1, m_ffn1_w_in, m_ffn1_w_out, m_norm_mix, m_norm_ffn2, m_ffn2_w_in, m_ffn2_w_out, m_w_out, m_mem_norm, m_mem_w_kv, m_mem_q_norm, m_mem_k_norm, m_fox_w_in, m_fox_b_f, m_fox_q_norm, m_fox_k_norm, m_gmlp_w_in, m_gmlp_v_norm, m_gmlp_w_s, m_gmlp_b_s, v_norm_ffn1, v_ffn1_w_in, v_ffn1_w_out, v_norm_mix, v_norm_ffn2, v_ffn2_w_in, v_ffn2_w_out, v_w_out, v_mem_norm, v_mem_w_kv, v_mem_q_norm, v_mem_k_norm, v_fox_w_in, v_fox_b_f, v_fox_q_norm, v_fox_k_norm, v_gmlp_w_in, v_gmlp_v_norm, v_gmlp_w_s, v_gmlp_b_s):
    w = dict(norm_ffn1=norm_ffn1, ffn1_w_in=ffn1_w_in, ffn1_w_out=ffn1_w_out, norm_mix=norm_mix, norm_ffn2=norm_ffn2,
             ffn2_w_in=ffn2_w_in, ffn2_w_out=ffn2_w_out, w_out=w_out, mem_norm=mem_norm, mem_w_kv=mem_w_kv,
             mem_q_norm=mem_q_norm, mem_k_norm=mem_k_norm, fox_w_in=fox_w_in, fox_b_f=fox_b_f, fox_q_norm=fox_q_norm,
             fox_k_norm=fox_k_norm, gmlp_w_in=gmlp_w_in, gmlp_v_norm=gmlp_v_norm, gmlp_w_s=gmlp_w_s, gmlp_b_s=gmlp_b_s)
    m = dict(norm_ffn1=m_norm_ffn1, ffn1_w_in=m_ffn1_w_in, ffn1_w_out=m_ffn1_w_out, norm_mix=m_norm_mix,
             norm_ffn2=m_norm_ffn2, ffn2_w_in=m_ffn2_w_in, ffn2_w_out=m_ffn2_w_out, w_out=m_w_out, mem_norm=m_mem_norm,
             mem_w_kv=m_mem_w_kv, mem_q_norm=m_mem_q_norm, mem_k_norm=m_mem_k_norm, fox_w_in=m_fox_w_in,
             fox_b_f=m_fox_b_f, fox_q_norm=m_fox_q_norm, fox_k_norm=m_fox_k_norm, gmlp_w_in=m_gmlp_w_in,
             gmlp_v_norm=m_gmlp_v_norm, gmlp_w_s=m_gmlp_w_s, gmlp_b_s=m_gmlp_b_s)
    v = dict(norm_ffn1=v_norm_ffn1, ffn1_w_in=v_ffn1_w_in, ffn1_w_out=v_ffn1_w_out, norm_mix=v_norm_mix,
             norm_ffn2=v_norm_ffn2, ffn2_w_in=v_ffn2_w_in, ffn2_w_out=v_ffn2_w_out, w_out=v_w_out, mem_norm=v_mem_norm,
             mem_w_kv=v_mem_w_kv, mem_q_norm=v_mem_q_norm, mem_k_norm=v_mem_k_norm, fox_w_in=v_fox_w_in,
             fox_b_f=v_fox_b_f, fox_q_norm=v_fox_q_norm, fox_k_norm=v_fox_k_norm, gmlp_w_in=v_gmlp_w_in,
             gmlp_v_norm=v_gmlp_v_norm, gmlp_w_s=v_gmlp_w_s, gmlp_b_s=v_gmlp_b_s)
    D = x.shape[-1]
    tok = D - MEM_WIDTH
    xi, yi, ci = _position()
    chip = 2 * xi + yi

    device = 4 * xi + 2 * yi + ci

    comm = _Exchange(w, tok, chip, ci)
    vn = w["gmlp_v_norm"]
    vn_slab, _, _ = _small_slab([vn.reshape(-1)], device)
    first, vn_all = comm.first_weights(vn_slab)
    W = {k: w[k] for k in _SMALL}
    W["gmlp_v_norm"] = _chips_to_cols(vn_all[0::2].reshape(4, -1)[:, :vn.size].reshape((4,) + vn.shape))
    for k in _BIG:
        W[k] = [first.get((k, i)) for i in range(w[k].shape[0])]

    loss, grad_x, g = local_step(x[0], mem[0], loss_target[0], W, comm)

    small_list = [jnp.stack(g[k]).reshape(-1) for k in _SMALL] + [loss.reshape(-1)]
    small, small_sizes, small_rows = _small_slab(small_list, device)
    red, small_all = comm.finish_grads(g, small)
    small_sum = ordered_sum(small_all, name="small_sum")
    off = 0
    for k, n, r in zip(_SMALL, small_sizes, small_rows):
        red[k] = small_sum[off:off + r].reshape(-1)[:n].reshape((-1,) + w[k].shape[1:] if k != "gmlp_v_norm"
                                                                else (w[k].shape[0], -1))
        off += r
    loss_total = small_sum[off, 0]
    vn_cols = w["gmlp_v_norm"].shape[-1]
    red["gmlp_v_norm"] = lax.dynamic_slice_in_dim(red["gmlp_v_norm"], chip * vn_cols, vn_cols, axis=-1)

    deltas, new_m, new_v = {}, {}, {}
    for k in WEIGHT_ORDER:
        wk = w[k] if w[k].ndim > 1 else w[k].reshape(1, -1)
        upd = adamw(wk, red[k].reshape(wk.shape), m[k].reshape(wk.shape), v[k].reshape(wk.shape), name=f"adamw_{k}")
        deltas[k], new_m[k], new_v[k] = (u.reshape(w[k].shape) for u in upd)
    return (loss_total, grad_x[None], *[red[k].reshape(w[k].shape) for k in WEIGHT_ORDER],
            *[deltas[k] for k in WEIGHT_ORDER], *[new_m[k] for k in WEIGHT_ORDER], *[new_v[k] for k in WEIGHT_ORDER])
```

```python
import functools
import math
from typing import Callable, NamedTuple

import jax
import jax.numpy as jnp
from jax import lax
from jax.experimental import pallas as pl
from jax.experimental.pallas import tpu as pltpu

F32 = jnp.float32
BF16 = jnp.bfloat16
EPS = 1e-6
HEAD_DIM = 64
MEM_WIDTH = 256
CHUNK = 128
LANES = 128
NEG = -1e30
VMEM_LIMIT_BYTES = 56 * 1024 * 1024
ATTN_Q_BLOCK = 1024
ATTN_K_BLOCK = 1024
QK_SCALE = 0.125
MESH_ID = pl.DeviceIdType.MESH

ADAM_LR = 0.001
ADAM_B1 = 0.9
ADAM_B2 = 0.999
ADAM_EPS = 1e-08
ADAM_WD = 0.01
ADAM_STEP = 10


def _tile(n, pref, align):
    t = (min(pref, n) // align) * align
    while t >= align:
        if n % t == 0:
            return t
        t -= align
    return n


def _params(sem):
    return pltpu.CompilerParams(dimension_semantics=sem, vmem_limit_bytes=VMEM_LIMIT_BYTES)


def _dot(a, b, ca, cb):
    return lax.dot_general(a, b, (((ca,), (cb,)), ((), ())), preferred_element_type=F32)


def _sigmoid(x):
    return 1.0 / (1.0 + jnp.exp(-x))


def _sigmoid_fast(x):
    return pl.reciprocal(1.0 + jnp.exp(-x), approx=True)


_GELU_C = math.sqrt(2.0 / math.pi)


def _gelu(x):
    return 0.5 * x * (1.0 + jnp.tanh(_GELU_C * (x + 0.044715 * (x * x * x))))


def _gelu_grad(x):
    t = jnp.tanh(_GELU_C * (x + 0.044715 * (x * x * x)))
    return 0.5 * (1.0 + t) + 0.5 * x * (1.0 - t * t) * (_GELU_C * (1.0 + 3.0 * 0.044715 * (x * x)))


def matmul(a, b, *, ta=False, tb=False, out_dtype=F32, scale=None, res=None,
           tm=1024, tn=512, tk=1024, name):
    if ta:
        K, M = a.shape
    else:
        M, K = a.shape
    N = b.shape[0] if tb else b.shape[1]
    tm = _tile(M, tm, LANES if ta else 16)
    tn = _tile(N, tn, LANES)
    tk = _tile(K, tk, LANES)
    nk = K // tk
    a_spec = pl.BlockSpec((tk, tm), lambda i, j, k: (k, i)) if ta else pl.BlockSpec((tm, tk), lambda i, j, k: (i, k))
    b_spec = pl.BlockSpec((tn, tk), lambda i, j, k: (j, k)) if tb else pl.BlockSpec((tk, tn), lambda i, j, k: (k, j))
    o_spec = pl.BlockSpec((tm, tn), lambda i, j, k: (i, j))
    ca, cb = (0 if ta else 1), (1 if tb else 0)
    has_res = res is not None

    def body(*refs):
        a_ref, b_ref = refs[0], refs[1]
        res_ref = refs[2] if has_res else None
        o_ref = refs[3] if has_res else refs[2]
        acc_ref = refs[-1]
        k = pl.program_id(2)
        prod = _dot(a_ref[...].astype(BF16), b_ref[...].astype(BF16), ca, cb)

        def finish(acc):
            if scale is not None:
                acc = acc * scale
            if has_res:
                acc = res_ref[...] + acc
            o_ref[...] = acc.astype(out_dtype)

        if nk == 1:
            finish(prod)
        else:
            @pl.when(k == 0)
            def _():
                acc_ref[...] = prod

            @pl.when(k > 0)
            def _():
                acc_ref[...] += prod

            @pl.when(k == nk - 1)
            def _():
                finish(acc_ref[...])

    in_specs = [a_spec, b_spec] + ([o_spec] if has_res else [])
    args = (a, b) + ((res,) if has_res else ())
    return pl.pallas_call(
        body, grid=(M // tm, N // tn, nk), in_specs=in_specs, out_specs=o_spec,
        out_shape=jax.ShapeDtypeStruct((M, N), out_dtype),
        scratch_shapes=[pltpu.VMEM((tm, tn) if nk > 1 else (8, LANES), F32)],
        compiler_params=_params(("parallel", "parallel", "arbitrary")), name=name)(*args)


def swiglu_fwd(h, w_slab, *, name):
    S, D = h.shape
    Fc = w_slab.shape[-1]
    tm = _tile(S, 512, 16)

    def body(h_ref, wa_ref, wb_ref, a_ref, b_ref, act_ref):
        hv = h_ref[...]
        a = _dot(hv, wa_ref[...], 1, 0)
        b = _dot(hv, wb_ref[...], 1, 0)
        a_ref[...] = a.astype(BF16)
        b_ref[...] = b.astype(BF16)
        act_ref[...] = (a * _sigmoid_fast(a) * b).astype(BF16)

    out = pl.BlockSpec((tm, Fc), lambda j, i: (i, j))
    return pl.pallas_call(
        body, grid=(2, S // tm),
        in_specs=[pl.BlockSpec((tm, D), lambda j, i: (i, 0)),
                  pl.BlockSpec((None, D, Fc), lambda j, i: (j, 0, 0)),
                  pl.BlockSpec((None, D, Fc), lambda j, i: (j + 2, 0, 0))],
        out_specs=[out, out, out],
        out_shape=[jax.ShapeDtypeStruct((S, 2 * Fc), BF16)] * 3,
        compiler_params=_params(("parallel", "parallel")), name=name)(h, w_slab, w_slab)


def swiglu_bwd(dy, w_out, a, b, *, name):
    S, D = dy.shape
    F = w_out.shape[0]
    fc = F // 2
    tm = _tile(S, 512, 16)

    def body(dy_ref, w_ref, a_ref, b_ref, da_ref, db_ref):
        dact = 0.5 * _dot(dy_ref[...].astype(BF16), w_ref[...], 1, 1)
        av = a_ref[...].astype(F32)
        sg = _sigmoid_fast(av)
        da_ref[...] = (dact * b_ref[...].astype(F32) * (sg * (1.0 + av * (1.0 - sg)))).astype(BF16)
        db_ref[...] = (dact * (av * sg)).astype(BF16)

    blk = pl.BlockSpec((tm, fc), lambda j, i: (i, j))
    return pl.pallas_call(
        body, grid=(2, S // tm),
        in_specs=[pl.BlockSpec((tm, D), lambda j, i: (i, 0)), pl.BlockSpec((fc, D), lambda j, i: (j, 0)), blk, blk],
        out_specs=[blk, blk],
        out_shape=[jax.ShapeDtypeStruct((S, F), BF16), jax.ShapeDtypeStruct((S, F), BF16)],
        compiler_params=_params(("parallel", "parallel")), name=name)(dy, w_out, a, b)


def ffn_dh(da, db, w_slab, x, g, dy, *, name):
    S, F = da.shape
    D, Fc = w_slab.shape[-2:]
    tm = _tile(S, 1024, 16)
    sub = _tile(tm, 256, 8)

    def body(da_ref, db_ref, w_ref, x_ref, g_ref, dy_ref, dx_ref, dg_ref, acc_ref):
        i, k = pl.program_id(0), pl.program_id(1)

        @pl.when(k == 0)
        def _():
            acc_ref[...] = jnp.zeros_like(acc_ref)

        @pl.when(k < 2)
        def _():
            acc_ref[...] += _dot(da_ref[...], w_ref[...], 1, 1)

        @pl.when(k >= 2)
        def _():
            acc_ref[...] += _dot(db_ref[...], w_ref[...], 1, 1)

        @pl.when(k == 3)
        def _():
            part = None
            for c in range(tm // sub):
                rows = pl.ds(c * sub, sub)
                xv, dh = x_ref[rows, :], acc_ref[rows, :]
                r = lax.rsqrt(jnp.mean(xv * xv, axis=-1, keepdims=True) + EPS)
                u = dh * g_ref[...]
                dx_ref[rows, :] = dy_ref[rows, :] + (r * u - xv * (r * r * r) * jnp.mean(xv * u, axis=-1, keepdims=True))
                p = jnp.sum(dh * xv * r, axis=0, keepdims=True)
                part = p if part is None else part + p

            @pl.when(i == 0)
            def _():
                dg_ref[...] = part

            @pl.when(i > 0)
            def _():
                dg_ref[...] += part

    row = pl.BlockSpec((tm, D), lambda i, k: (i, 0))
    vec = pl.BlockSpec((1, D), lambda i, k: (0, 0))
    return pl.pallas_call(
        body, grid=(S // tm, 4),
        in_specs=[pl.BlockSpec((tm, Fc), lambda i, k: (i, jnp.minimum(k, 1))),
                  pl.BlockSpec((tm, Fc), lambda i, k: (i, jnp.maximum(k - 2, 0))),
                  pl.BlockSpec((None, D, Fc), lambda i, k: (k, 0, 0)), row, vec, row],
        out_specs=[row, vec],
        out_shape=[jax.ShapeDtypeStruct((S, D), F32), jax.ShapeDtypeStruct((1, D), F32)],
        scratch_shapes=[pltpu.VMEM((tm, D), F32)],
        compiler_params=_params(("arbitrary", "arbitrary")), name=name)(da, db, w_slab, x, g.reshape(1, D), dy)


def grad_cols(h, da, db, *, name):
    S, D = h.shape
    Fc = da.shape[1] // 2
    tk = _tile(S, 1024, 16)
    nk = S // tk

    def body(h_ref, da_ref, db_ref, o_ref, acc_ref):
        ch, k = pl.program_id(0), pl.program_id(1)

        @pl.when(k == 0)
        def _():
            acc_ref[...] = jnp.zeros_like(acc_ref)

        @pl.when(ch < 2)
        def _():
            acc_ref[...] += _dot(h_ref[...], da_ref[...], 0, 0)

        @pl.when(ch >= 2)
        def _():
            acc_ref[...] += _dot(h_ref[...], db_ref[...], 0, 0)

        @pl.when(k == nk - 1)
        def _():
            o_ref[...] = acc_ref[...]

    return pl.pallas_call(
        body, grid=(4, nk),
        in_specs=[pl.BlockSpec((tk, D), lambda ch, k: (k, 0)),
                  pl.BlockSpec((tk, Fc), lambda ch, k: (jnp.where(ch < 2, k, 0), jnp.minimum(ch, 1))),
                  pl.BlockSpec((tk, Fc), lambda ch, k: (jnp.where(ch >= 2, k, 0), jnp.maximum(ch - 2, 0)))],
        out_specs=pl.BlockSpec((None, D, Fc), lambda ch, k: (ch, 0, 0)),
        out_shape=jax.ShapeDtypeStruct((4, D, Fc), F32),
        scratch_shapes=[pltpu.VMEM((D, Fc), F32)],
        compiler_params=_params(("parallel", "arbitrary")), name=name)(h, da, db)


def grad_rows(a, b, *, scale=None, name):
    S, M = a.shape
    N = b.shape[1]
    R = M // 4
    tn = _tile(N, 512, LANES)
    tk = _tile(S, 1024, 16)
    nk = S // tk

    def body(a_ref, b_ref, o_ref, acc_ref):
        k = pl.program_id(1)

        @pl.when(k == 0)
        def _():
            acc_ref[...] = jnp.zeros_like(acc_ref)

        acc_ref[...] += _dot(a_ref[...].astype(BF16), b_ref[...].astype(BF16), 0, 0)

        @pl.when(k == nk - 1)
        def _():
            for d in range(4):
                part = acc_ref[d * R:(d + 1) * R, :]
                o_ref[d] = part if scale is None else part * scale

    return pl.pallas_call(
        body, grid=(N // tn, nk),
        in_specs=[pl.BlockSpec((tk, M), lambda j, k: (k, 0)), pl.BlockSpec((tk, tn), lambda j, k: (k, j))],
        out_specs=pl.BlockSpec((4, R, tn), lambda j, k: (0, 0, j)),
        out_shape=jax.ShapeDtypeStruct((4, R, N), F32),
        scratch_shapes=[pltpu.VMEM((M, tn), F32)],
        compiler_params=_params(("parallel", "arbitrary")), name=name)(a, b)


def rms_fwd(x, g, *, name):
    S, D = x.shape
    ts = _tile(S, 1024, 16)

    def body(x_ref, g_ref, h_ref):
        xv = x_ref[...]
        r = lax.rsqrt(jnp.mean(xv * xv, axis=-1, keepdims=True) + EPS)
        h_ref[...] = (xv * r * g_ref[...]).astype(BF16)

    return pl.pallas_call(
        body, grid=(S // ts,),
        in_specs=[pl.BlockSpec((ts, D), lambda i: (i, 0)), pl.BlockSpec((1, D), lambda i: (0, 0))],
        out_specs=pl.BlockSpec((ts, D), lambda i: (i, 0)),
        out_shape=jax.ShapeDtypeStruct((S, D), BF16),
        compiler_params=_params(("parallel",)), name=name)(x, g.reshape(1, D))


def rms_bwd(x, dh, g, res, *, name):
    S, D = x.shape
    ts = _tile(S, 512, 16)
    has_res = res is not None

    def body(*refs):
        x_ref, dh_ref, g_ref = refs[:3]
        res_ref = refs[3] if has_res else None
        dx_ref, dg_ref = refs[-2:]
        i = pl.program_id(0)
        xv, dhv = x_ref[...], dh_ref[...].astype(F32)
        r = lax.rsqrt(jnp.mean(xv * xv, axis=-1, keepdims=True) + EPS)
        u = dhv * g_ref[...]
        dx = r * u - xv * (r * r * r) * jnp.mean(xv * u, axis=-1, keepdims=True)
        if has_res:
            dx = res_ref[...] + dx
        dx_ref[...] = dx
        part = jnp.sum(dhv * xv * r, axis=0, keepdims=True)

        @pl.when(i == 0)
        def _():
            dg_ref[...] = part

        @pl.when(i > 0)
        def _():
            dg_ref[...] += part

    row = pl.BlockSpec((ts, D), lambda i: (i, 0))
    vec = pl.BlockSpec((1, D), lambda i: (0, 0))
    args = (x, dh, g.reshape(1, D)) + ((res,) if has_res else ())
    return pl.pallas_call(
        body, grid=(S // ts,), in_specs=[row, row, vec] + ([row] if has_res else []),
        out_specs=[row, vec],
        out_shape=[jax.ShapeDtypeStruct((S, D), F32), jax.ShapeDtypeStruct((1, D), F32)],
        compiler_params=_params(("arbitrary",)), name=name)(*args)


def _low_half(shape):
    return lax.broadcasted_iota(jnp.int32, shape, len(shape) - 1) < HEAD_DIM


def _half_sums(x, low):
    sa = jnp.sum(jnp.where(low, x, 0.0), axis=1, keepdims=True)
    sb = jnp.sum(jnp.where(low, 0.0, x), axis=1, keepdims=True)
    return jnp.where(low, sa, sb)


def pairnorm_fwd(x, col0, n_pairs, g, *, scale=None, name):
    S = x.shape[0]
    ts = _tile(S, 1024, 16)

    def body(x_ref, g_ref, o_ref):
        xv = x_ref[...]
        r = lax.rsqrt(_half_sums(xv * xv, _low_half(xv.shape)) * (1.0 / HEAD_DIM) + EPS)
        y = xv * r * g_ref[...]
        o_ref[...] = (y if scale is None else y * scale).astype(BF16)

    return pl.pallas_call(
        body, grid=(S // ts, n_pairs),
        in_specs=[pl.BlockSpec((ts, LANES), lambda i, j: (i, col0 + j)), pl.BlockSpec((1, LANES), lambda i, j: (0, 0))],
        out_specs=pl.BlockSpec((ts, LANES), lambda i, j: (i, j)),
        out_shape=jax.ShapeDtypeStruct((S, n_pairs * LANES), BF16),
        compiler_params=_params(("parallel", "parallel")), name=name)(x, jnp.tile(g.reshape(1, HEAD_DIM), (1, 2)))


def pairnorm_bwd(x, col0, n_pairs, dy, g, *, name):
    S = x.shape[0]
    ts = _tile(S, 1024, 16)

    def body(x_ref, dy_ref, g_ref, dx_ref, dg_ref):
        first = jnp.logical_and(pl.program_id(0) == 0, pl.program_id(1) == 0)
        xv, dyv = x_ref[...], dy_ref[...]
        low = _low_half(xv.shape)
        r = lax.rsqrt(_half_sums(xv * xv, low) * (1.0 / HEAD_DIM) + EPS)
        u = dyv * g_ref[...]
        dx_ref[...] = r * u - xv * (r * r * r) * (_half_sums(xv * u, low) * (1.0 / HEAD_DIM))
        part = jnp.sum(dyv * xv * r, axis=0, keepdims=True)

        @pl.when(first)
        def _():
            dg_ref[...] = part

        @pl.when(jnp.logical_not(first))
        def _():
            dg_ref[...] += part

    vec = pl.BlockSpec((1, LANES), lambda i, j: (0, 0))
    blk = pl.BlockSpec((ts, LANES), lambda i, j: (i, j))
    return pl.pallas_call(
        body, grid=(S // ts, n_pairs),
        in_specs=[pl.BlockSpec((ts, LANES), lambda i, j: (i, col0 + j)), blk, vec], out_specs=[blk, vec],
        out_shape=[jax.ShapeDtypeStruct((S, n_pairs * LANES), F32), jax.ShapeDtypeStruct((1, LANES), F32)],
        compiler_params=_params(("arbitrary", "arbitrary")), name=name)(x, dy, jnp.tile(g.reshape(1, HEAD_DIM), (1, 2)))


def _split3(x):
    x1 = x.astype(BF16)
    r1 = x - x1.astype(F32)
    x2 = r1.astype(BF16)
    x3 = (r1 - x2.astype(F32)).astype(BF16)
    return x1, x2, x3


def _tri_ones(n, lower):
    r = lax.broadcasted_iota(jnp.int32, (n, n), 0)
    c = lax.broadcasted_iota(jnp.int32, (n, n), 1)
    return jnp.where((c <= r) if lower else (c >= r), 1.0, 0.0).astype(BF16)


def fgate_fwd(z, col0, bias, n_heads, *, name):
    S, L = z.shape[0], LANES
    tb = _tile(S, 256, 16)

    def body(z_ref, b_ref, c_ref, col_ref, carry):
        i = pl.program_id(0)

        @pl.when(i == 0)
        def _():
            carry[...] = jnp.zeros_like(carry)

        zz = z_ref[...] + b_ref[...]
        lf = jnp.minimum(zz, 0.0) - jnp.log(1.0 + jnp.exp(-jnp.abs(zz)))
        tri = _tri_ones(tb, True)
        x1, x2, x3 = _split3(lf)
        c = (_dot(tri, x1, 1, 0) + _dot(tri, x2, 1, 0)) + _dot(tri, x3, 1, 0) + carry[...]
        c_ref[...] = c
        lane = lax.broadcasted_iota(jnp.int32, c.shape, 1)
        for h in range(n_heads):
            col_ref[h // 2, h % 2] = jnp.sum(jnp.where(lane == h, c, 0.0), axis=1, keepdims=True)
        carry[...] += jnp.sum(lf, axis=0, keepdims=True)

    return pl.pallas_call(
        body, grid=(S // tb,),
        in_specs=[pl.BlockSpec((tb, L), lambda i: (i, col0)), pl.BlockSpec((1, L), lambda i: (0, 0))],
        out_specs=[pl.BlockSpec((tb, L), lambda i: (i, 0)),
                   pl.BlockSpec((n_heads // 2, 2, tb, 1), lambda i: (0, 0, i, 0))],
        out_shape=[jax.ShapeDtypeStruct((S, L), F32), jax.ShapeDtypeStruct((n_heads // 2, 2, S, 1), F32)],
        scratch_shapes=[pltpu.VMEM((1, L), F32)],
        compiler_params=_params(("arbitrary",)), name=name)(z, bias)


def fgate_bwd(z, col0, bias, drs, dcs, *, name):
    S, L = z.shape[0], LANES
    tb = _tile(S, 256, 16)
    nb = S // tb

    def body(z_ref, b_ref, drs_ref, dcs_ref, dz_ref, db_ref, carry):
        i = pl.program_id(0)

        @pl.when(i == 0)
        def _():
            carry[...] = jnp.zeros_like(carry)

        tri = _tri_ones(tb, False)
        dc = drs_ref[...] - dcs_ref[...]
        x1, x2, x3 = _split3(dc)
        dlf = (_dot(tri, x1, 1, 0) + _dot(tri, x2, 1, 0)) + _dot(tri, x3, 1, 0) + carry[...]
        carry[...] += jnp.sum(dc, axis=0, keepdims=True)
        dz = dlf * _sigmoid(-(z_ref[...] + b_ref[...]))
        dz_ref[...] = dz
        part = jnp.sum(dz, axis=0, keepdims=True)

        @pl.when(i == 0)
        def _():
            db_ref[...] = part

        @pl.when(i > 0)
        def _():
            db_ref[...] += part

    rev = pl.BlockSpec((tb, L), lambda i: (nb - 1 - i, 0))
    vec = pl.BlockSpec((1, L), lambda i: (0, 0))
    return pl.pallas_call(
        body, grid=(nb,), in_specs=[pl.BlockSpec((tb, L), lambda i: (nb - 1 - i, col0)), vec, rev, rev],
        out_specs=[rev, vec],
        out_shape=[jax.ShapeDtypeStruct((S, L), F32), jax.ShapeDtypeStruct((1, L), F32)],
        scratch_shapes=[pltpu.VMEM((1, L), F32)],
        compiler_params=_params(("arbitrary",)), name=name)(z, bias, drs, dcs)


def _one_head(x, low, a):
    return jnp.where(low if a == 0 else jnp.logical_not(low), x, jnp.zeros_like(x))


class Rider(NamedTuple):
    inputs: tuple
    out_shapes: tuple
    aliases: dict
    sems: tuple
    plan: Callable


def _with_rider(rider, n_in, n_out, n_scratch):
    if rider is None:
        return [], [], [], [], {}, lambda refs: (refs[:n_in], refs[n_in:n_in + n_out], refs[n_in + n_out:], None)
    e_in, e_out = len(rider.inputs), len(rider.out_shapes)

    def split(refs):
        ins, r_in = refs[:n_in], refs[n_in:n_in + e_in]
        o0 = n_in + e_in
        outs, r_out = refs[o0:o0 + n_out], refs[o0 + n_out:o0 + n_out + e_out]
        s0 = o0 + n_out + e_out
        return ins, outs, refs[s0:s0 + n_scratch], rider.plan(r_in, r_out, refs[s0 + n_scratch:])

    aliases = {n_in + a: n_out + b for a, b in rider.aliases.items()}
    return list(rider.inputs), [_ANY] * e_in, list(rider.out_shapes), [_ANY] * e_out, aliases, split


def attn_fwd(q, q0, k, k0, v, v0, n_pairs, decay, *, causal, rider=None, name):
    Sq, Sk = q.shape[0], k.shape[0]
    tq = _tile(Sq, ATTN_Q_BLOCK, LANES)
    tk = _tile(Sk, ATTN_K_BLOCK, LANES)
    nq, nk = Sq // tq, Sk // tk
    bias = decay is not None
    r_args, r_in_specs, r_shapes, r_out_specs, aliases, split = _with_rider(rider, 5 if bias else 3, 2, 2)

    def row_sum_lanes(acc, low, a):
        other = jnp.logical_not(low) if a == 0 else low
        return jnp.max(jnp.where(other, acc, 0.0), axis=1, keepdims=True)

    def body(*refs):
        ins, (o_ref, lse_ref), (m_sc, acc_sc), ride = split(refs)
        q_ref, k_ref, v_ref = ins[:3]
        cq_ref, ck_ref = (ins[3], ins[4]) if bias else (None, None)
        pr, i, j = pl.program_id(0), pl.program_id(1), pl.program_id(2)
        if ride is not None:
            pl.when(jnp.logical_and(pr == 0, jnp.logical_and(i == 0, j == 0)))(ride[0])

        @pl.when(j == 0)
        def _():
            m_sc[...] = jnp.full_like(m_sc, NEG)
            acc_sc[...] = jnp.zeros_like(acc_sc)

        def compute(masked):
            qv, kv, vv = q_ref[...], k_ref[...], v_ref[...].astype(BF16)
            low, low_k = _low_half(qv.shape), _low_half(kv.shape)
            for a in range(2):
                s = _dot(_one_head(qv, low, a), kv, 1, 1)
                if bias:
                    s = s + (cq_ref[a] - ck_ref[a])
                if masked:
                    row = i * tq + lax.broadcasted_iota(jnp.int32, (tq, tk), 0)
                    col = j * tk + lax.broadcasted_iota(jnp.int32, (tq, tk), 1)
                    s = jnp.where(col <= row, s, NEG)
                m_prev = m_sc[a]
                m_new = jnp.maximum(m_prev, jnp.max(s, axis=1, keepdims=True))
                alpha = jnp.exp(m_prev - m_new)
                p = jnp.exp(s - m_new).astype(BF16)
                va = jnp.where(low_k if a == 0 else jnp.logical_not(low_k), vv, jnp.ones_like(vv))
                acc_sc[a] = alpha * acc_sc[a] + _dot(p, va, 1, 0)
                m_sc[a] = m_new

        if causal:
            live = j * tk <= i * tq + (tq - 1)
            crosses = j * tk + (tk - 1) > i * tq
            pl.when(jnp.logical_and(live, crosses))(functools.partial(compute, True))
            pl.when(jnp.logical_and(live, jnp.logical_not(crosses)))(functools.partial(compute, False))
        else:
            compute(False)

        @pl.when(j == nk - 1)
        def _():
            low = _low_half((tq, LANES))
            l = [row_sum_lanes(acc_sc[a], low, a) for a in range(2)]
            o_ref[...] = jnp.where(low, acc_sc[0] / l[0], acc_sc[1] / l[1])
            for a in range(2):
                lse_ref[a] = m_sc[a] + jnp.log(l[a])

        if ride is not None:
            pl.when(jnp.logical_and(pr == n_pairs - 1, jnp.logical_and(i == nq - 1, j == nk - 1)))(ride[1])

    def kv_blk(i, j):
        return jnp.minimum(j, (i * tq + tq - 1) // tk) if causal else j

    in_specs = [pl.BlockSpec((tq, LANES), lambda p, i, j: (i, q0 + p)),
                pl.BlockSpec((tk, LANES), lambda p, i, j: (kv_blk(i, j), k0 + p)),
                pl.BlockSpec((tk, LANES), lambda p, i, j: (kv_blk(i, j), v0 + p))]
    args = [q, k, v]
    if bias:
        in_specs += [pl.BlockSpec((None, 2, tq, 1), lambda p, i, j: (p, 0, i, 0)),
                     pl.BlockSpec((None, 2, 1, tk), lambda p, i, j: (p, 0, 0, kv_blk(i, j)))]
        args += list(decay)
    out = pl.pallas_call(
        body, grid=(n_pairs, nq, nk), in_specs=in_specs + r_in_specs,
        out_specs=[pl.BlockSpec((tq, LANES), lambda p, i, j: (i, p)),
                   pl.BlockSpec((None, 2, tq, 1), lambda p, i, j: (p, 0, i, 0))] + r_out_specs,
        out_shape=[jax.ShapeDtypeStruct((Sq, n_pairs * LANES), F32),
                   jax.ShapeDtypeStruct((n_pairs, 2, Sq, 1), F32)] + r_shapes,
        scratch_shapes=[pltpu.VMEM((2, tq, 1), F32), pltpu.VMEM((2, tq, LANES), F32)]
        + (list(rider.sems) if rider else []),
        input_output_aliases=aliases,
        compiler_params=_params(("arbitrary",) * 3 if rider else ("parallel", "parallel", "arbitrary")),
        name=name)(*args, *r_args)
    return out[0], out[1], out[2:]


def attn_delta(o, do, do0, n_pairs, *, name):
    S = o.shape[0]
    ts = _tile(S, 1024, 16)

    def body(o_ref, do_ref, out_ref):
        prod = o_ref[...] * do_ref[...]
        low = _low_half(prod.shape)
        out_ref[0] = jnp.sum(jnp.where(low, prod, 0.0), axis=1, keepdims=True)
        out_ref[1] = jnp.sum(jnp.where(low, 0.0, prod), axis=1, keepdims=True)

    return pl.pallas_call(
        body, grid=(n_pairs, S // ts),
        in_specs=[pl.BlockSpec((ts, LANES), lambda p, i: (i, p)), pl.BlockSpec((ts, LANES), lambda p, i: (i, do0 + p))],
        out_specs=pl.BlockSpec((None, 2, ts, 1), lambda p, i: (p, 0, i, 0)),
        out_shape=jax.ShapeDtypeStruct((n_pairs, 2, S, 1), F32),
        compiler_params=_params(("parallel", "parallel")), name=name)(o, do)


def attn_bwd(q, q0, k, k0, v, v0, do, do0, n_pairs, lse, delta, decay, *, causal, rider=None, name):
    Sq, Sk = q.shape[0], k.shape[0]
    tq = _tile(Sq, ATTN_Q_BLOCK, LANES)
    tk = _tile(Sk, ATTN_K_BLOCK, LANES)
    nq, nk = Sq // tq, Sk // tk
    bias = decay is not None

    r_args, r_in_specs, r_shapes, r_out_specs, aliases, split = _with_rider(
        rider, 8 if bias else 6, 5 if bias else 3, 0)

    def body(*refs):
        ins, outs, _, ride = split(refs)
        q_ref, k_ref, v_ref, do_ref, lse_ref, dl_ref = ins[:6]
        cq_ref, ck_ref = (ins[6], ins[7]) if bias else (None, None)
        dq_ref, dk_ref, dv_ref = outs[:3]
        dcs_ref, drs_ref = (outs[3], outs[4]) if bias else (None, None)
        pr, j, i = pl.program_id(0), pl.program_id(1), pl.program_id(2)
        if ride is not None:
            pl.when(jnp.logical_and(pr == 0, jnp.logical_and(i == 0, j == 0)))(ride[0])

        @pl.when(i == 0)
        def _():
            dk_ref[...] = jnp.zeros_like(dk_ref)
            dv_ref[...] = jnp.zeros_like(dv_ref)
            if bias:
                dcs_ref[...] = jnp.zeros_like(dcs_ref)

        rows = pl.ds(pl.multiple_of(i * tq, tq), tq)

        def compute(masked):
            qv, kv, vv, dov = q_ref[...], k_ref[...], v_ref[...].astype(BF16), do_ref[...].astype(BF16)
            low, low_k = _low_half(qv.shape), _low_half(kv.shape)
            dq_part, row_parts = None, []
            for a in range(2):
                qa, ka, doa = _one_head(qv, low, a), _one_head(kv, low_k, a), _one_head(dov, low, a)
                s = _dot(qa, kv, 1, 1)
                if bias:
                    s = s + (cq_ref[a] - ck_ref[a])
                p = jnp.exp(s - lse_ref[a])
                if masked:
                    row = i * tq + lax.broadcasted_iota(jnp.int32, (tq, tk), 0)
                    col = j * tk + lax.broadcasted_iota(jnp.int32, (tq, tk), 1)
                    p = jnp.where(col <= row, p, 0.0)
                dv_ref[...] += _dot(p.astype(BF16), doa, 0, 0)
                dp = _dot(doa, vv, 1, 1)
                ds = p * (dp - dl_ref[a])
                dsb = ds.astype(BF16)
                dk_ref[...] += _dot(dsb, qa, 0, 0)
                if bias:
                    dcs_ref[a] += jnp.sum(ds, axis=0, keepdims=True)
                    row_parts.append(jnp.sum(ds, axis=1, keepdims=True))
                part = _dot(dsb, ka, 1, 0) * QK_SCALE
                dq_part = part if dq_part is None else dq_part + part

            @pl.when(j == 0)
            def _():
                dq_ref[rows, :] = dq_part
                for a, rp in enumerate(row_parts):
                    drs_ref[a, rows, :] = rp

            @pl.when(j > 0)
            def _():
                dq_ref[rows, :] += dq_part
                for a, rp in enumerate(row_parts):
                    drs_ref[a, rows, :] += rp

        if causal:
            live = j * tk <= i * tq + (tq - 1)
            crosses = j * tk + (tk - 1) > i * tq
            pl.when(jnp.logical_and(live, crosses))(functools.partial(compute, True))
            pl.when(jnp.logical_and(live, jnp.logical_not(crosses)))(functools.partial(compute, False))
        else:
            compute(False)

        if ride is not None:
            pl.when(jnp.logical_and(pr == n_pairs - 1, jnp.logical_and(i == nq - 1, j == nk - 1)))(ride[1])

    def q_blk(j, i):
        return jnp.maximum(i, (j * tk) // tq) if causal else i

    col1 = pl.BlockSpec((None, 2, tq, 1), lambda p, j, i: (p, 0, q_blk(j, i), 0))
    in_specs = [pl.BlockSpec((tq, LANES), lambda p, j, i: (q_blk(j, i), q0 + p)),
                pl.BlockSpec((tk, LANES), lambda p, j, i: (j, k0 + p)),
                pl.BlockSpec((tk, LANES), lambda p, j, i: (j, v0 + p)),
                pl.BlockSpec((tq, LANES), lambda p, j, i: (q_blk(j, i), do0 + p)), col1, col1]
    args = [q, k, v, do, lse, delta]
    kout = pl.BlockSpec((tk, LANES), lambda p, j, i: (j, p))
    out_specs = [pl.BlockSpec((Sq, LANES), lambda p, j, i: (0, p)), kout, kout]
    out_shape = [jax.ShapeDtypeStruct((Sq, n_pairs * LANES), F32), jax.ShapeDtypeStruct((Sk, n_pairs * LANES), F32),
                 jax.ShapeDtypeStruct((Sk, n_pairs * LANES), F32)]
    if bias:
        in_specs += [col1, pl.BlockSpec((None, 2, 1, tk), lambda p, j, i: (p, 0, 0, j))]
        args += list(decay)
        out_specs += [pl.BlockSpec((None, 2, 1, tk), lambda p, j, i: (p, 0, 0, j)),
                      pl.BlockSpec((None, 2, Sq, 1), lambda p, j, i: (p, 0, 0, 0))]
        out_shape += [jax.ShapeDtypeStruct((n_pairs, 2, 1, Sk), F32), jax.ShapeDtypeStruct((n_pairs, 2, Sq, 1), F32)]
    n_own = len(out_shape)
    out = pl.pallas_call(
        body, grid=(n_pairs, nk, nq), in_specs=in_specs + r_in_specs, out_specs=out_specs + r_out_specs,
        out_shape=out_shape + r_shapes, scratch_shapes=list(rider.sems) if rider else [],
        input_output_aliases=aliases,
        compiler_params=_params(("arbitrary",) * 3 if rider else ("parallel", "arbitrary", "arbitrary")),
        name=name)(*args, *r_args)
    return tuple(out[:n_own]), out[n_own:]


def _tril_mask(n):
    r = lax.broadcasted_iota(jnp.int32, (n, n), 0)
    c = lax.broadcasted_iota(jnp.int32, (n, n), 1)
    return c <= r


def _gmlp_operands(v_gain, w_s, b_s):
    G = w_s.shape[0]
    return (v_gain.reshape(G // 2, 1, LANES), w_s.reshape(G // 2, 2, CHUNK, CHUNK), b_s.reshape(G // 2, 2, CHUNK, 1))


def _gmlp_gate(wt, vh, b_ref, low):
    gate = _dot(wt[0], _one_head(vh, low, 0), 1, 0) + _dot(wt[1], _one_head(vh, low, 1), 1, 0)
    return gate + jnp.where(low, b_ref[0], b_ref[1])


def gmlp_fwd(proj, v0, n_pairs, vg, w, b, *, name):
    S = proj.shape[0]
    ts = _tile(S, 1024, CHUNK)

    def body(up_ref, vp_ref, vg_ref, w_ref, b_ref, o_ref):
        mask = _tril_mask(CHUNK)
        wt = [jnp.where(mask, w_ref[a], 0.0).astype(BF16) for a in range(2)]
        low = _low_half((CHUNK, LANES))
        for c in range(ts // CHUNK):
            sl = pl.ds(c * CHUNK, CHUNK)
            vz = _gelu(vp_ref[sl, :])
            r = lax.rsqrt(_half_sums(vz * vz, low) * (1.0 / HEAD_DIM) + EPS)
            vh = (vz * r * vg_ref[...]).astype(BF16)
            o_ref[sl, :] = _gelu(up_ref[sl, :]) * _gmlp_gate(wt, vh, b_ref, low)

    return pl.pallas_call(
        body, grid=(n_pairs, S // ts),
        in_specs=[pl.BlockSpec((ts, LANES), lambda p, i: (i, p)), pl.BlockSpec((ts, LANES), lambda p, i: (i, v0 + p)),
                  pl.BlockSpec((None, 1, LANES), lambda p, i: (p, 0, 0)),
                  pl.BlockSpec((None, 2, CHUNK, CHUNK), lambda p, i: (p, 0, 0, 0)),
                  pl.BlockSpec((None, 2, CHUNK, 1), lambda p, i: (p, 0, 0, 0))],
        out_specs=pl.BlockSpec((ts, LANES), lambda p, i: (i, p)),
        out_shape=jax.ShapeDtypeStruct((S, n_pairs * LANES), F32),
        compiler_params=_params(("parallel", "parallel")), name=name)(proj, proj, vg, w, b)


def gmlp_bwd(proj, v0, n_pairs, vg, w, wT, b, do, *, name):
    S = proj.shape[0]
    ts = _tile(S, 1024, CHUNK)

    def body(up_ref, vp_ref, vg_ref, w_ref, wT_ref, b_ref, do_ref, dup_ref, dvp_ref, dw_ref, db_ref, dvg_ref):
        i = pl.program_id(1)

        @pl.when(i == 0)
        def _():
            dw_ref[...] = jnp.zeros_like(dw_ref)
            db_ref[...] = jnp.zeros_like(db_ref)
            dvg_ref[...] = jnp.zeros_like(dvg_ref)

        mask = _tril_mask(CHUNK)
        wt = [jnp.where(mask, w_ref[a], 0.0).astype(BF16) for a in range(2)]
        wtT = [jnp.where(mask.T, wT_ref[a], 0.0).astype(BF16) for a in range(2)]
        low = _low_half((CHUNK, LANES))
        vgain = vg_ref[...]
        for c in range(ts // CHUNK):
            sl = pl.ds(c * CHUNK, CHUNK)
            u_pre, v_pre, dout = up_ref[sl, :], vp_ref[sl, :], do_ref[sl, :]
            vz = _gelu(v_pre)
            r = lax.rsqrt(_half_sums(vz * vz, low) * (1.0 / HEAD_DIM) + EPS)
            vh = (vz * r * vgain).astype(BF16)
            gate = _gmlp_gate(wt, vh, b_ref, low)
            dgate = dout * _gelu(u_pre)
            dup_ref[sl, :] = dout * gate * _gelu_grad(u_pre)
            dvh = None
            for a in range(2):
                dga = _one_head(dgate, low, a)
                dgb = dga.astype(BF16)
                dw_ref[a] += jnp.where(mask, _dot(dgb, vh, 1, 1), 0.0)
                db_ref[a] += jnp.sum(dga, axis=1, keepdims=True)
                part = _dot(wtT[a], dgb, 1, 0)
                dvh = part if dvh is None else dvh + part
            dvg_ref[...] += jnp.sum(dvh * vz * r, axis=0, keepdims=True)
            t = dvh * vgain
            dvz = r * t - vz * (r * r * r) * (_half_sums(vz * t, low) * (1.0 / HEAD_DIM))
            dvp_ref[sl, :] = dvz * _gelu_grad(v_pre)

    ublk = pl.BlockSpec((ts, LANES), lambda p, i: (i, p))
    wblk = pl.BlockSpec((None, 2, CHUNK, CHUNK), lambda p, i: (p, 0, 0, 0))
    bblk = pl.BlockSpec((None, 2, CHUNK, 1), lambda p, i: (p, 0, 0, 0))
    gblk = pl.BlockSpec((None, 1, LANES), lambda p, i: (p, 0, 0))
    return pl.pallas_call(
        body, grid=(n_pairs, S // ts),
        in_specs=[ublk, pl.BlockSpec((ts, LANES), lambda p, i: (i, v0 + p)), gblk, wblk, wblk, bblk, ublk],
        out_specs=[ublk, ublk, wblk, bblk, gblk],
        out_shape=[jax.ShapeDtypeStruct((S, n_pairs * LANES), F32), jax.ShapeDtypeStruct((S, n_pairs * LANES), F32),
                   jax.ShapeDtypeStruct((n_pairs, 2, CHUNK, CHUNK), F32), jax.ShapeDtypeStruct((n_pairs, 2, CHUNK, 1), F32),
                   jax.ShapeDtypeStruct((n_pairs, 1, LANES), F32)],
        compiler_params=_params(("parallel", "arbitrary")), name=name)(proj, proj, vg, w, wT, b, do)


def loss_head(y, target, *, name):
    S, D = y.shape
    ts = _tile(S, 512, 8)

    def body(y_ref, t_ref, dy_ref, loss_ref):
        i = pl.program_id(0)
        e = y_ref[...] - t_ref[...]
        dy_ref[...] = e * (1.0 / D)
        part = jnp.sum(jnp.sum(e * e, axis=1, keepdims=True), axis=0, keepdims=True) * (0.5 / D)

        @pl.when(i == 0)
        def _():
            loss_ref[...] = part

        @pl.when(i > 0)
        def _():
            loss_ref[...] += part

    row = pl.BlockSpec((ts, D), lambda i: (i, 0))
    return pl.pallas_call(
        body, grid=(S // ts,), in_specs=[row, row],
        out_specs=[row, pl.BlockSpec((1, 1), lambda i: (0, 0))],
        out_shape=[jax.ShapeDtypeStruct((S, D), F32), jax.ShapeDtypeStruct((1, 1), F32)],
        compiler_params=_params(("arbitrary",)), name=name)(y, target)


def adamw(w, g, m, v, *, name):
    shape = w.shape
    C = shape[-1]
    R = w.size // C
    tr = _tile(R, max(8, (256 * 1024) // C // 8 * 8), 8)

    def body(w_ref, g_ref, m_ref, v_ref, d_ref, nm_ref, nv_ref):
        gv = g_ref[...]
        nm = ADAM_B1 * m_ref[...] + (1.0 - ADAM_B1) * gv
        nv = ADAM_B2 * v_ref[...] + (1.0 - ADAM_B2) * (gv * gv)
        m_hat = nm / (1.0 - ADAM_B1 ** ADAM_STEP)
        v_hat = nv / (1.0 - ADAM_B2 ** ADAM_STEP)
        d_ref[...] = -ADAM_LR * (m_hat / (jnp.sqrt(v_hat) + ADAM_EPS) + ADAM_WD * w_ref[...])
        nm_ref[...] = nm
        nv_ref[...] = nv

    blk = pl.BlockSpec((tr, C), lambda i: (i, 0))
    out = pl.pallas_call(
        body, grid=(R // tr,), in_specs=[blk] * 4, out_specs=[blk] * 3,
        out_shape=[jax.ShapeDtypeStruct((R, C), F32)] * 3,
        compiler_params=_params(("parallel",)), name=name)(*(a.reshape(R, C) for a in (w, g, m, v)))
    return tuple(o.reshape(shape) for o in out)


def pair_sum(p, landed, half, *, name):
    n, R, C = landed.shape
    tr = _tile(R, 256, 16)
    nr = R // tr

    def body(half_ref, p_ref, l_ref, o_ref):
        o_ref[...] = (p_ref[...] + l_ref[...]).astype(BF16)

    return pl.pallas_call(
        body,
        grid_spec=pltpu.PrefetchScalarGridSpec(
            num_scalar_prefetch=1, grid=(n, nr),
            in_specs=[pl.BlockSpec((None, tr, C), lambda k, r, half_ref: (k, half_ref[0] * nr + r, 0)),
                      pl.BlockSpec((None, tr, C), lambda k, r, half_ref: (k, r, 0))],
            out_specs=pl.BlockSpec((None, tr, C), lambda k, r, half_ref: (k, r, 0))),
        out_shape=jax.ShapeDtypeStruct((n, R, C), BF16),
        compiler_params=_params(("parallel", "parallel")), name=name)(half, p, landed)


def chip_sum(own, landed, chip, *, name):
    n, R, C = own.shape
    tr = _tile(R, 256, 16)

    def body(chip_ref, own_ref, *rest):
        l_refs, o_ref = rest[:n], rest[n]
        me = chip_ref[0]
        acc = None
        for d in range(n):
            term = jnp.where(me == d, own_ref[...], l_refs[d][...]).astype(F32)
            acc = term if acc is None else acc + term
        o_ref[...] = acc

    def landed_spec(d):
        return pl.BlockSpec((None, tr, C), lambda r, chip_ref: (jnp.where(chip_ref[0] == d, (d + 1) % n, d), r, 0))

    return pl.pallas_call(
        body,
        grid_spec=pltpu.PrefetchScalarGridSpec(
            num_scalar_prefetch=1, grid=(R // tr,),
            in_specs=[pl.BlockSpec((None, tr, C), lambda r, chip_ref: (chip_ref[0], r, 0))]
            + [landed_spec(d) for d in range(n)],
            out_specs=pl.BlockSpec((tr, C), lambda r, chip_ref: (r, 0))),
        out_shape=jax.ShapeDtypeStruct((R, C), F32),
        compiler_params=_params(("parallel",)), name=name)(chip, own, *([landed] * n))


def ordered_sum(parts, *, name):
    n, R, C = parts.shape
    tr = _tile(R, 256, 16)

    def body(p_ref, o_ref):
        acc = p_ref[0].astype(F32)
        for d in range(1, n):
            acc = acc + p_ref[d].astype(F32)
        o_ref[...] = acc

    return pl.pallas_call(
        body, grid=(R // tr,), in_specs=[pl.BlockSpec((n, tr, C), lambda r: (0, r, 0))],
        out_specs=pl.BlockSpec((tr, C), lambda r: (r, 0)),
        out_shape=jax.ShapeDtypeStruct((R, C), F32),
        compiler_params=_params(("parallel",)), name=name)(parts)


_ANY = pl.BlockSpec(memory_space=pl.ANY)


def _position():
    return lax.axis_index("x"), lax.axis_index("y"), lax.axis_index("c")


def _remote(src, dst, send_sem, recv_sem, device):
    return pltpu.make_async_remote_copy(src_ref=src, dst_ref=dst, send_sem=send_sem, recv_sem=recv_sem,
                                        device_id=device, device_id_type=MESH_ID)


def _small_all_gather(s_ref, all_ref, send_sems, recv_sems, x, y, c):
    me = 4 * x + 2 * y + c
    copies = []
    for f in range(1, 8):
        peer = ((1 - x) if f & 4 else x, (1 - y) if f & 2 else y, (1 - c) if f & 1 else c)
        cp = _remote(s_ref, all_ref.at[me], send_sems.at[f - 1], recv_sems.at[f - 1], peer)
        cp.start()
        copies.append((cp, peer, f - 1))

    def finish():
        for cp, peer, s in copies:
            slot = all_ref.at[4 * peer[0] + 2 * peer[1] + peer[2]]
            _remote(slot, slot, send_sems.at[s], recv_sems.at[s], peer).wait_recv()
        for cp, _, _ in copies:
            cp.wait_send()

    return finish


def _core_rows(ref, core):
    h = ref.shape[1] // 2
    return pl.ds(core * h, h)


def _gather_plan(outs, send_sems, recv_sems):
    n = len(outs)
    x, y, c = _position()
    k = 2 * x + y
    sibling = (x, y, 1 - c)
    chips = [(1 - x, y), (x, 1 - y), (1 - x, 1 - y)]

    def first():
        return [_remote(outs[w].at[k, _core_rows(outs[w], c)], outs[w].at[k, _core_rows(outs[w], c)],
                        send_sems.at[w, j], recv_sems.at[w, j], (px, py, c))
                for j, (px, py) in enumerate(chips) for w in range(n)]

    def start():
        for cp in first():
            cp.start()

    def finish():
        passed = []
        for j, (px, py) in enumerate(chips):
            for w in range(n):
                slot = outs[w].at[2 * px + py, _core_rows(outs[w], c)]
                _remote(slot, slot, send_sems.at[w, j], recv_sems.at[w, j], (px, py, c)).wait_recv()
                cp = _remote(slot, slot, send_sems.at[w, 3 + j], recv_sems.at[w, 3 + j], sibling)
                cp.start()
                passed.append(cp)
        for j, (px, py) in enumerate(chips):
            for w in range(n):
                slot = outs[w].at[2 * px + py, _core_rows(outs[w], 1 - c)]
                _remote(slot, slot, send_sems.at[w, 3 + j], recv_sems.at[w, 3 + j], sibling).wait_recv()
        for cp in first() + passed:
            cp.wait_send()

    return start, finish


def _gather_sems(n):
    return (pltpu.SemaphoreType.DMA((n, 6)), pltpu.SemaphoreType.DMA((n, 6)))


def gather_rider(slabs):
    return Rider(tuple(slabs), tuple(jax.ShapeDtypeStruct(a.shape, a.dtype) for a in slabs),
                 {i: i for i in range(len(slabs))}, _gather_sems(len(slabs)),
                 lambda ins, outs, sems: _gather_plan(outs, sems[0], sems[1]))


def gather_weights(slabs, small_slab, *, name):
    n = len(slabs)

    def body(*refs):
        outs, all_ref = refs[n + 1:2 * n + 1], refs[2 * n + 1]
        send_sems, recv_sems, s_send, s_recv = refs[2 * n + 2:]
        x, y, c = _position()
        finish_small = _small_all_gather(all_ref.at[4 * x + 2 * y + c], all_ref, s_send, s_recv, x, y, c)
        start, finish = _gather_plan(outs, send_sems, recv_sems)
        start()
        finish()
        finish_small()

    args = list(slabs) + [small_slab]
    out = pl.pallas_call(
        body, in_specs=[_ANY] * (n + 1), out_specs=[_ANY] * (n + 1),
        out_shape=[jax.ShapeDtypeStruct(a.shape, a.dtype) for a in args],
        input_output_aliases={i: i for i in range(n + 1)},
        scratch_shapes=list(_gather_sems(n)) + [pltpu.SemaphoreType.DMA((7,)), pltpu.SemaphoreType.DMA((7,))],
        name=name)(*args)
    return out[:n], out[n]


def exchange_with_sibling(parts, small_slab, *, name):
    n = len(parts)
    has_small = small_slab is not None
    n_arg = n + (1 if has_small else 0)

    def body(*refs):
        p_refs = refs[:n]
        lands = refs[n_arg:n_arg + n]
        send_sems, recv_sems = refs[2 * n_arg], refs[2 * n_arg + 1]
        x, y, c = _position()
        sibling = (x, y, 1 - c)
        if has_small:
            all_ref = refs[n_arg + n]
            finish_small = _small_all_gather(all_ref.at[4 * x + 2 * y + c], all_ref, refs[2 * n_arg + 2],
                                             refs[2 * n_arg + 3], x, y, c)
        sends = []
        for w in range(n):
            for d in range(4):
                cp = _remote(p_refs[w].at[d, _core_rows(p_refs[w], 1 - c)], lands[w].at[d],
                             send_sems.at[w, d], recv_sems.at[w, d], sibling)
                cp.start()
                sends.append(cp)
        for cp in sends:
            cp.wait_recv()
        for cp in sends:
            cp.wait_send()
        if has_small:
            finish_small()

    small_args = [small_slab] if has_small else []
    out = pl.pallas_call(
        body, in_specs=[_ANY] * n_arg, out_specs=[_ANY] * n_arg,
        out_shape=[jax.ShapeDtypeStruct((4, p.shape[1] // 2, p.shape[2]), p.dtype) for p in parts]
        + [jax.ShapeDtypeStruct(s.shape, s.dtype) for s in small_args],
        input_output_aliases={n: n} if has_small else {},
        scratch_shapes=[pltpu.SemaphoreType.DMA((n, 4)), pltpu.SemaphoreType.DMA((n, 4))]
        + ([pltpu.SemaphoreType.DMA((7,)), pltpu.SemaphoreType.DMA((7,))] if has_small else []),
        name=name)(*parts, *small_args)
    return out[:n], (out[n] if has_small else None)


def _scatter_plan(q_refs, outs, send_sems, recv_sems):
    n = len(q_refs)
    x, y, c = _position()
    k = 2 * x + y
    chips = [(1 - x, y), (x, 1 - y), (1 - x, 1 - y)]

    def sends():
        return [_remote(q_refs[w].at[2 * px + py], outs[w].at[k], send_sems.at[w, j], recv_sems.at[w, j], (px, py, c))
                for j, (px, py) in enumerate(chips) for w in range(n)]

    def start():
        for cp in sends():
            cp.start()

    def finish():
        for j, (px, py) in enumerate(chips):
            for w in range(n):
                slot = outs[w].at[2 * px + py]
                _remote(slot, slot, send_sems.at[w, j], recv_sems.at[w, j], (px, py, c)).wait_recv()
        for cp in sends():
            cp.wait_send()

    return start, finish


def _scatter_sems(n):
    return (pltpu.SemaphoreType.DMA((n, 3)), pltpu.SemaphoreType.DMA((n, 3)))


def scatter_rider(parts):
    return Rider(tuple(parts), tuple(jax.ShapeDtypeStruct(q.shape, q.dtype) for q in parts), {},
                 _scatter_sems(len(parts)), lambda ins, outs, sems: _scatter_plan(ins, outs, sems[0], sems[1]))


def scatter_to_chips(parts, *, name):
    n = len(parts)

    def body(*refs):
        start, finish = _scatter_plan(refs[:n], refs[n:2 * n], refs[2 * n], refs[2 * n + 1])
        start()
        finish()

    return pl.pallas_call(
        body, in_specs=[_ANY] * n, out_specs=[_ANY] * n,
        out_shape=[jax.ShapeDtypeStruct(q.shape, q.dtype) for q in parts],
        scratch_shapes=list(_scatter_sems(n)), name=name)(*parts)


def share_with_sibling(parts, *, name):
    n = len(parts)

    def body(*refs):
        r_refs, outs = refs[:n], refs[n:2 * n]
        send_sems, recv_sems = refs[2 * n:]
        x, y, c = _position()
        sends = []
        for w in range(n):
            cp = _remote(r_refs[w], outs[w], send_sems.at[w], recv_sems.at[w], (x, y, 1 - c))
            cp.start()
            sends.append(cp)
        for cp in sends:
            cp.wait_recv()
        for cp in sends:
            cp.wait_send()

    return pl.pallas_call(
        body, in_specs=[_ANY] * n, out_specs=[_ANY] * n,
        out_shape=[jax.ShapeDtypeStruct(r.shape, r.dtype) for r in parts],
        scratch_shapes=[pltpu.SemaphoreType.DMA((n,)), pltpu.SemaphoreType.DMA((n,))],
        name=name)(*parts)


def _cols_to_chips(full):
    *lead, R, C4 = full.shape
    t = full.reshape(*lead, R, 4, C4 // 4)
    return jnp.moveaxis(t, -2, 0)


def _chips_to_cols(sh):
    t = jnp.moveaxis(sh, 0, -2)
    return t.reshape(*t.shape[:-2], t.shape[-2] * t.shape[-1])


def _slot_in_empty(own, index, n):
    return lax.dynamic_update_slice(lax.empty((n,) + own.shape, own.dtype), own[None], (index,) + (0,) * own.ndim)


def _fold_pair(dg):
    return dg[0, :HEAD_DIM] + dg[0, HEAD_DIM:]


def _ffn_fwd(x, g, w_in_slab, w_out, tag):
    h = rms_fwd(x, g, name=f"{tag}_rms")
    a, b, act = swiglu_fwd(h, w_in_slab, name=f"{tag}_in")
    y = matmul(act, w_out, res=x, scale=0.5, tm=1024, tn=512, tk=w_out.shape[0], name=f"{tag}_out")
    return y, (x, h, a, b, act)


def _ffn_bwd(dy, saved, g, w_in_slab, w_out, tag):
    x, h, a, b, act = saved
    da, db = swiglu_bwd(dy, w_out, a, b, name=f"{tag}_dact")
    dw_out = grad_rows(act, dy, scale=0.5, name=f"{tag}_dwout")
    dw_in = grad_cols(h, da, db, name=f"{tag}_dwin")
    dx, dg = ffn_dh(da, db, w_in_slab, x, g, dy, name=f"{tag}_dh")
    return dx, dg[0], dw_in, dw_out


MEM_PAIRS = MEM_WIDTH // LANES


def _mem_attn_fwd(proj, mq0, mem_n, w_kv, g_q, g_k, tag):
    qh = pairnorm_fwd(proj, mq0, MEM_PAIRS, g_q, scale=QK_SCALE,name=f"{tag}_qnorm")
    kv = matmul(mem_n, w_kv, tm=256, tn=512, tk=1024, name=f"{tag}_kv")
    kh = pairnorm_fwd(kv, 0, MEM_PAIRS, g_k, name=f"{tag}_knorm")
    o, lse, _ = attn_fwd(qh, 0, kh, 0, kv, MEM_PAIRS, MEM_PAIRS, None, causal=False, name=f"{tag}_attn")
    return o, (qh, kv, kh, o, lse)


def _mem_attn_bwd(dmix, do0, proj, mq0, saved, mem_n, g_q, g_k, tag):
    qh, kv, kh, o, lse = saved
    delta = attn_delta(o, dmix, do0, MEM_PAIRS, name=f"{tag}_delta")
    (dqh, dkh, dv), _ = attn_bwd(qh, 0, kh, 0, kv, MEM_PAIRS, dmix, do0, MEM_PAIRS, lse, delta, None,
                                 causal=False, name=f"{tag}_dattn")
    dq_pre, dgq = pairnorm_bwd(proj, mq0, MEM_PAIRS, dqh, g_q, name=f"{tag}_dqnorm")
    dk_pre, dgk = pairnorm_bwd(kv, 0, MEM_PAIRS, dkh, g_k, name=f"{tag}_dknorm")
    dkv = jnp.concatenate([dk_pre, dv], axis=1)
    dw_kv = grad_rows(mem_n, dkv, name=f"{tag}_dwkv")
    return dq_pre, _fold_pair(dgq), _fold_pair(dgk), dw_kv, dkv


def _per_head_lanes(x, H):
    return jnp.pad(x.reshape(H, -1).T, ((0, 0), (0, LANES - H)))


def _fox_fwd(proj, b_f, g_q, g_k, tok, rider, tag):
    H, P = tok // HEAD_DIM, tok // LANES
    bias = jnp.pad(b_f.reshape(1, H), ((0, 0), (0, LANES - H)))
    qh = pairnorm_fwd(proj, 0, P, g_q, scale=QK_SCALE,name=f"{tag}_qnorm")
    kh = pairnorm_fwd(proj, P, P, g_k, name=f"{tag}_knorm")
    c, c_cols = fgate_fwd(proj, 3 * P + MEM_PAIRS, bias, H, name=f"{tag}_fgate")
    decay = (c_cols, c[:, :H].T.reshape(P, 2, 1, c.shape[0]))
    o, lse, rode = attn_fwd(qh, 0, kh, 0, proj, 2 * P, P, decay, causal=True, rider=rider, name=f"{tag}_attn")
    return o, (qh, kh, bias, decay, o, lse), rode


def _fox_bwd(dmix, proj, saved, g_q, g_k, tok, rider, tag):
    qh, kh, bias, decay, o, lse = saved
    H, P = tok // HEAD_DIM, tok // LANES
    delta = attn_delta(o, dmix, 0, P, name=f"{tag}_delta")
    (dqh, dkh, dv, dcs, drs), rode = attn_bwd(qh, 0, kh, 0, proj, 2 * P, dmix, 0, P, lse, delta, decay, causal=True,
                                              rider=rider, name=f"{tag}_dattn")
    dq_pre, dgq = pairnorm_bwd(proj, 0, P, dqh, g_q, name=f"{tag}_dqnorm")
    dk_pre, dgk = pairnorm_bwd(proj, P, P, dkh, g_k, name=f"{tag}_dknorm")
    dz, dbias = fgate_bwd(proj, 3 * P + MEM_PAIRS, bias, _per_head_lanes(drs, H), _per_head_lanes(dcs, H),
                          name=f"{tag}_dfgate")
    dqkv = jnp.concatenate([dq_pre, dk_pre, dv], axis=1)
    return dqkv, dz, dbias[0, :H], _fold_pair(dgq), _fold_pair(dgk), rode


def local_step(x, mem, target, W, comm=None):
    S, D = x.shape
    tok = D - MEM_WIDTH
    P = tok // LANES
    depth = W["norm_ffn1"].shape[0]
    mem_n = rms_fwd(mem, W["mem_norm"], name="mem_rms")
    saved = []
    for i in range(depth):
        kind, j = i % 2, i // 2
        t = f"l{i}"
        x1, s1 = _ffn_fwd(x, W["norm_ffn1"][i], W["ffn1_w_in"][i], W["ffn1_w_out"][i], f"{t}_ffn1")
        h = rms_fwd(x1, W["norm_mix"][i], name=f"{t}_mix_rms")
        w_mix = W["fox_w_in"][j] if kind == 0 else W["gmlp_w_in"][j]
        proj = matmul(h, w_mix, tm=1024, tn=896, tk=D, name=f"{t}_mix_in")
        if kind == 0:
            rider = comm.late_weights_rider() if (comm is not None and i == 0) else None
            o_tok, s_tok, rode = _fox_fwd(proj, W["fox_b_f"][j], W["fox_q_norm"][j], W["fox_k_norm"][j], tok, rider,
                                          f"{t}_fox")
            if rider is not None:
                comm.accept_late_weights(W, rode)
            mq0 = 3 * P
        else:
            vg, ws, bs = _gmlp_operands(W["gmlp_v_norm"][j], W["gmlp_w_s"][j], W["gmlp_b_s"][j])
            o_tok = gmlp_fwd(proj, P, P, vg, ws, bs, name=f"{t}_gmlp")
            s_tok = None
            mq0 = 2 * P
        o_mem, s_mem = _mem_attn_fwd(proj, mq0, mem_n, W["mem_w_kv"][i], W["mem_q_norm"][i], W["mem_k_norm"][i],
                                     f"{t}_mem")
        mix = jnp.concatenate([o_tok, o_mem], axis=1).astype(BF16)
        x2 = matmul(mix, W["w_out"][i], res=x1, tm=1024, tn=512, tk=D, name=f"{t}_mix_out")
        x3, s3 = _ffn_fwd(x2, W["norm_ffn2"][i], W["ffn2_w_in"][i], W["ffn2_w_out"][i], f"{t}_ffn2")
        saved.append((s1, x1, h, proj, mq0, s_tok, s_mem, mix, s3))
        x = x3

    dx, loss = loss_head(x, target, name="loss_head")

    G = {k: [None] * depth for k in ("norm_ffn1", "norm_mix", "norm_ffn2", "mem_q_norm", "mem_k_norm", "ffn1_w_in",
                                     "ffn1_w_out", "ffn2_w_in", "ffn2_w_out", "w_out", "mem_w_kv")}
    n_fox, n_gmlp = (depth + 1) // 2, depth // 2
    for k in ("fox_w_in", "fox_b_f", "fox_q_norm", "fox_k_norm"):
        G[k] = [None] * n_fox
    for k in ("gmlp_w_in", "gmlp_v_norm", "gmlp_w_s", "gmlp_b_s"):
        G[k] = [None] * n_gmlp
    dkv_all = [None] * depth
    for i in reversed(range(depth)):
        kind, j = i % 2, i // 2
        t = f"l{i}"
        s1, x1, h, proj, mq0, s_tok, s_mem, mix, s3 = saved[i]
        dx, G["norm_ffn2"][i], G["ffn2_w_in"][i], G["ffn2_w_out"][i] = _ffn_bwd(
            dx, s3, W["norm_ffn2"][i], W["ffn2_w_in"][i], W["ffn2_w_out"][i], f"{t}_ffn2")
        dmix = matmul(dx, W["w_out"][i], tb=True, tm=1024, tn=1024, tk=D, name=f"{t}_dmix")
        G["w_out"][i] = grad_rows(mix, dx, name=f"{t}_dwmixout")
        dmq, G["mem_q_norm"][i], G["mem_k_norm"][i], G["mem_w_kv"][i], dkv_all[i] = _mem_attn_bwd(
            dmix, P, proj, mq0, s_mem, mem_n, W["mem_q_norm"][i], W["mem_k_norm"][i], f"{t}_mem")
        if kind == 0:
            rider = comm.early_grads_rider(G) if (comm is not None and i == 0) else None
            dqkv, dz, G["fox_b_f"][j], G["fox_q_norm"][j], G["fox_k_norm"][j], rode = _fox_bwd(
                dmix, proj, s_tok, W["fox_q_norm"][j], W["fox_k_norm"][j], tok, rider, f"{t}_fox")
            if rider is not None:
                comm.accept_early_grads(rode)
            dproj = jnp.concatenate([dqkv, dmq, dz], axis=1).astype(BF16)
            w_mix, wkey = W["fox_w_in"][j], "fox_w_in"
        else:
            vg, ws, bs = _gmlp_operands(W["gmlp_v_norm"][j], W["gmlp_w_s"][j], W["gmlp_b_s"][j])
            dup, dvp, dws, dbs, dvg = gmlp_bwd(proj, P, P, vg, ws, jnp.swapaxes(ws, 2, 3), bs, dmix,
                                               name=f"{t}_dgmlp")
            G["gmlp_w_s"][j] = dws.reshape(W["gmlp_w_s"][j].shape)
            G["gmlp_b_s"][j] = dbs.reshape(W["gmlp_b_s"][j].shape)
            G["gmlp_v_norm"][j] = dvg.reshape(-1)
            dproj = jnp.concatenate([dup, dvp, dmq], axis=1).astype(BF16)
            w_mix, wkey = W["gmlp_w_in"][j], "gmlp_w_in"
        G[wkey][j] = matmul(h, dproj, ta=True, tm=1024, tn=896, tk=1024, name=f"{t}_dwmixin")
        dh = matmul(dproj, w_mix, tb=True, tm=1024, tn=1024, tk=896, name=f"{t}_dhmix")
        dx, dgm = rms_bwd(x1, dh, W["norm_mix"][i], dx, name=f"{t}_dmixrms")
        G["norm_mix"][i] = dgm[0]
        dx, G["norm_ffn1"][i], G["ffn1_w_in"][i], G["ffn1_w_out"][i] = _ffn_bwd(
            dx, s1, W["norm_ffn1"][i], W["ffn1_w_in"][i], W["ffn1_w_out"][i], f"{t}_ffn1")
    w_kv_all = jnp.concatenate([W["mem_w_kv"][i] for i in range(depth)], axis=1)
    dmem_n = matmul(jnp.concatenate(dkv_all, axis=1), w_kv_all, tb=True, tm=256, tn=512, tk=1024, name="dmem_n")
    _, dmemg = rms_bwd(mem, dmem_n, W["mem_norm"], None, name="dmem_rms")
    G["mem_norm"] = [dmemg[0]]
    return loss, dx, G


def _fox_cols_to_compute(w, tok):
    H = tok // HEAD_DIM
    qkv, f, mq = w[..., :3 * tok], w[..., 3 * tok:3 * tok + H], w[..., 3 * tok + H:]
    f = jnp.pad(f, [(0, 0)] * (w.ndim - 1) + [(0, LANES - H)])
    return jnp.concatenate([qkv, mq, f], axis=-1)


def _fox_cols_from_compute(w, tok):
    H = tok // HEAD_DIM
    qkv, mq, f = w[..., :3 * tok], w[..., 3 * tok:3 * tok + MEM_WIDTH], w[..., 3 * tok + MEM_WIDTH:3 * tok + MEM_WIDTH + H]
    return jnp.concatenate([qkv, f, mq], axis=-1)


_BIG = ("ffn1_w_in", "ffn1_w_out", "ffn2_w_in", "ffn2_w_out", "w_out", "mem_w_kv", "fox_w_in", "gmlp_w_in")
_SMALL = ("norm_ffn1", "norm_mix", "norm_ffn2", "mem_norm", "mem_q_norm", "mem_k_norm", "fox_b_f", "fox_q_norm",
          "fox_k_norm", "gmlp_v_norm", "gmlp_w_s", "gmlp_b_s")
WEIGHT_ORDER = ("norm_ffn1", "ffn1_w_in", "ffn1_w_out", "norm_mix", "norm_ffn2", "ffn2_w_in", "ffn2_w_out", "w_out",
                "mem_norm", "mem_w_kv", "mem_q_norm", "mem_k_norm", "fox_w_in", "fox_b_f", "fox_q_norm", "fox_k_norm",
                "gmlp_w_in", "gmlp_v_norm", "gmlp_w_s", "gmlp_b_s")


def _small_slab(rows_list, index):
    sizes = [s.shape[0] for s in rows_list]
    n_rows = [-(-n // LANES) for n in sizes]
    small = jnp.concatenate([jnp.pad(s, (0, r * LANES - n)).reshape(r, LANES)
                             for s, n, r in zip(rows_list, sizes, n_rows)], axis=0)
    small = jnp.pad(small, ((0, -small.shape[0] % 64), (0, 0)))
    return _slot_in_empty(small, index, 8), sizes, n_rows


_FIRST_WEIGHTS = (("ffn1_w_in", 0), ("ffn1_w_out", 0), ("fox_w_in", 0))


def _weight_from_slab(name, slab, tok):
    if name in ("ffn1_w_in", "ffn2_w_in"):
        return slab
    if name == "fox_w_in":
        return _fox_cols_to_compute(_chips_to_cols(slab), tok)
    if name == "gmlp_w_in":
        return _chips_to_cols(slab)
    return slab.reshape(4 * slab.shape[1], slab.shape[2])


def _grad_to_slab(name, g, tok):
    if name == "fox_w_in":
        return _cols_to_chips(_fox_cols_from_compute(g, tok))
    if name == "gmlp_w_in":
        return _cols_to_chips(g)
    return g


class _Exchange:
    def __init__(self, shards, tok, chip, core):
        self.tok, self.core = tok, core
        self.half = core.reshape(1).astype(jnp.int32)
        self.chip_id = chip.reshape(1).astype(jnp.int32)
        items = [(k, i) for k in _BIG for i in range(shards[k].shape[0])]
        self.slabs = {it: _slot_in_empty(shards[it[0]][it[1]].astype(BF16), chip, 4) for it in items}
        self.late = [it for it in items if it not in _FIRST_WEIGHTS]
        self.reduced = {}
        self.early = None

    def first_weights(self, small_slab):
        got, small_all = gather_weights([self.slabs[it] for it in _FIRST_WEIGHTS], small_slab, name="gather_first")
        return {it: _weight_from_slab(it[0], s, self.tok) for it, s in zip(_FIRST_WEIGHTS, got)}, small_all

    def late_weights_rider(self):
        return gather_rider([self.slabs[it] for it in self.late])

    def accept_late_weights(self, W, got):
        for (k, i), s in zip(self.late, got):
            W[k][i] = _weight_from_slab(k, s, self.tok)

    def _pair_sums(self, G, items, small_slab, tag):
        parts = [_grad_to_slab(k, G[k][i], self.tok) for k, i in items]
        landed, small_all = exchange_with_sibling(parts, small_slab, name=f"grad_exchange_{tag}")
        pair = [pair_sum(p, l, self.half, name=f"grad_pair_sum_{k}{i}") for (k, i), p, l in zip(items, parts, landed)]
        return pair, small_all

    def early_grads_rider(self, G):
        items = [(k, i) for k in _BIG for i in range(len(G[k])) if G[k][i] is not None]
        pair, _ = self._pair_sums(G, items, None, "early")
        self.early = (items, pair)
        return scatter_rider(pair)

    def accept_early_grads(self, landed):
        items, pair = self.early
        self._chip_sums(items, pair, landed)

    def _chip_sums(self, items, pair, landed):
        for (k, i), q, l in zip(items, pair, landed):
            self.reduced[(k, i)] = chip_sum(q, l, self.chip_id, name=f"grad_chip_sum_{k}{i}")

    def finish_grads(self, G, small_slab):
        items = [(k, i) for k in _BIG for i in range(len(G[k])) if (k, i) not in self.reduced]
        pair, small_all = self._pair_sums(G, items, small_slab, "late")
        self._chip_sums(items, pair, scatter_to_chips(pair, name="grad_scatter_late"))
        order = sorted(self.reduced)
        other = share_with_sibling([self.reduced[it] for it in order], name="grad_share")
        full = {}
        for it, a, b in zip(order, [self.reduced[it] for it in order], other):
            full[it] = jnp.where(self.core == 0, jnp.concatenate([a, b]), jnp.concatenate([b, a]))
        names = sorted({k for k, _ in order})
        return {k: jnp.stack([full[(k, i)] for i in range(len(G[k]))]) for k in names}, small_all


def kernel(x, mem, norm_ffn1, ffn1_w_in, ffn1_w_out, norm_mix, norm_ffn2, ffn2_w_in, ffn2_w_out, w_out, mem_norm, mem_w_kv, mem_q_norm, mem_k_norm, fox_w_in, fox_b_f, fox_q_norm, fox_k_norm, gmlp_w_in, gmlp_v_norm, gmlp_w_s, gmlp_b_s, loss_target, m_norm_ffn1, m_ffn1_w_in, m_ffn1_w_out, m_norm_mix, m_norm_ffn2, m_ffn2_w_in, m_ffn2_w_out, m_w_out, m_mem_norm, m_mem_w_kv, m_mem_q_norm, m_mem_k_norm, m_fox_w_in, m_fox_b_f, m_fox_q_norm, m_fox_k_norm, m_gmlp_w_in, m_gmlp_v_norm, m_gmlp_w_s, m_gmlp_b_s, v_norm_ffn1, v_ffn1_w_in, v_ffn1_w_out, v_norm_mix, v_norm_ffn2, v_ffn2_w_in, v_ffn2_w_out, v_w_out, v_mem_norm, v_mem_w_kv, v_mem_q_norm, v_mem_k_norm, v_fox_w_in, v_fox_b_f, v_fox_q_norm, v_fox_k_norm, v_gmlp_w_in, v_gmlp_v_norm, v_gmlp_w_s, v_gmlp_b_s):
    w = dict(norm_ffn1=norm_ffn1, ffn1_w_in=ffn1_w_in, ffn1_w_out=ffn1_w_out, norm_mix=norm_mix, norm_ffn2=norm_ffn2,
             ffn2_w_in=ffn2_w_in, ffn2_w_out=ffn2_w_out, w_out=w_out, mem_norm=mem_norm, mem_w_kv=mem_w_kv,
             mem_q_norm=mem_q_norm, mem_k_norm=mem_k_norm, fox_w_in=fox_w_in, fox_b_f=fox_b_f, fox_q_norm=fox_q_norm,
             fox_k_norm=fox_k_norm, gmlp_w_in=gmlp_w_in, gmlp_v_norm=gmlp_v_norm, gmlp_w_s=gmlp_w_s, gmlp_b_s=gmlp_b_s)
    m = dict(norm_ffn1=m_norm_ffn1, ffn1_w_in=m_ffn1_w_in, ffn1_w_out=m_ffn1_w_out, norm_mix=m_norm_mix,
             norm_ffn2=m_norm_ffn2, ffn2_w_in=m_ffn2_w_in, ffn2_w_out=m_ffn2_w_out, w_out=m_w_out, mem_norm=m_mem_norm,
             mem_w_kv=m_mem_w_kv, mem_q_norm=m_mem_q_norm, mem_k_norm=m_mem_k_norm, fox_w_in=m_fox_w_in,
             fox_b_f=m_fox_b_f, fox_q_norm=m_fox_q_norm, fox_k_norm=m_fox_k_norm, gmlp_w_in=m_gmlp_w_in,
             gmlp_v_norm=m_gmlp_v_norm, gmlp_w_s=m_gmlp_w_s, gmlp_b_s=m_gmlp_b_s)
    v = dict(norm_ffn1=v_norm_ffn1, ffn1_w_in=v_ffn1_w_in, ffn1_w_out=v_ffn1_w_out, norm_mix=v_norm_mix,
             norm_ffn2=v_norm_ffn2, ffn2_w_in=v_ffn2_w_in, ffn2_w_out=v_ffn2_w_out, w_out=v_w_out, mem_norm=v_mem_norm,
             mem_w_kv=v_mem_w_kv, mem_q_norm=v_mem_q_norm, mem_k_norm=v_mem_k_norm, fox_w_in=v_fox_w_in,
             fox_b_f=v_fox_b_f, fox_q_norm=v_fox_q_norm, fox_k_norm=v_fox_k_norm, gmlp_w_in=v_gmlp_w_in,
             gmlp_v_norm=v_gmlp_v_norm, gmlp_w_s=v_gmlp_w_s, gmlp_b_s=v_gmlp_b_s)
    D = x.shape[-1]
    tok = D - MEM_WIDTH
    xi, yi, ci = _position()
    chip = 2 * xi + yi

    device = 4 * xi + 2 * yi + ci

    comm = _Exchange(w, tok, chip, ci)
    vn = w["gmlp_v_norm"]
    vn_slab, _, _ = _small_slab([vn.reshape(-1)], device)
    first, vn_all = comm.first_weights(vn_slab)
    W = {k: w[k] for k in _SMALL}
    W["gmlp_v_norm"] = _chips_to_cols(vn_all[0::2].reshape(4, -1)[:, :vn.size].reshape((4,) + vn.shape))
    for k in _BIG:
        W[k] = [first.get((k, i)) for i in range(w[k].shape[0])]

    loss, grad_x, g = local_step(x[0], mem[0], loss_target[0], W, comm)

    small_list = [jnp.stack(g[k]).reshape(-1) for k in _SMALL] + [loss.reshape(-1)]
    small, small_sizes, small_rows = _small_slab(small_list, device)
    red, small_all = comm.finish_grads(g, small)
    small_sum = ordered_sum(small_all, name="small_sum")
    off = 0
    for k, n, r in zip(_SMALL, small_sizes, small_rows):
        red[k] = small_sum[off:off + r].reshape(-1)[:n].reshape((-1,) + w[k].shape[1:] if k != "gmlp_v_norm"
                                                                else (w[k].shape[0], -1))
        off += r
    loss_total = small_sum[off, 0]
    vn_cols = w["gmlp_v_norm"].shape[-1]
    red["gmlp_v_norm"] = lax.dynamic_slice_in_dim(red["gmlp_v_norm"], chip * vn_cols, vn_cols, axis=-1)

    deltas, new_m, new_v = {}, {}, {}
    for k in WEIGHT_ORDER:
        wk = w[k] if w[k].ndim > 1 else w[k].reshape(1, -1)
        upd = adamw(wk, red[k].reshape(wk.shape), m[k].reshape(wk.shape), v[k].reshape(wk.shape), name=f"adamw_{k}")
        deltas[k], new_m[k], new_v[k] = (u.reshape(w[k].shape) for u in upd)
    return (loss_total, grad_x[None], *[red[k].reshape(w[k].shape) for k in WEIGHT_ORDER],
            *[deltas[k] for k in WEIGHT_ORDER], *[new_m[k] for k in WEIGHT_ORDER], *[new_v[k] for k in WEIGHT_ORDER])
```

```python
import functools
import math
from typing import Callable, NamedTuple

import jax
import jax.numpy as jnp
from jax import lax
from jax.experimental import pallas as pl
from jax.experimental.pallas import tpu as pltpu

F32 = jnp.float32
BF16 = jnp.bfloat16
EPS = 1e-6
HEAD_DIM = 64
MEM_WIDTH = 256
CHUNK = 128
LANES = 128
NEG = -1e30
VMEM_LIMIT_BYTES = 56 * 1024 * 1024
ATTN_Q_BLOCK = 1024
ATTN_K_BLOCK = 1024
QK_SCALE = 0.125
MESH_ID = pl.DeviceIdType.MESH

ADAM_LR = 0.001
ADAM_B1 = 0.9
ADAM_B2 = 0.999
ADAM_EPS = 1e-08
ADAM_WD = 0.01
ADAM_STEP = 10


def _tile(n, pref, align):
    t = (min(pref, n) // align) * align
    while t >= align:
        if n % t == 0:
            return t
        t -= align
    return n


def _params(sem):
    return pltpu.CompilerParams(dimension_semantics=sem, vmem_limit_bytes=VMEM_LIMIT_BYTES)


def _dot(a, b, ca, cb):
    return lax.dot_general(a, b, (((ca,), (cb,)), ((), ())), preferred_element_type=F32)


def _sigmoid(x):
    return 1.0 / (1.0 + jnp.exp(-x))


_GELU_C = math.sqrt(2.0 / math.pi)


def _gelu(x):
    return 0.5 * x * (1.0 + jnp.tanh(_GELU_C * (x + 0.044715 * (x * x * x))))


def _gelu_grad(x):
    t = jnp.tanh(_GELU_C * (x + 0.044715 * (x * x * x)))
    return 0.5 * (1.0 + t) + 0.5 * x * (1.0 - t * t) * (_GELU_C * (1.0 + 3.0 * 0.044715 * (x * x)))


def matmul(a, b, *, ta=False, tb=False, out_dtype=F32, scale=None, res=None,
           tm=1024, tn=512, tk=1024, name):
    if ta:
        K, M = a.shape
    else:
        M, K = a.shape
    N = b.shape[0] if tb else b.shape[1]
    tm = _tile(M, tm, LANES if ta else 16)
    tn = _tile(N, tn, LANES)
    tk = _tile(K, tk, LANES)
    nk = K // tk
    a_spec = pl.BlockSpec((tk, tm), lambda i, j, k: (k, i)) if ta else pl.BlockSpec((tm, tk), lambda i, j, k: (i, k))
    b_spec = pl.BlockSpec((tn, tk), lambda i, j, k: (j, k)) if tb else pl.BlockSpec((tk, tn), lambda i, j, k: (k, j))
    o_spec = pl.BlockSpec((tm, tn), lambda i, j, k: (i, j))
    ca, cb = (0 if ta else 1), (1 if tb else 0)
    has_res = res is not None

    def body(*refs):
        a_ref, b_ref = refs[0], refs[1]
        res_ref = refs[2] if has_res else None
        o_ref = refs[3] if has_res else refs[2]
        acc_ref = refs[-1]
        k = pl.program_id(2)
        prod = _dot(a_ref[...].astype(BF16), b_ref[...].astype(BF16), ca, cb)

        def finish(acc):
            if scale is not None:
                acc = acc * scale
            if has_res:
                acc = res_ref[...] + acc
            o_ref[...] = acc.astype(out_dtype)

        if nk == 1:
            finish(prod)
        else:
            @pl.when(k == 0)
            def _():
                acc_ref[...] = prod

            @pl.when(k > 0)
            def _():
                acc_ref[...] += prod

            @pl.when(k == nk - 1)
            def _():
                finish(acc_ref[...])

    in_specs = [a_spec, b_spec] + ([o_spec] if has_res else [])
    args = (a, b) + ((res,) if has_res else ())
    return pl.pallas_call(
        body, grid=(M // tm, N // tn, nk), in_specs=in_specs, out_specs=o_spec,
        out_shape=jax.ShapeDtypeStruct((M, N), out_dtype),
        scratch_shapes=[pltpu.VMEM((tm, tn) if nk > 1 else (8, LANES), F32)],
        compiler_params=_params(("parallel", "parallel", "arbitrary")), name=name)(*args)


def swiglu_fwd(h, w_slab, *, name):
    S, D = h.shape
    Fc = w_slab.shape[-1]
    tm = _tile(S, 512, 16)

    def body(h_ref, wa_ref, wb_ref, a_ref, b_ref, act_ref):
        hv = h_ref[...]
        a = _dot(hv, wa_ref[...], 1, 0)
        b = _dot(hv, wb_ref[...], 1, 0)
        a_ref[...] = a.astype(BF16)
        b_ref[...] = b.astype(BF16)
        act_ref[...] = (a * _sigmoid(a) * b).astype(BF16)

    out = pl.BlockSpec((tm, Fc), lambda j, i: (i, j))
    return pl.pallas_call(
        body, grid=(2, S // tm),
        in_specs=[pl.BlockSpec((tm, D), lambda j, i: (i, 0)),
                  pl.BlockSpec((None, D, Fc), lambda j, i: (j, 0, 0)),
                  pl.BlockSpec((None, D, Fc), lambda j, i: (j + 2, 0, 0))],
        out_specs=[out, out, out],
        out_shape=[jax.ShapeDtypeStruct((S, 2 * Fc), BF16)] * 3,
        compiler_params=_params(("parallel", "parallel")), name=name)(h, w_slab, w_slab)


def swiglu_bwd(dy, w_out, a, b, *, name):
    S, D = dy.shape
    F = w_out.shape[0]
    fc = F // 2
    tm = _tile(S, 512, 16)

    def body(dy_ref, w_ref, a_ref, b_ref, da_ref, db_ref):
        dact = 0.5 * _dot(dy_ref[...].astype(BF16), w_ref[...], 1, 1)
        av = a_ref[...].astype(F32)
        sg = _sigmoid(av)
        da_ref[...] = (dact * b_ref[...].astype(F32) * (sg * (1.0 + av * (1.0 - sg)))).astype(BF16)
        db_ref[...] = (dact * (av * sg)).astype(BF16)

    blk = pl.BlockSpec((tm, fc), lambda j, i: (i, j))
    return pl.pallas_call(
        body, grid=(2, S // tm),
        in_specs=[pl.BlockSpec((tm, D), lambda j, i: (i, 0)), pl.BlockSpec((fc, D), lambda j, i: (j, 0)), blk, blk],
        out_specs=[blk, blk],
        out_shape=[jax.ShapeDtypeStruct((S, F), BF16), jax.ShapeDtypeStruct((S, F), BF16)],
        compiler_params=_params(("parallel", "parallel")), name=name)(dy, w_out, a, b)


def ffn_dh(da, db, w_slab, x, g, dy, *, name):
    S, F = da.shape
    D, Fc = w_slab.shape[-2:]
    tm = _tile(S, 1024, 16)
    sub = _tile(tm, 256, 8)

    def body(da_ref, db_ref, w_ref, x_ref, g_ref, dy_ref, dx_ref, dg_ref, acc_ref):
        i, k = pl.program_id(0), pl.program_id(1)

        @pl.when(k == 0)
        def _():
            acc_ref[...] = jnp.zeros_like(acc_ref)

        @pl.when(k < 2)
        def _():
            acc_ref[...] += _dot(da_ref[...], w_ref[...], 1, 1)

        @pl.when(k >= 2)
        def _():
            acc_ref[...] += _dot(db_ref[...], w_ref[...], 1, 1)

        @pl.when(k == 3)
        def _():
            part = None
            for c in range(tm // sub):
                rows = pl.ds(c * sub, sub)
                xv, dh = x_ref[rows, :], acc_ref[rows, :]
                r = lax.rsqrt(jnp.mean(xv * xv, axis=-1, keepdims=True) + EPS)
                u = dh * g_ref[...]
                dx_ref[rows, :] = dy_ref[rows, :] + (r * u - xv * (r * r * r) * jnp.mean(xv * u, axis=-1, keepdims=True))
                p = jnp.sum(dh * xv * r, axis=0, keepdims=True)
                part = p if part is None else part + p

            @pl.when(i == 0)
            def _():
                dg_ref[...] = part

            @pl.when(i > 0)
            def _():
                dg_ref[...] += part

    row = pl.BlockSpec((tm, D), lambda i, k: (i, 0))
    vec = pl.BlockSpec((1, D), lambda i, k: (0, 0))
    return pl.pallas_call(
        body, grid=(S // tm, 4),
        in_specs=[pl.BlockSpec((tm, Fc), lambda i, k: (i, jnp.minimum(k, 1))),
                  pl.BlockSpec((tm, Fc), lambda i, k: (i, jnp.maximum(k - 2, 0))),
                  pl.BlockSpec((None, D, Fc), lambda i, k: (k, 0, 0)), row, vec, row],
        out_specs=[row, vec],
        out_shape=[jax.ShapeDtypeStruct((S, D), F32), jax.ShapeDtypeStruct((1, D), F32)],
        scratch_shapes=[pltpu.VMEM((tm, D), F32)],
        compiler_params=_params(("arbitrary", "arbitrary")), name=name)(da, db, w_slab, x, g.reshape(1, D), dy)


def grad_cols(h, da, db, *, name):
    S, D = h.shape
    Fc = da.shape[1] // 2
    tk = _tile(S, 1024, 16)
    nk = S // tk

    def body(h_ref, da_ref, db_ref, o_ref, acc_ref):
        ch, k = pl.program_id(0), pl.program_id(1)

        @pl.when(k == 0)
        def _():
            acc_ref[...] = jnp.zeros_like(acc_ref)

        @pl.when(ch < 2)
        def _():
            acc_ref[...] += _dot(h_ref[...], da_ref[...], 0, 0)

        @pl.when(ch >= 2)
        def _():
            acc_ref[...] += _dot(h_ref[...], db_ref[...], 0, 0)

        @pl.when(k == nk - 1)
        def _():
            o_ref[...] = acc_ref[...]

    return pl.pallas_call(
        body, grid=(4, nk),
        in_specs=[pl.BlockSpec((tk, D), lambda ch, k: (k, 0)),
                  pl.BlockSpec((tk, Fc), lambda ch, k: (jnp.where(ch < 2, k, 0), jnp.minimum(ch, 1))),
                  pl.BlockSpec((tk, Fc), lambda ch, k: (jnp.where(ch >= 2, k, 0), jnp.maximum(ch - 2, 0)))],
        out_specs=pl.BlockSpec((None, D, Fc), lambda ch, k: (ch, 0, 0)),
        out_shape=jax.ShapeDtypeStruct((4, D, Fc), F32),
        scratch_shapes=[pltpu.VMEM((D, Fc), F32)],
        compiler_params=_params(("parallel", "arbitrary")), name=name)(h, da, db)


def grad_rows(a, b, *, scale=None, name):
    S, M = a.shape
    N = b.shape[1]
    R = M // 4
    tn = _tile(N, 512, LANES)
    tk = _tile(S, 1024, 16)
    nk = S // tk

    def body(a_ref, b_ref, o_ref, acc_ref):
        k = pl.program_id(1)

        @pl.when(k == 0)
        def _():
            acc_ref[...] = jnp.zeros_like(acc_ref)

        acc_ref[...] += _dot(a_ref[...].astype(BF16), b_ref[...].astype(BF16), 0, 0)

        @pl.when(k == nk - 1)
        def _():
            for d in range(4):
                part = acc_ref[d * R:(d + 1) * R, :]
                o_ref[d] = part if scale is None else part * scale

    return pl.pallas_call(
        body, grid=(N // tn, nk),
        in_specs=[pl.BlockSpec((tk, M), lambda j, k: (k, 0)), pl.BlockSpec((tk, tn), lambda j, k: (k, j))],
        out_specs=pl.BlockSpec((4, R, tn), lambda j, k: (0, 0, j)),
        out_shape=jax.ShapeDtypeStruct((4, R, N), F32),
        scratch_shapes=[pltpu.VMEM((M, tn), F32)],
        compiler_params=_params(("parallel", "arbitrary")), name=name)(a, b)


def rms_fwd(x, g, *, name):
    S, D = x.shape
    ts = _tile(S, 1024, 16)

    def body(x_ref, g_ref, h_ref):
        xv = x_ref[...]
        r = lax.rsqrt(jnp.mean(xv * xv, axis=-1, keepdims=True) + EPS)
        h_ref[...] = (xv * r * g_ref[...]).astype(BF16)

    return pl.pallas_call(
        body, grid=(S // ts,),
        in_specs=[pl.BlockSpec((ts, D), lambda i: (i, 0)), pl.BlockSpec((1, D), lambda i: (0, 0))],
        out_specs=pl.BlockSpec((ts, D), lambda i: (i, 0)),
        out_shape=jax.ShapeDtypeStruct((S, D), BF16),
        compiler_params=_params(("parallel",)), name=name)(x, g.reshape(1, D))


def rms_bwd(x, dh, g, res, *, name):
    S, D = x.shape
    ts = _tile(S, 512, 16)
    has_res = res is not None

    def body(*refs):
        x_ref, dh_ref, g_ref = refs[:3]
        res_ref = refs[3] if has_res else None
        dx_ref, dg_ref = refs[-2:]
        i = pl.program_id(0)
        xv, dhv = x_ref[...], dh_ref[...].astype(F32)
        r = lax.rsqrt(jnp.mean(xv * xv, axis=-1, keepdims=True) + EPS)
        u = dhv * g_ref[...]
        dx = r * u - xv * (r * r * r) * jnp.mean(xv * u, axis=-1, keepdims=True)
        if has_res:
            dx = res_ref[...] + dx
        dx_ref[...] = dx
        part = jnp.sum(dhv * xv * r, axis=0, keepdims=True)

        @pl.when(i == 0)
        def _():
            dg_ref[...] = part

        @pl.when(i > 0)
        def _():
            dg_ref[...] += part

    row = pl.BlockSpec((ts, D), lambda i: (i, 0))
    vec = pl.BlockSpec((1, D), lambda i: (0, 0))
    args = (x, dh, g.reshape(1, D)) + ((res,) if has_res else ())
    return pl.pallas_call(
        body, grid=(S // ts,), in_specs=[row, row, vec] + ([row] if has_res else []),
        out_specs=[row, vec],
        out_shape=[jax.ShapeDtypeStruct((S, D), F32), jax.ShapeDtypeStruct((1, D), F32)],
        compiler_params=_params(("arbitrary",)), name=name)(*args)


def _low_half(shape):
    return lax.broadcasted_iota(jnp.int32, shape, len(shape) - 1) < HEAD_DIM


def _half_sums(x, low):
    sa = jnp.sum(jnp.where(low, x, 0.0), axis=1, keepdims=True)
    sb = jnp.sum(jnp.where(low, 0.0, x), axis=1, keepdims=True)
    return jnp.where(low, sa, sb)


def pairnorm_fwd(x, col0, n_pairs, g, *, scale=None, name):
    S = x.shape[0]
    ts = _tile(S, 1024, 16)

    def body(x_ref, g_ref, o_ref):
        xv = x_ref[...]
        r = lax.rsqrt(_half_sums(xv * xv, _low_half(xv.shape)) * (1.0 / HEAD_DIM) + EPS)
        y = xv * r * g_ref[...]
        o_ref[...] = (y if scale is None else y * scale).astype(BF16)

    return pl.pallas_call(
        body, grid=(S // ts, n_pairs),
        in_specs=[pl.BlockSpec((ts, LANES), lambda i, j: (i, col0 + j)), pl.BlockSpec((1, LANES), lambda i, j: (0, 0))],
        out_specs=pl.BlockSpec((ts, LANES), lambda i, j: (i, j)),
        out_shape=jax.ShapeDtypeStruct((S, n_pairs * LANES), BF16),
        compiler_params=_params(("parallel", "parallel")), name=name)(x, jnp.tile(g.reshape(1, HEAD_DIM), (1, 2)))


def pairnorm_bwd(x, col0, n_pairs, dy, g, *, name):
    S = x.shape[0]
    ts = _tile(S, 1024, 16)

    def body(x_ref, dy_ref, g_ref, dx_ref, dg_ref):
        first = jnp.logical_and(pl.program_id(0) == 0, pl.program_id(1) == 0)
        xv, dyv = x_ref[...], dy_ref[...]
        low = _low_half(xv.shape)
        r = lax.rsqrt(_half_sums(xv * xv, low) * (1.0 / HEAD_DIM) + EPS)
        u = dyv * g_ref[...]
        dx_ref[...] = r * u - xv * (r * r * r) * (_half_sums(xv * u, low) * (1.0 / HEAD_DIM))
        part = jnp.sum(dyv * xv * r, axis=0, keepdims=True)

        @pl.when(first)
        def _():
            dg_ref[...] = part

        @pl.when(jnp.logical_not(first))
        def _():
            dg_ref[...] += part

    vec = pl.BlockSpec((1, LANES), lambda i, j: (0, 0))
    blk = pl.BlockSpec((ts, LANES), lambda i, j: (i, j))
    return pl.pallas_call(
        body, grid=(S // ts, n_pairs),
        in_specs=[pl.BlockSpec((ts, LANES), lambda i, j: (i, col0 + j)), blk, vec], out_specs=[blk, vec],
        out_shape=[jax.ShapeDtypeStruct((S, n_pairs * LANES), F32), jax.ShapeDtypeStruct((1, LANES), F32)],
        compiler_params=_params(("arbitrary", "arbitrary")), name=name)(x, dy, jnp.tile(g.reshape(1, HEAD_DIM), (1, 2)))


def _split3(x):
    x1 = x.astype(BF16)
    r1 = x - x1.astype(F32)
    x2 = r1.astype(BF16)
    x3 = (r1 - x2.astype(F32)).astype(BF16)
    return x1, x2, x3


def _tri_ones(n, lower):
    r = lax.broadcasted_iota(jnp.int32, (n, n), 0)
    c = lax.broadcasted_iota(jnp.int32, (n, n), 1)
    return jnp.where((c <= r) if lower else (c >= r), 1.0, 0.0).astype(BF16)


def fgate_fwd(z, col0, bias, *, name):
    S, L = z.shape[0], LANES
    tb = _tile(S, 256, 16)

    def body(z_ref, b_ref, c_ref, carry):
        i = pl.program_id(0)

        @pl.when(i == 0)
        def _():
            carry[...] = jnp.zeros_like(carry)

        zz = z_ref[...] + b_ref[...]
        lf = jnp.minimum(zz, 0.0) - jnp.log(1.0 + jnp.exp(-jnp.abs(zz)))
        tri = _tri_ones(tb, True)
        x1, x2, x3 = _split3(lf)
        c = (_dot(tri, x1, 1, 0) + _dot(tri, x2, 1, 0)) + _dot(tri, x3, 1, 0) + carry[...]
        c_ref[...] = c
        carry[...] += jnp.sum(lf, axis=0, keepdims=True)

    return pl.pallas_call(
        body, grid=(S // tb,),
        in_specs=[pl.BlockSpec((tb, L), lambda i: (i, col0)), pl.BlockSpec((1, L), lambda i: (0, 0))],
        out_specs=pl.BlockSpec((tb, L), lambda i: (i, 0)),
        out_shape=jax.ShapeDtypeStruct((S, L), F32),
        scratch_shapes=[pltpu.VMEM((1, L), F32)],
        compiler_params=_params(("arbitrary",)), name=name)(z, bias)


def fgate_bwd(z, col0, bias, drs, dcs, *, name):
    S, L = z.shape[0], LANES
    n_pairs = drs.shape[0]
    tb = _tile(S, 256, 16)
    nb = S // tb

    def body(z_ref, b_ref, drs_ref, dcs_ref, dz_ref, db_ref, carry):
        i = pl.program_id(0)

        @pl.when(i == 0)
        def _():
            carry[...] = jnp.zeros_like(carry)

        tri = _tri_ones(tb, False)
        lane = lax.broadcasted_iota(jnp.int32, (tb, L), 1)
        dc = -dcs_ref[...]
        for h in range(2 * n_pairs):
            dc = dc + jnp.where(lane == h, jnp.sum(drs_ref[h // 2, h % 2], axis=1, keepdims=True), 0.0)
        x1, x2, x3 = _split3(dc)
        dlf = (_dot(tri, x1, 1, 0) + _dot(tri, x2, 1, 0)) + _dot(tri, x3, 1, 0) + carry[...]
        carry[...] += jnp.sum(dc, axis=0, keepdims=True)
        dz = dlf * _sigmoid(-(z_ref[...] + b_ref[...]))
        dz_ref[...] = dz
        part = jnp.sum(dz, axis=0, keepdims=True)

        @pl.when(i == 0)
        def _():
            db_ref[...] = part

        @pl.when(i > 0)
        def _():
            db_ref[...] += part

    rev = pl.BlockSpec((tb, L), lambda i: (nb - 1 - i, 0))
    vec = pl.BlockSpec((1, L), lambda i: (0, 0))
    return pl.pallas_call(
        body, grid=(nb,),
        in_specs=[pl.BlockSpec((tb, L), lambda i: (nb - 1 - i, col0)), vec,
                  pl.BlockSpec((n_pairs, 2, tb, L), lambda i: (0, 0, nb - 1 - i, 0)), rev],
        out_specs=[rev, vec],
        out_shape=[jax.ShapeDtypeStruct((S, L), F32), jax.ShapeDtypeStruct((1, L), F32)],
        scratch_shapes=[pltpu.VMEM((1, L), F32)],
        compiler_params=_params(("arbitrary",)), name=name)(z, bias, drs, dcs)


def _one_head(x, low, a):
    return jnp.where(low if a == 0 else jnp.logical_not(low), x, jnp.zeros_like(x))


class Rider(NamedTuple):
    inputs: tuple
    out_shapes: tuple
    aliases: dict
    sems: tuple
    plan: Callable


def _with_rider(rider, n_in, n_out, n_scratch):
    if rider is None:
        return [], [], [], [], {}, lambda refs: (refs[:n_in], refs[n_in:n_in + n_out], refs[n_in + n_out:], None)
    e_in, e_out = len(rider.inputs), len(rider.out_shapes)

    def split(refs):
        ins, r_in = refs[:n_in], refs[n_in:n_in + e_in]
        o0 = n_in + e_in
        outs, r_out = refs[o0:o0 + n_out], refs[o0 + n_out:o0 + n_out + e_out]
        s0 = o0 + n_out + e_out
        return ins, outs, refs[s0:s0 + n_scratch], rider.plan(r_in, r_out, refs[s0 + n_scratch:])

    aliases = {n_in + a: n_out + b for a, b in rider.aliases.items()}
    return list(rider.inputs), [_ANY] * e_in, list(rider.out_shapes), [_ANY] * e_out, aliases, split


def attn_fwd(q, q0, k, k0, v, v0, n_pairs, decay, *, causal, rider=None, name):
    Sq, Sk = q.shape[0], k.shape[0]
    tq = _tile(Sq, ATTN_Q_BLOCK, LANES)
    tk = _tile(Sk, ATTN_K_BLOCK, LANES)
    nq, nk = Sq // tq, Sk // tk
    bias = decay is not None
    r_args, r_in_specs, r_shapes, r_out_specs, aliases, split = _with_rider(rider, 4 if bias else 3, 2, 2)

    def row_sum_lanes(acc, low, a):
        other = jnp.logical_not(low) if a == 0 else low
        return jnp.max(jnp.where(other, acc, 0.0), axis=1, keepdims=True)

    def body(*refs):
        ins, (o_ref, lse_ref), (m_sc, acc_sc), ride = split(refs)
        q_ref, k_ref, v_ref = ins[:3]
        ck_ref = ins[3] if bias else None
        pr, i, j = pl.program_id(0), pl.program_id(1), pl.program_id(2)
        if ride is not None:
            pl.when(jnp.logical_and(pr == 0, jnp.logical_and(i == 0, j == 0)))(ride[0])

        @pl.when(j == 0)
        def _():
            m_sc[...] = jnp.full_like(m_sc, NEG)
            acc_sc[...] = jnp.zeros_like(acc_sc)

        def compute(masked):
            qv, kv, vv = q_ref[...], k_ref[...], v_ref[...].astype(BF16)
            low, low_k = _low_half(qv.shape), _low_half(kv.shape)
            for a in range(2):
                s = _dot(_one_head(qv, low, a), kv, 1, 1)
                if bias:
                    s = s - ck_ref[a]
                if masked:
                    row = i * tq + lax.broadcasted_iota(jnp.int32, (tq, tk), 0)
                    col = j * tk + lax.broadcasted_iota(jnp.int32, (tq, tk), 1)
                    s = jnp.where(col <= row, s, NEG)
                m_prev = m_sc[a]
                m_new = jnp.maximum(m_prev, jnp.max(s, axis=1, keepdims=True))
                alpha = jnp.exp(m_prev - m_new)
                p = jnp.exp(s - m_new).astype(BF16)
                va = jnp.where(low_k if a == 0 else jnp.logical_not(low_k), vv, jnp.ones_like(vv))
                acc_sc[a] = alpha * acc_sc[a] + _dot(p, va, 1, 0)
                m_sc[a] = m_new

        if causal:
            live = j * tk <= i * tq + (tq - 1)
            crosses = j * tk + (tk - 1) > i * tq
            pl.when(jnp.logical_and(live, crosses))(functools.partial(compute, True))
            pl.when(jnp.logical_and(live, jnp.logical_not(crosses)))(functools.partial(compute, False))
        else:
            compute(False)

        @pl.when(j == nk - 1)
        def _():
            low = _low_half((tq, LANES))
            l = [row_sum_lanes(acc_sc[a], low, a) for a in range(2)]
            o_ref[...] = jnp.where(low, acc_sc[0] / l[0], acc_sc[1] / l[1])
            for a in range(2):
                lse_ref[a] = m_sc[a] + jnp.log(l[a])

        if ride is not None:
            pl.when(jnp.logical_and(pr == n_pairs - 1, jnp.logical_and(i == nq - 1, j == nk - 1)))(ride[1])

    def kv_blk(i, j):
        return jnp.minimum(j, (i * tq + tq - 1) // tk) if causal else j

    in_specs = [pl.BlockSpec((tq, LANES), lambda p, i, j: (i, q0 + p)),
                pl.BlockSpec((tk, LANES), lambda p, i, j: (kv_blk(i, j), k0 + p)),
                pl.BlockSpec((tk, LANES), lambda p, i, j: (kv_blk(i, j), v0 + p))]
    args = [q, k, v]
    if bias:
        in_specs.append(pl.BlockSpec((None, 2, 1, tk), lambda p, i, j: (p, 0, 0, kv_blk(i, j))))
        args.append(decay)
    out = pl.pallas_call(
        body, grid=(n_pairs, nq, nk), in_specs=in_specs + r_in_specs,
        out_specs=[pl.BlockSpec((tq, LANES), lambda p, i, j: (i, p)),
                   pl.BlockSpec((None, 2, tq, 1), lambda p, i, j: (p, 0, i, 0))] + r_out_specs,
        out_shape=[jax.ShapeDtypeStruct((Sq, n_pairs * LANES), F32),
                   jax.ShapeDtypeStruct((n_pairs, 2, Sq, 1), F32)] + r_shapes,
        scratch_shapes=[pltpu.VMEM((2, tq, 1), F32), pltpu.VMEM((2, tq, LANES), F32)]
        + (list(rider.sems) if rider else []),
        input_output_aliases=aliases,
        compiler_params=_params(("arbitrary",) * 3 if rider else ("parallel", "parallel", "arbitrary")),
        name=name)(*args, *r_args)
    return out[0], out[1], out[2:]


def attn_delta(o, do, do0, n_pairs, *, name):
    S = o.shape[0]
    ts = _tile(S, 1024, 16)

    def body(o_ref, do_ref, out_ref):
        prod = o_ref[...] * do_ref[...]
        low = _low_half(prod.shape)
        out_ref[0] = jnp.sum(jnp.where(low, prod, 0.0), axis=1, keepdims=True)
        out_ref[1] = jnp.sum(jnp.where(low, 0.0, prod), axis=1, keepdims=True)

    return pl.pallas_call(
        body, grid=(n_pairs, S // ts),
        in_specs=[pl.BlockSpec((ts, LANES), lambda p, i: (i, p)), pl.BlockSpec((ts, LANES), lambda p, i: (i, do0 + p))],
        out_specs=pl.BlockSpec((None, 2, ts, 1), lambda p, i: (p, 0, i, 0)),
        out_shape=jax.ShapeDtypeStruct((n_pairs, 2, S, 1), F32),
        compiler_params=_params(("parallel", "parallel")), name=name)(o, do)


def attn_bwd(q, q0, k, k0, v, v0, do, do0, n_pairs, lse, delta, decay, *, causal, rider=None, name):
    Sq, Sk = q.shape[0], k.shape[0]
    tq = _tile(Sq, ATTN_Q_BLOCK, LANES)
    tk = _tile(Sk, ATTN_K_BLOCK, LANES)
    nq, nk = Sq // tq, Sk // tk
    bias = decay is not None

    r_args, r_in_specs, r_shapes, r_out_specs, aliases, split = _with_rider(
        rider, 7 if bias else 6, 5 if bias else 3, 0)

    def body(*refs):
        ins, outs, _, ride = split(refs)
        q_ref, k_ref, v_ref, do_ref, lse_ref, dl_ref = ins[:6]
        ck_ref = ins[6] if bias else None
        dq_ref, dk_ref, dv_ref = outs[:3]
        dcs_ref, drs_ref = (outs[3], outs[4]) if bias else (None, None)
        pr, j, i = pl.program_id(0), pl.program_id(1), pl.program_id(2)
        if ride is not None:
            pl.when(jnp.logical_and(pr == 0, jnp.logical_and(i == 0, j == 0)))(ride[0])

        @pl.when(i == 0)
        def _():
            dk_ref[...] = jnp.zeros_like(dk_ref)
            dv_ref[...] = jnp.zeros_like(dv_ref)
            if bias:
                dcs_ref[...] = jnp.zeros_like(dcs_ref)

        rows = pl.ds(pl.multiple_of(i * tq, tq), tq)

        def compute(masked):
            qv, kv, vv, dov = q_ref[...], k_ref[...], v_ref[...].astype(BF16), do_ref[...].astype(BF16)
            low, low_k = _low_half(qv.shape), _low_half(kv.shape)
            dq_part, row_parts = None, []
            for a in range(2):
                qa, ka, doa = _one_head(qv, low, a), _one_head(kv, low_k, a), _one_head(dov, low, a)
                s = _dot(qa, kv, 1, 1)
                if bias:
                    s = s - ck_ref[a]
                p = jnp.exp(s - lse_ref[a])
                if masked:
                    row = i * tq + lax.broadcasted_iota(jnp.int32, (tq, tk), 0)
                    col = j * tk + lax.broadcasted_iota(jnp.int32, (tq, tk), 1)
                    p = jnp.where(col <= row, p, 0.0)
                dv_ref[...] += _dot(p.astype(BF16), doa, 0, 0)
                dp = _dot(doa, vv, 1, 1)
                ds = p * (dp - dl_ref[a])
                dsb = ds.astype(BF16)
                dk_ref[...] += _dot(dsb, qa, 0, 0)
                if bias:
                    dcs_ref[a] += jnp.sum(ds, axis=0, keepdims=True)
                    lanes = ds[:, :LANES]
                    for c in range(1, tk // LANES):
                        lanes = lanes + ds[:, c * LANES:(c + 1) * LANES]
                    row_parts.append(lanes)
                part = _dot(dsb, ka, 1, 0) * QK_SCALE
                dq_part = part if dq_part is None else dq_part + part

            @pl.when(j == 0)
            def _():
                dq_ref[rows, :] = dq_part
                for a, rp in enumerate(row_parts):
                    drs_ref[a, rows, :] = rp

            @pl.when(j > 0)
            def _():
                dq_ref[rows, :] += dq_part
                for a, rp in enumerate(row_parts):
                    drs_ref[a, rows, :] += rp

        if causal:
            live = j * tk <= i * tq + (tq - 1)
            crosses = j * tk + (tk - 1) > i * tq
            pl.when(jnp.logical_and(live, crosses))(functools.partial(compute, True))
            pl.when(jnp.logical_and(live, jnp.logical_not(crosses)))(functools.partial(compute, False))
        else:
            compute(False)

        if ride is not None:
            pl.when(jnp.logical_and(pr == n_pairs - 1, jnp.logical_and(i == nq - 1, j == nk - 1)))(ride[1])

    def q_blk(j, i):
        return jnp.maximum(i, (j * tk) // tq) if causal else i

    col1 = pl.BlockSpec((None, 2, tq, 1), lambda p, j, i: (p, 0, q_blk(j, i), 0))
    in_specs = [pl.BlockSpec((tq, LANES), lambda p, j, i: (q_blk(j, i), q0 + p)),
                pl.BlockSpec((tk, LANES), lambda p, j, i: (j, k0 + p)),
                pl.BlockSpec((tk, LANES), lambda p, j, i: (j, v0 + p)),
                pl.BlockSpec((tq, LANES), lambda p, j, i: (q_blk(j, i), do0 + p)), col1, col1]
    args = [q, k, v, do, lse, delta]
    kout = pl.BlockSpec((tk, LANES), lambda p, j, i: (j, p))
    out_specs = [pl.BlockSpec((Sq, LANES), lambda p, j, i: (0, p)), kout, kout]
    out_shape = [jax.ShapeDtypeStruct((Sq, n_pairs * LANES), F32), jax.ShapeDtypeStruct((Sk, n_pairs * LANES), F32),
                 jax.ShapeDtypeStruct((Sk, n_pairs * LANES), F32)]
    if bias:
        in_specs.append(pl.BlockSpec((None, 2, 1, tk), lambda p, j, i: (p, 0, 0, j)))
        args.append(decay)
        out_specs += [pl.BlockSpec((None, 2, 1, tk), lambda p, j, i: (p, 0, 0, j)),
                      pl.BlockSpec((None, 2, Sq, LANES), lambda p, j, i: (p, 0, 0, 0))]
        out_shape += [jax.ShapeDtypeStruct((n_pairs, 2, 1, Sk), F32),
                      jax.ShapeDtypeStruct((n_pairs, 2, Sq, LANES), F32)]
    n_own = len(out_shape)
    out = pl.pallas_call(
        body, grid=(n_pairs, nk, nq), in_specs=in_specs + r_in_specs, out_specs=out_specs + r_out_specs,
        out_shape=out_shape + r_shapes, scratch_shapes=list(rider.sems) if rider else [],
        input_output_aliases=aliases,
        compiler_params=_params(("arbitrary",) * 3 if rider else ("parallel", "arbitrary", "arbitrary")),
        name=name)(*args, *r_args)
    return tuple(out[:n_own]), out[n_own:]


def _tril_mask(n):
    r = lax.broadcasted_iota(jnp.int32, (n, n), 0)
    c = lax.broadcasted_iota(jnp.int32, (n, n), 1)
    return c <= r


def _gmlp_operands(v_gain, w_s, b_s):
    G = w_s.shape[0]
    return (v_gain.reshape(G // 2, 1, LANES), w_s.reshape(G // 2, 2, CHUNK, CHUNK), b_s.reshape(G // 2, 2, CHUNK, 1))


def _gmlp_gate(wt, vh, b_ref, low):
    gate = _dot(wt[0], _one_head(vh, low, 0), 1, 0) + _dot(wt[1], _one_head(vh, low, 1), 1, 0)
    return gate + jnp.where(low, b_ref[0], b_ref[1])


def gmlp_fwd(proj, v0, n_pairs, vg, w, b, *, name):
    S = proj.shape[0]
    ts = _tile(S, 1024, CHUNK)

    def body(up_ref, vp_ref, vg_ref, w_ref, b_ref, o_ref):
        mask = _tril_mask(CHUNK)
        wt = [jnp.where(mask, w_ref[a], 0.0).astype(BF16) for a in range(2)]
        low = _low_half((CHUNK, LANES))
        for c in range(ts // CHUNK):
            sl = pl.ds(c * CHUNK, CHUNK)
            vz = _gelu(vp_ref[sl, :])
            r = lax.rsqrt(_half_sums(vz * vz, low) * (1.0 / HEAD_DIM) + EPS)
            vh = (vz * r * vg_ref[...]).astype(BF16)
            o_ref[sl, :] = _gelu(up_ref[sl, :]) * _gmlp_gate(wt, vh, b_ref, low)

    return pl.pallas_call(
        body, grid=(n_pairs, S // ts),
        in_specs=[pl.BlockSpec((ts, LANES), lambda p, i: (i, p)), pl.BlockSpec((ts, LANES), lambda p, i: (i, v0 + p)),
                  pl.BlockSpec((None, 1, LANES), lambda p, i: (p, 0, 0)),
                  pl.BlockSpec((None, 2, CHUNK, CHUNK), lambda p, i: (p, 0, 0, 0)),
                  pl.BlockSpec((None, 2, CHUNK, 1), lambda p, i: (p, 0, 0, 0))],
        out_specs=pl.BlockSpec((ts, LANES), lambda p, i: (i, p)),
        out_shape=jax.ShapeDtypeStruct((S, n_pairs * LANES), F32),
        compiler_params=_params(("parallel", "parallel")), name=name)(proj, proj, vg, w, b)


def gmlp_bwd(proj, v0, n_pairs, vg, w, wT, b, do, *, name):
    S = proj.shape[0]
    ts = _tile(S, 1024, CHUNK)

    def body(up_ref, vp_ref, vg_ref, w_ref, wT_ref, b_ref, do_ref, dup_ref, dvp_ref, dw_ref, db_ref, dvg_ref):
        i = pl.program_id(1)

        @pl.when(i == 0)
        def _():
            dw_ref[...] = jnp.zeros_like(dw_ref)
            db_ref[...] = jnp.zeros_like(db_ref)
            dvg_ref[...] = jnp.zeros_like(dvg_ref)

        mask = _tril_mask(CHUNK)
        wt = [jnp.where(mask, w_ref[a], 0.0).astype(BF16) for a in range(2)]
        wtT = [jnp.where(mask.T, wT_ref[a], 0.0).astype(BF16) for a in range(2)]
        low = _low_half((CHUNK, LANES))
        vgain = vg_ref[...]
        for c in range(ts // CHUNK):
            sl = pl.ds(c * CHUNK, CHUNK)
            u_pre, v_pre, dout = up_ref[sl, :], vp_ref[sl, :], do_ref[sl, :]
            vz = _gelu(v_pre)
            r = lax.rsqrt(_half_sums(vz * vz, low) * (1.0 / HEAD_DIM) + EPS)
            vh = (vz * r * vgain).astype(BF16)
            gate = _gmlp_gate(wt, vh, b_ref, low)
            dgate = dout * _gelu(u_pre)
            dup_ref[sl, :] = dout * gate * _gelu_grad(u_pre)
            dvh = None
            for a in range(2):
                dga = _one_head(dgate, low, a)
                dgb = dga.astype(BF16)
                dw_ref[a] += jnp.where(mask, _dot(dgb, vh, 1, 1), 0.0)
                db_ref[a] += jnp.sum(dga, axis=1, keepdims=True)
                part = _dot(wtT[a], dgb, 1, 0)
                dvh = part if dvh is None else dvh + part
            dvg_ref[...] += jnp.sum(dvh * vz * r, axis=0, keepdims=True)
            t = dvh * vgain
            dvz = r * t - vz * (r * r * r) * (_half_sums(vz * t, low) * (1.0 / HEAD_DIM))
            dvp_ref[sl, :] = dvz * _gelu_grad(v_pre)

    ublk = pl.BlockSpec((ts, LANES), lambda p, i: (i, p))
    wblk = pl.BlockSpec((None, 2, CHUNK, CHUNK), lambda p, i: (p, 0, 0, 0))
    bblk = pl.BlockSpec((None, 2, CHUNK, 1), lambda p, i: (p, 0, 0, 0))
    gblk = pl.BlockSpec((None, 1, LANES), lambda p, i: (p, 0, 0))
    return pl.pallas_call(
        body, grid=(n_pairs, S // ts),
        in_specs=[ublk, pl.BlockSpec((ts, LANES), lambda p, i: (i, v0 + p)), gblk, wblk, wblk, bblk, ublk],
        out_specs=[ublk, ublk, wblk, bblk, gblk],
        out_shape=[jax.ShapeDtypeStruct((S, n_pairs * LANES), F32), jax.ShapeDtypeStruct((S, n_pairs * LANES), F32),
                   jax.ShapeDtypeStruct((n_pairs, 2, CHUNK, CHUNK), F32), jax.ShapeDtypeStruct((n_pairs, 2, CHUNK, 1), F32),
                   jax.ShapeDtypeStruct((n_pairs, 1, LANES), F32)],
        compiler_params=_params(("parallel", "arbitrary")), name=name)(proj, proj, vg, w, wT, b, do)


def loss_head(y, target, *, name):
    S, D = y.shape
    ts = _tile(S, 512, 8)

    def body(y_ref, t_ref, dy_ref, loss_ref):
        i = pl.program_id(0)
        e = y_ref[...] - t_ref[...]
        dy_ref[...] = e * (1.0 / D)
        part = jnp.sum(jnp.sum(e * e, axis=1, keepdims=True), axis=0, keepdims=True) * (0.5 / D)

        @pl.when(i == 0)
        def _():
            loss_ref[...] = part

        @pl.when(i > 0)
        def _():
            loss_ref[...] += part

    row = pl.BlockSpec((ts, D), lambda i: (i, 0))
    return pl.pallas_call(
        body, grid=(S // ts,), in_specs=[row, row],
        out_specs=[row, pl.BlockSpec((1, 1), lambda i: (0, 0))],
        out_shape=[jax.ShapeDtypeStruct((S, D), F32), jax.ShapeDtypeStruct((1, 1), F32)],
        compiler_params=_params(("arbitrary",)), name=name)(y, target)


def adamw(w, g, m, v, *, name):
    shape = w.shape
    C = shape[-1]
    R = w.size // C
    tr = _tile(R, max(8, (256 * 1024) // C // 8 * 8), 8)

    def body(w_ref, g_ref, m_ref, v_ref, d_ref, nm_ref, nv_ref):
        gv = g_ref[...]
        nm = ADAM_B1 * m_ref[...] + (1.0 - ADAM_B1) * gv
        nv = ADAM_B2 * v_ref[...] + (1.0 - ADAM_B2) * (gv * gv)
        m_hat = nm / (1.0 - ADAM_B1 ** ADAM_STEP)
        v_hat = nv / (1.0 - ADAM_B2 ** ADAM_STEP)
        d_ref[...] = -ADAM_LR * (m_hat / (jnp.sqrt(v_hat) + ADAM_EPS) + ADAM_WD * w_ref[...])
        nm_ref[...] = nm
        nv_ref[...] = nv

    blk = pl.BlockSpec((tr, C), lambda i: (i, 0))
    out = pl.pallas_call(
        body, grid=(R // tr,), in_specs=[blk] * 4, out_specs=[blk] * 3,
        out_shape=[jax.ShapeDtypeStruct((R, C), F32)] * 3,
        compiler_params=_params(("parallel",)), name=name)(*(a.reshape(R, C) for a in (w, g, m, v)))
    return tuple(o.reshape(shape) for o in out)


def pair_sum(p, landed, half, *, name):
    n, R, C = landed.shape
    tr = _tile(R, 256, 16)
    nr = R // tr

    def body(half_ref, p_ref, l_ref, o_ref):
        o_ref[...] = (p_ref[...] + l_ref[...]).astype(BF16)

    return pl.pallas_call(
        body,
        grid_spec=pltpu.PrefetchScalarGridSpec(
            num_scalar_prefetch=1, grid=(n, nr),
            in_specs=[pl.BlockSpec((None, tr, C), lambda k, r, half_ref: (k, half_ref[0] * nr + r, 0)),
                      pl.BlockSpec((None, tr, C), lambda k, r, half_ref: (k, r, 0))],
            out_specs=pl.BlockSpec((None, tr, C), lambda k, r, half_ref: (k, r, 0))),
        out_shape=jax.ShapeDtypeStruct((n, R, C), BF16),
        compiler_params=_params(("parallel", "parallel")), name=name)(half, p, landed)


def chip_sum(own, landed, chip, *, name):
    n, R, C = own.shape
    tr = _tile(R, 256, 16)

    def body(chip_ref, own_ref, *rest):
        l_refs, o_ref = rest[:n], rest[n]
        me = chip_ref[0]
        acc = None
        for d in range(n):
            term = jnp.where(me == d, own_ref[...], l_refs[d][...]).astype(F32)
            acc = term if acc is None else acc + term
        o_ref[...] = acc

    def landed_spec(d):
        return pl.BlockSpec((None, tr, C), lambda r, chip_ref: (jnp.where(chip_ref[0] == d, (d + 1) % n, d), r, 0))

    return pl.pallas_call(
        body,
        grid_spec=pltpu.PrefetchScalarGridSpec(
            num_scalar_prefetch=1, grid=(R // tr,),
            in_specs=[pl.BlockSpec((None, tr, C), lambda r, chip_ref: (chip_ref[0], r, 0))]
            + [landed_spec(d) for d in range(n)],
            out_specs=pl.BlockSpec((tr, C), lambda r, chip_ref: (r, 0))),
        out_shape=jax.ShapeDtypeStruct((R, C), F32),
        compiler_params=_params(("parallel",)), name=name)(chip, own, *([landed] * n))


def ordered_sum(parts, *, name):
    n, R, C = parts.shape
    tr = _tile(R, 256, 16)

    def body(p_ref, o_ref):
        acc = p_ref[0].astype(F32)
        for d in range(1, n):
            acc = acc + p_ref[d].astype(F32)
        o_ref[...] = acc

    return pl.pallas_call(
        body, grid=(R // tr,), in_specs=[pl.BlockSpec((n, tr, C), lambda r: (0, r, 0))],
        out_specs=pl.BlockSpec((tr, C), lambda r: (r, 0)),
        out_shape=jax.ShapeDtypeStruct((R, C), F32),
        compiler_params=_params(("parallel",)), name=name)(parts)


_ANY = pl.BlockSpec(memory_space=pl.ANY)


def _position():
    return lax.axis_index("x"), lax.axis_index("y"), lax.axis_index("c")


def _remote(src, dst, send_sem, recv_sem, device):
    return pltpu.make_async_remote_copy(src_ref=src, dst_ref=dst, send_sem=send_sem, recv_sem=recv_sem,
                                        device_id=device, device_id_type=MESH_ID)


def _small_all_gather(s_ref, all_ref, send_sems, recv_sems, x, y, c):
    me = 4 * x + 2 * y + c
    copies = []
    for f in range(1, 8):
        peer = ((1 - x) if f & 4 else x, (1 - y) if f & 2 else y, (1 - c) if f & 1 else c)
        cp = _remote(s_ref, all_ref.at[me], send_sems.at[f - 1], recv_sems.at[f - 1], peer)
        cp.start()
        copies.append((cp, peer, f - 1))

    def finish():
        for cp, peer, s in copies:
            slot = all_ref.at[4 * peer[0] + 2 * peer[1] + peer[2]]
            _remote(slot, slot, send_sems.at[s], recv_sems.at[s], peer).wait_recv()
        for cp, _, _ in copies:
            cp.wait_send()

    return finish


def _core_rows(ref, core):
    h = ref.shape[1] // 2
    return pl.ds(core * h, h)


def _gather_plan(outs, send_sems, recv_sems):
    n = len(outs)
    x, y, c = _position()
    k = 2 * x + y
    sibling = (x, y, 1 - c)
    chips = [(1 - x, y), (x, 1 - y), (1 - x, 1 - y)]

    def first():
        return [_remote(outs[w].at[k, _core_rows(outs[w], c)], outs[w].at[k, _core_rows(outs[w], c)],
                        send_sems.at[w, j], recv_sems.at[w, j], (px, py, c))
                for j, (px, py) in enumerate(chips) for w in range(n)]

    def start():
        for cp in first():
            cp.start()

    def finish():
        passed = []
        for j, (px, py) in enumerate(chips):
            for w in range(n):
                slot = outs[w].at[2 * px + py, _core_rows(outs[w], c)]
                _remote(slot, slot, send_sems.at[w, j], recv_sems.at[w, j], (px, py, c)).wait_recv()
                cp = _remote(slot, slot, send_sems.at[w, 3 + j], recv_sems.at[w, 3 + j], sibling)
                cp.start()
                passed.append(cp)
        for j, (px, py) in enumerate(chips):
            for w in range(n):
                slot = outs[w].at[2 * px + py, _core_rows(outs[w], 1 - c)]
                _remote(slot, slot, send_sems.at[w, 3 + j], recv_sems.at[w, 3 + j], sibling).wait_recv()
        for cp in first() + passed:
            cp.wait_send()

    return start, finish


def _gather_sems(n):
    return (pltpu.SemaphoreType.DMA((n, 6)), pltpu.SemaphoreType.DMA((n, 6)))


def gather_rider(slabs):
    return Rider(tuple(slabs), tuple(jax.ShapeDtypeStruct(a.shape, a.dtype) for a in slabs),
                 {i: i for i in range(len(slabs))}, _gather_sems(len(slabs)),
                 lambda ins, outs, sems: _gather_plan(outs, sems[0], sems[1]))


def gather_weights(slabs, small_slab, *, name):
    n = len(slabs)

    def body(*refs):
        outs, all_ref = refs[n + 1:2 * n + 1], refs[2 * n + 1]
        send_sems, recv_sems, s_send, s_recv = refs[2 * n + 2:]
        x, y, c = _position()
        finish_small = _small_all_gather(all_ref.at[4 * x + 2 * y + c], all_ref, s_send, s_recv, x, y, c)
        start, finish = _gather_plan(outs, send_sems, recv_sems)
        start()
        finish()
        finish_small()

    args = list(slabs) + [small_slab]
    out = pl.pallas_call(
        body, in_specs=[_ANY] * (n + 1), out_specs=[_ANY] * (n + 1),
        out_shape=[jax.ShapeDtypeStruct(a.shape, a.dtype) for a in args],
        input_output_aliases={i: i for i in range(n + 1)},
        scratch_shapes=list(_gather_sems(n)) + [pltpu.SemaphoreType.DMA((7,)), pltpu.SemaphoreType.DMA((7,))],
        name=name)(*args)
    return out[:n], out[n]


def exchange_with_sibling(parts, small_slab, *, name):
    n = len(parts)
    has_small = small_slab is not None
    n_arg = n + (1 if has_small else 0)

    def body(*refs):
        p_refs = refs[:n]
        lands = refs[n_arg:n_arg + n]
        send_sems, recv_sems = refs[2 * n_arg], refs[2 * n_arg + 1]
        x, y, c = _position()
        sibling = (x, y, 1 - c)
        if has_small:
            all_ref = refs[n_arg + n]
            finish_small = _small_all_gather(all_ref.at[4 * x + 2 * y + c], all_ref, refs[2 * n_arg + 2],
                                             refs[2 * n_arg + 3], x, y, c)
        sends = []
        for w in range(n):
            for d in range(4):
                cp = _remote(p_refs[w].at[d, _core_rows(p_refs[w], 1 - c)], lands[w].at[d],
                             send_sems.at[w, d], recv_sems.at[w, d], sibling)
                cp.start()
                sends.append(cp)
        for cp in sends:
            cp.wait_recv()
        for cp in sends:
            cp.wait_send()
        if has_small:
            finish_small()

    small_args = [small_slab] if has_small else []
    out = pl.pallas_call(
        body, in_specs=[_ANY] * n_arg, out_specs=[_ANY] * n_arg,
        out_shape=[jax.ShapeDtypeStruct((4, p.shape[1] // 2, p.shape[2]), p.dtype) for p in parts]
        + [jax.ShapeDtypeStruct(s.shape, s.dtype) for s in small_args],
        input_output_aliases={n: n} if has_small else {},
        scratch_shapes=[pltpu.SemaphoreType.DMA((n, 4)), pltpu.SemaphoreType.DMA((n, 4))]
        + ([pltpu.SemaphoreType.DMA((7,)), pltpu.SemaphoreType.DMA((7,))] if has_small else []),
        name=name)(*parts, *small_args)
    return out[:n], (out[n] if has_small else None)


def _scatter_plan(q_refs, outs, send_sems, recv_sems):
    n = len(q_refs)
    x, y, c = _position()
    k = 2 * x + y
    chips = [(1 - x, y), (x, 1 - y), (1 - x, 1 - y)]

    def sends():
        return [_remote(q_refs[w].at[2 * px + py], outs[w].at[k], send_sems.at[w, j], recv_sems.at[w, j], (px, py, c))
                for j, (px, py) in enumerate(chips) for w in range(n)]

    def start():
        for cp in sends():
            cp.start()

    def finish():
        for j, (px, py) in enumerate(chips):
            for w in range(n):
                slot = outs[w].at[2 * px + py]
                _remote(slot, slot, send_sems.at[w, j], recv_sems.at[w, j], (px, py, c)).wait_recv()
        for cp in sends():
            cp.wait_send()

    return start, finish


def _scatter_sems(n):
    return (pltpu.SemaphoreType.DMA((n, 3)), pltpu.SemaphoreType.DMA((n, 3)))


def scatter_rider(parts):
    return Rider(tuple(parts), tuple(jax.ShapeDtypeStruct(q.shape, q.dtype) for q in parts), {},
                 _scatter_sems(len(parts)), lambda ins, outs, sems: _scatter_plan(ins, outs, sems[0], sems[1]))


def scatter_to_chips(parts, *, name):
    n = len(parts)

    def body(*refs):
        start, finish = _scatter_plan(refs[:n], refs[n:2 * n], refs[2 * n], refs[2 * n + 1])
        start()
        finish()

    return pl.pallas_call(
        body, in_specs=[_ANY] * n, out_specs=[_ANY] * n,
        out_shape=[jax.ShapeDtypeStruct(q.shape, q.dtype) for q in parts],
        scratch_shapes=list(_scatter_sems(n)), name=name)(*parts)


def share_with_sibling(parts, *, name):
    n = len(parts)

    def body(*refs):
        r_refs, outs = refs[:n], refs[n:2 * n]
        send_sems, recv_sems = refs[2 * n:]
        x, y, c = _position()
        sends = []
        for w in range(n):
            cp = _remote(r_refs[w], outs[w], send_sems.at[w], recv_sems.at[w], (x, y, 1 - c))
            cp.start()
            sends.append(cp)
        for cp in sends:
            cp.wait_recv()
        for cp in sends:
            cp.wait_send()

    return pl.pallas_call(
        body, in_specs=[_ANY] * n, out_specs=[_ANY] * n,
        out_shape=[jax.ShapeDtypeStruct(r.shape, r.dtype) for r in parts],
        scratch_shapes=[pltpu.SemaphoreType.DMA((n,)), pltpu.SemaphoreType.DMA((n,))],
        name=name)(*parts)


def _cols_to_chips(full):
    *lead, R, C4 = full.shape
    t = full.reshape(*lead, R, 4, C4 // 4)
    return jnp.moveaxis(t, -2, 0)


def _chips_to_cols(sh):
    t = jnp.moveaxis(sh, 0, -2)
    return t.reshape(*t.shape[:-2], t.shape[-2] * t.shape[-1])


def _slot_in_empty(own, index, n):
    return lax.dynamic_update_slice(lax.empty((n,) + own.shape, own.dtype), own[None], (index,) + (0,) * own.ndim)


def _fold_pair(dg):
    return dg[0, :HEAD_DIM] + dg[0, HEAD_DIM:]


def _ffn_fwd(x, g, w_in_slab, w_out, tag):
    h = rms_fwd(x, g, name=f"{tag}_rms")
    a, b, act = swiglu_fwd(h, w_in_slab, name=f"{tag}_in")
    y = matmul(act, w_out, res=x, scale=0.5, tm=1024, tn=512, tk=w_out.shape[0], name=f"{tag}_out")
    return y, (x, h, a, b, act)


def _ffn_bwd(dy, saved, g, w_in_slab, w_out, tag):
    x, h, a, b, act = saved
    da, db = swiglu_bwd(dy, w_out, a, b, name=f"{tag}_dact")
    dw_out = grad_rows(act, dy, scale=0.5, name=f"{tag}_dwout")
    dw_in = grad_cols(h, da, db, name=f"{tag}_dwin")
    dx, dg = ffn_dh(da, db, w_in_slab, x, g, dy, name=f"{tag}_dh")
    return dx, dg[0], dw_in, dw_out


MEM_PAIRS = MEM_WIDTH // LANES


def _mem_attn_fwd(proj, mq0, mem_n, w_kv, g_q, g_k, tag):
    qh = pairnorm_fwd(proj, mq0, MEM_PAIRS, g_q, scale=QK_SCALE,name=f"{tag}_qnorm")
    kv = matmul(mem_n, w_kv, tm=256, tn=512, tk=1024, name=f"{tag}_kv")
    kh = pairnorm_fwd(kv, 0, MEM_PAIRS, g_k, name=f"{tag}_knorm")
    o, lse, _ = attn_fwd(qh, 0, kh, 0, kv, MEM_PAIRS, MEM_PAIRS, None, causal=False, name=f"{tag}_attn")
    return o, (qh, kv, kh, o, lse)


def _mem_attn_bwd(dmix, do0, proj, mq0, saved, mem_n, g_q, g_k, tag):
    qh, kv, kh, o, lse = saved
    delta = attn_delta(o, dmix, do0, MEM_PAIRS, name=f"{tag}_delta")
    (dqh, dkh, dv), _ = attn_bwd(qh, 0, kh, 0, kv, MEM_PAIRS, dmix, do0, MEM_PAIRS, lse, delta, None,
                                 causal=False, name=f"{tag}_dattn")
    dq_pre, dgq = pairnorm_bwd(proj, mq0, MEM_PAIRS, dqh, g_q, name=f"{tag}_dqnorm")
    dk_pre, dgk = pairnorm_bwd(kv, 0, MEM_PAIRS, dkh, g_k, name=f"{tag}_dknorm")
    dkv = jnp.concatenate([dk_pre, dv], axis=1)
    dw_kv = grad_rows(mem_n, dkv, name=f"{tag}_dwkv")
    return dq_pre, _fold_pair(dgq), _fold_pair(dgk), dw_kv, dkv


def _per_head_lanes(x, H):
    return jnp.pad(x.reshape(H, -1).T, ((0, 0), (0, LANES - H)))


def _fox_fwd(proj, b_f, g_q, g_k, tok, rider, tag):
    H, P = tok // HEAD_DIM, tok // LANES
    bias = jnp.pad(b_f.reshape(1, H), ((0, 0), (0, LANES - H)))
    qh = pairnorm_fwd(proj, 0, P, g_q, scale=QK_SCALE,name=f"{tag}_qnorm")
    kh = pairnorm_fwd(proj, P, P, g_k, name=f"{tag}_knorm")
    c = fgate_fwd(proj, 3 * P + MEM_PAIRS, bias, name=f"{tag}_fgate")
    decay = c[:, :H].T.reshape(P, 2, 1, c.shape[0])
    o, lse, rode = attn_fwd(qh, 0, kh, 0, proj, 2 * P, P, decay, causal=True, rider=rider, name=f"{tag}_attn")
    return o, (qh, kh, bias, decay, o, lse), rode


def _fox_bwd(dmix, proj, saved, g_q, g_k, tok, rider, tag):
    qh, kh, bias, decay, o, lse = saved
    H, P = tok // HEAD_DIM, tok // LANES
    delta = attn_delta(o, dmix, 0, P, name=f"{tag}_delta")
    (dqh, dkh, dv, dcs, drs), rode = attn_bwd(qh, 0, kh, 0, proj, 2 * P, dmix, 0, P, lse, delta, decay, causal=True,
                                              rider=rider, name=f"{tag}_dattn")
    dq_pre, dgq = pairnorm_bwd(proj, 0, P, dqh, g_q, name=f"{tag}_dqnorm")
    dk_pre, dgk = pairnorm_bwd(proj, P, P, dkh, g_k, name=f"{tag}_dknorm")
    dz, dbias = fgate_bwd(proj, 3 * P + MEM_PAIRS, bias, drs, _per_head_lanes(dcs, H), name=f"{tag}_dfgate")
    dqkv = jnp.concatenate([dq_pre, dk_pre, dv], axis=1)
    return dqkv, dz, dbias[0, :H], _fold_pair(dgq), _fold_pair(dgk), rode


def local_step(x, mem, target, W, comm=None):
    S, D = x.shape
    tok = D - MEM_WIDTH
    P = tok // LANES
    depth = W["norm_ffn1"].shape[0]
    mem_n = rms_fwd(mem, W["mem_norm"], name="mem_rms")
    saved = []
    for i in range(depth):
        kind, j = i % 2, i // 2
        t = f"l{i}"
        x1, s1 = _ffn_fwd(x, W["norm_ffn1"][i], W["ffn1_w_in"][i], W["ffn1_w_out"][i], f"{t}_ffn1")
        h = rms_fwd(x1, W["norm_mix"][i], name=f"{t}_mix_rms")
        w_mix = W["fox_w_in"][j] if kind == 0 else W["gmlp_w_in"][j]
        proj = matmul(h, w_mix, tm=1024, tn=896, tk=D, name=f"{t}_mix_in")
        if kind == 0:
            rider = comm.late_weights_rider() if (comm is not None and i == 0) else None
            o_tok, s_tok, rode = _fox_fwd(proj, W["fox_b_f"][j], W["fox_q_norm"][j], W["fox_k_norm"][j], tok, rider,
                                          f"{t}_fox")
            if rider is not None:
                comm.accept_late_weights(W, rode)
            mq0 = 3 * P
        else:
            vg, ws, bs = _gmlp_operands(W["gmlp_v_norm"][j], W["gmlp_w_s"][j], W["gmlp_b_s"][j])
            o_tok = gmlp_fwd(proj, P, P, vg, ws, bs, name=f"{t}_gmlp")
            s_tok = None
            mq0 = 2 * P
        o_mem, s_mem = _mem_attn_fwd(proj, mq0, mem_n, W["mem_w_kv"][i], W["mem_q_norm"][i], W["mem_k_norm"][i],
                                     f"{t}_mem")
        mix = jnp.concatenate([o_tok, o_mem], axis=1).astype(BF16)
        x2 = matmul(mix, W["w_out"][i], res=x1, tm=1024, tn=512, tk=D, name=f"{t}_mix_out")
        x3, s3 = _ffn_fwd(x2, W["norm_ffn2"][i], W["ffn2_w_in"][i], W["ffn2_w_out"][i], f"{t}_ffn2")
        saved.append((s1, x1, h, proj, mq0, s_tok, s_mem, mix, s3))
        x = x3

    dx, loss = loss_head(x, target, name="loss_head")

    G = {k: [None] * depth for k in ("norm_ffn1", "norm_mix", "norm_ffn2", "mem_q_norm", "mem_k_norm", "ffn1_w_in",
                                     "ffn1_w_out", "ffn2_w_in", "ffn2_w_out", "w_out", "mem_w_kv")}
    n_fox, n_gmlp = (depth + 1) // 2, depth // 2
    for k in ("fox_w_in", "fox_b_f", "fox_q_norm", "fox_k_norm"):
        G[k] = [None] * n_fox
    for k in ("gmlp_w_in", "gmlp_v_norm", "gmlp_w_s", "gmlp_b_s"):
        G[k] = [None] * n_gmlp
    dkv_all = [None] * depth
    for i in reversed(range(depth)):
        kind, j = i % 2, i // 2
        t = f"l{i}"
        s1, x1, h, proj, mq0, s_tok, s_mem, mix, s3 = saved[i]
        dx, G["norm_ffn2"][i], G["ffn2_w_in"][i], G["ffn2_w_out"][i] = _ffn_bwd(
            dx, s3, W["norm_ffn2"][i], W["ffn2_w_in"][i], W["ffn2_w_out"][i], f"{t}_ffn2")
        dmix = matmul(dx, W["w_out"][i], tb=True, tm=1024, tn=1024, tk=D, name=f"{t}_dmix")
        G["w_out"][i] = grad_rows(mix, dx, name=f"{t}_dwmixout")
        dmq, G["mem_q_norm"][i], G["mem_k_norm"][i], G["mem_w_kv"][i], dkv_all[i] = _mem_attn_bwd(
            dmix, P, proj, mq0, s_mem, mem_n, W["mem_q_norm"][i], W["mem_k_norm"][i], f"{t}_mem")
        if kind == 0:
            rider = comm.early_grads_rider(G) if (comm is not None and i == 0) else None
            dqkv, dz, G["fox_b_f"][j], G["fox_q_norm"][j], G["fox_k_norm"][j], rode = _fox_bwd(
                dmix, proj, s_tok, W["fox_q_norm"][j], W["fox_k_norm"][j], tok, rider, f"{t}_fox")
            if rider is not None:
                comm.accept_early_grads(rode)
            dproj = jnp.concatenate([dqkv, dmq, dz], axis=1).astype(BF16)
            w_mix, wkey = W["fox_w_in"][j], "fox_w_in"
        else:
            vg, ws, bs = _gmlp_operands(W["gmlp_v_norm"][j], W["gmlp_w_s"][j], W["gmlp_b_s"][j])
            dup, dvp, dws, dbs, dvg = gmlp_bwd(proj, P, P, vg, ws, jnp.swapaxes(ws, 2, 3), bs, dmix,
                                               name=f"{t}_dgmlp")
            G["gmlp_w_s"][j] = dws.reshape(W["gmlp_w_s"][j].shape)
            G["gmlp_b_s"][j] = dbs.reshape(W["gmlp_b_s"][j].shape)
            G["gmlp_v_norm"][j] = dvg.reshape(-1)
            dproj = jnp.concatenate([dup, dvp, dmq], axis=1).astype(BF16)
            w_mix, wkey = W["gmlp_w_in"][j], "gmlp_w_in"
        G[wkey][j] = matmul(h, dproj, ta=True, tm=1024, tn=896, tk=1024, name=f"{t}_dwmixin")
        dh = matmul(dproj, w_mix, tb=True, tm=1024, tn=1024, tk=896, name=f"{t}_dhmix")
        dx, dgm = rms_bwd(x1, dh, W["norm_mix"][i], dx, name=f"{t}_dmixrms")
        G["norm_mix"][i] = dgm[0]
        dx, G["norm_ffn1"][i], G["ffn1_w_in"][i], G["ffn1_w_out"][i] = _ffn_bwd(
            dx, s1, W["norm_ffn1"][i], W["ffn1_w_in"][i], W["ffn1_w_out"][i], f"{t}_ffn1")
    w_kv_all = jnp.concatenate([W["mem_w_kv"][i] for i in range(depth)], axis=1)
    dmem_n = matmul(jnp.concatenate(dkv_all, axis=1), w_kv_all, tb=True, tm=256, tn=512, tk=1024, name="dmem_n")
    _, dmemg = rms_bwd(mem, dmem_n, W["mem_norm"], None, name="dmem_rms")
    G["mem_norm"] = [dmemg[0]]
    return loss, dx, G


def _fox_cols_to_compute(w, tok):
    H = tok // HEAD_DIM
    qkv, f, mq = w[..., :3 * tok], w[..., 3 * tok:3 * tok + H], w[..., 3 * tok + H:]
    f = jnp.pad(f, [(0, 0)] * (w.ndim - 1) + [(0, LANES - H)])
    return jnp.concatenate([qkv, mq, f], axis=-1)


def _fox_cols_from_compute(w, tok):
    H = tok // HEAD_DIM
    qkv, mq, f = w[..., :3 * tok], w[..., 3 * tok:3 * tok + MEM_WIDTH], w[..., 3 * tok + MEM_WIDTH:3 * tok + MEM_WIDTH + H]
    return jnp.concatenate([qkv, f, mq], axis=-1)


_BIG = ("ffn1_w_in", "ffn1_w_out", "ffn2_w_in", "ffn2_w_out", "w_out", "mem_w_kv", "fox_w_in", "gmlp_w_in")
_SMALL = ("norm_ffn1", "norm_mix", "norm_ffn2", "mem_norm", "mem_q_norm", "mem_k_norm", "fox_b_f", "fox_q_norm",
          "fox_k_norm", "gmlp_v_norm", "gmlp_w_s", "gmlp_b_s")
WEIGHT_ORDER = ("norm_ffn1", "ffn1_w_in", "ffn1_w_out", "norm_mix", "norm_ffn2", "ffn2_w_in", "ffn2_w_out", "w_out",
                "mem_norm", "mem_w_kv", "mem_q_norm", "mem_k_norm", "fox_w_in", "fox_b_f", "fox_q_norm", "fox_k_norm",
                "gmlp_w_in", "gmlp_v_norm", "gmlp_w_s", "gmlp_b_s")


def _small_slab(rows_list, index):
    sizes = [s.shape[0] for s in rows_list]
    n_rows = [-(-n // LANES) for n in sizes]
    small = jnp.concatenate([jnp.pad(s, (0, r * LANES - n)).reshape(r, LANES)
                             for s, n, r in zip(rows_list, sizes, n_rows)], axis=0)
    small = jnp.pad(small, ((0, -small.shape[0] % 64), (0, 0)))
    return _slot_in_empty(small, index, 8), sizes, n_rows


_FIRST_WEIGHTS = (("ffn1_w_in", 0), ("ffn1_w_out", 0), ("fox_w_in", 0))


def _weight_from_slab(name, slab, tok):
    if name in ("ffn1_w_in", "ffn2_w_in"):
        return slab
    if name == "fox_w_in":
        return _fox_cols_to_compute(_chips_to_cols(slab), tok)
    if name == "gmlp_w_in":
        return _chips_to_cols(slab)
    return slab.reshape(4 * slab.shape[1], slab.shape[2])


def _grad_to_slab(name, g, tok):
    if name == "fox_w_in":
        return _cols_to_chips(_fox_cols_from_compute(g, tok))
    if name == "gmlp_w_in":
        return _cols_to_chips(g)
    return g


class _Exchange:
    def __init__(self, shards, tok, chip, core):
        self.tok, self.core = tok, core
        self.half = core.reshape(1).astype(jnp.int32)
        self.chip_id = chip.reshape(1).astype(jnp.int32)
        items = [(k, i) for k in _BIG for i in range(shards[k].shape[0])]
        self.slabs = {it: _slot_in_empty(shards[it[0]][it[1]].astype(BF16), chip, 4) for it in items}
        self.late = [it for it in items if it not in _FIRST_WEIGHTS]
        self.reduced = {}
        self.early = None

    def first_weights(self, small_slab):
        got, small_all = gather_weights([self.slabs[it] for it in _FIRST_WEIGHTS], small_slab, name="gather_first")
        return {it: _weight_from_slab(it[0], s, self.tok) for it, s in zip(_FIRST_WEIGHTS, got)}, small_all

    def late_weights_rider(self):
        return gather_rider([self.slabs[it] for it in self.late])

    def accept_late_weights(self, W, got):
        for (k, i), s in zip(self.late, got):
            W[k][i] = _weight_from_slab(k, s, self.tok)

    def _pair_sums(self, G, items, small_slab, tag):
        parts = [_grad_to_slab(k, G[k][i], self.tok) for k, i in items]
        landed, small_all = exchange_with_sibling(parts, small_slab, name=f"grad_exchange_{tag}")
        pair = [pair_sum(p, l, self.half, name=f"grad_pair_sum_{k}{i}") for (k, i), p, l in zip(items, parts, landed)]
        return pair, small_all

    def early_grads_rider(self, G):
        items = [(k, i) for k in _BIG for i in range(len(G[k])) if G[k][i] is not None]
        pair, _ = self._pair_sums(G, items, None, "early")
        self.early = (items, pair)
        return scatter_rider(pair)

    def accept_early_grads(self, landed):
        items, pair = self.early
        self._chip_sums(items, pair, landed)

    def _chip_sums(self, items, pair, landed):
        for (k, i), q, l in zip(items, pair, landed):
            self.reduced[(k, i)] = chip_sum(q, l, self.chip_id, name=f"grad_chip_sum_{k}{i}")

    def finish_grads(self, G, small_slab):
        items = [(k, i) for k in _BIG for i in range(len(G[k])) if (k, i) not in self.reduced]
        pair, small_all = self._pair_sums(G, items, small_slab, "late")
        self._chip_sums(items, pair, scatter_to_chips(pair, name="grad_scatter_late"))
        order = sorted(self.reduced)
        other = share_with_sibling([self.reduced[it] for it in order], name="grad_share")
        full = {}
        for it, a, b in zip(order, [self.reduced[it] for it in order], other):
            full[it] = jnp.where(self.core == 0, jnp.concatenate([a, b]), jnp.concatenate([b, a]))
        names = sorted({k for k, _ in order})
        return {k: jnp.stack([full[(k, i)] for i in range(len(G[k]))]) for k in names}, small_all


def kernel(x, mem, norm_ffn1, ffn1_w_in, ffn1_w_out, norm_mix, norm_ffn2, ffn2_w_in, ffn2_w_out, w_out, mem_norm, mem_w_kv, mem_q_norm, mem_k_norm, fox_w_in, fox_b_f, fox_q_norm, fox_k_norm, gmlp_w_in, gmlp_v_norm, gmlp_w_s, gmlp_b_s, loss_target, m_norm_ffn1, m_ffn1_w_in, m_ffn1_w_out, m_norm_mix, m_norm_ffn2, m_ffn2_w_in, m_ffn2_w_out, m_w_out, m_mem_norm, m_mem_w_kv, m_mem_q_norm, m_mem_k_norm, m_fox_w_in, m_fox_b_f, m_fox_q_norm, m_fox_k_norm, m_gmlp_w_in, m_gmlp_v_norm, m_gmlp_w_s, m_gmlp_b_s, v_norm_ffn1, v_ffn1_w_in, v_ffn1_w_out, v_norm_mix, v_norm_ffn2, v_ffn2_w_in, v_ffn2_w_out, v_w_out, v_mem_norm, v_mem_w_kv, v_mem_q_norm, v_mem_k_norm, v_fox_w_in, v_fox_b_f, v_fox_q_norm, v_fox_k_norm, v_gmlp_w_in, v_gmlp_v_norm, v_gmlp_w_s, v_gmlp_b_s):
    w = dict(norm_ffn1=norm_ffn1, ffn1_w_in=ffn1_w_in, ffn1_w_out=ffn1_w_out, norm_mix=norm_mix, norm_ffn2=norm_ffn2,
             ffn2_w_in=ffn2_w_in, ffn2_w_out=ffn2_w_out, w_out=w_out, mem_norm=mem_norm, mem_w_kv=mem_w_kv,
             mem_q_norm=mem_q_norm, mem_k_norm=mem_k_norm, fox_w_in=fox_w_in, fox_b_f=fox_b_f, fox_q_norm=fox_q_norm,
             fox_k_norm=fox_k_norm, gmlp_w_in=gmlp_w_in, gmlp_v_norm=gmlp_v_norm, gmlp_w_s=gmlp_w_s, gmlp_b_s=gmlp_b_s)
    m = dict(norm_ffn1=m_norm_ffn1, ffn1_w_in=m_ffn1_w_in, ffn1_w_out=m_ffn1_w_out, norm_mix=m_norm_mix,
             norm_ffn2=m_norm_ffn2, ffn2_w_in=m_ffn2_w_in, ffn2_w_out=m_ffn2_w_out, w_out=m_w_out, mem_norm=m_mem_norm,
             mem_w_kv=m_mem_w_kv, mem_q_norm=m_mem_q_norm, mem_k_norm=m_mem_k_norm, fox_w_in=m_fox_w_in,
             fox_b_f=m_fox_b_f, fox_q_norm=m_fox_q_norm, fox_k_norm=m_fox_k_norm, gmlp_w_in=m_gmlp_w_in,
             gmlp_v_norm=m_gmlp_v_norm, gmlp_w_s=m_gmlp_w_s, gmlp_b_s=m_gmlp_b_s)
    v = dict(norm_ffn1=v_norm_ffn1, ffn1_w_in=v_ffn1_w_in, ffn1_w_out=v_ffn1_w_out, norm_mix=v_norm_mix,
             norm_ffn2=v_norm_ffn2, ffn2_w_in=v_ffn2_w_in, ffn2_w_out=v_ffn2_w_out, w_out=v_w_out, mem_norm=v_mem_norm,
             mem_w_kv=v_mem_w_kv, mem_q_norm=v_mem_q_norm, mem_k_norm=v_mem_k_norm, fox_w_in=v_fox_w_in,
             fox_b_f=v_fox_b_f, fox_q_norm=v_fox_q_norm, fox_k_norm=v_fox_k_norm, gmlp_w_in=v_gmlp_w_in,
             gmlp_v_norm=v_gmlp_v_norm, gmlp_w_s=v_gmlp_w_s, gmlp_b_s=v_gmlp_b_s)
    D = x.shape[-1]
    tok = D - MEM_WIDTH
    xi, yi, ci = _position()
    chip = 2 * xi + yi

    device = 4 * xi + 2 * yi + ci

    comm = _Exchange(w, tok, chip, ci)
    vn = w["gmlp_v_norm"]
    vn_slab, _, _ = _small_slab([vn.reshape(-1)], device)
    first, vn_all = comm.first_weights(vn_slab)
    W = {k: w[k] for k in _SMALL}
    W["gmlp_v_norm"] = _chips_to_cols(vn_all[0::2].reshape(4, -1)[:, :vn.size].reshape((4,) + vn.shape))
    for k in _BIG:
        W[k] = [first.get((k, i)) for i in range(w[k].shape[0])]

    loss, grad_x, g = local_step(x[0], mem[0], loss_target[0], W, comm)

    small_list = [jnp.stack(g[k]).reshape(-1) for k in _SMALL] + [loss.reshape(-1)]
    small, small_sizes, small_rows = _small_slab(small_list, device)
    red, small_all = comm.finish_grads(g, small)
    small_sum = ordered_sum(small_all, name="small_sum")
    off = 0
    for k, n, r in zip(_SMALL, small_sizes, small_rows):
        red[k] = small_sum[off:off + r].reshape(-1)[:n].reshape((-1,) + w[k].shape[1:] if k != "gmlp_v_norm"
                                                                else (w[k].shape[0], -1))
        off += r
    loss_total = small_sum[off, 0]
    vn_cols = w["gmlp_v_norm"].shape[-1]
    red["gmlp_v_norm"] = lax.dynamic_slice_in_dim(red["gmlp_v_norm"], chip * vn_cols, vn_cols, axis=-1)

    deltas, new_m, new_v = {}, {}, {}
    for k in WEIGHT_ORDER:
        wk = w[k] if w[k].ndim > 1 else w[k].reshape(1, -1)
        upd = adamw(wk, red[k].reshape(wk.shape), m[k].reshape(wk.shape), v[k].reshape(wk.shape), name=f"adamw_{k}")
        deltas[k], new_m[k], new_v[k] = (u.reshape(w[k].shape) for u in upd)
    return (loss_total, grad_x[None], *[red[k].reshape(w[k].shape) for k in WEIGHT_ORDER],
            *[deltas[k] for k in WEIGHT_ORDER], *[new_m[k] for k in WEIGHT_ORDER], *[new_v[k] for k in WEIGHT_ORDER])
```

```python
import functools
import math
from typing import Callable, NamedTuple

import jax
import jax.numpy as jnp
from jax import lax
from jax.experimental import pallas as pl
from jax.experimental.pallas import tpu as pltpu

F32 = jnp.float32
BF16 = jnp.bfloat16
EPS = 1e-6
HEAD_DIM = 64
MEM_WIDTH = 256
CHUNK = 128
LANES = 128
NEG = -1e30
VMEM_LIMIT_BYTES = 56 * 1024 * 1024
ATTN_Q_BLOCK = 1024
ATTN_K_BLOCK = 1024
ATTN_ROW_CHUNK = 1024
QK_SCALE = 0.125
MESH_ID = pl.DeviceIdType.MESH

ADAM_LR = 0.001
ADAM_B1 = 0.9
ADAM_B2 = 0.999
ADAM_EPS = 1e-08
ADAM_WD = 0.01
ADAM_STEP = 10


def _tile(n, pref, align):
    t = (min(pref, n) // align) * align
    while t >= align:
        if n % t == 0:
            return t
        t -= align
    return n


def _params(sem):
    return pltpu.CompilerParams(dimension_semantics=sem, vmem_limit_bytes=VMEM_LIMIT_BYTES)


def _dot(a, b, ca, cb):
    return lax.dot_general(a, b, (((ca,), (cb,)), ((), ())), preferred_element_type=F32)


def _sigmoid(x):
    return 1.0 / (1.0 + jnp.exp(-x))


_GELU_C = math.sqrt(2.0 / math.pi)


def _gelu(x):
    return 0.5 * x * (1.0 + jnp.tanh(_GELU_C * (x + 0.044715 * (x * x * x))))


def _gelu_grad(x):
    t = jnp.tanh(_GELU_C * (x + 0.044715 * (x * x * x)))
    return 0.5 * (1.0 + t) + 0.5 * x * (1.0 - t * t) * (_GELU_C * (1.0 + 3.0 * 0.044715 * (x * x)))


def matmul(a, b, *, ta=False, tb=False, out_dtype=F32, scale=None, res=None,
           tm=1024, tn=512, tk=1024, name):
    if ta:
        K, M = a.shape
    else:
        M, K = a.shape
    N = b.shape[0] if tb else b.shape[1]
    tm = _tile(M, tm, LANES if ta else 16)
    tn = _tile(N, tn, LANES)
    tk = _tile(K, tk, LANES)
    nk = K // tk
    a_spec = pl.BlockSpec((tk, tm), lambda i, j, k: (k, i)) if ta else pl.BlockSpec((tm, tk), lambda i, j, k: (i, k))
    b_spec = pl.BlockSpec((tn, tk), lambda i, j, k: (j, k)) if tb else pl.BlockSpec((tk, tn), lambda i, j, k: (k, j))
    o_spec = pl.BlockSpec((tm, tn), lambda i, j, k: (i, j))
    ca, cb = (0 if ta else 1), (1 if tb else 0)
    has_res = res is not None

    def body(*refs):
        a_ref, b_ref = refs[0], refs[1]
        res_ref = refs[2] if has_res else None
        o_ref = refs[3] if has_res else refs[2]
        acc_ref = refs[-1]
        k = pl.program_id(2)
        prod = _dot(a_ref[...].astype(BF16), b_ref[...].astype(BF16), ca, cb)

        def finish(acc):
            if scale is not None:
                acc = acc * scale
            if has_res:
                acc = res_ref[...] + acc
            o_ref[...] = acc.astype(out_dtype)

        if nk == 1:
            finish(prod)
        else:
            @pl.when(k == 0)
            def _():
                acc_ref[...] = prod

            @pl.when(k > 0)
            def _():
                acc_ref[...] += prod

            @pl.when(k == nk - 1)
            def _():
                finish(acc_ref[...])

    in_specs = [a_spec, b_spec] + ([o_spec] if has_res else [])
    args = (a, b) + ((res,) if has_res else ())
    return pl.pallas_call(
        body, grid=(M // tm, N // tn, nk), in_specs=in_specs, out_specs=o_spec,
        out_shape=jax.ShapeDtypeStruct((M, N), out_dtype),
        scratch_shapes=[pltpu.VMEM((tm, tn) if nk > 1 else (8, LANES), F32)],
        compiler_params=_params(("parallel", "parallel", "arbitrary")), name=name)(*args)


def swiglu_fwd(h, w_slab, *, name):
    S, D = h.shape
    Fc = w_slab.shape[-1]
    tm = _tile(S, 512, 16)

    def body(h_ref, wa_ref, wb_ref, a_ref, b_ref, act_ref):
        hv = h_ref[...]
        a = _dot(hv, wa_ref[...], 1, 0)
        b = _dot(hv, wb_ref[...], 1, 0)
        a_ref[...] = a.astype(BF16)
        b_ref[...] = b.astype(BF16)
        act_ref[...] = (a * _sigmoid(a) * b).astype(BF16)

    out = pl.BlockSpec((tm, Fc), lambda j, i: (i, j))
    return pl.pallas_call(
        body, grid=(2, S // tm),
        in_specs=[pl.BlockSpec((tm, D), lambda j, i: (i, 0)),
                  pl.BlockSpec((None, D, Fc), lambda j, i: (j, 0, 0)),
                  pl.BlockSpec((None, D, Fc), lambda j, i: (j + 2, 0, 0))],
        out_specs=[out, out, out],
        out_shape=[jax.ShapeDtypeStruct((S, 2 * Fc), BF16)] * 3,
        compiler_params=_params(("parallel", "parallel")), name=name)(h, w_slab, w_slab)


def swiglu_bwd(dy, w_out, a, b, *, name):
    S, D = dy.shape
    F = w_out.shape[0]
    fc = F // 2
    tm = _tile(S, 512, 16)

    def body(dy_ref, w_ref, a_ref, b_ref, da_ref, db_ref):
        dact = 0.5 * _dot(dy_ref[...].astype(BF16), w_ref[...], 1, 1)
        av = a_ref[...].astype(F32)
        sg = _sigmoid(av)
        da_ref[...] = (dact * b_ref[...].astype(F32) * (sg * (1.0 + av * (1.0 - sg)))).astype(BF16)
        db_ref[...] = (dact * (av * sg)).astype(BF16)

    blk = pl.BlockSpec((tm, fc), lambda j, i: (i, j))
    return pl.pallas_call(
        body, grid=(2, S // tm),
        in_specs=[pl.BlockSpec((tm, D), lambda j, i: (i, 0)), pl.BlockSpec((fc, D), lambda j, i: (j, 0)), blk, blk],
        out_specs=[blk, blk],
        out_shape=[jax.ShapeDtypeStruct((S, F), BF16), jax.ShapeDtypeStruct((S, F), BF16)],
        compiler_params=_params(("parallel", "parallel")), name=name)(dy, w_out, a, b)


def ffn_dh(da, db, w_slab, x, g, dy, *, name):
    S, F = da.shape
    D, Fc = w_slab.shape[-2:]
    tm = _tile(S, 1024, 16)
    sub = _tile(tm, 256, 8)

    def body(da_ref, db_ref, w_ref, x_ref, g_ref, dy_ref, dx_ref, dg_ref, acc_ref):
        i, k = pl.program_id(0), pl.program_id(1)

        @pl.when(k == 0)
        def _():
            acc_ref[...] = jnp.zeros_like(acc_ref)

        @pl.when(k < 2)
        def _():
            acc_ref[...] += _dot(da_ref[...], w_ref[...], 1, 1)

        @pl.when(k >= 2)
        def _():
            acc_ref[...] += _dot(db_ref[...], w_ref[...], 1, 1)

        @pl.when(k == 3)
        def _():
            part = None
            for c in range(tm // sub):
                rows = pl.ds(c * sub, sub)
                xv, dh = x_ref[rows, :], acc_ref[rows, :]
                r = lax.rsqrt(jnp.mean(xv * xv, axis=-1, keepdims=True) + EPS)
                u = dh * g_ref[...]
                dx_ref[rows, :] = dy_ref[rows, :] + (r * u - xv * (r * r * r) * jnp.mean(xv * u, axis=-1, keepdims=True))
                p = jnp.sum(dh * xv * r, axis=0, keepdims=True)
                part = p if part is None else part + p

            @pl.when(i == 0)
            def _():
                dg_ref[...] = part

            @pl.when(i > 0)
            def _():
                dg_ref[...] += part

    row = pl.BlockSpec((tm, D), lambda i, k: (i, 0))
    vec = pl.BlockSpec((1, D), lambda i, k: (0, 0))
    return pl.pallas_call(
        body, grid=(S // tm, 4),
        in_specs=[pl.BlockSpec((tm, Fc), lambda i, k: (i, jnp.minimum(k, 1))),
                  pl.BlockSpec((tm, Fc), lambda i, k: (i, jnp.maximum(k - 2, 0))),
                  pl.BlockSpec((None, D, Fc), lambda i, k: (k, 0, 0)), row, vec, row],
        out_specs=[row, vec],
        out_shape=[jax.ShapeDtypeStruct((S, D), F32), jax.ShapeDtypeStruct((1, D), F32)],
        scratch_shapes=[pltpu.VMEM((tm, D), F32)],
        compiler_params=_params(("arbitrary", "arbitrary")), name=name)(da, db, w_slab, x, g.reshape(1, D), dy)


def grad_cols(h, da, db, *, name):
    S, D = h.shape
    Fc = da.shape[1] // 2
    tk = _tile(S, 1024, 16)
    nk = S // tk

    def body(h_ref, da_ref, db_ref, o_ref, acc_ref):
        ch, k = pl.program_id(0), pl.program_id(1)

        @pl.when(k == 0)
        def _():
            acc_ref[...] = jnp.zeros_like(acc_ref)

        @pl.when(ch < 2)
        def _():
            acc_ref[...] += _dot(h_ref[...], da_ref[...], 0, 0)

        @pl.when(ch >= 2)
        def _():
            acc_ref[...] += _dot(h_ref[...], db_ref[...], 0, 0)

        @pl.when(k == nk - 1)
        def _():
            o_ref[...] = acc_ref[...]

    return pl.pallas_call(
        body, grid=(4, nk),
        in_specs=[pl.BlockSpec((tk, D), lambda ch, k: (k, 0)),
                  pl.BlockSpec((tk, Fc), lambda ch, k: (jnp.where(ch < 2, k, 0), jnp.minimum(ch, 1))),
                  pl.BlockSpec((tk, Fc), lambda ch, k: (jnp.where(ch >= 2, k, 0), jnp.maximum(ch - 2, 0)))],
        out_specs=pl.BlockSpec((None, D, Fc), lambda ch, k: (ch, 0, 0)),
        out_shape=jax.ShapeDtypeStruct((4, D, Fc), F32),
        scratch_shapes=[pltpu.VMEM((D, Fc), F32)],
        compiler_params=_params(("parallel", "arbitrary")), name=name)(h, da, db)


def grad_rows(a, b, *, scale=None, name):
    S, M = a.shape
    N = b.shape[1]
    R = M // 4
    tn = _tile(N, 512, LANES)
    tk = _tile(S, 1024, 16)
    nk = S // tk

    def body(a_ref, b_ref, o_ref, acc_ref):
        k = pl.program_id(1)

        @pl.when(k == 0)
        def _():
            acc_ref[...] = jnp.zeros_like(acc_ref)

        acc_ref[...] += _dot(a_ref[...].astype(BF16), b_ref[...].astype(BF16), 0, 0)

        @pl.when(k == nk - 1)
        def _():
            for d in range(4):
                part = acc_ref[d * R:(d + 1) * R, :]
                o_ref[d] = part if scale is None else part * scale

    return pl.pallas_call(
        body, grid=(N // tn, nk),
        in_specs=[pl.BlockSpec((tk, M), lambda j, k: (k, 0)), pl.BlockSpec((tk, tn), lambda j, k: (k, j))],
        out_specs=pl.BlockSpec((4, R, tn), lambda j, k: (0, 0, j)),
        out_shape=jax.ShapeDtypeStruct((4, R, N), F32),
        scratch_shapes=[pltpu.VMEM((M, tn), F32)],
        compiler_params=_params(("parallel", "arbitrary")), name=name)(a, b)


def rms_fwd(x, g, *, name):
    S, D = x.shape
    ts = _tile(S, 1024, 16)

    def body(x_ref, g_ref, h_ref):
        xv = x_ref[...]
        r = lax.rsqrt(jnp.mean(xv * xv, axis=-1, keepdims=True) + EPS)
        h_ref[...] = (xv * r * g_ref[...]).astype(BF16)

    return pl.pallas_call(
        body, grid=(S // ts,),
        in_specs=[pl.BlockSpec((ts, D), lambda i: (i, 0)), pl.BlockSpec((1, D), lambda i: (0, 0))],
        out_specs=pl.BlockSpec((ts, D), lambda i: (i, 0)),
        out_shape=jax.ShapeDtypeStruct((S, D), BF16),
        compiler_params=_params(("parallel",)), name=name)(x, g.reshape(1, D))


def rms_bwd(x, dh, g, res, *, name):
    S, D = x.shape
    ts = _tile(S, 512, 16)
    has_res = res is not None

    def body(*refs):
        x_ref, dh_ref, g_ref = refs[:3]
        res_ref = refs[3] if has_res else None
        dx_ref, dg_ref = refs[-2:]
        i = pl.program_id(0)
        xv, dhv = x_ref[...], dh_ref[...].astype(F32)
        r = lax.rsqrt(jnp.mean(xv * xv, axis=-1, keepdims=True) + EPS)
        u = dhv * g_ref[...]
        dx = r * u - xv * (r * r * r) * jnp.mean(xv * u, axis=-1, keepdims=True)
        if has_res:
            dx = res_ref[...] + dx
        dx_ref[...] = dx
        part = jnp.sum(dhv * xv * r, axis=0, keepdims=True)

        @pl.when(i == 0)
        def _():
            dg_ref[...] = part

        @pl.when(i > 0)
        def _():
            dg_ref[...] += part

    row = pl.BlockSpec((ts, D), lambda i: (i, 0))
    vec = pl.BlockSpec((1, D), lambda i: (0, 0))
    args = (x, dh, g.reshape(1, D)) + ((res,) if has_res else ())
    return pl.pallas_call(
        body, grid=(S // ts,), in_specs=[row, row, vec] + ([row] if has_res else []),
        out_specs=[row, vec],
        out_shape=[jax.ShapeDtypeStruct((S, D), F32), jax.ShapeDtypeStruct((1, D), F32)],
        compiler_params=_params(("arbitrary",)), name=name)(*args)


def _low_half(shape):
    return lax.broadcasted_iota(jnp.int32, shape, len(shape) - 1) < HEAD_DIM


def _half_sums(x, low):
    sa = jnp.sum(jnp.where(low, x, 0.0), axis=1, keepdims=True)
    sb = jnp.sum(jnp.where(low, 0.0, x), axis=1, keepdims=True)
    return jnp.where(low, sa, sb)


def pairnorm_fwd(x, col0, n_pairs, g, *, scale=None, name):
    S = x.shape[0]
    ts = _tile(S, 512, 16)
    W = n_pairs * LANES
    assert col0 % n_pairs == 0

    def body(x_ref, g_ref, o_ref):
        for p in range(n_pairs):
            cols = pl.ds(p * LANES, LANES)
            xv = x_ref[:, cols]
            r = lax.rsqrt(_half_sums(xv * xv, _low_half(xv.shape)) * (1.0 / HEAD_DIM) + EPS)
            y = xv * r * g_ref[...]
            o_ref[:, cols] = (y if scale is None else y * scale).astype(BF16)

    return pl.pallas_call(
        body, grid=(S // ts,),
        in_specs=[pl.BlockSpec((ts, W), lambda i: (i, col0 // n_pairs)), pl.BlockSpec((1, LANES), lambda i: (0, 0))],
        out_specs=pl.BlockSpec((ts, W), lambda i: (i, 0)),
        out_shape=jax.ShapeDtypeStruct((S, W), BF16),
        compiler_params=_params(("parallel",)), name=name)(x, jnp.tile(g.reshape(1, HEAD_DIM), (1, 2)))


def pairnorm_bwd(x, col0, n_pairs, dy, g, *, name):
    S = x.shape[0]
    ts = _tile(S, 512, 16)
    W = n_pairs * LANES
    assert col0 % n_pairs == 0

    def body(x_ref, dy_ref, g_ref, dx_ref, dg_ref):
        part = None
        for p in range(n_pairs):
            cols = pl.ds(p * LANES, LANES)
            xv, dyv = x_ref[:, cols], dy_ref[:, cols]
            low = _low_half(xv.shape)
            r = lax.rsqrt(_half_sums(xv * xv, low) * (1.0 / HEAD_DIM) + EPS)
            u = dyv * g_ref[...]
            dx_ref[:, cols] = r * u - xv * (r * r * r) * (_half_sums(xv * u, low) * (1.0 / HEAD_DIM))
            pp = jnp.sum(dyv * xv * r, axis=0, keepdims=True)
            part = pp if part is None else part + pp

        @pl.when(pl.program_id(0) == 0)
        def _():
            dg_ref[...] = part

        @pl.when(pl.program_id(0) > 0)
        def _():
            dg_ref[...] += part

    vec = pl.BlockSpec((1, LANES), lambda i: (0, 0))
    blk = pl.BlockSpec((ts, W), lambda i: (i, 0))
    return pl.pallas_call(
        body, grid=(S // ts,),
        in_specs=[pl.BlockSpec((ts, W), lambda i: (i, col0 // n_pairs)), blk, vec], out_specs=[blk, vec],
        out_shape=[jax.ShapeDtypeStruct((S, W), F32), jax.ShapeDtypeStruct((1, LANES), F32)],
        compiler_params=_params(("arbitrary",)), name=name)(x, dy, jnp.tile(g.reshape(1, HEAD_DIM), (1, 2)))


def _split3(x):
    x1 = x.astype(BF16)
    r1 = x - x1.astype(F32)
    x2 = r1.astype(BF16)
    x3 = (r1 - x2.astype(F32)).astype(BF16)
    return x1, x2, x3


def _tri_ones(n, lower):
    r = lax.broadcasted_iota(jnp.int32, (n, n), 0)
    c = lax.broadcasted_iota(jnp.int32, (n, n), 1)
    return jnp.where((c <= r) if lower else (c >= r), 1.0, 0.0).astype(BF16)


def fgate_fwd(z, col0, bias, *, name):
    S, L = z.shape[0], LANES
    tb = _tile(S, 256, 16)

    def body(z_ref, b_ref, c_ref, carry):
        i = pl.program_id(0)

        @pl.when(i == 0)
        def _():
            carry[...] = jnp.zeros_like(carry)

        zz = z_ref[...] + b_ref[...]
        lf = jnp.minimum(zz, 0.0) - jnp.log(1.0 + jnp.exp(-jnp.abs(zz)))
        tri = _tri_ones(tb, True)
        x1, x2, x3 = _split3(lf)
        c = (_dot(tri, x1, 1, 0) + _dot(tri, x2, 1, 0)) + _dot(tri, x3, 1, 0) + carry[...]
        c_ref[...] = c
        carry[...] += jnp.sum(lf, axis=0, keepdims=True)

    return pl.pallas_call(
        body, grid=(S // tb,),
        in_specs=[pl.BlockSpec((tb, L), lambda i: (i, col0)), pl.BlockSpec((1, L), lambda i: (0, 0))],
        out_specs=pl.BlockSpec((tb, L), lambda i: (i, 0)),
        out_shape=jax.ShapeDtypeStruct((S, L), F32),
        scratch_shapes=[pltpu.VMEM((1, L), F32)],
        compiler_params=_params(("arbitrary",)), name=name)(z, bias)


def fgate_bwd(z, col0, bias, drs, dcs, *, name):
    S, L = z.shape[0], LANES
    n_pairs = drs.shape[0]
    tb = _tile(S, 256, 16)
    nb = S // tb

    def body(z_ref, b_ref, drs_ref, dcs_ref, dz_ref, db_ref, carry):
        i = pl.program_id(0)

        @pl.when(i == 0)
        def _():
            carry[...] = jnp.zeros_like(carry)

        tri = _tri_ones(tb, False)
        lane = lax.broadcasted_iota(jnp.int32, (tb, L), 1)
        dc = -dcs_ref[...]
        for h in range(2 * n_pairs):
            dc = dc + jnp.where(lane == h, jnp.sum(drs_ref[h // 2, h % 2], axis=1, keepdims=True), 0.0)
        x1, x2, x3 = _split3(dc)
        dlf = (_dot(tri, x1, 1, 0) + _dot(tri, x2, 1, 0)) + _dot(tri, x3, 1, 0) + carry[...]
        carry[...] += jnp.sum(dc, axis=0, keepdims=True)
        dz = dlf * _sigmoid(-(z_ref[...] + b_ref[...]))
        dz_ref[...] = dz
        part = jnp.sum(dz, axis=0, keepdims=True)

        @pl.when(i == 0)
        def _():
            db_ref[...] = part

        @pl.when(i > 0)
        def _():
            db_ref[...] += part

    rev = pl.BlockSpec((tb, L), lambda i: (nb - 1 - i, 0))
    vec = pl.BlockSpec((1, L), lambda i: (0, 0))
    return pl.pallas_call(
        body, grid=(nb,),
        in_specs=[pl.BlockSpec((tb, L), lambda i: (nb - 1 - i, col0)), vec,
                  pl.BlockSpec((n_pairs, 2, tb, L), lambda i: (0, 0, nb - 1 - i, 0)), rev],
        out_specs=[rev, vec],
        out_shape=[jax.ShapeDtypeStruct((S, L), F32), jax.ShapeDtypeStruct((1, L), F32)],
        scratch_shapes=[pltpu.VMEM((1, L), F32)],
        compiler_params=_params(("arbitrary",)), name=name)(z, bias, drs, dcs)


def _one_head(x, low, a):
    return jnp.where(low if a == 0 else jnp.logical_not(low), x, jnp.zeros_like(x))


class Rider(NamedTuple):
    inputs: tuple
    out_shapes: tuple
    aliases: dict
    sems: tuple
    plan: Callable


def _with_rider(rider, n_in, n_out, n_scratch):
    if rider is None:
        return [], [], [], [], {}, lambda refs: (refs[:n_in], refs[n_in:n_in + n_out], refs[n_in + n_out:], None)
    e_in, e_out = len(rider.inputs), len(rider.out_shapes)

    def split(refs):
        ins, r_in = refs[:n_in], refs[n_in:n_in + e_in]
        o0 = n_in + e_in
        outs, r_out = refs[o0:o0 + n_out], refs[o0 + n_out:o0 + n_out + e_out]
        s0 = o0 + n_out + e_out
        return ins, outs, refs[s0:s0 + n_scratch], rider.plan(r_in, r_out, refs[s0 + n_scratch:])

    aliases = {n_in + a: n_out + b for a, b in rider.aliases.items()}
    return list(rider.inputs), [_ANY] * e_in, list(rider.out_shapes), [_ANY] * e_out, aliases, split


def attn_fwd(q, q0, k, k0, v, v0, n_pairs, decay, *, causal, rider=None, name):
    Sq, Sk = q.shape[0], k.shape[0]
    tq = _tile(Sq, ATTN_Q_BLOCK, LANES)
    tk = _tile(Sk, ATTN_K_BLOCK, LANES)
    nq, nk = Sq // tq, Sk // tk
    bias = decay is not None
    rs = _tile(tq, ATTN_ROW_CHUNK, 16)
    r_args, r_in_specs, r_shapes, r_out_specs, aliases, split = _with_rider(rider, 4 if bias else 3, 2, 4)

    def row_sum_lanes(acc, low, a):
        other = jnp.logical_not(low) if a == 0 else low
        return jnp.max(jnp.where(other, acc, 0.0), axis=1, keepdims=True)

    def body(*refs):
        ins, (o_ref, lse_ref), scratch, ride = split(refs)
        m_sc, acc_sc = scratch[:2], scratch[2:]
        q_ref, k_ref, v_ref = ins[:3]
        ck_ref = ins[3] if bias else None
        pr, i, j = pl.program_id(0), pl.program_id(1), pl.program_id(2)
        if ride is not None:
            pl.when(jnp.logical_and(pr == 0, jnp.logical_and(i == 0, j == 0)))(ride[0])

        @pl.when(j == 0)
        def _():
            for a in range(2):
                m_sc[a][...] = jnp.full_like(m_sc[a], NEG)
                acc_sc[a][...] = jnp.zeros_like(acc_sc[a])

        def compute(masked):
            kv, vv = k_ref[...], v_ref[...].astype(BF16)
            low_k = _low_half(kv.shape)
            va = [jnp.where(low_k if a == 0 else jnp.logical_not(low_k), vv, jnp.ones_like(vv)) for a in range(2)]
            for r in range(tq // rs):
                rows = pl.ds(r * rs, rs)
                qv = q_ref[rows, :]
                low = _low_half(qv.shape)
                for a in range(2):
                    s = _dot(_one_head(qv, low, a), kv, 1, 1)
                    if bias:
                        s = s - ck_ref[a]
                    if masked:
                        row = i * tq + r * rs + lax.broadcasted_iota(jnp.int32, (rs, tk), 0)
                        col = j * tk + lax.broadcasted_iota(jnp.int32, (rs, tk), 1)
                        s = jnp.where(col <= row, s, NEG)
                    m_prev = m_sc[a][rows, :]
                    m_new = jnp.maximum(m_prev, jnp.max(s, axis=1, keepdims=True))
                    alpha = jnp.exp(m_prev - m_new)
                    p = jnp.exp(s - m_new).astype(BF16)
                    acc_sc[a][rows, :] = alpha * acc_sc[a][rows, :] + _dot(p, va[a], 1, 0)
                    m_sc[a][rows, :] = m_new

        if causal:
            live = j * tk <= i * tq + (tq - 1)
            crosses = j * tk + (tk - 1) > i * tq
            pl.when(jnp.logical_and(live, crosses))(functools.partial(compute, True))
            pl.when(jnp.logical_and(live, jnp.logical_not(crosses)))(functools.partial(compute, False))
        else:
            compute(False)

        @pl.when(j == nk - 1)
        def _():
            low = _low_half((tq, LANES))
            l = [row_sum_lanes(acc_sc[a][...], low, a) for a in range(2)]
            o_ref[...] = jnp.where(low, acc_sc[0][...] / l[0], acc_sc[1][...] / l[1])
            for a in range(2):
                lse_ref[a] = m_sc[a][...] + jnp.log(l[a])

        if ride is not None:
            pl.when(jnp.logical_and(pr == n_pairs - 1, jnp.logical_and(i == nq - 1, j == nk - 1)))(ride[1])

    def kv_blk(i, j):
        return jnp.minimum(j, (i * tq + tq - 1) // tk) if causal else j

    in_specs = [pl.BlockSpec((tq, LANES), lambda p, i, j: (i, q0 + p)),
                pl.BlockSpec((tk, LANES), lambda p, i, j: (kv_blk(i, j), k0 + p)),
                pl.BlockSpec((tk, LANES), lambda p, i, j: (kv_blk(i, j), v0 + p))]
    args = [q, k, v]
    if bias:
        in_specs.append(pl.BlockSpec((None, 2, 1, tk), lambda p, i, j: (p, 0, 0, kv_blk(i, j))))
        args.append(decay)
    out = pl.pallas_call(
        body, grid=(n_pairs, nq, nk), in_specs=in_specs + r_in_specs,
        out_specs=[pl.BlockSpec((tq, LANES), lambda p, i, j: (i, p)),
                   pl.BlockSpec((None, 2, tq, 1), lambda p, i, j: (p, 0, i, 0))] + r_out_specs,
        out_shape=[jax.ShapeDtypeStruct((Sq, n_pairs * LANES), F32),
                   jax.ShapeDtypeStruct((n_pairs, 2, Sq, 1), F32)] + r_shapes,
        scratch_shapes=[pltpu.VMEM((tq, 1), F32)] * 2 + [pltpu.VMEM((tq, LANES), F32)] * 2
        + (list(rider.sems) if rider else []),
        input_output_aliases=aliases,
        compiler_params=_params(("arbitrary",) * 3 if rider else ("parallel", "parallel", "arbitrary")),
        name=name)(*args, *r_args)
    return out[0], out[1], out[2:]


def attn_delta(o, do, do0, n_pairs, *, name):
    S = o.shape[0]
    ts = _tile(S, 512, 16)
    W = n_pairs * LANES
    assert do0 % n_pairs == 0

    def body(o_ref, do_ref, out_ref):
        for p in range(n_pairs):
            cols = pl.ds(p * LANES, LANES)
            prod = o_ref[:, cols] * do_ref[:, cols]
            low = _low_half(prod.shape)
            out_ref[p, 0] = jnp.sum(jnp.where(low, prod, 0.0), axis=1, keepdims=True)
            out_ref[p, 1] = jnp.sum(jnp.where(low, 0.0, prod), axis=1, keepdims=True)

    return pl.pallas_call(
        body, grid=(S // ts,),
        in_specs=[pl.BlockSpec((ts, W), lambda i: (i, 0)), pl.BlockSpec((ts, W), lambda i: (i, do0 // n_pairs))],
        out_specs=pl.BlockSpec((n_pairs, 2, ts, 1), lambda i: (0, 0, i, 0)),
        out_shape=jax.ShapeDtypeStruct((n_pairs, 2, S, 1), F32),
        compiler_params=_params(("parallel",)), name=name)(o, do)


def attn_bwd(q, q0, k, k0, v, v0, do, do0, n_pairs, lse, delta, decay, *, causal, rider=None, name):
    Sq, Sk = q.shape[0], k.shape[0]
    tq = _tile(Sq, ATTN_Q_BLOCK, LANES)
    tk = _tile(Sk, ATTN_K_BLOCK, LANES)
    nq, nk = Sq // tq, Sk // tk
    bias = decay is not None

    rs = _tile(tq, ATTN_ROW_CHUNK, 16)
    r_args, r_in_specs, r_shapes, r_out_specs, aliases, split = _with_rider(
        rider, 7 if bias else 6, 5 if bias else 3, 0)

    def body(*refs):
        ins, outs, _, ride = split(refs)
        q_ref, k_ref, v_ref, do_ref, lse_ref, dl_ref = ins[:6]
        ck_ref = ins[6] if bias else None
        dq_ref, dk_ref, dv_ref = outs[:3]
        dcs_ref, drs_ref = (outs[3], outs[4]) if bias else (None, None)
        pr, j, i = pl.program_id(0), pl.program_id(1), pl.program_id(2)
        if ride is not None:
            pl.when(jnp.logical_and(pr == 0, jnp.logical_and(i == 0, j == 0)))(ride[0])

        @pl.when(i == 0)
        def _():
            dk_ref[...] = jnp.zeros_like(dk_ref)
            dv_ref[...] = jnp.zeros_like(dv_ref)
            if bias:
                dcs_ref[...] = jnp.zeros_like(dcs_ref)

        def compute(masked):
            kv, vv = k_ref[...], v_ref[...].astype(BF16)
            low_k = _low_half(kv.shape)
            ka = [_one_head(kv, low_k, a) for a in range(2)]
            for r in range(tq // rs):
                here = pl.ds(r * rs, rs)
                rows = pl.ds(pl.multiple_of(i * tq + r * rs, rs), rs)
                qv, dov = q_ref[here, :], do_ref[here, :].astype(BF16)
                low = _low_half(qv.shape)
                dq_part, dk_part, dv_part, row_parts, col_parts = None, None, None, [], []
                for a in range(2):
                    qa, doa = _one_head(qv, low, a), _one_head(dov, low, a)
                    s = _dot(qa, kv, 1, 1)
                    if bias:
                        s = s - ck_ref[a]
                    p = jnp.exp(s - lse_ref[a, here])
                    if masked:
                        row = i * tq + r * rs + lax.broadcasted_iota(jnp.int32, (rs, tk), 0)
                        col = j * tk + lax.broadcasted_iota(jnp.int32, (rs, tk), 1)
                        p = jnp.where(col <= row, p, 0.0)
                    dv_a = _dot(p.astype(BF16), doa, 0, 0)
                    dp = _dot(doa, vv, 1, 1)
                    ds = p * (dp - dl_ref[a, here])
                    dsb = ds.astype(BF16)
                    dk_a = _dot(dsb, qa, 0, 0)
                    if bias:
                        col_parts.append(jnp.sum(ds, axis=0, keepdims=True))
                        lanes = ds[:, :LANES]
                        for c in range(1, tk // LANES):
                            lanes = lanes + ds[:, c * LANES:(c + 1) * LANES]
                        row_parts.append(lanes)
                    part = _dot(dsb, ka[a], 1, 0) * QK_SCALE
                    dq_part = part if dq_part is None else dq_part + part
                    dk_part = dk_a if dk_part is None else dk_part + dk_a
                    dv_part = dv_a if dv_part is None else dv_part + dv_a
                dv_ref[...] += dv_part
                dk_ref[...] += dk_part
                for a, cp in enumerate(col_parts):
                    dcs_ref[a] += cp

                @pl.when(j == 0)
                def _(rows=rows, dq_part=dq_part, row_parts=row_parts):
                    dq_ref[rows, :] = dq_part
                    for a, rp in enumerate(row_parts):
                        drs_ref[a, rows, :] = rp

                @pl.when(j > 0)
                def _(rows=rows, dq_part=dq_part, row_parts=row_parts):
                    dq_ref[rows, :] += dq_part
                    for a, rp in enumerate(row_parts):
                        drs_ref[a, rows, :] += rp

        if causal:
            live = j * tk <= i * tq + (tq - 1)
            crosses = j * tk + (tk - 1) > i * tq
            pl.when(jnp.logical_and(live, crosses))(functools.partial(compute, True))
            pl.when(jnp.logical_and(live, jnp.logical_not(crosses)))(functools.partial(compute, False))
        else:
            compute(False)

        if ride is not None:
            pl.when(jnp.logical_and(pr == n_pairs - 1, jnp.logical_and(i == nq - 1, j == nk - 1)))(ride[1])

    def q_blk(j, i):
        return jnp.maximum(i, (j * tk) // tq) if causal else i

    col1 = pl.BlockSpec((None, 2, tq, 1), lambda p, j, i: (p, 0, q_blk(j, i), 0))
    in_specs = [pl.BlockSpec((tq, LANES), lambda p, j, i: (q_blk(j, i), q0 + p)),
                pl.BlockSpec((tk, LANES), lambda p, j, i: (j, k0 + p)),
                pl.BlockSpec((tk, LANES), lambda p, j, i: (j, v0 + p)),
                pl.BlockSpec((tq, LANES), lambda p, j, i: (q_blk(j, i), do0 + p)), col1, col1]
    args = [q, k, v, do, lse, delta]
    kout = pl.BlockSpec((tk, LANES), lambda p, j, i: (j, p))
    out_specs = [pl.BlockSpec((Sq, LANES), lambda p, j, i: (0, p)), kout, kout]
    out_shape = [jax.ShapeDtypeStruct((Sq, n_pairs * LANES), F32), jax.ShapeDtypeStruct((Sk, n_pairs * LANES), F32),
                 jax.ShapeDtypeStruct((Sk, n_pairs * LANES), F32)]
    if bias:
        in_specs.append(pl.BlockSpec((None, 2, 1, tk), lambda p, j, i: (p, 0, 0, j)))
        args.append(decay)
        out_specs += [pl.BlockSpec((None, 2, 1, tk), lambda p, j, i: (p, 0, 0, j)),
                      pl.BlockSpec((None, 2, Sq, LANES), lambda p, j, i: (p, 0, 0, 0))]
        out_shape += [jax.ShapeDtypeStruct((n_pairs, 2, 1, Sk), F32),
                      jax.ShapeDtypeStruct((n_pairs, 2, Sq, LANES), F32)]
    n_own = len(out_shape)
    out = pl.pallas_call(
        body, grid=(n_pairs, nk, nq), in_specs=in_specs + r_in_specs, out_specs=out_specs + r_out_specs,
        out_shape=out_shape + r_shapes, scratch_shapes=list(rider.sems) if rider else [],
        input_output_aliases=aliases,
        compiler_params=_params(("arbitrary",) * 3 if rider else ("parallel", "arbitrary", "arbitrary")),
        name=name)(*args, *r_args)
    return tuple(out[:n_own]), out[n_own:]


def _tril_mask(n):
    r = lax.broadcasted_iota(jnp.int32, (n, n), 0)
    c = lax.broadcasted_iota(jnp.int32, (n, n), 1)
    return c <= r


def _gmlp_operands(v_gain, w_s, b_s):
    G = w_s.shape[0]
    return (v_gain.reshape(G // 2, 1, LANES), w_s.reshape(G // 2, 2, CHUNK, CHUNK), b_s.reshape(G // 2, 2, CHUNK, 1))


def _gmlp_gate(wt, vh, b_ref, low):
    gate = _dot(wt[0], _one_head(vh, low, 0), 1, 0) + _dot(wt[1], _one_head(vh, low, 1), 1, 0)
    return gate + jnp.where(low, b_ref[0], b_ref[1])


def gmlp_fwd(proj, v0, n_pairs, vg, w, b, *, name):
    S = proj.shape[0]
    ts = _tile(S, 1024, CHUNK)

    def body(up_ref, vp_ref, vg_ref, w_ref, b_ref, o_ref):
        mask = _tril_mask(CHUNK)
        wt = [jnp.where(mask, w_ref[a], 0.0).astype(BF16) for a in range(2)]
        low = _low_half((CHUNK, LANES))
        for c in range(ts // CHUNK):
            sl = pl.ds(c * CHUNK, CHUNK)
            vz = _gelu(vp_ref[sl, :])
            r = lax.rsqrt(_half_sums(vz * vz, low) * (1.0 / HEAD_DIM) + EPS)
            vh = (vz * r * vg_ref[...]).astype(BF16)
            o_ref[sl, :] = _gelu(up_ref[sl, :]) * _gmlp_gate(wt, vh, b_ref, low)

    return pl.pallas_call(
        body, grid=(n_pairs, S // ts),
        in_specs=[pl.BlockSpec((ts, LANES), lambda p, i: (i, p)), pl.BlockSpec((ts, LANES), lambda p, i: (i, v0 + p)),
                  pl.BlockSpec((None, 1, LANES), lambda p, i: (p, 0, 0)),
                  pl.BlockSpec((None, 2, CHUNK, CHUNK), lambda p, i: (p, 0, 0, 0)),
                  pl.BlockSpec((None, 2, CHUNK, 1), lambda p, i: (p, 0, 0, 0))],
        out_specs=pl.BlockSpec((ts, LANES), lambda p, i: (i, p)),
        out_shape=jax.ShapeDtypeStruct((S, n_pairs * LANES), F32),
        compiler_params=_params(("parallel", "parallel")), name=name)(proj, proj, vg, w, b)


def gmlp_bwd(proj, v0, n_pairs, vg, w, wT, b, do, *, name):
    S = proj.shape[0]
    ts = _tile(S, 1024, CHUNK)

    def body(up_ref, vp_ref, vg_ref, w_ref, wT_ref, b_ref, do_ref, dup_ref, dvp_ref, dw_ref, db_ref, dvg_ref):
        i = pl.program_id(1)

        @pl.when(i == 0)
        def _():
            dw_ref[...] = jnp.zeros_like(dw_ref)
            db_ref[...] = jnp.zeros_like(db_ref)
            dvg_ref[...] = jnp.zeros_like(dvg_ref)

        mask = _tril_mask(CHUNK)
        wt = [jnp.where(mask, w_ref[a], 0.0).astype(BF16) for a in range(2)]
        wtT = [jnp.where(mask.T, wT_ref[a], 0.0).astype(BF16) for a in range(2)]
        low = _low_half((CHUNK, LANES))
        vgain = vg_ref[...]
        for c in range(ts // CHUNK):
            sl = pl.ds(c * CHUNK, CHUNK)
            u_pre, v_pre, dout = up_ref[sl, :], vp_ref[sl, :], do_ref[sl, :]
            vz = _gelu(v_pre)
            r = lax.rsqrt(_half_sums(vz * vz, low) * (1.0 / HEAD_DIM) + EPS)
            vh = (vz * r * vgain).astype(BF16)
            gate = _gmlp_gate(wt, vh, b_ref, low)
            dgate = dout * _gelu(u_pre)
            dup_ref[sl, :] = dout * gate * _gelu_grad(u_pre)
            dvh = None
            for a in range(2):
                dga = _one_head(dgate, low, a)
                dgb = dga.astype(BF16)
                dw_ref[a] += jnp.where(mask, _dot(dgb, vh, 1, 1), 0.0)
                db_ref[a] += jnp.sum(dga, axis=1, keepdims=True)
                part = _dot(wtT[a], dgb, 1, 0)
                dvh = part if dvh is None else dvh + part
            dvg_ref[...] += jnp.sum(dvh * vz * r, axis=0, keepdims=True)
            t = dvh * vgain
            dvz = r * t - vz * (r * r * r) * (_half_sums(vz * t, low) * (1.0 / HEAD_DIM))
            dvp_ref[sl, :] = dvz * _gelu_grad(v_pre)

    ublk = pl.BlockSpec((ts, LANES), lambda p, i: (i, p))
    wblk = pl.BlockSpec((None, 2, CHUNK, CHUNK), lambda p, i: (p, 0, 0, 0))
    bblk = pl.BlockSpec((None, 2, CHUNK, 1), lambda p, i: (p, 0, 0, 0))
    gblk = pl.BlockSpec((None, 1, LANES), lambda p, i: (p, 0, 0))
    return pl.pallas_call(
        body, grid=(n_pairs, S // ts),
        in_specs=[ublk, pl.BlockSpec((ts, LANES), lambda p, i: (i, v0 + p)), gblk, wblk, wblk, bblk, ublk],
        out_specs=[ublk, ublk, wblk, bblk, gblk],
        out_shape=[jax.ShapeDtypeStruct((S, n_pairs * LANES), F32), jax.ShapeDtypeStruct((S, n_pairs * LANES), F32),
                   jax.ShapeDtypeStruct((n_pairs, 2, CHUNK, CHUNK), F32), jax.ShapeDtypeStruct((n_pairs, 2, CHUNK, 1), F32),
                   jax.ShapeDtypeStruct((n_pairs, 1, LANES), F32)],
        compiler_params=_params(("parallel", "arbitrary")), name=name)(proj, proj, vg, w, wT, b, do)


def loss_head(y, target, *, name):
    S, D = y.shape
    ts = _tile(S, 512, 8)

    def body(y_ref, t_ref, dy_ref, loss_ref):
        i = pl.program_id(0)
        e = y_ref[...] - t_ref[...]
        dy_ref[...] = e * (1.0 / D)
        part = jnp.sum(jnp.sum(e * e, axis=1, keepdims=True), axis=0, keepdims=True) * (0.5 / D)

        @pl.when(i == 0)
        def _():
            loss_ref[...] = part

        @pl.when(i > 0)
        def _():
            loss_ref[...] += part

    row = pl.BlockSpec((ts, D), lambda i: (i, 0))
    return pl.pallas_call(
        body, grid=(S // ts,), in_specs=[row, row],
        out_specs=[row, pl.BlockSpec((1, 1), lambda i: (0, 0))],
        out_shape=[jax.ShapeDtypeStruct((S, D), F32), jax.ShapeDtypeStruct((1, 1), F32)],
        compiler_params=_params(("arbitrary",)), name=name)(y, target)


def adamw(w, g, m, v, *, name):
    shape = w.shape
    C = shape[-1]
    R = w.size // C
    tr = _tile(R, max(8, (256 * 1024) // C // 8 * 8), 8)

    def body(w_ref, g_ref, m_ref, v_ref, d_ref, nm_ref, nv_ref):
        gv = g_ref[...]
        nm = ADAM_B1 * m_ref[...] + (1.0 - ADAM_B1) * gv
        nv = ADAM_B2 * v_ref[...] + (1.0 - ADAM_B2) * (gv * gv)
        m_hat = nm / (1.0 - ADAM_B1 ** ADAM_STEP)
        v_hat = nv / (1.0 - ADAM_B2 ** ADAM_STEP)
        d_ref[...] = -ADAM_LR * (m_hat / (jnp.sqrt(v_hat) + ADAM_EPS) + ADAM_WD * w_ref[...])
        nm_ref[...] = nm
        nv_ref[...] = nv

    blk = pl.BlockSpec((tr, C), lambda i: (i, 0))
    out = pl.pallas_call(
        body, grid=(R // tr,), in_specs=[blk] * 4, out_specs=[blk] * 3,
        out_shape=[jax.ShapeDtypeStruct((R, C), F32)] * 3,
        compiler_params=_params(("parallel",)), name=name)(*(a.reshape(R, C) for a in (w, g, m, v)))
    return tuple(o.reshape(shape) for o in out)


def pair_sum(p, landed, half, *, name):
    n, R, C = landed.shape
    tr = _tile(R, 256, 16)
    nr = R // tr

    def body(half_ref, p_ref, l_ref, o_ref):
        o_ref[...] = (p_ref[...] + l_ref[...]).astype(BF16)

    return pl.pallas_call(
        body,
        grid_spec=pltpu.PrefetchScalarGridSpec(
            num_scalar_prefetch=1, grid=(n, nr),
            in_specs=[pl.BlockSpec((None, tr, C), lambda k, r, half_ref: (k, half_ref[0] * nr + r, 0)),
                      pl.BlockSpec((None, tr, C), lambda k, r, half_ref: (k, r, 0))],
            out_specs=pl.BlockSpec((None, tr, C), lambda k, r, half_ref: (k, r, 0))),
        out_shape=jax.ShapeDtypeStruct((n, R, C), BF16),
        compiler_params=_params(("parallel", "parallel")), name=name)(half, p, landed)


def chip_sum(own, landed, chip, *, name):
    n, R, C = own.shape
    tr = _tile(R, 256, 16)

    def body(chip_ref, own_ref, *rest):
        l_refs, o_ref = rest[:n], rest[n]
        me = chip_ref[0]
        acc = None
        for d in range(n):
            term = jnp.where(me == d, own_ref[...], l_refs[d][...]).astype(F32)
            acc = term if acc is None else acc + term
        o_ref[...] = acc

    def landed_spec(d):
        return pl.BlockSpec((None, tr, C), lambda r, chip_ref: (jnp.where(chip_ref[0] == d, (d + 1) % n, d), r, 0))

    return pl.pallas_call(
        body,
        grid_spec=pltpu.PrefetchScalarGridSpec(
            num_scalar_prefetch=1, grid=(R // tr,),
            in_specs=[pl.BlockSpec((None, tr, C), lambda r, chip_ref: (chip_ref[0], r, 0))]
            + [landed_spec(d) for d in range(n)],
            out_specs=pl.BlockSpec((tr, C), lambda r, chip_ref: (r, 0))),
        out_shape=jax.ShapeDtypeStruct((R, C), F32),
        compiler_params=_params(("parallel",)), name=name)(chip, own, *([landed] * n))


def ordered_sum(parts, *, name):
    n, R, C = parts.shape
    tr = _tile(R, 256, 16)

    def body(p_ref, o_ref):
        acc = p_ref[0].astype(F32)
        for d in range(1, n):
            acc = acc + p_ref[d].astype(F32)
        o_ref[...] = acc

    return pl.pallas_call(
        body, grid=(R // tr,), in_specs=[pl.BlockSpec((n, tr, C), lambda r: (0, r, 0))],
        out_specs=pl.BlockSpec((tr, C), lambda r: (r, 0)),
        out_shape=jax.ShapeDtypeStruct((R, C), F32),
        compiler_params=_params(("parallel",)), name=name)(parts)


_ANY = pl.BlockSpec(memory_space=pl.ANY)


def _position():
    return lax.axis_index("x"), lax.axis_index("y"), lax.axis_index("c")


def _remote(src, dst, send_sem, recv_sem, device):
    return pltpu.make_async_remote_copy(src_ref=src, dst_ref=dst, send_sem=send_sem, recv_sem=recv_sem,
                                        device_id=device, device_id_type=MESH_ID)


def _small_all_gather(s_ref, all_ref, send_sems, recv_sems, x, y, c):
    me = 4 * x + 2 * y + c
    copies = []
    for f in range(1, 8):
        peer = ((1 - x) if f & 4 else x, (1 - y) if f & 2 else y, (1 - c) if f & 1 else c)
        cp = _remote(s_ref, all_ref.at[me], send_sems.at[f - 1], recv_sems.at[f - 1], peer)
        cp.start()
        copies.append((cp, peer, f - 1))

    def finish():
        for cp, peer, s in copies:
            slot = all_ref.at[4 * peer[0] + 2 * peer[1] + peer[2]]
            _remote(slot, slot, send_sems.at[s], recv_sems.at[s], peer).wait_recv()
        for cp, _, _ in copies:
            cp.wait_send()

    return finish


def _core_rows(ref, core):
    h = ref.shape[1] // 2
    return pl.ds(core * h, h)


def _gather_plan(outs, send_sems, recv_sems):
    n = len(outs)
    x, y, c = _position()
    k = 2 * x + y
    sibling = (x, y, 1 - c)
    chips = [(1 - x, y), (x, 1 - y), (1 - x, 1 - y)]

    def first():
        return [_remote(outs[w].at[k, _core_rows(outs[w], c)], outs[w].at[k, _core_rows(outs[w], c)],
                        send_sems.at[w, j], recv_sems.at[w, j], (px, py, c))
                for j, (px, py) in enumerate(chips) for w in range(n)]

    def start():
        for cp in first():
            cp.start()

    def finish():
        passed = []
        for j, (px, py) in enumerate(chips):
            for w in range(n):
                slot = outs[w].at[2 * px + py, _core_rows(outs[w], c)]
                _remote(slot, slot, send_sems.at[w, j], recv_sems.at[w, j], (px, py, c)).wait_recv()
                cp = _remote(slot, slot, send_sems.at[w, 3 + j], recv_sems.at[w, 3 + j], sibling)
                cp.start()
                passed.append(cp)
        for j, (px, py) in enumerate(chips):
            for w in range(n):
                slot = outs[w].at[2 * px + py, _core_rows(outs[w], 1 - c)]
                _remote(slot, slot, send_sems.at[w, 3 + j], recv_sems.at[w, 3 + j], sibling).wait_recv()
        for cp in first() + passed:
            cp.wait_send()

    return start, finish


def _gather_sems(n):
    return (pltpu.SemaphoreType.DMA((n, 6)), pltpu.SemaphoreType.DMA((n, 6)))


def gather_rider(slabs):
    return Rider(tuple(slabs), tuple(jax.ShapeDtypeStruct(a.shape, a.dtype) for a in slabs),
                 {i: i for i in range(len(slabs))}, _gather_sems(len(slabs)),
                 lambda ins, outs, sems: _gather_plan(outs, sems[0], sems[1]))


def gather_weights(slabs, small_slab, *, name):
    n = len(slabs)

    def body(*refs):
        outs, all_ref = refs[n + 1:2 * n + 1], refs[2 * n + 1]
        send_sems, recv_sems, s_send, s_recv = refs[2 * n + 2:]
        x, y, c = _position()
        finish_small = _small_all_gather(all_ref.at[4 * x + 2 * y + c], all_ref, s_send, s_recv, x, y, c)
        start, finish = _gather_plan(outs, send_sems, recv_sems)
        start()
        finish()
        finish_small()

    args = list(slabs) + [small_slab]
    out = pl.pallas_call(
        body, in_specs=[_ANY] * (n + 1), out_specs=[_ANY] * (n + 1),
        out_shape=[jax.ShapeDtypeStruct(a.shape, a.dtype) for a in args],
        input_output_aliases={i: i for i in range(n + 1)},
        scratch_shapes=list(_gather_sems(n)) + [pltpu.SemaphoreType.DMA((7,)), pltpu.SemaphoreType.DMA((7,))],
        name=name)(*args)
    return out[:n], out[n]


def exchange_with_sibling(parts, small_slab, *, name):
    n = len(parts)
    has_small = small_slab is not None
    n_arg = n + (1 if has_small else 0)

    def body(*refs):
        p_refs = refs[:n]
        lands = refs[n_arg:n_arg + n]
        send_sems, recv_sems = refs[2 * n_arg], refs[2 * n_arg + 1]
        x, y, c = _position()
        sibling = (x, y, 1 - c)
        if has_small:
            all_ref = refs[n_arg + n]
            finish_small = _small_all_gather(all_ref.at[4 * x + 2 * y + c], all_ref, refs[2 * n_arg + 2],
                                             refs[2 * n_arg + 3], x, y, c)
        sends = []
        for w in range(n):
            for d in range(4):
                cp = _remote(p_refs[w].at[d, _core_rows(p_refs[w], 1 - c)], lands[w].at[d],
                             send_sems.at[w, d], recv_sems.at[w, d], sibling)
                cp.start()
                sends.append(cp)
        for cp in sends:
            cp.wait_recv()
        for cp in sends:
            cp.wait_send()
        if has_small:
            finish_small()

    small_args = [small_slab] if has_small else []
    out = pl.pallas_call(
        body, in_specs=[_ANY] * n_arg, out_specs=[_ANY] * n_arg,
        out_shape=[jax.ShapeDtypeStruct((4, p.shape[1] // 2, p.shape[2]), p.dtype) for p in parts]
        + [jax.ShapeDtypeStruct(s.shape, s.dtype) for s in small_args],
        input_output_aliases={n: n} if has_small else {},
        scratch_shapes=[pltpu.SemaphoreType.DMA((n, 4)), pltpu.SemaphoreType.DMA((n, 4))]
        + ([pltpu.SemaphoreType.DMA((7,)), pltpu.SemaphoreType.DMA((7,))] if has_small else []),
        name=name)(*parts, *small_args)
    return out[:n], (out[n] if has_small else None)


def _scatter_plan(q_refs, outs, send_sems, recv_sems):
    n = len(q_refs)
    x, y, c = _position()
    k = 2 * x + y
    chips = [(1 - x, y), (x, 1 - y), (1 - x, 1 - y)]

    def sends():
        return [_remote(q_refs[w].at[2 * px + py], outs[w].at[k], send_sems.at[w, j], recv_sems.at[w, j], (px, py, c))
                for j, (px, py) in enumerate(chips) for w in range(n)]

    def start():
        for cp in sends():
            cp.start()

    def finish():
        for j, (px, py) in enumerate(chips):
            for w in range(n):
                slot = outs[w].at[2 * px + py]
                _remote(slot, slot, send_sems.at[w, j], recv_sems.at[w, j], (px, py, c)).wait_recv()
        for cp in sends():
            cp.wait_send()

    return start, finish


def _scatter_sems(n):
    return (pltpu.SemaphoreType.DMA((n, 3)), pltpu.SemaphoreType.DMA((n, 3)))


def scatter_rider(parts):
    return Rider(tuple(parts), tuple(jax.ShapeDtypeStruct(q.shape, q.dtype) for q in parts), {},
                 _scatter_sems(len(parts)), lambda ins, outs, sems: _scatter_plan(ins, outs, sems[0], sems[1]))


def scatter_to_chips(parts, *, name):
    n = len(parts)

    def body(*refs):
        start, finish = _scatter_plan(refs[:n], refs[n:2 * n], refs[2 * n], refs[2 * n + 1])
        start()
        finish()

    return pl.pallas_call(
        body, in_specs=[_ANY] * n, out_specs=[_ANY] * n,
        out_shape=[jax.ShapeDtypeStruct(q.shape, q.dtype) for q in parts],
        scratch_shapes=list(_scatter_sems(n)), name=name)(*parts)


def share_with_sibling(parts, *, name):
    n = len(parts)

    def body(*refs):
        r_refs, outs = refs[:n], refs[n:2 * n]
        send_sems, recv_sems = refs[2 * n:]
        x, y, c = _position()
        sends = []
        for w in range(n):
            cp = _remote(r_refs[w], outs[w], send_sems.at[w], recv_sems.at[w], (x, y, 1 - c))
            cp.start()
            sends.append(cp)
        for cp in sends:
            cp.wait_recv()
        for cp in sends:
            cp.wait_send()

    return pl.pallas_call(
        body, in_specs=[_ANY] * n, out_specs=[_ANY] * n,
        out_shape=[jax.ShapeDtypeStruct(r.shape, r.dtype) for r in parts],
        scratch_shapes=[pltpu.SemaphoreType.DMA((n,)), pltpu.SemaphoreType.DMA((n,))],
        name=name)(*parts)


def _cols_to_chips(full):
    *lead, R, C4 = full.shape
    t = full.reshape(*lead, R, 4, C4 // 4)
    return jnp.moveaxis(t, -2, 0)


def _chips_to_cols(sh):
    t = jnp.moveaxis(sh, 0, -2)
    return t.reshape(*t.shape[:-2], t.shape[-2] * t.shape[-1])


def _slot_in_empty(own, index, n):
    return lax.dynamic_update_slice(lax.empty((n,) + own.shape, own.dtype), own[None], (index,) + (0,) * own.ndim)


def _fold_pair(dg):
    return dg[0, :HEAD_DIM] + dg[0, HEAD_DIM:]


def _ffn_fwd(x, g, w_in_slab, w_out, tag):
    h = rms_fwd(x, g, name=f"{tag}_rms")
    a, b, act = swiglu_fwd(h, w_in_slab, name=f"{tag}_in")
    y = matmul(act, w_out, res=x, scale=0.5, tm=1024, tn=512, tk=w_out.shape[0], name=f"{tag}_out")
    return y, (x, h, a, b, act)


def _ffn_bwd(dy, saved, g, w_in_slab, w_out, tag):
    x, h, a, b, act = saved
    da, db = swiglu_bwd(dy, w_out, a, b, name=f"{tag}_dact")
    dw_out = grad_rows(act, dy, scale=0.5, name=f"{tag}_dwout")
    dw_in = grad_cols(h, da, db, name=f"{tag}_dwin")
    dx, dg = ffn_dh(da, db, w_in_slab, x, g, dy, name=f"{tag}_dh")
    return dx, dg[0], dw_in, dw_out


MEM_PAIRS = MEM_WIDTH // LANES


def _mem_attn_fwd(proj, mq0, mem_n, w_kv, g_q, g_k, tag):
    qh = pairnorm_fwd(proj, mq0, MEM_PAIRS, g_q, scale=QK_SCALE,name=f"{tag}_qnorm")
    kv = matmul(mem_n, w_kv, tm=256, tn=512, tk=1024, name=f"{tag}_kv")
    kh = pairnorm_fwd(kv, 0, MEM_PAIRS, g_k, name=f"{tag}_knorm")
    o, lse, _ = attn_fwd(qh, 0, kh, 0, kv, MEM_PAIRS, MEM_PAIRS, None, causal=False, name=f"{tag}_attn")
    return o, (qh, kv, kh, o, lse)


def _mem_attn_bwd(dmix, do0, proj, mq0, saved, mem_n, g_q, g_k, tag):
    qh, kv, kh, o, lse = saved
    delta = attn_delta(o, dmix, do0, MEM_PAIRS, name=f"{tag}_delta")
    (dqh, dkh, dv), _ = attn_bwd(qh, 0, kh, 0, kv, MEM_PAIRS, dmix, do0, MEM_PAIRS, lse, delta, None,
                                 causal=False, name=f"{tag}_dattn")
    dq_pre, dgq = pairnorm_bwd(proj, mq0, MEM_PAIRS, dqh, g_q, name=f"{tag}_dqnorm")
    dk_pre, dgk = pairnorm_bwd(kv, 0, MEM_PAIRS, dkh, g_k, name=f"{tag}_dknorm")
    dkv = jnp.concatenate([dk_pre, dv], axis=1)
    dw_kv = grad_rows(mem_n, dkv, name=f"{tag}_dwkv")
    return dq_pre, _fold_pair(dgq), _fold_pair(dgk), dw_kv, dkv


def _per_head_lanes(x, H):
    return jnp.pad(x.reshape(H, -1).T, ((0, 0), (0, LANES - H)))


def _fox_fwd(proj, b_f, g_q, g_k, tok, rider, tag):
    H, P = tok // HEAD_DIM, tok // LANES
    bias = jnp.pad(b_f.reshape(1, H), ((0, 0), (0, LANES - H)))
    qh = pairnorm_fwd(proj, 0, P, g_q, scale=QK_SCALE,name=f"{tag}_qnorm")
    kh = pairnorm_fwd(proj, P, P, g_k, name=f"{tag}_knorm")
    c = fgate_fwd(proj, 3 * P + MEM_PAIRS, bias, name=f"{tag}_fgate")
    decay = c[:, :H].T.reshape(P, 2, 1, c.shape[0])
    o, lse, rode = attn_fwd(qh, 0, kh, 0, proj, 2 * P, P, decay, causal=True, rider=rider, name=f"{tag}_attn")
    return o, (qh, kh, bias, decay, o, lse), rode


def _fox_bwd(dmix, proj, saved, g_q, g_k, tok, rider, tag):
    qh, kh, bias, decay, o, lse = saved
    H, P = tok // HEAD_DIM, tok // LANES
    delta = attn_delta(o, dmix, 0, P, name=f"{tag}_delta")
    (dqh, dkh, dv, dcs, drs), rode = attn_bwd(qh, 0, kh, 0, proj, 2 * P, dmix, 0, P, lse, delta, decay, causal=True,
                                              rider=rider, name=f"{tag}_dattn")
    dq_pre, dgq = pairnorm_bwd(proj, 0, P, dqh, g_q, name=f"{tag}_dqnorm")
    dk_pre, dgk = pairnorm_bwd(proj, P, P, dkh, g_k, name=f"{tag}_dknorm")
    dz, dbias = fgate_bwd(proj, 3 * P + MEM_PAIRS, bias, drs, _per_head_lanes(dcs, H), name=f"{tag}_dfgate")
    dqkv = jnp.concatenate([dq_pre, dk_pre, dv], axis=1)
    return dqkv, dz, dbias[0, :H], _fold_pair(dgq), _fold_pair(dgk), rode


def local_step(x, mem, target, W, comm=None):
    S, D = x.shape
    tok = D - MEM_WIDTH
    P = tok // LANES
    depth = W["norm_ffn1"].shape[0]
    mem_n = rms_fwd(mem, W["mem_norm"], name="mem_rms")
    saved = []
    for i in range(depth):
        kind, j = i % 2, i // 2
        t = f"l{i}"
        x1, s1 = _ffn_fwd(x, W["norm_ffn1"][i], W["ffn1_w_in"][i], W["ffn1_w_out"][i], f"{t}_ffn1")
        h = rms_fwd(x1, W["norm_mix"][i], name=f"{t}_mix_rms")
        w_mix = W["fox_w_in"][j] if kind == 0 else W["gmlp_w_in"][j]
        proj = matmul(h, w_mix, tm=1024, tn=896, tk=D, name=f"{t}_mix_in")
        if kind == 0:
            rider = comm.late_weights_rider() if (comm is not None and i == 0) else None
            o_tok, s_tok, rode = _fox_fwd(proj, W["fox_b_f"][j], W["fox_q_norm"][j], W["fox_k_norm"][j], tok, rider,
                                          f"{t}_fox")
            if rider is not None:
                comm.accept_late_weights(W, rode)
            mq0 = 3 * P
        else:
            vg, ws, bs = _gmlp_operands(W["gmlp_v_norm"][j], W["gmlp_w_s"][j], W["gmlp_b_s"][j])
            o_tok = gmlp_fwd(proj, P, P, vg, ws, bs, name=f"{t}_gmlp")
            s_tok = None
            mq0 = 2 * P
        o_mem, s_mem = _mem_attn_fwd(proj, mq0, mem_n, W["mem_w_kv"][i], W["mem_q_norm"][i], W["mem_k_norm"][i],
                                     f"{t}_mem")
        mix = jnp.concatenate([o_tok, o_mem], axis=1).astype(BF16)
        x2 = matmul(mix, W["w_out"][i], res=x1, tm=1024, tn=512, tk=D, name=f"{t}_mix_out")
        x3, s3 = _ffn_fwd(x2, W["norm_ffn2"][i], W["ffn2_w_in"][i], W["ffn2_w_out"][i], f"{t}_ffn2")
        saved.append((s1, x1, h, proj, mq0, s_tok, s_mem, mix, s3))
        x = x3

    dx, loss = loss_head(x, target, name="loss_head")

    G = {k: [None] * depth for k in ("norm_ffn1", "norm_mix", "norm_ffn2", "mem_q_norm", "mem_k_norm", "ffn1_w_in",
                                     "ffn1_w_out", "ffn2_w_in", "ffn2_w_out", "w_out", "mem_w_kv")}
    n_fox, n_gmlp = (depth + 1) // 2, depth // 2
    for k in ("fox_w_in", "fox_b_f", "fox_q_norm", "fox_k_norm"):
        G[k] = [None] * n_fox
    for k in ("gmlp_w_in", "gmlp_v_norm", "gmlp_w_s", "gmlp_b_s"):
        G[k] = [None] * n_gmlp
    dkv_all = [None] * depth
    for i in reversed(range(depth)):
        kind, j = i % 2, i // 2
        t = f"l{i}"
        s1, x1, h, proj, mq0, s_tok, s_mem, mix, s3 = saved[i]
        dx, G["norm_ffn2"][i], G["ffn2_w_in"][i], G["ffn2_w_out"][i] = _ffn_bwd(
            dx, s3, W["norm_ffn2"][i], W["ffn2_w_in"][i], W["ffn2_w_out"][i], f"{t}_ffn2")
        dmix = matmul(dx, W["w_out"][i], tb=True, tm=1024, tn=1024, tk=D, name=f"{t}_dmix")
        G["w_out"][i] = grad_rows(mix, dx, name=f"{t}_dwmixout")
        dmq, G["mem_q_norm"][i], G["mem_k_norm"][i], G["mem_w_kv"][i], dkv_all[i] = _mem_attn_bwd(
            dmix, P, proj, mq0, s_mem, mem_n, W["mem_q_norm"][i], W["mem_k_norm"][i], f"{t}_mem")
        if kind == 0:
            rider = comm.early_grads_rider(G) if (comm is not None and i == 0) else None
            dqkv, dz, G["fox_b_f"][j], G["fox_q_norm"][j], G["fox_k_norm"][j], rode = _fox_bwd(
                dmix, proj, s_tok, W["fox_q_norm"][j], W["fox_k_norm"][j], tok, rider, f"{t}_fox")
            if rider is not None:
                comm.accept_early_grads(rode)
            dproj = jnp.concatenate([dqkv, dmq, dz], axis=1).astype(BF16)
            w_mix, wkey = W["fox_w_in"][j], "fox_w_in"
        else:
            vg, ws, bs = _gmlp_operands(W["gmlp_v_norm"][j], W["gmlp_w_s"][j], W["gmlp_b_s"][j])
            dup, dvp, dws, dbs, dvg = gmlp_bwd(proj, P, P, vg, ws, jnp.swapaxes(ws, 2, 3), bs, dmix,
                                               name=f"{t}_dgmlp")
            G["gmlp_w_s"][j] = dws.reshape(W["gmlp_w_s"][j].shape)
            G["gmlp_b_s"][j] = dbs.reshape(W["gmlp_b_s"][j].shape)
            G["gmlp_v_norm"][j] = dvg.reshape(-1)
            dproj = jnp.concatenate([dup, dvp, dmq], axis=1).astype(BF16)
            w_mix, wkey = W["gmlp_w_in"][j], "gmlp_w_in"
        G[wkey][j] = matmul(h, dproj, ta=True, tm=1024, tn=896, tk=1024, name=f"{t}_dwmixin")
        dh = matmul(dproj, w_mix, tb=True, tm=1024, tn=1024, tk=896, name=f"{t}_dhmix")
        dx, dgm = rms_bwd(x1, dh, W["norm_mix"][i], dx, name=f"{t}_dmixrms")
        G["norm_mix"][i] = dgm[0]
        dx, G["norm_ffn1"][i], G["ffn1_w_in"][i], G["ffn1_w_out"][i] = _ffn_bwd(
            dx, s1, W["norm_ffn1"][i], W["ffn1_w_in"][i], W["ffn1_w_out"][i], f"{t}_ffn1")
    w_kv_all = jnp.concatenate([W["mem_w_kv"][i] for i in range(depth)], axis=1)
    dmem_n = matmul(jnp.concatenate(dkv_all, axis=1), w_kv_all, tb=True, tm=256, tn=512, tk=1024, name="dmem_n")
    _, dmemg = rms_bwd(mem, dmem_n, W["mem_norm"], None, name="dmem_rms")
    G["mem_norm"] = [dmemg[0]]
    return loss, dx, G


def _fox_cols_to_compute(w, tok):
    H = tok // HEAD_DIM
    qkv, f, mq = w[..., :3 * tok], w[..., 3 * tok:3 * tok + H], w[..., 3 * tok + H:]
    f = jnp.pad(f, [(0, 0)] * (w.ndim - 1) + [(0, LANES - H)])
    return jnp.concatenate([qkv, mq, f], axis=-1)


def _fox_cols_from_compute(w, tok):
    H = tok // HEAD_DIM
    qkv, mq, f = w[..., :3 * tok], w[..., 3 * tok:3 * tok + MEM_WIDTH], w[..., 3 * tok + MEM_WIDTH:3 * tok + MEM_WIDTH + H]
    return jnp.concatenate([qkv, f, mq], axis=-1)


_BIG = ("ffn1_w_in", "ffn1_w_out", "ffn2_w_in", "ffn2_w_out", "w_out", "mem_w_kv", "fox_w_in", "gmlp_w_in")
_SMALL = ("norm_ffn1", "norm_mix", "norm_ffn2", "mem_norm", "mem_q_norm", "mem_k_norm", "fox_b_f", "fox_q_norm",
          "fox_k_norm", "gmlp_v_norm", "gmlp_w_s", "gmlp_b_s")
WEIGHT_ORDER = ("norm_ffn1", "ffn1_w_in", "ffn1_w_out", "norm_mix", "norm_ffn2", "ffn2_w_in", "ffn2_w_out", "w_out",
                "mem_norm", "mem_w_kv", "mem_q_norm", "mem_k_norm", "fox_w_in", "fox_b_f", "fox_q_norm", "fox_k_norm",
                "gmlp_w_in", "gmlp_v_norm", "gmlp_w_s", "gmlp_b_s")


def _small_slab(rows_list, index):
    sizes = [s.shape[0] for s in rows_list]
    n_rows = [-(-n // LANES) for n in sizes]
    small = jnp.concatenate([jnp.pad(s, (0, r * LANES - n)).reshape(r, LANES)
                             for s, n, r in zip(rows_list, sizes, n_rows)], axis=0)
    small = jnp.pad(small, ((0, -small.shape[0] % 64), (0, 0)))
    return _slot_in_empty(small, index, 8), sizes, n_rows


_FIRST_WEIGHTS = (("ffn1_w_in", 0), ("ffn1_w_out", 0), ("fox_w_in", 0))


def _weight_from_slab(name, slab, tok):
    if name in ("ffn1_w_in", "ffn2_w_in"):
        return slab
    if name == "fox_w_in":
        return _fox_cols_to_compute(_chips_to_cols(slab), tok)
    if name == "gmlp_w_in":
        return _chips_to_cols(slab)
    return slab.reshape(4 * slab.shape[1], slab.shape[2])


def _grad_to_slab(name, g, tok):
    if name == "fox_w_in":
        return _cols_to_chips(_fox_cols_from_compute(g, tok))
    if name == "gmlp_w_in":
        return _cols_to_chips(g)
    return g


class _Exchange:
    def __init__(self, shards, tok, chip, core):
        self.tok, self.core = tok, core
        self.half = core.reshape(1).astype(jnp.int32)
        self.chip_id = chip.reshape(1).astype(jnp.int32)
        items = [(k, i) for k in _BIG for i in range(shards[k].shape[0])]
        self.slabs = {it: _slot_in_empty(shards[it[0]][it[1]].astype(BF16), chip, 4) for it in items}
        self.late = [it for it in items if it not in _FIRST_WEIGHTS]
        self.reduced = {}
        self.early = None

    def first_weights(self, small_slab):
        got, small_all = gather_weights([self.slabs[it] for it in _FIRST_WEIGHTS], small_slab, name="gather_first")
        return {it: _weight_from_slab(it[0], s, self.tok) for it, s in zip(_FIRST_WEIGHTS, got)}, small_all

    def late_weights_rider(self):
        return gather_rider([self.slabs[it] for it in self.late])

    def accept_late_weights(self, W, got):
        for (k, i), s in zip(self.late, got):
            W[k][i] = _weight_from_slab(k, s, self.tok)

    def _pair_sums(self, G, items, small_slab, tag):
        parts = [_grad_to_slab(k, G[k][i], self.tok) for k, i in items]
        landed, small_all = exchange_with_sibling(parts, small_slab, name=f"grad_exchange_{tag}")
        pair = [pair_sum(p, l, self.half, name=f"grad_pair_sum_{k}{i}") for (k, i), p, l in zip(items, parts, landed)]
        return pair, small_all

    def early_grads_rider(self, G):
        items = [(k, i) for k in _BIG for i in range(len(G[k])) if G[k][i] is not None]
        pair, _ = self._pair_sums(G, items, None, "early")
        self.early = (items, pair)
        return scatter_rider(pair)

    def accept_early_grads(self, landed):
        items, pair = self.early
        self._chip_sums(items, pair, landed)

    def _chip_sums(self, items, pair, landed):
        for (k, i), q, l in zip(items, pair, landed):
            self.reduced[(k, i)] = chip_sum(q, l, self.chip_id, name=f"grad_chip_sum_{k}{i}")

    def finish_grads(self, G, small_slab):
        items = [(k, i) for k in _BIG for i in range(len(G[k])) if (k, i) not in self.reduced]
        pair, small_all = self._pair_sums(G, items, small_slab, "late")
        self._chip_sums(items, pair, scatter_to_chips(pair, name="grad_scatter_late"))
        order = sorted(self.reduced)
        other = share_with_sibling([self.reduced[it] for it in order], name="grad_share")
        full = {}
        for it, a, b in zip(order, [self.reduced[it] for it in order], other):
            full[it] = jnp.where(self.core == 0, jnp.concatenate([a, b]), jnp.concatenate([b, a]))
        names = sorted({k for k, _ in order})
        return {k: jnp.stack([full[(k, i)] for i in range(len(G[k]))]) for k in names}, small_all


def kernel(x, mem, norm_ffn1, ffn1_w_in, ffn1_w_out, norm_mix, norm_ffn2, ffn2_w_in, ffn2_w_out, w_out, mem_norm, mem_w_kv, mem_q_norm, mem_k_norm, fox_w_in, fox_b_f, fox_q_norm, fox_k_norm, gmlp_w_in, gmlp_v_norm, gmlp_w_s, gmlp_b_s, loss_target, m_norm_ffn1, m_ffn1_w_in, m_ffn1_w_out, m_norm_mix, m_norm_ffn2, m_ffn2_w_in, m_ffn2_w_out, m_w_out, m_mem_norm, m_mem_w_kv, m_mem_q_norm, m_mem_k_norm, m_fox_w_in, m_fox_b_f, m_fox_q_norm, m_fox_k_norm, m_gmlp_w_in, m_gmlp_v_norm, m_gmlp_w_s, m_gmlp_b_s, v_norm_ffn1, v_ffn1_w_in, v_ffn1_w_out, v_norm_mix, v_norm_ffn2, v_ffn2_w_in, v_ffn2_w_out, v_w_out, v_mem_norm, v_mem_w_kv, v_mem_q_norm, v_mem_k_norm, v_fox_w_in, v_fox_b_f, v_fox_q_norm, v_fox_k_norm, v_gmlp_w_in, v_gmlp_v_norm, v_gmlp_w_s, v_gmlp_b_s):
    w = dict(norm_ffn1=norm_ffn1, ffn1_w_in=ffn1_w_in, ffn1_w_out=ffn1_w_out, norm_mix=norm_mix, norm_ffn2=norm_ffn2,
             ffn2_w_in=ffn2_w_in, ffn2_w_out=ffn2_w_out, w_out=w_out, mem_norm=mem_norm, mem_w_kv=mem_w_kv,
             mem_q_norm=mem_q_norm, mem_k_norm=mem_k_norm, fox_w_in=fox_w_in, fox_b_f=fox_b_f, fox_q_norm=fox_q_norm,
             fox_k_norm=fox_k_norm, gmlp_w_in=gmlp_w_in, gmlp_v_norm=gmlp_v_norm, gmlp_w_s=gmlp_w_s, gmlp_b_s=gmlp_b_s)
    m = dict(norm_ffn1=m_norm_ffn1, ffn1_w_in=m_ffn1_w_in, ffn1_w_out=m_ffn1_w_out, norm_mix=m_norm_mix,
             norm_ffn2=m_norm_ffn2, ffn2_w_in=m_ffn2_w_in, ffn2_w_out=m_ffn2_w_out, w_out=m_w_out, mem_norm=m_mem_norm,
             mem_w_kv=m_mem_w_kv, mem_q_norm=m_mem_q_norm, mem_k_norm=m_mem_k_norm, fox_w_in=m_fox_w_in,
             fox_b_f=m_fox_b_f, fox_q_norm=m_fox_q_norm, fox_k_norm=m_fox_k_norm, gmlp_w_in=m_gmlp_w_in,
             gmlp_v_norm=m_gmlp_v_norm, gmlp_w_s=m_gmlp_w_s, gmlp_b_s=m_gmlp_b_s)
    v = dict(norm_ffn1=v_norm_ffn1, ffn1_w_in=v_ffn1_w_in, ffn1_w_out=v_ffn1_w_out, norm_mix=v_norm_mix,
             norm_ffn2=v_norm_ffn2, ffn2_w_in=v_ffn2_w_in, ffn2_w_out=v_ffn2_w_out, w_out=v_w_out, mem_norm=v_mem_norm,
             mem_w_kv=v_mem_w_kv, mem_q_norm=v_mem_q_norm, mem_k_norm=v_mem_k_norm, fox_w_in=v_fox_w_in,
             fox_b_f=v_fox_b_f, fox_q_norm=v_fox_q_norm, fox_k_norm=v_fox_k_norm, gmlp_w_in=v_gmlp_w_in,
             gmlp_v_norm=v_gmlp_v_norm, gmlp_w_s=v_gmlp_w_s, gmlp_b_s=v_gmlp_b_s)
    D = x.shape[-1]
    tok = D - MEM_WIDTH
    xi, yi, ci = _position()
    chip = 2 * xi + yi

    device = 4 * xi + 2 * yi + ci

    comm = _Exchange(w, tok, chip, ci)
    vn = w["gmlp_v_norm"]
    vn_slab, _, _ = _small_slab([vn.reshape(-1)], device)
    first, vn_all = comm.first_weights(vn_slab)
    W = {k: w[k] for k in _SMALL}
    W["gmlp_v_norm"] = _chips_to_cols(vn_all[0::2].reshape(4, -1)[:, :vn.size].reshape((4,) + vn.shape))
    for k in _BIG:
        W[k] = [first.get((k, i)) for i in range(w[k].shape[0])]

    loss, grad_x, g = local_step(x[0], mem[0], loss_target[0], W, comm)

    small_list = [jnp.stack(g[k]).reshape(-1) for k in _SMALL] + [loss.reshape(-1)]
    small, small_sizes, small_rows = _small_slab(small_list, device)
    red, small_all = comm.finish_grads(g, small)
    small_sum = ordered_sum(small_all, name="small_sum")
    off = 0
    for k, n, r in zip(_SMALL, small_sizes, small_rows):
        red[k] = small_sum[off:off + r].reshape(-1)[:n].reshape((-1,) + w[k].shape[1:] if k != "gmlp_v_norm"
                                                                else (w[k].shape[0], -1))
        off += r
    loss_total = small_sum[off, 0]
    vn_cols = w["gmlp_v_norm"].shape[-1]
    red["gmlp_v_norm"] = lax.dynamic_slice_in_dim(red["gmlp_v_norm"], chip * vn_cols, vn_cols, axis=-1)

    deltas, new_m, new_v = {}, {}, {}
    for k in WEIGHT_ORDER:
        wk = w[k] if w[k].ndim > 1 else w[k].reshape(1, -1)
        upd = adamw(wk, red[k].reshape(wk.shape), m[k].reshape(wk.shape), v[k].reshape(wk.shape), name=f"adamw_{k}")
        deltas[k], new_m[k], new_v[k] = (u.reshape(w[k].shape) for u in upd)
    return (loss_total, grad_x[None], *[red[k].reshape(w[k].shape) for k in WEIGHT_ORDER],
            *[deltas[k] for k in WEIGHT_ORDER], *[new_m[k] for k in WEIGHT_ORDER], *[new_v[k] for k in WEIGHT_ORDER])
```

```python
import functools
import math
from typing import Callable, NamedTuple

import jax
import jax.numpy as jnp
from jax import lax
from jax.experimental import pallas as pl
from jax.experimental.pallas import tpu as pltpu

F32 = jnp.float32
BF16 = jnp.bfloat16
EPS = 1e-6
HEAD_DIM = 64
MEM_WIDTH = 256
CHUNK = 128
LANES = 128
NEG = -1e30
VMEM_LIMIT_BYTES = 56 * 1024 * 1024
ATTN_Q_BLOCK = 1024
ATTN_K_BLOCK = 1024
ATTN_ROW_CHUNK = 1024
QK_SCALE = 0.125
MESH_ID = pl.DeviceIdType.MESH

ADAM_LR = 0.001
ADAM_B1 = 0.9
ADAM_B2 = 0.999
ADAM_EPS = 1e-08
ADAM_WD = 0.01
ADAM_STEP = 10


def _tile(n, pref, align):
    t = (min(pref, n) // align) * align
    while t >= align:
        if n % t == 0:
            return t
        t -= align
    return n


def _params(sem):
    return pltpu.CompilerParams(dimension_semantics=sem, vmem_limit_bytes=VMEM_LIMIT_BYTES)


def _dot(a, b, ca, cb):
    return lax.dot_general(a, b, (((ca,), (cb,)), ((), ())), preferred_element_type=F32)


def _sigmoid(x):
    return 1.0 / (1.0 + jnp.exp(-x))


_GELU_C = math.sqrt(2.0 / math.pi)


def _gelu(x):
    return 0.5 * x * (1.0 + jnp.tanh(_GELU_C * (x + 0.044715 * (x * x * x))))


def _gelu_grad(x):
    t = jnp.tanh(_GELU_C * (x + 0.044715 * (x * x * x)))
    return 0.5 * (1.0 + t) + 0.5 * x * (1.0 - t * t) * (_GELU_C * (1.0 + 3.0 * 0.044715 * (x * x)))


def matmul(a, b, *, ta=False, tb=False, out_dtype=F32, scale=None, res=None,
           tm=1024, tn=512, tk=1024, name):
    if ta:
        K, M = a.shape
    else:
        M, K = a.shape
    N = b.shape[0] if tb else b.shape[1]
    tm = _tile(M, tm, LANES if ta else 16)
    tn = _tile(N, tn, LANES)
    tk = _tile(K, tk, LANES)
    nk = K // tk
    a_spec = pl.BlockSpec((tk, tm), lambda i, j, k: (k, i)) if ta else pl.BlockSpec((tm, tk), lambda i, j, k: (i, k))
    b_spec = pl.BlockSpec((tn, tk), lambda i, j, k: (j, k)) if tb else pl.BlockSpec((tk, tn), lambda i, j, k: (k, j))
    o_spec = pl.BlockSpec((tm, tn), lambda i, j, k: (i, j))
    ca, cb = (0 if ta else 1), (1 if tb else 0)
    has_res = res is not None

    def body(*refs):
        a_ref, b_ref = refs[0], refs[1]
        res_ref = refs[2] if has_res else None
        o_ref = refs[3] if has_res else refs[2]
        acc_ref = refs[-1]
        k = pl.program_id(2)
        prod = _dot(a_ref[...].astype(BF16), b_ref[...].astype(BF16), ca, cb)

        def finish(acc):
            if scale is not None:
                acc = acc * scale
            if has_res:
                acc = res_ref[...] + acc
            o_ref[...] = acc.astype(out_dtype)

        if nk == 1:
            finish(prod)
        else:
            @pl.when(k == 0)
            def _():
                acc_ref[...] = prod

            @pl.when(k > 0)
            def _():
                acc_ref[...] += prod

            @pl.when(k == nk - 1)
            def _():
                finish(acc_ref[...])

    in_specs = [a_spec, b_spec] + ([o_spec] if has_res else [])
    args = (a, b) + ((res,) if has_res else ())
    return pl.pallas_call(
        body, grid=(M // tm, N // tn, nk), in_specs=in_specs, out_specs=o_spec,
        out_shape=jax.ShapeDtypeStruct((M, N), out_dtype),
        scratch_shapes=[pltpu.VMEM((tm, tn) if nk > 1 else (8, LANES), F32)],
        compiler_params=_params(("parallel", "parallel", "arbitrary")), name=name)(*args)


def swiglu_fwd(h, w_slab, *, name):
    S, D = h.shape
    Fc = w_slab.shape[-1]
    tm = _tile(S, 512, 16)

    def body(h_ref, wa_ref, wb_ref, a_ref, b_ref, act_ref):
        hv = h_ref[...]
        a = _dot(hv, wa_ref[...], 1, 0)
        b = _dot(hv, wb_ref[...], 1, 0)
        a_ref[...] = a.astype(BF16)
        b_ref[...] = b.astype(BF16)
        act_ref[...] = (a * _sigmoid(a) * b).astype(BF16)

    out = pl.BlockSpec((tm, Fc), lambda j, i: (i, j))
    return pl.pallas_call(
        body, grid=(2, S // tm),
        in_specs=[pl.BlockSpec((tm, D), lambda j, i: (i, 0)),
                  pl.BlockSpec((None, D, Fc), lambda j, i: (j, 0, 0)),
                  pl.BlockSpec((None, D, Fc), lambda j, i: (j + 2, 0, 0))],
        out_specs=[out, out, out],
        out_shape=[jax.ShapeDtypeStruct((S, 2 * Fc), BF16)] * 3,
        compiler_params=_params(("parallel", "parallel")), name=name)(h, w_slab, w_slab)


def swiglu_bwd(dy, w_out, a, b, *, name):
    S, D = dy.shape
    F = w_out.shape[0]
    fc = F // 2
    tm = _tile(S, 512, 16)

    def body(dy_ref, w_ref, a_ref, b_ref, da_ref, db_ref):
        dact = 0.5 * _dot(dy_ref[...].astype(BF16), w_ref[...], 1, 1)
        av = a_ref[...].astype(F32)
        sg = _sigmoid(av)
        da_ref[...] = (dact * b_ref[...].astype(F32) * (sg * (1.0 + av * (1.0 - sg)))).astype(BF16)
        db_ref[...] = (dact * (av * sg)).astype(BF16)

    blk = pl.BlockSpec((tm, fc), lambda j, i: (i, j))
    return pl.pallas_call(
        body, grid=(2, S // tm),
        in_specs=[pl.BlockSpec((tm, D), lambda j, i: (i, 0)), pl.BlockSpec((fc, D), lambda j, i: (j, 0)), blk, blk],
        out_specs=[blk, blk],
        out_shape=[jax.ShapeDtypeStruct((S, F), BF16), jax.ShapeDtypeStruct((S, F), BF16)],
        compiler_params=_params(("parallel", "parallel")), name=name)(dy, w_out, a, b)


def ffn_dh(da, db, w_slab, x, g, dy, *, name):
    S, F = da.shape
    D, Fc = w_slab.shape[-2:]
    tm = _tile(S, 1024, 16)
    sub = _tile(tm, 256, 8)

    def body(da_ref, db_ref, w_ref, x_ref, g_ref, dy_ref, dx_ref, dg_ref, acc_ref):
        i, k = pl.program_id(0), pl.program_id(1)

        @pl.when(k == 0)
        def _():
            acc_ref[...] = jnp.zeros_like(acc_ref)

        @pl.when(k < 2)
        def _():
            acc_ref[...] += _dot(da_ref[...], w_ref[...], 1, 1)

        @pl.when(k >= 2)
        def _():
            acc_ref[...] += _dot(db_ref[...], w_ref[...], 1, 1)

        @pl.when(k == 3)
        def _():
            part = None
            for c in range(tm // sub):
                rows = pl.ds(c * sub, sub)
                xv, dh = x_ref[rows, :], acc_ref[rows, :]
                r = lax.rsqrt(jnp.mean(xv * xv, axis=-1, keepdims=True) + EPS)
                u = dh * g_ref[...]
                dx_ref[rows, :] = dy_ref[rows, :] + (r * u - xv * (r * r * r) * jnp.mean(xv * u, axis=-1, keepdims=True))
                p = jnp.sum(dh * xv * r, axis=0, keepdims=True)
                part = p if part is None else part + p

            @pl.when(i == 0)
            def _():
                dg_ref[...] = part

            @pl.when(i > 0)
            def _():
                dg_ref[...] += part

    row = pl.BlockSpec((tm, D), lambda i, k: (i, 0))
    vec = pl.BlockSpec((1, D), lambda i, k: (0, 0))
    return pl.pallas_call(
        body, grid=(S // tm, 4),
        in_specs=[pl.BlockSpec((tm, Fc), lambda i, k: (i, jnp.minimum(k, 1))),
                  pl.BlockSpec((tm, Fc), lambda i, k: (i, jnp.maximum(k - 2, 0))),
                  pl.BlockSpec((None, D, Fc), lambda i, k: (k, 0, 0)), row, vec, row],
        out_specs=[row, vec],
        out_shape=[jax.ShapeDtypeStruct((S, D), F32), jax.ShapeDtypeStruct((1, D), F32)],
        scratch_shapes=[pltpu.VMEM((tm, D), F32)],
        compiler_params=_params(("arbitrary", "arbitrary")), name=name)(da, db, w_slab, x, g.reshape(1, D), dy)


def grad_cols(h, da, db, *, name):
    S, D = h.shape
    Fc = da.shape[1] // 2
    tk = _tile(S, 1024, 16)
    nk = S // tk

    def body(h_ref, da_ref, db_ref, o_ref, acc_ref):
        ch, k = pl.program_id(0), pl.program_id(1)

        @pl.when(k == 0)
        def _():
            acc_ref[...] = jnp.zeros_like(acc_ref)

        @pl.when(ch < 2)
        def _():
            acc_ref[...] += _dot(h_ref[...], da_ref[...], 0, 0)

        @pl.when(ch >= 2)
        def _():
            acc_ref[...] += _dot(h_ref[...], db_ref[...], 0, 0)

        @pl.when(k == nk - 1)
        def _():
            o_ref[...] = acc_ref[...]

    return pl.pallas_call(
        body, grid=(4, nk),
        in_specs=[pl.BlockSpec((tk, D), lambda ch, k: (k, 0)),
                  pl.BlockSpec((tk, Fc), lambda ch, k: (jnp.where(ch < 2, k, 0), jnp.minimum(ch, 1))),
                  pl.BlockSpec((tk, Fc), lambda ch, k: (jnp.where(ch >= 2, k, 0), jnp.maximum(ch - 2, 0)))],
        out_specs=pl.BlockSpec((None, D, Fc), lambda ch, k: (ch, 0, 0)),
        out_shape=jax.ShapeDtypeStruct((4, D, Fc), F32),
        scratch_shapes=[pltpu.VMEM((D, Fc), F32)],
        compiler_params=_params(("parallel", "arbitrary")), name=name)(h, da, db)


def grad_rows(a, b, *, scale=None, name):
    S, M = a.shape
    N = b.shape[1]
    R = M // 4
    tn = _tile(N, 512, LANES)
    tk = _tile(S, 1024, 16)
    nk = S // tk

    def body(a_ref, b_ref, o_ref, acc_ref):
        k = pl.program_id(1)

        @pl.when(k == 0)
        def _():
            acc_ref[...] = jnp.zeros_like(acc_ref)

        acc_ref[...] += _dot(a_ref[...].astype(BF16), b_ref[...].astype(BF16), 0, 0)

        @pl.when(k == nk - 1)
        def _():
            for d in range(4):
                part = acc_ref[d * R:(d + 1) * R, :]
                o_ref[d] = part if scale is None else part * scale

    return pl.pallas_call(
        body, grid=(N // tn, nk),
        in_specs=[pl.BlockSpec((tk, M), lambda j, k: (k, 0)), pl.BlockSpec((tk, tn), lambda j, k: (k, j))],
        out_specs=pl.BlockSpec((4, R, tn), lambda j, k: (0, 0, j)),
        out_shape=jax.ShapeDtypeStruct((4, R, N), F32),
        scratch_shapes=[pltpu.VMEM((M, tn), F32)],
        compiler_params=_params(("parallel", "arbitrary")), name=name)(a, b)


def rms_fwd(x, g, *, name):
    S, D = x.shape
    ts = _tile(S, 1024, 16)

    def body(x_ref, g_ref, h_ref):
        xv = x_ref[...]
        r = lax.rsqrt(jnp.mean(xv * xv, axis=-1, keepdims=True) + EPS)
        h_ref[...] = (xv * r * g_ref[...]).astype(BF16)

    return pl.pallas_call(
        body, grid=(S // ts,),
        in_specs=[pl.BlockSpec((ts, D), lambda i: (i, 0)), pl.BlockSpec((1, D), lambda i: (0, 0))],
        out_specs=pl.BlockSpec((ts, D), lambda i: (i, 0)),
        out_shape=jax.ShapeDtypeStruct((S, D), BF16),
        compiler_params=_params(("parallel",)), name=name)(x, g.reshape(1, D))


def rms_bwd(x, dh, g, res, *, name):
    S, D = x.shape
    ts = _tile(S, 512, 16)
    has_res = res is not None

    def body(*refs):
        x_ref, dh_ref, g_ref = refs[:3]
        res_ref = refs[3] if has_res else None
        dx_ref, dg_ref = refs[-2:]
        i = pl.program_id(0)
        xv, dhv = x_ref[...], dh_ref[...].astype(F32)
        r = lax.rsqrt(jnp.mean(xv * xv, axis=-1, keepdims=True) + EPS)
        u = dhv * g_ref[...]
        dx = r * u - xv * (r * r * r) * jnp.mean(xv * u, axis=-1, keepdims=True)
        if has_res:
            dx = res_ref[...] + dx
        dx_ref[...] = dx
        part = jnp.sum(dhv * xv * r, axis=0, keepdims=True)

        @pl.when(i == 0)
        def _():
            dg_ref[...] = part

        @pl.when(i > 0)
        def _():
            dg_ref[...] += part

    row = pl.BlockSpec((ts, D), lambda i: (i, 0))
    vec = pl.BlockSpec((1, D), lambda i: (0, 0))
    args = (x, dh, g.reshape(1, D)) + ((res,) if has_res else ())
    return pl.pallas_call(
        body, grid=(S // ts,), in_specs=[row, row, vec] + ([row] if has_res else []),
        out_specs=[row, vec],
        out_shape=[jax.ShapeDtypeStruct((S, D), F32), jax.ShapeDtypeStruct((1, D), F32)],
        compiler_params=_params(("arbitrary",)), name=name)(*args)


def _low_half(shape):
    return lax.broadcasted_iota(jnp.int32, shape, len(shape) - 1) < HEAD_DIM


def _half_sums(x, low):
    sa = jnp.sum(jnp.where(low, x, 0.0), axis=1, keepdims=True)
    sb = jnp.sum(jnp.where(low, 0.0, x), axis=1, keepdims=True)
    return jnp.where(low, sa, sb)


def pairnorm_fwd(x, col0, n_pairs, g, *, scale=None, name):
    S = x.shape[0]
    ts = _tile(S, 512, 16)
    W = n_pairs * LANES
    assert col0 % n_pairs == 0

    def body(x_ref, g_ref, o_ref):
        for p in range(n_pairs):
            cols = pl.ds(p * LANES, LANES)
            xv = x_ref[:, cols]
            r = lax.rsqrt(_half_sums(xv * xv, _low_half(xv.shape)) * (1.0 / HEAD_DIM) + EPS)
            y = xv * r * g_ref[...]
            o_ref[:, cols] = (y if scale is None else y * scale).astype(BF16)

    return pl.pallas_call(
        body, grid=(S // ts,),
        in_specs=[pl.BlockSpec((ts, W), lambda i: (i, col0 // n_pairs)), pl.BlockSpec((1, LANES), lambda i: (0, 0))],
        out_specs=pl.BlockSpec((ts, W), lambda i: (i, 0)),
        out_shape=jax.ShapeDtypeStruct((S, W), BF16),
        compiler_params=_params(("parallel",)), name=name)(x, jnp.tile(g.reshape(1, HEAD_DIM), (1, 2)))


def pairnorm_bwd(x, col0, n_pairs, dy, g, *, name):
    S = x.shape[0]
    ts = _tile(S, 512, 16)
    W = n_pairs * LANES
    assert col0 % n_pairs == 0

    def body(x_ref, dy_ref, g_ref, dx_ref, dg_ref):
        part = None
        for p in range(n_pairs):
            cols = pl.ds(p * LANES, LANES)
            xv, dyv = x_ref[:, cols], dy_ref[:, cols]
            low = _low_half(xv.shape)
            r = lax.rsqrt(_half_sums(xv * xv, low) * (1.0 / HEAD_DIM) + EPS)
            u = dyv * g_ref[...]
            dx_ref[:, cols] = r * u - xv * (r * r * r) * (_half_sums(xv * u, low) * (1.0 / HEAD_DIM))
            pp = jnp.sum(dyv * xv * r, axis=0, keepdims=True)
            part = pp if part is None else part + pp

        @pl.when(pl.program_id(0) == 0)
        def _():
            dg_ref[...] = part

        @pl.when(pl.program_id(0) > 0)
        def _():
            dg_ref[...] += part

    vec = pl.BlockSpec((1, LANES), lambda i: (0, 0))
    blk = pl.BlockSpec((ts, W), lambda i: (i, 0))
    return pl.pallas_call(
        body, grid=(S // ts,),
        in_specs=[pl.BlockSpec((ts, W), lambda i: (i, col0 // n_pairs)), blk, vec], out_specs=[blk, vec],
        out_shape=[jax.ShapeDtypeStruct((S, W), F32), jax.ShapeDtypeStruct((1, LANES), F32)],
        compiler_params=_params(("arbitrary",)), name=name)(x, dy, jnp.tile(g.reshape(1, HEAD_DIM), (1, 2)))


def _split3(x):
    x1 = x.astype(BF16)
    r1 = x - x1.astype(F32)
    x2 = r1.astype(BF16)
    x3 = (r1 - x2.astype(F32)).astype(BF16)
    return x1, x2, x3


def _tri_ones(n, lower):
    r = lax.broadcasted_iota(jnp.int32, (n, n), 0)
    c = lax.broadcasted_iota(jnp.int32, (n, n), 1)
    return jnp.where((c <= r) if lower else (c >= r), 1.0, 0.0).astype(BF16)


def fgate_fwd(z, col0, bias, *, name):
    S, L = z.shape[0], LANES
    tb = _tile(S, 256, 16)

    def body(z_ref, b_ref, c_ref, carry):
        i = pl.program_id(0)

        @pl.when(i == 0)
        def _():
            carry[...] = jnp.zeros_like(carry)

        zz = z_ref[...] + b_ref[...]
        lf = jnp.minimum(zz, 0.0) - jnp.log(1.0 + jnp.exp(-jnp.abs(zz)))
        tri = _tri_ones(tb, True)
        x1, x2, x3 = _split3(lf)
        c = (_dot(tri, x1, 1, 0) + _dot(tri, x2, 1, 0)) + _dot(tri, x3, 1, 0) + carry[...]
        c_ref[...] = c
        carry[...] += jnp.sum(lf, axis=0, keepdims=True)

    return pl.pallas_call(
        body, grid=(S // tb,),
        in_specs=[pl.BlockSpec((tb, L), lambda i: (i, col0)), pl.BlockSpec((1, L), lambda i: (0, 0))],
        out_specs=pl.BlockSpec((tb, L), lambda i: (i, 0)),
        out_shape=jax.ShapeDtypeStruct((S, L), F32),
        scratch_shapes=[pltpu.VMEM((1, L), F32)],
        compiler_params=_params(("arbitrary",)), name=name)(z, bias)


def fgate_bwd(z, col0, bias, drs, dcs, *, name):
    S, L = z.shape[0], LANES
    n_pairs = drs.shape[0]
    tb = _tile(S, 256, 16)
    nb = S // tb

    def body(z_ref, b_ref, drs_ref, dcs_ref, dz_ref, db_ref, carry):
        i = pl.program_id(0)

        @pl.when(i == 0)
        def _():
            carry[...] = jnp.zeros_like(carry)

        tri = _tri_ones(tb, False)
        lane = lax.broadcasted_iota(jnp.int32, (tb, L), 1)
        dc = -dcs_ref[...]
        for h in range(2 * n_pairs):
            dc = dc + jnp.where(lane == h, jnp.sum(drs_ref[h // 2, h % 2], axis=1, keepdims=True), 0.0)
        x1, x2, x3 = _split3(dc)
        dlf = (_dot(tri, x1, 1, 0) + _dot(tri, x2, 1, 0)) + _dot(tri, x3, 1, 0) + carry[...]
        carry[...] += jnp.sum(dc, axis=0, keepdims=True)
        dz = dlf * _sigmoid(-(z_ref[...] + b_ref[...]))
        dz_ref[...] = dz
        part = jnp.sum(dz, axis=0, keepdims=True)

        @pl.when(i == 0)
        def _():
            db_ref[...] = part

        @pl.when(i > 0)
        def _():
            db_ref[...] += part

    rev = pl.BlockSpec((tb, L), lambda i: (nb - 1 - i, 0))
    vec = pl.BlockSpec((1, L), lambda i: (0, 0))
    return pl.pallas_call(
        body, grid=(nb,),
        in_specs=[pl.BlockSpec((tb, L), lambda i: (nb - 1 - i, col0)), vec,
                  pl.BlockSpec((n_pairs, 2, tb, L), lambda i: (0, 0, nb - 1 - i, 0)), rev],
        out_specs=[rev, vec],
        out_shape=[jax.ShapeDtypeStruct((S, L), F32), jax.ShapeDtypeStruct((1, L), F32)],
        scratch_shapes=[pltpu.VMEM((1, L), F32)],
        compiler_params=_params(("arbitrary",)), name=name)(z, bias, drs, dcs)


def _one_head(x, low, a):
    return jnp.where(low if a == 0 else jnp.logical_not(low), x, jnp.zeros_like(x))


class Rider(NamedTuple):
    inputs: tuple
    out_shapes: tuple
    aliases: dict
    sems: tuple
    plan: Callable


def _with_rider(rider, n_in, n_out, n_scratch):
    if rider is None:
        return [], [], [], [], {}, lambda refs: (refs[:n_in], refs[n_in:n_in + n_out], refs[n_in + n_out:], None)
    e_in, e_out = len(rider.inputs), len(rider.out_shapes)

    def split(refs):
        ins, r_in = refs[:n_in], refs[n_in:n_in + e_in]
        o0 = n_in + e_in
        outs, r_out = refs[o0:o0 + n_out], refs[o0 + n_out:o0 + n_out + e_out]
        s0 = o0 + n_out + e_out
        return ins, outs, refs[s0:s0 + n_scratch], rider.plan(r_in, r_out, refs[s0 + n_scratch:])

    aliases = {n_in + a: n_out + b for a, b in rider.aliases.items()}
    return list(rider.inputs), [_ANY] * e_in, list(rider.out_shapes), [_ANY] * e_out, aliases, split


def attn_fwd(q, q0, k, k0, v, v0, n_pairs, decay, *, causal, rider=None, name):
    Sq, Sk = q.shape[0], k.shape[0]
    tq = _tile(Sq, ATTN_Q_BLOCK, LANES)
    tk = _tile(Sk, ATTN_K_BLOCK, LANES)
    nq, nk = Sq // tq, Sk // tk
    bias = decay is not None
    rs = _tile(tq, ATTN_ROW_CHUNK, 16)
    r_args, r_in_specs, r_shapes, r_out_specs, aliases, split = _with_rider(rider, 4 if bias else 3, 2, 4)

    def row_sum_lanes(acc, low, a):
        other = jnp.logical_not(low) if a == 0 else low
        return jnp.max(jnp.where(other, acc, 0.0), axis=1, keepdims=True)

    live = [(i, j) for i in range(nq) for j in range(nk) if not causal or j * tk <= i * tq + tq - 1]
    n_live = len(live)

    def body(i_tab, j_tab, *refs):
        ins, (o_ref, lse_ref), scratch, ride = split(refs)
        m_sc, acc_sc = scratch[:2], scratch[2:]
        q_ref, k_ref, v_ref = ins[:3]
        ck_ref = ins[3] if bias else None
        pr, t = pl.program_id(0), pl.program_id(1)
        i, j = i_tab[t], j_tab[t]
        last_j = (i * tq + tq - 1) // tk if causal else nk - 1
        if ride is not None:
            pl.when(jnp.logical_and(pr == 0, t == 0))(ride[0])

        @pl.when(j == 0)
        def _():
            for a in range(2):
                m_sc[a][...] = jnp.full_like(m_sc[a], NEG)
                acc_sc[a][...] = jnp.zeros_like(acc_sc[a])

        def compute(masked):
            kv, vv = k_ref[...], v_ref[...].astype(BF16)
            low_k = _low_half(kv.shape)
            va = [jnp.where(low_k if a == 0 else jnp.logical_not(low_k), vv, jnp.ones_like(vv)) for a in range(2)]
            for r in range(tq // rs):
                rows = pl.ds(r * rs, rs)
                qv = q_ref[rows, :]
                low = _low_half(qv.shape)
                for a in range(2):
                    s = _dot(_one_head(qv, low, a), kv, 1, 1)
                    if bias:
                        s = s - ck_ref[a]
                    if masked:
                        row = i * tq + r * rs + lax.broadcasted_iota(jnp.int32, (rs, tk), 0)
                        col = j * tk + lax.broadcasted_iota(jnp.int32, (rs, tk), 1)
                        s = jnp.where(col <= row, s, NEG)
                    m_prev = m_sc[a][rows, :]
                    m_new = jnp.maximum(m_prev, jnp.max(s, axis=1, keepdims=True))
                    alpha = jnp.exp(m_prev - m_new)
                    p = jnp.exp(s - m_new).astype(BF16)
                    acc_sc[a][rows, :] = alpha * acc_sc[a][rows, :] + _dot(p, va[a], 1, 0)
                    m_sc[a][rows, :] = m_new

        if causal:
            crosses = j * tk + (tk - 1) > i * tq
            pl.when(crosses)(functools.partial(compute, True))
            pl.when(jnp.logical_not(crosses))(functools.partial(compute, False))
        else:
            compute(False)

        @pl.when(j == last_j)
        def _():
            low = _low_half((tq, LANES))
            l = [row_sum_lanes(acc_sc[a][...], low, a) for a in range(2)]
            o_ref[...] = jnp.where(low, acc_sc[0][...] / l[0], acc_sc[1][...] / l[1])
            for a in range(2):
                lse_ref[a] = m_sc[a][...] + jnp.log(l[a])

        if ride is not None:
            pl.when(jnp.logical_and(pr == n_pairs - 1, t == n_live - 1))(ride[1])

    in_specs = [pl.BlockSpec((tq, LANES), lambda p, t, it, jt: (it[t], q0 + p)),
                pl.BlockSpec((tk, LANES), lambda p, t, it, jt: (jt[t], k0 + p)),
                pl.BlockSpec((tk, LANES), lambda p, t, it, jt: (jt[t], v0 + p))]
    args = [q, k, v]
    if bias:
        in_specs.append(pl.BlockSpec((None, 2, 1, tk), lambda p, t, it, jt: (p, 0, 0, jt[t])))
        args.append(decay)
    tabs = [jnp.asarray([b[n] for b in live], jnp.int32) for n in range(2)]
    out = pl.pallas_call(
        body,
        grid_spec=pltpu.PrefetchScalarGridSpec(
            num_scalar_prefetch=2, grid=(n_pairs, n_live), in_specs=in_specs + r_in_specs,
            out_specs=[pl.BlockSpec((tq, LANES), lambda p, t, it, jt: (it[t], p)),
                       pl.BlockSpec((None, 2, tq, 1), lambda p, t, it, jt: (p, 0, it[t], 0))] + r_out_specs,
            scratch_shapes=[pltpu.VMEM((tq, 1), F32)] * 2 + [pltpu.VMEM((tq, LANES), F32)] * 2
            + (list(rider.sems) if rider else [])),
        out_shape=[jax.ShapeDtypeStruct((Sq, n_pairs * LANES), F32),
                   jax.ShapeDtypeStruct((n_pairs, 2, Sq, 1), F32)] + r_shapes,
        input_output_aliases={2 + a: b for a, b in aliases.items()},
        compiler_params=_params(("arbitrary", "arbitrary") if rider else ("parallel", "arbitrary")),
        name=name)(*tabs, *args, *r_args)
    return out[0], out[1], out[2:]


def attn_delta(o, do, do0, n_pairs, *, name):
    S = o.shape[0]
    ts = _tile(S, 512, 16)
    W = n_pairs * LANES
    assert do0 % n_pairs == 0

    def body(o_ref, do_ref, out_ref):
        for p in range(n_pairs):
            cols = pl.ds(p * LANES, LANES)
            prod = o_ref[:, cols] * do_ref[:, cols]
            low = _low_half(prod.shape)
            out_ref[p, 0] = jnp.sum(jnp.where(low, prod, 0.0), axis=1, keepdims=True)
            out_ref[p, 1] = jnp.sum(jnp.where(low, 0.0, prod), axis=1, keepdims=True)

    return pl.pallas_call(
        body, grid=(S // ts,),
        in_specs=[pl.BlockSpec((ts, W), lambda i: (i, 0)), pl.BlockSpec((ts, W), lambda i: (i, do0 // n_pairs))],
        out_specs=pl.BlockSpec((n_pairs, 2, ts, 1), lambda i: (0, 0, i, 0)),
        out_shape=jax.ShapeDtypeStruct((n_pairs, 2, S, 1), F32),
        compiler_params=_params(("parallel",)), name=name)(o, do)


def attn_bwd(q, q0, k, k0, v, v0, do, do0, n_pairs, lse, delta, decay, *, causal, rider=None, name):
    Sq, Sk = q.shape[0], k.shape[0]
    tq = _tile(Sq, ATTN_Q_BLOCK, LANES)
    tk = _tile(Sk, ATTN_K_BLOCK, LANES)
    nq, nk = Sq // tq, Sk // tk
    bias = decay is not None

    rs = _tile(tq, ATTN_ROW_CHUNK, 16)
    r_args, r_in_specs, r_shapes, r_out_specs, aliases, split = _with_rider(
        rider, 7 if bias else 6, 5 if bias else 3, 0)

    live = [(i, j) for j in range(nk) for i in range(nq) if not causal or j * tk <= i * tq + tq - 1]
    n_live = len(live)

    def body(i_tab, j_tab, *refs):
        ins, outs, _, ride = split(refs)
        q_ref, k_ref, v_ref, do_ref, lse_ref, dl_ref = ins[:6]
        ck_ref = ins[6] if bias else None
        dq_ref, dk_ref, dv_ref = outs[:3]
        dcs_ref, drs_ref = (outs[3], outs[4]) if bias else (None, None)
        pr, t = pl.program_id(0), pl.program_id(1)
        i, j = i_tab[t], j_tab[t]
        first_i = (j * tk) // tq if causal else 0
        if ride is not None:
            pl.when(jnp.logical_and(pr == 0, t == 0))(ride[0])

        @pl.when(i == first_i)
        def _():
            dk_ref[...] = jnp.zeros_like(dk_ref)
            dv_ref[...] = jnp.zeros_like(dv_ref)
            if bias:
                dcs_ref[...] = jnp.zeros_like(dcs_ref)

        def compute(masked):
            kv, vv = k_ref[...], v_ref[...].astype(BF16)
            low_k = _low_half(kv.shape)
            ka = [_one_head(kv, low_k, a) for a in range(2)]
            for r in range(tq // rs):
                here = pl.ds(r * rs, rs)
                rows = pl.ds(pl.multiple_of(i * tq + r * rs, rs), rs)
                qv, dov = q_ref[here, :], do_ref[here, :].astype(BF16)
                low = _low_half(qv.shape)
                dq_part, dk_part, dv_part, row_parts, col_parts = None, None, None, [], []
                for a in range(2):
                    qa, doa = _one_head(qv, low, a), _one_head(dov, low, a)
                    s = _dot(qa, kv, 1, 1)
                    if bias:
                        s = s - ck_ref[a]
                    p = jnp.exp(s - lse_ref[a, here])
                    if masked:
                        row = i * tq + r * rs + lax.broadcasted_iota(jnp.int32, (rs, tk), 0)
                        col = j * tk + lax.broadcasted_iota(jnp.int32, (rs, tk), 1)
                        p = jnp.where(col <= row, p, 0.0)
                    dv_a = _dot(p.astype(BF16), doa, 0, 0)
                    dp = _dot(doa, vv, 1, 1)
                    ds = p * (dp - dl_ref[a, here])
                    dsb = ds.astype(BF16)
                    dk_a = _dot(dsb, qa, 0, 0)
                    if bias:
                        col_parts.append(jnp.sum(ds, axis=0, keepdims=True))
                        lanes = ds[:, :LANES]
                        for c in range(1, tk // LANES):
                            lanes = lanes + ds[:, c * LANES:(c + 1) * LANES]
                        row_parts.append(lanes)
                    part = _dot(dsb, ka[a], 1, 0) * QK_SCALE
                    dq_part = part if dq_part is None else dq_part + part
                    dk_part = dk_a if dk_part is None else dk_part + dk_a
                    dv_part = dv_a if dv_part is None else dv_part + dv_a
                dv_ref[...] += dv_part
                dk_ref[...] += dk_part
                for a, cp in enumerate(col_parts):
                    dcs_ref[a] += cp

                @pl.when(j == 0)
                def _(rows=rows, dq_part=dq_part, row_parts=row_parts):
                    dq_ref[rows, :] = dq_part
                    for a, rp in enumerate(row_parts):
                        drs_ref[a, rows, :] = rp

                @pl.when(j > 0)
                def _(rows=rows, dq_part=dq_part, row_parts=row_parts):
                    dq_ref[rows, :] += dq_part
                    for a, rp in enumerate(row_parts):
                        drs_ref[a, rows, :] += rp

        if causal:
            crosses = j * tk + (tk - 1) > i * tq
            pl.when(crosses)(functools.partial(compute, True))
            pl.when(jnp.logical_not(crosses))(functools.partial(compute, False))
        else:
            compute(False)

        if ride is not None:
            pl.when(jnp.logical_and(pr == n_pairs - 1, t == n_live - 1))(ride[1])

    col1 = pl.BlockSpec((None, 2, tq, 1), lambda p, t, it, jt: (p, 0, it[t], 0))
    in_specs = [pl.BlockSpec((tq, LANES), lambda p, t, it, jt: (it[t], q0 + p)),
                pl.BlockSpec((tk, LANES), lambda p, t, it, jt: (jt[t], k0 + p)),
                pl.BlockSpec((tk, LANES), lambda p, t, it, jt: (jt[t], v0 + p)),
                pl.BlockSpec((tq, LANES), lambda p, t, it, jt: (it[t], do0 + p)), col1, col1]
    args = [q, k, v, do, lse, delta]
    kout = pl.BlockSpec((tk, LANES), lambda p, t, it, jt: (jt[t], p))
    out_specs = [pl.BlockSpec((Sq, LANES), lambda p, t, it, jt: (0, p)), kout, kout]
    out_shape = [jax.ShapeDtypeStruct((Sq, n_pairs * LANES), F32), jax.ShapeDtypeStruct((Sk, n_pairs * LANES), F32),
                 jax.ShapeDtypeStruct((Sk, n_pairs * LANES), F32)]
    if bias:
        in_specs.append(pl.BlockSpec((None, 2, 1, tk), lambda p, t, it, jt: (p, 0, 0, jt[t])))
        args.append(decay)
        out_specs += [pl.BlockSpec((None, 2, 1, tk), lambda p, t, it, jt: (p, 0, 0, jt[t])),
                      pl.BlockSpec((None, 2, Sq, LANES), lambda p, t, it, jt: (p, 0, 0, 0))]
        out_shape += [jax.ShapeDtypeStruct((n_pairs, 2, 1, Sk), F32),
                      jax.ShapeDtypeStruct((n_pairs, 2, Sq, LANES), F32)]
    n_own = len(out_shape)
    tabs = [jnp.asarray([b[n] for b in live], jnp.int32) for n in range(2)]
    out = pl.pallas_call(
        body,
        grid_spec=pltpu.PrefetchScalarGridSpec(
            num_scalar_prefetch=2, grid=(n_pairs, n_live), in_specs=in_specs + r_in_specs,
            out_specs=out_specs + r_out_specs, scratch_shapes=list(rider.sems) if rider else []),
        out_shape=out_shape + r_shapes,
        input_output_aliases={2 + a: b for a, b in aliases.items()},
        compiler_params=_params(("arbitrary", "arbitrary") if rider else ("parallel", "arbitrary")),
        name=name)(*tabs, *args, *r_args)
    return tuple(out[:n_own]), out[n_own:]


def _tril_mask(n):
    r = lax.broadcasted_iota(jnp.int32, (n, n), 0)
    c = lax.broadcasted_iota(jnp.int32, (n, n), 1)
    return c <= r


def _gmlp_operands(v_gain, w_s, b_s):
    G = w_s.shape[0]
    return (v_gain.reshape(G // 2, 1, LANES), w_s.reshape(G // 2, 2, CHUNK, CHUNK), b_s.reshape(G // 2, 2, CHUNK, 1))


def _gmlp_gate(wt, vh, b_ref, low):
    gate = _dot(wt[0], _one_head(vh, low, 0), 1, 0) + _dot(wt[1], _one_head(vh, low, 1), 1, 0)
    return gate + jnp.where(low, b_ref[0], b_ref[1])


def gmlp_fwd(proj, v0, n_pairs, vg, w, b, *, name):
    S = proj.shape[0]
    ts = _tile(S, 1024, CHUNK)

    def body(up_ref, vp_ref, vg_ref, w_ref, b_ref, o_ref):
        mask = _tril_mask(CHUNK)
        wt = [jnp.where(mask, w_ref[a], 0.0).astype(BF16) for a in range(2)]
        low = _low_half((CHUNK, LANES))
        for c in range(ts // CHUNK):
            sl = pl.ds(c * CHUNK, CHUNK)
            vz = _gelu(vp_ref[sl, :])
            r = lax.rsqrt(_half_sums(vz * vz, low) * (1.0 / HEAD_DIM) + EPS)
            vh = (vz * r * vg_ref[...]).astype(BF16)
            o_ref[sl, :] = _gelu(up_ref[sl, :]) * _gmlp_gate(wt, vh, b_ref, low)

    return pl.pallas_call(
        body, grid=(n_pairs, S // ts),
        in_specs=[pl.BlockSpec((ts, LANES), lambda p, i: (i, p)), pl.BlockSpec((ts, LANES), lambda p, i: (i, v0 + p)),
                  pl.BlockSpec((None, 1, LANES), lambda p, i: (p, 0, 0)),
                  pl.BlockSpec((None, 2, CHUNK, CHUNK), lambda p, i: (p, 0, 0, 0)),
                  pl.BlockSpec((None, 2, CHUNK, 1), lambda p, i: (p, 0, 0, 0))],
        out_specs=pl.BlockSpec((ts, LANES), lambda p, i: (i, p)),
        out_shape=jax.ShapeDtypeStruct((S, n_pairs * LANES), F32),
        compiler_params=_params(("parallel", "parallel")), name=name)(proj, proj, vg, w, b)


def gmlp_bwd(proj, v0, n_pairs, vg, w, wT, b, do, *, name):
    S = proj.shape[0]
    ts = _tile(S, 1024, CHUNK)

    def body(up_ref, vp_ref, vg_ref, w_ref, wT_ref, b_ref, do_ref, dup_ref, dvp_ref, dw_ref, db_ref, dvg_ref):
        i = pl.program_id(1)

        @pl.when(i == 0)
        def _():
            dw_ref[...] = jnp.zeros_like(dw_ref)
            db_ref[...] = jnp.zeros_like(db_ref)
            dvg_ref[...] = jnp.zeros_like(dvg_ref)

        mask = _tril_mask(CHUNK)
        wt = [jnp.where(mask, w_ref[a], 0.0).astype(BF16) for a in range(2)]
        wtT = [jnp.where(mask.T, wT_ref[a], 0.0).astype(BF16) for a in range(2)]
        low = _low_half((CHUNK, LANES))
        vgain = vg_ref[...]
        for c in range(ts // CHUNK):
            sl = pl.ds(c * CHUNK, CHUNK)
            u_pre, v_pre, dout = up_ref[sl, :], vp_ref[sl, :], do_ref[sl, :]
            vz = _gelu(v_pre)
            r = lax.rsqrt(_half_sums(vz * vz, low) * (1.0 / HEAD_DIM) + EPS)
            vh = (vz * r * vgain).astype(BF16)
            gate = _gmlp_gate(wt, vh, b_ref, low)
            dgate = dout * _gelu(u_pre)
            dup_ref[sl, :] = dout * gate * _gelu_grad(u_pre)
            dvh = None
            for a in range(2):
                dga = _one_head(dgate, low, a)
                dgb = dga.astype(BF16)
                dw_ref[a] += jnp.where(mask, _dot(dgb, vh, 1, 1), 0.0)
                db_ref[a] += jnp.sum(dga, axis=1, keepdims=True)
                part = _dot(wtT[a], dgb, 1, 0)
                dvh = part if dvh is None else dvh + part
            dvg_ref[...] += jnp.sum(dvh * vz * r, axis=0, keepdims=True)
            t = dvh * vgain
            dvz = r * t - vz * (r * r * r) * (_half_sums(vz * t, low) * (1.0 / HEAD_DIM))
            dvp_ref[sl, :] = dvz * _gelu_grad(v_pre)

    ublk = pl.BlockSpec((ts, LANES), lambda p, i: (i, p))
    wblk = pl.BlockSpec((None, 2, CHUNK, CHUNK), lambda p, i: (p, 0, 0, 0))
    bblk = pl.BlockSpec((None, 2, CHUNK, 1), lambda p, i: (p, 0, 0, 0))
    gblk = pl.BlockSpec((None, 1, LANES), lambda p, i: (p, 0, 0))
    return pl.pallas_call(
        body, grid=(n_pairs, S // ts),
        in_specs=[ublk, pl.BlockSpec((ts, LANES), lambda p, i: (i, v0 + p)), gblk, wblk, wblk, bblk, ublk],
        out_specs=[ublk, ublk, wblk, bblk, gblk],
        out_shape=[jax.ShapeDtypeStruct((S, n_pairs * LANES), F32), jax.ShapeDtypeStruct((S, n_pairs * LANES), F32),
                   jax.ShapeDtypeStruct((n_pairs, 2, CHUNK, CHUNK), F32), jax.ShapeDtypeStruct((n_pairs, 2, CHUNK, 1), F32),
                   jax.ShapeDtypeStruct((n_pairs, 1, LANES), F32)],
        compiler_params=_params(("parallel", "arbitrary")), name=name)(proj, proj, vg, w, wT, b, do)


def loss_head(y, target, *, name):
    S, D = y.shape
    ts = _tile(S, 512, 8)

    def body(y_ref, t_ref, dy_ref, loss_ref):
        i = pl.program_id(0)
        e = y_ref[...] - t_ref[...]
        dy_ref[...] = e * (1.0 / D)
        part = jnp.sum(jnp.sum(e * e, axis=1, keepdims=True), axis=0, keepdims=True) * (0.5 / D)

        @pl.when(i == 0)
        def _():
            loss_ref[...] = part

        @pl.when(i > 0)
        def _():
            loss_ref[...] += part

    row = pl.BlockSpec((ts, D), lambda i: (i, 0))
    return pl.pallas_call(
        body, grid=(S // ts,), in_specs=[row, row],
        out_specs=[row, pl.BlockSpec((1, 1), lambda i: (0, 0))],
        out_shape=[jax.ShapeDtypeStruct((S, D), F32), jax.ShapeDtypeStruct((1, 1), F32)],
        compiler_params=_params(("arbitrary",)), name=name)(y, target)


def adamw(w, g, m, v, *, name):
    shape = w.shape
    C = shape[-1]
    R = w.size // C
    tr = _tile(R, max(8, (256 * 1024) // C // 8 * 8), 8)

    def body(w_ref, g_ref, m_ref, v_ref, d_ref, nm_ref, nv_ref):
        gv = g_ref[...]
        nm = ADAM_B1 * m_ref[...] + (1.0 - ADAM_B1) * gv
        nv = ADAM_B2 * v_ref[...] + (1.0 - ADAM_B2) * (gv * gv)
        m_hat = nm / (1.0 - ADAM_B1 ** ADAM_STEP)
        v_hat = nv / (1.0 - ADAM_B2 ** ADAM_STEP)
        d_ref[...] = -ADAM_LR * (m_hat / (jnp.sqrt(v_hat) + ADAM_EPS) + ADAM_WD * w_ref[...])
        nm_ref[...] = nm
        nv_ref[...] = nv

    blk = pl.BlockSpec((tr, C), lambda i: (i, 0))
    out = pl.pallas_call(
        body, grid=(R // tr,), in_specs=[blk] * 4, out_specs=[blk] * 3,
        out_shape=[jax.ShapeDtypeStruct((R, C), F32)] * 3,
        compiler_params=_params(("parallel",)), name=name)(*(a.reshape(R, C) for a in (w, g, m, v)))
    return tuple(o.reshape(shape) for o in out)


def pair_sum(p, landed, half, *, name):
    n, R, C = landed.shape
    tr = _tile(R, 256, 16)
    nr = R // tr

    def body(half_ref, p_ref, l_ref, o_ref):
        o_ref[...] = (p_ref[...] + l_ref[...]).astype(BF16)

    return pl.pallas_call(
        body,
        grid_spec=pltpu.PrefetchScalarGridSpec(
            num_scalar_prefetch=1, grid=(n, nr),
            in_specs=[pl.BlockSpec((None, tr, C), lambda k, r, half_ref: (k, half_ref[0] * nr + r, 0)),
                      pl.BlockSpec((None, tr, C), lambda k, r, half_ref: (k, r, 0))],
            out_specs=pl.BlockSpec((None, tr, C), lambda k, r, half_ref: (k, r, 0))),
        out_shape=jax.ShapeDtypeStruct((n, R, C), BF16),
        compiler_params=_params(("parallel", "parallel")), name=name)(half, p, landed)


def chip_sum(own, landed, chip, *, name):
    n, R, C = own.shape
    tr = _tile(R, 256, 16)

    def body(chip_ref, own_ref, *rest):
        l_refs, o_ref = rest[:n], rest[n]
        me = chip_ref[0]
        acc = None
        for d in range(n):
            term = jnp.where(me == d, own_ref[...], l_refs[d][...]).astype(F32)
            acc = term if acc is None else acc + term
        o_ref[...] = acc

    def landed_spec(d):
        return pl.BlockSpec((None, tr, C), lambda r, chip_ref: (jnp.where(chip_ref[0] == d, (d + 1) % n, d), r, 0))

    return pl.pallas_call(
        body,
        grid_spec=pltpu.PrefetchScalarGridSpec(
            num_scalar_prefetch=1, grid=(R // tr,),
            in_specs=[pl.BlockSpec((None, tr, C), lambda r, chip_ref: (chip_ref[0], r, 0))]
            + [landed_spec(d) for d in range(n)],
            out_specs=pl.BlockSpec((tr, C), lambda r, chip_ref: (r, 0))),
        out_shape=jax.ShapeDtypeStruct((R, C), F32),
        compiler_params=_params(("parallel",)), name=name)(chip, own, *([landed] * n))


def ordered_sum(parts, *, name):
    n, R, C = parts.shape
    tr = _tile(R, 256, 16)

    def body(p_ref, o_ref):
        acc = p_ref[0].astype(F32)
        for d in range(1, n):
            acc = acc + p_ref[d].astype(F32)
        o_ref[...] = acc

    return pl.pallas_call(
        body, grid=(R // tr,), in_specs=[pl.BlockSpec((n, tr, C), lambda r: (0, r, 0))],
        out_specs=pl.BlockSpec((tr, C), lambda r: (r, 0)),
        out_shape=jax.ShapeDtypeStruct((R, C), F32),
        compiler_params=_params(("parallel",)), name=name)(parts)


_ANY = pl.BlockSpec(memory_space=pl.ANY)


def _position():
    return lax.axis_index("x"), lax.axis_index("y"), lax.axis_index("c")


def _remote(src, dst, send_sem, recv_sem, device):
    return pltpu.make_async_remote_copy(src_ref=src, dst_ref=dst, send_sem=send_sem, recv_sem=recv_sem,
                                        device_id=device, device_id_type=MESH_ID)


def _small_all_gather(s_ref, all_ref, send_sems, recv_sems, x, y, c):
    me = 4 * x + 2 * y + c
    copies = []
    for f in range(1, 8):
        peer = ((1 - x) if f & 4 else x, (1 - y) if f & 2 else y, (1 - c) if f & 1 else c)
        cp = _remote(s_ref, all_ref.at[me], send_sems.at[f - 1], recv_sems.at[f - 1], peer)
        cp.start()
        copies.append((cp, peer, f - 1))

    def finish():
        for cp, peer, s in copies:
            slot = all_ref.at[4 * peer[0] + 2 * peer[1] + peer[2]]
            _remote(slot, slot, send_sems.at[s], recv_sems.at[s], peer).wait_recv()
        for cp, _, _ in copies:
            cp.wait_send()

    return finish


def _core_rows(ref, core):
    h = ref.shape[1] // 2
    return pl.ds(core * h, h)


def _gather_plan(outs, send_sems, recv_sems):
    n = len(outs)
    x, y, c = _position()
    k = 2 * x + y
    sibling = (x, y, 1 - c)
    chips = [(1 - x, y), (x, 1 - y), (1 - x, 1 - y)]

    def first():
        return [_remote(outs[w].at[k, _core_rows(outs[w], c)], outs[w].at[k, _core_rows(outs[w], c)],
                        send_sems.at[w, j], recv_sems.at[w, j], (px, py, c))
                for j, (px, py) in enumerate(chips) for w in range(n)]

    def start():
        for cp in first():
            cp.start()

    def finish():
        passed = []
        for j, (px, py) in enumerate(chips):
            for w in range(n):
                slot = outs[w].at[2 * px + py, _core_rows(outs[w], c)]
                _remote(slot, slot, send_sems.at[w, j], recv_sems.at[w, j], (px, py, c)).wait_recv()
                cp = _remote(slot, slot, send_sems.at[w, 3 + j], recv_sems.at[w, 3 + j], sibling)
                cp.start()
                passed.append(cp)
        for j, (px, py) in enumerate(chips):
            for w in range(n):
                slot = outs[w].at[2 * px + py, _core_rows(outs[w], 1 - c)]
                _remote(slot, slot, send_sems.at[w, 3 + j], recv_sems.at[w, 3 + j], sibling).wait_recv()
        for cp in first() + passed:
            cp.wait_send()

    return start, finish


def _gather_sems(n):
    return (pltpu.SemaphoreType.DMA((n, 6)), pltpu.SemaphoreType.DMA((n, 6)))


def gather_rider(slabs):
    return Rider(tuple(slabs), tuple(jax.ShapeDtypeStruct(a.shape, a.dtype) for a in slabs),
                 {i: i for i in range(len(slabs))}, _gather_sems(len(slabs)),
                 lambda ins, outs, sems: _gather_plan(outs, sems[0], sems[1]))


def gather_weights(slabs, small_slab, *, name):
    n = len(slabs)

    def body(*refs):
        outs, all_ref = refs[n + 1:2 * n + 1], refs[2 * n + 1]
        send_sems, recv_sems, s_send, s_recv = refs[2 * n + 2:]
        x, y, c = _position()
        finish_small = _small_all_gather(all_ref.at[4 * x + 2 * y + c], all_ref, s_send, s_recv, x, y, c)
        start, finish = _gather_plan(outs, send_sems, recv_sems)
        start()
        finish()
        finish_small()

    args = list(slabs) + [small_slab]
    out = pl.pallas_call(
        body, in_specs=[_ANY] * (n + 1), out_specs=[_ANY] * (n + 1),
        out_shape=[jax.ShapeDtypeStruct(a.shape, a.dtype) for a in args],
        input_output_aliases={i: i for i in range(n + 1)},
        scratch_shapes=list(_gather_sems(n)) + [pltpu.SemaphoreType.DMA((7,)), pltpu.SemaphoreType.DMA((7,))],
        name=name)(*args)
    return out[:n], out[n]


def exchange_with_sibling(parts, small_slab, *, name):
    n = len(parts)
    has_small = small_slab is not None
    n_arg = n + (1 if has_small else 0)

    def body(*refs):
        p_refs = refs[:n]
        lands = refs[n_arg:n_arg + n]
        send_sems, recv_sems = refs[2 * n_arg], refs[2 * n_arg + 1]
        x, y, c = _position()
        sibling = (x, y, 1 - c)
        if has_small:
            all_ref = refs[n_arg + n]
            finish_small = _small_all_gather(all_ref.at[4 * x + 2 * y + c], all_ref, refs[2 * n_arg + 2],
                                             refs[2 * n_arg + 3], x, y, c)
        sends = []
        for w in range(n):
            for d in range(4):
                cp = _remote(p_refs[w].at[d, _core_rows(p_refs[w], 1 - c)], lands[w].at[d],
                             send_sems.at[w, d], recv_sems.at[w, d], sibling)
                cp.start()
                sends.append(cp)
        for cp in sends:
            cp.wait_recv()
        for cp in sends:
            cp.wait_send()
        if has_small:
            finish_small()

    small_args = [small_slab] if has_small else []
    out = pl.pallas_call(
        body, in_specs=[_ANY] * n_arg, out_specs=[_ANY] * n_arg,
        out_shape=[jax.ShapeDtypeStruct((4, p.shape[1] // 2, p.shape[2]), p.dtype) for p in parts]
        + [jax.ShapeDtypeStruct(s.shape, s.dtype) for s in small_args],
        input_output_aliases={n: n} if has_small else {},
        scratch_shapes=[pltpu.SemaphoreType.DMA((n, 4)), pltpu.SemaphoreType.DMA((n, 4))]
        + ([pltpu.SemaphoreType.DMA((7,)), pltpu.SemaphoreType.DMA((7,))] if has_small else []),
        name=name)(*parts, *small_args)
    return out[:n], (out[n] if has_small else None)


def _scatter_plan(q_refs, outs, send_sems, recv_sems):
    n = len(q_refs)
    x, y, c = _position()
    k = 2 * x + y
    chips = [(1 - x, y), (x, 1 - y), (1 - x, 1 - y)]

    def sends():
        return [_remote(q_refs[w].at[2 * px + py], outs[w].at[k], send_sems.at[w, j], recv_sems.at[w, j], (px, py, c))
                for j, (px, py) in enumerate(chips) for w in range(n)]

    def start():
        for cp in sends():
            cp.start()

    def finish():
        for j, (px, py) in enumerate(chips):
            for w in range(n):
                slot = outs[w].at[2 * px + py]
                _remote(slot, slot, send_sems.at[w, j], recv_sems.at[w, j], (px, py, c)).wait_recv()
        for cp in sends():
            cp.wait_send()

    return start, finish


def _scatter_sems(n):
    return (pltpu.SemaphoreType.DMA((n, 3)), pltpu.SemaphoreType.DMA((n, 3)))


def scatter_rider(parts):
    return Rider(tuple(parts), tuple(jax.ShapeDtypeStruct(q.shape, q.dtype) for q in parts), {},
                 _scatter_sems(len(parts)), lambda ins, outs, sems: _scatter_plan(ins, outs, sems[0], sems[1]))


def scatter_to_chips(parts, *, name):
    n = len(parts)

    def body(*refs):
        start, finish = _scatter_plan(refs[:n], refs[n:2 * n], refs[2 * n], refs[2 * n + 1])
        start()
        finish()

    return pl.pallas_call(
        body, in_specs=[_ANY] * n, out_specs=[_ANY] * n,
        out_shape=[jax.ShapeDtypeStruct(q.shape, q.dtype) for q in parts],
        scratch_shapes=list(_scatter_sems(n)), name=name)(*parts)


def share_with_sibling(parts, *, name):
    n = len(parts)

    def body(*refs):
        r_refs, outs = refs[:n], refs[n:2 * n]
        send_sems, recv_sems = refs[2 * n:]
        x, y, c = _position()
        sends = []
        for w in range(n):
            cp = _remote(r_refs[w], outs[w], send_sems.at[w], recv_sems.at[w], (x, y, 1 - c))
            cp.start()
            sends.append(cp)
        for cp in sends:
            cp.wait_recv()
        for cp in sends:
            cp.wait_send()

    return pl.pallas_call(
        body, in_specs=[_ANY] * n, out_specs=[_ANY] * n,
        out_shape=[jax.ShapeDtypeStruct(r.shape, r.dtype) for r in parts],
        scratch_shapes=[pltpu.SemaphoreType.DMA((n,)), pltpu.SemaphoreType.DMA((n,))],
        name=name)(*parts)


def _cols_to_chips(full):
    *lead, R, C4 = full.shape
    t = full.reshape(*lead, R, 4, C4 // 4)
    return jnp.moveaxis(t, -2, 0)


def _chips_to_cols(sh):
    t = jnp.moveaxis(sh, 0, -2)
    return t.reshape(*t.shape[:-2], t.shape[-2] * t.shape[-1])


def _slot_in_empty(own, index, n):
    return lax.dynamic_update_slice(lax.empty((n,) + own.shape, own.dtype), own[None], (index,) + (0,) * own.ndim)


def _fold_pair(dg):
    return dg[0, :HEAD_DIM] + dg[0, HEAD_DIM:]


def _ffn_fwd(x, g, w_in_slab, w_out, tag):
    h = rms_fwd(x, g, name=f"{tag}_rms")
    a, b, act = swiglu_fwd(h, w_in_slab, name=f"{tag}_in")
    y = matmul(act, w_out, res=x, scale=0.5, tm=1024, tn=512, tk=w_out.shape[0], name=f"{tag}_out")
    return y, (x, h, a, b, act)


def _ffn_bwd(dy, saved, g, w_in_slab, w_out, tag):
    x, h, a, b, act = saved
    da, db = swiglu_bwd(dy, w_out, a, b, name=f"{tag}_dact")
    dw_out = grad_rows(act, dy, scale=0.5, name=f"{tag}_dwout")
    dw_in = grad_cols(h, da, db, name=f"{tag}_dwin")
    dx, dg = ffn_dh(da, db, w_in_slab, x, g, dy, name=f"{tag}_dh")
    return dx, dg[0], dw_in, dw_out


MEM_PAIRS = MEM_WIDTH // LANES


def _mem_attn_fwd(proj, mq0, mem_n, w_kv, g_q, g_k, tag):
    qh = pairnorm_fwd(proj, mq0, MEM_PAIRS, g_q, scale=QK_SCALE,name=f"{tag}_qnorm")
    kv = matmul(mem_n, w_kv, tm=256, tn=512, tk=1024, name=f"{tag}_kv")
    kh = pairnorm_fwd(kv, 0, MEM_PAIRS, g_k, name=f"{tag}_knorm")
    o, lse, _ = attn_fwd(qh, 0, kh, 0, kv, MEM_PAIRS, MEM_PAIRS, None, causal=False, name=f"{tag}_attn")
    return o, (qh, kv, kh, o, lse)


def _mem_attn_bwd(dmix, do0, proj, mq0, saved, mem_n, g_q, g_k, tag):
    qh, kv, kh, o, lse = saved
    delta = attn_delta(o, dmix, do0, MEM_PAIRS, name=f"{tag}_delta")
    (dqh, dkh, dv), _ = attn_bwd(qh, 0, kh, 0, kv, MEM_PAIRS, dmix, do0, MEM_PAIRS, lse, delta, None,
                                 causal=False, name=f"{tag}_dattn")
    dq_pre, dgq = pairnorm_bwd(proj, mq0, MEM_PAIRS, dqh, g_q, name=f"{tag}_dqnorm")
    dk_pre, dgk = pairnorm_bwd(kv, 0, MEM_PAIRS, dkh, g_k, name=f"{tag}_dknorm")
    dkv = jnp.concatenate([dk_pre, dv], axis=1)
    dw_kv = grad_rows(mem_n, dkv, name=f"{tag}_dwkv")
    return dq_pre, _fold_pair(dgq), _fold_pair(dgk), dw_kv, dkv


def _per_head_lanes(x, H):
    return jnp.pad(x.reshape(H, -1).T, ((0, 0), (0, LANES - H)))


def _fox_fwd(proj, b_f, g_q, g_k, tok, rider, tag):
    H, P = tok // HEAD_DIM, tok // LANES
    bias = jnp.pad(b_f.reshape(1, H), ((0, 0), (0, LANES - H)))
    qh = pairnorm_fwd(proj, 0, P, g_q, scale=QK_SCALE,name=f"{tag}_qnorm")
    kh = pairnorm_fwd(proj, P, P, g_k, name=f"{tag}_knorm")
    c = fgate_fwd(proj, 3 * P + MEM_PAIRS, bias, name=f"{tag}_fgate")
    decay = c[:, :H].T.reshape(P, 2, 1, c.shape[0])
    o, lse, rode = attn_fwd(qh, 0, kh, 0, proj, 2 * P, P, decay, causal=True, rider=rider, name=f"{tag}_attn")
    return o, (qh, kh, bias, decay, o, lse), rode


def _fox_bwd(dmix, proj, saved, g_q, g_k, tok, rider, tag):
    qh, kh, bias, decay, o, lse = saved
    H, P = tok // HEAD_DIM, tok // LANES
    delta = attn_delta(o, dmix, 0, P, name=f"{tag}_delta")
    (dqh, dkh, dv, dcs, drs), rode = attn_bwd(qh, 0, kh, 0, proj, 2 * P, dmix, 0, P, lse, delta, decay, causal=True,
                                              rider=rider, name=f"{tag}_dattn")
    dq_pre, dgq = pairnorm_bwd(proj, 0, P, dqh, g_q, name=f"{tag}_dqnorm")
    dk_pre, dgk = pairnorm_bwd(proj, P, P, dkh, g_k, name=f"{tag}_dknorm")
    dz, dbias = fgate_bwd(proj, 3 * P + MEM_PAIRS, bias, drs, _per_head_lanes(dcs, H), name=f"{tag}_dfgate")
    dqkv = jnp.concatenate([dq_pre, dk_pre, dv], axis=1)
    return dqkv, dz, dbias[0, :H], _fold_pair(dgq), _fold_pair(dgk), rode


def local_step(x, mem, target, W, comm=None):
    S, D = x.shape
    tok = D - MEM_WIDTH
    P = tok // LANES
    depth = W["norm_ffn1"].shape[0]
    mem_n = rms_fwd(mem, W["mem_norm"], name="mem_rms")
    saved = []
    for i in range(depth):
        kind, j = i % 2, i // 2
        t = f"l{i}"
        x1, s1 = _ffn_fwd(x, W["norm_ffn1"][i], W["ffn1_w_in"][i], W["ffn1_w_out"][i], f"{t}_ffn1")
        h = rms_fwd(x1, W["norm_mix"][i], name=f"{t}_mix_rms")
        w_mix = W["fox_w_in"][j] if kind == 0 else W["gmlp_w_in"][j]
        proj = matmul(h, w_mix, tm=1024, tn=896, tk=D, name=f"{t}_mix_in")
        if kind == 0:
            rider = comm.late_weights_rider() if (comm is not None and i == 0) else None
            o_tok, s_tok, rode = _fox_fwd(proj, W["fox_b_f"][j], W["fox_q_norm"][j], W["fox_k_norm"][j], tok, rider,
                                          f"{t}_fox")
            if rider is not None:
                comm.accept_late_weights(W, rode)
            mq0 = 3 * P
        else:
            vg, ws, bs = _gmlp_operands(W["gmlp_v_norm"][j], W["gmlp_w_s"][j], W["gmlp_b_s"][j])
            o_tok = gmlp_fwd(proj, P, P, vg, ws, bs, name=f"{t}_gmlp")
            s_tok = None
            mq0 = 2 * P
        o_mem, s_mem = _mem_attn_fwd(proj, mq0, mem_n, W["mem_w_kv"][i], W["mem_q_norm"][i], W["mem_k_norm"][i],
                                     f"{t}_mem")
        mix = jnp.concatenate([o_tok, o_mem], axis=1).astype(BF16)
        x2 = matmul(mix, W["w_out"][i], res=x1, tm=1024, tn=512, tk=D, name=f"{t}_mix_out")
        x3, s3 = _ffn_fwd(x2, W["norm_ffn2"][i], W["ffn2_w_in"][i], W["ffn2_w_out"][i], f"{t}_ffn2")
        saved.append((s1, x1, h, proj, mq0, s_tok, s_mem, mix, s3))
        x = x3

    dx, loss = loss_head(x, target, name="loss_head")

    G = {k: [None] * depth for k in ("norm_ffn1", "norm_mix", "norm_ffn2", "mem_q_norm", "mem_k_norm", "ffn1_w_in",
                                     "ffn1_w_out", "ffn2_w_in", "ffn2_w_out", "w_out", "mem_w_kv")}
    n_fox, n_gmlp = (depth + 1) // 2, depth // 2
    for k in ("fox_w_in", "fox_b_f", "fox_q_norm", "fox_k_norm"):
        G[k] = [None] * n_fox
    for k in ("gmlp_w_in", "gmlp_v_norm", "gmlp_w_s", "gmlp_b_s"):
        G[k] = [None] * n_gmlp
    dkv_all = [None] * depth
    for i in reversed(range(depth)):
        kind, j = i % 2, i // 2
        t = f"l{i}"
        s1, x1, h, proj, mq0, s_tok, s_mem, mix, s3 = saved[i]
        dx, G["norm_ffn2"][i], G["ffn2_w_in"][i], G["ffn2_w_out"][i] = _ffn_bwd(
            dx, s3, W["norm_ffn2"][i], W["ffn2_w_in"][i], W["ffn2_w_out"][i], f"{t}_ffn2")
        dmix = matmul(dx, W["w_out"][i], tb=True, tm=1024, tn=1024, tk=D, name=f"{t}_dmix")
        G["w_out"][i] = grad_rows(mix, dx, name=f"{t}_dwmixout")
        dmq, G["mem_q_norm"][i], G["mem_k_norm"][i], G["mem_w_kv"][i], dkv_all[i] = _mem_attn_bwd(
            dmix, P, proj, mq0, s_mem, mem_n, W["mem_q_norm"][i], W["mem_k_norm"][i], f"{t}_mem")
        if kind == 0:
            rider = comm.early_grads_rider(G) if (comm is not None and i == 0) else None
            dqkv, dz, G["fox_b_f"][j], G["fox_q_norm"][j], G["fox_k_norm"][j], rode = _fox_bwd(
                dmix, proj, s_tok, W["fox_q_norm"][j], W["fox_k_norm"][j], tok, rider, f"{t}_fox")
            if rider is not None:
                comm.accept_early_grads(rode)
            dproj = jnp.concatenate([dqkv, dmq, dz], axis=1).astype(BF16)
            w_mix, wkey = W["fox_w_in"][j], "fox_w_in"
        else:
            vg, ws, bs = _gmlp_operands(W["gmlp_v_norm"][j], W["gmlp_w_s"][j], W["gmlp_b_s"][j])
            dup, dvp, dws, dbs, dvg = gmlp_bwd(proj, P, P, vg, ws, jnp.swapaxes(ws, 2, 3), bs, dmix,
                                               name=f"{t}_dgmlp")
            G["gmlp_w_s"][j] = dws.reshape(W["gmlp_w_s"][j].shape)
            G["gmlp_b_s"][j] = dbs.reshape(W["gmlp_b_s"][j].shape)
            G["gmlp_v_norm"][j] = dvg.reshape(-1)
            dproj = jnp.concatenate([dup, dvp, dmq], axis=1).astype(BF16)
            w_mix, wkey = W["gmlp_w_in"][j], "gmlp_w_in"
        G[wkey][j] = matmul(h, dproj, ta=True, tm=1024, tn=896, tk=1024, name=f"{t}_dwmixin")
        dh = matmul(dproj, w_mix, tb=True, tm=1024, tn=1024, tk=896, name=f"{t}_dhmix")
        dx, dgm = rms_bwd(x1, dh, W["norm_mix"][i], dx, name=f"{t}_dmixrms")
        G["norm_mix"][i] = dgm[0]
        dx, G["norm_ffn1"][i], G["ffn1_w_in"][i], G["ffn1_w_out"][i] = _ffn_bwd(
            dx, s1, W["norm_ffn1"][i], W["ffn1_w_in"][i], W["ffn1_w_out"][i], f"{t}_ffn1")
    w_kv_all = jnp.concatenate([W["mem_w_kv"][i] for i in range(depth)], axis=1)
    dmem_n = matmul(jnp.concatenate(dkv_all, axis=1), w_kv_all, tb=True, tm=256, tn=512, tk=1024, name="dmem_n")
    _, dmemg = rms_bwd(mem, dmem_n, W["mem_norm"], None, name="dmem_rms")
    G["mem_norm"] = [dmemg[0]]
    return loss, dx, G


def _fox_cols_to_compute(w, tok):
    H = tok // HEAD_DIM
    qkv, f, mq = w[..., :3 * tok], w[..., 3 * tok:3 * tok + H], w[..., 3 * tok + H:]
    f = jnp.pad(f, [(0, 0)] * (w.ndim - 1) + [(0, LANES - H)])
    return jnp.concatenate([qkv, mq, f], axis=-1)


def _fox_cols_from_compute(w, tok):
    H = tok // HEAD_DIM
    qkv, mq, f = w[..., :3 * tok], w[..., 3 * tok:3 * tok + MEM_WIDTH], w[..., 3 * tok + MEM_WIDTH:3 * tok + MEM_WIDTH + H]
    return jnp.concatenate([qkv, f, mq], axis=-1)


_BIG = ("ffn1_w_in", "ffn1_w_out", "ffn2_w_in", "ffn2_w_out", "w_out", "mem_w_kv", "fox_w_in", "gmlp_w_in")
_SMALL = ("norm_ffn1", "norm_mix", "norm_ffn2", "mem_norm", "mem_q_norm", "mem_k_norm", "fox_b_f", "fox_q_norm",
          "fox_k_norm", "gmlp_v_norm", "gmlp_w_s", "gmlp_b_s")
WEIGHT_ORDER = ("norm_ffn1", "ffn1_w_in", "ffn1_w_out", "norm_mix", "norm_ffn2", "ffn2_w_in", "ffn2_w_out", "w_out",
                "mem_norm", "mem_w_kv", "mem_q_norm", "mem_k_norm", "fox_w_in", "fox_b_f", "fox_q_norm", "fox_k_norm",
                "gmlp_w_in", "gmlp_v_norm", "gmlp_w_s", "gmlp_b_s")


def _small_slab(rows_list, index):
    sizes = [s.shape[0] for s in rows_list]
    n_rows = [-(-n // LANES) for n in sizes]
    small = jnp.concatenate([jnp.pad(s, (0, r * LANES - n)).reshape(r, LANES)
                             for s, n, r in zip(rows_list, sizes, n_rows)], axis=0)
    small = jnp.pad(small, ((0, -small.shape[0] % 64), (0, 0)))
    return _slot_in_empty(small, index, 8), sizes, n_rows


_FIRST_WEIGHTS = (("ffn1_w_in", 0), ("ffn1_w_out", 0), ("fox_w_in", 0))


def _weight_from_slab(name, slab, tok):
    if name in ("ffn1_w_in", "ffn2_w_in"):
        return slab
    if name == "fox_w_in":
        return _fox_cols_to_compute(_chips_to_cols(slab), tok)
    if name == "gmlp_w_in":
        return _chips_to_cols(slab)
    return slab.reshape(4 * slab.shape[1], slab.shape[2])


def _grad_to_slab(name, g, tok):
    if name == "fox_w_in":
        return _cols_to_chips(_fox_cols_from_compute(g, tok))
    if name == "gmlp_w_in":
        return _cols_to_chips(g)
    return g


class _Exchange:
    def __init__(self, shards, tok, chip, core):
        self.tok, self.core = tok, core
        self.half = core.reshape(1).astype(jnp.int32)
        self.chip_id = chip.reshape(1).astype(jnp.int32)
        items = [(k, i) for k in _BIG for i in range(shards[k].shape[0])]
        self.slabs = {it: _slot_in_empty(shards[it[0]][it[1]].astype(BF16), chip, 4) for it in items}
        self.late = [it for it in items if it not in _FIRST_WEIGHTS]
        self.reduced = {}
        self.early = None

    def first_weights(self, small_slab):
        got, small_all = gather_weights([self.slabs[it] for it in _FIRST_WEIGHTS], small_slab, name="gather_first")
        return {it: _weight_from_slab(it[0], s, self.tok) for it, s in zip(_FIRST_WEIGHTS, got)}, small_all

    def late_weights_rider(self):
        return gather_rider([self.slabs[it] for it in self.late])

    def accept_late_weights(self, W, got):
        for (k, i), s in zip(self.late, got):
            W[k][i] = _weight_from_slab(k, s, self.tok)

    def _pair_sums(self, G, items, small_slab, tag):
        parts = [_grad_to_slab(k, G[k][i], self.tok) for k, i in items]
        landed, small_all = exchange_with_sibling(parts, small_slab, name=f"grad_exchange_{tag}")
        pair = [pair_sum(p, l, self.half, name=f"grad_pair_sum_{k}{i}") for (k, i), p, l in zip(items, parts, landed)]
        return pair, small_all

    def early_grads_rider(self, G):
        items = [(k, i) for k in _BIG for i in range(len(G[k])) if G[k][i] is not None]
        pair, _ = self._pair_sums(G, items, None, "early")
        self.early = (items, pair)
        return scatter_rider(pair)

    def accept_early_grads(self, landed):
        items, pair = self.early
        self._chip_sums(items, pair, landed)

    def _chip_sums(self, items, pair, landed):
        for (k, i), q, l in zip(items, pair, landed):
            self.reduced[(k, i)] = chip_sum(q, l, self.chip_id, name=f"grad_chip_sum_{k}{i}")

    def finish_grads(self, G, small_slab):
        items = [(k, i) for k in _BIG for i in range(len(G[k])) if (k, i) not in self.reduced]
        pair, small_all = self._pair_sums(G, items, small_slab, "late")
        self._chip_sums(items, pair, scatter_to_chips(pair, name="grad_scatter_late"))
        order = sorted(self.reduced)
        other = share_with_sibling([self.reduced[it] for it in order], name="grad_share")
        full = {}
        for it, a, b in zip(order, [self.reduced[it] for it in order], other):
            full[it] = jnp.where(self.core == 0, jnp.concatenate([a, b]), jnp.concatenate([b, a]))
        names = sorted({k for k, _ in order})
        return {k: jnp.stack([full[(k, i)] for i in range(len(G[k]))]) for k in names}, small_all


def kernel(x, mem, norm_ffn1, ffn1_w_in, ffn1_w_out, norm_mix, norm_ffn2, ffn2_w_in, ffn2_w_out, w_out, mem_norm, mem_w_kv, mem_q_norm, mem_k_norm, fox_w_in, fox_b_f, fox_q_norm, fox_k_norm, gmlp_w_in, gmlp_v_norm, gmlp_w_s, gmlp_b_s, loss_target, m_norm_ffn1, m_ffn1_w_in, m_ffn1_w_out, m_norm_mix, m_norm_ffn2, m_ffn2_w_in, m_ffn2_w_out, m_w_out, m_mem_norm, m_mem_w_kv, m_mem_q_norm, m_mem_k_norm, m_fox_w_in, m_fox_b_f, m_fox_q_norm, m_fox_k_norm, m_gmlp_w_in, m_gmlp_v_norm, m_gmlp_w_s, m_gmlp_b_s, v_norm_ffn1, v_ffn1_w_in, v_ffn1_w_out, v_norm_mix, v_norm_ffn2, v_ffn2_w_in, v_ffn2_w_out, v_w_out, v_mem_norm, v_mem_w_kv, v_mem_q_norm, v_mem_k_norm, v_fox_w_in, v_fox_b_f, v_fox_q_norm, v_fox_k_norm, v_gmlp_w_in, v_gmlp_v_norm, v_gmlp_w_s, v_gmlp_b_s):
    w = dict(norm_ffn1=norm_ffn1, ffn1_w_in=ffn1_w_in, ffn1_w_out=ffn1_w_out, norm_mix=norm_mix, norm_ffn2=norm_ffn2,
             ffn2_w_in=ffn2_w_in, ffn2_w_out=ffn2_w_out, w_out=w_out, mem_norm=mem_norm, mem_w_kv=mem_w_kv,
             mem_q_norm=mem_q_norm, mem_k_norm=mem_k_norm, fox_w_in=fox_w_in, fox_b_f=fox_b_f, fox_q_norm=fox_q_norm,
             fox_k_norm=fox_k_norm, gmlp_w_in=gmlp_w_in, gmlp_v_norm=gmlp_v_norm, gmlp_w_s=gmlp_w_s, gmlp_b_s=gmlp_b_s)
    m = dict(norm_ffn1=m_norm_ffn1, ffn1_w_in=m_ffn1_w_in, ffn1_w_out=m_ffn1_w_out, norm_mix=m_norm_mix,
             norm_ffn2=m_norm_ffn2, ffn2_w_in=m_ffn2_w_in, ffn2_w_out=m_ffn2_w_out, w_out=m_w_out, mem_norm=m_mem_norm,
             mem_w_kv=m_mem_w_kv, mem_q_norm=m_mem_q_norm, mem_k_norm=m_mem_k_norm, fox_w_in=m_fox_w_in,
             fox_b_f=m_fox_b_f, fox_q_norm=m_fox_q_norm, fox_k_norm=m_fox_k_norm, gmlp_w_in=m_gmlp_w_in,
             gmlp_v_norm=m_gmlp_v_norm, gmlp_w_s=m_gmlp_w_s, gmlp_b_s=m_gmlp_b_s)
    v = dict(norm_ffn1=v_norm_ffn1, ffn1_w_in=v_ffn1_w_in, ffn1_w_out=v_ffn1_w_out, norm_mix=v_norm_mix,
             norm_ffn2=v_norm_ffn2, ffn2_w_in=v_ffn2_w_in, ffn2_w_out=v_ffn2_w_out, w_out=v_w_out, mem_norm=v_mem_norm,
             mem_w_kv=v_mem_w_kv, mem_q_norm=v_mem_q_norm, mem_k_norm=v_mem_k_norm, fox_w_in=v_fox_w_in,
             fox_b_f=v_fox_b_f, fox_q_norm=v_fox_q_norm, fox_k_norm=v_fox_k_norm, gmlp_w_in=v_gmlp_w_in,
             gmlp_v_norm=v_gmlp_v_norm, gmlp_w_s=v_gmlp_w_s, gmlp_b_s=v_gmlp_b_s)
    D = x.shape[-1]
    tok = D - MEM_WIDTH
    xi, yi, ci = _position()
    chip = 2 * xi + yi

    device = 4 * xi + 2 * yi + ci

    comm = _Exchange(w, tok, chip, ci)
    vn = w["gmlp_v_norm"]
    vn_slab, _, _ = _small_slab([vn.reshape(-1)], device)
    first, vn_all = comm.first_weights(vn_slab)
    W = {k: w[k] for k in _SMALL}
    W["gmlp_v_norm"] = _chips_to_cols(vn_all[0::2].reshape(4, -1)[:, :vn.size].reshape((4,) + vn.shape))
    for k in _BIG:
        W[k] = [first.get((k, i)) for i in range(w[k].shape[0])]

    loss, grad_x, g = local_step(x[0], mem[0], loss_target[0], W, comm)

    small_list = [jnp.stack(g[k]).reshape(-1) for k in _SMALL] + [loss.reshape(-1)]
    small, small_sizes, small_rows = _small_slab(small_list, device)
    red, small_all = comm.finish_grads(g, small)
    small_sum = ordered_sum(small_all, name="small_sum")
    off = 0
    for k, n, r in zip(_SMALL, small_sizes, small_rows):
        red[k] = small_sum[off:off + r].reshape(-1)[:n].reshape((-1,) + w[k].shape[1:] if k != "gmlp_v_norm"
                                                                else (w[k].shape[0], -1))
        off += r
    loss_total = small_sum[off, 0]
    vn_cols = w["gmlp_v_norm"].shape[-1]
    red["gmlp_v_norm"] = lax.dynamic_slice_in_dim(red["gmlp_v_norm"], chip * vn_cols, vn_cols, axis=-1)

    deltas, new_m, new_v = {}, {}, {}
    for k in WEIGHT_ORDER:
        wk = w[k] if w[k].ndim > 1 else w[k].reshape(1, -1)
        upd = adamw(wk, red[k].reshape(wk.shape), m[k].reshape(wk.shape), v[k].reshape(wk.shape), name=f"adamw_{k}")
        deltas[k], new_m[k], new_v[k] = (u.reshape(w[k].shape) for u in upd)
    return (loss_total, grad_x[None], *[red[k].reshape(w[k].shape) for k in WEIGHT_ORDER],
            *[deltas[k] for k in WEIGHT_ORDER], *[new_m[k] for k in WEIGHT_ORDER], *[new_v[k] for k in WEIGHT_ORDER])
```

```python
import functools
import math
from typing import Callable, NamedTuple

import jax
import jax.numpy as jnp
from jax import lax
from jax.experimental import pallas as pl
from jax.experimental.pallas import tpu as pltpu

F32 = jnp.float32
BF16 = jnp.bfloat16
EPS = 1e-6
HEAD_DIM = 64
MEM_WIDTH = 256
CHUNK = 128
LANES = 128
NEG = -1e30
VMEM_LIMIT_BYTES = 56 * 1024 * 1024
ATTN_Q_BLOCK = 1024
ATTN_K_BLOCK = 1024
ATTN_ROW_CHUNK = 1024
QK_SCALE = 0.125
MESH_ID = pl.DeviceIdType.MESH

ADAM_LR = 0.001
ADAM_B1 = 0.9
ADAM_B2 = 0.999
ADAM_EPS = 1e-08
ADAM_WD = 0.01
ADAM_STEP = 10


def _tile(n, pref, align):
    t = (min(pref, n) // align) * align
    while t >= align:
        if n % t == 0:
            return t
        t -= align
    return n


def _params(sem):
    return pltpu.CompilerParams(dimension_semantics=sem, vmem_limit_bytes=VMEM_LIMIT_BYTES)


def _dot(a, b, ca, cb):
    return lax.dot_general(a, b, (((ca,), (cb,)), ((), ())), preferred_element_type=F32)


def _sigmoid(x):
    return 1.0 / (1.0 + jnp.exp(-x))


_GELU_C = math.sqrt(2.0 / math.pi)


def _gelu(x):
    return 0.5 * x * (1.0 + jnp.tanh(_GELU_C * (x + 0.044715 * (x * x * x))))


def _gelu_grad(x):
    t = jnp.tanh(_GELU_C * (x + 0.044715 * (x * x * x)))
    return 0.5 * (1.0 + t) + 0.5 * x * (1.0 - t * t) * (_GELU_C * (1.0 + 3.0 * 0.044715 * (x * x)))


def matmul(a, b, *, ta=False, tb=False, out_dtype=F32, scale=None, res=None,
           tm=1024, tn=512, tk=1024, name):
    if ta:
        K, M = a.shape
    else:
        M, K = a.shape
    N = b.shape[0] if tb else b.shape[1]
    tm = _tile(M, tm, LANES if ta else 16)
    tn = _tile(N, tn, LANES)
    tk = _tile(K, tk, LANES)
    nk = K // tk
    a_spec = pl.BlockSpec((tk, tm), lambda i, j, k: (k, i)) if ta else pl.BlockSpec((tm, tk), lambda i, j, k: (i, k))
    b_spec = pl.BlockSpec((tn, tk), lambda i, j, k: (j, k)) if tb else pl.BlockSpec((tk, tn), lambda i, j, k: (k, j))
    o_spec = pl.BlockSpec((tm, tn), lambda i, j, k: (i, j))
    ca, cb = (0 if ta else 1), (1 if tb else 0)
    has_res = res is not None

    def body(*refs):
        a_ref, b_ref = refs[0], refs[1]
        res_ref = refs[2] if has_res else None
        o_ref = refs[3] if has_res else refs[2]
        acc_ref = refs[-1]
        k = pl.program_id(2)
        prod = _dot(a_ref[...].astype(BF16), b_ref[...].astype(BF16), ca, cb)

        def finish(acc):
            if scale is not None:
                acc = acc * scale
            if has_res:
                acc = res_ref[...] + acc
            o_ref[...] = acc.astype(out_dtype)

        if nk == 1:
            finish(prod)
        else:
            @pl.when(k == 0)
            def _():
                acc_ref[...] = prod

            @pl.when(k > 0)
            def _():
                acc_ref[...] += prod

            @pl.when(k == nk - 1)
            def _():
                finish(acc_ref[...])

    in_specs = [a_spec, b_spec] + ([o_spec] if has_res else [])
    args = (a, b) + ((res,) if has_res else ())
    return pl.pallas_call(
        body, grid=(M // tm, N // tn, nk), in_specs=in_specs, out_specs=o_spec,
        out_shape=jax.ShapeDtypeStruct((M, N), out_dtype),
        scratch_shapes=[pltpu.VMEM((tm, tn) if nk > 1 else (8, LANES), F32)],
        compiler_params=_params(("parallel", "parallel", "arbitrary")), name=name)(*args)


def swiglu_fwd(h, w_slab, *, name):
    S, D = h.shape
    Fc = w_slab.shape[-1]
    tm = _tile(S, 1024, 16)

    def body(h_ref, wa_ref, wb_ref, a_ref, b_ref, act_ref):
        hv = h_ref[...]
        a = _dot(hv, wa_ref[...], 1, 0)
        b = _dot(hv, wb_ref[...], 1, 0)
        a_ref[...] = a.astype(BF16)
        b_ref[...] = b.astype(BF16)
        act_ref[...] = (a * _sigmoid(a) * b).astype(BF16)

    out = pl.BlockSpec((tm, Fc), lambda j, i: (i, j))
    return pl.pallas_call(
        body, grid=(2, S // tm),
        in_specs=[pl.BlockSpec((tm, D), lambda j, i: (i, 0)),
                  pl.BlockSpec((None, D, Fc), lambda j, i: (j, 0, 0)),
                  pl.BlockSpec((None, D, Fc), lambda j, i: (j + 2, 0, 0))],
        out_specs=[out, out, out],
        out_shape=[jax.ShapeDtypeStruct((S, 2 * Fc), BF16)] * 3,
        compiler_params=_params(("parallel", "parallel")), name=name)(h, w_slab, w_slab)


def swiglu_bwd(dy, w_out, a, b, *, name):
    S, D = dy.shape
    F = w_out.shape[0]
    fc = F // 2
    tm = _tile(S, 512, 16)

    def body(dy_ref, w_ref, a_ref, b_ref, da_ref, db_ref):
        dact = 0.5 * _dot(dy_ref[...].astype(BF16), w_ref[...], 1, 1)
        av = a_ref[...].astype(F32)
        sg = _sigmoid(av)
        da_ref[...] = (dact * b_ref[...].astype(F32) * (sg * (1.0 + av * (1.0 - sg)))).astype(BF16)
        db_ref[...] = (dact * (av * sg)).astype(BF16)

    blk = pl.BlockSpec((tm, fc), lambda j, i: (i, j))
    return pl.pallas_call(
        body, grid=(2, S // tm),
        in_specs=[pl.BlockSpec((tm, D), lambda j, i: (i, 0)), pl.BlockSpec((fc, D), lambda j, i: (j, 0)), blk, blk],
        out_specs=[blk, blk],
        out_shape=[jax.ShapeDtypeStruct((S, F), BF16), jax.ShapeDtypeStruct((S, F), BF16)],
        compiler_params=_params(("parallel", "parallel")), name=name)(dy, w_out, a, b)


def ffn_dh(da, db, w_slab, x, g, dy, *, name):
    S, F = da.shape
    D, Fc = w_slab.shape[-2:]
    tm = _tile(S, 1024, 16)
    sub = _tile(tm, 256, 8)

    def body(da_ref, db_ref, w_ref, x_ref, g_ref, dy_ref, dx_ref, dg_ref, acc_ref):
        i, k = pl.program_id(0), pl.program_id(1)

        @pl.when(k == 0)
        def _():
            acc_ref[...] = jnp.zeros_like(acc_ref)

        @pl.when(k < 2)
        def _():
            acc_ref[...] += _dot(da_ref[...], w_ref[...], 1, 1)

        @pl.when(k >= 2)
        def _():
            acc_ref[...] += _dot(db_ref[...], w_ref[...], 1, 1)

        @pl.when(k == 3)
        def _():
            part = None
            for c in range(tm // sub):
                rows = pl.ds(c * sub, sub)
                xv, dh = x_ref[rows, :], acc_ref[rows, :]
                r = lax.rsqrt(jnp.mean(xv * xv, axis=-1, keepdims=True) + EPS)
                u = dh * g_ref[...]
                dx_ref[rows, :] = dy_ref[rows, :] + (r * u - xv * (r * r * r) * jnp.mean(xv * u, axis=-1, keepdims=True))
                p = jnp.sum(dh * xv * r, axis=0, keepdims=True)
                part = p if part is None else part + p

            @pl.when(i == 0)
            def _():
                dg_ref[...] = part

            @pl.when(i > 0)
            def _():
                dg_ref[...] += part

    row = pl.BlockSpec((tm, D), lambda i, k: (i, 0))
    vec = pl.BlockSpec((1, D), lambda i, k: (0, 0))
    return pl.pallas_call(
        body, grid=(S // tm, 4),
        in_specs=[pl.BlockSpec((tm, Fc), lambda i, k: (i, jnp.minimum(k, 1))),
                  pl.BlockSpec((tm, Fc), lambda i, k: (i, jnp.maximum(k - 2, 0))),
                  pl.BlockSpec((None, D, Fc), lambda i, k: (k, 0, 0)), row, vec, row],
        out_specs=[row, vec],
        out_shape=[jax.ShapeDtypeStruct((S, D), F32), jax.ShapeDtypeStruct((1, D), F32)],
        scratch_shapes=[pltpu.VMEM((tm, D), F32)],
        compiler_params=_params(("arbitrary", "arbitrary")), name=name)(da, db, w_slab, x, g.reshape(1, D), dy)


def mix_dh(dproj, w, x, g, dy, *, name):
    S, N = dproj.shape
    D = w.shape[0]
    tm = _tile(S, 1024, 16)
    sub = _tile(tm, 256, 8)
    tk = _tile(N, 896, LANES)
    nk = N // tk

    def body(p_ref, w_ref, x_ref, g_ref, dy_ref, dx_ref, dg_ref, acc_ref):
        i, k = pl.program_id(0), pl.program_id(1)
        prod = _dot(p_ref[...], w_ref[...], 1, 1)

        @pl.when(k == 0)
        def _():
            acc_ref[...] = prod

        @pl.when(k > 0)
        def _():
            acc_ref[...] += prod

        @pl.when(k == nk - 1)
        def _():
            part = None
            for c in range(tm // sub):
                rows = pl.ds(c * sub, sub)
                xv, dh = x_ref[rows, :], acc_ref[rows, :]
                r = lax.rsqrt(jnp.mean(xv * xv, axis=-1, keepdims=True) + EPS)
                u = dh * g_ref[...]
                dx_ref[rows, :] = dy_ref[rows, :] + (r * u - xv * (r * r * r) * jnp.mean(xv * u, axis=-1, keepdims=True))
                pp = jnp.sum(dh * xv * r, axis=0, keepdims=True)
                part = pp if part is None else part + pp

            @pl.when(i == 0)
            def _():
                dg_ref[...] = part

            @pl.when(i > 0)
            def _():
                dg_ref[...] += part

    row = pl.BlockSpec((tm, D), lambda i, k: (i, 0))
    vec = pl.BlockSpec((1, D), lambda i, k: (0, 0))
    return pl.pallas_call(
        body, grid=(S // tm, nk),
        in_specs=[pl.BlockSpec((tm, tk), lambda i, k: (i, k)), pl.BlockSpec((D, tk), lambda i, k: (0, k)), row, vec, row],
        out_specs=[row, vec],
        out_shape=[jax.ShapeDtypeStruct((S, D), F32), jax.ShapeDtypeStruct((1, D), F32)],
        scratch_shapes=[pltpu.VMEM((tm, D), F32)],
        compiler_params=_params(("arbitrary", "arbitrary")), name=name)(dproj, w, x, g.reshape(1, D), dy)


def grad_cols(h, da, db, *, name):
    S, D = h.shape
    Fc = da.shape[1] // 2
    tk = _tile(S, 1024, 16)
    nk = S // tk

    def body(h_ref, da_ref, db_ref, o_ref, acc_ref):
        ch, k = pl.program_id(0), pl.program_id(1)

        @pl.when(k == 0)
        def _():
            acc_ref[...] = jnp.zeros_like(acc_ref)

        @pl.when(ch < 2)
        def _():
            acc_ref[...] += _dot(h_ref[...], da_ref[...], 0, 0)

        @pl.when(ch >= 2)
        def _():
            acc_ref[...] += _dot(h_ref[...], db_ref[...], 0, 0)

        @pl.when(k == nk - 1)
        def _():
            o_ref[...] = acc_ref[...]

    return pl.pallas_call(
        body, grid=(4, nk),
        in_specs=[pl.BlockSpec((tk, D), lambda ch, k: (k, 0)),
                  pl.BlockSpec((tk, Fc), lambda ch, k: (jnp.where(ch < 2, k, 0), jnp.minimum(ch, 1))),
                  pl.BlockSpec((tk, Fc), lambda ch, k: (jnp.where(ch >= 2, k, 0), jnp.maximum(ch - 2, 0)))],
        out_specs=pl.BlockSpec((None, D, Fc), lambda ch, k: (ch, 0, 0)),
        out_shape=jax.ShapeDtypeStruct((4, D, Fc), F32),
        scratch_shapes=[pltpu.VMEM((D, Fc), F32)],
        compiler_params=_params(("parallel", "arbitrary")), name=name)(h, da, db)


def grad_rows(a, b, *, scale=None, name):
    S, M = a.shape
    N = b.shape[1]
    R = M // 4
    tn = _tile(N, 512, LANES)
    tk = _tile(S, 1024, 16)
    nk = S // tk

    def body(a_ref, b_ref, o_ref, acc_ref):
        k = pl.program_id(1)

        @pl.when(k == 0)
        def _():
            acc_ref[...] = jnp.zeros_like(acc_ref)

        acc_ref[...] += _dot(a_ref[...].astype(BF16), b_ref[...].astype(BF16), 0, 0)

        @pl.when(k == nk - 1)
        def _():
            for d in range(4):
                part = acc_ref[d * R:(d + 1) * R, :]
                o_ref[d] = part if scale is None else part * scale

    return pl.pallas_call(
        body, grid=(N // tn, nk),
        in_specs=[pl.BlockSpec((tk, M), lambda j, k: (k, 0)), pl.BlockSpec((tk, tn), lambda j, k: (k, j))],
        out_specs=pl.BlockSpec((4, R, tn), lambda j, k: (0, 0, j)),
        out_shape=jax.ShapeDtypeStruct((4, R, N), F32),
        scratch_shapes=[pltpu.VMEM((M, tn), F32)],
        compiler_params=_params(("parallel", "arbitrary")), name=name)(a, b)


def rms_fwd(x, g, *, name):
    S, D = x.shape
    ts = _tile(S, 1024, 16)

    def body(x_ref, g_ref, h_ref):
        xv = x_ref[...]
        r = lax.rsqrt(jnp.mean(xv * xv, axis=-1, keepdims=True) + EPS)
        h_ref[...] = (xv * r * g_ref[...]).astype(BF16)

    return pl.pallas_call(
        body, grid=(S // ts,),
        in_specs=[pl.BlockSpec((ts, D), lambda i: (i, 0)), pl.BlockSpec((1, D), lambda i: (0, 0))],
        out_specs=pl.BlockSpec((ts, D), lambda i: (i, 0)),
        out_shape=jax.ShapeDtypeStruct((S, D), BF16),
        compiler_params=_params(("parallel",)), name=name)(x, g.reshape(1, D))


def rms_bwd(x, dh, g, res, *, name):
    S, D = x.shape
    ts = _tile(S, 512, 16)
    has_res = res is not None

    def body(*refs):
        x_ref, dh_ref, g_ref = refs[:3]
        res_ref = refs[3] if has_res else None
        dx_ref, dg_ref = refs[-2:]
        i = pl.program_id(0)
        xv, dhv = x_ref[...], dh_ref[...].astype(F32)
        r = lax.rsqrt(jnp.mean(xv * xv, axis=-1, keepdims=True) + EPS)
        u = dhv * g_ref[...]
        dx = r * u - xv * (r * r * r) * jnp.mean(xv * u, axis=-1, keepdims=True)
        if has_res:
            dx = res_ref[...] + dx
        dx_ref[...] = dx
        part = jnp.sum(dhv * xv * r, axis=0, keepdims=True)

        @pl.when(i == 0)
        def _():
            dg_ref[...] = part

        @pl.when(i > 0)
        def _():
            dg_ref[...] += part

    row = pl.BlockSpec((ts, D), lambda i: (i, 0))
    vec = pl.BlockSpec((1, D), lambda i: (0, 0))
    args = (x, dh, g.reshape(1, D)) + ((res,) if has_res else ())
    return pl.pallas_call(
        body, grid=(S // ts,), in_specs=[row, row, vec] + ([row] if has_res else []),
        out_specs=[row, vec],
        out_shape=[jax.ShapeDtypeStruct((S, D), F32), jax.ShapeDtypeStruct((1, D), F32)],
        compiler_params=_params(("arbitrary",)), name=name)(*args)


def _low_half(shape):
    return lax.broadcasted_iota(jnp.int32, shape, len(shape) - 1) < HEAD_DIM


def _half_sums(x, low):
    sa = jnp.sum(jnp.where(low, x, 0.0), axis=1, keepdims=True)
    sb = jnp.sum(jnp.where(low, 0.0, x), axis=1, keepdims=True)
    return jnp.where(low, sa, sb)


def pairnorm_fwd(x, col0, n_pairs, g, *, scale=None, name):
    S = x.shape[0]
    ts = _tile(S, 512, 16)
    W = n_pairs * LANES
    assert col0 % n_pairs == 0

    def body(x_ref, g_ref, o_ref):
        for p in range(n_pairs):
            cols = pl.ds(p * LANES, LANES)
            xv = x_ref[:, cols]
            r = lax.rsqrt(_half_sums(xv * xv, _low_half(xv.shape)) * (1.0 / HEAD_DIM) + EPS)
            y = xv * r * g_ref[...]
            o_ref[:, cols] = (y if scale is None else y * scale).astype(BF16)

    return pl.pallas_call(
        body, grid=(S // ts,),
        in_specs=[pl.BlockSpec((ts, W), lambda i: (i, col0 // n_pairs)), pl.BlockSpec((1, LANES), lambda i: (0, 0))],
        out_specs=pl.BlockSpec((ts, W), lambda i: (i, 0)),
        out_shape=jax.ShapeDtypeStruct((S, W), BF16),
        compiler_params=_params(("parallel",)), name=name)(x, jnp.tile(g.reshape(1, HEAD_DIM), (1, 2)))


def pairnorm_bwd(x, col0, n_pairs, dy, g, *, name):
    S = x.shape[0]
    ts = _tile(S, 512, 16)
    W = n_pairs * LANES
    assert col0 % n_pairs == 0

    def body(x_ref, dy_ref, g_ref, dx_ref, dg_ref):
        part = None
        for p in range(n_pairs):
            cols = pl.ds(p * LANES, LANES)
            xv, dyv = x_ref[:, cols], dy_ref[:, cols]
            low = _low_half(xv.shape)
            r = lax.rsqrt(_half_sums(xv * xv, low) * (1.0 / HEAD_DIM) + EPS)
            u = dyv * g_ref[...]
            dx_ref[:, cols] = r * u - xv * (r * r * r) * (_half_sums(xv * u, low) * (1.0 / HEAD_DIM))
            pp = jnp.sum(dyv * xv * r, axis=0, keepdims=True)
            part = pp if part is None else part + pp

        @pl.when(pl.program_id(0) == 0)
        def _():
            dg_ref[...] = part

        @pl.when(pl.program_id(0) > 0)
        def _():
            dg_ref[...] += part

    vec = pl.BlockSpec((1, LANES), lambda i: (0, 0))
    blk = pl.BlockSpec((ts, W), lambda i: (i, 0))
    return pl.pallas_call(
        body, grid=(S // ts,),
        in_specs=[pl.BlockSpec((ts, W), lambda i: (i, col0 // n_pairs)), blk, vec], out_specs=[blk, vec],
        out_shape=[jax.ShapeDtypeStruct((S, W), F32), jax.ShapeDtypeStruct((1, LANES), F32)],
        compiler_params=_params(("arbitrary",)), name=name)(x, dy, jnp.tile(g.reshape(1, HEAD_DIM), (1, 2)))


def _split3(x):
    x1 = x.astype(BF16)
    r1 = x - x1.astype(F32)
    x2 = r1.astype(BF16)
    x3 = (r1 - x2.astype(F32)).astype(BF16)
    return x1, x2, x3


def _tri_ones(n, lower):
    r = lax.broadcasted_iota(jnp.int32, (n, n), 0)
    c = lax.broadcasted_iota(jnp.int32, (n, n), 1)
    return jnp.where((c <= r) if lower else (c >= r), 1.0, 0.0).astype(BF16)


def fgate_fwd(z, col0, bias, *, name):
    S, L = z.shape[0], LANES
    tb = _tile(S, 256, 16)

    def body(z_ref, b_ref, c_ref, carry):
        i = pl.program_id(0)

        @pl.when(i == 0)
        def _():
            carry[...] = jnp.zeros_like(carry)

        zz = z_ref[...] + b_ref[...]
        lf = jnp.minimum(zz, 0.0) - jnp.log(1.0 + jnp.exp(-jnp.abs(zz)))
        tri = _tri_ones(tb, True)
        x1, x2, x3 = _split3(lf)
        c = (_dot(tri, x1, 1, 0) + _dot(tri, x2, 1, 0)) + _dot(tri, x3, 1, 0) + carry[...]
        c_ref[...] = c
        carry[...] += jnp.sum(lf, axis=0, keepdims=True)

    return pl.pallas_call(
        body, grid=(S // tb,),
        in_specs=[pl.BlockSpec((tb, L), lambda i: (i, col0)), pl.BlockSpec((1, L), lambda i: (0, 0))],
        out_specs=pl.BlockSpec((tb, L), lambda i: (i, 0)),
        out_shape=jax.ShapeDtypeStruct((S, L), F32),
        scratch_shapes=[pltpu.VMEM((1, L), F32)],
        compiler_params=_params(("arbitrary",)), name=name)(z, bias)


def fgate_bwd(z, col0, bias, drs, dcs, *, name):
    S, L = z.shape[0], LANES
    n_pairs = drs.shape[0]
    tb = _tile(S, 256, 16)
    nb = S // tb

    def body(z_ref, b_ref, drs_ref, dcs_ref, dz_ref, db_ref, carry):
        i = pl.program_id(0)

        @pl.when(i == 0)
        def _():
            carry[...] = jnp.zeros_like(carry)

        tri = _tri_ones(tb, False)
        lane = lax.broadcasted_iota(jnp.int32, (tb, L), 1)
        dc = -dcs_ref[...]
        for h in range(2 * n_pairs):
            dc = dc + jnp.where(lane == h, jnp.sum(drs_ref[h // 2, h % 2], axis=1, keepdims=True), 0.0)
        x1, x2, x3 = _split3(dc)
        dlf = (_dot(tri, x1, 1, 0) + _dot(tri, x2, 1, 0)) + _dot(tri, x3, 1, 0) + carry[...]
        carry[...] += jnp.sum(dc, axis=0, keepdims=True)
        dz = dlf * _sigmoid(-(z_ref[...] + b_ref[...]))
        dz_ref[...] = dz
        part = jnp.sum(dz, axis=0, keepdims=True)

        @pl.when(i == 0)
        def _():
            db_ref[...] = part

        @pl.when(i > 0)
        def _():
            db_ref[...] += part

    rev = pl.BlockSpec((tb, L), lambda i: (nb - 1 - i, 0))
    vec = pl.BlockSpec((1, L), lambda i: (0, 0))
    return pl.pallas_call(
        body, grid=(nb,),
        in_specs=[pl.BlockSpec((tb, L), lambda i: (nb - 1 - i, col0)), vec,
                  pl.BlockSpec((n_pairs, 2, tb, L), lambda i: (0, 0, nb - 1 - i, 0)), rev],
        out_specs=[rev, vec],
        out_shape=[jax.ShapeDtypeStruct((S, L), F32), jax.ShapeDtypeStruct((1, L), F32)],
        scratch_shapes=[pltpu.VMEM((1, L), F32)],
        compiler_params=_params(("arbitrary",)), name=name)(z, bias, drs, dcs)


def _one_head(x, low, a):
    return jnp.where(low if a == 0 else jnp.logical_not(low), x, jnp.zeros_like(x))


class Rider(NamedTuple):
    inputs: tuple
    out_shapes: tuple
    aliases: dict
    sems: tuple
    plan: Callable


def _with_rider(rider, n_in, n_out, n_scratch):
    if rider is None:
        return [], [], [], [], {}, lambda refs: (refs[:n_in], refs[n_in:n_in + n_out], refs[n_in + n_out:], None)
    e_in, e_out = len(rider.inputs), len(rider.out_shapes)

    def split(refs):
        ins, r_in = refs[:n_in], refs[n_in:n_in + e_in]
        o0 = n_in + e_in
        outs, r_out = refs[o0:o0 + n_out], refs[o0 + n_out:o0 + n_out + e_out]
        s0 = o0 + n_out + e_out
        return ins, outs, refs[s0:s0 + n_scratch], rider.plan(r_in, r_out, refs[s0 + n_scratch:])

    aliases = {n_in + a: n_out + b for a, b in rider.aliases.items()}
    return list(rider.inputs), [_ANY] * e_in, list(rider.out_shapes), [_ANY] * e_out, aliases, split


def attn_fwd(q, q0, k, k0, v, v0, n_pairs, decay, *, causal, rider=None, name):
    Sq, Sk = q.shape[0], k.shape[0]
    tq = _tile(Sq, ATTN_Q_BLOCK if causal else 2 * ATTN_Q_BLOCK, LANES)
    tk = _tile(Sk, ATTN_K_BLOCK, LANES)
    nq, nk = Sq // tq, Sk // tk
    bias = decay is not None
    rs = _tile(tq, ATTN_ROW_CHUNK, 16)
    r_args, r_in_specs, r_shapes, r_out_specs, aliases, split = _with_rider(rider, 4 if bias else 3, 2, 4)

    def row_sum_lanes(acc, low, a):
        other = jnp.logical_not(low) if a == 0 else low
        return jnp.max(jnp.where(other, acc, 0.0), axis=1, keepdims=True)

    live = [(i, j) for i in range(nq) for j in range(nk) if not causal or j * tk <= i * tq + tq - 1]
    n_live = len(live)

    def body(i_tab, j_tab, *refs):
        ins, (o_ref, lse_ref), scratch, ride = split(refs)
        m_sc, acc_sc = scratch[:2], scratch[2:]
        q_ref, k_ref, v_ref = ins[:3]
        ck_ref = ins[3] if bias else None
        pr, t = pl.program_id(0), pl.program_id(1)
        i, j = i_tab[t], j_tab[t]
        last_j = (i * tq + tq - 1) // tk if causal else nk - 1
        if ride is not None:
            pl.when(jnp.logical_and(pr == 0, t == 0))(ride[0])

        @pl.when(j == 0)
        def _():
            for a in range(2):
                m_sc[a][...] = jnp.full_like(m_sc[a], NEG)
                acc_sc[a][...] = jnp.zeros_like(acc_sc[a])

        def compute(masked):
            kv, vv = k_ref[...], v_ref[...].astype(BF16)
            low_k = _low_half(kv.shape)
            va = [jnp.where(low_k if a == 0 else jnp.logical_not(low_k), vv, jnp.ones_like(vv)) for a in range(2)]
            for r in range(tq // rs):
                rows = pl.ds(r * rs, rs)
                qv = q_ref[rows, :]
                low = _low_half(qv.shape)
                for a in range(2):
                    s = _dot(_one_head(qv, low, a), kv, 1, 1)
                    if bias:
                        s = s - ck_ref[a]
                    if masked:
                        row = i * tq + r * rs + lax.broadcasted_iota(jnp.int32, (rs, tk), 0)
                        col = j * tk + lax.broadcasted_iota(jnp.int32, (rs, tk), 1)
                        s = jnp.where(col <= row, s, NEG)
                    m_prev = m_sc[a][rows, :]
                    m_new = jnp.maximum(m_prev, jnp.max(s, axis=1, keepdims=True))
                    alpha = jnp.exp(m_prev - m_new)
                    p = jnp.exp(s - m_new).astype(BF16)
                    acc_sc[a][rows, :] = alpha * acc_sc[a][rows, :] + _dot(p, va[a], 1, 0)
                    m_sc[a][rows, :] = m_new

        if causal:
            crosses = j * tk + (tk - 1) > i * tq
            pl.when(crosses)(functools.partial(compute, True))
            pl.when(jnp.logical_not(crosses))(functools.partial(compute, False))
        else:
            compute(False)

        @pl.when(j == last_j)
        def _():
            low = _low_half((tq, LANES))
            l = [row_sum_lanes(acc_sc[a][...], low, a) for a in range(2)]
            o_ref[...] = jnp.where(low, acc_sc[0][...] / l[0], acc_sc[1][...] / l[1])
            for a in range(2):
                lse_ref[a] = m_sc[a][...] + jnp.log(l[a])

        if ride is not None:
            pl.when(jnp.logical_and(pr == n_pairs - 1, t == n_live - 1))(ride[1])

    in_specs = [pl.BlockSpec((tq, LANES), lambda p, t, it, jt: (it[t], q0 + p)),
                pl.BlockSpec((tk, LANES), lambda p, t, it, jt: (jt[t], k0 + p)),
                pl.BlockSpec((tk, LANES), lambda p, t, it, jt: (jt[t], v0 + p))]
    args = [q, k, v]
    if bias:
        in_specs.append(pl.BlockSpec((None, 2, 1, tk), lambda p, t, it, jt: (p, 0, 0, jt[t])))
        args.append(decay)
    tabs = [jnp.asarray([b[n] for b in live], jnp.int32) for n in range(2)]
    out = pl.pallas_call(
        body,
        grid_spec=pltpu.PrefetchScalarGridSpec(
            num_scalar_prefetch=2, grid=(n_pairs, n_live), in_specs=in_specs + r_in_specs,
            out_specs=[pl.BlockSpec((tq, LANES), lambda p, t, it, jt: (it[t], p)),
                       pl.BlockSpec((None, 2, tq, 1), lambda p, t, it, jt: (p, 0, it[t], 0))] + r_out_specs,
            scratch_shapes=[pltpu.VMEM((tq, 1), F32)] * 2 + [pltpu.VMEM((tq, LANES), F32)] * 2
            + (list(rider.sems) if rider else [])),
        out_shape=[jax.ShapeDtypeStruct((Sq, n_pairs * LANES), F32),
                   jax.ShapeDtypeStruct((n_pairs, 2, Sq, 1), F32)] + r_shapes,
        input_output_aliases={2 + a: b for a, b in aliases.items()},
        compiler_params=_params(("arbitrary", "arbitrary") if rider else ("parallel", "arbitrary")),
        name=name)(*tabs, *args, *r_args)
    return out[0], out[1], out[2:]


def attn_delta(o, do, do0, n_pairs, *, name):
    S = o.shape[0]
    ts = _tile(S, 512, 16)
    W = n_pairs * LANES
    assert do0 % n_pairs == 0

    def body(o_ref, do_ref, out_ref):
        for p in range(n_pairs):
            cols = pl.ds(p * LANES, LANES)
            prod = o_ref[:, cols] * do_ref[:, cols]
            low = _low_half(prod.shape)
            out_ref[p, 0] = jnp.sum(jnp.where(low, prod, 0.0), axis=1, keepdims=True)
            out_ref[p, 1] = jnp.sum(jnp.where(low, 0.0, prod), axis=1, keepdims=True)

    return pl.pallas_call(
        body, grid=(S // ts,),
        in_specs=[pl.BlockSpec((ts, W), lambda i: (i, 0)), pl.BlockSpec((ts, W), lambda i: (i, do0 // n_pairs))],
        out_specs=pl.BlockSpec((n_pairs, 2, ts, 1), lambda i: (0, 0, i, 0)),
        out_shape=jax.ShapeDtypeStruct((n_pairs, 2, S, 1), F32),
        compiler_params=_params(("parallel",)), name=name)(o, do)


def attn_bwd(q, q0, k, k0, v, v0, do, do0, n_pairs, lse, delta, decay, *, causal, rider=None, name):
    Sq, Sk = q.shape[0], k.shape[0]
    tq = _tile(Sq, ATTN_Q_BLOCK if causal else 2 * ATTN_Q_BLOCK, LANES)
    tk = _tile(Sk, ATTN_K_BLOCK, LANES)
    nq, nk = Sq // tq, Sk // tk
    bias = decay is not None

    rs = _tile(tq, ATTN_ROW_CHUNK, 16)
    r_args, r_in_specs, r_shapes, r_out_specs, aliases, split = _with_rider(
        rider, 7 if bias else 6, 5 if bias else 3, 0)

    live = [(i, j) for j in range(nk) for i in range(nq) if not causal or j * tk <= i * tq + tq - 1]
    n_live = len(live)

    def body(i_tab, j_tab, *refs):
        ins, outs, _, ride = split(refs)
        q_ref, k_ref, v_ref, do_ref, lse_ref, dl_ref = ins[:6]
        ck_ref = ins[6] if bias else None
        dq_ref, dk_ref, dv_ref = outs[:3]
        dcs_ref, drs_ref = (outs[3], outs[4]) if bias else (None, None)
        pr, t = pl.program_id(0), pl.program_id(1)
        i, j = i_tab[t], j_tab[t]
        first_i = (j * tk) // tq if causal else 0
        if ride is not None:
            pl.when(jnp.logical_and(pr == 0, t == 0))(ride[0])

        @pl.when(i == first_i)
        def _():
            dk_ref[...] = jnp.zeros_like(dk_ref)
            dv_ref[...] = jnp.zeros_like(dv_ref)
            if bias:
                dcs_ref[...] = jnp.zeros_like(dcs_ref)

        def compute(masked):
            kv, vv = k_ref[...], v_ref[...].astype(BF16)
            low_k = _low_half(kv.shape)
            ka = [_one_head(kv, low_k, a) for a in range(2)]
            for r in range(tq // rs):
                here = pl.ds(r * rs, rs)
                rows = pl.ds(pl.multiple_of(i * tq + r * rs, rs), rs)
                qv, dov = q_ref[here, :], do_ref[here, :].astype(BF16)
                low = _low_half(qv.shape)
                dq_part, dk_part, dv_part, row_parts, col_parts = None, None, None, [], []
                for a in range(2):
                    qa, doa = _one_head(qv, low, a), _one_head(dov, low, a)
                    s = _dot(qa, kv, 1, 1)
                    if bias:
                        s = s - ck_ref[a]
                    p = jnp.exp(s - lse_ref[a, here])
                    if masked:
                        row = i * tq + r * rs + lax.broadcasted_iota(jnp.int32, (rs, tk), 0)
                        col = j * tk + lax.broadcasted_iota(jnp.int32, (rs, tk), 1)
                        p = jnp.where(col <= row, p, 0.0)
                    dv_a = _dot(p.astype(BF16), doa, 0, 0)
                    dp = _dot(doa, vv, 1, 1)
                    ds = p * (dp - dl_ref[a, here])
                    dsb = ds.astype(BF16)
                    dk_a = _dot(dsb, qa, 0, 0)
                    if bias:
                        col_parts.append(jnp.sum(ds, axis=0, keepdims=True))
                        lanes = ds[:, :LANES]
                        for c in range(1, tk // LANES):
                            lanes = lanes + ds[:, c * LANES:(c + 1) * LANES]
                        row_parts.append(lanes)
                    part = _dot(dsb, ka[a], 1, 0) * QK_SCALE
                    dq_part = part if dq_part is None else dq_part + part
                    dk_part = dk_a if dk_part is None else dk_part + dk_a
                    dv_part = dv_a if dv_part is None else dv_part + dv_a
                dv_ref[...] += dv_part
                dk_ref[...] += dk_part
                for a, cp in enumerate(col_parts):
                    dcs_ref[a] += cp

                @pl.when(j == 0)
                def _(rows=rows, dq_part=dq_part, row_parts=row_parts):
                    dq_ref[rows, :] = dq_part
                    for a, rp in enumerate(row_parts):
                        drs_ref[a, rows, :] = rp

                @pl.when(j > 0)
                def _(rows=rows, dq_part=dq_part, row_parts=row_parts):
                    dq_ref[rows, :] += dq_part
                    for a, rp in enumerate(row_parts):
                        drs_ref[a, rows, :] += rp

        if causal:
            crosses = j * tk + (tk - 1) > i * tq
            pl.when(crosses)(functools.partial(compute, True))
            pl.when(jnp.logical_not(crosses))(functools.partial(compute, False))
        else:
            compute(False)

        if ride is not None:
            pl.when(jnp.logical_and(pr == n_pairs - 1, t == n_live - 1))(ride[1])

    col1 = pl.BlockSpec((None, 2, tq, 1), lambda p, t, it, jt: (p, 0, it[t], 0))
    in_specs = [pl.BlockSpec((tq, LANES), lambda p, t, it, jt: (it[t], q0 + p)),
                pl.BlockSpec((tk, LANES), lambda p, t, it, jt: (jt[t], k0 + p)),
                pl.BlockSpec((tk, LANES), lambda p, t, it, jt: (jt[t], v0 + p)),
                pl.BlockSpec((tq, LANES), lambda p, t, it, jt: (it[t], do0 + p)), col1, col1]
    args = [q, k, v, do, lse, delta]
    kout = pl.BlockSpec((tk, LANES), lambda p, t, it, jt: (jt[t], p))
    out_specs = [pl.BlockSpec((Sq, LANES), lambda p, t, it, jt: (0, p)), kout, kout]
    out_shape = [jax.ShapeDtypeStruct((Sq, n_pairs * LANES), F32), jax.ShapeDtypeStruct((Sk, n_pairs * LANES), F32),
                 jax.ShapeDtypeStruct((Sk, n_pairs * LANES), F32)]
    if bias:
        in_specs.append(pl.BlockSpec((None, 2, 1, tk), lambda p, t, it, jt: (p, 0, 0, jt[t])))
        args.append(decay)
        out_specs += [pl.BlockSpec((None, 2, 1, tk), lambda p, t, it, jt: (p, 0, 0, jt[t])),
                      pl.BlockSpec((None, 2, Sq, LANES), lambda p, t, it, jt: (p, 0, 0, 0))]
        out_shape += [jax.ShapeDtypeStruct((n_pairs, 2, 1, Sk), F32),
                      jax.ShapeDtypeStruct((n_pairs, 2, Sq, LANES), F32)]
    n_own = len(out_shape)
    tabs = [jnp.asarray([b[n] for b in live], jnp.int32) for n in range(2)]
    out = pl.pallas_call(
        body,
        grid_spec=pltpu.PrefetchScalarGridSpec(
            num_scalar_prefetch=2, grid=(n_pairs, n_live), in_specs=in_specs + r_in_specs,
            out_specs=out_specs + r_out_specs, scratch_shapes=list(rider.sems) if rider else []),
        out_shape=out_shape + r_shapes,
        input_output_aliases={2 + a: b for a, b in aliases.items()},
        compiler_params=_params(("arbitrary", "arbitrary") if rider else ("parallel", "arbitrary")),
        name=name)(*tabs, *args, *r_args)
    return tuple(out[:n_own]), out[n_own:]


def _tril_mask(n):
    r = lax.broadcasted_iota(jnp.int32, (n, n), 0)
    c = lax.broadcasted_iota(jnp.int32, (n, n), 1)
    return c <= r


def _gmlp_operands(v_gain, w_s, b_s):
    G = w_s.shape[0]
    return (v_gain.reshape(G // 2, 1, LANES), w_s.reshape(G // 2, 2, CHUNK, CHUNK), b_s.reshape(G // 2, 2, CHUNK, 1))


def _gmlp_gate(wt, vh, b_ref, low):
    gate = _dot(wt[0], _one_head(vh, low, 0), 1, 0) + _dot(wt[1], _one_head(vh, low, 1), 1, 0)
    return gate + jnp.where(low, b_ref[0], b_ref[1])


def gmlp_fwd(proj, v0, n_pairs, vg, w, b, *, name):
    S = proj.shape[0]
    ts = _tile(S, 1024, CHUNK)

    def body(up_ref, vp_ref, vg_ref, w_ref, b_ref, o_ref):
        mask = _tril_mask(CHUNK)
        wt = [jnp.where(mask, w_ref[a], 0.0).astype(BF16) for a in range(2)]
        low = _low_half((CHUNK, LANES))
        for c in range(ts // CHUNK):
            sl = pl.ds(c * CHUNK, CHUNK)
            vz = _gelu(vp_ref[sl, :])
            r = lax.rsqrt(_half_sums(vz * vz, low) * (1.0 / HEAD_DIM) + EPS)
            vh = (vz * r * vg_ref[...]).astype(BF16)
            o_ref[sl, :] = _gelu(up_ref[sl, :]) * _gmlp_gate(wt, vh, b_ref, low)

    return pl.pallas_call(
        body, grid=(n_pairs, S // ts),
        in_specs=[pl.BlockSpec((ts, LANES), lambda p, i: (i, p)), pl.BlockSpec((ts, LANES), lambda p, i: (i, v0 + p)),
                  pl.BlockSpec((None, 1, LANES), lambda p, i: (p, 0, 0)),
                  pl.BlockSpec((None, 2, CHUNK, CHUNK), lambda p, i: (p, 0, 0, 0)),
                  pl.BlockSpec((None, 2, CHUNK, 1), lambda p, i: (p, 0, 0, 0))],
        out_specs=pl.BlockSpec((ts, LANES), lambda p, i: (i, p)),
        out_shape=jax.ShapeDtypeStruct((S, n_pairs * LANES), F32),
        compiler_params=_params(("parallel", "parallel")), name=name)(proj, proj, vg, w, b)


def gmlp_bwd(proj, v0, n_pairs, vg, w, wT, b, do, *, name):
    S = proj.shape[0]
    ts = _tile(S, 1024, CHUNK)

    def body(up_ref, vp_ref, vg_ref, w_ref, wT_ref, b_ref, do_ref, dup_ref, dvp_ref, dw_ref, db_ref, dvg_ref):
        i = pl.program_id(1)

        @pl.when(i == 0)
        def _():
            dw_ref[...] = jnp.zeros_like(dw_ref)
            db_ref[...] = jnp.zeros_like(db_ref)
            dvg_ref[...] = jnp.zeros_like(dvg_ref)

        mask = _tril_mask(CHUNK)
        wt = [jnp.where(mask, w_ref[a], 0.0).astype(BF16) for a in range(2)]
        wtT = [jnp.where(mask.T, wT_ref[a], 0.0).astype(BF16) for a in range(2)]
        low = _low_half((CHUNK, LANES))
        vgain = vg_ref[...]
        for c in range(ts // CHUNK):
            sl = pl.ds(c * CHUNK, CHUNK)
            u_pre, v_pre, dout = up_ref[sl, :], vp_ref[sl, :], do_ref[sl, :]
            vz = _gelu(v_pre)
            r = lax.rsqrt(_half_sums(vz * vz, low) * (1.0 / HEAD_DIM) + EPS)
            vh = (vz * r * vgain).astype(BF16)
            gate = _gmlp_gate(wt, vh, b_ref, low)
            dgate = dout * _gelu(u_pre)
            dup_ref[sl, :] = dout * gate * _gelu_grad(u_pre)
            dvh = None
            for a in range(2):
                dga = _one_head(dgate, low, a)
                dgb = dga.astype(BF16)
                dw_ref[a] += jnp.where(mask, _dot(dgb, vh, 1, 1), 0.0)
                db_ref[a] += jnp.sum(dga, axis=1, keepdims=True)
                part = _dot(wtT[a], dgb, 1, 0)
                dvh = part if dvh is None else dvh + part
            dvg_ref[...] += jnp.sum(dvh * vz * r, axis=0, keepdims=True)
            t = dvh * vgain
            dvz = r * t - vz * (r * r * r) * (_half_sums(vz * t, low) * (1.0 / HEAD_DIM))
            dvp_ref[sl, :] = dvz * _gelu_grad(v_pre)

    ublk = pl.BlockSpec((ts, LANES), lambda p, i: (i, p))
    wblk = pl.BlockSpec((None, 2, CHUNK, CHUNK), lambda p, i: (p, 0, 0, 0))
    bblk = pl.BlockSpec((None, 2, CHUNK, 1), lambda p, i: (p, 0, 0, 0))
    gblk = pl.BlockSpec((None, 1, LANES), lambda p, i: (p, 0, 0))
    return pl.pallas_call(
        body, grid=(n_pairs, S // ts),
        in_specs=[ublk, pl.BlockSpec((ts, LANES), lambda p, i: (i, v0 + p)), gblk, wblk, wblk, bblk, ublk],
        out_specs=[ublk, ublk, wblk, bblk, gblk],
        out_shape=[jax.ShapeDtypeStruct((S, n_pairs * LANES), F32), jax.ShapeDtypeStruct((S, n_pairs * LANES), F32),
                   jax.ShapeDtypeStruct((n_pairs, 2, CHUNK, CHUNK), F32), jax.ShapeDtypeStruct((n_pairs, 2, CHUNK, 1), F32),
                   jax.ShapeDtypeStruct((n_pairs, 1, LANES), F32)],
        compiler_params=_params(("parallel", "arbitrary")), name=name)(proj, proj, vg, w, wT, b, do)


def loss_head(y, target, *, name):
    S, D = y.shape
    ts = _tile(S, 512, 8)

    def body(y_ref, t_ref, dy_ref, loss_ref):
        i = pl.program_id(0)
        e = y_ref[...] - t_ref[...]
        dy_ref[...] = e * (1.0 / D)
        part = jnp.sum(jnp.sum(e * e, axis=1, keepdims=True), axis=0, keepdims=True) * (0.5 / D)

        @pl.when(i == 0)
        def _():
            loss_ref[...] = part

        @pl.when(i > 0)
        def _():
            loss_ref[...] += part

    row = pl.BlockSpec((ts, D), lambda i: (i, 0))
    return pl.pallas_call(
        body, grid=(S // ts,), in_specs=[row, row],
        out_specs=[row, pl.BlockSpec((1, 1), lambda i: (0, 0))],
        out_shape=[jax.ShapeDtypeStruct((S, D), F32), jax.ShapeDtypeStruct((1, 1), F32)],
        compiler_params=_params(("arbitrary",)), name=name)(y, target)


def adamw(w, g, m, v, *, name):
    shape = w.shape
    C = shape[-1]
    R = w.size // C
    tr = _tile(R, max(8, (256 * 1024) // C // 8 * 8), 8)

    def body(w_ref, g_ref, m_ref, v_ref, d_ref, nm_ref, nv_ref):
        gv = g_ref[...]
        nm = ADAM_B1 * m_ref[...] + (1.0 - ADAM_B1) * gv
        nv = ADAM_B2 * v_ref[...] + (1.0 - ADAM_B2) * (gv * gv)
        m_hat = nm / (1.0 - ADAM_B1 ** ADAM_STEP)
        v_hat = nv / (1.0 - ADAM_B2 ** ADAM_STEP)
        d_ref[...] = -ADAM_LR * (m_hat / (jnp.sqrt(v_hat) + ADAM_EPS) + ADAM_WD * w_ref[...])
        nm_ref[...] = nm
        nv_ref[...] = nv

    blk = pl.BlockSpec((tr, C), lambda i: (i, 0))
    out = pl.pallas_call(
        body, grid=(R // tr,), in_specs=[blk] * 4, out_specs=[blk] * 3,
        out_shape=[jax.ShapeDtypeStruct((R, C), F32)] * 3,
        compiler_params=_params(("parallel",)), name=name)(*(a.reshape(R, C) for a in (w, g, m, v)))
    return tuple(o.reshape(shape) for o in out)


def pair_sum(p, landed, half, *, name):
    n, R, C = landed.shape
    tr = _tile(R, 256, 16)
    nr = R // tr

    def body(half_ref, p_ref, l_ref, o_ref):
        o_ref[...] = (p_ref[...] + l_ref[...]).astype(BF16)

    return pl.pallas_call(
        body,
        grid_spec=pltpu.PrefetchScalarGridSpec(
            num_scalar_prefetch=1, grid=(n, nr),
            in_specs=[pl.BlockSpec((None, tr, C), lambda k, r, half_ref: (k, half_ref[0] * nr + r, 0)),
                      pl.BlockSpec((None, tr, C), lambda k, r, half_ref: (k, r, 0))],
            out_specs=pl.BlockSpec((None, tr, C), lambda k, r, half_ref: (k, r, 0))),
        out_shape=jax.ShapeDtypeStruct((n, R, C), BF16),
        compiler_params=_params(("parallel", "parallel")), name=name)(half, p, landed)


def chip_sum(own, landed, chip, *, name):
    n, R, C = own.shape
    tr = _tile(R, 256, 16)

    def body(chip_ref, own_ref, *rest):
        l_refs, o_ref = rest[:n], rest[n]
        me = chip_ref[0]
        acc = None
        for d in range(n):
            term = jnp.where(me == d, own_ref[...], l_refs[d][...]).astype(F32)
            acc = term if acc is None else acc + term
        o_ref[...] = acc

    def landed_spec(d):
        return pl.BlockSpec((None, tr, C), lambda r, chip_ref: (jnp.where(chip_ref[0] == d, (d + 1) % n, d), r, 0))

    return pl.pallas_call(
        body,
        grid_spec=pltpu.PrefetchScalarGridSpec(
            num_scalar_prefetch=1, grid=(R // tr,),
            in_specs=[pl.BlockSpec((None, tr, C), lambda r, chip_ref: (chip_ref[0], r, 0))]
            + [landed_spec(d) for d in range(n)],
            out_specs=pl.BlockSpec((tr, C), lambda r, chip_ref: (r, 0))),
        out_shape=jax.ShapeDtypeStruct((R, C), F32),
        compiler_params=_params(("parallel",)), name=name)(chip, own, *([landed] * n))


def ordered_sum(parts, *, name):
    n, R, C = parts.shape
    tr = _tile(R, 256, 16)

    def body(p_ref, o_ref):
        acc = p_ref[0].astype(F32)
        for d in range(1, n):
            acc = acc + p_ref[d].astype(F32)
        o_ref[...] = acc

    return pl.pallas_call(
        body, grid=(R // tr,), in_specs=[pl.BlockSpec((n, tr, C), lambda r: (0, r, 0))],
        out_specs=pl.BlockSpec((tr, C), lambda r: (r, 0)),
        out_shape=jax.ShapeDtypeStruct((R, C), F32),
        compiler_params=_params(("parallel",)), name=name)(parts)


_ANY = pl.BlockSpec(memory_space=pl.ANY)


def _position():
    return lax.axis_index("x"), lax.axis_index("y"), lax.axis_index("c")


def _remote(src, dst, send_sem, recv_sem, device):
    return pltpu.make_async_remote_copy(src_ref=src, dst_ref=dst, send_sem=send_sem, recv_sem=recv_sem,
                                        device_id=device, device_id_type=MESH_ID)


def _small_all_gather(s_ref, all_ref, send_sems, recv_sems, x, y, c):
    me = 4 * x + 2 * y + c
    copies = []
    for f in range(1, 8):
        peer = ((1 - x) if f & 4 else x, (1 - y) if f & 2 else y, (1 - c) if f & 1 else c)
        cp = _remote(s_ref, all_ref.at[me], send_sems.at[f - 1], recv_sems.at[f - 1], peer)
        cp.start()
        copies.append((cp, peer, f - 1))

    def finish():
        for cp, peer, s in copies:
            slot = all_ref.at[4 * peer[0] + 2 * peer[1] + peer[2]]
            _remote(slot, slot, send_sems.at[s], recv_sems.at[s], peer).wait_recv()
        for cp, _, _ in copies:
            cp.wait_send()

    return finish


def _core_rows(ref, core):
    h = ref.shape[1] // 2
    return pl.ds(core * h, h)


def _gather_plan(outs, send_sems, recv_sems):
    n = len(outs)
    x, y, c = _position()
    k = 2 * x + y
    sibling = (x, y, 1 - c)
    chips = [(1 - x, y), (x, 1 - y), (1 - x, 1 - y)]

    def first():
        return [_remote(outs[w].at[k, _core_rows(outs[w], c)], outs[w].at[k, _core_rows(outs[w], c)],
                        send_sems.at[w, j], recv_sems.at[w, j], (px, py, c))
                for j, (px, py) in enumerate(chips) for w in range(n)]

    def start():
        for cp in first():
            cp.start()

    def finish():
        passed = []
        for j, (px, py) in enumerate(chips):
            for w in range(n):
                slot = outs[w].at[2 * px + py, _core_rows(outs[w], c)]
                _remote(slot, slot, send_sems.at[w, j], recv_sems.at[w, j], (px, py, c)).wait_recv()
                cp = _remote(slot, slot, send_sems.at[w, 3 + j], recv_sems.at[w, 3 + j], sibling)
                cp.start()
                passed.append(cp)
        for j, (px, py) in enumerate(chips):
            for w in range(n):
                slot = outs[w].at[2 * px + py, _core_rows(outs[w], 1 - c)]
                _remote(slot, slot, send_sems.at[w, 3 + j], recv_sems.at[w, 3 + j], sibling).wait_recv()
        for cp in first() + passed:
            cp.wait_send()

    return start, finish


def _gather_sems(n):
    return (pltpu.SemaphoreType.DMA((n, 6)), pltpu.SemaphoreType.DMA((n, 6)))


def gather_rider(slabs):
    return Rider(tuple(slabs), tuple(jax.ShapeDtypeStruct(a.shape, a.dtype) for a in slabs),
                 {i: i for i in range(len(slabs))}, _gather_sems(len(slabs)),
                 lambda ins, outs, sems: _gather_plan(outs, sems[0], sems[1]))


def gather_weights(slabs, small_slab, *, name):
    n = len(slabs)

    def body(*refs):
        outs, all_ref = refs[n + 1:2 * n + 1], refs[2 * n + 1]
        send_sems, recv_sems, s_send, s_recv = refs[2 * n + 2:]
        x, y, c = _position()
        finish_small = _small_all_gather(all_ref.at[4 * x + 2 * y + c], all_ref, s_send, s_recv, x, y, c)
        start, finish = _gather_plan(outs, send_sems, recv_sems)
        start()
        finish()
        finish_small()

    args = list(slabs) + [small_slab]
    out = pl.pallas_call(
        body, in_specs=[_ANY] * (n + 1), out_specs=[_ANY] * (n + 1),
        out_shape=[jax.ShapeDtypeStruct(a.shape, a.dtype) for a in args],
        input_output_aliases={i: i for i in range(n + 1)},
        scratch_shapes=list(_gather_sems(n)) + [pltpu.SemaphoreType.DMA((7,)), pltpu.SemaphoreType.DMA((7,))],
        name=name)(*args)
    return out[:n], out[n]


def exchange_with_sibling(parts, small_slab, *, name):
    n = len(parts)
    has_small = small_slab is not None
    n_arg = n + (1 if has_small else 0)

    def body(*refs):
        p_refs = refs[:n]
        lands = refs[n_arg:n_arg + n]
        send_sems, recv_sems = refs[2 * n_arg], refs[2 * n_arg + 1]
        x, y, c = _position()
        sibling = (x, y, 1 - c)
        if has_small:
            all_ref = refs[n_arg + n]
            finish_small = _small_all_gather(all_ref.at[4 * x + 2 * y + c], all_ref, refs[2 * n_arg + 2],
                                             refs[2 * n_arg + 3], x, y, c)
        sends = []
        for w in range(n):
            for d in range(4):
                cp = _remote(p_refs[w].at[d, _core_rows(p_refs[w], 1 - c)], lands[w].at[d],
                             send_sems.at[w, d], recv_sems.at[w, d], sibling)
                cp.start()
                sends.append(cp)
        for cp in sends:
            cp.wait_recv()
        for cp in sends:
            cp.wait_send()
        if has_small:
            finish_small()

    small_args = [small_slab] if has_small else []
    out = pl.pallas_call(
        body, in_specs=[_ANY] * n_arg, out_specs=[_ANY] * n_arg,
        out_shape=[jax.ShapeDtypeStruct((4, p.shape[1] // 2, p.shape[2]), p.dtype) for p in parts]
        + [jax.ShapeDtypeStruct(s.shape, s.dtype) for s in small_args],
        input_output_aliases={n: n} if has_small else {},
        scratch_shapes=[pltpu.SemaphoreType.DMA((n, 4)), pltpu.SemaphoreType.DMA((n, 4))]
        + ([pltpu.SemaphoreType.DMA((7,)), pltpu.SemaphoreType.DMA((7,))] if has_small else []),
        name=name)(*parts, *small_args)
    return out[:n], (out[n] if has_small else None)


def _scatter_plan(q_refs, outs, send_sems, recv_sems):
    n = len(q_refs)
    x, y, c = _position()
    k = 2 * x + y
    chips = [(1 - x, y), (x, 1 - y), (1 - x, 1 - y)]

    def sends():
        return [_remote(q_refs[w].at[2 * px + py], outs[w].at[k], send_sems.at[w, j], recv_sems.at[w, j], (px, py, c))
                for j, (px, py) in enumerate(chips) for w in range(n)]

    def start():
        for cp in sends():
            cp.start()

    def finish():
        for j, (px, py) in enumerate(chips):
            for w in range(n):
                slot = outs[w].at[2 * px + py]
                _remote(slot, slot, send_sems.at[w, j], recv_sems.at[w, j], (px, py, c)).wait_recv()
        for cp in sends():
            cp.wait_send()

    return start, finish


def _scatter_sems(n):
    return (pltpu.SemaphoreType.DMA((n, 3)), pltpu.SemaphoreType.DMA((n, 3)))


def scatter_rider(parts):
    return Rider(tuple(parts), tuple(jax.ShapeDtypeStruct(q.shape, q.dtype) for q in parts), {},
                 _scatter_sems(len(parts)), lambda ins, outs, sems: _scatter_plan(ins, outs, sems[0], sems[1]))


def scatter_to_chips(parts, *, name):
    n = len(parts)

    def body(*refs):
        start, finish = _scatter_plan(refs[:n], refs[n:2 * n], refs[2 * n], refs[2 * n + 1])
        start()
        finish()

    return pl.pallas_call(
        body, in_specs=[_ANY] * n, out_specs=[_ANY] * n,
        out_shape=[jax.ShapeDtypeStruct(q.shape, q.dtype) for q in parts],
        scratch_shapes=list(_scatter_sems(n)), name=name)(*parts)


def share_with_sibling(parts, *, name):
    n = len(parts)

    def body(*refs):
        r_refs, outs = refs[:n], refs[n:2 * n]
        send_sems, recv_sems = refs[2 * n:]
        x, y, c = _position()
        sends = []
        for w in range(n):
            cp = _remote(r_refs[w], outs[w], send_sems.at[w], recv_sems.at[w], (x, y, 1 - c))
            cp.start()
            sends.append(cp)
        for cp in sends:
            cp.wait_recv()
        for cp in sends:
            cp.wait_send()

    return pl.pallas_call(
        body, in_specs=[_ANY] * n, out_specs=[_ANY] * n,
        out_shape=[jax.ShapeDtypeStruct(r.shape, r.dtype) for r in parts],
        scratch_shapes=[pltpu.SemaphoreType.DMA((n,)), pltpu.SemaphoreType.DMA((n,))],
        name=name)(*parts)


def _cols_to_chips(full):
    *lead, R, C4 = full.shape
    t = full.reshape(*lead, R, 4, C4 // 4)
    return jnp.moveaxis(t, -2, 0)


def _chips_to_cols(sh):
    t = jnp.moveaxis(sh, 0, -2)
    return t.reshape(*t.shape[:-2], t.shape[-2] * t.shape[-1])


def _slot_in_empty(own, index, n):
    return lax.dynamic_update_slice(lax.empty((n,) + own.shape, own.dtype), own[None], (index,) + (0,) * own.ndim)


def _fold_pair(dg):
    return dg[0, :HEAD_DIM] + dg[0, HEAD_DIM:]


def _ffn_fwd(x, g, w_in_slab, w_out, tag):
    h = rms_fwd(x, g, name=f"{tag}_rms")
    a, b, act = swiglu_fwd(h, w_in_slab, name=f"{tag}_in")
    y = matmul(act, w_out, res=x, scale=0.5, tm=1024, tn=512, tk=w_out.shape[0], name=f"{tag}_out")
    return y, (x, h, a, b, act)


def _ffn_bwd(dy, saved, g, w_in_slab, w_out, tag):
    x, h, a, b, act = saved
    da, db = swiglu_bwd(dy, w_out, a, b, name=f"{tag}_dact")
    dw_out = grad_rows(act, dy, scale=0.5, name=f"{tag}_dwout")
    dw_in = grad_cols(h, da, db, name=f"{tag}_dwin")
    dx, dg = ffn_dh(da, db, w_in_slab, x, g, dy, name=f"{tag}_dh")
    return dx, dg[0], dw_in, dw_out


MEM_PAIRS = MEM_WIDTH // LANES


def _mem_attn_fwd(proj, mq0, mem_n, w_kv, g_q, g_k, tag):
    qh = pairnorm_fwd(proj, mq0, MEM_PAIRS, g_q, scale=QK_SCALE,name=f"{tag}_qnorm")
    kv = matmul(mem_n, w_kv, tm=256, tn=512, tk=1024, name=f"{tag}_kv")
    kh = pairnorm_fwd(kv, 0, MEM_PAIRS, g_k, name=f"{tag}_knorm")
    o, lse, _ = attn_fwd(qh, 0, kh, 0, kv, MEM_PAIRS, MEM_PAIRS, None, causal=False, name=f"{tag}_attn")
    return o, (qh, kv, kh, o, lse)


def _mem_attn_bwd(dmix, do0, proj, mq0, saved, mem_n, g_q, g_k, tag):
    qh, kv, kh, o, lse = saved
    delta = attn_delta(o, dmix, do0, MEM_PAIRS, name=f"{tag}_delta")
    (dqh, dkh, dv), _ = attn_bwd(qh, 0, kh, 0, kv, MEM_PAIRS, dmix, do0, MEM_PAIRS, lse, delta, None,
                                 causal=False, name=f"{tag}_dattn")
    dq_pre, dgq = pairnorm_bwd(proj, mq0, MEM_PAIRS, dqh, g_q, name=f"{tag}_dqnorm")
    dk_pre, dgk = pairnorm_bwd(kv, 0, MEM_PAIRS, dkh, g_k, name=f"{tag}_dknorm")
    dkv = jnp.concatenate([dk_pre, dv], axis=1)
    dw_kv = grad_rows(mem_n, dkv, name=f"{tag}_dwkv")
    return dq_pre, _fold_pair(dgq), _fold_pair(dgk), dw_kv, dkv


def _per_head_lanes(x, H):
    return jnp.pad(x.reshape(H, -1).T, ((0, 0), (0, LANES - H)))


def _fox_fwd(proj, b_f, g_q, g_k, tok, rider, tag):
    H, P = tok // HEAD_DIM, tok // LANES
    bias = jnp.pad(b_f.reshape(1, H), ((0, 0), (0, LANES - H)))
    qh = pairnorm_fwd(proj, 0, P, g_q, scale=QK_SCALE,name=f"{tag}_qnorm")
    kh = pairnorm_fwd(proj, P, P, g_k, name=f"{tag}_knorm")
    c = fgate_fwd(proj, 3 * P + MEM_PAIRS, bias, name=f"{tag}_fgate")
    decay = c[:, :H].T.reshape(P, 2, 1, c.shape[0])
    o, lse, rode = attn_fwd(qh, 0, kh, 0, proj, 2 * P, P, decay, causal=True, rider=rider, name=f"{tag}_attn")
    return o, (qh, kh, bias, decay, o, lse), rode


def _fox_bwd(dmix, proj, saved, g_q, g_k, tok, rider, tag):
    qh, kh, bias, decay, o, lse = saved
    H, P = tok // HEAD_DIM, tok // LANES
    delta = attn_delta(o, dmix, 0, P, name=f"{tag}_delta")
    (dqh, dkh, dv, dcs, drs), rode = attn_bwd(qh, 0, kh, 0, proj, 2 * P, dmix, 0, P, lse, delta, decay, causal=True,
                                              rider=rider, name=f"{tag}_dattn")
    dq_pre, dgq = pairnorm_bwd(proj, 0, P, dqh, g_q, name=f"{tag}_dqnorm")
    dk_pre, dgk = pairnorm_bwd(proj, P, P, dkh, g_k, name=f"{tag}_dknorm")
    dz, dbias = fgate_bwd(proj, 3 * P + MEM_PAIRS, bias, drs, _per_head_lanes(dcs, H), name=f"{tag}_dfgate")
    dqkv = jnp.concatenate([dq_pre, dk_pre, dv], axis=1)
    return dqkv, dz, dbias[0, :H], _fold_pair(dgq), _fold_pair(dgk), rode


def local_step(x, mem, target, W, comm=None):
    S, D = x.shape
    tok = D - MEM_WIDTH
    P = tok // LANES
    depth = W["norm_ffn1"].shape[0]
    mem_n = rms_fwd(mem, W["mem_norm"], name="mem_rms")
    saved = []
    for i in range(depth):
        kind, j = i % 2, i // 2
        t = f"l{i}"
        x1, s1 = _ffn_fwd(x, W["norm_ffn1"][i], W["ffn1_w_in"][i], W["ffn1_w_out"][i], f"{t}_ffn1")
        h = rms_fwd(x1, W["norm_mix"][i], name=f"{t}_mix_rms")
        w_mix = W["fox_w_in"][j] if kind == 0 else W["gmlp_w_in"][j]
        proj = matmul(h, w_mix, tm=1024, tn=896, tk=D, name=f"{t}_mix_in")
        if kind == 0:
            rider = comm.late_weights_rider() if (comm is not None and i == 0) else None
            o_tok, s_tok, rode = _fox_fwd(proj, W["fox_b_f"][j], W["fox_q_norm"][j], W["fox_k_norm"][j], tok, rider,
                                          f"{t}_fox")
            if rider is not None:
                comm.accept_late_weights(W, rode)
            mq0 = 3 * P
        else:
            vg, ws, bs = _gmlp_operands(W["gmlp_v_norm"][j], W["gmlp_w_s"][j], W["gmlp_b_s"][j])
            o_tok = gmlp_fwd(proj, P, P, vg, ws, bs, name=f"{t}_gmlp")
            s_tok = None
            mq0 = 2 * P
        o_mem, s_mem = _mem_attn_fwd(proj, mq0, mem_n, W["mem_w_kv"][i], W["mem_q_norm"][i], W["mem_k_norm"][i],
                                     f"{t}_mem")
        mix = jnp.concatenate([o_tok, o_mem], axis=1).astype(BF16)
        x2 = matmul(mix, W["w_out"][i], res=x1, tm=1024, tn=512, tk=D, name=f"{t}_mix_out")
        x3, s3 = _ffn_fwd(x2, W["norm_ffn2"][i], W["ffn2_w_in"][i], W["ffn2_w_out"][i], f"{t}_ffn2")
        saved.append((s1, x1, h, proj, mq0, s_tok, s_mem, mix, s3))
        x = x3

    dx, loss = loss_head(x, target, name="loss_head")

    G = {k: [None] * depth for k in ("norm_ffn1", "norm_mix", "norm_ffn2", "mem_q_norm", "mem_k_norm", "ffn1_w_in",
                                     "ffn1_w_out", "ffn2_w_in", "ffn2_w_out", "w_out", "mem_w_kv")}
    n_fox, n_gmlp = (depth + 1) // 2, depth // 2
    for k in ("fox_w_in", "fox_b_f", "fox_q_norm", "fox_k_norm"):
        G[k] = [None] * n_fox
    for k in ("gmlp_w_in", "gmlp_v_norm", "gmlp_w_s", "gmlp_b_s"):
        G[k] = [None] * n_gmlp
    dkv_all = [None] * depth
    for i in reversed(range(depth)):
        kind, j = i % 2, i // 2
        t = f"l{i}"
        s1, x1, h, proj, mq0, s_tok, s_mem, mix, s3 = saved[i]
        dx, G["norm_ffn2"][i], G["ffn2_w_in"][i], G["ffn2_w_out"][i] = _ffn_bwd(
            dx, s3, W["norm_ffn2"][i], W["ffn2_w_in"][i], W["ffn2_w_out"][i], f"{t}_ffn2")
        dmix = matmul(dx, W["w_out"][i], tb=True, tm=1024, tn=1024, tk=D, name=f"{t}_dmix")
        G["w_out"][i] = grad_rows(mix, dx, name=f"{t}_dwmixout")
        dmq, G["mem_q_norm"][i], G["mem_k_norm"][i], G["mem_w_kv"][i], dkv_all[i] = _mem_attn_bwd(
            dmix, P, proj, mq0, s_mem, mem_n, W["mem_q_norm"][i], W["mem_k_norm"][i], f"{t}_mem")
        if kind == 0:
            rider = comm.early_grads_rider(G) if (comm is not None and i == 0) else None
            dqkv, dz, G["fox_b_f"][j], G["fox_q_norm"][j], G["fox_k_norm"][j], rode = _fox_bwd(
                dmix, proj, s_tok, W["fox_q_norm"][j], W["fox_k_norm"][j], tok, rider, f"{t}_fox")
            if rider is not None:
                comm.accept_early_grads(rode)
            dproj = jnp.concatenate([dqkv, dmq, dz], axis=1).astype(BF16)
            w_mix, wkey = W["fox_w_in"][j], "fox_w_in"
        else:
            vg, ws, bs = _gmlp_operands(W["gmlp_v_norm"][j], W["gmlp_w_s"][j], W["gmlp_b_s"][j])
            dup, dvp, dws, dbs, dvg = gmlp_bwd(proj, P, P, vg, ws, jnp.swapaxes(ws, 2, 3), bs, dmix,
                                               name=f"{t}_dgmlp")
            G["gmlp_w_s"][j] = dws.reshape(W["gmlp_w_s"][j].shape)
            G["gmlp_b_s"][j] = dbs.reshape(W["gmlp_b_s"][j].shape)
            G["gmlp_v_norm"][j] = dvg.reshape(-1)
            dproj = jnp.concatenate([dup, dvp, dmq], axis=1).astype(BF16)
            w_mix, wkey = W["gmlp_w_in"][j], "gmlp_w_in"
        G[wkey][j] = matmul(h, dproj, ta=True, tm=1024, tn=896, tk=1024, name=f"{t}_dwmixin")
        dx, dgm = mix_dh(dproj, w_mix, x1, W["norm_mix"][i], dx, name=f"{t}_dhmix")
        G["norm_mix"][i] = dgm[0]
        dx, G["norm_ffn1"][i], G["ffn1_w_in"][i], G["ffn1_w_out"][i] = _ffn_bwd(
            dx, s1, W["norm_ffn1"][i], W["ffn1_w_in"][i], W["ffn1_w_out"][i], f"{t}_ffn1")
    w_kv_all = jnp.concatenate([W["mem_w_kv"][i] for i in range(depth)], axis=1)
    dmem_n = matmul(jnp.concatenate(dkv_all, axis=1), w_kv_all, tb=True, tm=256, tn=512, tk=1024, name="dmem_n")
    _, dmemg = rms_bwd(mem, dmem_n, W["mem_norm"], None, name="dmem_rms")
    G["mem_norm"] = [dmemg[0]]
    return loss, dx, G


def _fox_cols_to_compute(w, tok):
    H = tok // HEAD_DIM
    qkv, f, mq = w[..., :3 * tok], w[..., 3 * tok:3 * tok + H], w[..., 3 * tok + H:]
    f = jnp.pad(f, [(0, 0)] * (w.ndim - 1) + [(0, LANES - H)])
    return jnp.concatenate([qkv, mq, f], axis=-1)


def _fox_cols_from_compute(w, tok):
    H = tok // HEAD_DIM
    qkv, mq, f = w[..., :3 * tok], w[..., 3 * tok:3 * tok + MEM_WIDTH], w[..., 3 * tok + MEM_WIDTH:3 * tok + MEM_WIDTH + H]
    return jnp.concatenate([qkv, f, mq], axis=-1)


_BIG = ("ffn1_w_in", "ffn1_w_out", "ffn2_w_in", "ffn2_w_out", "w_out", "mem_w_kv", "fox_w_in", "gmlp_w_in")
_SMALL = ("norm_ffn1", "norm_mix", "norm_ffn2", "mem_norm", "mem_q_norm", "mem_k_norm", "fox_b_f", "fox_q_norm",
          "fox_k_norm", "gmlp_v_norm", "gmlp_w_s", "gmlp_b_s")
WEIGHT_ORDER = ("norm_ffn1", "ffn1_w_in", "ffn1_w_out", "norm_mix", "norm_ffn2", "ffn2_w_in", "ffn2_w_out", "w_out",
                "mem_norm", "mem_w_kv", "mem_q_norm", "mem_k_norm", "fox_w_in", "fox_b_f", "fox_q_norm", "fox_k_norm",
                "gmlp_w_in", "gmlp_v_norm", "gmlp_w_s", "gmlp_b_s")


def _small_slab(rows_list, index):
    sizes = [s.shape[0] for s in rows_list]
    n_rows = [-(-n // LANES) for n in sizes]
    small = jnp.concatenate([jnp.pad(s, (0, r * LANES - n)).reshape(r, LANES)
                             for s, n, r in zip(rows_list, sizes, n_rows)], axis=0)
    small = jnp.pad(small, ((0, -small.shape[0] % 64), (0, 0)))
    return _slot_in_empty(small, index, 8), sizes, n_rows


_FIRST_WEIGHTS = (("ffn1_w_in", 0), ("ffn1_w_out", 0), ("fox_w_in", 0))


def _weight_from_slab(name, slab, tok):
    if name in ("ffn1_w_in", "ffn2_w_in"):
        return slab
    if name == "fox_w_in":
        return _fox_cols_to_compute(_chips_to_cols(slab), tok)
    if name == "gmlp_w_in":
        return _chips_to_cols(slab)
    return slab.reshape(4 * slab.shape[1], slab.shape[2])


def _grad_to_slab(name, g, tok):
    if name == "fox_w_in":
        return _cols_to_chips(_fox_cols_from_compute(g, tok))
    if name == "gmlp_w_in":
        return _cols_to_chips(g)
    return g


class _Exchange:
    def __init__(self, shards, tok, chip, core):
        self.tok, self.core = tok, core
        self.half = core.reshape(1).astype(jnp.int32)
        self.chip_id = chip.reshape(1).astype(jnp.int32)
        items = [(k, i) for k in _BIG for i in range(shards[k].shape[0])]
        self.slabs = {it: _slot_in_empty(shards[it[0]][it[1]].astype(BF16), chip, 4) for it in items}
        self.late = [it for it in items if it not in _FIRST_WEIGHTS]
        self.reduced = {}
        self.early = None

    def first_weights(self, small_slab):
        got, small_all = gather_weights([self.slabs[it] for it in _FIRST_WEIGHTS], small_slab, name="gather_first")
        return {it: _weight_from_slab(it[0], s, self.tok) for it, s in zip(_FIRST_WEIGHTS, got)}, small_all

    def late_weights_rider(self):
        return gather_rider([self.slabs[it] for it in self.late])

    def accept_late_weights(self, W, got):
        for (k, i), s in zip(self.late, got):
            W[k][i] = _weight_from_slab(k, s, self.tok)

    def _pair_sums(self, G, items, small_slab, tag):
        parts = [_grad_to_slab(k, G[k][i], self.tok) for k, i in items]
        landed, small_all = exchange_with_sibling(parts, small_slab, name=f"grad_exchange_{tag}")
        pair = [pair_sum(p, l, self.half, name=f"grad_pair_sum_{k}{i}") for (k, i), p, l in zip(items, parts, landed)]
        return pair, small_all

    def early_grads_rider(self, G):
        items = [(k, i) for k in _BIG for i in range(len(G[k])) if G[k][i] is not None]
        pair, _ = self._pair_sums(G, items, None, "early")
        self.early = (items, pair)
        return scatter_rider(pair)

    def accept_early_grads(self, landed):
        items, pair = self.early
        self._chip_sums(items, pair, landed)

    def _chip_sums(self, items, pair, landed):
        for (k, i), q, l in zip(items, pair, landed):
            self.reduced[(k, i)] = chip_sum(q, l, self.chip_id, name=f"grad_chip_sum_{k}{i}")

    def finish_grads(self, G, small_slab):
        items = [(k, i) for k in _BIG for i in range(len(G[k])) if (k, i) not in self.reduced]
        pair, small_all = self._pair_sums(G, items, small_slab, "late")
        self._chip_sums(items, pair, scatter_to_chips(pair, name="grad_scatter_late"))
        order = sorted(self.reduced)
        other = share_with_sibling([self.reduced[it] for it in order], name="grad_share")
        full = {}
        for it, a, b in zip(order, [self.reduced[it] for it in order], other):
            full[it] = jnp.where(self.core == 0, jnp.concatenate([a, b]), jnp.concatenate([b, a]))
        names = sorted({k for k, _ in order})
        return {k: jnp.stack([full[(k, i)] for i in range(len(G[k]))]) for k in names}, small_all


def kernel(x, mem, norm_ffn1, ffn1_w_in, ffn1_w_out, norm_mix, norm_ffn2, ffn2_w_in, ffn2_w_out, w_out, mem_norm, mem_w_kv, mem_q_norm, mem_k_norm, fox_w_in, fox_b_f, fox_q_norm, fox_k_norm, gmlp_w_in, gmlp_v_norm, gmlp_w_s, gmlp_b_s, loss_target, m_norm_ffn1, m_ffn1_w_in, m_ffn1_w_out, m_norm_mix, m_norm_ffn2, m_ffn2_w_in, m_ffn2_w_out, m_w_out, m_mem_norm, m_mem_w_kv, m_mem_q_norm, m_mem_k_norm, m_fox_w_in, m_fox_b_f, m_fox_q_norm, m_fox_k_norm, m_gmlp_w_in, m_gmlp_v_norm, m_gmlp_w_s, m_gmlp_b_s, v_norm_ffn1, v_ffn1_w_in, v_ffn1_w_out, v_norm_mix, v_norm_ffn2, v_ffn2_w_in, v_ffn2_w_out, v_w_out, v_mem_norm, v_mem_w_kv, v_mem_q_norm, v_mem_k_norm, v_fox_w_in, v_fox_b_f, v_fox_q_norm, v_fox_k_norm, v_gmlp_w_in, v_gmlp_v_norm, v_gmlp_w_s, v_gmlp_b_s):
    w = dict(norm_ffn1=norm_ffn1, ffn1_w_in=ffn1_w_in, ffn1_w_out=ffn1_w_out, norm_mix=norm_mix, norm_ffn2=norm_ffn2,
             ffn2_w_in=ffn2_w_in, ffn2_w_out=ffn2_w_out, w_out=w_out, mem_norm=mem_norm, mem_w_kv=mem_w_kv,
             mem_q_norm=mem_q_norm, mem_k_norm=mem_k_norm, fox_w_in=fox_w_in, fox_b_f=fox_b_f, fox_q_norm=fox_q_norm,
             fox_k_norm=fox_k_norm, gmlp_w_in=gmlp_w_in, gmlp_v_norm=gmlp_v_norm, gmlp_w_s=gmlp_w_s, gmlp_b_s=gmlp_b_s)
    m = dict(norm_ffn1=m_norm_ffn1, ffn1_w_in=m_ffn1_w_in, ffn1_w_out=m_ffn1_w_out, norm_mix=m_norm_mix,
             norm_ffn2=m_norm_ffn2, ffn2_w_in=m_ffn2_w_in, ffn2_w_out=m_ffn2_w_out, w_out=m_w_out, mem_norm=m_mem_norm,
             mem_w_kv=m_mem_w_kv, mem_q_norm=m_mem_q_norm, mem_k_norm=m_mem_k_norm, fox_w_in=m_fox_w_in,
             fox_b_f=m_fox_b_f, fox_q_norm=m_fox_q_norm, fox_k_norm=m_fox_k_norm, gmlp_w_in=m_gmlp_w_in,
             gmlp_v_norm=m_gmlp_v_norm, gmlp_w_s=m_gmlp_w_s, gmlp_b_s=m_gmlp_b_s)
    v = dict(norm_ffn1=v_norm_ffn1, ffn1_w_in=v_ffn1_w_in, ffn1_w_out=v_ffn1_w_out, norm_mix=v_norm_mix,
             norm_ffn2=v_norm_ffn2, ffn2_w_in=v_ffn2_w_in, ffn2_w_out=v_ffn2_w_out, w_out=v_w_out, mem_norm=v_mem_norm,
             mem_w_kv=v_mem_w_kv, mem_q_norm=v_mem_q_norm, mem_k_norm=v_mem_k_norm, fox_w_in=v_fox_w_in,
             fox_b_f=v_fox_b_f, fox_q_norm=v_fox_q_norm, fox_k_norm=v_fox_k_norm, gmlp_w_in=v_gmlp_w_in,
             gmlp_v_norm=v_gmlp_v_norm, gmlp_w_s=v_gmlp_w_s, gmlp_b_s=v_gmlp_b_s)
    D = x.shape[-1]
    tok = D - MEM_WIDTH
    xi, yi, ci = _position()
    chip = 2 * xi + yi

    device = 4 * xi + 2 * yi + ci

    comm = _Exchange(w, tok, chip, ci)
    vn = w["gmlp_v_norm"]
    vn_slab, _, _ = _small_slab([vn.reshape(-1)], device)
    first, vn_all = comm.first_weights(vn_slab)
    W = {k: w[k] for k in _SMALL}
    W["gmlp_v_norm"] = _chips_to_cols(vn_all[0::2].reshape(4, -1)[:, :vn.size].reshape((4,) + vn.shape))
    for k in _BIG:
        W[k] = [first.get((k, i)) for i in range(w[k].shape[0])]

    loss, grad_x, g = local_step(x[0], mem[0], loss_target[0], W, comm)

    small_list = [jnp.stack(g[k]).reshape(-1) for k in _SMALL] + [loss.reshape(-1)]
    small, small_sizes, small_rows = _small_slab(small_list, device)
    red, small_all = comm.finish_grads(g, small)
    small_sum = ordered_sum(small_all, name="small_sum")
    off = 0
    for k, n, r in zip(_SMALL, small_sizes, small_rows):
        red[k] = small_sum[off:off + r].reshape(-1)[:n].reshape((-1,) + w[k].shape[1:] if k != "gmlp_v_norm"
                                                                else (w[k].shape[0], -1))
        off += r
    loss_total = small_sum[off, 0]
    vn_cols = w["gmlp_v_norm"].shape[-1]
    red["gmlp_v_norm"] = lax.dynamic_slice_in_dim(red["gmlp_v_norm"], chip * vn_cols, vn_cols, axis=-1)

    deltas, new_m, new_v = {}, {}, {}
    for k in WEIGHT_ORDER:
        wk = w[k] if w[k].ndim > 1 else w[k].reshape(1, -1)
        upd = adamw(wk, red[k].reshape(wk.shape), m[k].reshape(wk.shape), v[k].reshape(wk.shape), name=f"adamw_{k}")
        deltas[k], new_m[k], new_v[k] = (u.reshape(w[k].shape) for u in upd)
    return (loss_total, grad_x[None], *[red[k].reshape(w[k].shape) for k in WEIGHT_ORDER],
            *[deltas[k] for k in WEIGHT_ORDER], *[new_m[k] for k in WEIGHT_ORDER], *[new_v[k] for k in WEIGHT_ORDER])
```

```python
import functools
import math
from typing import Callable, NamedTuple

import jax
import jax.numpy as jnp
from jax import lax
from jax.experimental import pallas as pl
from jax.experimental.pallas import tpu as pltpu

F32 = jnp.float32
BF16 = jnp.bfloat16
EPS = 1e-6
HEAD_DIM = 64
MEM_WIDTH = 256
CHUNK = 128
LANES = 128
NEG = -1e30
VMEM_LIMIT_BYTES = 56 * 1024 * 1024
ATTN_Q_BLOCK = 1024
ATTN_K_BLOCK = 1024
ATTN_ROW_CHUNK = 1024
QK_SCALE = 0.125
MESH_ID = pl.DeviceIdType.MESH

ADAM_LR = 0.001
ADAM_B1 = 0.9
ADAM_B2 = 0.999
ADAM_EPS = 1e-08
ADAM_WD = 0.01
ADAM_STEP = 10


def _tile(n, pref, align):
    t = (min(pref, n) // align) * align
    while t >= align:
        if n % t == 0:
            return t
        t -= align
    return n


def _params(sem):
    return pltpu.CompilerParams(dimension_semantics=sem, vmem_limit_bytes=VMEM_LIMIT_BYTES)


def _dot(a, b, ca, cb):
    return lax.dot_general(a, b, (((ca,), (cb,)), ((), ())), preferred_element_type=F32)


def _sigmoid(x):
    return 1.0 / (1.0 + jnp.exp(-x))


_GELU_C = math.sqrt(2.0 / math.pi)


def _gelu(x):
    return 0.5 * x * (1.0 + jnp.tanh(_GELU_C * (x + 0.044715 * (x * x * x))))


def _gelu_grad(x):
    t = jnp.tanh(_GELU_C * (x + 0.044715 * (x * x * x)))
    return 0.5 * (1.0 + t) + 0.5 * x * (1.0 - t * t) * (_GELU_C * (1.0 + 3.0 * 0.044715 * (x * x)))


def matmul(a, b, *, ta=False, tb=False, out_dtype=F32, scale=None, res=None,
           tm=1024, tn=512, tk=1024, name):
    if ta:
        K, M = a.shape
    else:
        M, K = a.shape
    N = b.shape[0] if tb else b.shape[1]
    tm = _tile(M, tm, LANES if ta else 16)
    tn = _tile(N, tn, LANES)
    tk = _tile(K, tk, LANES)
    nk = K // tk
    a_spec = pl.BlockSpec((tk, tm), lambda i, j, k: (k, i)) if ta else pl.BlockSpec((tm, tk), lambda i, j, k: (i, k))
    b_spec = pl.BlockSpec((tn, tk), lambda i, j, k: (j, k)) if tb else pl.BlockSpec((tk, tn), lambda i, j, k: (k, j))
    o_spec = pl.BlockSpec((tm, tn), lambda i, j, k: (i, j))
    ca, cb = (0 if ta else 1), (1 if tb else 0)
    has_res = res is not None

    def body(*refs):
        a_ref, b_ref = refs[0], refs[1]
        res_ref = refs[2] if has_res else None
        o_ref = refs[3] if has_res else refs[2]
        acc_ref = refs[-1]
        k = pl.program_id(2)
        prod = _dot(a_ref[...].astype(BF16), b_ref[...].astype(BF16), ca, cb)

        def finish(acc):
            if scale is not None:
                acc = acc * scale
            if has_res:
                acc = res_ref[...] + acc
            o_ref[...] = acc.astype(out_dtype)

        if nk == 1:
            finish(prod)
        else:
            @pl.when(k == 0)
            def _():
                acc_ref[...] = prod

            @pl.when(k > 0)
            def _():
                acc_ref[...] += prod

            @pl.when(k == nk - 1)
            def _():
                finish(acc_ref[...])

    in_specs = [a_spec, b_spec] + ([o_spec] if has_res else [])
    args = (a, b) + ((res,) if has_res else ())
    return pl.pallas_call(
        body, grid=(M // tm, N // tn, nk), in_specs=in_specs, out_specs=o_spec,
        out_shape=jax.ShapeDtypeStruct((M, N), out_dtype),
        scratch_shapes=[pltpu.VMEM((tm, tn) if nk > 1 else (8, LANES), F32)],
        compiler_params=_params(("parallel", "parallel", "arbitrary")), name=name)(*args)


def swiglu_fwd(h, w_slab, *, name):
    S, D = h.shape
    Fc = w_slab.shape[-1]
    tm = _tile(S, 1024, 16)

    def body(h_ref, wa_ref, wb_ref, a_ref, b_ref, act_ref):
        hv = h_ref[...]
        a = _dot(hv, wa_ref[...], 1, 0)
        b = _dot(hv, wb_ref[...], 1, 0)
        a_ref[...] = a.astype(BF16)
        b_ref[...] = b.astype(BF16)
        act_ref[...] = (a * _sigmoid(a) * b).astype(BF16)

    out = pl.BlockSpec((tm, Fc), lambda j, i: (i, j))
    return pl.pallas_call(
        body, grid=(2, S // tm),
        in_specs=[pl.BlockSpec((tm, D), lambda j, i: (i, 0)),
                  pl.BlockSpec((None, D, Fc), lambda j, i: (j, 0, 0)),
                  pl.BlockSpec((None, D, Fc), lambda j, i: (j + 2, 0, 0))],
        out_specs=[out, out, out],
        out_shape=[jax.ShapeDtypeStruct((S, 2 * Fc), BF16)] * 3,
        compiler_params=_params(("parallel", "parallel")), name=name)(h, w_slab, w_slab)


def swiglu_bwd(dy, w_out, a, b, *, name):
    S, D = dy.shape
    F = w_out.shape[0]
    fc = F // 2
    tm = _tile(S, 512, 16)

    def body(dy_ref, w_ref, a_ref, b_ref, da_ref, db_ref):
        dact = 0.5 * _dot(dy_ref[...].astype(BF16), w_ref[...], 1, 1)
        av = a_ref[...].astype(F32)
        sg = _sigmoid(av)
        da_ref[...] = (dact * b_ref[...].astype(F32) * (sg * (1.0 + av * (1.0 - sg)))).astype(BF16)
        db_ref[...] = (dact * (av * sg)).astype(BF16)

    blk = pl.BlockSpec((tm, fc), lambda j, i: (i, j))
    return pl.pallas_call(
        body, grid=(2, S // tm),
        in_specs=[pl.BlockSpec((tm, D), lambda j, i: (i, 0)), pl.BlockSpec((fc, D), lambda j, i: (j, 0)), blk, blk],
        out_specs=[blk, blk],
        out_shape=[jax.ShapeDtypeStruct((S, F), BF16), jax.ShapeDtypeStruct((S, F), BF16)],
        compiler_params=_params(("parallel", "parallel")), name=name)(dy, w_out, a, b)


def ffn_dh(da, db, w_slab, x, g, dy, *, name):
    S, F = da.shape
    D, Fc = w_slab.shape[-2:]
    tm = _tile(S, 1024, 16)
    sub = _tile(tm, 256, 8)

    def body(da_ref, db_ref, w_ref, x_ref, g_ref, dy_ref, dx_ref, dg_ref, acc_ref):
        i, k = pl.program_id(0), pl.program_id(1)

        @pl.when(k == 0)
        def _():
            acc_ref[...] = jnp.zeros_like(acc_ref)

        @pl.when(k < 2)
        def _():
            acc_ref[...] += _dot(da_ref[...], w_ref[...], 1, 1)

        @pl.when(k >= 2)
        def _():
            acc_ref[...] += _dot(db_ref[...], w_ref[...], 1, 1)

        @pl.when(k == 3)
        def _():
            part = None
            for c in range(tm // sub):
                rows = pl.ds(c * sub, sub)
                xv, dh = x_ref[rows, :], acc_ref[rows, :]
                r = lax.rsqrt(jnp.mean(xv * xv, axis=-1, keepdims=True) + EPS)
                u = dh * g_ref[...]
                dx_ref[rows, :] = dy_ref[rows, :] + (r * u - xv * (r * r * r) * jnp.mean(xv * u, axis=-1, keepdims=True))
                p = jnp.sum(dh * xv * r, axis=0, keepdims=True)
                part = p if part is None else part + p

            @pl.when(i == 0)
            def _():
                dg_ref[...] = part

            @pl.when(i > 0)
            def _():
                dg_ref[...] += part

    row = pl.BlockSpec((tm, D), lambda i, k: (i, 0))
    vec = pl.BlockSpec((1, D), lambda i, k: (0, 0))
    return pl.pallas_call(
        body, grid=(S // tm, 4),
        in_specs=[pl.BlockSpec((tm, Fc), lambda i, k: (i, jnp.minimum(k, 1))),
                  pl.BlockSpec((tm, Fc), lambda i, k: (i, jnp.maximum(k - 2, 0))),
                  pl.BlockSpec((None, D, Fc), lambda i, k: (k, 0, 0)), row, vec, row],
        out_specs=[row, vec],
        out_shape=[jax.ShapeDtypeStruct((S, D), F32), jax.ShapeDtypeStruct((1, D), F32)],
        scratch_shapes=[pltpu.VMEM((tm, D), F32)],
        compiler_params=_params(("arbitrary", "arbitrary")), name=name)(da, db, w_slab, x, g.reshape(1, D), dy)


def mix_dh(dproj, w, x, g, dy, *, name):
    S, N = dproj.shape
    D = w.shape[0]
    tm = _tile(S, 1024, 16)
    sub = _tile(tm, 256, 8)
    tk = _tile(N, 896, LANES)
    nk = N // tk

    def body(p_ref, w_ref, x_ref, g_ref, dy_ref, dx_ref, dg_ref, acc_ref):
        i, k = pl.program_id(0), pl.program_id(1)
        prod = _dot(p_ref[...], w_ref[...], 1, 1)

        @pl.when(k == 0)
        def _():
            acc_ref[...] = prod

        @pl.when(k > 0)
        def _():
            acc_ref[...] += prod

        @pl.when(k == nk - 1)
        def _():
            part = None
            for c in range(tm // sub):
                rows = pl.ds(c * sub, sub)
                xv, dh = x_ref[rows, :], acc_ref[rows, :]
                r = lax.rsqrt(jnp.mean(xv * xv, axis=-1, keepdims=True) + EPS)
                u = dh * g_ref[...]
                dx_ref[rows, :] = dy_ref[rows, :] + (r * u - xv * (r * r * r) * jnp.mean(xv * u, axis=-1, keepdims=True))
                pp = jnp.sum(dh * xv * r, axis=0, keepdims=True)
                part = pp if part is None else part + pp

            @pl.when(i == 0)
            def _():
                dg_ref[...] = part

            @pl.when(i > 0)
            def _():
                dg_ref[...] += part

    row = pl.BlockSpec((tm, D), lambda i, k: (i, 0))
    vec = pl.BlockSpec((1, D), lambda i, k: (0, 0))
    return pl.pallas_call(
        body, grid=(S // tm, nk),
        in_specs=[pl.BlockSpec((tm, tk), lambda i, k: (i, k)), pl.BlockSpec((D, tk), lambda i, k: (0, k)), row, vec, row],
        out_specs=[row, vec],
        out_shape=[jax.ShapeDtypeStruct((S, D), F32), jax.ShapeDtypeStruct((1, D), F32)],
        scratch_shapes=[pltpu.VMEM((tm, D), F32)],
        compiler_params=_params(("arbitrary", "arbitrary")), name=name)(dproj, w, x, g.reshape(1, D), dy)


def grad_cols(h, da, db, *, name):
    S, D = h.shape
    Fc = da.shape[1] // 2
    tk = _tile(S, 1024, 16)
    nk = S // tk

    def body(h_ref, da_ref, db_ref, o_ref, acc_ref):
        ch, k = pl.program_id(0), pl.program_id(1)

        @pl.when(k == 0)
        def _():
            acc_ref[...] = jnp.zeros_like(acc_ref)

        @pl.when(ch < 2)
        def _():
            acc_ref[...] += _dot(h_ref[...], da_ref[...], 0, 0)

        @pl.when(ch >= 2)
        def _():
            acc_ref[...] += _dot(h_ref[...], db_ref[...], 0, 0)

        @pl.when(k == nk - 1)
        def _():
            o_ref[...] = acc_ref[...]

    return pl.pallas_call(
        body, grid=(4, nk),
        in_specs=[pl.BlockSpec((tk, D), lambda ch, k: (k, 0)),
                  pl.BlockSpec((tk, Fc), lambda ch, k: (jnp.where(ch < 2, k, 0), jnp.minimum(ch, 1))),
                  pl.BlockSpec((tk, Fc), lambda ch, k: (jnp.where(ch >= 2, k, 0), jnp.maximum(ch - 2, 0)))],
        out_specs=pl.BlockSpec((None, D, Fc), lambda ch, k: (ch, 0, 0)),
        out_shape=jax.ShapeDtypeStruct((4, D, Fc), F32),
        scratch_shapes=[pltpu.VMEM((D, Fc), F32)],
        compiler_params=_params(("parallel", "arbitrary")), name=name)(h, da, db)


def grad_rows(a, b, *, scale=None, name):
    S, M = a.shape
    N = b.shape[1]
    R = M // 4
    tn = _tile(N, 512, LANES)
    tk = _tile(S, 1024, 16)
    nk = S // tk

    def body(a_ref, b_ref, o_ref, acc_ref):
        k = pl.program_id(1)

        @pl.when(k == 0)
        def _():
            acc_ref[...] = jnp.zeros_like(acc_ref)

        acc_ref[...] += _dot(a_ref[...].astype(BF16), b_ref[...].astype(BF16), 0, 0)

        @pl.when(k == nk - 1)
        def _():
            for d in range(4):
                part = acc_ref[d * R:(d + 1) * R, :]
                o_ref[d] = part if scale is None else part * scale

    return pl.pallas_call(
        body, grid=(N // tn, nk),
        in_specs=[pl.BlockSpec((tk, M), lambda j, k: (k, 0)), pl.BlockSpec((tk, tn), lambda j, k: (k, j))],
        out_specs=pl.BlockSpec((4, R, tn), lambda j, k: (0, 0, j)),
        out_shape=jax.ShapeDtypeStruct((4, R, N), F32),
        scratch_shapes=[pltpu.VMEM((M, tn), F32)],
        compiler_params=_params(("parallel", "arbitrary")), name=name)(a, b)


def rms_fwd(x, g, *, name):
    S, D = x.shape
    ts = _tile(S, 1024, 16)

    def body(x_ref, g_ref, h_ref):
        xv = x_ref[...]
        r = lax.rsqrt(jnp.mean(xv * xv, axis=-1, keepdims=True) + EPS)
        h_ref[...] = (xv * r * g_ref[...]).astype(BF16)

    return pl.pallas_call(
        body, grid=(S // ts,),
        in_specs=[pl.BlockSpec((ts, D), lambda i: (i, 0)), pl.BlockSpec((1, D), lambda i: (0, 0))],
        out_specs=pl.BlockSpec((ts, D), lambda i: (i, 0)),
        out_shape=jax.ShapeDtypeStruct((S, D), BF16),
        compiler_params=_params(("parallel",)), name=name)(x, g.reshape(1, D))


def rms_bwd(x, dh, g, res, *, name):
    S, D = x.shape
    ts = _tile(S, 512, 16)
    has_res = res is not None

    def body(*refs):
        x_ref, dh_ref, g_ref = refs[:3]
        res_ref = refs[3] if has_res else None
        dx_ref, dg_ref = refs[-2:]
        i = pl.program_id(0)
        xv, dhv = x_ref[...], dh_ref[...].astype(F32)
        r = lax.rsqrt(jnp.mean(xv * xv, axis=-1, keepdims=True) + EPS)
        u = dhv * g_ref[...]
        dx = r * u - xv * (r * r * r) * jnp.mean(xv * u, axis=-1, keepdims=True)
        if has_res:
            dx = res_ref[...] + dx
        dx_ref[...] = dx
        part = jnp.sum(dhv * xv * r, axis=0, keepdims=True)

        @pl.when(i == 0)
        def _():
            dg_ref[...] = part

        @pl.when(i > 0)
        def _():
            dg_ref[...] += part

    row = pl.BlockSpec((ts, D), lambda i: (i, 0))
    vec = pl.BlockSpec((1, D), lambda i: (0, 0))
    args = (x, dh, g.reshape(1, D)) + ((res,) if has_res else ())
    return pl.pallas_call(
        body, grid=(S // ts,), in_specs=[row, row, vec] + ([row] if has_res else []),
        out_specs=[row, vec],
        out_shape=[jax.ShapeDtypeStruct((S, D), F32), jax.ShapeDtypeStruct((1, D), F32)],
        compiler_params=_params(("arbitrary",)), name=name)(*args)


def _low_half(shape):
    return lax.broadcasted_iota(jnp.int32, shape, len(shape) - 1) < HEAD_DIM


def _half_sums(x, low):
    sa = jnp.sum(jnp.where(low, x, 0.0), axis=1, keepdims=True)
    sb = jnp.sum(jnp.where(low, 0.0, x), axis=1, keepdims=True)
    return jnp.where(low, sa, sb)


def pairnorm_fwd(x, col0, n_pairs, g, *, scale=None, name):
    S = x.shape[0]
    ts = _tile(S, 512, 16)
    W = n_pairs * LANES
    assert col0 % n_pairs == 0

    def body(x_ref, g_ref, o_ref):
        for p in range(n_pairs):
            cols = pl.ds(p * LANES, LANES)
            xv = x_ref[:, cols]
            r = lax.rsqrt(_half_sums(xv * xv, _low_half(xv.shape)) * (1.0 / HEAD_DIM) + EPS)
            y = xv * r * g_ref[...]
            o_ref[:, cols] = (y if scale is None else y * scale).astype(BF16)

    return pl.pallas_call(
        body, grid=(S // ts,),
        in_specs=[pl.BlockSpec((ts, W), lambda i: (i, col0 // n_pairs)), pl.BlockSpec((1, LANES), lambda i: (0, 0))],
        out_specs=pl.BlockSpec((ts, W), lambda i: (i, 0)),
        out_shape=jax.ShapeDtypeStruct((S, W), BF16),
        compiler_params=_params(("parallel",)), name=name)(x, jnp.tile(g.reshape(1, HEAD_DIM), (1, 2)))


def pairnorm_bwd(x, col0, n_pairs, dy, g, *, out_dtype=F32, name):
    S = x.shape[0]
    ts = _tile(S, 512, 16)
    W = n_pairs * LANES
    assert col0 % n_pairs == 0

    def body(x_ref, dy_ref, g_ref, dx_ref, dg_ref):
        part = None
        for p in range(n_pairs):
            cols = pl.ds(p * LANES, LANES)
            xv, dyv = x_ref[:, cols], dy_ref[:, cols]
            low = _low_half(xv.shape)
            r = lax.rsqrt(_half_sums(xv * xv, low) * (1.0 / HEAD_DIM) + EPS)
            u = dyv * g_ref[...]
            dx = r * u - xv * (r * r * r) * (_half_sums(xv * u, low) * (1.0 / HEAD_DIM))
            dx_ref[:, cols] = dx.astype(out_dtype)
            pp = jnp.sum(dyv * xv * r, axis=0, keepdims=True)
            part = pp if part is None else part + pp

        @pl.when(pl.program_id(0) == 0)
        def _():
            dg_ref[...] = part

        @pl.when(pl.program_id(0) > 0)
        def _():
            dg_ref[...] += part

    vec = pl.BlockSpec((1, LANES), lambda i: (0, 0))
    blk = pl.BlockSpec((ts, W), lambda i: (i, 0))
    return pl.pallas_call(
        body, grid=(S // ts,),
        in_specs=[pl.BlockSpec((ts, W), lambda i: (i, col0 // n_pairs)), blk, vec], out_specs=[blk, vec],
        out_shape=[jax.ShapeDtypeStruct((S, W), out_dtype), jax.ShapeDtypeStruct((1, LANES), F32)],
        compiler_params=_params(("arbitrary",)), name=name)(x, dy, jnp.tile(g.reshape(1, HEAD_DIM), (1, 2)))


def _split3(x):
    x1 = x.astype(BF16)
    r1 = x - x1.astype(F32)
    x2 = r1.astype(BF16)
    x3 = (r1 - x2.astype(F32)).astype(BF16)
    return x1, x2, x3


def _tri_ones(n, lower):
    r = lax.broadcasted_iota(jnp.int32, (n, n), 0)
    c = lax.broadcasted_iota(jnp.int32, (n, n), 1)
    return jnp.where((c <= r) if lower else (c >= r), 1.0, 0.0).astype(BF16)


def fgate_fwd(z, col0, bias, *, name):
    S, L = z.shape[0], LANES
    tb = _tile(S, 256, 16)

    def body(z_ref, b_ref, c_ref, carry):
        i = pl.program_id(0)

        @pl.when(i == 0)
        def _():
            carry[...] = jnp.zeros_like(carry)

        zz = z_ref[...] + b_ref[...]
        lf = jnp.minimum(zz, 0.0) - jnp.log(1.0 + jnp.exp(-jnp.abs(zz)))
        tri = _tri_ones(tb, True)
        x1, x2, x3 = _split3(lf)
        c = (_dot(tri, x1, 1, 0) + _dot(tri, x2, 1, 0)) + _dot(tri, x3, 1, 0) + carry[...]
        c_ref[...] = c
        carry[...] += jnp.sum(lf, axis=0, keepdims=True)

    return pl.pallas_call(
        body, grid=(S // tb,),
        in_specs=[pl.BlockSpec((tb, L), lambda i: (i, col0)), pl.BlockSpec((1, L), lambda i: (0, 0))],
        out_specs=pl.BlockSpec((tb, L), lambda i: (i, 0)),
        out_shape=jax.ShapeDtypeStruct((S, L), F32),
        scratch_shapes=[pltpu.VMEM((1, L), F32)],
        compiler_params=_params(("arbitrary",)), name=name)(z, bias)


def fgate_bwd(z, col0, bias, drs, dcs, *, name):
    S, L = z.shape[0], LANES
    n_pairs = drs.shape[0]
    tb = _tile(S, 256, 16)
    nb = S // tb

    def body(z_ref, b_ref, drs_ref, dcs_ref, dz_ref, db_ref, carry):
        i = pl.program_id(0)

        @pl.when(i == 0)
        def _():
            carry[...] = jnp.zeros_like(carry)

        tri = _tri_ones(tb, False)
        lane = lax.broadcasted_iota(jnp.int32, (tb, L), 1)
        dc = -dcs_ref[...]
        for h in range(2 * n_pairs):
            dc = dc + jnp.where(lane == h, jnp.sum(drs_ref[h // 2, h % 2], axis=1, keepdims=True), 0.0)
        x1, x2, x3 = _split3(dc)
        dlf = (_dot(tri, x1, 1, 0) + _dot(tri, x2, 1, 0)) + _dot(tri, x3, 1, 0) + carry[...]
        carry[...] += jnp.sum(dc, axis=0, keepdims=True)
        dz = dlf * _sigmoid(-(z_ref[...] + b_ref[...]))
        dz_ref[...] = dz
        part = jnp.sum(dz, axis=0, keepdims=True)

        @pl.when(i == 0)
        def _():
            db_ref[...] = part

        @pl.when(i > 0)
        def _():
            db_ref[...] += part

    rev = pl.BlockSpec((tb, L), lambda i: (nb - 1 - i, 0))
    vec = pl.BlockSpec((1, L), lambda i: (0, 0))
    return pl.pallas_call(
        body, grid=(nb,),
        in_specs=[pl.BlockSpec((tb, L), lambda i: (nb - 1 - i, col0)), vec,
                  pl.BlockSpec((n_pairs, 2, tb, L), lambda i: (0, 0, nb - 1 - i, 0)), rev],
        out_specs=[rev, vec],
        out_shape=[jax.ShapeDtypeStruct((S, L), F32), jax.ShapeDtypeStruct((1, L), F32)],
        scratch_shapes=[pltpu.VMEM((1, L), F32)],
        compiler_params=_params(("arbitrary",)), name=name)(z, bias, drs, dcs)


def _one_head(x, low, a):
    return jnp.where(low if a == 0 else jnp.logical_not(low), x, jnp.zeros_like(x))


class Rider(NamedTuple):
    inputs: tuple
    out_shapes: tuple
    aliases: dict
    sems: tuple
    plan: Callable


def _with_rider(rider, n_in, n_out, n_scratch):
    if rider is None:
        return [], [], [], [], {}, lambda refs: (refs[:n_in], refs[n_in:n_in + n_out], refs[n_in + n_out:], None)
    e_in, e_out = len(rider.inputs), len(rider.out_shapes)

    def split(refs):
        ins, r_in = refs[:n_in], refs[n_in:n_in + e_in]
        o0 = n_in + e_in
        outs, r_out = refs[o0:o0 + n_out], refs[o0 + n_out:o0 + n_out + e_out]
        s0 = o0 + n_out + e_out
        return ins, outs, refs[s0:s0 + n_scratch], rider.plan(r_in, r_out, refs[s0 + n_scratch:])

    aliases = {n_in + a: n_out + b for a, b in rider.aliases.items()}
    return list(rider.inputs), [_ANY] * e_in, list(rider.out_shapes), [_ANY] * e_out, aliases, split


def attn_fwd(q, q0, k, k0, v, v0, n_pairs, decay, *, causal, rider=None, name):
    Sq, Sk = q.shape[0], k.shape[0]
    tq = _tile(Sq, ATTN_Q_BLOCK if causal else 4 * ATTN_Q_BLOCK, LANES)
    tk = _tile(Sk, ATTN_K_BLOCK, LANES)
    nq, nk = Sq // tq, Sk // tk
    bias = decay is not None
    rs = _tile(tq, ATTN_ROW_CHUNK, 16)
    r_args, r_in_specs, r_shapes, r_out_specs, aliases, split = _with_rider(rider, 4 if bias else 3, 2, 4)

    def row_sum_lanes(acc, low, a):
        other = jnp.logical_not(low) if a == 0 else low
        return jnp.max(jnp.where(other, acc, 0.0), axis=1, keepdims=True)

    live = [(i, j) for i in range(nq) for j in range(nk) if not causal or j * tk <= i * tq + tq - 1]
    n_live = len(live)

    def body(i_tab, j_tab, *refs):
        ins, (o_ref, lse_ref), scratch, ride = split(refs)
        m_sc, acc_sc = scratch[:2], scratch[2:]
        q_ref, k_ref, v_ref = ins[:3]
        ck_ref = ins[3] if bias else None
        pr, t = pl.program_id(0), pl.program_id(1)
        i, j = i_tab[t], j_tab[t]
        last_j = (i * tq + tq - 1) // tk if causal else nk - 1
        if ride is not None:
            pl.when(jnp.logical_and(pr == 0, t == 0))(ride[0])

        @pl.when(j == 0)
        def _():
            for a in range(2):
                m_sc[a][...] = jnp.full_like(m_sc[a], NEG)
                acc_sc[a][...] = jnp.zeros_like(acc_sc[a])

        def compute(masked):
            kv, vv = k_ref[...], v_ref[...].astype(BF16)
            low_k = _low_half(kv.shape)
            va = [jnp.where(low_k if a == 0 else jnp.logical_not(low_k), vv, jnp.ones_like(vv)) for a in range(2)]
            for r in range(tq // rs):
                rows = pl.ds(r * rs, rs)
                qv = q_ref[rows, :]
                low = _low_half(qv.shape)
                for a in range(2):
                    s = _dot(_one_head(qv, low, a), kv, 1, 1)
                    if bias:
                        s = s - ck_ref[a]
                    if masked:
                        row = i * tq + r * rs + lax.broadcasted_iota(jnp.int32, (rs, tk), 0)
                        col = j * tk + lax.broadcasted_iota(jnp.int32, (rs, tk), 1)
                        s = jnp.where(col <= row, s, NEG)
                    m_prev = m_sc[a][rows, :]
                    m_new = jnp.maximum(m_prev, jnp.max(s, axis=1, keepdims=True))
                    alpha = jnp.exp(m_prev - m_new)
                    p = jnp.exp(s - m_new).astype(BF16)
                    acc_sc[a][rows, :] = alpha * acc_sc[a][rows, :] + _dot(p, va[a], 1, 0)
                    m_sc[a][rows, :] = m_new

        if causal:
            crosses = j * tk + (tk - 1) > i * tq
            pl.when(crosses)(functools.partial(compute, True))
            pl.when(jnp.logical_not(crosses))(functools.partial(compute, False))
        else:
            compute(False)

        @pl.when(j == last_j)
        def _():
            low = _low_half((tq, LANES))
            l = [row_sum_lanes(acc_sc[a][...], low, a) for a in range(2)]
            o_ref[...] = jnp.where(low, acc_sc[0][...] / l[0], acc_sc[1][...] / l[1])
            for a in range(2):
                lse_ref[a] = m_sc[a][...] + jnp.log(l[a])

        if ride is not None:
            pl.when(jnp.logical_and(pr == n_pairs - 1, t == n_live - 1))(ride[1])

    in_specs = [pl.BlockSpec((tq, LANES), lambda p, t, it, jt: (it[t], q0 + p)),
                pl.BlockSpec((tk, LANES), lambda p, t, it, jt: (jt[t], k0 + p)),
                pl.BlockSpec((tk, LANES), lambda p, t, it, jt: (jt[t], v0 + p))]
    args = [q, k, v]
    if bias:
        in_specs.append(pl.BlockSpec((None, 2, 1, tk), lambda p, t, it, jt: (p, 0, 0, jt[t])))
        args.append(decay)
    tabs = [jnp.asarray([b[n] for b in live], jnp.int32) for n in range(2)]
    out = pl.pallas_call(
        body,
        grid_spec=pltpu.PrefetchScalarGridSpec(
            num_scalar_prefetch=2, grid=(n_pairs, n_live), in_specs=in_specs + r_in_specs,
            out_specs=[pl.BlockSpec((tq, LANES), lambda p, t, it, jt: (it[t], p)),
                       pl.BlockSpec((None, 2, tq, 1), lambda p, t, it, jt: (p, 0, it[t], 0))] + r_out_specs,
            scratch_shapes=[pltpu.VMEM((tq, 1), F32)] * 2 + [pltpu.VMEM((tq, LANES), F32)] * 2
            + (list(rider.sems) if rider else [])),
        out_shape=[jax.ShapeDtypeStruct((Sq, n_pairs * LANES), F32),
                   jax.ShapeDtypeStruct((n_pairs, 2, Sq, 1), F32)] + r_shapes,
        input_output_aliases={2 + a: b for a, b in aliases.items()},
        compiler_params=_params(("arbitrary", "arbitrary") if rider else ("parallel", "arbitrary")),
        name=name)(*tabs, *args, *r_args)
    return out[0], out[1], out[2:]


def attn_delta(o, do, do0, n_pairs, *, name):
    S = o.shape[0]
    ts = _tile(S, 512, 16)
    W = n_pairs * LANES
    assert do0 % n_pairs == 0

    def body(o_ref, do_ref, out_ref):
        for p in range(n_pairs):
            cols = pl.ds(p * LANES, LANES)
            prod = o_ref[:, cols] * do_ref[:, cols]
            low = _low_half(prod.shape)
            out_ref[p, 0] = jnp.sum(jnp.where(low, prod, 0.0), axis=1, keepdims=True)
            out_ref[p, 1] = jnp.sum(jnp.where(low, 0.0, prod), axis=1, keepdims=True)

    return pl.pallas_call(
        body, grid=(S // ts,),
        in_specs=[pl.BlockSpec((ts, W), lambda i: (i, 0)), pl.BlockSpec((ts, W), lambda i: (i, do0 // n_pairs))],
        out_specs=pl.BlockSpec((n_pairs, 2, ts, 1), lambda i: (0, 0, i, 0)),
        out_shape=jax.ShapeDtypeStruct((n_pairs, 2, S, 1), F32),
        compiler_params=_params(("parallel",)), name=name)(o, do)


def attn_bwd(q, q0, k, k0, v, v0, do, do0, n_pairs, lse, delta, decay, *, causal, rider=None, name):
    Sq, Sk = q.shape[0], k.shape[0]
    tq = _tile(Sq, ATTN_Q_BLOCK if causal else 4 * ATTN_Q_BLOCK, LANES)
    tk = _tile(Sk, ATTN_K_BLOCK, LANES)
    nq, nk = Sq // tq, Sk // tk
    bias = decay is not None

    rs = _tile(tq, ATTN_ROW_CHUNK, 16)
    r_args, r_in_specs, r_shapes, r_out_specs, aliases, split = _with_rider(
        rider, 7 if bias else 6, 5 if bias else 3, 0)

    live = [(i, j) for j in range(nk) for i in range(nq) if not causal or j * tk <= i * tq + tq - 1]
    n_live = len(live)

    def body(i_tab, j_tab, *refs):
        ins, outs, _, ride = split(refs)
        q_ref, k_ref, v_ref, do_ref, lse_ref, dl_ref = ins[:6]
        ck_ref = ins[6] if bias else None
        dq_ref, dk_ref, dv_ref = outs[:3]
        dcs_ref, drs_ref = (outs[3], outs[4]) if bias else (None, None)
        pr, t = pl.program_id(0), pl.program_id(1)
        i, j = i_tab[t], j_tab[t]
        first_i = (j * tk) // tq if causal else 0
        if ride is not None:
            pl.when(jnp.logical_and(pr == 0, t == 0))(ride[0])

        @pl.when(i == first_i)
        def _():
            dk_ref[...] = jnp.zeros_like(dk_ref)
            dv_ref[...] = jnp.zeros_like(dv_ref)
            if bias:
                dcs_ref[...] = jnp.zeros_like(dcs_ref)

        def compute(masked):
            kv, vv = k_ref[...], v_ref[...].astype(BF16)
            low_k = _low_half(kv.shape)
            ka = [_one_head(kv, low_k, a) for a in range(2)]
            for r in range(tq // rs):
                here = pl.ds(r * rs, rs)
                rows = pl.ds(pl.multiple_of(i * tq + r * rs, rs), rs)
                qv, dov = q_ref[here, :], do_ref[here, :].astype(BF16)
                low = _low_half(qv.shape)
                dq_part, dk_part, dv_part, row_parts, col_parts = None, None, None, [], []
                for a in range(2):
                    qa, doa = _one_head(qv, low, a), _one_head(dov, low, a)
                    s = _dot(qa, kv, 1, 1)
                    if bias:
                        s = s - ck_ref[a]
                    p = jnp.exp(s - lse_ref[a, here])
                    if masked:
                        row = i * tq + r * rs + lax.broadcasted_iota(jnp.int32, (rs, tk), 0)
                        col = j * tk + lax.broadcasted_iota(jnp.int32, (rs, tk), 1)
                        p = jnp.where(col <= row, p, 0.0)
                    dv_a = _dot(p.astype(BF16), doa, 0, 0)
                    dp = _dot(doa, vv, 1, 1)
                    ds = p * (dp - dl_ref[a, here])
                    dsb = ds.astype(BF16)
                    dk_a = _dot(dsb, qa, 0, 0)
                    if bias:
                        col_parts.append(jnp.sum(ds, axis=0, keepdims=True))
                        lanes = ds[:, :LANES]
                        for c in range(1, tk // LANES):
                            lanes = lanes + ds[:, c * LANES:(c + 1) * LANES]
                        row_parts.append(lanes)
                    part = _dot(dsb, ka[a], 1, 0) * QK_SCALE
                    dq_part = part if dq_part is None else dq_part + part
                    dk_part = dk_a if dk_part is None else dk_part + dk_a
                    dv_part = dv_a if dv_part is None else dv_part + dv_a
                dv_ref[...] += dv_part
                dk_ref[...] += dk_part
                for a, cp in enumerate(col_parts):
                    dcs_ref[a] += cp

                @pl.when(j == 0)
                def _(rows=rows, dq_part=dq_part, row_parts=row_parts):
                    dq_ref[rows, :] = dq_part
                    for a, rp in enumerate(row_parts):
                        drs_ref[a, rows, :] = rp

                @pl.when(j > 0)
                def _(rows=rows, dq_part=dq_part, row_parts=row_parts):
                    dq_ref[rows, :] += dq_part
                    for a, rp in enumerate(row_parts):
                        drs_ref[a, rows, :] += rp

        if causal:
            crosses = j * tk + (tk - 1) > i * tq
            pl.when(crosses)(functools.partial(compute, True))
            pl.when(jnp.logical_not(crosses))(functools.partial(compute, False))
        else:
            compute(False)

        if ride is not None:
            pl.when(jnp.logical_and(pr == n_pairs - 1, t == n_live - 1))(ride[1])

    col1 = pl.BlockSpec((None, 2, tq, 1), lambda p, t, it, jt: (p, 0, it[t], 0))
    in_specs = [pl.BlockSpec((tq, LANES), lambda p, t, it, jt: (it[t], q0 + p)),
                pl.BlockSpec((tk, LANES), lambda p, t, it, jt: (jt[t], k0 + p)),
                pl.BlockSpec((tk, LANES), lambda p, t, it, jt: (jt[t], v0 + p)),
                pl.BlockSpec((tq, LANES), lambda p, t, it, jt: (it[t], do0 + p)), col1, col1]
    args = [q, k, v, do, lse, delta]
    kout = pl.BlockSpec((tk, LANES), lambda p, t, it, jt: (jt[t], p))
    out_specs = [pl.BlockSpec((Sq, LANES), lambda p, t, it, jt: (0, p)), kout, kout]
    out_shape = [jax.ShapeDtypeStruct((Sq, n_pairs * LANES), F32), jax.ShapeDtypeStruct((Sk, n_pairs * LANES), F32),
                 jax.ShapeDtypeStruct((Sk, n_pairs * LANES), F32)]
    if bias:
        in_specs.append(pl.BlockSpec((None, 2, 1, tk), lambda p, t, it, jt: (p, 0, 0, jt[t])))
        args.append(decay)
        out_specs += [pl.BlockSpec((None, 2, 1, tk), lambda p, t, it, jt: (p, 0, 0, jt[t])),
                      pl.BlockSpec((None, 2, Sq, LANES), lambda p, t, it, jt: (p, 0, 0, 0))]
        out_shape += [jax.ShapeDtypeStruct((n_pairs, 2, 1, Sk), F32),
                      jax.ShapeDtypeStruct((n_pairs, 2, Sq, LANES), F32)]
    n_own = len(out_shape)
    tabs = [jnp.asarray([b[n] for b in live], jnp.int32) for n in range(2)]
    out = pl.pallas_call(
        body,
        grid_spec=pltpu.PrefetchScalarGridSpec(
            num_scalar_prefetch=2, grid=(n_pairs, n_live), in_specs=in_specs + r_in_specs,
            out_specs=out_specs + r_out_specs, scratch_shapes=list(rider.sems) if rider else []),
        out_shape=out_shape + r_shapes,
        input_output_aliases={2 + a: b for a, b in aliases.items()},
        compiler_params=_params(("arbitrary", "arbitrary") if rider else ("parallel", "arbitrary")),
        name=name)(*tabs, *args, *r_args)
    return tuple(out[:n_own]), out[n_own:]


def _tril_mask(n):
    r = lax.broadcasted_iota(jnp.int32, (n, n), 0)
    c = lax.broadcasted_iota(jnp.int32, (n, n), 1)
    return c <= r


def _gmlp_operands(v_gain, w_s, b_s):
    G = w_s.shape[0]
    return (v_gain.reshape(G // 2, 1, LANES), w_s.reshape(G // 2, 2, CHUNK, CHUNK), b_s.reshape(G // 2, 2, CHUNK, 1))


def _gmlp_gate(wt, vh, b_ref, low):
    gate = _dot(wt[0], _one_head(vh, low, 0), 1, 0) + _dot(wt[1], _one_head(vh, low, 1), 1, 0)
    return gate + jnp.where(low, b_ref[0], b_ref[1])


def gmlp_fwd(proj, v0, n_pairs, vg, w, b, *, name):
    S = proj.shape[0]
    ts = _tile(S, 1024, CHUNK)

    def body(up_ref, vp_ref, vg_ref, w_ref, b_ref, o_ref):
        mask = _tril_mask(CHUNK)
        wt = [jnp.where(mask, w_ref[a], 0.0).astype(BF16) for a in range(2)]
        low = _low_half((CHUNK, LANES))
        for c in range(ts // CHUNK):
            sl = pl.ds(c * CHUNK, CHUNK)
            vz = _gelu(vp_ref[sl, :])
            r = lax.rsqrt(_half_sums(vz * vz, low) * (1.0 / HEAD_DIM) + EPS)
            vh = (vz * r * vg_ref[...]).astype(BF16)
            o_ref[sl, :] = _gelu(up_ref[sl, :]) * _gmlp_gate(wt, vh, b_ref, low)

    return pl.pallas_call(
        body, grid=(n_pairs, S // ts),
        in_specs=[pl.BlockSpec((ts, LANES), lambda p, i: (i, p)), pl.BlockSpec((ts, LANES), lambda p, i: (i, v0 + p)),
                  pl.BlockSpec((None, 1, LANES), lambda p, i: (p, 0, 0)),
                  pl.BlockSpec((None, 2, CHUNK, CHUNK), lambda p, i: (p, 0, 0, 0)),
                  pl.BlockSpec((None, 2, CHUNK, 1), lambda p, i: (p, 0, 0, 0))],
        out_specs=pl.BlockSpec((ts, LANES), lambda p, i: (i, p)),
        out_shape=jax.ShapeDtypeStruct((S, n_pairs * LANES), F32),
        compiler_params=_params(("parallel", "parallel")), name=name)(proj, proj, vg, w, b)


def gmlp_bwd(proj, v0, n_pairs, vg, w, wT, b, do, *, name):
    S = proj.shape[0]
    ts = _tile(S, 1024, CHUNK)

    def body(up_ref, vp_ref, vg_ref, w_ref, wT_ref, b_ref, do_ref, dup_ref, dvp_ref, dw_ref, db_ref, dvg_ref):
        i = pl.program_id(1)

        @pl.when(i == 0)
        def _():
            dw_ref[...] = jnp.zeros_like(dw_ref)
            db_ref[...] = jnp.zeros_like(db_ref)
            dvg_ref[...] = jnp.zeros_like(dvg_ref)

        mask = _tril_mask(CHUNK)
        wt = [jnp.where(mask, w_ref[a], 0.0).astype(BF16) for a in range(2)]
        wtT = [jnp.where(mask.T, wT_ref[a], 0.0).astype(BF16) for a in range(2)]
        low = _low_half((CHUNK, LANES))
        vgain = vg_ref[...]
        for c in range(ts // CHUNK):
            sl = pl.ds(c * CHUNK, CHUNK)
            u_pre, v_pre, dout = up_ref[sl, :], vp_ref[sl, :], do_ref[sl, :]
            vz = _gelu(v_pre)
            r = lax.rsqrt(_half_sums(vz * vz, low) * (1.0 / HEAD_DIM) + EPS)
            vh = (vz * r * vgain).astype(BF16)
            gate = _gmlp_gate(wt, vh, b_ref, low)
            dgate = dout * _gelu(u_pre)
            dup_ref[sl, :] = dout * gate * _gelu_grad(u_pre)
            dvh = None
            for a in range(2):
                dga = _one_head(dgate, low, a)
                dgb = dga.astype(BF16)
                dw_ref[a] += jnp.where(mask, _dot(dgb, vh, 1, 1), 0.0)
                db_ref[a] += jnp.sum(dga, axis=1, keepdims=True)
                part = _dot(wtT[a], dgb, 1, 0)
                dvh = part if dvh is None else dvh + part
            dvg_ref[...] += jnp.sum(dvh * vz * r, axis=0, keepdims=True)
            t = dvh * vgain
            dvz = r * t - vz * (r * r * r) * (_half_sums(vz * t, low) * (1.0 / HEAD_DIM))
            dvp_ref[sl, :] = dvz * _gelu_grad(v_pre)

    ublk = pl.BlockSpec((ts, LANES), lambda p, i: (i, p))
    wblk = pl.BlockSpec((None, 2, CHUNK, CHUNK), lambda p, i: (p, 0, 0, 0))
    bblk = pl.BlockSpec((None, 2, CHUNK, 1), lambda p, i: (p, 0, 0, 0))
    gblk = pl.BlockSpec((None, 1, LANES), lambda p, i: (p, 0, 0))
    return pl.pallas_call(
        body, grid=(n_pairs, S // ts),
        in_specs=[ublk, pl.BlockSpec((ts, LANES), lambda p, i: (i, v0 + p)), gblk, wblk, wblk, bblk, ublk],
        out_specs=[ublk, ublk, wblk, bblk, gblk],
        out_shape=[jax.ShapeDtypeStruct((S, n_pairs * LANES), F32), jax.ShapeDtypeStruct((S, n_pairs * LANES), F32),
                   jax.ShapeDtypeStruct((n_pairs, 2, CHUNK, CHUNK), F32), jax.ShapeDtypeStruct((n_pairs, 2, CHUNK, 1), F32),
                   jax.ShapeDtypeStruct((n_pairs, 1, LANES), F32)],
        compiler_params=_params(("parallel", "arbitrary")), name=name)(proj, proj, vg, w, wT, b, do)


def loss_head(y, target, *, name):
    S, D = y.shape
    ts = _tile(S, 512, 8)

    def body(y_ref, t_ref, dy_ref, loss_ref):
        i = pl.program_id(0)
        e = y_ref[...] - t_ref[...]
        dy_ref[...] = e * (1.0 / D)
        part = jnp.sum(jnp.sum(e * e, axis=1, keepdims=True), axis=0, keepdims=True) * (0.5 / D)

        @pl.when(i == 0)
        def _():
            loss_ref[...] = part

        @pl.when(i > 0)
        def _():
            loss_ref[...] += part

    row = pl.BlockSpec((ts, D), lambda i: (i, 0))
    return pl.pallas_call(
        body, grid=(S // ts,), in_specs=[row, row],
        out_specs=[row, pl.BlockSpec((1, 1), lambda i: (0, 0))],
        out_shape=[jax.ShapeDtypeStruct((S, D), F32), jax.ShapeDtypeStruct((1, 1), F32)],
        compiler_params=_params(("arbitrary",)), name=name)(y, target)


def adamw(w, g, m, v, *, name):
    shape = w.shape
    C = shape[-1]
    R = w.size // C
    tr = _tile(R, max(8, (256 * 1024) // C // 8 * 8), 8)

    def body(w_ref, g_ref, m_ref, v_ref, d_ref, nm_ref, nv_ref):
        gv = g_ref[...]
        nm = ADAM_B1 * m_ref[...] + (1.0 - ADAM_B1) * gv
        nv = ADAM_B2 * v_ref[...] + (1.0 - ADAM_B2) * (gv * gv)
        m_hat = nm / (1.0 - ADAM_B1 ** ADAM_STEP)
        v_hat = nv / (1.0 - ADAM_B2 ** ADAM_STEP)
        d_ref[...] = -ADAM_LR * (m_hat / (jnp.sqrt(v_hat) + ADAM_EPS) + ADAM_WD * w_ref[...])
        nm_ref[...] = nm
        nv_ref[...] = nv

    blk = pl.BlockSpec((tr, C), lambda i: (i, 0))
    out = pl.pallas_call(
        body, grid=(R // tr,), in_specs=[blk] * 4, out_specs=[blk] * 3,
        out_shape=[jax.ShapeDtypeStruct((R, C), F32)] * 3,
        compiler_params=_params(("parallel",)), name=name)(*(a.reshape(R, C) for a in (w, g, m, v)))
    return tuple(o.reshape(shape) for o in out)


def pair_sum(p, landed, half, *, name):
    n, R, C = landed.shape
    tr = _tile(R, 256, 16)
    nr = R // tr

    def body(half_ref, p_ref, l_ref, o_ref):
        o_ref[...] = (p_ref[...] + l_ref[...]).astype(BF16)

    return pl.pallas_call(
        body,
        grid_spec=pltpu.PrefetchScalarGridSpec(
            num_scalar_prefetch=1, grid=(n, nr),
            in_specs=[pl.BlockSpec((None, tr, C), lambda k, r, half_ref: (k, half_ref[0] * nr + r, 0)),
                      pl.BlockSpec((None, tr, C), lambda k, r, half_ref: (k, r, 0))],
            out_specs=pl.BlockSpec((None, tr, C), lambda k, r, half_ref: (k, r, 0))),
        out_shape=jax.ShapeDtypeStruct((n, R, C), BF16),
        compiler_params=_params(("parallel", "parallel")), name=name)(half, p, landed)


def chip_sum(own, landed, chip, *, name):
    n, R, C = own.shape
    tr = _tile(R, 256, 16)

    def body(chip_ref, own_ref, *rest):
        l_refs, o_ref = rest[:n], rest[n]
        me = chip_ref[0]
        acc = None
        for d in range(n):
            term = jnp.where(me == d, own_ref[...], l_refs[d][...]).astype(F32)
            acc = term if acc is None else acc + term
        o_ref[...] = acc

    def landed_spec(d):
        return pl.BlockSpec((None, tr, C), lambda r, chip_ref: (jnp.where(chip_ref[0] == d, (d + 1) % n, d), r, 0))

    return pl.pallas_call(
        body,
        grid_spec=pltpu.PrefetchScalarGridSpec(
            num_scalar_prefetch=1, grid=(R // tr,),
            in_specs=[pl.BlockSpec((None, tr, C), lambda r, chip_ref: (chip_ref[0], r, 0))]
            + [landed_spec(d) for d in range(n)],
            out_specs=pl.BlockSpec((tr, C), lambda r, chip_ref: (r, 0))),
        out_shape=jax.ShapeDtypeStruct((R, C), F32),
        compiler_params=_params(("parallel",)), name=name)(chip, own, *([landed] * n))


def ordered_sum(parts, *, name):
    n, R, C = parts.shape
    tr = _tile(R, 256, 16)

    def body(p_ref, o_ref):
        acc = p_ref[0].astype(F32)
        for d in range(1, n):
            acc = acc + p_ref[d].astype(F32)
        o_ref[...] = acc

    return pl.pallas_call(
        body, grid=(R // tr,), in_specs=[pl.BlockSpec((n, tr, C), lambda r: (0, r, 0))],
        out_specs=pl.BlockSpec((tr, C), lambda r: (r, 0)),
        out_shape=jax.ShapeDtypeStruct((R, C), F32),
        compiler_params=_params(("parallel",)), name=name)(parts)


_ANY = pl.BlockSpec(memory_space=pl.ANY)


def _position():
    return lax.axis_index("x"), lax.axis_index("y"), lax.axis_index("c")


def _remote(src, dst, send_sem, recv_sem, device):
    return pltpu.make_async_remote_copy(src_ref=src, dst_ref=dst, send_sem=send_sem, recv_sem=recv_sem,
                                        device_id=device, device_id_type=MESH_ID)


def _small_all_gather(s_ref, all_ref, send_sems, recv_sems, x, y, c):
    me = 4 * x + 2 * y + c
    copies = []
    for f in range(1, 8):
        peer = ((1 - x) if f & 4 else x, (1 - y) if f & 2 else y, (1 - c) if f & 1 else c)
        cp = _remote(s_ref, all_ref.at[me], send_sems.at[f - 1], recv_sems.at[f - 1], peer)
        cp.start()
        copies.append((cp, peer, f - 1))

    def finish():
        for cp, peer, s in copies:
            slot = all_ref.at[4 * peer[0] + 2 * peer[1] + peer[2]]
            _remote(slot, slot, send_sems.at[s], recv_sems.at[s], peer).wait_recv()
        for cp, _, _ in copies:
            cp.wait_send()

    return finish


def _core_rows(ref, core):
    h = ref.shape[1] // 2
    return pl.ds(core * h, h)


def _gather_plan(outs, send_sems, recv_sems):
    n = len(outs)
    x, y, c = _position()
    k = 2 * x + y
    sibling = (x, y, 1 - c)
    chips = [(1 - x, y), (x, 1 - y), (1 - x, 1 - y)]

    def first():
        return [_remote(outs[w].at[k, _core_rows(outs[w], c)], outs[w].at[k, _core_rows(outs[w], c)],
                        send_sems.at[w, j], recv_sems.at[w, j], (px, py, c))
                for j, (px, py) in enumerate(chips) for w in range(n)]

    def start():
        for cp in first():
            cp.start()

    def finish():
        passed = []
        for j, (px, py) in enumerate(chips):
            for w in range(n):
                slot = outs[w].at[2 * px + py, _core_rows(outs[w], c)]
                _remote(slot, slot, send_sems.at[w, j], recv_sems.at[w, j], (px, py, c)).wait_recv()
                cp = _remote(slot, slot, send_sems.at[w, 3 + j], recv_sems.at[w, 3 + j], sibling)
                cp.start()
                passed.append(cp)
        for j, (px, py) in enumerate(chips):
            for w in range(n):
                slot = outs[w].at[2 * px + py, _core_rows(outs[w], 1 - c)]
                _remote(slot, slot, send_sems.at[w, 3 + j], recv_sems.at[w, 3 + j], sibling).wait_recv()
        for cp in first() + passed:
            cp.wait_send()

    return start, finish


def _gather_sems(n):
    return (pltpu.SemaphoreType.DMA((n, 6)), pltpu.SemaphoreType.DMA((n, 6)))


def gather_rider(slabs):
    return Rider(tuple(slabs), tuple(jax.ShapeDtypeStruct(a.shape, a.dtype) for a in slabs),
                 {i: i for i in range(len(slabs))}, _gather_sems(len(slabs)),
                 lambda ins, outs, sems: _gather_plan(outs, sems[0], sems[1]))


def gather_weights(slabs, small_slab, *, name):
    n = len(slabs)

    def body(*refs):
        outs, all_ref = refs[n + 1:2 * n + 1], refs[2 * n + 1]
        send_sems, recv_sems, s_send, s_recv = refs[2 * n + 2:]
        x, y, c = _position()
        finish_small = _small_all_gather(all_ref.at[4 * x + 2 * y + c], all_ref, s_send, s_recv, x, y, c)
        start, finish = _gather_plan(outs, send_sems, recv_sems)
        start()
        finish()
        finish_small()

    args = list(slabs) + [small_slab]
    out = pl.pallas_call(
        body, in_specs=[_ANY] * (n + 1), out_specs=[_ANY] * (n + 1),
        out_shape=[jax.ShapeDtypeStruct(a.shape, a.dtype) for a in args],
        input_output_aliases={i: i for i in range(n + 1)},
        scratch_shapes=list(_gather_sems(n)) + [pltpu.SemaphoreType.DMA((7,)), pltpu.SemaphoreType.DMA((7,))],
        name=name)(*args)
    return out[:n], out[n]


def exchange_with_sibling(parts, small_slab, *, name):
    n = len(parts)
    has_small = small_slab is not None
    n_arg = n + (1 if has_small else 0)

    def body(*refs):
        p_refs = refs[:n]
        lands = refs[n_arg:n_arg + n]
        send_sems, recv_sems = refs[2 * n_arg], refs[2 * n_arg + 1]
        x, y, c = _position()
        sibling = (x, y, 1 - c)
        if has_small:
            all_ref = refs[n_arg + n]
            finish_small = _small_all_gather(all_ref.at[4 * x + 2 * y + c], all_ref, refs[2 * n_arg + 2],
                                             refs[2 * n_arg + 3], x, y, c)
        sends = []
        for w in range(n):
            for d in range(4):
                cp = _remote(p_refs[w].at[d, _core_rows(p_refs[w], 1 - c)], lands[w].at[d],
                             send_sems.at[w, d], recv_sems.at[w, d], sibling)
                cp.start()
                sends.append(cp)
        for cp in sends:
            cp.wait_recv()
        for cp in sends:
            cp.wait_send()
        if has_small:
            finish_small()

    small_args = [small_slab] if has_small else []
    out = pl.pallas_call(
        body, in_specs=[_ANY] * n_arg, out_specs=[_ANY] * n_arg,
        out_shape=[jax.ShapeDtypeStruct((4, p.shape[1] // 2, p.shape[2]), p.dtype) for p in parts]
        + [jax.ShapeDtypeStruct(s.shape, s.dtype) for s in small_args],
        input_output_aliases={n: n} if has_small else {},
        scratch_shapes=[pltpu.SemaphoreType.DMA((n, 4)), pltpu.SemaphoreType.DMA((n, 4))]
        + ([pltpu.SemaphoreType.DMA((7,)), pltpu.SemaphoreType.DMA((7,))] if has_small else []),
        name=name)(*parts, *small_args)
    return out[:n], (out[n] if has_small else None)


def _scatter_plan(q_refs, outs, send_sems, recv_sems):
    n = len(q_refs)
    x, y, c = _position()
    k = 2 * x + y
    chips = [(1 - x, y), (x, 1 - y), (1 - x, 1 - y)]

    def sends():
        return [_remote(q_refs[w].at[2 * px + py], outs[w].at[k], send_sems.at[w, j], recv_sems.at[w, j], (px, py, c))
                for j, (px, py) in enumerate(chips) for w in range(n)]

    def start():
        for cp in sends():
            cp.start()

    def finish():
        for j, (px, py) in enumerate(chips):
            for w in range(n):
                slot = outs[w].at[2 * px + py]
                _remote(slot, slot, send_sems.at[w, j], recv_sems.at[w, j], (px, py, c)).wait_recv()
        for cp in sends():
            cp.wait_send()

    return start, finish


def _scatter_sems(n):
    return (pltpu.SemaphoreType.DMA((n, 3)), pltpu.SemaphoreType.DMA((n, 3)))


def scatter_rider(parts):
    return Rider(tuple(parts), tuple(jax.ShapeDtypeStruct(q.shape, q.dtype) for q in parts), {},
                 _scatter_sems(len(parts)), lambda ins, outs, sems: _scatter_plan(ins, outs, sems[0], sems[1]))


def scatter_to_chips(parts, *, name):
    n = len(parts)

    def body(*refs):
        start, finish = _scatter_plan(refs[:n], refs[n:2 * n], refs[2 * n], refs[2 * n + 1])
        start()
        finish()

    return pl.pallas_call(
        body, in_specs=[_ANY] * n, out_specs=[_ANY] * n,
        out_shape=[jax.ShapeDtypeStruct(q.shape, q.dtype) for q in parts],
        scratch_shapes=list(_scatter_sems(n)), name=name)(*parts)


def share_with_sibling(parts, *, name):
    n = len(parts)

    def body(*refs):
        r_refs, outs = refs[:n], refs[n:2 * n]
        send_sems, recv_sems = refs[2 * n:]
        x, y, c = _position()
        sends = []
        for w in range(n):
            cp = _remote(r_refs[w], outs[w], send_sems.at[w], recv_sems.at[w], (x, y, 1 - c))
            cp.start()
            sends.append(cp)
        for cp in sends:
            cp.wait_recv()
        for cp in sends:
            cp.wait_send()

    return pl.pallas_call(
        body, in_specs=[_ANY] * n, out_specs=[_ANY] * n,
        out_shape=[jax.ShapeDtypeStruct(r.shape, r.dtype) for r in parts],
        scratch_shapes=[pltpu.SemaphoreType.DMA((n,)), pltpu.SemaphoreType.DMA((n,))],
        name=name)(*parts)


def _cols_to_chips(full):
    *lead, R, C4 = full.shape
    t = full.reshape(*lead, R, 4, C4 // 4)
    return jnp.moveaxis(t, -2, 0)


def _chips_to_cols(sh):
    t = jnp.moveaxis(sh, 0, -2)
    return t.reshape(*t.shape[:-2], t.shape[-2] * t.shape[-1])


def _slot_in_empty(own, index, n):
    return lax.dynamic_update_slice(lax.empty((n,) + own.shape, own.dtype), own[None], (index,) + (0,) * own.ndim)


def _fold_pair(dg):
    return dg[0, :HEAD_DIM] + dg[0, HEAD_DIM:]


def _ffn_fwd(x, g, w_in_slab, w_out, tag):
    h = rms_fwd(x, g, name=f"{tag}_rms")
    a, b, act = swiglu_fwd(h, w_in_slab, name=f"{tag}_in")
    y = matmul(act, w_out, res=x, scale=0.5, tm=1024, tn=512, tk=w_out.shape[0], name=f"{tag}_out")
    return y, (x, h, a, b, act)


def _ffn_bwd(dy, saved, g, w_in_slab, w_out, tag):
    x, h, a, b, act = saved
    da, db = swiglu_bwd(dy, w_out, a, b, name=f"{tag}_dact")
    dw_out = grad_rows(act, dy, scale=0.5, name=f"{tag}_dwout")
    dw_in = grad_cols(h, da, db, name=f"{tag}_dwin")
    dx, dg = ffn_dh(da, db, w_in_slab, x, g, dy, name=f"{tag}_dh")
    return dx, dg[0], dw_in, dw_out


MEM_PAIRS = MEM_WIDTH // LANES


def _mem_attn_fwd(proj, mq0, mem_n, w_kv, g_q, g_k, tag):
    qh = pairnorm_fwd(proj, mq0, MEM_PAIRS, g_q, scale=QK_SCALE,name=f"{tag}_qnorm")
    kv = matmul(mem_n, w_kv, tm=256, tn=512, tk=1024, name=f"{tag}_kv")
    kh = pairnorm_fwd(kv, 0, MEM_PAIRS, g_k, name=f"{tag}_knorm")
    o, lse, _ = attn_fwd(qh, 0, kh, 0, kv, MEM_PAIRS, MEM_PAIRS, None, causal=False, name=f"{tag}_attn")
    return o, (qh, kv, kh, o, lse)


def _mem_attn_bwd(dmix, do0, proj, mq0, saved, mem_n, g_q, g_k, tag):
    qh, kv, kh, o, lse = saved
    delta = attn_delta(o, dmix, do0, MEM_PAIRS, name=f"{tag}_delta")
    (dqh, dkh, dv), _ = attn_bwd(qh, 0, kh, 0, kv, MEM_PAIRS, dmix, do0, MEM_PAIRS, lse, delta, None,
                                 causal=False, name=f"{tag}_dattn")
    dq_pre, dgq = pairnorm_bwd(proj, mq0, MEM_PAIRS, dqh, g_q, out_dtype=BF16, name=f"{tag}_dqnorm")
    dk_pre, dgk = pairnorm_bwd(kv, 0, MEM_PAIRS, dkh, g_k, name=f"{tag}_dknorm")
    dkv = jnp.concatenate([dk_pre, dv], axis=1)
    dw_kv = grad_rows(mem_n, dkv, name=f"{tag}_dwkv")
    return dq_pre, _fold_pair(dgq), _fold_pair(dgk), dw_kv, dkv


def _per_head_lanes(x, H):
    return jnp.pad(x.reshape(H, -1).T, ((0, 0), (0, LANES - H)))


def _fox_fwd(proj, b_f, g_q, g_k, tok, rider, tag):
    H, P = tok // HEAD_DIM, tok // LANES
    bias = jnp.pad(b_f.reshape(1, H), ((0, 0), (0, LANES - H)))
    qh = pairnorm_fwd(proj, 0, P, g_q, scale=QK_SCALE,name=f"{tag}_qnorm")
    kh = pairnorm_fwd(proj, P, P, g_k, name=f"{tag}_knorm")
    c = fgate_fwd(proj, 3 * P + MEM_PAIRS, bias, name=f"{tag}_fgate")
    decay = c[:, :H].T.reshape(P, 2, 1, c.shape[0])
    o, lse, rode = attn_fwd(qh, 0, kh, 0, proj, 2 * P, P, decay, causal=True, rider=rider, name=f"{tag}_attn")
    return o, (qh, kh, bias, decay, o, lse), rode


def _fox_bwd(dmix, proj, saved, g_q, g_k, tok, rider, tag):
    qh, kh, bias, decay, o, lse = saved
    H, P = tok // HEAD_DIM, tok // LANES
    delta = attn_delta(o, dmix, 0, P, name=f"{tag}_delta")
    (dqh, dkh, dv, dcs, drs), rode = attn_bwd(qh, 0, kh, 0, proj, 2 * P, dmix, 0, P, lse, delta, decay, causal=True,
                                              rider=rider, name=f"{tag}_dattn")
    dq_pre, dgq = pairnorm_bwd(proj, 0, P, dqh, g_q, out_dtype=BF16, name=f"{tag}_dqnorm")
    dk_pre, dgk = pairnorm_bwd(proj, P, P, dkh, g_k, out_dtype=BF16, name=f"{tag}_dknorm")
    dz, dbias = fgate_bwd(proj, 3 * P + MEM_PAIRS, bias, drs, _per_head_lanes(dcs, H), name=f"{tag}_dfgate")
    dqkv = jnp.concatenate([dq_pre, dk_pre, dv.astype(BF16)], axis=1)
    return dqkv, dz, dbias[0, :H], _fold_pair(dgq), _fold_pair(dgk), rode


def local_step(x, mem, target, W, comm=None):
    S, D = x.shape
    tok = D - MEM_WIDTH
    P = tok // LANES
    depth = W["norm_ffn1"].shape[0]
    mem_n = rms_fwd(mem, W["mem_norm"], name="mem_rms")
    saved = []
    for i in range(depth):
        kind, j = i % 2, i // 2
        t = f"l{i}"
        x1, s1 = _ffn_fwd(x, W["norm_ffn1"][i], W["ffn1_w_in"][i], W["ffn1_w_out"][i], f"{t}_ffn1")
        h = rms_fwd(x1, W["norm_mix"][i], name=f"{t}_mix_rms")
        w_mix = W["fox_w_in"][j] if kind == 0 else W["gmlp_w_in"][j]
        proj = matmul(h, w_mix, tm=1024, tn=896, tk=D, name=f"{t}_mix_in")
        if kind == 0:
            rider = comm.late_weights_rider() if (comm is not None and i == 0) else None
            o_tok, s_tok, rode = _fox_fwd(proj, W["fox_b_f"][j], W["fox_q_norm"][j], W["fox_k_norm"][j], tok, rider,
                                          f"{t}_fox")
            if rider is not None:
                comm.accept_late_weights(W, rode)
            mq0 = 3 * P
        else:
            vg, ws, bs = _gmlp_operands(W["gmlp_v_norm"][j], W["gmlp_w_s"][j], W["gmlp_b_s"][j])
            o_tok = gmlp_fwd(proj, P, P, vg, ws, bs, name=f"{t}_gmlp")
            s_tok = None
            mq0 = 2 * P
        o_mem, s_mem = _mem_attn_fwd(proj, mq0, mem_n, W["mem_w_kv"][i], W["mem_q_norm"][i], W["mem_k_norm"][i],
                                     f"{t}_mem")
        mix = jnp.concatenate([o_tok.astype(BF16), o_mem.astype(BF16)], axis=1)
        x2 = matmul(mix, W["w_out"][i], res=x1, tm=1024, tn=512, tk=D, name=f"{t}_mix_out")
        x3, s3 = _ffn_fwd(x2, W["norm_ffn2"][i], W["ffn2_w_in"][i], W["ffn2_w_out"][i], f"{t}_ffn2")
        saved.append((s1, x1, h, proj, mq0, s_tok, s_mem, mix, s3))
        x = x3

    dx, loss = loss_head(x, target, name="loss_head")

    G = {k: [None] * depth for k in ("norm_ffn1", "norm_mix", "norm_ffn2", "mem_q_norm", "mem_k_norm", "ffn1_w_in",
                                     "ffn1_w_out", "ffn2_w_in", "ffn2_w_out", "w_out", "mem_w_kv")}
    n_fox, n_gmlp = (depth + 1) // 2, depth // 2
    for k in ("fox_w_in", "fox_b_f", "fox_q_norm", "fox_k_norm"):
        G[k] = [None] * n_fox
    for k in ("gmlp_w_in", "gmlp_v_norm", "gmlp_w_s", "gmlp_b_s"):
        G[k] = [None] * n_gmlp
    dkv_all = [None] * depth
    for i in reversed(range(depth)):
        kind, j = i % 2, i // 2
        t = f"l{i}"
        s1, x1, h, proj, mq0, s_tok, s_mem, mix, s3 = saved[i]
        dx, G["norm_ffn2"][i], G["ffn2_w_in"][i], G["ffn2_w_out"][i] = _ffn_bwd(
            dx, s3, W["norm_ffn2"][i], W["ffn2_w_in"][i], W["ffn2_w_out"][i], f"{t}_ffn2")
        dmix = matmul(dx, W["w_out"][i], tb=True, tm=1024, tn=1024, tk=D, name=f"{t}_dmix")
        G["w_out"][i] = grad_rows(mix, dx, name=f"{t}_dwmixout")
        dmq, G["mem_q_norm"][i], G["mem_k_norm"][i], G["mem_w_kv"][i], dkv_all[i] = _mem_attn_bwd(
            dmix, P, proj, mq0, s_mem, mem_n, W["mem_q_norm"][i], W["mem_k_norm"][i], f"{t}_mem")
        if kind == 0:
            rider = comm.early_grads_rider(G) if (comm is not None and i == 0) else None
            dqkv, dz, G["fox_b_f"][j], G["fox_q_norm"][j], G["fox_k_norm"][j], rode = _fox_bwd(
                dmix, proj, s_tok, W["fox_q_norm"][j], W["fox_k_norm"][j], tok, rider, f"{t}_fox")
            if rider is not None:
                comm.accept_early_grads(rode)
            dproj = jnp.concatenate([dqkv, dmq, dz.astype(BF16)], axis=1)
            w_mix, wkey = W["fox_w_in"][j], "fox_w_in"
        else:
            vg, ws, bs = _gmlp_operands(W["gmlp_v_norm"][j], W["gmlp_w_s"][j], W["gmlp_b_s"][j])
            dup, dvp, dws, dbs, dvg = gmlp_bwd(proj, P, P, vg, ws, jnp.swapaxes(ws, 2, 3), bs, dmix,
                                               name=f"{t}_dgmlp")
            G["gmlp_w_s"][j] = dws.reshape(W["gmlp_w_s"][j].shape)
            G["gmlp_b_s"][j] = dbs.reshape(W["gmlp_b_s"][j].shape)
            G["gmlp_v_norm"][j] = dvg.reshape(-1)
            dproj = jnp.concatenate([dup.astype(BF16), dvp.astype(BF16), dmq], axis=1)
            w_mix, wkey = W["gmlp_w_in"][j], "gmlp_w_in"
        G[wkey][j] = matmul(h, dproj, ta=True, tm=1024, tn=896, tk=1024, name=f"{t}_dwmixin")
        dx, dgm = mix_dh(dproj, w_mix, x1, W["norm_mix"][i], dx, name=f"{t}_dhmix")
        G["norm_mix"][i] = dgm[0]
        dx, G["norm_ffn1"][i], G["ffn1_w_in"][i], G["ffn1_w_out"][i] = _ffn_bwd(
            dx, s1, W["norm_ffn1"][i], W["ffn1_w_in"][i], W["ffn1_w_out"][i], f"{t}_ffn1")
    w_kv_all = jnp.concatenate([W["mem_w_kv"][i] for i in range(depth)], axis=1)
    dmem_n = matmul(jnp.concatenate(dkv_all, axis=1), w_kv_all, tb=True, tm=256, tn=512, tk=1024, name="dmem_n")
    _, dmemg = rms_bwd(mem, dmem_n, W["mem_norm"], None, name="dmem_rms")
    G["mem_norm"] = [dmemg[0]]
    return loss, dx, G


def _fox_cols_to_compute(w, tok):
    H = tok // HEAD_DIM
    qkv, f, mq = w[..., :3 * tok], w[..., 3 * tok:3 * tok + H], w[..., 3 * tok + H:]
    f = jnp.pad(f, [(0, 0)] * (w.ndim - 1) + [(0, LANES - H)])
    return jnp.concatenate([qkv, mq, f], axis=-1)


def _fox_cols_from_compute(w, tok):
    H = tok // HEAD_DIM
    qkv, mq, f = w[..., :3 * tok], w[..., 3 * tok:3 * tok + MEM_WIDTH], w[..., 3 * tok + MEM_WIDTH:3 * tok + MEM_WIDTH + H]
    return jnp.concatenate([qkv, f, mq], axis=-1)


_BIG = ("ffn1_w_in", "ffn1_w_out", "ffn2_w_in", "ffn2_w_out", "w_out", "mem_w_kv", "fox_w_in", "gmlp_w_in")
_SMALL = ("norm_ffn1", "norm_mix", "norm_ffn2", "mem_norm", "mem_q_norm", "mem_k_norm", "fox_b_f", "fox_q_norm",
          "fox_k_norm", "gmlp_v_norm", "gmlp_w_s", "gmlp_b_s")
WEIGHT_ORDER = ("norm_ffn1", "ffn1_w_in", "ffn1_w_out", "norm_mix", "norm_ffn2", "ffn2_w_in", "ffn2_w_out", "w_out",
                "mem_norm", "mem_w_kv", "mem_q_norm", "mem_k_norm", "fox_w_in", "fox_b_f", "fox_q_norm", "fox_k_norm",
                "gmlp_w_in", "gmlp_v_norm", "gmlp_w_s", "gmlp_b_s")


def _small_slab(rows_list, index):
    sizes = [s.shape[0] for s in rows_list]
    n_rows = [-(-n // LANES) for n in sizes]
    small = jnp.concatenate([jnp.pad(s, (0, r * LANES - n)).reshape(r, LANES)
                             for s, n, r in zip(rows_list, sizes, n_rows)], axis=0)
    small = jnp.pad(small, ((0, -small.shape[0] % 64), (0, 0)))
    return _slot_in_empty(small, index, 8), sizes, n_rows


_FIRST_WEIGHTS = (("ffn1_w_in", 0), ("ffn1_w_out", 0), ("fox_w_in", 0))


def _weight_from_slab(name, slab, tok):
    if name in ("ffn1_w_in", "ffn2_w_in"):
        return slab
    if name == "fox_w_in":
        return _fox_cols_to_compute(_chips_to_cols(slab), tok)
    if name == "gmlp_w_in":
        return _chips_to_cols(slab)
    return slab.reshape(4 * slab.shape[1], slab.shape[2])


def _grad_to_slab(name, g, tok):
    if name == "fox_w_in":
        return _cols_to_chips(_fox_cols_from_compute(g, tok))
    if name == "gmlp_w_in":
        return _cols_to_chips(g)
    return g


class _Exchange:
    def __init__(self, shards, tok, chip, core):
        self.tok, self.core = tok, core
        self.half = core.reshape(1).astype(jnp.int32)
        self.chip_id = chip.reshape(1).astype(jnp.int32)
        items = [(k, i) for k in _BIG for i in range(shards[k].shape[0])]
        self.slabs = {it: _slot_in_empty(shards[it[0]][it[1]].astype(BF16), chip, 4) for it in items}
        self.late = [it for it in items if it not in _FIRST_WEIGHTS]
        self.reduced = {}
        self.early = None

    def first_weights(self, small_slab):
        got, small_all = gather_weights([self.slabs[it] for it in _FIRST_WEIGHTS], small_slab, name="gather_first")
        return {it: _weight_from_slab(it[0], s, self.tok) for it, s in zip(_FIRST_WEIGHTS, got)}, small_all

    def late_weights_rider(self):
        return gather_rider([self.slabs[it] for it in self.late])

    def accept_late_weights(self, W, got):
        for (k, i), s in zip(self.late, got):
            W[k][i] = _weight_from_slab(k, s, self.tok)

    def _pair_sums(self, G, items, small_slab, tag):
        parts = [_grad_to_slab(k, G[k][i], self.tok) for k, i in items]
        landed, small_all = exchange_with_sibling(parts, small_slab, name=f"grad_exchange_{tag}")
        pair = [pair_sum(p, l, self.half, name=f"grad_pair_sum_{k}{i}") for (k, i), p, l in zip(items, parts, landed)]
        return pair, small_all

    def early_grads_rider(self, G):
        items = [(k, i) for k in _BIG for i in range(len(G[k])) if G[k][i] is not None]
        pair, _ = self._pair_sums(G, items, None, "early")
        self.early = (items, pair)
        return scatter_rider(pair)

    def accept_early_grads(self, landed):
        items, pair = self.early
        self._chip_sums(items, pair, landed)

    def _chip_sums(self, items, pair, landed):
        for (k, i), q, l in zip(items, pair, landed):
            self.reduced[(k, i)] = chip_sum(q, l, self.chip_id, name=f"grad_chip_sum_{k}{i}")

    def finish_grads(self, G, small_slab):
        items = [(k, i) for k in _BIG for i in range(len(G[k])) if (k, i) not in self.reduced]
        pair, small_all = self._pair_sums(G, items, small_slab, "late")
        self._chip_sums(items, pair, scatter_to_chips(pair, name="grad_scatter_late"))
        order = sorted(self.reduced)
        other = share_with_sibling([self.reduced[it] for it in order], name="grad_share")
        full = {}
        for it, a, b in zip(order, [self.reduced[it] for it in order], other):
            full[it] = jnp.where(self.core == 0, jnp.concatenate([a, b]), jnp.concatenate([b, a]))
        names = sorted({k for k, _ in order})
        return {k: jnp.stack([full[(k, i)] for i in range(len(G[k]))]) for k in names}, small_all


def kernel(x, mem, norm_ffn1, ffn1_w_in, ffn1_w_out, norm_mix, norm_ffn2, ffn2_w_in, ffn2_w_out, w_out, mem_norm, mem_w_kv, mem_q_norm, mem_k_norm, fox_w_in, fox_b_f, fox_q_norm, fox_k_norm, gmlp_w_in, gmlp_v_norm, gmlp_w_s, gmlp_b_s, loss_target, m_norm_ffn1, m_ffn1_w_in, m_ffn1_w_out, m_norm_mix, m_norm_ffn2, m_ffn2_w_in, m_ffn2_w_out, m_w_out, m_mem_norm, m_mem_w_kv, m_mem_q_norm, m_mem_k_norm, m_fox_w_in, m_fox_b_f, m_fox_q_norm, m_fox_k_norm, m_gmlp_w_in, m_gmlp_v_norm, m_gmlp_w_s, m_gmlp_b_s, v_norm_ffn1, v_ffn1_w_in, v_ffn1_w_out, v_norm_mix, v_norm_ffn2, v_ffn2_w_in, v_ffn2_w_out, v_w_out, v_mem_norm, v_mem_w_kv, v_mem_q_norm, v_mem_k_norm, v_fox_w_in, v_fox_b_f, v_fox_q_norm, v_fox_k_norm, v_gmlp_w_in, v_gmlp_v_norm, v_gmlp_w_s, v_gmlp_b_s):
    w = dict(norm_ffn1=norm_ffn1, ffn1_w_in=ffn1_w_in, ffn1_w_out=ffn1_w_out, norm_mix=norm_mix, norm_ffn2=norm_ffn2,
             ffn2_w_in=ffn2_w_in, ffn2_w_out=ffn2_w_out, w_out=w_out, mem_norm=mem_norm, mem_w_kv=mem_w_kv,
             mem_q_norm=mem_q_norm, mem_k_norm=mem_k_norm, fox_w_in=fox_w_in, fox_b_f=fox_b_f, fox_q_norm=fox_q_norm,
             fox_k_norm=fox_k_norm, gmlp_w_in=gmlp_w_in, gmlp_v_norm=gmlp_v_norm, gmlp_w_s=gmlp_w_s, gmlp_b_s=gmlp_b_s)
    m = dict(norm_ffn1=m_norm_ffn1, ffn1_w_in=m_ffn1_w_in, ffn1_w_out=m_ffn1_w_out, norm_mix=m_norm_mix,
             norm_ffn2=m_norm_ffn2, ffn2_w_in=m_ffn2_w_in, ffn2_w_out=m_ffn2_w_out, w_out=m_w_out, mem_norm=m_mem_norm,
             mem_w_kv=m_mem_w_kv, mem_q_norm=m_mem_q_norm, mem_k_norm=m_mem_k_norm, fox_w_in=m_fox_w_in,
             fox_b_f=m_fox_b_f, fox_q_norm=m_fox_q_norm, fox_k_norm=m_fox_k_norm, gmlp_w_in=m_gmlp_w_in,
             gmlp_v_norm=m_gmlp_v_norm, gmlp_w_s=m_gmlp_w_s, gmlp_b_s=m_gmlp_b_s)
    v = dict(norm_ffn1=v_norm_ffn1, ffn1_w_in=v_ffn1_w_in, ffn1_w_out=v_ffn1_w_out, norm_mix=v_norm_mix,
             norm_ffn2=v_norm_ffn2, ffn2_w_in=v_ffn2_w_in, ffn2_w_out=v_ffn2_w_out, w_out=v_w_out, mem_norm=v_mem_norm,
             mem_w_kv=v_mem_w_kv, mem_q_norm=v_mem_q_norm, mem_k_norm=v_mem_k_norm, fox_w_in=v_fox_w_in,
             fox_b_f=v_fox_b_f, fox_q_norm=v_fox_q_norm, fox_k_norm=v_fox_k_norm, gmlp_w_in=v_gmlp_w_in,
             gmlp_v_norm=v_gmlp_v_norm, gmlp_w_s=v_gmlp_w_s, gmlp_b_s=v_gmlp_b_s)
    D = x.shape[-1]
    tok = D - MEM_WIDTH
    xi, yi, ci = _position()
    chip = 2 * xi + yi

    device = 4 * xi + 2 * yi + ci

    comm = _Exchange(w, tok, chip, ci)
    vn = w["gmlp_v_norm"]
    vn_slab, _, _ = _small_slab([vn.reshape(-1)], device)
    first, vn_all = comm.first_weights(vn_slab)
    W = {k: w[k] for k in _SMALL}
    W["gmlp_v_norm"] = _chips_to_cols(vn_all[0::2].reshape(4, -1)[:, :vn.size].reshape((4,) + vn.shape))
    for k in _BIG:
        W[k] = [first.get((k, i)) for i in range(w[k].shape[0])]

    loss, grad_x, g = local_step(x[0], mem[0], loss_target[0], W, comm)

    small_list = [jnp.stack(g[k]).reshape(-1) for k in _SMALL] + [loss.reshape(-1)]
    small, small_sizes, small_rows = _small_slab(small_list, device)
    red, small_all = comm.finish_grads(g, small)
    small_sum = ordered_sum(small_all, name="small_sum")
    off = 0
    for k, n, r in zip(_SMALL, small_sizes, small_rows):
        red[k] = small_sum[off:off + r].reshape(-1)[:n].reshape((-1,) + w[k].shape[1:] if k != "gmlp_v_norm"
                                                                else (w[k].shape[0], -1))
        off += r
    loss_total = small_sum[off, 0]
    vn_cols = w["gmlp_v_norm"].shape[-1]
    red["gmlp_v_norm"] = lax.dynamic_slice_in_dim(red["gmlp_v_norm"], chip * vn_cols, vn_cols, axis=-1)

    deltas, new_m, new_v = {}, {}, {}
    for k in WEIGHT_ORDER:
        wk = w[k] if w[k].ndim > 1 else w[k].reshape(1, -1)
        upd = adamw(wk, red[k].reshape(wk.shape), m[k].reshape(wk.shape), v[k].reshape(wk.shape), name=f"adamw_{k}")
        deltas[k], new_m[k], new_v[k] = (u.reshape(w[k].shape) for u in upd)
    return (loss_total, grad_x[None], *[red[k].reshape(w[k].shape) for k in WEIGHT_ORDER],
            *[deltas[k] for k in WEIGHT_ORDER], *[new_m[k] for k in WEIGHT_ORDER], *[new_v[k] for k in WEIGHT_ORDER])
```

```python
import functools
import math
from typing import Callable, NamedTuple

import jax
import jax.numpy as jnp
from jax import lax
from jax.experimental import pallas as pl
from jax.experimental.pallas import tpu as pltpu

F32 = jnp.float32
BF16 = jnp.bfloat16
EPS = 1e-6
HEAD_DIM = 64
MEM_WIDTH = 256
CHUNK = 128
LANES = 128
NEG = -1e30
VMEM_LIMIT_BYTES = 56 * 1024 * 1024
ATTN_Q_BLOCK = 1024
ATTN_K_BLOCK = 1024
ATTN_ROW_CHUNK = 1024
QK_SCALE = 0.125
MESH_ID = pl.DeviceIdType.MESH

ADAM_LR = 0.001
ADAM_B1 = 0.9
ADAM_B2 = 0.999
ADAM_EPS = 1e-08
ADAM_WD = 0.01
ADAM_STEP = 10


def _tile(n, pref, align):
    t = (min(pref, n) // align) * align
    while t >= align:
        if n % t == 0:
            return t
        t -= align
    return n


def _params(sem):
    return pltpu.CompilerParams(dimension_semantics=sem, vmem_limit_bytes=VMEM_LIMIT_BYTES)


def _dot(a, b, ca, cb):
    return lax.dot_general(a, b, (((ca,), (cb,)), ((), ())), preferred_element_type=F32)


def _sigmoid(x):
    return 1.0 / (1.0 + jnp.exp(-x))


_GELU_C = math.sqrt(2.0 / math.pi)


def _gelu(x):
    return 0.5 * x * (1.0 + jnp.tanh(_GELU_C * (x + 0.044715 * (x * x * x))))


def _gelu_grad(x):
    t = jnp.tanh(_GELU_C * (x + 0.044715 * (x * x * x)))
    return 0.5 * (1.0 + t) + 0.5 * x * (1.0 - t * t) * (_GELU_C * (1.0 + 3.0 * 0.044715 * (x * x)))


def matmul(a, b, *, ta=False, tb=False, out_dtype=F32, scale=None, res=None,
           tm=1024, tn=512, tk=1024, name):
    if ta:
        K, M = a.shape
    else:
        M, K = a.shape
    N = b.shape[0] if tb else b.shape[1]
    tm = _tile(M, tm, LANES if ta else 16)
    tn = _tile(N, tn, LANES)
    tk = _tile(K, tk, LANES)
    nk = K // tk
    a_spec = pl.BlockSpec((tk, tm), lambda i, j, k: (k, i)) if ta else pl.BlockSpec((tm, tk), lambda i, j, k: (i, k))
    b_spec = pl.BlockSpec((tn, tk), lambda i, j, k: (j, k)) if tb else pl.BlockSpec((tk, tn), lambda i, j, k: (k, j))
    o_spec = pl.BlockSpec((tm, tn), lambda i, j, k: (i, j))
    ca, cb = (0 if ta else 1), (1 if tb else 0)
    has_res = res is not None

    def body(*refs):
        a_ref, b_ref = refs[0], refs[1]
        res_ref = refs[2] if has_res else None
        o_ref = refs[3] if has_res else refs[2]
        acc_ref = refs[-1]
        k = pl.program_id(2)
        prod = _dot(a_ref[...].astype(BF16), b_ref[...].astype(BF16), ca, cb)

        def finish(acc):
            if scale is not None:
                acc = acc * scale
            if has_res:
                acc = res_ref[...] + acc
            o_ref[...] = acc.astype(out_dtype)

        if nk == 1:
            finish(prod)
        else:
            @pl.when(k == 0)
            def _():
                acc_ref[...] = prod

            @pl.when(k > 0)
            def _():
                acc_ref[...] += prod

            @pl.when(k == nk - 1)
            def _():
                finish(acc_ref[...])

    in_specs = [a_spec, b_spec] + ([o_spec] if has_res else [])
    args = (a, b) + ((res,) if has_res else ())
    return pl.pallas_call(
        body, grid=(M // tm, N // tn, nk), in_specs=in_specs, out_specs=o_spec,
        out_shape=jax.ShapeDtypeStruct((M, N), out_dtype),
        scratch_shapes=[pltpu.VMEM((tm, tn) if nk > 1 else (8, LANES), F32)],
        compiler_params=_params(("parallel", "parallel", "arbitrary")), name=name)(*args)


def swiglu_fwd(h, w_slab, *, name):
    S, D = h.shape
    Fc = w_slab.shape[-1]
    tm = _tile(S, 1024, 16)

    def body(h_ref, wa_ref, wb_ref, a_ref, b_ref, act_ref):
        hv = h_ref[...]
        a = _dot(hv, wa_ref[...], 1, 0)
        b = _dot(hv, wb_ref[...], 1, 0)
        a_ref[...] = a.astype(BF16)
        b_ref[...] = b.astype(BF16)
        act_ref[...] = (a * _sigmoid(a) * b).astype(BF16)

    out = pl.BlockSpec((tm, Fc), lambda j, i: (i, j))
    return pl.pallas_call(
        body, grid=(2, S // tm),
        in_specs=[pl.BlockSpec((tm, D), lambda j, i: (i, 0)),
                  pl.BlockSpec((None, D, Fc), lambda j, i: (j, 0, 0)),
                  pl.BlockSpec((None, D, Fc), lambda j, i: (j + 2, 0, 0))],
        out_specs=[out, out, out],
        out_shape=[jax.ShapeDtypeStruct((S, 2 * Fc), BF16)] * 3,
        compiler_params=_params(("parallel", "parallel")), name=name)(h, w_slab, w_slab)


def swiglu_bwd(dy, w_out, a, b, *, name):
    S, D = dy.shape
    F = w_out.shape[0]
    fc = F // 2
    tm = _tile(S, 512, 16)

    def body(dy_ref, w_ref, a_ref, b_ref, da_ref, db_ref):
        dact = 0.5 * _dot(dy_ref[...].astype(BF16), w_ref[...], 1, 1)
        av = a_ref[...].astype(F32)
        sg = _sigmoid(av)
        da_ref[...] = (dact * b_ref[...].astype(F32) * (sg * (1.0 + av * (1.0 - sg)))).astype(BF16)
        db_ref[...] = (dact * (av * sg)).astype(BF16)

    blk = pl.BlockSpec((tm, fc), lambda j, i: (i, j))
    return pl.pallas_call(
        body, grid=(2, S // tm),
        in_specs=[pl.BlockSpec((tm, D), lambda j, i: (i, 0)), pl.BlockSpec((fc, D), lambda j, i: (j, 0)), blk, blk],
        out_specs=[blk, blk],
        out_shape=[jax.ShapeDtypeStruct((S, F), BF16), jax.ShapeDtypeStruct((S, F), BF16)],
        compiler_params=_params(("parallel", "parallel")), name=name)(dy, w_out, a, b)


def ffn_dh(da, db, w_slab, x, g, dy, *, name):
    S, F = da.shape
    D, Fc = w_slab.shape[-2:]
    tm = _tile(S, 1024, 16)
    sub = _tile(tm, 256, 8)

    def body(da_ref, db_ref, w_ref, x_ref, g_ref, dy_ref, dx_ref, dg_ref, acc_ref):
        i, k = pl.program_id(0), pl.program_id(1)

        @pl.when(k == 0)
        def _():
            acc_ref[...] = jnp.zeros_like(acc_ref)

        @pl.when(k < 2)
        def _():
            acc_ref[...] += _dot(da_ref[...], w_ref[...], 1, 1)

        @pl.when(k >= 2)
        def _():
            acc_ref[...] += _dot(db_ref[...], w_ref[...], 1, 1)

        @pl.when(k == 3)
        def _():
            part = None
            for c in range(tm // sub):
                rows = pl.ds(c * sub, sub)
                xv, dh = x_ref[rows, :], acc_ref[rows, :]
                r = lax.rsqrt(jnp.mean(xv * xv, axis=-1, keepdims=True) + EPS)
                u = dh * g_ref[...]
                dx_ref[rows, :] = dy_ref[rows, :] + (r * u - xv * (r * r * r) * jnp.mean(xv * u, axis=-1, keepdims=True))
                p = jnp.sum(dh * xv * r, axis=0, keepdims=True)
                part = p if part is None else part + p

            @pl.when(i == 0)
            def _():
                dg_ref[...] = part

            @pl.when(i > 0)
            def _():
                dg_ref[...] += part

    row = pl.BlockSpec((tm, D), lambda i, k: (i, 0))
    vec = pl.BlockSpec((1, D), lambda i, k: (0, 0))
    return pl.pallas_call(
        body, grid=(S // tm, 4),
        in_specs=[pl.BlockSpec((tm, Fc), lambda i, k: (i, jnp.minimum(k, 1))),
                  pl.BlockSpec((tm, Fc), lambda i, k: (i, jnp.maximum(k - 2, 0))),
                  pl.BlockSpec((None, D, Fc), lambda i, k: (k, 0, 0)), row, vec, row],
        out_specs=[row, vec],
        out_shape=[jax.ShapeDtypeStruct((S, D), F32), jax.ShapeDtypeStruct((1, D), F32)],
        scratch_shapes=[pltpu.VMEM((tm, D), F32)],
        compiler_params=_params(("arbitrary", "arbitrary")), name=name)(da, db, w_slab, x, g.reshape(1, D), dy)


def mix_dh(dproj, w, x, g, dy, *, name):
    S, N = dproj.shape
    D = w.shape[0]
    tm = _tile(S, 1024, 16)
    sub = _tile(tm, 256, 8)
    tk = _tile(N, 896, LANES)
    nk = N // tk

    def body(p_ref, w_ref, x_ref, g_ref, dy_ref, dx_ref, dg_ref, acc_ref):
        i, k = pl.program_id(0), pl.program_id(1)
        prod = _dot(p_ref[...], w_ref[...], 1, 1)

        @pl.when(k == 0)
        def _():
            acc_ref[...] = prod

        @pl.when(k > 0)
        def _():
            acc_ref[...] += prod

        @pl.when(k == nk - 1)
        def _():
            part = None
            for c in range(tm // sub):
                rows = pl.ds(c * sub, sub)
                xv, dh = x_ref[rows, :], acc_ref[rows, :]
                r = lax.rsqrt(jnp.mean(xv * xv, axis=-1, keepdims=True) + EPS)
                u = dh * g_ref[...]
                dx_ref[rows, :] = dy_ref[rows, :] + (r * u - xv * (r * r * r) * jnp.mean(xv * u, axis=-1, keepdims=True))
                pp = jnp.sum(dh * xv * r, axis=0, keepdims=True)
                part = pp if part is None else part + pp

            @pl.when(i == 0)
            def _():
                dg_ref[...] = part

            @pl.when(i > 0)
            def _():
                dg_ref[...] += part

    row = pl.BlockSpec((tm, D), lambda i, k: (i, 0))
    vec = pl.BlockSpec((1, D), lambda i, k: (0, 0))
    return pl.pallas_call(
        body, grid=(S // tm, nk),
        in_specs=[pl.BlockSpec((tm, tk), lambda i, k: (i, k)), pl.BlockSpec((D, tk), lambda i, k: (0, k)), row, vec, row],
        out_specs=[row, vec],
        out_shape=[jax.ShapeDtypeStruct((S, D), F32), jax.ShapeDtypeStruct((1, D), F32)],
        scratch_shapes=[pltpu.VMEM((tm, D), F32)],
        compiler_params=_params(("arbitrary", "arbitrary")), name=name)(dproj, w, x, g.reshape(1, D), dy)


def grad_cols(h, da, db, *, name):
    S, D = h.shape
    Fc = da.shape[1] // 2
    tk = _tile(S, 1024, 16)
    nk = S // tk

    def body(h_ref, da_ref, db_ref, o_ref, acc_ref):
        ch, k = pl.program_id(0), pl.program_id(1)

        @pl.when(k == 0)
        def _():
            acc_ref[...] = jnp.zeros_like(acc_ref)

        @pl.when(ch < 2)
        def _():
            acc_ref[...] += _dot(h_ref[...], da_ref[...], 0, 0)

        @pl.when(ch >= 2)
        def _():
            acc_ref[...] += _dot(h_ref[...], db_ref[...], 0, 0)

        @pl.when(k == nk - 1)
        def _():
            o_ref[...] = acc_ref[...]

    return pl.pallas_call(
        body, grid=(4, nk),
        in_specs=[pl.BlockSpec((tk, D), lambda ch, k: (k, 0)),
                  pl.BlockSpec((tk, Fc), lambda ch, k: (jnp.where(ch < 2, k, 0), jnp.minimum(ch, 1))),
                  pl.BlockSpec((tk, Fc), lambda ch, k: (jnp.where(ch >= 2, k, 0), jnp.maximum(ch - 2, 0)))],
        out_specs=pl.BlockSpec((None, D, Fc), lambda ch, k: (ch, 0, 0)),
        out_shape=jax.ShapeDtypeStruct((4, D, Fc), F32),
        scratch_shapes=[pltpu.VMEM((D, Fc), F32)],
        compiler_params=_params(("parallel", "arbitrary")), name=name)(h, da, db)


def grad_rows(a, b, *, scale=None, name):
    S, M = a.shape
    N = b.shape[1]
    R = M // 4
    tn = _tile(N, 512, LANES)
    tk = _tile(S, 1024, 16)
    nk = S // tk

    def body(a_ref, b_ref, o_ref, acc_ref):
        k = pl.program_id(1)

        @pl.when(k == 0)
        def _():
            acc_ref[...] = jnp.zeros_like(acc_ref)

        acc_ref[...] += _dot(a_ref[...].astype(BF16), b_ref[...].astype(BF16), 0, 0)

        @pl.when(k == nk - 1)
        def _():
            for d in range(4):
                part = acc_ref[d * R:(d + 1) * R, :]
                o_ref[d] = part if scale is None else part * scale

    return pl.pallas_call(
        body, grid=(N // tn, nk),
        in_specs=[pl.BlockSpec((tk, M), lambda j, k: (k, 0)), pl.BlockSpec((tk, tn), lambda j, k: (k, j))],
        out_specs=pl.BlockSpec((4, R, tn), lambda j, k: (0, 0, j)),
        out_shape=jax.ShapeDtypeStruct((4, R, N), F32),
        scratch_shapes=[pltpu.VMEM((M, tn), F32)],
        compiler_params=_params(("parallel", "arbitrary")), name=name)(a, b)


def rms_fwd(x, g, *, name):
    S, D = x.shape
    ts = _tile(S, 1024, 16)

    def body(x_ref, g_ref, h_ref):
        xv = x_ref[...]
        r = lax.rsqrt(jnp.mean(xv * xv, axis=-1, keepdims=True) + EPS)
        h_ref[...] = (xv * r * g_ref[...]).astype(BF16)

    return pl.pallas_call(
        body, grid=(S // ts,),
        in_specs=[pl.BlockSpec((ts, D), lambda i: (i, 0)), pl.BlockSpec((1, D), lambda i: (0, 0))],
        out_specs=pl.BlockSpec((ts, D), lambda i: (i, 0)),
        out_shape=jax.ShapeDtypeStruct((S, D), BF16),
        compiler_params=_params(("parallel",)), name=name)(x, g.reshape(1, D))


def rms_bwd(x, dh, g, res, *, name):
    S, D = x.shape
    ts = _tile(S, 512, 16)
    has_res = res is not None

    def body(*refs):
        x_ref, dh_ref, g_ref = refs[:3]
        res_ref = refs[3] if has_res else None
        dx_ref, dg_ref = refs[-2:]
        i = pl.program_id(0)
        xv, dhv = x_ref[...], dh_ref[...].astype(F32)
        r = lax.rsqrt(jnp.mean(xv * xv, axis=-1, keepdims=True) + EPS)
        u = dhv * g_ref[...]
        dx = r * u - xv * (r * r * r) * jnp.mean(xv * u, axis=-1, keepdims=True)
        if has_res:
            dx = res_ref[...] + dx
        dx_ref[...] = dx
        part = jnp.sum(dhv * xv * r, axis=0, keepdims=True)

        @pl.when(i == 0)
        def _():
            dg_ref[...] = part

        @pl.when(i > 0)
        def _():
            dg_ref[...] += part

    row = pl.BlockSpec((ts, D), lambda i: (i, 0))
    vec = pl.BlockSpec((1, D), lambda i: (0, 0))
    args = (x, dh, g.reshape(1, D)) + ((res,) if has_res else ())
    return pl.pallas_call(
        body, grid=(S // ts,), in_specs=[row, row, vec] + ([row] if has_res else []),
        out_specs=[row, vec],
        out_shape=[jax.ShapeDtypeStruct((S, D), F32), jax.ShapeDtypeStruct((1, D), F32)],
        compiler_params=_params(("arbitrary",)), name=name)(*args)


def _low_half(shape):
    return lax.broadcasted_iota(jnp.int32, shape, len(shape) - 1) < HEAD_DIM


def _half_sums(x, low):
    sa = jnp.sum(jnp.where(low, x, 0.0), axis=1, keepdims=True)
    sb = jnp.sum(jnp.where(low, 0.0, x), axis=1, keepdims=True)
    return jnp.where(low, sa, sb)


def pairnorm_fwd(x, col0, n_pairs, g, *, scale=None, name):
    S = x.shape[0]
    ts = _tile(S, 512, 16)
    W = n_pairs * LANES
    assert col0 % n_pairs == 0

    def body(x_ref, g_ref, o_ref):
        for p in range(n_pairs):
            cols = pl.ds(p * LANES, LANES)
            xv = x_ref[:, cols]
            r = lax.rsqrt(_half_sums(xv * xv, _low_half(xv.shape)) * (1.0 / HEAD_DIM) + EPS)
            y = xv * r * g_ref[...]
            o_ref[:, cols] = (y if scale is None else y * scale).astype(BF16)

    return pl.pallas_call(
        body, grid=(S // ts,),
        in_specs=[pl.BlockSpec((ts, W), lambda i: (i, col0 // n_pairs)), pl.BlockSpec((1, LANES), lambda i: (0, 0))],
        out_specs=pl.BlockSpec((ts, W), lambda i: (i, 0)),
        out_shape=jax.ShapeDtypeStruct((S, W), BF16),
        compiler_params=_params(("parallel",)), name=name)(x, jnp.tile(g.reshape(1, HEAD_DIM), (1, 2)))


def pairnorm_bwd(x, col0, n_pairs, dy, g, *, out_dtype=F32, name):
    S = x.shape[0]
    ts = _tile(S, 512, 16)
    W = n_pairs * LANES
    assert col0 % n_pairs == 0

    def body(x_ref, dy_ref, g_ref, dx_ref, dg_ref):
        part = None
        for p in range(n_pairs):
            cols = pl.ds(p * LANES, LANES)
            xv, dyv = x_ref[:, cols], dy_ref[:, cols]
            low = _low_half(xv.shape)
            r = lax.rsqrt(_half_sums(xv * xv, low) * (1.0 / HEAD_DIM) + EPS)
            u = dyv * g_ref[...]
            dx = r * u - xv * (r * r * r) * (_half_sums(xv * u, low) * (1.0 / HEAD_DIM))
            dx_ref[:, cols] = dx.astype(out_dtype)
            pp = jnp.sum(dyv * xv * r, axis=0, keepdims=True)
            part = pp if part is None else part + pp

        @pl.when(pl.program_id(0) == 0)
        def _():
            dg_ref[...] = part

        @pl.when(pl.program_id(0) > 0)
        def _():
            dg_ref[...] += part

    vec = pl.BlockSpec((1, LANES), lambda i: (0, 0))
    blk = pl.BlockSpec((ts, W), lambda i: (i, 0))
    return pl.pallas_call(
        body, grid=(S // ts,),
        in_specs=[pl.BlockSpec((ts, W), lambda i: (i, col0 // n_pairs)), blk, vec], out_specs=[blk, vec],
        out_shape=[jax.ShapeDtypeStruct((S, W), out_dtype), jax.ShapeDtypeStruct((1, LANES), F32)],
        compiler_params=_params(("arbitrary",)), name=name)(x, dy, jnp.tile(g.reshape(1, HEAD_DIM), (1, 2)))


def _split3(x):
    x1 = x.astype(BF16)
    r1 = x - x1.astype(F32)
    x2 = r1.astype(BF16)
    x3 = (r1 - x2.astype(F32)).astype(BF16)
    return x1, x2, x3


def _tri_ones(n, lower):
    r = lax.broadcasted_iota(jnp.int32, (n, n), 0)
    c = lax.broadcasted_iota(jnp.int32, (n, n), 1)
    return jnp.where((c <= r) if lower else (c >= r), 1.0, 0.0).astype(BF16)


def fgate_fwd(z, col0, bias, *, name):
    S, L = z.shape[0], LANES
    tb = _tile(S, 512, 16)

    def body(z_ref, b_ref, c_ref, carry):
        i = pl.program_id(0)

        @pl.when(i == 0)
        def _():
            carry[...] = jnp.zeros_like(carry)

        zz = z_ref[...] + b_ref[...]
        lf = jnp.minimum(zz, 0.0) - jnp.log(1.0 + jnp.exp(-jnp.abs(zz)))
        tri = _tri_ones(tb, True)
        x1, x2, x3 = _split3(lf)
        c = (_dot(tri, x1, 1, 0) + _dot(tri, x2, 1, 0)) + _dot(tri, x3, 1, 0) + carry[...]
        c_ref[...] = c
        carry[...] += jnp.sum(lf, axis=0, keepdims=True)

    return pl.pallas_call(
        body, grid=(S // tb,),
        in_specs=[pl.BlockSpec((tb, L), lambda i: (i, col0)), pl.BlockSpec((1, L), lambda i: (0, 0))],
        out_specs=pl.BlockSpec((tb, L), lambda i: (i, 0)),
        out_shape=jax.ShapeDtypeStruct((S, L), F32),
        scratch_shapes=[pltpu.VMEM((1, L), F32)],
        compiler_params=_params(("arbitrary",)), name=name)(z, bias)


def fgate_bwd(z, col0, bias, drs, dcs, *, name):
    S, L = z.shape[0], LANES
    n_pairs = drs.shape[0]
    tb = _tile(S, 512, 16)
    nb = S // tb

    def body(z_ref, b_ref, drs_ref, dcs_ref, dz_ref, db_ref, carry):
        i = pl.program_id(0)

        @pl.when(i == 0)
        def _():
            carry[...] = jnp.zeros_like(carry)

        tri = _tri_ones(tb, False)
        lane = lax.broadcasted_iota(jnp.int32, (tb, L), 1)
        dc = -dcs_ref[...]
        for h in range(2 * n_pairs):
            dc = dc + jnp.where(lane == h, jnp.sum(drs_ref[h // 2, h % 2], axis=1, keepdims=True), 0.0)
        x1, x2, x3 = _split3(dc)
        dlf = (_dot(tri, x1, 1, 0) + _dot(tri, x2, 1, 0)) + _dot(tri, x3, 1, 0) + carry[...]
        carry[...] += jnp.sum(dc, axis=0, keepdims=True)
        dz = dlf * _sigmoid(-(z_ref[...] + b_ref[...]))
        dz_ref[...] = dz
        part = jnp.sum(dz, axis=0, keepdims=True)

        @pl.when(i == 0)
        def _():
            db_ref[...] = part

        @pl.when(i > 0)
        def _():
            db_ref[...] += part

    rev = pl.BlockSpec((tb, L), lambda i: (nb - 1 - i, 0))
    vec = pl.BlockSpec((1, L), lambda i: (0, 0))
    return pl.pallas_call(
        body, grid=(nb,),
        in_specs=[pl.BlockSpec((tb, L), lambda i: (nb - 1 - i, col0)), vec,
                  pl.BlockSpec((n_pairs, 2, tb, L), lambda i: (0, 0, nb - 1 - i, 0)), rev],
        out_specs=[rev, vec],
        out_shape=[jax.ShapeDtypeStruct((S, L), F32), jax.ShapeDtypeStruct((1, L), F32)],
        scratch_shapes=[pltpu.VMEM((1, L), F32)],
        compiler_params=_params(("arbitrary",)), name=name)(z, bias, drs, dcs)


def _one_head(x, low, a):
    return jnp.where(low if a == 0 else jnp.logical_not(low), x, jnp.zeros_like(x))


class Rider(NamedTuple):
    inputs: tuple
    out_shapes: tuple
    aliases: dict
    sems: tuple
    plan: Callable


def _with_rider(rider, n_in, n_out, n_scratch):
    if rider is None:
        return [], [], [], [], {}, lambda refs: (refs[:n_in], refs[n_in:n_in + n_out], refs[n_in + n_out:], None)
    e_in, e_out = len(rider.inputs), len(rider.out_shapes)

    def split(refs):
        ins, r_in = refs[:n_in], refs[n_in:n_in + e_in]
        o0 = n_in + e_in
        outs, r_out = refs[o0:o0 + n_out], refs[o0 + n_out:o0 + n_out + e_out]
        s0 = o0 + n_out + e_out
        return ins, outs, refs[s0:s0 + n_scratch], rider.plan(r_in, r_out, refs[s0 + n_scratch:])

    aliases = {n_in + a: n_out + b for a, b in rider.aliases.items()}
    return list(rider.inputs), [_ANY] * e_in, list(rider.out_shapes), [_ANY] * e_out, aliases, split


def attn_fwd(q, q0, k, k0, v, v0, n_pairs, decay, *, causal, rider=None, name):
    Sq, Sk = q.shape[0], k.shape[0]
    tq = _tile(Sq, ATTN_Q_BLOCK if causal else 4 * ATTN_Q_BLOCK, LANES)
    tk = _tile(Sk, ATTN_K_BLOCK, LANES)
    nq, nk = Sq // tq, Sk // tk
    bias = decay is not None
    rs = _tile(tq, ATTN_ROW_CHUNK, 16)
    r_args, r_in_specs, r_shapes, r_out_specs, aliases, split = _with_rider(rider, 4 if bias else 3, 2, 4)

    def row_sum_lanes(acc, low, a):
        other = jnp.logical_not(low) if a == 0 else low
        return jnp.max(jnp.where(other, acc, 0.0), axis=1, keepdims=True)

    live = [(i, j) for i in range(nq) for j in range(nk) if not causal or j * tk <= i * tq + tq - 1]
    n_live = len(live)

    def body(i_tab, j_tab, *refs):
        ins, (o_ref, lse_ref), scratch, ride = split(refs)
        m_sc, acc_sc = scratch[:2], scratch[2:]
        q_ref, k_ref, v_ref = ins[:3]
        ck_ref = ins[3] if bias else None
        pr, t = pl.program_id(0), pl.program_id(1)
        i, j = i_tab[t], j_tab[t]
        last_j = (i * tq + tq - 1) // tk if causal else nk - 1
        if ride is not None:
            pl.when(jnp.logical_and(pr == 0, t == 0))(ride[0])

        @pl.when(j == 0)
        def _():
            for a in range(2):
                m_sc[a][...] = jnp.full_like(m_sc[a], NEG)
                acc_sc[a][...] = jnp.zeros_like(acc_sc[a])

        def compute(masked):
            kv, vv = k_ref[...], v_ref[...].astype(BF16)
            low_k = _low_half(kv.shape)
            va = [jnp.where(low_k if a == 0 else jnp.logical_not(low_k), vv, jnp.ones_like(vv)) for a in range(2)]
            for r in range(tq // rs):
                rows = pl.ds(r * rs, rs)
                qv = q_ref[rows, :]
                low = _low_half(qv.shape)
                for a in range(2):
                    s = _dot(_one_head(qv, low, a), kv, 1, 1)
                    if bias:
                        s = s - ck_ref[a]
                    if masked:
                        row = i * tq + r * rs + lax.broadcasted_iota(jnp.int32, (rs, tk), 0)
                        col = j * tk + lax.broadcasted_iota(jnp.int32, (rs, tk), 1)
                        s = jnp.where(col <= row, s, NEG)
                    m_prev = m_sc[a][rows, :]
                    m_new = jnp.maximum(m_prev, jnp.max(s, axis=1, keepdims=True))
                    alpha = jnp.exp(m_prev - m_new)
                    p = jnp.exp(s - m_new).astype(BF16)
                    acc_sc[a][rows, :] = alpha * acc_sc[a][rows, :] + _dot(p, va[a], 1, 0)
                    m_sc[a][rows, :] = m_new

        if causal:
            crosses = j * tk + (tk - 1) > i * tq
            pl.when(crosses)(functools.partial(compute, True))
            pl.when(jnp.logical_not(crosses))(functools.partial(compute, False))
        else:
            compute(False)

        @pl.when(j == last_j)
        def _():
            low = _low_half((tq, LANES))
            l = [row_sum_lanes(acc_sc[a][...], low, a) for a in range(2)]
            o_ref[...] = jnp.where(low, acc_sc[0][...] / l[0], acc_sc[1][...] / l[1])
            for a in range(2):
                lse_ref[a] = m_sc[a][...] + jnp.log(l[a])

        if ride is not None:
            pl.when(jnp.logical_and(pr == n_pairs - 1, t == n_live - 1))(ride[1])

    in_specs = [pl.BlockSpec((tq, LANES), lambda p, t, it, jt: (it[t], q0 + p)),
                pl.BlockSpec((tk, LANES), lambda p, t, it, jt: (jt[t], k0 + p)),
                pl.BlockSpec((tk, LANES), lambda p, t, it, jt: (jt[t], v0 + p))]
    args = [q, k, v]
    if bias:
        in_specs.append(pl.BlockSpec((None, 2, 1, tk), lambda p, t, it, jt: (p, 0, 0, jt[t])))
        args.append(decay)
    tabs = [jnp.asarray([b[n] for b in live], jnp.int32) for n in range(2)]
    out = pl.pallas_call(
        body,
        grid_spec=pltpu.PrefetchScalarGridSpec(
            num_scalar_prefetch=2, grid=(n_pairs, n_live), in_specs=in_specs + r_in_specs,
            out_specs=[pl.BlockSpec((tq, LANES), lambda p, t, it, jt: (it[t], p)),
                       pl.BlockSpec((None, 2, tq, 1), lambda p, t, it, jt: (p, 0, it[t], 0))] + r_out_specs,
            scratch_shapes=[pltpu.VMEM((tq, 1), F32)] * 2 + [pltpu.VMEM((tq, LANES), F32)] * 2
            + (list(rider.sems) if rider else [])),
        out_shape=[jax.ShapeDtypeStruct((Sq, n_pairs * LANES), F32),
                   jax.ShapeDtypeStruct((n_pairs, 2, Sq, 1), F32)] + r_shapes,
        input_output_aliases={2 + a: b for a, b in aliases.items()},
        compiler_params=_params(("arbitrary", "arbitrary") if rider else ("parallel", "arbitrary")),
        name=name)(*tabs, *args, *r_args)
    return out[0], out[1], out[2:]


def attn_delta(o, do, do0, n_pairs, *, name):
    S = o.shape[0]
    ts = _tile(S, 512, 16)
    W = n_pairs * LANES
    assert do0 % n_pairs == 0

    def body(o_ref, do_ref, out_ref):
        for p in range(n_pairs):
            cols = pl.ds(p * LANES, LANES)
            prod = o_ref[:, cols] * do_ref[:, cols]
            low = _low_half(prod.shape)
            out_ref[p, 0] = jnp.sum(jnp.where(low, prod, 0.0), axis=1, keepdims=True)
            out_ref[p, 1] = jnp.sum(jnp.where(low, 0.0, prod), axis=1, keepdims=True)

    return pl.pallas_call(
        body, grid=(S // ts,),
        in_specs=[pl.BlockSpec((ts, W), lambda i: (i, 0)), pl.BlockSpec((ts, W), lambda i: (i, do0 // n_pairs))],
        out_specs=pl.BlockSpec((n_pairs, 2, ts, 1), lambda i: (0, 0, i, 0)),
        out_shape=jax.ShapeDtypeStruct((n_pairs, 2, S, 1), F32),
        compiler_params=_params(("parallel",)), name=name)(o, do)


def attn_bwd(q, q0, k, k0, v, v0, do, do0, n_pairs, lse, delta, decay, *, causal, rider=None, name):
    Sq, Sk = q.shape[0], k.shape[0]
    tq = _tile(Sq, ATTN_Q_BLOCK if causal else 4 * ATTN_Q_BLOCK, LANES)
    tk = _tile(Sk, ATTN_K_BLOCK, LANES)
    nq, nk = Sq // tq, Sk // tk
    bias = decay is not None

    rs = _tile(tq, ATTN_ROW_CHUNK, 16)
    r_args, r_in_specs, r_shapes, r_out_specs, aliases, split = _with_rider(
        rider, 7 if bias else 6, 5 if bias else 3, 0)

    live = [(i, j) for j in range(nk) for i in range(nq) if not causal or j * tk <= i * tq + tq - 1]
    n_live = len(live)

    def body(i_tab, j_tab, *refs):
        ins, outs, _, ride = split(refs)
        q_ref, k_ref, v_ref, do_ref, lse_ref, dl_ref = ins[:6]
        ck_ref = ins[6] if bias else None
        dq_ref, dk_ref, dv_ref = outs[:3]
        dcs_ref, drs_ref = (outs[3], outs[4]) if bias else (None, None)
        pr, t = pl.program_id(0), pl.program_id(1)
        i, j = i_tab[t], j_tab[t]
        first_i = (j * tk) // tq if causal else 0
        if ride is not None:
            pl.when(jnp.logical_and(pr == 0, t == 0))(ride[0])

        @pl.when(i == first_i)
        def _():
            dk_ref[...] = jnp.zeros_like(dk_ref)
            dv_ref[...] = jnp.zeros_like(dv_ref)
            if bias:
                dcs_ref[...] = jnp.zeros_like(dcs_ref)

        def compute(masked):
            kv, vv = k_ref[...], v_ref[...].astype(BF16)
            low_k = _low_half(kv.shape)
            ka = [_one_head(kv, low_k, a) for a in range(2)]
            for r in range(tq // rs):
                here = pl.ds(r * rs, rs)
                rows = pl.ds(pl.multiple_of(i * tq + r * rs, rs), rs)
                qv, dov = q_ref[here, :], do_ref[here, :].astype(BF16)
                low = _low_half(qv.shape)
                dq_part, dk_part, dv_part, row_parts, col_parts = None, None, None, [], []
                for a in range(2):
                    qa, doa = _one_head(qv, low, a), _one_head(dov, low, a)
                    s = _dot(qa, kv, 1, 1)
                    if bias:
                        s = s - ck_ref[a]
                    p = jnp.exp(s - lse_ref[a, here])
                    if masked:
                        row = i * tq + r * rs + lax.broadcasted_iota(jnp.int32, (rs, tk), 0)
                        col = j * tk + lax.broadcasted_iota(jnp.int32, (rs, tk), 1)
                        p = jnp.where(col <= row, p, 0.0)
                    dv_a = _dot(p.astype(BF16), doa, 0, 0)
                    dp = _dot(doa, vv, 1, 1)
                    ds = p * (dp - dl_ref[a, here])
                    dsb = ds.astype(BF16)
                    dk_a = _dot(dsb, qa, 0, 0)
                    if bias:
                        col_parts.append(jnp.sum(ds, axis=0, keepdims=True))
                        lanes = ds[:, :LANES]
                        for c in range(1, tk // LANES):
                            lanes = lanes + ds[:, c * LANES:(c + 1) * LANES]
                        row_parts.append(lanes)
                    part = _dot(dsb, ka[a], 1, 0) * QK_SCALE
                    dq_part = part if dq_part is None else dq_part + part
                    dk_part = dk_a if dk_part is None else dk_part + dk_a
                    dv_part = dv_a if dv_part is None else dv_part + dv_a
                dv_ref[...] += dv_part
                dk_ref[...] += dk_part
                for a, cp in enumerate(col_parts):
                    dcs_ref[a] += cp

                @pl.when(j == 0)
                def _(rows=rows, dq_part=dq_part, row_parts=row_parts):
                    dq_ref[rows, :] = dq_part
                    for a, rp in enumerate(row_parts):
                        drs_ref[a, rows, :] = rp

                @pl.when(j > 0)
                def _(rows=rows, dq_part=dq_part, row_parts=row_parts):
                    dq_ref[rows, :] += dq_part
                    for a, rp in enumerate(row_parts):
                        drs_ref[a, rows, :] += rp

        if causal:
            crosses = j * tk + (tk - 1) > i * tq
            pl.when(crosses)(functools.partial(compute, True))
            pl.when(jnp.logical_not(crosses))(functools.partial(compute, False))
        else:
            compute(False)

        if ride is not None:
            pl.when(jnp.logical_and(pr == n_pairs - 1, t == n_live - 1))(ride[1])

    col1 = pl.BlockSpec((None, 2, tq, 1), lambda p, t, it, jt: (p, 0, it[t], 0))
    in_specs = [pl.BlockSpec((tq, LANES), lambda p, t, it, jt: (it[t], q0 + p)),
                pl.BlockSpec((tk, LANES), lambda p, t, it, jt: (jt[t], k0 + p)),
                pl.BlockSpec((tk, LANES), lambda p, t, it, jt: (jt[t], v0 + p)),
                pl.BlockSpec((tq, LANES), lambda p, t, it, jt: (it[t], do0 + p)), col1, col1]
    args = [q, k, v, do, lse, delta]
    kout = pl.BlockSpec((tk, LANES), lambda p, t, it, jt: (jt[t], p))
    out_specs = [pl.BlockSpec((Sq, LANES), lambda p, t, it, jt: (0, p)), kout, kout]
    out_shape = [jax.ShapeDtypeStruct((Sq, n_pairs * LANES), F32), jax.ShapeDtypeStruct((Sk, n_pairs * LANES), F32),
                 jax.ShapeDtypeStruct((Sk, n_pairs * LANES), F32)]
    if bias:
        in_specs.append(pl.BlockSpec((None, 2, 1, tk), lambda p, t, it, jt: (p, 0, 0, jt[t])))
        args.append(decay)
        out_specs += [pl.BlockSpec((None, 2, 1, tk), lambda p, t, it, jt: (p, 0, 0, jt[t])),
                      pl.BlockSpec((None, 2, Sq, LANES), lambda p, t, it, jt: (p, 0, 0, 0))]
        out_shape += [jax.ShapeDtypeStruct((n_pairs, 2, 1, Sk), F32),
                      jax.ShapeDtypeStruct((n_pairs, 2, Sq, LANES), F32)]
    n_own = len(out_shape)
    tabs = [jnp.asarray([b[n] for b in live], jnp.int32) for n in range(2)]
    out = pl.pallas_call(
        body,
        grid_spec=pltpu.PrefetchScalarGridSpec(
            num_scalar_prefetch=2, grid=(n_pairs, n_live), in_specs=in_specs + r_in_specs,
            out_specs=out_specs + r_out_specs, scratch_shapes=list(rider.sems) if rider else []),
        out_shape=out_shape + r_shapes,
        input_output_aliases={2 + a: b for a, b in aliases.items()},
        compiler_params=_params(("arbitrary", "arbitrary") if rider else ("parallel", "arbitrary")),
        name=name)(*tabs, *args, *r_args)
    return tuple(out[:n_own]), out[n_own:]


def _tril_mask(n):
    r = lax.broadcasted_iota(jnp.int32, (n, n), 0)
    c = lax.broadcasted_iota(jnp.int32, (n, n), 1)
    return c <= r


def _gmlp_operands(v_gain, w_s, b_s):
    G = w_s.shape[0]
    return (v_gain.reshape(G // 2, 1, LANES), w_s.reshape(G // 2, 2, CHUNK, CHUNK), b_s.reshape(G // 2, 2, CHUNK, 1))


def _gmlp_gate(wt, vh, b_ref, low):
    gate = _dot(wt[0], _one_head(vh, low, 0), 1, 0) + _dot(wt[1], _one_head(vh, low, 1), 1, 0)
    return gate + jnp.where(low, b_ref[0], b_ref[1])


def gmlp_fwd(proj, v0, n_pairs, vg, w, b, *, name):
    S = proj.shape[0]
    ts = _tile(S, 1024, CHUNK)

    def body(up_ref, vp_ref, vg_ref, w_ref, b_ref, o_ref):
        mask = _tril_mask(CHUNK)
        wt = [jnp.where(mask, w_ref[a], 0.0).astype(BF16) for a in range(2)]
        low = _low_half((CHUNK, LANES))
        for c in range(ts // CHUNK):
            sl = pl.ds(c * CHUNK, CHUNK)
            vz = _gelu(vp_ref[sl, :])
            r = lax.rsqrt(_half_sums(vz * vz, low) * (1.0 / HEAD_DIM) + EPS)
            vh = (vz * r * vg_ref[...]).astype(BF16)
            o_ref[sl, :] = _gelu(up_ref[sl, :]) * _gmlp_gate(wt, vh, b_ref, low)

    return pl.pallas_call(
        body, grid=(n_pairs, S // ts),
        in_specs=[pl.BlockSpec((ts, LANES), lambda p, i: (i, p)), pl.BlockSpec((ts, LANES), lambda p, i: (i, v0 + p)),
                  pl.BlockSpec((None, 1, LANES), lambda p, i: (p, 0, 0)),
                  pl.BlockSpec((None, 2, CHUNK, CHUNK), lambda p, i: (p, 0, 0, 0)),
                  pl.BlockSpec((None, 2, CHUNK, 1), lambda p, i: (p, 0, 0, 0))],
        out_specs=pl.BlockSpec((ts, LANES), lambda p, i: (i, p)),
        out_shape=jax.ShapeDtypeStruct((S, n_pairs * LANES), F32),
        compiler_params=_params(("parallel", "parallel")), name=name)(proj, proj, vg, w, b)


def gmlp_bwd(proj, v0, n_pairs, vg, w, wT, b, do, *, name):
    S = proj.shape[0]
    ts = _tile(S, 1024, CHUNK)

    def body(up_ref, vp_ref, vg_ref, w_ref, wT_ref, b_ref, do_ref, dup_ref, dvp_ref, dw_ref, db_ref, dvg_ref):
        i = pl.program_id(1)

        @pl.when(i == 0)
        def _():
            dw_ref[...] = jnp.zeros_like(dw_ref)
            db_ref[...] = jnp.zeros_like(db_ref)
            dvg_ref[...] = jnp.zeros_like(dvg_ref)

        mask = _tril_mask(CHUNK)
        wt = [jnp.where(mask, w_ref[a], 0.0).astype(BF16) for a in range(2)]
        wtT = [jnp.where(mask.T, wT_ref[a], 0.0).astype(BF16) for a in range(2)]
        low = _low_half((CHUNK, LANES))
        vgain = vg_ref[...]
        for c in range(ts // CHUNK):
            sl = pl.ds(c * CHUNK, CHUNK)
            u_pre, v_pre, dout = up_ref[sl, :], vp_ref[sl, :], do_ref[sl, :]
            vz = _gelu(v_pre)
            r = lax.rsqrt(_half_sums(vz * vz, low) * (1.0 / HEAD_DIM) + EPS)
            vh = (vz * r * vgain).astype(BF16)
            gate = _gmlp_gate(wt, vh, b_ref, low)
            dgate = dout * _gelu(u_pre)
            dup_ref[sl, :] = dout * gate * _gelu_grad(u_pre)
            dvh = None
            for a in range(2):
                dga = _one_head(dgate, low, a)
                dgb = dga.astype(BF16)
                dw_ref[a] += jnp.where(mask, _dot(dgb, vh, 1, 1), 0.0)
                db_ref[a] += jnp.sum(dga, axis=1, keepdims=True)
                part = _dot(wtT[a], dgb, 1, 0)
                dvh = part if dvh is None else dvh + part
            dvg_ref[...] += jnp.sum(dvh * vz * r, axis=0, keepdims=True)
            t = dvh * vgain
            dvz = r * t - vz * (r * r * r) * (_half_sums(vz * t, low) * (1.0 / HEAD_DIM))
            dvp_ref[sl, :] = dvz * _gelu_grad(v_pre)

    ublk = pl.BlockSpec((ts, LANES), lambda p, i: (i, p))
    wblk = pl.BlockSpec((None, 2, CHUNK, CHUNK), lambda p, i: (p, 0, 0, 0))
    bblk = pl.BlockSpec((None, 2, CHUNK, 1), lambda p, i: (p, 0, 0, 0))
    gblk = pl.BlockSpec((None, 1, LANES), lambda p, i: (p, 0, 0))
    return pl.pallas_call(
        body, grid=(n_pairs, S // ts),
        in_specs=[ublk, pl.BlockSpec((ts, LANES), lambda p, i: (i, v0 + p)), gblk, wblk, wblk, bblk, ublk],
        out_specs=[ublk, ublk, wblk, bblk, gblk],
        out_shape=[jax.ShapeDtypeStruct((S, n_pairs * LANES), F32), jax.ShapeDtypeStruct((S, n_pairs * LANES), F32),
                   jax.ShapeDtypeStruct((n_pairs, 2, CHUNK, CHUNK), F32), jax.ShapeDtypeStruct((n_pairs, 2, CHUNK, 1), F32),
                   jax.ShapeDtypeStruct((n_pairs, 1, LANES), F32)],
        compiler_params=_params(("parallel", "arbitrary")), name=name)(proj, proj, vg, w, wT, b, do)


def loss_head(y, target, *, name):
    S, D = y.shape
    ts = _tile(S, 512, 8)

    def body(y_ref, t_ref, dy_ref, loss_ref):
        i = pl.program_id(0)
        e = y_ref[...] - t_ref[...]
        dy_ref[...] = e * (1.0 / D)
        part = jnp.sum(jnp.sum(e * e, axis=1, keepdims=True), axis=0, keepdims=True) * (0.5 / D)

        @pl.when(i == 0)
        def _():
            loss_ref[...] = part

        @pl.when(i > 0)
        def _():
            loss_ref[...] += part

    row = pl.BlockSpec((ts, D), lambda i: (i, 0))
    return pl.pallas_call(
        body, grid=(S // ts,), in_specs=[row, row],
        out_specs=[row, pl.BlockSpec((1, 1), lambda i: (0, 0))],
        out_shape=[jax.ShapeDtypeStruct((S, D), F32), jax.ShapeDtypeStruct((1, 1), F32)],
        compiler_params=_params(("arbitrary",)), name=name)(y, target)


def adamw(w, g, m, v, *, name):
    shape = w.shape
    C = shape[-1]
    R = w.size // C
    tr = _tile(R, max(8, (256 * 1024) // C // 8 * 8), 8)

    def body(w_ref, g_ref, m_ref, v_ref, d_ref, nm_ref, nv_ref):
        gv = g_ref[...]
        nm = ADAM_B1 * m_ref[...] + (1.0 - ADAM_B1) * gv
        nv = ADAM_B2 * v_ref[...] + (1.0 - ADAM_B2) * (gv * gv)
        m_hat = nm / (1.0 - ADAM_B1 ** ADAM_STEP)
        v_hat = nv / (1.0 - ADAM_B2 ** ADAM_STEP)
        d_ref[...] = -ADAM_LR * (m_hat / (jnp.sqrt(v_hat) + ADAM_EPS) + ADAM_WD * w_ref[...])
        nm_ref[...] = nm
        nv_ref[...] = nv

    blk = pl.BlockSpec((tr, C), lambda i: (i, 0))
    out = pl.pallas_call(
        body, grid=(R // tr,), in_specs=[blk] * 4, out_specs=[blk] * 3,
        out_shape=[jax.ShapeDtypeStruct((R, C), F32)] * 3,
        compiler_params=_params(("parallel",)), name=name)(*(a.reshape(R, C) for a in (w, g, m, v)))
    return tuple(o.reshape(shape) for o in out)


def pair_sum(p, landed, half, *, name):
    n, R, C = landed.shape
    tr = _tile(R, 512, 16)
    nr = R // tr

    def body(half_ref, p_ref, l_ref, o_ref):
        o_ref[...] = (p_ref[...] + l_ref[...]).astype(BF16)

    return pl.pallas_call(
        body,
        grid_spec=pltpu.PrefetchScalarGridSpec(
            num_scalar_prefetch=1, grid=(n, nr),
            in_specs=[pl.BlockSpec((None, tr, C), lambda k, r, half_ref: (k, half_ref[0] * nr + r, 0)),
                      pl.BlockSpec((None, tr, C), lambda k, r, half_ref: (k, r, 0))],
            out_specs=pl.BlockSpec((None, tr, C), lambda k, r, half_ref: (k, r, 0))),
        out_shape=jax.ShapeDtypeStruct((n, R, C), BF16),
        compiler_params=_params(("parallel", "parallel")), name=name)(half, p, landed)


def chip_sum(own, landed, chip, *, name):
    n, R, C = own.shape
    tr = _tile(R, 512, 16)

    def body(chip_ref, own_ref, *rest):
        l_refs, o_ref = rest[:n], rest[n]
        me = chip_ref[0]
        acc = None
        for d in range(n):
            term = jnp.where(me == d, own_ref[...], l_refs[d][...]).astype(F32)
            acc = term if acc is None else acc + term
        o_ref[...] = acc

    def landed_spec(d):
        return pl.BlockSpec((None, tr, C), lambda r, chip_ref: (jnp.where(chip_ref[0] == d, (d + 1) % n, d), r, 0))

    return pl.pallas_call(
        body,
        grid_spec=pltpu.PrefetchScalarGridSpec(
            num_scalar_prefetch=1, grid=(R // tr,),
            in_specs=[pl.BlockSpec((None, tr, C), lambda r, chip_ref: (chip_ref[0], r, 0))]
            + [landed_spec(d) for d in range(n)],
            out_specs=pl.BlockSpec((tr, C), lambda r, chip_ref: (r, 0))),
        out_shape=jax.ShapeDtypeStruct((R, C), F32),
        compiler_params=_params(("parallel",)), name=name)(chip, own, *([landed] * n))


def ordered_sum(parts, *, name):
    n, R, C = parts.shape
    tr = _tile(R, 512, 16)

    def body(p_ref, o_ref):
        acc = p_ref[0].astype(F32)
        for d in range(1, n):
            acc = acc + p_ref[d].astype(F32)
        o_ref[...] = acc

    return pl.pallas_call(
        body, grid=(R // tr,), in_specs=[pl.BlockSpec((n, tr, C), lambda r: (0, r, 0))],
        out_specs=pl.BlockSpec((tr, C), lambda r: (r, 0)),
        out_shape=jax.ShapeDtypeStruct((R, C), F32),
        compiler_params=_params(("parallel",)), name=name)(parts)


_ANY = pl.BlockSpec(memory_space=pl.ANY)


def _position():
    return lax.axis_index("x"), lax.axis_index("y"), lax.axis_index("c")


def _remote(src, dst, send_sem, recv_sem, device):
    return pltpu.make_async_remote_copy(src_ref=src, dst_ref=dst, send_sem=send_sem, recv_sem=recv_sem,
                                        device_id=device, device_id_type=MESH_ID)


def _small_all_gather(s_ref, all_ref, send_sems, recv_sems, x, y, c):
    me = 4 * x + 2 * y + c
    copies = []
    for f in range(1, 8):
        peer = ((1 - x) if f & 4 else x, (1 - y) if f & 2 else y, (1 - c) if f & 1 else c)
        cp = _remote(s_ref, all_ref.at[me], send_sems.at[f - 1], recv_sems.at[f - 1], peer)
        cp.start()
        copies.append((cp, peer, f - 1))

    def finish():
        for cp, peer, s in copies:
            slot = all_ref.at[4 * peer[0] + 2 * peer[1] + peer[2]]
            _remote(slot, slot, send_sems.at[s], recv_sems.at[s], peer).wait_recv()
        for cp, _, _ in copies:
            cp.wait_send()

    return finish


def _core_rows(ref, core):
    h = ref.shape[1] // 2
    return pl.ds(core * h, h)


def _gather_plan(outs, send_sems, recv_sems):
    n = len(outs)
    x, y, c = _position()
    k = 2 * x + y
    sibling = (x, y, 1 - c)
    chips = [(1 - x, y), (x, 1 - y), (1 - x, 1 - y)]

    def first():
        return [_remote(outs[w].at[k, _core_rows(outs[w], c)], outs[w].at[k, _core_rows(outs[w], c)],
                        send_sems.at[w, j], recv_sems.at[w, j], (px, py, c))
                for j, (px, py) in enumerate(chips) for w in range(n)]

    def start():
        for cp in first():
            cp.start()

    def finish():
        passed = []
        for j, (px, py) in enumerate(chips):
            for w in range(n):
                slot = outs[w].at[2 * px + py, _core_rows(outs[w], c)]
                _remote(slot, slot, send_sems.at[w, j], recv_sems.at[w, j], (px, py, c)).wait_recv()
                cp = _remote(slot, slot, send_sems.at[w, 3 + j], recv_sems.at[w, 3 + j], sibling)
                cp.start()
                passed.append(cp)
        for j, (px, py) in enumerate(chips):
            for w in range(n):
                slot = outs[w].at[2 * px + py, _core_rows(outs[w], 1 - c)]
                _remote(slot, slot, send_sems.at[w, 3 + j], recv_sems.at[w, 3 + j], sibling).wait_recv()
        for cp in first() + passed:
            cp.wait_send()

    return start, finish


def _gather_sems(n):
    return (pltpu.SemaphoreType.DMA((n, 6)), pltpu.SemaphoreType.DMA((n, 6)))


def gather_rider(slabs):
    return Rider(tuple(slabs), tuple(jax.ShapeDtypeStruct(a.shape, a.dtype) for a in slabs),
                 {i: i for i in range(len(slabs))}, _gather_sems(len(slabs)),
                 lambda ins, outs, sems: _gather_plan(outs, sems[0], sems[1]))


def gather_weights(slabs, small_slab, *, name):
    n = len(slabs)

    def body(*refs):
        outs, all_ref = refs[n + 1:2 * n + 1], refs[2 * n + 1]
        send_sems, recv_sems, s_send, s_recv = refs[2 * n + 2:]
        x, y, c = _position()
        finish_small = _small_all_gather(all_ref.at[4 * x + 2 * y + c], all_ref, s_send, s_recv, x, y, c)
        start, finish = _gather_plan(outs, send_sems, recv_sems)
        start()
        finish()
        finish_small()

    args = list(slabs) + [small_slab]
    out = pl.pallas_call(
        body, in_specs=[_ANY] * (n + 1), out_specs=[_ANY] * (n + 1),
        out_shape=[jax.ShapeDtypeStruct(a.shape, a.dtype) for a in args],
        input_output_aliases={i: i for i in range(n + 1)},
        scratch_shapes=list(_gather_sems(n)) + [pltpu.SemaphoreType.DMA((7,)), pltpu.SemaphoreType.DMA((7,))],
        name=name)(*args)
    return out[:n], out[n]


def exchange_with_sibling(parts, small_slab, *, name):
    n = len(parts)
    has_small = small_slab is not None
    n_arg = n + (1 if has_small else 0)

    def body(*refs):
        p_refs = refs[:n]
        lands = refs[n_arg:n_arg + n]
        send_sems, recv_sems = refs[2 * n_arg], refs[2 * n_arg + 1]
        x, y, c = _position()
        sibling = (x, y, 1 - c)
        if has_small:
            all_ref = refs[n_arg + n]
            finish_small = _small_all_gather(all_ref.at[4 * x + 2 * y + c], all_ref, refs[2 * n_arg + 2],
                                             refs[2 * n_arg + 3], x, y, c)
        sends = []
        for w in range(n):
            for d in range(4):
                cp = _remote(p_refs[w].at[d, _core_rows(p_refs[w], 1 - c)], lands[w].at[d],
                             send_sems.at[w, d], recv_sems.at[w, d], sibling)
                cp.start()
                sends.append(cp)
        for cp in sends:
            cp.wait_recv()
        for cp in sends:
            cp.wait_send()
        if has_small:
            finish_small()

    small_args = [small_slab] if has_small else []
    out = pl.pallas_call(
        body, in_specs=[_ANY] * n_arg, out_specs=[_ANY] * n_arg,
        out_shape=[jax.ShapeDtypeStruct((4, p.shape[1] // 2, p.shape[2]), p.dtype) for p in parts]
        + [jax.ShapeDtypeStruct(s.shape, s.dtype) for s in small_args],
        input_output_aliases={n: n} if has_small else {},
        scratch_shapes=[pltpu.SemaphoreType.DMA((n, 4)), pltpu.SemaphoreType.DMA((n, 4))]
        + ([pltpu.SemaphoreType.DMA((7,)), pltpu.SemaphoreType.DMA((7,))] if has_small else []),
        name=name)(*parts, *small_args)
    return out[:n], (out[n] if has_small else None)


def _scatter_plan(q_refs, outs, send_sems, recv_sems):
    n = len(q_refs)
    x, y, c = _position()
    k = 2 * x + y
    chips = [(1 - x, y), (x, 1 - y), (1 - x, 1 - y)]

    def sends():
        return [_remote(q_refs[w].at[2 * px + py], outs[w].at[k], send_sems.at[w, j], recv_sems.at[w, j], (px, py, c))
                for j, (px, py) in enumerate(chips) for w in range(n)]

    def start():
        for cp in sends():
            cp.start()

    def finish():
        for j, (px, py) in enumerate(chips):
            for w in range(n):
                slot = outs[w].at[2 * px + py]
                _remote(slot, slot, send_sems.at[w, j], recv_sems.at[w, j], (px, py, c)).wait_recv()
        for cp in sends():
            cp.wait_send()

    return start, finish


def _scatter_sems(n):
    return (pltpu.SemaphoreType.DMA((n, 3)), pltpu.SemaphoreType.DMA((n, 3)))


def scatter_rider(parts):
    return Rider(tuple(parts), tuple(jax.ShapeDtypeStruct(q.shape, q.dtype) for q in parts), {},
                 _scatter_sems(len(parts)), lambda ins, outs, sems: _scatter_plan(ins, outs, sems[0], sems[1]))


def scatter_to_chips(parts, *, name):
    n = len(parts)

    def body(*refs):
        start, finish = _scatter_plan(refs[:n], refs[n:2 * n], refs[2 * n], refs[2 * n + 1])
        start()
        finish()

    return pl.pallas_call(
        body, in_specs=[_ANY] * n, out_specs=[_ANY] * n,
        out_shape=[jax.ShapeDtypeStruct(q.shape, q.dtype) for q in parts],
        scratch_shapes=list(_scatter_sems(n)), name=name)(*parts)


def share_with_sibling(parts, *, name):
    n = len(parts)

    def body(*refs):
        r_refs, outs = refs[:n], refs[n:2 * n]
        send_sems, recv_sems = refs[2 * n:]
        x, y, c = _position()
        sends = []
        for w in range(n):
            cp = _remote(r_refs[w], outs[w], send_sems.at[w], recv_sems.at[w], (x, y, 1 - c))
            cp.start()
            sends.append(cp)
        for cp in sends:
            cp.wait_recv()
        for cp in sends:
            cp.wait_send()

    return pl.pallas_call(
        body, in_specs=[_ANY] * n, out_specs=[_ANY] * n,
        out_shape=[jax.ShapeDtypeStruct(r.shape, r.dtype) for r in parts],
        scratch_shapes=[pltpu.SemaphoreType.DMA((n,)), pltpu.SemaphoreType.DMA((n,))],
        name=name)(*parts)


def _cols_to_chips(full):
    *lead, R, C4 = full.shape
    t = full.reshape(*lead, R, 4, C4 // 4)
    return jnp.moveaxis(t, -2, 0)


def _chips_to_cols(sh):
    t = jnp.moveaxis(sh, 0, -2)
    return t.reshape(*t.shape[:-2], t.shape[-2] * t.shape[-1])


def _slot_in_empty(own, index, n):
    return lax.dynamic_update_slice(lax.empty((n,) + own.shape, own.dtype), own[None], (index,) + (0,) * own.ndim)


def _fold_pair(dg):
    return dg[0, :HEAD_DIM] + dg[0, HEAD_DIM:]


def _ffn_fwd(x, g, w_in_slab, w_out, tag):
    h = rms_fwd(x, g, name=f"{tag}_rms")
    a, b, act = swiglu_fwd(h, w_in_slab, name=f"{tag}_in")
    y = matmul(act, w_out, res=x, scale=0.5, tm=1024, tn=512, tk=w_out.shape[0], name=f"{tag}_out")
    return y, (x, h, a, b, act)


def _ffn_bwd(dy, saved, g, w_in_slab, w_out, tag):
    x, h, a, b, act = saved
    da, db = swiglu_bwd(dy, w_out, a, b, name=f"{tag}_dact")
    dw_out = grad_rows(act, dy, scale=0.5, name=f"{tag}_dwout")
    dw_in = grad_cols(h, da, db, name=f"{tag}_dwin")
    dx, dg = ffn_dh(da, db, w_in_slab, x, g, dy, name=f"{tag}_dh")
    return dx, dg[0], dw_in, dw_out


MEM_PAIRS = MEM_WIDTH // LANES


def _mem_attn_fwd(proj, mq0, mem_n, w_kv, g_q, g_k, tag):
    qh = pairnorm_fwd(proj, mq0, MEM_PAIRS, g_q, scale=QK_SCALE,name=f"{tag}_qnorm")
    kv = matmul(mem_n, w_kv, tm=256, tn=512, tk=1024, name=f"{tag}_kv")
    kh = pairnorm_fwd(kv, 0, MEM_PAIRS, g_k, name=f"{tag}_knorm")
    o, lse, _ = attn_fwd(qh, 0, kh, 0, kv, MEM_PAIRS, MEM_PAIRS, None, causal=False, name=f"{tag}_attn")
    return o, (qh, kv, kh, o, lse)


def _mem_attn_bwd(dmix, do0, proj, mq0, saved, mem_n, g_q, g_k, tag):
    qh, kv, kh, o, lse = saved
    delta = attn_delta(o, dmix, do0, MEM_PAIRS, name=f"{tag}_delta")
    (dqh, dkh, dv), _ = attn_bwd(qh, 0, kh, 0, kv, MEM_PAIRS, dmix, do0, MEM_PAIRS, lse, delta, None,
                                 causal=False, name=f"{tag}_dattn")
    dq_pre, dgq = pairnorm_bwd(proj, mq0, MEM_PAIRS, dqh, g_q, out_dtype=BF16, name=f"{tag}_dqnorm")
    dk_pre, dgk = pairnorm_bwd(kv, 0, MEM_PAIRS, dkh, g_k, name=f"{tag}_dknorm")
    dkv = jnp.concatenate([dk_pre, dv], axis=1)
    dw_kv = grad_rows(mem_n, dkv, name=f"{tag}_dwkv")
    return dq_pre, _fold_pair(dgq), _fold_pair(dgk), dw_kv, dkv


def _per_head_lanes(x, H):
    return jnp.pad(x.reshape(H, -1).T, ((0, 0), (0, LANES - H)))


def _fox_fwd(proj, b_f, g_q, g_k, tok, rider, tag):
    H, P = tok // HEAD_DIM, tok // LANES
    bias = jnp.pad(b_f.reshape(1, H), ((0, 0), (0, LANES - H)))
    qh = pairnorm_fwd(proj, 0, P, g_q, scale=QK_SCALE,name=f"{tag}_qnorm")
    kh = pairnorm_fwd(proj, P, P, g_k, name=f"{tag}_knorm")
    c = fgate_fwd(proj, 3 * P + MEM_PAIRS, bias, name=f"{tag}_fgate")
    decay = c[:, :H].T.reshape(P, 2, 1, c.shape[0])
    o, lse, rode = attn_fwd(qh, 0, kh, 0, proj, 2 * P, P, decay, causal=True, rider=rider, name=f"{tag}_attn")
    return o, (qh, kh, bias, decay, o, lse), rode


def _fox_bwd(dmix, proj, saved, g_q, g_k, tok, rider, tag):
    qh, kh, bias, decay, o, lse = saved
    H, P = tok // HEAD_DIM, tok // LANES
    delta = attn_delta(o, dmix, 0, P, name=f"{tag}_delta")
    (dqh, dkh, dv, dcs, drs), rode = attn_bwd(qh, 0, kh, 0, proj, 2 * P, dmix, 0, P, lse, delta, decay, causal=True,
                                              rider=rider, name=f"{tag}_dattn")
    dq_pre, dgq = pairnorm_bwd(proj, 0, P, dqh, g_q, out_dtype=BF16, name=f"{tag}_dqnorm")
    dk_pre, dgk = pairnorm_bwd(proj, P, P, dkh, g_k, out_dtype=BF16, name=f"{tag}_dknorm")
    dz, dbias = fgate_bwd(proj, 3 * P + MEM_PAIRS, bias, drs, _per_head_lanes(dcs, H), name=f"{tag}_dfgate")
    dqkv = jnp.concatenate([dq_pre, dk_pre, dv.astype(BF16)], axis=1)
    return dqkv, dz, dbias[0, :H], _fold_pair(dgq), _fold_pair(dgk), rode


def local_step(x, mem, target, W, comm=None):
    S, D = x.shape
    tok = D - MEM_WIDTH
    P = tok // LANES
    depth = W["norm_ffn1"].shape[0]
    mem_n = rms_fwd(mem, W["mem_norm"], name="mem_rms")
    saved = []
    for i in range(depth):
        kind, j = i % 2, i // 2
        t = f"l{i}"
        x1, s1 = _ffn_fwd(x, W["norm_ffn1"][i], W["ffn1_w_in"][i], W["ffn1_w_out"][i], f"{t}_ffn1")
        h = rms_fwd(x1, W["norm_mix"][i], name=f"{t}_mix_rms")
        w_mix = W["fox_w_in"][j] if kind == 0 else W["gmlp_w_in"][j]
        proj = matmul(h, w_mix, tm=1024, tn=896, tk=D, name=f"{t}_mix_in")
        if kind == 0:
            rider = comm.late_weights_rider() if (comm is not None and i == 0) else None
            o_tok, s_tok, rode = _fox_fwd(proj, W["fox_b_f"][j], W["fox_q_norm"][j], W["fox_k_norm"][j], tok, rider,
                                          f"{t}_fox")
            if rider is not None:
                comm.accept_late_weights(W, rode)
            mq0 = 3 * P
        else:
            vg, ws, bs = _gmlp_operands(W["gmlp_v_norm"][j], W["gmlp_w_s"][j], W["gmlp_b_s"][j])
            o_tok = gmlp_fwd(proj, P, P, vg, ws, bs, name=f"{t}_gmlp")
            s_tok = None
            mq0 = 2 * P
        o_mem, s_mem = _mem_attn_fwd(proj, mq0, mem_n, W["mem_w_kv"][i], W["mem_q_norm"][i], W["mem_k_norm"][i],
                                     f"{t}_mem")
        mix = jnp.concatenate([o_tok.astype(BF16), o_mem.astype(BF16)], axis=1)
        x2 = matmul(mix, W["w_out"][i], res=x1, tm=1024, tn=512, tk=D, name=f"{t}_mix_out")
        x3, s3 = _ffn_fwd(x2, W["norm_ffn2"][i], W["ffn2_w_in"][i], W["ffn2_w_out"][i], f"{t}_ffn2")
        saved.append((s1, x1, h, proj, mq0, s_tok, s_mem, mix, s3))
        x = x3

    dx, loss = loss_head(x, target, name="loss_head")

    G = {k: [None] * depth for k in ("norm_ffn1", "norm_mix", "norm_ffn2", "mem_q_norm", "mem_k_norm", "ffn1_w_in",
                                     "ffn1_w_out", "ffn2_w_in", "ffn2_w_out", "w_out", "mem_w_kv")}
    n_fox, n_gmlp = (depth + 1) // 2, depth // 2
    for k in ("fox_w_in", "fox_b_f", "fox_q_norm", "fox_k_norm"):
        G[k] = [None] * n_fox
    for k in ("gmlp_w_in", "gmlp_v_norm", "gmlp_w_s", "gmlp_b_s"):
        G[k] = [None] * n_gmlp
    dkv_all = [None] * depth
    for i in reversed(range(depth)):
        kind, j = i % 2, i // 2
        t = f"l{i}"
        s1, x1, h, proj, mq0, s_tok, s_mem, mix, s3 = saved[i]
        dx, G["norm_ffn2"][i], G["ffn2_w_in"][i], G["ffn2_w_out"][i] = _ffn_bwd(
            dx, s3, W["norm_ffn2"][i], W["ffn2_w_in"][i], W["ffn2_w_out"][i], f"{t}_ffn2")
        dmix = matmul(dx, W["w_out"][i], tb=True, tm=1024, tn=1024, tk=D, name=f"{t}_dmix")
        G["w_out"][i] = grad_rows(mix, dx, name=f"{t}_dwmixout")
        dmq, G["mem_q_norm"][i], G["mem_k_norm"][i], G["mem_w_kv"][i], dkv_all[i] = _mem_attn_bwd(
            dmix, P, proj, mq0, s_mem, mem_n, W["mem_q_norm"][i], W["mem_k_norm"][i], f"{t}_mem")
        if kind == 0:
            rider = comm.early_grads_rider(G) if (comm is not None and i == 0) else None
            dqkv, dz, G["fox_b_f"][j], G["fox_q_norm"][j], G["fox_k_norm"][j], rode = _fox_bwd(
                dmix, proj, s_tok, W["fox_q_norm"][j], W["fox_k_norm"][j], tok, rider, f"{t}_fox")
            if rider is not None:
                comm.accept_early_grads(rode)
            dproj = jnp.concatenate([dqkv, dmq, dz.astype(BF16)], axis=1)
            w_mix, wkey = W["fox_w_in"][j], "fox_w_in"
        else:
            vg, ws, bs = _gmlp_operands(W["gmlp_v_norm"][j], W["gmlp_w_s"][j], W["gmlp_b_s"][j])
            dup, dvp, dws, dbs, dvg = gmlp_bwd(proj, P, P, vg, ws, jnp.swapaxes(ws, 2, 3), bs, dmix,
                                               name=f"{t}_dgmlp")
            G["gmlp_w_s"][j] = dws.reshape(W["gmlp_w_s"][j].shape)
            G["gmlp_b_s"][j] = dbs.reshape(W["gmlp_b_s"][j].shape)
            G["gmlp_v_norm"][j] = dvg.reshape(-1)
            dproj = jnp.concatenate([dup.astype(BF16), dvp.astype(BF16), dmq], axis=1)
            w_mix, wkey = W["gmlp_w_in"][j], "gmlp_w_in"
        G[wkey][j] = matmul(h, dproj, ta=True, tm=1024, tn=896, tk=1024, name=f"{t}_dwmixin")
        dx, dgm = mix_dh(dproj, w_mix, x1, W["norm_mix"][i], dx, name=f"{t}_dhmix")
        G["norm_mix"][i] = dgm[0]
        dx, G["norm_ffn1"][i], G["ffn1_w_in"][i], G["ffn1_w_out"][i] = _ffn_bwd(
            dx, s1, W["norm_ffn1"][i], W["ffn1_w_in"][i], W["ffn1_w_out"][i], f"{t}_ffn1")
    w_kv_all = jnp.concatenate([W["mem_w_kv"][i] for i in range(depth)], axis=1)
    dmem_n = matmul(jnp.concatenate(dkv_all, axis=1), w_kv_all, tb=True, tm=256, tn=512, tk=1024, name="dmem_n")
    _, dmemg = rms_bwd(mem, dmem_n, W["mem_norm"], None, name="dmem_rms")
    G["mem_norm"] = [dmemg[0]]
    return loss, dx, G


def _fox_cols_to_compute(w, tok):
    H = tok // HEAD_DIM
    qkv, f, mq = w[..., :3 * tok], w[..., 3 * tok:3 * tok + H], w[..., 3 * tok + H:]
    f = jnp.pad(f, [(0, 0)] * (w.ndim - 1) + [(0, LANES - H)])
    return jnp.concatenate([qkv, mq, f], axis=-1)


def _fox_cols_from_compute(w, tok):
    H = tok // HEAD_DIM
    qkv, mq, f = w[..., :3 * tok], w[..., 3 * tok:3 * tok + MEM_WIDTH], w[..., 3 * tok + MEM_WIDTH:3 * tok + MEM_WIDTH + H]
    return jnp.concatenate([qkv, f, mq], axis=-1)


_BIG = ("ffn1_w_in", "ffn1_w_out", "ffn2_w_in", "ffn2_w_out", "w_out", "mem_w_kv", "fox_w_in", "gmlp_w_in")
_SMALL = ("norm_ffn1", "norm_mix", "norm_ffn2", "mem_norm", "mem_q_norm", "mem_k_norm", "fox_b_f", "fox_q_norm",
          "fox_k_norm", "gmlp_v_norm", "gmlp_w_s", "gmlp_b_s")
WEIGHT_ORDER = ("norm_ffn1", "ffn1_w_in", "ffn1_w_out", "norm_mix", "norm_ffn2", "ffn2_w_in", "ffn2_w_out", "w_out",
                "mem_norm", "mem_w_kv", "mem_q_norm", "mem_k_norm", "fox_w_in", "fox_b_f", "fox_q_norm", "fox_k_norm",
                "gmlp_w_in", "gmlp_v_norm", "gmlp_w_s", "gmlp_b_s")


def _small_slab(rows_list, index):
    sizes = [s.shape[0] for s in rows_list]
    n_rows = [-(-n // LANES) for n in sizes]
    small = jnp.concatenate([jnp.pad(s, (0, r * LANES - n)).reshape(r, LANES)
                             for s, n, r in zip(rows_list, sizes, n_rows)], axis=0)
    small = jnp.pad(small, ((0, -small.shape[0] % 64), (0, 0)))
    return _slot_in_empty(small, index, 8), sizes, n_rows


_FIRST_WEIGHTS = (("ffn1_w_in", 0), ("ffn1_w_out", 0), ("fox_w_in", 0))


def _weight_from_slab(name, slab, tok):
    if name in ("ffn1_w_in", "ffn2_w_in"):
        return slab
    if name == "fox_w_in":
        return _fox_cols_to_compute(_chips_to_cols(slab), tok)
    if name == "gmlp_w_in":
        return _chips_to_cols(slab)
    return slab.reshape(4 * slab.shape[1], slab.shape[2])


def _grad_to_slab(name, g, tok):
    if name == "fox_w_in":
        return _cols_to_chips(_fox_cols_from_compute(g, tok))
    if name == "gmlp_w_in":
        return _cols_to_chips(g)
    return g


class _Exchange:
    def __init__(self, shards, tok, chip, core):
        self.tok, self.core = tok, core
        self.half = core.reshape(1).astype(jnp.int32)
        self.chip_id = chip.reshape(1).astype(jnp.int32)
        items = [(k, i) for k in _BIG for i in range(shards[k].shape[0])]
        self.slabs = {it: _slot_in_empty(shards[it[0]][it[1]].astype(BF16), chip, 4) for it in items}
        self.late = [it for it in items if it not in _FIRST_WEIGHTS]
        self.reduced = {}
        self.early = None

    def first_weights(self, small_slab):
        got, small_all = gather_weights([self.slabs[it] for it in _FIRST_WEIGHTS], small_slab, name="gather_first")
        return {it: _weight_from_slab(it[0], s, self.tok) for it, s in zip(_FIRST_WEIGHTS, got)}, small_all

    def late_weights_rider(self):
        return gather_rider([self.slabs[it] for it in self.late])

    def accept_late_weights(self, W, got):
        for (k, i), s in zip(self.late, got):
            W[k][i] = _weight_from_slab(k, s, self.tok)

    def _pair_sums(self, G, items, small_slab, tag):
        parts = [_grad_to_slab(k, G[k][i], self.tok) for k, i in items]
        landed, small_all = exchange_with_sibling(parts, small_slab, name=f"grad_exchange_{tag}")
        pair = [pair_sum(p, l, self.half, name=f"grad_pair_sum_{k}{i}") for (k, i), p, l in zip(items, parts, landed)]
        return pair, small_all

    def early_grads_rider(self, G):
        items = [(k, i) for k in _BIG for i in range(len(G[k])) if G[k][i] is not None]
        pair, _ = self._pair_sums(G, items, None, "early")
        self.early = (items, pair)
        return scatter_rider(pair)

    def accept_early_grads(self, landed):
        items, pair = self.early
        self._chip_sums(items, pair, landed)

    def _chip_sums(self, items, pair, landed):
        for (k, i), q, l in zip(items, pair, landed):
            self.reduced[(k, i)] = chip_sum(q, l, self.chip_id, name=f"grad_chip_sum_{k}{i}")

    def finish_grads(self, G, small_slab):
        items = [(k, i) for k in _BIG for i in range(len(G[k])) if (k, i) not in self.reduced]
        pair, small_all = self._pair_sums(G, items, small_slab, "late")
        self._chip_sums(items, pair, scatter_to_chips(pair, name="grad_scatter_late"))
        order = sorted(self.reduced)
        other = share_with_sibling([self.reduced[it] for it in order], name="grad_share")
        full = {}
        for it, a, b in zip(order, [self.reduced[it] for it in order], other):
            full[it] = jnp.where(self.core == 0, jnp.concatenate([a, b]), jnp.concatenate([b, a]))
        names = sorted({k for k, _ in order})
        return {k: jnp.stack([full[(k, i)] for i in range(len(G[k]))]) for k in names}, small_all


def kernel(x, mem, norm_ffn1, ffn1_w_in, ffn1_w_out, norm_mix, norm_ffn2, ffn2_w_in, ffn2_w_out, w_out, mem_norm, mem_w_kv, mem_q_norm, mem_k_norm, fox_w_in, fox_b_f, fox_q_norm, fox_k_norm, gmlp_w_in, gmlp_v_norm, gmlp_w_s, gmlp_b_s, loss_target, m_norm_ffn1, m_ffn1_w_in, m_ffn1_w_out, m_norm_mix, m_norm_ffn2, m_ffn2_w_in, m_ffn2_w_out, m_w_out, m_mem_norm, m_mem_w_kv, m_mem_q_norm, m_mem_k_norm, m_fox_w_in, m_fox_b_f, m_fox_q_norm, m_fox_k_norm, m_gmlp_w_in, m_gmlp_v_norm, m_gmlp_w_s, m_gmlp_b_s, v_norm_ffn1, v_ffn1_w_in, v_ffn1_w_out, v_norm_mix, v_norm_ffn2, v_ffn2_w_in, v_ffn2_w_out, v_w_out, v_mem_norm, v_mem_w_kv, v_mem_q_norm, v_mem_k_norm, v_fox_w_in, v_fox_b_f, v_fox_q_norm, v_fox_k_norm, v_gmlp_w_in, v_gmlp_v_norm, v_gmlp_w_s, v_gmlp_b_s):
    w = dict(norm_ffn1=norm_ffn1, ffn1_w_in=ffn1_w_in, ffn1_w_out=ffn1_w_out, norm_mix=norm_mix, norm_ffn2=norm_ffn2,
             ffn2_w_in=ffn2_w_in, ffn2_w_out=ffn2_w_out, w_out=w_out, mem_norm=mem_norm, mem_w_kv=mem_w_kv,
             mem_q_norm=mem_q_norm, mem_k_norm=mem_k_norm, fox_w_in=fox_w_in, fox_b_f=fox_b_f, fox_q_norm=fox_q_norm,
             fox_k_norm=fox_k_norm, gmlp_w_in=gmlp_w_in, gmlp_v_norm=gmlp_v_norm, gmlp_w_s=gmlp_w_s, gmlp_b_s=gmlp_b_s)
    m = dict(norm_ffn1=m_norm_ffn1, ffn1_w_in=m_ffn1_w_in, ffn1_w_out=m_ffn1_w_out, norm_mix=m_norm_mix,
             norm_ffn2=m_norm_ffn2, ffn2_w_in=m_ffn2_w_in, ffn2_w_out=m_ffn2_w_out, w_out=m_w_out, mem_norm=m_mem_norm,
             mem_w_kv=m_mem_w_kv, mem_q_norm=m_mem_q_norm, mem_k_norm=m_mem_k_norm, fox_w_in=m_fox_w_in,
             fox_b_f=m_fox_b_f, fox_q_norm=m_fox_q_norm, fox_k_norm=m_fox_k_norm, gmlp_w_in=m_gmlp_w_in,
             gmlp_v_norm=m_gmlp_v_norm, gmlp_w_s=m_gmlp_w_s, gmlp_b_s=m_gmlp_b_s)
    v = dict(norm_ffn1=v_norm_ffn1, ffn1_w_in=v_ffn1_w_in, ffn1_w_out=v_ffn1_w_out, norm_mix=v_norm_mix,
             norm_ffn2=v_norm_ffn2, ffn2_w_in=v_ffn2_w_in, ffn2_w_out=v_ffn2_w_out, w_out=v_w_out, mem_norm=v_mem_norm,
             mem_w_kv=v_mem_w_kv, mem_q_norm=v_mem_q_norm, mem_k_norm=v_mem_k_norm, fox_w_in=v_fox_w_in,
             fox_b_f=v_fox_b_f, fox_q_norm=v_fox_q_norm, fox_k_norm=v_fox_k_norm, gmlp_w_in=v_gmlp_w_in,
             gmlp_v_norm=v_gmlp_v_norm, gmlp_w_s=v_gmlp_w_s, gmlp_b_s=v_gmlp_b_s)
    D = x.shape[-1]
    tok = D - MEM_WIDTH
    xi, yi, ci = _position()
    chip = 2 * xi + yi

    device = 4 * xi + 2 * yi + ci

    comm = _Exchange(w, tok, chip, ci)
    vn = w["gmlp_v_norm"]
    vn_slab, _, _ = _small_slab([vn.reshape(-1)], device)
    first, vn_all = comm.first_weights(vn_slab)
    W = {k: w[k] for k in _SMALL}
    W["gmlp_v_norm"] = _chips_to_cols(vn_all[0::2].reshape(4, -1)[:, :vn.size].reshape((4,) + vn.shape))
    for k in _BIG:
        W[k] = [first.get((k, i)) for i in range(w[k].shape[0])]

    loss, grad_x, g = local_step(x[0], mem[0], loss_target[0], W, comm)

    small_list = [jnp.stack(g[k]).reshape(-1) for k in _SMALL] + [loss.reshape(-1)]
    small, small_sizes, small_rows = _small_slab(small_list, device)
    red, small_all = comm.finish_grads(g, small)
    small_sum = ordered_sum(small_all, name="small_sum")
    off = 0
    for k, n, r in zip(_SMALL, small_sizes, small_rows):
        red[k] = small_sum[off:off + r].reshape(-1)[:n].reshape((-1,) + w[k].shape[1:] if k != "gmlp_v_norm"
                                                                else (w[k].shape[0], -1))
        off += r
    loss_total = small_sum[off, 0]
    vn_cols = w["gmlp_v_norm"].shape[-1]
    red["gmlp_v_norm"] = lax.dynamic_slice_in_dim(red["gmlp_v_norm"], chip * vn_cols, vn_cols, axis=-1)

    deltas, new_m, new_v = {}, {}, {}
    for k in WEIGHT_ORDER:
        wk = w[k] if w[k].ndim > 1 else w[k].reshape(1, -1)
        upd = adamw(wk, red[k].reshape(wk.shape), m[k].reshape(wk.shape), v[k].reshape(wk.shape), name=f"adamw_{k}")
        deltas[k], new_m[k], new_v[k] = (u.reshape(w[k].shape) for u in upd)
    return (loss_total, grad_x[None], *[red[k].reshape(w[k].shape) for k in WEIGHT_ORDER],
            *[deltas[k] for k in WEIGHT_ORDER], *[new_m[k] for k in WEIGHT_ORDER], *[new_v[k] for k in WEIGHT_ORDER])
```

```python
import functools
import math
from typing import Callable, NamedTuple

import jax
import jax.numpy as jnp
from jax import lax
from jax.experimental import pallas as pl
from jax.experimental.pallas import tpu as pltpu

F32 = jnp.float32
BF16 = jnp.bfloat16
EPS = 1e-6
HEAD_DIM = 64
MEM_WIDTH = 256
CHUNK = 128
LANES = 128
NEG = -1e30
VMEM_LIMIT_BYTES = 56 * 1024 * 1024
ATTN_Q_BLOCK = 1024
ATTN_K_BLOCK = 1024
ATTN_ROW_CHUNK = 1024
QK_SCALE = 0.125
MESH_ID = pl.DeviceIdType.MESH

ADAM_LR = 0.001
ADAM_B1 = 0.9
ADAM_B2 = 0.999
ADAM_EPS = 1e-08
ADAM_WD = 0.01
ADAM_STEP = 10


def _tile(n, pref, align):
    t = (min(pref, n) // align) * align
    while t >= align:
        if n % t == 0:
            return t
        t -= align
    return n


def _params(sem):
    return pltpu.CompilerParams(dimension_semantics=sem, vmem_limit_bytes=VMEM_LIMIT_BYTES)


def _dot(a, b, ca, cb):
    return lax.dot_general(a, b, (((ca,), (cb,)), ((), ())), preferred_element_type=F32)


def _sigmoid(x):
    return 1.0 / (1.0 + jnp.exp(-x))


_GELU_C = math.sqrt(2.0 / math.pi)


def _gelu(x):
    return 0.5 * x * (1.0 + jnp.tanh(_GELU_C * (x + 0.044715 * (x * x * x))))


def _gelu_grad(x):
    t = jnp.tanh(_GELU_C * (x + 0.044715 * (x * x * x)))
    return 0.5 * (1.0 + t) + 0.5 * x * (1.0 - t * t) * (_GELU_C * (1.0 + 3.0 * 0.044715 * (x * x)))


def matmul(a, b, *, ta=False, tb=False, out_dtype=F32, scale=None, res=None,
           tm=1024, tn=512, tk=1024, name):
    if ta:
        K, M = a.shape
    else:
        M, K = a.shape
    N = b.shape[0] if tb else b.shape[1]
    tm = _tile(M, tm, LANES if ta else 16)
    tn = _tile(N, tn, LANES)
    tk = _tile(K, tk, LANES)
    nk = K // tk
    a_spec = pl.BlockSpec((tk, tm), lambda i, j, k: (k, i)) if ta else pl.BlockSpec((tm, tk), lambda i, j, k: (i, k))
    b_spec = pl.BlockSpec((tn, tk), lambda i, j, k: (j, k)) if tb else pl.BlockSpec((tk, tn), lambda i, j, k: (k, j))
    o_spec = pl.BlockSpec((tm, tn), lambda i, j, k: (i, j))
    ca, cb = (0 if ta else 1), (1 if tb else 0)
    has_res = res is not None

    def body(*refs):
        a_ref, b_ref = refs[0], refs[1]
        res_ref = refs[2] if has_res else None
        o_ref = refs[3] if has_res else refs[2]
        acc_ref = refs[-1]
        k = pl.program_id(2)
        prod = _dot(a_ref[...].astype(BF16), b_ref[...].astype(BF16), ca, cb)

        def finish(acc):
            if scale is not None:
                acc = acc * scale
            if has_res:
                acc = res_ref[...] + acc
            o_ref[...] = acc.astype(out_dtype)

        if nk == 1:
            finish(prod)
        else:
            @pl.when(k == 0)
            def _():
                acc_ref[...] = prod

            @pl.when(k > 0)
            def _():
                acc_ref[...] += prod

            @pl.when(k == nk - 1)
            def _():
                finish(acc_ref[...])

    in_specs = [a_spec, b_spec] + ([o_spec] if has_res else [])
    args = (a, b) + ((res,) if has_res else ())
    return pl.pallas_call(
        body, grid=(M // tm, N // tn, nk), in_specs=in_specs, out_specs=o_spec,
        out_shape=jax.ShapeDtypeStruct((M, N), out_dtype),
        scratch_shapes=[pltpu.VMEM((tm, tn) if nk > 1 else (8, LANES), F32)],
        compiler_params=_params(("parallel", "parallel", "arbitrary")), name=name)(*args)


def swiglu_fwd(h, w_slab, *, name):
    S, D = h.shape
    Fc = w_slab.shape[-1]
    tm = _tile(S, 1024, 16)

    def body(h_ref, wa_ref, wb_ref, a_ref, b_ref, act_ref):
        hv = h_ref[...]
        a = _dot(hv, wa_ref[...], 1, 0)
        b = _dot(hv, wb_ref[...], 1, 0)
        a_ref[...] = a.astype(BF16)
        b_ref[...] = b.astype(BF16)
        act_ref[...] = (a * _sigmoid(a) * b).astype(BF16)

    out = pl.BlockSpec((tm, Fc), lambda j, i: (i, j))
    return pl.pallas_call(
        body, grid=(2, S // tm),
        in_specs=[pl.BlockSpec((tm, D), lambda j, i: (i, 0)),
                  pl.BlockSpec((None, D, Fc), lambda j, i: (j, 0, 0)),
                  pl.BlockSpec((None, D, Fc), lambda j, i: (j + 2, 0, 0))],
        out_specs=[out, out, out],
        out_shape=[jax.ShapeDtypeStruct((S, 2 * Fc), BF16)] * 3,
        compiler_params=_params(("parallel", "parallel")), name=name)(h, w_slab, w_slab)


def swiglu_bwd(dy, w_out, a, b, *, name):
    S, D = dy.shape
    F = w_out.shape[0]
    fc = F // 2
    tm = _tile(S, 512, 16)

    def body(dy_ref, w_ref, a_ref, b_ref, da_ref, db_ref):
        dact = 0.5 * _dot(dy_ref[...].astype(BF16), w_ref[...], 1, 1)
        av = a_ref[...].astype(F32)
        sg = _sigmoid(av)
        da_ref[...] = (dact * b_ref[...].astype(F32) * (sg * (1.0 + av * (1.0 - sg)))).astype(BF16)
        db_ref[...] = (dact * (av * sg)).astype(BF16)

    blk = pl.BlockSpec((tm, fc), lambda j, i: (i, j))
    return pl.pallas_call(
        body, grid=(2, S // tm),
        in_specs=[pl.BlockSpec((tm, D), lambda j, i: (i, 0)), pl.BlockSpec((fc, D), lambda j, i: (j, 0)), blk, blk],
        out_specs=[blk, blk],
        out_shape=[jax.ShapeDtypeStruct((S, F), BF16), jax.ShapeDtypeStruct((S, F), BF16)],
        compiler_params=_params(("parallel", "parallel")), name=name)(dy, w_out, a, b)


def ffn_dh(da, db, w_slab, x, g, dy, *, name):
    S, F = da.shape
    D, Fc = w_slab.shape[-2:]
    tm = _tile(S, 1024, 16)
    sub = _tile(tm, 256, 8)

    def body(da_ref, db_ref, w_ref, x_ref, g_ref, dy_ref, dx_ref, dg_ref, acc_ref):
        i, k = pl.program_id(0), pl.program_id(1)

        @pl.when(k == 0)
        def _():
            acc_ref[...] = jnp.zeros_like(acc_ref)

        @pl.when(k < 2)
        def _():
            acc_ref[...] += _dot(da_ref[...], w_ref[...], 1, 1)

        @pl.when(k >= 2)
        def _():
            acc_ref[...] += _dot(db_ref[...], w_ref[...], 1, 1)

        @pl.when(k == 3)
        def _():
            part = None
            for c in range(tm // sub):
                rows = pl.ds(c * sub, sub)
                xv, dh = x_ref[rows, :], acc_ref[rows, :]
                r = lax.rsqrt(jnp.mean(xv * xv, axis=-1, keepdims=True) + EPS)
                u = dh * g_ref[...]
                dx_ref[rows, :] = dy_ref[rows, :] + (r * u - xv * (r * r * r) * jnp.mean(xv * u, axis=-1, keepdims=True))
                p = jnp.sum(dh * xv * r, axis=0, keepdims=True)
                part = p if part is None else part + p

            @pl.when(i == 0)
            def _():
                dg_ref[...] = part

            @pl.when(i > 0)
            def _():
                dg_ref[...] += part

    row = pl.BlockSpec((tm, D), lambda i, k: (i, 0))
    vec = pl.BlockSpec((1, D), lambda i, k: (0, 0))
    return pl.pallas_call(
        body, grid=(S // tm, 4),
        in_specs=[pl.BlockSpec((tm, Fc), lambda i, k: (i, jnp.minimum(k, 1))),
                  pl.BlockSpec((tm, Fc), lambda i, k: (i, jnp.maximum(k - 2, 0))),
                  pl.BlockSpec((None, D, Fc), lambda i, k: (k, 0, 0)), row, vec, row],
        out_specs=[row, vec],
        out_shape=[jax.ShapeDtypeStruct((S, D), F32), jax.ShapeDtypeStruct((1, D), F32)],
        scratch_shapes=[pltpu.VMEM((tm, D), F32)],
        compiler_params=_params(("arbitrary", "arbitrary")), name=name)(da, db, w_slab, x, g.reshape(1, D), dy)


def mix_dh(dproj, w, x, g, dy, *, name):
    S, N = dproj.shape
    D = w.shape[0]
    tm = _tile(S, 1024, 16)
    sub = _tile(tm, 256, 8)
    tk = _tile(N, 896, LANES)
    nk = N // tk

    def body(p_ref, w_ref, x_ref, g_ref, dy_ref, dx_ref, dg_ref, acc_ref):
        i, k = pl.program_id(0), pl.program_id(1)
        prod = _dot(p_ref[...], w_ref[...], 1, 1)

        @pl.when(k == 0)
        def _():
            acc_ref[...] = prod

        @pl.when(k > 0)
        def _():
            acc_ref[...] += prod

        @pl.when(k == nk - 1)
        def _():
            part = None
            for c in range(tm // sub):
                rows = pl.ds(c * sub, sub)
                xv, dh = x_ref[rows, :], acc_ref[rows, :]
                r = lax.rsqrt(jnp.mean(xv * xv, axis=-1, keepdims=True) + EPS)
                u = dh * g_ref[...]
                dx_ref[rows, :] = dy_ref[rows, :] + (r * u - xv * (r * r * r) * jnp.mean(xv * u, axis=-1, keepdims=True))
                pp = jnp.sum(dh * xv * r, axis=0, keepdims=True)
                part = pp if part is None else part + pp

            @pl.when(i == 0)
            def _():
                dg_ref[...] = part

            @pl.when(i > 0)
            def _():
                dg_ref[...] += part

    row = pl.BlockSpec((tm, D), lambda i, k: (i, 0))
    vec = pl.BlockSpec((1, D), lambda i, k: (0, 0))
    return pl.pallas_call(
        body, grid=(S // tm, nk),
        in_specs=[pl.BlockSpec((tm, tk), lambda i, k: (i, k)), pl.BlockSpec((D, tk), lambda i, k: (0, k)), row, vec, row],
        out_specs=[row, vec],
        out_shape=[jax.ShapeDtypeStruct((S, D), F32), jax.ShapeDtypeStruct((1, D), F32)],
        scratch_shapes=[pltpu.VMEM((tm, D), F32)],
        compiler_params=_params(("arbitrary", "arbitrary")), name=name)(dproj, w, x, g.reshape(1, D), dy)


def grad_cols(h, da, db, *, name):
    S, D = h.shape
    Fc = da.shape[1] // 2
    tk = _tile(S, 1024, 16)
    nk = S // tk

    def body(h_ref, da_ref, db_ref, o_ref, acc_ref):
        ch, k = pl.program_id(0), pl.program_id(1)

        @pl.when(k == 0)
        def _():
            acc_ref[...] = jnp.zeros_like(acc_ref)

        @pl.when(ch < 2)
        def _():
            acc_ref[...] += _dot(h_ref[...], da_ref[...], 0, 0)

        @pl.when(ch >= 2)
        def _():
            acc_ref[...] += _dot(h_ref[...], db_ref[...], 0, 0)

        @pl.when(k == nk - 1)
        def _():
            o_ref[...] = acc_ref[...]

    return pl.pallas_call(
        body, grid=(4, nk),
        in_specs=[pl.BlockSpec((tk, D), lambda ch, k: (k, 0)),
                  pl.BlockSpec((tk, Fc), lambda ch, k: (jnp.where(ch < 2, k, 0), jnp.minimum(ch, 1))),
                  pl.BlockSpec((tk, Fc), lambda ch, k: (jnp.where(ch >= 2, k, 0), jnp.maximum(ch - 2, 0)))],
        out_specs=pl.BlockSpec((None, D, Fc), lambda ch, k: (ch, 0, 0)),
        out_shape=jax.ShapeDtypeStruct((4, D, Fc), F32),
        scratch_shapes=[pltpu.VMEM((D, Fc), F32)],
        compiler_params=_params(("parallel", "arbitrary")), name=name)(h, da, db)


def grad_rows(a, b, *, scale=None, name):
    S, M = a.shape
    N = b.shape[1]
    R = M // 4
    tn = _tile(N, 512, LANES)
    tk = _tile(S, 1024, 16)
    nk = S // tk

    def body(a_ref, b_ref, o_ref, acc_ref):
        k = pl.program_id(1)

        @pl.when(k == 0)
        def _():
            acc_ref[...] = jnp.zeros_like(acc_ref)

        acc_ref[...] += _dot(a_ref[...].astype(BF16), b_ref[...].astype(BF16), 0, 0)

        @pl.when(k == nk - 1)
        def _():
            for d in range(4):
                part = acc_ref[d * R:(d + 1) * R, :]
                o_ref[d] = part if scale is None else part * scale

    return pl.pallas_call(
        body, grid=(N // tn, nk),
        in_specs=[pl.BlockSpec((tk, M), lambda j, k: (k, 0)), pl.BlockSpec((tk, tn), lambda j, k: (k, j))],
        out_specs=pl.BlockSpec((4, R, tn), lambda j, k: (0, 0, j)),
        out_shape=jax.ShapeDtypeStruct((4, R, N), F32),
        scratch_shapes=[pltpu.VMEM((M, tn), F32)],
        compiler_params=_params(("parallel", "arbitrary")), name=name)(a, b)


def rms_fwd(x, g, *, name):
    S, D = x.shape
    ts = _tile(S, 1024, 16)

    def body(x_ref, g_ref, h_ref):
        xv = x_ref[...]
        r = lax.rsqrt(jnp.mean(xv * xv, axis=-1, keepdims=True) + EPS)
        h_ref[...] = (xv * r * g_ref[...]).astype(BF16)

    return pl.pallas_call(
        body, grid=(S // ts,),
        in_specs=[pl.BlockSpec((ts, D), lambda i: (i, 0)), pl.BlockSpec((1, D), lambda i: (0, 0))],
        out_specs=pl.BlockSpec((ts, D), lambda i: (i, 0)),
        out_shape=jax.ShapeDtypeStruct((S, D), BF16),
        compiler_params=_params(("parallel",)), name=name)(x, g.reshape(1, D))


def rms_bwd(x, dh, g, res, *, name):
    S, D = x.shape
    ts = _tile(S, 512, 16)
    has_res = res is not None

    def body(*refs):
        x_ref, dh_ref, g_ref = refs[:3]
        res_ref = refs[3] if has_res else None
        dx_ref, dg_ref = refs[-2:]
        i = pl.program_id(0)
        xv, dhv = x_ref[...], dh_ref[...].astype(F32)
        r = lax.rsqrt(jnp.mean(xv * xv, axis=-1, keepdims=True) + EPS)
        u = dhv * g_ref[...]
        dx = r * u - xv * (r * r * r) * jnp.mean(xv * u, axis=-1, keepdims=True)
        if has_res:
            dx = res_ref[...] + dx
        dx_ref[...] = dx
        part = jnp.sum(dhv * xv * r, axis=0, keepdims=True)

        @pl.when(i == 0)
        def _():
            dg_ref[...] = part

        @pl.when(i > 0)
        def _():
            dg_ref[...] += part

    row = pl.BlockSpec((ts, D), lambda i: (i, 0))
    vec = pl.BlockSpec((1, D), lambda i: (0, 0))
    args = (x, dh, g.reshape(1, D)) + ((res,) if has_res else ())
    return pl.pallas_call(
        body, grid=(S // ts,), in_specs=[row, row, vec] + ([row] if has_res else []),
        out_specs=[row, vec],
        out_shape=[jax.ShapeDtypeStruct((S, D), F32), jax.ShapeDtypeStruct((1, D), F32)],
        compiler_params=_params(("arbitrary",)), name=name)(*args)


def _low_half(shape):
    return lax.broadcasted_iota(jnp.int32, shape, len(shape) - 1) < HEAD_DIM


def _half_sums(x, low):
    sa = jnp.sum(jnp.where(low, x, 0.0), axis=1, keepdims=True)
    sb = jnp.sum(jnp.where(low, 0.0, x), axis=1, keepdims=True)
    return jnp.where(low, sa, sb)


def pairnorm_fwd(x, col0, n_pairs, g, *, scale=None, name):
    S = x.shape[0]
    ts = _tile(S, 512, 16)
    W = n_pairs * LANES
    assert col0 % n_pairs == 0

    def body(x_ref, g_ref, o_ref):
        for p in range(n_pairs):
            cols = pl.ds(p * LANES, LANES)
            xv = x_ref[:, cols]
            r = lax.rsqrt(_half_sums(xv * xv, _low_half(xv.shape)) * (1.0 / HEAD_DIM) + EPS)
            y = xv * r * g_ref[...]
            o_ref[:, cols] = (y if scale is None else y * scale).astype(BF16)

    return pl.pallas_call(
        body, grid=(S // ts,),
        in_specs=[pl.BlockSpec((ts, W), lambda i: (i, col0 // n_pairs)), pl.BlockSpec((1, LANES), lambda i: (0, 0))],
        out_specs=pl.BlockSpec((ts, W), lambda i: (i, 0)),
        out_shape=jax.ShapeDtypeStruct((S, W), BF16),
        compiler_params=_params(("parallel",)), name=name)(x, jnp.tile(g.reshape(1, HEAD_DIM), (1, 2)))


def pairnorm_bwd(x, col0, n_pairs, dy, g, *, out_dtype=F32, name):
    S = x.shape[0]
    ts = _tile(S, 512, 16)
    W = n_pairs * LANES
    assert col0 % n_pairs == 0

    def body(x_ref, dy_ref, g_ref, dx_ref, dg_ref):
        part = None
        for p in range(n_pairs):
            cols = pl.ds(p * LANES, LANES)
            xv, dyv = x_ref[:, cols], dy_ref[:, cols]
            low = _low_half(xv.shape)
            r = lax.rsqrt(_half_sums(xv * xv, low) * (1.0 / HEAD_DIM) + EPS)
            u = dyv * g_ref[...]
            dx = r * u - xv * (r * r * r) * (_half_sums(xv * u, low) * (1.0 / HEAD_DIM))
            dx_ref[:, cols] = dx.astype(out_dtype)
            pp = jnp.sum(dyv * xv * r, axis=0, keepdims=True)
            part = pp if part is None else part + pp

        @pl.when(pl.program_id(0) == 0)
        def _():
            dg_ref[...] = part

        @pl.when(pl.program_id(0) > 0)
        def _():
            dg_ref[...] += part

    vec = pl.BlockSpec((1, LANES), lambda i: (0, 0))
    blk = pl.BlockSpec((ts, W), lambda i: (i, 0))
    return pl.pallas_call(
        body, grid=(S // ts,),
        in_specs=[pl.BlockSpec((ts, W), lambda i: (i, col0 // n_pairs)), blk, vec], out_specs=[blk, vec],
        out_shape=[jax.ShapeDtypeStruct((S, W), out_dtype), jax.ShapeDtypeStruct((1, LANES), F32)],
        compiler_params=_params(("arbitrary",)), name=name)(x, dy, jnp.tile(g.reshape(1, HEAD_DIM), (1, 2)))


def _split3(x):
    x1 = x.astype(BF16)
    r1 = x - x1.astype(F32)
    x2 = r1.astype(BF16)
    x3 = (r1 - x2.astype(F32)).astype(BF16)
    return x1, x2, x3


def _tri_ones(n, lower):
    r = lax.broadcasted_iota(jnp.int32, (n, n), 0)
    c = lax.broadcasted_iota(jnp.int32, (n, n), 1)
    return jnp.where((c <= r) if lower else (c >= r), 1.0, 0.0).astype(BF16)


def fgate_fwd(z, col0, bias, *, name):
    S, L = z.shape[0], LANES
    tb = _tile(S, 512, 16)

    def body(z_ref, b_ref, c_ref, carry):
        i = pl.program_id(0)

        @pl.when(i == 0)
        def _():
            carry[...] = jnp.zeros_like(carry)

        zz = z_ref[...] + b_ref[...]
        lf = jnp.minimum(zz, 0.0) - jnp.log(1.0 + jnp.exp(-jnp.abs(zz)))
        tri = _tri_ones(tb, True)
        x1, x2, x3 = _split3(lf)
        c = (_dot(tri, x1, 1, 0) + _dot(tri, x2, 1, 0)) + _dot(tri, x3, 1, 0) + carry[...]
        c_ref[...] = c
        carry[...] += jnp.sum(lf, axis=0, keepdims=True)

    return pl.pallas_call(
        body, grid=(S // tb,),
        in_specs=[pl.BlockSpec((tb, L), lambda i: (i, col0)), pl.BlockSpec((1, L), lambda i: (0, 0))],
        out_specs=pl.BlockSpec((tb, L), lambda i: (i, 0)),
        out_shape=jax.ShapeDtypeStruct((S, L), F32),
        scratch_shapes=[pltpu.VMEM((1, L), F32)],
        compiler_params=_params(("arbitrary",)), name=name)(z, bias)


def fgate_bwd(z, col0, bias, drs, dcs, *, name):
    S, L = z.shape[0], LANES
    n_pairs = drs.shape[0]
    tb = _tile(S, 512, 16)
    nb = S // tb

    def body(z_ref, b_ref, drs_ref, dcs_ref, dz_ref, db_ref, carry):
        i = pl.program_id(0)

        @pl.when(i == 0)
        def _():
            carry[...] = jnp.zeros_like(carry)

        tri = _tri_ones(tb, False)
        lane = lax.broadcasted_iota(jnp.int32, (tb, L), 1)
        dc = -dcs_ref[...]
        for h in range(2 * n_pairs):
            dc = dc + jnp.where(lane == h, jnp.sum(drs_ref[h // 2, h % 2], axis=1, keepdims=True), 0.0)
        x1, x2, x3 = _split3(dc)
        dlf = (_dot(tri, x1, 1, 0) + _dot(tri, x2, 1, 0)) + _dot(tri, x3, 1, 0) + carry[...]
        carry[...] += jnp.sum(dc, axis=0, keepdims=True)
        dz = dlf * _sigmoid(-(z_ref[...] + b_ref[...]))
        dz_ref[...] = dz
        part = jnp.sum(dz, axis=0, keepdims=True)

        @pl.when(i == 0)
        def _():
            db_ref[...] = part

        @pl.when(i > 0)
        def _():
            db_ref[...] += part

    rev = pl.BlockSpec((tb, L), lambda i: (nb - 1 - i, 0))
    vec = pl.BlockSpec((1, L), lambda i: (0, 0))
    return pl.pallas_call(
        body, grid=(nb,),
        in_specs=[pl.BlockSpec((tb, L), lambda i: (nb - 1 - i, col0)), vec,
                  pl.BlockSpec((n_pairs, 2, tb, L), lambda i: (0, 0, nb - 1 - i, 0)), rev],
        out_specs=[rev, vec],
        out_shape=[jax.ShapeDtypeStruct((S, L), F32), jax.ShapeDtypeStruct((1, L), F32)],
        scratch_shapes=[pltpu.VMEM((1, L), F32)],
        compiler_params=_params(("arbitrary",)), name=name)(z, bias, drs, dcs)


def _one_head(x, low, a):
    return jnp.where(low if a == 0 else jnp.logical_not(low), x, jnp.zeros_like(x))


class Rider(NamedTuple):
    inputs: tuple
    out_shapes: tuple
    aliases: dict
    sems: tuple
    plan: Callable


def _with_rider(rider, n_in, n_out, n_scratch):
    if rider is None:
        return [], [], [], [], {}, lambda refs: (refs[:n_in], refs[n_in:n_in + n_out], refs[n_in + n_out:], None)
    e_in, e_out = len(rider.inputs), len(rider.out_shapes)

    def split(refs):
        ins, r_in = refs[:n_in], refs[n_in:n_in + e_in]
        o0 = n_in + e_in
        outs, r_out = refs[o0:o0 + n_out], refs[o0 + n_out:o0 + n_out + e_out]
        s0 = o0 + n_out + e_out
        return ins, outs, refs[s0:s0 + n_scratch], rider.plan(r_in, r_out, refs[s0 + n_scratch:])

    aliases = {n_in + a: n_out + b for a, b in rider.aliases.items()}
    return list(rider.inputs), [_ANY] * e_in, list(rider.out_shapes), [_ANY] * e_out, aliases, split


def attn_fwd(q, q0, k, k0, v, v0, n_pairs, decay, *, causal, rider=None, name):
    Sq, Sk = q.shape[0], k.shape[0]
    tq = _tile(Sq, ATTN_Q_BLOCK if causal else 4 * ATTN_Q_BLOCK, LANES)
    tk = _tile(Sk, ATTN_K_BLOCK, LANES)
    nq, nk = Sq // tq, Sk // tk
    bias = decay is not None
    rs = _tile(tq, ATTN_ROW_CHUNK, 16)
    r_args, r_in_specs, r_shapes, r_out_specs, aliases, split = _with_rider(rider, 4 if bias else 3, 2, 4)

    def row_sum_lanes(acc, low, a):
        other = jnp.logical_not(low) if a == 0 else low
        return jnp.max(jnp.where(other, acc, 0.0), axis=1, keepdims=True)

    live = [(i, j) for i in range(nq) for j in range(nk) if not causal or j * tk <= i * tq + tq - 1]
    n_live = len(live)

    def body(i_tab, j_tab, *refs):
        ins, (o_ref, lse_ref), scratch, ride = split(refs)
        m_sc, acc_sc = scratch[:2], scratch[2:]
        q_ref, k_ref, v_ref = ins[:3]
        ck_ref = ins[3] if bias else None
        pr, t = pl.program_id(0), pl.program_id(1)
        i, j = i_tab[t], j_tab[t]
        last_j = (i * tq + tq - 1) // tk if causal else nk - 1
        if ride is not None:
            pl.when(jnp.logical_and(pr == 0, t == 0))(ride[0])

        @pl.when(j == 0)
        def _():
            for a in range(2):
                m_sc[a][...] = jnp.full_like(m_sc[a], NEG)
                acc_sc[a][...] = jnp.zeros_like(acc_sc[a])

        def compute(masked):
            kv, vv = k_ref[...], v_ref[...].astype(BF16)
            low_k = _low_half(kv.shape)
            va = [jnp.where(low_k if a == 0 else jnp.logical_not(low_k), vv, jnp.ones_like(vv)) for a in range(2)]
            for r in range(tq // rs):
                rows = pl.ds(r * rs, rs)
                qv = q_ref[rows, :]
                low = _low_half(qv.shape)
                for a in range(2):
                    s = _dot(_one_head(qv, low, a), kv, 1, 1)
                    if bias:
                        s = s - ck_ref[a]
                    if masked:
                        row = i * tq + r * rs + lax.broadcasted_iota(jnp.int32, (rs, tk), 0)
                        col = j * tk + lax.broadcasted_iota(jnp.int32, (rs, tk), 1)
                        s = jnp.where(col <= row, s, NEG)
                    m_prev = m_sc[a][rows, :]
                    m_new = jnp.maximum(m_prev, jnp.max(s, axis=1, keepdims=True))
                    alpha = jnp.exp(m_prev - m_new)
                    p = jnp.exp(s - m_new).astype(BF16)
                    acc_sc[a][rows, :] = alpha * acc_sc[a][rows, :] + _dot(p, va[a], 1, 0)
                    m_sc[a][rows, :] = m_new

        if causal:
            crosses = j * tk + (tk - 1) > i * tq
            pl.when(crosses)(functools.partial(compute, True))
            pl.when(jnp.logical_not(crosses))(functools.partial(compute, False))
        else:
            compute(False)

        @pl.when(j == last_j)
        def _():
            low = _low_half((tq, LANES))
            l = [row_sum_lanes(acc_sc[a][...], low, a) for a in range(2)]
            o_ref[...] = jnp.where(low, acc_sc[0][...] / l[0], acc_sc[1][...] / l[1])
            for a in range(2):
                lse_ref[a] = m_sc[a][...] + jnp.log(l[a])

        if ride is not None:
            pl.when(jnp.logical_and(pr == n_pairs - 1, t == n_live - 1))(ride[1])

    in_specs = [pl.BlockSpec((tq, LANES), lambda p, t, it, jt: (it[t], q0 + p)),
                pl.BlockSpec((tk, LANES), lambda p, t, it, jt: (jt[t], k0 + p)),
                pl.BlockSpec((tk, LANES), lambda p, t, it, jt: (jt[t], v0 + p))]
    args = [q, k, v]
    if bias:
        in_specs.append(pl.BlockSpec((None, 2, 1, tk), lambda p, t, it, jt: (p, 0, 0, jt[t])))
        args.append(decay)
    tabs = [jnp.asarray([b[n] for b in live], jnp.int32) for n in range(2)]
    out = pl.pallas_call(
        body,
        grid_spec=pltpu.PrefetchScalarGridSpec(
            num_scalar_prefetch=2, grid=(n_pairs, n_live), in_specs=in_specs + r_in_specs,
            out_specs=[pl.BlockSpec((tq, LANES), lambda p, t, it, jt: (it[t], p)),
                       pl.BlockSpec((None, 2, tq, 1), lambda p, t, it, jt: (p, 0, it[t], 0))] + r_out_specs,
            scratch_shapes=[pltpu.VMEM((tq, 1), F32)] * 2 + [pltpu.VMEM((tq, LANES), F32)] * 2
            + (list(rider.sems) if rider else [])),
        out_shape=[jax.ShapeDtypeStruct((Sq, n_pairs * LANES), F32),
                   jax.ShapeDtypeStruct((n_pairs, 2, Sq, 1), F32)] + r_shapes,
        input_output_aliases={2 + a: b for a, b in aliases.items()},
        compiler_params=_params(("arbitrary", "arbitrary") if rider else ("parallel", "arbitrary")),
        name=name)(*tabs, *args, *r_args)
    return out[0], out[1], out[2:]


def attn_delta(o, do, do0, n_pairs, *, name):
    S = o.shape[0]
    ts = _tile(S, 512, 16)
    W = n_pairs * LANES
    assert do0 % n_pairs == 0

    def body(o_ref, do_ref, out_ref):
        for p in range(n_pairs):
            cols = pl.ds(p * LANES, LANES)
            prod = o_ref[:, cols] * do_ref[:, cols]
            low = _low_half(prod.shape)
            out_ref[p, 0] = jnp.sum(jnp.where(low, prod, 0.0), axis=1, keepdims=True)
            out_ref[p, 1] = jnp.sum(jnp.where(low, 0.0, prod), axis=1, keepdims=True)

    return pl.pallas_call(
        body, grid=(S // ts,),
        in_specs=[pl.BlockSpec((ts, W), lambda i: (i, 0)), pl.BlockSpec((ts, W), lambda i: (i, do0 // n_pairs))],
        out_specs=pl.BlockSpec((n_pairs, 2, ts, 1), lambda i: (0, 0, i, 0)),
        out_shape=jax.ShapeDtypeStruct((n_pairs, 2, S, 1), F32),
        compiler_params=_params(("parallel",)), name=name)(o, do)


def attn_bwd(q, q0, k, k0, v, v0, do, do0, n_pairs, lse, delta, decay, *, causal, rider=None, name):
    Sq, Sk = q.shape[0], k.shape[0]
    tq = _tile(Sq, ATTN_Q_BLOCK if causal else 4 * ATTN_Q_BLOCK, LANES)
    tk = _tile(Sk, ATTN_K_BLOCK, LANES)
    nq, nk = Sq // tq, Sk // tk
    bias = decay is not None

    rs = _tile(tq, ATTN_ROW_CHUNK, 16)
    r_args, r_in_specs, r_shapes, r_out_specs, aliases, split = _with_rider(
        rider, 7 if bias else 6, 5 if bias else 3, 0)

    live = [(i, j) for j in range(nk) for i in range(nq) if not causal or j * tk <= i * tq + tq - 1]
    n_live = len(live)

    def body(i_tab, j_tab, *refs):
        ins, outs, _, ride = split(refs)
        q_ref, k_ref, v_ref, do_ref, lse_ref, dl_ref = ins[:6]
        ck_ref = ins[6] if bias else None
        dq_ref, dk_ref, dv_ref = outs[:3]
        dcs_ref, drs_ref = (outs[3], outs[4]) if bias else (None, None)
        pr, t = pl.program_id(0), pl.program_id(1)
        i, j = i_tab[t], j_tab[t]
        first_i = (j * tk) // tq if causal else 0
        if ride is not None:
            pl.when(jnp.logical_and(pr == 0, t == 0))(ride[0])

        @pl.when(i == first_i)
        def _():
            dk_ref[...] = jnp.zeros_like(dk_ref)
            dv_ref[...] = jnp.zeros_like(dv_ref)
            if bias:
                dcs_ref[...] = jnp.zeros_like(dcs_ref)

        def compute(masked):
            kv, vv = k_ref[...], v_ref[...].astype(BF16)
            low_k = _low_half(kv.shape)
            ka = [_one_head(kv, low_k, a) for a in range(2)]
            for r in range(tq // rs):
                here = pl.ds(r * rs, rs)
                rows = pl.ds(pl.multiple_of(i * tq + r * rs, rs), rs)
                qv, dov = q_ref[here, :], do_ref[here, :].astype(BF16)
                low = _low_half(qv.shape)
                dq_part, dk_part, dv_part, row_parts, col_parts = None, None, None, [], []
                for a in range(2):
                    qa, doa = _one_head(qv, low, a), _one_head(dov, low, a)
                    s = _dot(qa, kv, 1, 1)
                    if bias:
                        s = s - ck_ref[a]
                    p = jnp.exp(s - lse_ref[a, here])
                    if masked:
                        row = i * tq + r * rs + lax.broadcasted_iota(jnp.int32, (rs, tk), 0)
                        col = j * tk + lax.broadcasted_iota(jnp.int32, (rs, tk), 1)
                        p = jnp.where(col <= row, p, 0.0)
                    dv_a = _dot(p.astype(BF16), doa, 0, 0)
                    dp = _dot(doa, vv, 1, 1)
                    ds = p * (dp - dl_ref[a, here])
                    dsb = ds.astype(BF16)
                    dk_a = _dot(dsb, qa, 0, 0)
                    if bias:
                        col_parts.append(jnp.sum(ds, axis=0, keepdims=True))
                        lanes = ds[:, :LANES]
                        for c in range(1, tk // LANES):
                            lanes = lanes + ds[:, c * LANES:(c + 1) * LANES]
                        row_parts.append(lanes)
                    part = _dot(dsb, ka[a], 1, 0) * QK_SCALE
                    dq_part = part if dq_part is None else dq_part + part
                    dk_part = dk_a if dk_part is None else dk_part + dk_a
                    dv_part = dv_a if dv_part is None else dv_part + dv_a
                dv_ref[...] += dv_part
                dk_ref[...] += dk_part
                for a, cp in enumerate(col_parts):
                    dcs_ref[a] += cp

                @pl.when(j == 0)
                def _(rows=rows, dq_part=dq_part, row_parts=row_parts):
                    dq_ref[rows, :] = dq_part
                    for a, rp in enumerate(row_parts):
                        drs_ref[a, rows, :] = rp

                @pl.when(j > 0)
                def _(rows=rows, dq_part=dq_part, row_parts=row_parts):
                    dq_ref[rows, :] += dq_part
                    for a, rp in enumerate(row_parts):
                        drs_ref[a, rows, :] += rp

        if causal:
            crosses = j * tk + (tk - 1) > i * tq
            pl.when(crosses)(functools.partial(compute, True))
            pl.when(jnp.logical_not(crosses))(functools.partial(compute, False))
        else:
            compute(False)

        if ride is not None:
            pl.when(jnp.logical_and(pr == n_pairs - 1, t == n_live - 1))(ride[1])

    col1 = pl.BlockSpec((None, 2, tq, 1), lambda p, t, it, jt: (p, 0, it[t], 0))
    in_specs = [pl.BlockSpec((tq, LANES), lambda p, t, it, jt: (it[t], q0 + p)),
                pl.BlockSpec((tk, LANES), lambda p, t, it, jt: (jt[t], k0 + p)),
                pl.BlockSpec((tk, LANES), lambda p, t, it, jt: (jt[t], v0 + p)),
                pl.BlockSpec((tq, LANES), lambda p, t, it, jt: (it[t], do0 + p)), col1, col1]
    args = [q, k, v, do, lse, delta]
    kout = pl.BlockSpec((tk, LANES), lambda p, t, it, jt: (jt[t], p))
    out_specs = [pl.BlockSpec((Sq, LANES), lambda p, t, it, jt: (0, p)), kout, kout]
    out_shape = [jax.ShapeDtypeStruct((Sq, n_pairs * LANES), F32), jax.ShapeDtypeStruct((Sk, n_pairs * LANES), F32),
                 jax.ShapeDtypeStruct((Sk, n_pairs * LANES), F32)]
    if bias:
        in_specs.append(pl.BlockSpec((None, 2, 1, tk), lambda p, t, it, jt: (p, 0, 0, jt[t])))
        args.append(decay)
        out_specs += [pl.BlockSpec((None, 2, 1, tk), lambda p, t, it, jt: (p, 0, 0, jt[t])),
                      pl.BlockSpec((None, 2, Sq, LANES), lambda p, t, it, jt: (p, 0, 0, 0))]
        out_shape += [jax.ShapeDtypeStruct((n_pairs, 2, 1, Sk), F32),
                      jax.ShapeDtypeStruct((n_pairs, 2, Sq, LANES), F32)]
    n_own = len(out_shape)
    tabs = [jnp.asarray([b[n] for b in live], jnp.int32) for n in range(2)]
    out = pl.pallas_call(
        body,
        grid_spec=pltpu.PrefetchScalarGridSpec(
            num_scalar_prefetch=2, grid=(n_pairs, n_live), in_specs=in_specs + r_in_specs,
            out_specs=out_specs + r_out_specs, scratch_shapes=list(rider.sems) if rider else []),
        out_shape=out_shape + r_shapes,
        input_output_aliases={2 + a: b for a, b in aliases.items()},
        compiler_params=_params(("arbitrary", "arbitrary") if rider else ("parallel", "arbitrary")),
        name=name)(*tabs, *args, *r_args)
    return tuple(out[:n_own]), out[n_own:]


def _tril_mask(n):
    r = lax.broadcasted_iota(jnp.int32, (n, n), 0)
    c = lax.broadcasted_iota(jnp.int32, (n, n), 1)
    return c <= r


def _gmlp_operands(v_gain, w_s, b_s):
    G = w_s.shape[0]
    return (v_gain.reshape(G // 2, 1, LANES), w_s.reshape(G // 2, 2, CHUNK, CHUNK), b_s.reshape(G // 2, 2, CHUNK, 1))


def _gmlp_gate(wt, vh, b_ref, low):
    gate = _dot(wt[0], _one_head(vh, low, 0), 1, 0) + _dot(wt[1], _one_head(vh, low, 1), 1, 0)
    return gate + jnp.where(low, b_ref[0], b_ref[1])


def gmlp_fwd(proj, v0, n_pairs, vg, w, b, *, name):
    S = proj.shape[0]
    ts = _tile(S, 1024, CHUNK)

    def body(up_ref, vp_ref, vg_ref, w_ref, b_ref, o_ref):
        mask = _tril_mask(CHUNK)
        wt = [jnp.where(mask, w_ref[a], 0.0).astype(BF16) for a in range(2)]
        low = _low_half((CHUNK, LANES))
        for c in range(ts // CHUNK):
            sl = pl.ds(c * CHUNK, CHUNK)
            vz = _gelu(vp_ref[sl, :])
            r = lax.rsqrt(_half_sums(vz * vz, low) * (1.0 / HEAD_DIM) + EPS)
            vh = (vz * r * vg_ref[...]).astype(BF16)
            o_ref[sl, :] = _gelu(up_ref[sl, :]) * _gmlp_gate(wt, vh, b_ref, low)

    return pl.pallas_call(
        body, grid=(n_pairs, S // ts),
        in_specs=[pl.BlockSpec((ts, LANES), lambda p, i: (i, p)), pl.BlockSpec((ts, LANES), lambda p, i: (i, v0 + p)),
                  pl.BlockSpec((None, 1, LANES), lambda p, i: (p, 0, 0)),
                  pl.BlockSpec((None, 2, CHUNK, CHUNK), lambda p, i: (p, 0, 0, 0)),
                  pl.BlockSpec((None, 2, CHUNK, 1), lambda p, i: (p, 0, 0, 0))],
        out_specs=pl.BlockSpec((ts, LANES), lambda p, i: (i, p)),
        out_shape=jax.ShapeDtypeStruct((S, n_pairs * LANES), F32),
        compiler_params=_params(("parallel", "parallel")), name=name)(proj, proj, vg, w, b)


def gmlp_bwd(proj, v0, n_pairs, vg, w, wT, b, do, *, name):
    S = proj.shape[0]
    ts = _tile(S, 1024, CHUNK)

    def body(up_ref, vp_ref, vg_ref, w_ref, wT_ref, b_ref, do_ref, dup_ref, dvp_ref, dw_ref, db_ref, dvg_ref):
        i = pl.program_id(1)

        @pl.when(i == 0)
        def _():
            dw_ref[...] = jnp.zeros_like(dw_ref)
            db_ref[...] = jnp.zeros_like(db_ref)
            dvg_ref[...] = jnp.zeros_like(dvg_ref)

        mask = _tril_mask(CHUNK)
        wt = [jnp.where(mask, w_ref[a], 0.0).astype(BF16) for a in range(2)]
        wtT = [jnp.where(mask.T, wT_ref[a], 0.0).astype(BF16) for a in range(2)]
        low = _low_half((CHUNK, LANES))
        vgain = vg_ref[...]
        for c in range(ts // CHUNK):
            sl = pl.ds(c * CHUNK, CHUNK)
            u_pre, v_pre, dout = up_ref[sl, :], vp_ref[sl, :], do_ref[sl, :]
            vz = _gelu(v_pre)
            r = lax.rsqrt(_half_sums(vz * vz, low) * (1.0 / HEAD_DIM) + EPS)
            vh = (vz * r * vgain).astype(BF16)
            gate = _gmlp_gate(wt, vh, b_ref, low)
            dgate = dout * _gelu(u_pre)
            dup_ref[sl, :] = dout * gate * _gelu_grad(u_pre)
            dvh = None
            for a in range(2):
                dga = _one_head(dgate, low, a)
                dgb = dga.astype(BF16)
                dw_ref[a] += jnp.where(mask, _dot(dgb, vh, 1, 1), 0.0)
                db_ref[a] += jnp.sum(dga, axis=1, keepdims=True)
                part = _dot(wtT[a], dgb, 1, 0)
                dvh = part if dvh is None else dvh + part
            dvg_ref[...] += jnp.sum(dvh * vz * r, axis=0, keepdims=True)
            t = dvh * vgain
            dvz = r * t - vz * (r * r * r) * (_half_sums(vz * t, low) * (1.0 / HEAD_DIM))
            dvp_ref[sl, :] = dvz * _gelu_grad(v_pre)

    ublk = pl.BlockSpec((ts, LANES), lambda p, i: (i, p))
    wblk = pl.BlockSpec((None, 2, CHUNK, CHUNK), lambda p, i: (p, 0, 0, 0))
    bblk = pl.BlockSpec((None, 2, CHUNK, 1), lambda p, i: (p, 0, 0, 0))
    gblk = pl.BlockSpec((None, 1, LANES), lambda p, i: (p, 0, 0))
    return pl.pallas_call(
        body, grid=(n_pairs, S // ts),
        in_specs=[ublk, pl.BlockSpec((ts, LANES), lambda p, i: (i, v0 + p)), gblk, wblk, wblk, bblk, ublk],
        out_specs=[ublk, ublk, wblk, bblk, gblk],
        out_shape=[jax.ShapeDtypeStruct((S, n_pairs * LANES), F32), jax.ShapeDtypeStruct((S, n_pairs * LANES), F32),
                   jax.ShapeDtypeStruct((n_pairs, 2, CHUNK, CHUNK), F32), jax.ShapeDtypeStruct((n_pairs, 2, CHUNK, 1), F32),
                   jax.ShapeDtypeStruct((n_pairs, 1, LANES), F32)],
        compiler_params=_params(("parallel", "arbitrary")), name=name)(proj, proj, vg, w, wT, b, do)


def loss_head(y, target, *, name):
    S, D = y.shape
    ts = _tile(S, 512, 8)

    def body(y_ref, t_ref, dy_ref, loss_ref):
        i = pl.program_id(0)
        e = y_ref[...] - t_ref[...]
        dy_ref[...] = e * (1.0 / D)
        part = jnp.sum(jnp.sum(e * e, axis=1, keepdims=True), axis=0, keepdims=True) * (0.5 / D)

        @pl.when(i == 0)
        def _():
            loss_ref[...] = part

        @pl.when(i > 0)
        def _():
            loss_ref[...] += part

    row = pl.BlockSpec((ts, D), lambda i: (i, 0))
    return pl.pallas_call(
        body, grid=(S // ts,), in_specs=[row, row],
        out_specs=[row, pl.BlockSpec((1, 1), lambda i: (0, 0))],
        out_shape=[jax.ShapeDtypeStruct((S, D), F32), jax.ShapeDtypeStruct((1, 1), F32)],
        compiler_params=_params(("arbitrary",)), name=name)(y, target)


def adamw(w, g, m, v, *, name):
    shape = w.shape
    C = shape[-1]
    R = w.size // C
    tr = _tile(R, max(8, (256 * 1024) // C // 8 * 8), 8)

    def body(w_ref, g_ref, m_ref, v_ref, d_ref, nm_ref, nv_ref):
        gv = g_ref[...]
        nm = ADAM_B1 * m_ref[...] + (1.0 - ADAM_B1) * gv
        nv = ADAM_B2 * v_ref[...] + (1.0 - ADAM_B2) * (gv * gv)
        m_hat = nm / (1.0 - ADAM_B1 ** ADAM_STEP)
        v_hat = nv / (1.0 - ADAM_B2 ** ADAM_STEP)
        d_ref[...] = -ADAM_LR * (m_hat / (jnp.sqrt(v_hat) + ADAM_EPS) + ADAM_WD * w_ref[...])
        nm_ref[...] = nm
        nv_ref[...] = nv

    blk = pl.BlockSpec((tr, C), lambda i: (i, 0))
    out = pl.pallas_call(
        body, grid=(R // tr,), in_specs=[blk] * 4, out_specs=[blk] * 3,
        out_shape=[jax.ShapeDtypeStruct((R, C), F32)] * 3,
        compiler_params=_params(("parallel",)), name=name)(*(a.reshape(R, C) for a in (w, g, m, v)))
    return tuple(o.reshape(shape) for o in out)


def pair_sum(p, landed, half, *, name):
    n, R, C = landed.shape
    tr = _tile(R, 512, 16)
    nr = R // tr

    def body(half_ref, p_ref, l_ref, o_ref):
        o_ref[...] = (p_ref[...] + l_ref[...]).astype(BF16)

    return pl.pallas_call(
        body,
        grid_spec=pltpu.PrefetchScalarGridSpec(
            num_scalar_prefetch=1, grid=(n, nr),
            in_specs=[pl.BlockSpec((None, tr, C), lambda k, r, half_ref: (k, half_ref[0] * nr + r, 0)),
                      pl.BlockSpec((None, tr, C), lambda k, r, half_ref: (k, r, 0))],
            out_specs=pl.BlockSpec((None, tr, C), lambda k, r, half_ref: (k, r, 0))),
        out_shape=jax.ShapeDtypeStruct((n, R, C), BF16),
        compiler_params=_params(("parallel", "parallel")), name=name)(half, p, landed)


def chip_sum(own, landed, chip, *, name):
    n, R, C = own.shape
    tr = _tile(R, 512, 16)

    def body(chip_ref, own_ref, *rest):
        l_refs, o_ref = rest[:n], rest[n]
        me = chip_ref[0]
        acc = None
        for d in range(n):
            term = jnp.where(me == d, own_ref[...], l_refs[d][...]).astype(F32)
            acc = term if acc is None else acc + term
        o_ref[...] = acc

    def landed_spec(d):
        return pl.BlockSpec((None, tr, C), lambda r, chip_ref: (jnp.where(chip_ref[0] == d, (d + 1) % n, d), r, 0))

    return pl.pallas_call(
        body,
        grid_spec=pltpu.PrefetchScalarGridSpec(
            num_scalar_prefetch=1, grid=(R // tr,),
            in_specs=[pl.BlockSpec((None, tr, C), lambda r, chip_ref: (chip_ref[0], r, 0))]
            + [landed_spec(d) for d in range(n)],
            out_specs=pl.BlockSpec((tr, C), lambda r, chip_ref: (r, 0))),
        out_shape=jax.ShapeDtypeStruct((R, C), F32),
        compiler_params=_params(("parallel",)), name=name)(chip, own, *([landed] * n))


def ordered_sum(parts, *, name):
    n, R, C = parts.shape
    tr = _tile(R, 512, 16)

    def body(p_ref, o_ref):
        acc = p_ref[0].astype(F32)
        for d in range(1, n):
            acc = acc + p_ref[d].astype(F32)
        o_ref[...] = acc

    return pl.pallas_call(
        body, grid=(R // tr,), in_specs=[pl.BlockSpec((n, tr, C), lambda r: (0, r, 0))],
        out_specs=pl.BlockSpec((tr, C), lambda r: (r, 0)),
        out_shape=jax.ShapeDtypeStruct((R, C), F32),
        compiler_params=_params(("parallel",)), name=name)(parts)


_ANY = pl.BlockSpec(memory_space=pl.ANY)


def _position():
    return lax.axis_index("x"), lax.axis_index("y"), lax.axis_index("c")


def _remote(src, dst, send_sem, recv_sem, device):
    return pltpu.make_async_remote_copy(src_ref=src, dst_ref=dst, send_sem=send_sem, recv_sem=recv_sem,
                                        device_id=device, device_id_type=MESH_ID)


def _small_all_gather(s_ref, all_ref, send_sems, recv_sems, x, y, c):
    me = 4 * x + 2 * y + c
    copies = []
    for f in range(1, 8):
        peer = ((1 - x) if f & 4 else x, (1 - y) if f & 2 else y, (1 - c) if f & 1 else c)
        cp = _remote(s_ref, all_ref.at[me], send_sems.at[f - 1], recv_sems.at[f - 1], peer)
        cp.start()
        copies.append((cp, peer, f - 1))

    def finish():
        for cp, peer, s in copies:
            slot = all_ref.at[4 * peer[0] + 2 * peer[1] + peer[2]]
            _remote(slot, slot, send_sems.at[s], recv_sems.at[s], peer).wait_recv()
        for cp, _, _ in copies:
            cp.wait_send()

    return finish


def _core_rows(ref, core):
    h = ref.shape[1] // 2
    return pl.ds(core * h, h)


def _gather_plan(outs, send_sems, recv_sems):
    n = len(outs)
    x, y, c = _position()
    k = 2 * x + y
    sibling = (x, y, 1 - c)
    chips = [(1 - x, y), (x, 1 - y), (1 - x, 1 - y)]

    def first():
        return [_remote(outs[w].at[k, _core_rows(outs[w], c)], outs[w].at[k, _core_rows(outs[w], c)],
                        send_sems.at[w, j], recv_sems.at[w, j], (px, py, c))
                for j, (px, py) in enumerate(chips) for w in range(n)]

    def start():
        for cp in first():
            cp.start()

    def finish():
        passed = []
        for j, (px, py) in enumerate(chips):
            for w in range(n):
                slot = outs[w].at[2 * px + py, _core_rows(outs[w], c)]
                _remote(slot, slot, send_sems.at[w, j], recv_sems.at[w, j], (px, py, c)).wait_recv()
                cp = _remote(slot, slot, send_sems.at[w, 3 + j], recv_sems.at[w, 3 + j], sibling)
                cp.start()
                passed.append(cp)
        for j, (px, py) in enumerate(chips):
            for w in range(n):
                slot = outs[w].at[2 * px + py, _core_rows(outs[w], 1 - c)]
                _remote(slot, slot, send_sems.at[w, 3 + j], recv_sems.at[w, 3 + j], sibling).wait_recv()
        for cp in first() + passed:
            cp.wait_send()

    return start, finish


def _gather_sems(n):
    return (pltpu.SemaphoreType.DMA((n, 6)), pltpu.SemaphoreType.DMA((n, 6)))


def gather_rider(slabs):
    return Rider(tuple(slabs), tuple(jax.ShapeDtypeStruct(a.shape, a.dtype) for a in slabs),
                 {i: i for i in range(len(slabs))}, _gather_sems(len(slabs)),
                 lambda ins, outs, sems: _gather_plan(outs, sems[0], sems[1]))


def gather_weights(slabs, small_slab, *, name):
    n = len(slabs)

    def body(*refs):
        outs, all_ref = refs[n + 1:2 * n + 1], refs[2 * n + 1]
        send_sems, recv_sems, s_send, s_recv = refs[2 * n + 2:]
        x, y, c = _position()
        finish_small = _small_all_gather(all_ref.at[4 * x + 2 * y + c], all_ref, s_send, s_recv, x, y, c)
        start, finish = _gather_plan(outs, send_sems, recv_sems)
        start()
        finish()
        finish_small()

    args = list(slabs) + [small_slab]
    out = pl.pallas_call(
        body, in_specs=[_ANY] * (n + 1), out_specs=[_ANY] * (n + 1),
        out_shape=[jax.ShapeDtypeStruct(a.shape, a.dtype) for a in args],
        input_output_aliases={i: i for i in range(n + 1)},
        scratch_shapes=list(_gather_sems(n)) + [pltpu.SemaphoreType.DMA((7,)), pltpu.SemaphoreType.DMA((7,))],
        name=name)(*args)
    return out[:n], out[n]


def exchange_with_sibling(parts, small_slab, *, name):
    n = len(parts)
    has_small = small_slab is not None
    n_arg = n + (1 if has_small else 0)

    def body(*refs):
        p_refs = refs[:n]
        lands = refs[n_arg:n_arg + n]
        send_sems, recv_sems = refs[2 * n_arg], refs[2 * n_arg + 1]
        x, y, c = _position()
        sibling = (x, y, 1 - c)
        if has_small:
            all_ref = refs[n_arg + n]
            finish_small = _small_all_gather(all_ref.at[4 * x + 2 * y + c], all_ref, refs[2 * n_arg + 2],
                                             refs[2 * n_arg + 3], x, y, c)
        sends = []
        for w in range(n):
            for d in range(4):
                cp = _remote(p_refs[w].at[d, _core_rows(p_refs[w], 1 - c)], lands[w].at[d],
                             send_sems.at[w, d], recv_sems.at[w, d], sibling)
                cp.start()
                sends.append(cp)
        for cp in sends:
            cp.wait_recv()
        for cp in sends:
            cp.wait_send()
        if has_small:
            finish_small()

    small_args = [small_slab] if has_small else []
    out = pl.pallas_call(
        body, in_specs=[_ANY] * n_arg, out_specs=[_ANY] * n_arg,
        out_shape=[jax.ShapeDtypeStruct((4, p.shape[1] // 2, p.shape[2]), p.dtype) for p in parts]
        + [jax.ShapeDtypeStruct(s.shape, s.dtype) for s in small_args],
        input_output_aliases={n: n} if has_small else {},
        scratch_shapes=[pltpu.SemaphoreType.DMA((n, 4)), pltpu.SemaphoreType.DMA((n, 4))]
        + ([pltpu.SemaphoreType.DMA((7,)), pltpu.SemaphoreType.DMA((7,))] if has_small else []),
        name=name)(*parts, *small_args)
    return out[:n], (out[n] if has_small else None)


def _scatter_plan(q_refs, outs, send_sems, recv_sems):
    n = len(q_refs)
    x, y, c = _position()
    k = 2 * x + y
    chips = [(1 - x, y), (x, 1 - y), (1 - x, 1 - y)]

    def sends():
        return [_remote(q_refs[w].at[2 * px + py], outs[w].at[k], send_sems.at[w, j], recv_sems.at[w, j], (px, py, c))
                for j, (px, py) in enumerate(chips) for w in range(n)]

    def start():
        for cp in sends():
            cp.start()

    def finish():
        for j, (px, py) in enumerate(chips):
            for w in range(n):
                slot = outs[w].at[2 * px + py]
                _remote(slot, slot, send_sems.at[w, j], recv_sems.at[w, j], (px, py, c)).wait_recv()
        for cp in sends():
            cp.wait_send()

    return start, finish


def _scatter_sems(n):
    return (pltpu.SemaphoreType.DMA((n, 3)), pltpu.SemaphoreType.DMA((n, 3)))


def scatter_rider(parts):
    return Rider(tuple(parts), tuple(jax.ShapeDtypeStruct(q.shape, q.dtype) for q in parts), {},
                 _scatter_sems(len(parts)), lambda ins, outs, sems: _scatter_plan(ins, outs, sems[0], sems[1]))


def scatter_to_chips(parts, *, name):
    n = len(parts)

    def body(*refs):
        start, finish = _scatter_plan(refs[:n], refs[n:2 * n], refs[2 * n], refs[2 * n + 1])
        start()
        finish()

    return pl.pallas_call(
        body, in_specs=[_ANY] * n, out_specs=[_ANY] * n,
        out_shape=[jax.ShapeDtypeStruct(q.shape, q.dtype) for q in parts],
        scratch_shapes=list(_scatter_sems(n)), name=name)(*parts)


def share_with_sibling(parts, *, name):
    n = len(parts)

    def body(*refs):
        r_refs, outs = refs[:n], refs[n:2 * n]
        send_sems, recv_sems = refs[2 * n:]
        x, y, c = _position()
        sends = []
        for w in range(n):
            cp = _remote(r_refs[w], outs[w], send_sems.at[w], recv_sems.at[w], (x, y, 1 - c))
            cp.start()
            sends.append(cp)
        for cp in sends:
            cp.wait_recv()
        for cp in sends:
            cp.wait_send()

    return pl.pallas_call(
        body, in_specs=[_ANY] * n, out_specs=[_ANY] * n,
        out_shape=[jax.ShapeDtypeStruct(r.shape, r.dtype) for r in parts],
        scratch_shapes=[pltpu.SemaphoreType.DMA((n,)), pltpu.SemaphoreType.DMA((n,))],
        name=name)(*parts)


def _cols_to_chips(full):
    *lead, R, C4 = full.shape
    t = full.reshape(*lead, R, 4, C4 // 4)
    return jnp.moveaxis(t, -2, 0)


def _chips_to_cols(sh):
    t = jnp.moveaxis(sh, 0, -2)
    return t.reshape(*t.shape[:-2], t.shape[-2] * t.shape[-1])


def _slot_in_empty(own, index, n):
    return lax.dynamic_update_slice(lax.empty((n,) + own.shape, own.dtype), own[None], (index,) + (0,) * own.ndim)


def _fold_pair(dg):
    return dg[0, :HEAD_DIM] + dg[0, HEAD_DIM:]


def _ffn_fwd(x, g, w_in_slab, w_out, tag):
    h = rms_fwd(x, g, name=f"{tag}_rms")
    a, b, act = swiglu_fwd(h, w_in_slab, name=f"{tag}_in")
    y = matmul(act, w_out, res=x, scale=0.5, tm=1024, tn=1024, tk=w_out.shape[0], name=f"{tag}_out")
    return y, (x, h, a, b, act)


def _ffn_bwd(dy, saved, g, w_in_slab, w_out, tag):
    x, h, a, b, act = saved
    da, db = swiglu_bwd(dy, w_out, a, b, name=f"{tag}_dact")
    dw_out = grad_rows(act, dy, scale=0.5, name=f"{tag}_dwout")
    dw_in = grad_cols(h, da, db, name=f"{tag}_dwin")
    dx, dg = ffn_dh(da, db, w_in_slab, x, g, dy, name=f"{tag}_dh")
    return dx, dg[0], dw_in, dw_out


MEM_PAIRS = MEM_WIDTH // LANES


def _mem_attn_fwd(proj, mq0, mem_n, w_kv, g_q, g_k, tag):
    qh = pairnorm_fwd(proj, mq0, MEM_PAIRS, g_q, scale=QK_SCALE,name=f"{tag}_qnorm")
    kv = matmul(mem_n, w_kv, tm=256, tn=512, tk=1024, name=f"{tag}_kv")
    kh = pairnorm_fwd(kv, 0, MEM_PAIRS, g_k, name=f"{tag}_knorm")
    o, lse, _ = attn_fwd(qh, 0, kh, 0, kv, MEM_PAIRS, MEM_PAIRS, None, causal=False, name=f"{tag}_attn")
    return o, (qh, kv, kh, o, lse)


def _mem_attn_bwd(dmix, do0, proj, mq0, saved, mem_n, g_q, g_k, tag):
    qh, kv, kh, o, lse = saved
    delta = attn_delta(o, dmix, do0, MEM_PAIRS, name=f"{tag}_delta")
    (dqh, dkh, dv), _ = attn_bwd(qh, 0, kh, 0, kv, MEM_PAIRS, dmix, do0, MEM_PAIRS, lse, delta, None,
                                 causal=False, name=f"{tag}_dattn")
    dq_pre, dgq = pairnorm_bwd(proj, mq0, MEM_PAIRS, dqh, g_q, out_dtype=BF16, name=f"{tag}_dqnorm")
    dk_pre, dgk = pairnorm_bwd(kv, 0, MEM_PAIRS, dkh, g_k, name=f"{tag}_dknorm")
    dkv = jnp.concatenate([dk_pre, dv], axis=1)
    dw_kv = grad_rows(mem_n, dkv, name=f"{tag}_dwkv")
    return dq_pre, _fold_pair(dgq), _fold_pair(dgk), dw_kv, dkv


def _per_head_lanes(x, H):
    return jnp.pad(x.reshape(H, -1).T, ((0, 0), (0, LANES - H)))


def _fox_fwd(proj, b_f, g_q, g_k, tok, rider, tag):
    H, P = tok // HEAD_DIM, tok // LANES
    bias = jnp.pad(b_f.reshape(1, H), ((0, 0), (0, LANES - H)))
    qh = pairnorm_fwd(proj, 0, P, g_q, scale=QK_SCALE,name=f"{tag}_qnorm")
    kh = pairnorm_fwd(proj, P, P, g_k, name=f"{tag}_knorm")
    c = fgate_fwd(proj, 3 * P + MEM_PAIRS, bias, name=f"{tag}_fgate")
    decay = c[:, :H].T.reshape(P, 2, 1, c.shape[0])
    o, lse, rode = attn_fwd(qh, 0, kh, 0, proj, 2 * P, P, decay, causal=True, rider=rider, name=f"{tag}_attn")
    return o, (qh, kh, bias, decay, o, lse), rode


def _fox_bwd(dmix, proj, saved, g_q, g_k, tok, rider, tag):
    qh, kh, bias, decay, o, lse = saved
    H, P = tok // HEAD_DIM, tok // LANES
    delta = attn_delta(o, dmix, 0, P, name=f"{tag}_delta")
    (dqh, dkh, dv, dcs, drs), rode = attn_bwd(qh, 0, kh, 0, proj, 2 * P, dmix, 0, P, lse, delta, decay, causal=True,
                                              rider=rider, name=f"{tag}_dattn")
    dq_pre, dgq = pairnorm_bwd(proj, 0, P, dqh, g_q, out_dtype=BF16, name=f"{tag}_dqnorm")
    dk_pre, dgk = pairnorm_bwd(proj, P, P, dkh, g_k, out_dtype=BF16, name=f"{tag}_dknorm")
    dz, dbias = fgate_bwd(proj, 3 * P + MEM_PAIRS, bias, drs, _per_head_lanes(dcs, H), name=f"{tag}_dfgate")
    dqkv = jnp.concatenate([dq_pre, dk_pre, dv.astype(BF16)], axis=1)
    return dqkv, dz, dbias[0, :H], _fold_pair(dgq), _fold_pair(dgk), rode


def local_step(x, mem, target, W, comm=None):
    S, D = x.shape
    tok = D - MEM_WIDTH
    P = tok // LANES
    depth = W["norm_ffn1"].shape[0]
    mem_n = rms_fwd(mem, W["mem_norm"], name="mem_rms")
    saved = []
    for i in range(depth):
        kind, j = i % 2, i // 2
        t = f"l{i}"
        x1, s1 = _ffn_fwd(x, W["norm_ffn1"][i], W["ffn1_w_in"][i], W["ffn1_w_out"][i], f"{t}_ffn1")
        h = rms_fwd(x1, W["norm_mix"][i], name=f"{t}_mix_rms")
        w_mix = W["fox_w_in"][j] if kind == 0 else W["gmlp_w_in"][j]
        proj = matmul(h, w_mix, tm=1024, tn=896, tk=D, name=f"{t}_mix_in")
        if kind == 0:
            rider = comm.late_weights_rider() if (comm is not None and i == 0) else None
            o_tok, s_tok, rode = _fox_fwd(proj, W["fox_b_f"][j], W["fox_q_norm"][j], W["fox_k_norm"][j], tok, rider,
                                          f"{t}_fox")
            if rider is not None:
                comm.accept_late_weights(W, rode)
            mq0 = 3 * P
        else:
            vg, ws, bs = _gmlp_operands(W["gmlp_v_norm"][j], W["gmlp_w_s"][j], W["gmlp_b_s"][j])
            o_tok = gmlp_fwd(proj, P, P, vg, ws, bs, name=f"{t}_gmlp")
            s_tok = None
            mq0 = 2 * P
        o_mem, s_mem = _mem_attn_fwd(proj, mq0, mem_n, W["mem_w_kv"][i], W["mem_q_norm"][i], W["mem_k_norm"][i],
                                     f"{t}_mem")
        mix = jnp.concatenate([o_tok.astype(BF16), o_mem.astype(BF16)], axis=1)
        x2 = matmul(mix, W["w_out"][i], res=x1, tm=1024, tn=512, tk=D, name=f"{t}_mix_out")
        x3, s3 = _ffn_fwd(x2, W["norm_ffn2"][i], W["ffn2_w_in"][i], W["ffn2_w_out"][i], f"{t}_ffn2")
        saved.append((s1, x1, h, proj, mq0, s_tok, s_mem, mix, s3))
        x = x3

    dx, loss = loss_head(x, target, name="loss_head")

    G = {k: [None] * depth for k in ("norm_ffn1", "norm_mix", "norm_ffn2", "mem_q_norm", "mem_k_norm", "ffn1_w_in",
                                     "ffn1_w_out", "ffn2_w_in", "ffn2_w_out", "w_out", "mem_w_kv")}
    n_fox, n_gmlp = (depth + 1) // 2, depth // 2
    for k in ("fox_w_in", "fox_b_f", "fox_q_norm", "fox_k_norm"):
        G[k] = [None] * n_fox
    for k in ("gmlp_w_in", "gmlp_v_norm", "gmlp_w_s", "gmlp_b_s"):
        G[k] = [None] * n_gmlp
    dkv_all = [None] * depth
    for i in reversed(range(depth)):
        kind, j = i % 2, i // 2
        t = f"l{i}"
        s1, x1, h, proj, mq0, s_tok, s_mem, mix, s3 = saved[i]
        dx, G["norm_ffn2"][i], G["ffn2_w_in"][i], G["ffn2_w_out"][i] = _ffn_bwd(
            dx, s3, W["norm_ffn2"][i], W["ffn2_w_in"][i], W["ffn2_w_out"][i], f"{t}_ffn2")
        dmix = matmul(dx, W["w_out"][i], tb=True, tm=1024, tn=1024, tk=D, name=f"{t}_dmix")
        G["w_out"][i] = grad_rows(mix, dx, name=f"{t}_dwmixout")
        dmq, G["mem_q_norm"][i], G["mem_k_norm"][i], G["mem_w_kv"][i], dkv_all[i] = _mem_attn_bwd(
            dmix, P, proj, mq0, s_mem, mem_n, W["mem_q_norm"][i], W["mem_k_norm"][i], f"{t}_mem")
        if kind == 0:
            rider = comm.early_grads_rider(G) if (comm is not None and i == 0) else None
            dqkv, dz, G["fox_b_f"][j], G["fox_q_norm"][j], G["fox_k_norm"][j], rode = _fox_bwd(
                dmix, proj, s_tok, W["fox_q_norm"][j], W["fox_k_norm"][j], tok, rider, f"{t}_fox")
            if rider is not None:
                comm.accept_early_grads(rode)
            dproj = jnp.concatenate([dqkv, dmq, dz.astype(BF16)], axis=1)
            w_mix, wkey = W["fox_w_in"][j], "fox_w_in"
        else:
            vg, ws, bs = _gmlp_operands(W["gmlp_v_norm"][j], W["gmlp_w_s"][j], W["gmlp_b_s"][j])
            dup, dvp, dws, dbs, dvg = gmlp_bwd(proj, P, P, vg, ws, jnp.swapaxes(ws, 2, 3), bs, dmix,
                                               name=f"{t}_dgmlp")
            G["gmlp_w_s"][j] = dws.reshape(W["gmlp_w_s"][j].shape)
            G["gmlp_b_s"][j] = dbs.reshape(W["gmlp_b_s"][j].shape)
            G["gmlp_v_norm"][j] = dvg.reshape(-1)
            dproj = jnp.concatenate([dup.astype(BF16), dvp.astype(BF16), dmq], axis=1)
            w_mix, wkey = W["gmlp_w_in"][j], "gmlp_w_in"
        G[wkey][j] = matmul(h, dproj, ta=True, tm=1024, tn=896, tk=1024, name=f"{t}_dwmixin")
        dx, dgm = mix_dh(dproj, w_mix, x1, W["norm_mix"][i], dx, name=f"{t}_dhmix")
        G["norm_mix"][i] = dgm[0]
        dx, G["norm_ffn1"][i], G["ffn1_w_in"][i], G["ffn1_w_out"][i] = _ffn_bwd(
            dx, s1, W["norm_ffn1"][i], W["ffn1_w_in"][i], W["ffn1_w_out"][i], f"{t}_ffn1")
    w_kv_all = jnp.concatenate([W["mem_w_kv"][i] for i in range(depth)], axis=1)
    dmem_n = matmul(jnp.concatenate(dkv_all, axis=1), w_kv_all, tb=True, tm=256, tn=512, tk=1024, name="dmem_n")
    _, dmemg = rms_bwd(mem, dmem_n, W["mem_norm"], None, name="dmem_rms")
    G["mem_norm"] = [dmemg[0]]
    return loss, dx, G


def _fox_cols_to_compute(w, tok):
    H = tok // HEAD_DIM
    qkv, f, mq = w[..., :3 * tok], w[..., 3 * tok:3 * tok + H], w[..., 3 * tok + H:]
    f = jnp.pad(f, [(0, 0)] * (w.ndim - 1) + [(0, LANES - H)])
    return jnp.concatenate([qkv, mq, f], axis=-1)


def _fox_cols_from_compute(w, tok):
    H = tok // HEAD_DIM
    qkv, mq, f = w[..., :3 * tok], w[..., 3 * tok:3 * tok + MEM_WIDTH], w[..., 3 * tok + MEM_WIDTH:3 * tok + MEM_WIDTH + H]
    return jnp.concatenate([qkv, f, mq], axis=-1)


_BIG = ("ffn1_w_in", "ffn1_w_out", "ffn2_w_in", "ffn2_w_out", "w_out", "mem_w_kv", "fox_w_in", "gmlp_w_in")
_SMALL = ("norm_ffn1", "norm_mix", "norm_ffn2", "mem_norm", "mem_q_norm", "mem_k_norm", "fox_b_f", "fox_q_norm",
          "fox_k_norm", "gmlp_v_norm", "gmlp_w_s", "gmlp_b_s")
WEIGHT_ORDER = ("norm_ffn1", "ffn1_w_in", "ffn1_w_out", "norm_mix", "norm_ffn2", "ffn2_w_in", "ffn2_w_out", "w_out",
                "mem_norm", "mem_w_kv", "mem_q_norm", "mem_k_norm", "fox_w_in", "fox_b_f", "fox_q_norm", "fox_k_norm",
                "gmlp_w_in", "gmlp_v_norm", "gmlp_w_s", "gmlp_b_s")


def _small_slab(rows_list, index):
    sizes = [s.shape[0] for s in rows_list]
    n_rows = [-(-n // LANES) for n in sizes]
    small = jnp.concatenate([jnp.pad(s, (0, r * LANES - n)).reshape(r, LANES)
                             for s, n, r in zip(rows_list, sizes, n_rows)], axis=0)
    small = jnp.pad(small, ((0, -small.shape[0] % 64), (0, 0)))
    return _slot_in_empty(small, index, 8), sizes, n_rows


_FIRST_WEIGHTS = (("ffn1_w_in", 0), ("ffn1_w_out", 0), ("fox_w_in", 0))


def _weight_from_slab(name, slab, tok):
    if name in ("ffn1_w_in", "ffn2_w_in"):
        return slab
    if name == "fox_w_in":
        return _fox_cols_to_compute(_chips_to_cols(slab), tok)
    if name == "gmlp_w_in":
        return _chips_to_cols(slab)
    return slab.reshape(4 * slab.shape[1], slab.shape[2])


def _grad_to_slab(name, g, tok):
    if name == "fox_w_in":
        return _cols_to_chips(_fox_cols_from_compute(g, tok))
    if name == "gmlp_w_in":
        return _cols_to_chips(g)
    return g


class _Exchange:
    def __init__(self, shards, tok, chip, core):
        self.tok, self.core = tok, core
        self.half = core.reshape(1).astype(jnp.int32)
        self.chip_id = chip.reshape(1).astype(jnp.int32)
        items = [(k, i) for k in _BIG for i in range(shards[k].shape[0])]
        self.slabs = {it: _slot_in_empty(shards[it[0]][it[1]].astype(BF16), chip, 4) for it in items}
        self.late = [it for it in items if it not in _FIRST_WEIGHTS]
        self.reduced = {}
        self.early = None

    def first_weights(self, small_slab):
        got, small_all = gather_weights([self.slabs[it] for it in _FIRST_WEIGHTS], small_slab, name="gather_first")
        return {it: _weight_from_slab(it[0], s, self.tok) for it, s in zip(_FIRST_WEIGHTS, got)}, small_all

    def late_weights_rider(self):
        return gather_rider([self.slabs[it] for it in self.late])

    def accept_late_weights(self, W, got):
        for (k, i), s in zip(self.late, got):
            W[k][i] = _weight_from_slab(k, s, self.tok)

    def _pair_sums(self, G, items, small_slab, tag):
        parts = [_grad_to_slab(k, G[k][i], self.tok) for k, i in items]
        landed, small_all = exchange_with_sibling(parts, small_slab, name=f"grad_exchange_{tag}")
        pair = [pair_sum(p, l, self.half, name=f"grad_pair_sum_{k}{i}") for (k, i), p, l in zip(items, parts, landed)]
        return pair, small_all

    def early_grads_rider(self, G):
        items = [(k, i) for k in _BIG for i in range(len(G[k])) if G[k][i] is not None]
        pair, _ = self._pair_sums(G, items, None, "early")
        self.early = (items, pair)
        return scatter_rider(pair)

    def accept_early_grads(self, landed):
        items, pair = self.early
        self._chip_sums(items, pair, landed)

    def _chip_sums(self, items, pair, landed):
        for (k, i), q, l in zip(items, pair, landed):
            self.reduced[(k, i)] = chip_sum(q, l, self.chip_id, name=f"grad_chip_sum_{k}{i}")

    def finish_grads(self, G, small_slab):
        items = [(k, i) for k in _BIG for i in range(len(G[k])) if (k, i) not in self.reduced]
        pair, small_all = self._pair_sums(G, items, small_slab, "late")
        self._chip_sums(items, pair, scatter_to_chips(pair, name="grad_scatter_late"))
        order = sorted(self.reduced)
        other = share_with_sibling([self.reduced[it] for it in order], name="grad_share")
        full = {}
        for it, a, b in zip(order, [self.reduced[it] for it in order], other):
            full[it] = jnp.where(self.core == 0, jnp.concatenate([a, b]), jnp.concatenate([b, a]))
        names = sorted({k for k, _ in order})
        return {k: jnp.stack([full[(k, i)] for i in range(len(G[k]))]) for k in names}, small_all


def kernel(x, mem, norm_ffn1, ffn1_w_in, ffn1_w_out, norm_mix, norm_ffn2, ffn2_w_in, ffn2_w_out, w_out, mem_norm, mem_w_kv, mem_q_norm, mem_k_norm, fox_w_in, fox_b_f, fox_q_norm, fox_k_norm, gmlp_w_in, gmlp_v_norm, gmlp_w_s, gmlp_b_s, loss_target, m_norm_ffn1, m_ffn1_w_in, m_ffn1_w_out, m_norm_mix, m_norm_ffn2, m_ffn2_w_in, m_ffn2_w_out, m_w_out, m_mem_norm, m_mem_w_kv, m_mem_q_norm, m_mem_k_norm, m_fox_w_in, m_fox_b_f, m_fox_q_norm, m_fox_k_norm, m_gmlp_w_in, m_gmlp_v_norm, m_gmlp_w_s, m_gmlp_b_s, v_norm_ffn1, v_ffn1_w_in, v_ffn1_w_out, v_norm_mix, v_norm_ffn2, v_ffn2_w_in, v_ffn2_w_out, v_w_out, v_mem_norm, v_mem_w_kv, v_mem_q_norm, v_mem_k_norm, v_fox_w_in, v_fox_b_f, v_fox_q_norm, v_fox_k_norm, v_gmlp_w_in, v_gmlp_v_norm, v_gmlp_w_s, v_gmlp_b_s):
    w = dict(norm_ffn1=norm_ffn1, ffn1_w_in=ffn1_w_in, ffn1_w_out=ffn1_w_out, norm_mix=norm_mix, norm_ffn2=norm_ffn2,
             ffn2_w_in=ffn2_w_in, ffn2_w_out=ffn2_w_out, w_out=w_out, mem_norm=mem_norm, mem_w_kv=mem_w_kv,
             mem_q_norm=mem_q_norm, mem_k_norm=mem_k_norm, fox_w_in=fox_w_in, fox_b_f=fox_b_f, fox_q_norm=fox_q_norm,
             fox_k_norm=fox_k_norm, gmlp_w_in=gmlp_w_in, gmlp_v_norm=gmlp_v_norm, gmlp_w_s=gmlp_w_s, gmlp_b_s=gmlp_b_s)
    m = dict(norm_ffn1=m_norm_ffn1, ffn1_w_in=m_ffn1_w_in, ffn1_w_out=m_ffn1_w_out, norm_mix=m_norm_mix,
             norm_ffn2=m_norm_ffn2, ffn2_w_in=m_ffn2_w_in, ffn2_w_out=m_ffn2_w_out, w_out=m_w_out, mem_norm=m_mem_norm,
             mem_w_kv=m_mem_w_kv, mem_q_norm=m_mem_q_norm, mem_k_norm=m_mem_k_norm, fox_w_in=m_fox_w_in,
             fox_b_f=m_fox_b_f, fox_q_norm=m_fox_q_norm, fox_k_norm=m_fox_k_norm, gmlp_w_in=m_gmlp_w_in,
             gmlp_v_norm=m_gmlp_v_norm, gmlp_w_s=m_gmlp_w_s, gmlp_b_s=m_gmlp_b_s)
    v = dict(norm_ffn1=v_norm_ffn1, ffn1_w_in=v_ffn1_w_in, ffn1_w_out=v_ffn1_w_out, norm_mix=v_norm_mix,
             norm_ffn2=v_norm_ffn2, ffn2_w_in=v_ffn2_w_in, ffn2_w_out=v_ffn2_w_out, w_out=v_w_out, mem_norm=v_mem_norm,
             mem_w_kv=v_mem_w_kv, mem_q_norm=v_mem_q_norm, mem_k_norm=v_mem_k_norm, fox_w_in=v_fox_w_in,
             fox_b_f=v_fox_b_f, fox_q_norm=v_fox_q_norm, fox_k_norm=v_fox_k_norm, gmlp_w_in=v_gmlp_w_in,
             gmlp_v_norm=v_gmlp_v_norm, gmlp_w_s=v_gmlp_w_s, gmlp_b_s=v_gmlp_b_s)
    D = x.shape[-1]
    tok = D - MEM_WIDTH
    xi, yi, ci = _position()
    chip = 2 * xi + yi

    device = 4 * xi + 2 * yi + ci

    comm = _Exchange(w, tok, chip, ci)
    vn = w["gmlp_v_norm"]
    vn_slab, _, _ = _small_slab([vn.reshape(-1)], device)
    first, vn_all = comm.first_weights(vn_slab)
    W = {k: w[k] for k in _SMALL}
    W["gmlp_v_norm"] = _chips_to_cols(vn_all[0::2].reshape(4, -1)[:, :vn.size].reshape((4,) + vn.shape))
    for k in _BIG:
        W[k] = [first.get((k, i)) for i in range(w[k].shape[0])]

    loss, grad_x, g = local_step(x[0], mem[0], loss_target[0], W, comm)

    small_list = [jnp.stack(g[k]).reshape(-1) for k in _SMALL] + [loss.reshape(-1)]
    small, small_sizes, small_rows = _small_slab(small_list, device)
    red, small_all = comm.finish_grads(g, small)
    small_sum = ordered_sum(small_all, name="small_sum")
    off = 0
    for k, n, r in zip(_SMALL, small_sizes, small_rows):
        red[k] = small_sum[off:off + r].reshape(-1)[:n].reshape((-1,) + w[k].shape[1:] if k != "gmlp_v_norm"
                                                                else (w[k].shape[0], -1))
        off += r
    loss_total = small_sum[off, 0]
    vn_cols = w["gmlp_v_norm"].shape[-1]
    red["gmlp_v_norm"] = lax.dynamic_slice_in_dim(red["gmlp_v_norm"], chip * vn_cols, vn_cols, axis=-1)

    deltas, new_m, new_v = {}, {}, {}
    for k in WEIGHT_ORDER:
        wk = w[k] if w[k].ndim > 1 else w[k].reshape(1, -1)
        upd = adamw(wk, red[k].reshape(wk.shape), m[k].reshape(wk.shape), v[k].reshape(wk.shape), name=f"adamw_{k}")
        deltas[k], new_m[k], new_v[k] = (u.reshape(w[k].shape) for u in upd)
    return (loss_total, grad_x[None], *[red[k].reshape(w[k].shape) for k in WEIGHT_ORDER],
            *[deltas[k] for k in WEIGHT_ORDER], *[new_m[k] for k in WEIGHT_ORDER], *[new_v[k] for k in WEIGHT_ORDER])
```
